```python
import math
import jax, jax.numpy as jnp
from jax import lax
import numpy as np


D_MODEL = 1024
BATCH = 4
SEQ = 8192
DEPTH = 1

D_MIX = D_MODEL
HYENA_WIDTH = D_MIX // 2
POOL_WIDTH = D_MIX - HYENA_WIDTH
HYENA_ORDER = 2
N_DIRECTIONS = 2
SHORT_CONV = 3
FILTER_EMB = 33
FILTER_BANDS = (FILTER_EMB - 1) // 2
FILTER_HIDDEN = 64
FAST_DECAY_PCT = 0.3
SLOW_DECAY_PCT = 1.5
DECAY_TARGET = 1e-2
POOL_WINDOWS = (2, 4, 8, 16)
POOL_GROUP = POOL_WIDTH // len(POOL_WINDOWS)
HYENA_IN = (HYENA_ORDER + 1) * HYENA_WIDTH
PROJ_WIDTH = HYENA_IN + HYENA_WIDTH + 2 * POOL_WIDTH
EPS = 1e-6

kernel_name = 'hyena_pool_hybrid_block'


def rms_norm(x, g):
    xf = x.astype(jnp.float32)
    y = xf * lax.rsqrt(jnp.mean(xf * xf, axis=-1, keepdims=True) + EPS)
    return (y * g.astype(jnp.float32)).astype(x.dtype)


def short_conv(u, w, b):
    L = u.shape[1]
    up = jnp.pad(u, ((0, 0), (1, 1), (0, 0)))
    return up[:, :L] * w[0] + up[:, 1:L + 1] * w[1] + up[:, 2:] * w[2] + b


def hyena_filters(L, w1, b1, w2, b2, w3, b3, freq, w_proj):
    f32 = jnp.float32
    w1, b1, w2, b2, w3, b3, freq, w_proj = [a.astype(f32) for a in (w1, b1, w2, b2, w3, b3, freq, w_proj)]
    t = jnp.linspace(0.0, 1.0, L, dtype=f32)[:, None]
    pos = jnp.arange(L, dtype=f32)[:, None]
    ang = 2.0 * math.pi * pos / L
    bands = jnp.linspace(1e-4, FILTER_BANDS - 1, FILTER_BANDS, dtype=f32)[None, :]
    z = jnp.concatenate([t, jnp.cos(bands * ang), -jnp.sin(bands * ang)], axis=-1)
    h = jnp.sin(freq * (z @ w1 + b1))
    h = jnp.sin(freq * (h @ w2 + b2))
    h = jnp.sin(freq * (h @ w3 + b3))
    h = (h @ w_proj).reshape(L, HYENA_ORDER, N_DIRECTIONS, HYENA_WIDTH)
    max_decay = math.log(DECAY_TARGET) / FAST_DECAY_PCT
    min_decay = math.log(DECAY_TARGET) / SLOW_DECAY_PCT
    deltas = jnp.linspace(min_decay, max_decay, HYENA_WIDTH, dtype=f32)
    decay = jnp.exp(-t * jnp.abs(deltas)[None, :])
    return h * decay[:, None, None, :]


def two_sided_spectrum(h):
    L = h.shape[0]
    k_fwd = h[:, :, 0]
    k_bwd = h[:, :, 1]
    k_full = jnp.concatenate([k_fwd, jnp.zeros_like(k_fwd[:1]), k_bwd[:0:-1]], axis=0)
    return jnp.fft.rfft(k_full, n=2 * L, axis=0)


def fft_conv(u, k_f, d):
    L = u.shape[1]
    uf = u.astype(jnp.float32)
    u_f = jnp.fft.rfft(uf, n=2 * L, axis=1)
    y = jnp.fft.irfft(u_f * k_f[None], n=2 * L, axis=1)[:, :L]
    return (y + uf * d.astype(jnp.float32)).astype(u.dtype)


def multiscale_pool(u, pool_w, pool_scale):
    B, L, C = u.shape
    uf = u.astype(jnp.float32)
    S = jnp.concatenate([jnp.zeros((B, 1, C), jnp.float32), jnp.cumsum(uf, axis=1)], axis=1)
    pos = jnp.arange(L)
    outs = []
    for g, w in enumerate(POOL_WINDOWS):
        sl = slice(g * POOL_GROUP, (g + 1) * POOL_GROUP)
        lo = jnp.clip(pos - w // 2, 0, L - 1)
        hi = jnp.clip(pos + (w - 1 - w // 2), 0, L - 1)
        total = jnp.take(S[..., sl], hi + 1, axis=1) - jnp.take(S[..., sl], lo, axis=1)
        cnt = (hi - lo + 1).astype(jnp.float32)[None, :, None]
        pooled = total / cnt - uf[..., sl]
        outs.append(jnp.einsum('blc,cd->bld', pooled, pool_w[g].astype(jnp.float32)))
    y = jnp.concatenate(outs, axis=-1) * pool_scale.astype(jnp.float32)
    return y.astype(u.dtype)


def hybrid_layer(x, pre_g, w_in, conv_w, conv_b, fw1, fb1, fw2, fb2, fw3, fb3, ffreq, fw_out,
                 hyena_d, pool_w, pool_scale, norm_h_g, norm_p_g, w_out, post_g):
    L = x.shape[1]
    h = rms_norm(x, pre_g)
    p = jnp.einsum('bld,dk->blk', h, w_in)
    hy = short_conv(p[..., :HYENA_IN], conv_w, conv_b)
    v = hy[..., :HYENA_WIDTH]
    gates = (hy[..., HYENA_WIDTH:2 * HYENA_WIDTH], hy[..., 2 * HYENA_WIDTH:])
    z_h = p[..., HYENA_IN:HYENA_IN + HYENA_WIDTH]
    u_p = p[..., HYENA_IN + HYENA_WIDTH:HYENA_IN + HYENA_WIDTH + POOL_WIDTH]
    z_p = p[..., HYENA_IN + HYENA_WIDTH + POOL_WIDTH:]
    k_f = two_sided_spectrum(hyena_filters(L, fw1, fb1, fw2, fb2, fw3, fb3, ffreq, fw_out))
    y = v
    for o in range(HYENA_ORDER):
        y = gates[o] * fft_conv(y, k_f[:, o], hyena_d[o])
    y_h = rms_norm(y * jax.nn.silu(z_h), norm_h_g)
    y_p = rms_norm(multiscale_pool(u_p, pool_w, pool_scale) * jax.nn.silu(z_p), norm_p_g)
    out = jnp.einsum('blk,kd->bld', jnp.concatenate([y_h, y_p], axis=-1), w_out)
    return x + rms_norm(out, post_g)


def setup_inputs(seed: int = 0) -> dict:
    key = jax.random.key(seed)
    ks = jax.random.split(key, 24)

    def nrm(k, shape, scale):
        return jax.random.normal(k, shape, jnp.float32) * scale

    n_filt_out = HYENA_ORDER * N_DIRECTIONS * HYENA_WIDTH
    return {
        'x': nrm(ks[0], (BATCH, SEQ, D_MODEL), 1.0),
        'pre_norm_g': 1.0 + nrm(ks[1], (DEPTH, D_MODEL), 0.02),
        'w_in': nrm(ks[2], (DEPTH, D_MODEL, PROJ_WIDTH), D_MODEL ** -0.5),
        'conv_w': nrm(ks[3], (DEPTH, SHORT_CONV, HYENA_IN), SHORT_CONV ** -0.5),
        'conv_b': nrm(ks[4], (DEPTH, HYENA_IN), 0.01),
        'filt_w1': nrm(ks[5], (DEPTH, FILTER_EMB, FILTER_HIDDEN), FILTER_EMB ** -0.5),
        'filt_b1': nrm(ks[6], (DEPTH, FILTER_HIDDEN), 0.1),
        'filt_w2': nrm(ks[7], (DEPTH, FILTER_HIDDEN, FILTER_HIDDEN), FILTER_HIDDEN ** -0.5),
        'filt_b2': nrm(ks[8], (DEPTH, FILTER_HIDDEN), 0.1),
        'filt_w3': nrm(ks[9], (DEPTH, FILTER_HIDDEN, FILTER_HIDDEN), FILTER_HIDDEN ** -0.5),
        'filt_b3': nrm(ks[10], (DEPTH, FILTER_HIDDEN), 0.1),
        'filt_freq': 1.0 + nrm(ks[11], (DEPTH, FILTER_HIDDEN), 0.02),
        'filt_w_out': nrm(ks[12], (DEPTH, FILTER_HIDDEN, n_filt_out), 0.5 * FILTER_HIDDEN ** -0.5),
        'hyena_d': nrm(ks[13], (DEPTH, HYENA_ORDER, HYENA_WIDTH), 0.5),
        'pool_w': nrm(ks[14], (DEPTH, len(POOL_WINDOWS), POOL_GROUP, POOL_GROUP), POOL_GROUP ** -0.5),
        'pool_scale': 1.0 + nrm(ks[15], (DEPTH, POOL_WIDTH), 0.02),
        'norm_h_g': 1.0 + nrm(ks[16], (DEPTH, HYENA_WIDTH), 0.02),
        'norm_p_g': 1.0 + nrm(ks[17], (DEPTH, POOL_WIDTH), 0.02),
        'w_out': nrm(ks[18], (DEPTH, D_MIX, D_MODEL), D_MIX ** -0.5),
        'post_norm_g': 1.0 + nrm(ks[19], (DEPTH, D_MODEL), 0.02),
    }


def reference(x, pre_norm_g, w_in, conv_w, conv_b, filt_w1, filt_b1, filt_w2, filt_b2, filt_w3, filt_b3,
              filt_freq, filt_w_out, hyena_d, pool_w, pool_scale, norm_h_g, norm_p_g, w_out, post_norm_g):
    for i in range(DEPTH):
        x = hybrid_layer(x, pre_norm_g[i], w_in[i], conv_w[i], conv_b[i],
                         filt_w1[i], filt_b1[i], filt_w2[i], filt_b2[i], filt_w3[i], filt_b3[i],
                         filt_freq[i], filt_w_out[i], hyena_d[i], pool_w[i], pool_scale[i],
                         norm_h_g[i], norm_p_g[i], w_out[i], post_norm_g[i])
    return x
```

```python
import functools
import math

import numpy as np
import jax
import jax.numpy as jnp
from jax import lax
from jax.experimental import pallas as pl
from jax.experimental.pallas import tpu as pltpu

F32 = jnp.float32
BF16 = jnp.bfloat16

D_MODEL = 1024
BATCH = 4
SEQ = 8192
HYENA_WIDTH = 512
POOL_WIDTH = 512
POOL_WINDOWS = (2, 4, 8, 16)
FILTER_EMB = 33
FILTER_BANDS = 16
FILTER_HIDDEN = 64
PROJ_WIDTH = 3072
EPS = 1e-6

LANES = 128
N_FFT = 2 * SEQ
N_TILES = 128
TILE_ROWS = SEQ // N_TILES
N_SLOTS = 64
CHUNK = 16
N_CHUNKS = N_TILES // CHUNK
A_STRIDE = 136
N_CBLK = HYENA_WIDTH // LANES
PROJ_BLOCKS = PROJ_WIDTH // LANES
IN_TILES = 8
VMEM_LIMIT = 60 * 1024 * 1024


def _dft_tables():
    n1 = np.arange(TILE_ROWS)
    n2 = np.arange(N_TILES)
    s = np.arange(N_SLOTS)
    ph = 2 * np.pi * (n2[:, None, None] * s[None, :, None] / N_FFT
                      + n1[None, None, :] * s[None, :, None] / 128.0)
    f1 = np.zeros((N_TILES, 128, TILE_ROWS))
    f1[:, :64, :] = np.cos(ph)
    f1[:, 64:, :] = -np.sin(ph)
    f1[:, 0, :] = 1.0
    f1[:, 64, :] = (-1.0) ** n1
    php = np.transpose(ph, (0, 2, 1))
    f1i = np.zeros((N_TILES, TILE_ROWS, 128))
    f1i[:, :, :64] = 2 * np.cos(php) / N_FFT
    f1i[:, :, 64:] = -2 * np.sin(php) / N_FFT
    f1i[:, :, 0] = 1.0 / N_FFT
    f1i[:, :, 64] = ((-1.0) ** n1)[None, :] / N_FFT
    k2 = np.arange(128)
    th = 2 * np.pi * np.outer(k2, n2) / 128.0
    c, sn = np.cos(th), np.sin(th)
    g = np.block([[c, sn], [-sn, c]])
    gi = np.block([[c, -sn], [sn, c]])
    kk = np.arange(64)
    tha = 2 * np.pi * np.outer(kk, n2) / 128.0
    thb = 2 * np.pi * np.outer(64 + 128 * kk, n2) / N_FFT
    g0 = np.zeros((256, 256))
    g0[0:64, 0:128] = np.cos(tha)
    g0[64:128, 128:256] = np.cos(thb)
    g0[128:192, 0:128] = -np.sin(tha)
    g0[128, 0:128] = (-1.0) ** n2
    g0[192:256, 128:256] = -np.sin(thb)
    g0i = np.zeros((256, 256))
    g0i[0:128, 0:64] = 2 * np.cos(tha.T)
    g0i[0:128, 0] = 1.0
    g0i[0:128, 128:192] = -2 * np.sin(tha.T)
    g0i[0:128, 128] = (-1.0) ** n2
    g0i[128:256, 64:128] = 2 * np.cos(thb.T)
    g0i[128:256, 192:256] = -2 * np.sin(thb.T)
    return f1, f1i, g, gi, g0, g0i


def _filter_features():
    pos = np.arange(SEQ, dtype=np.float64)
    t = pos / (SEQ - 1)
    ang = 2.0 * math.pi * pos / SEQ
    bands = np.linspace(1e-4, FILTER_BANDS - 1, FILTER_BANDS)
    z = np.concatenate([t[:, None], np.cos(bands[None, :] * ang[:, None]),
                        -np.sin(bands[None, :] * ang[:, None])], axis=-1)
    z = z.reshape(TILE_ROWS, N_TILES, FILTER_EMB).transpose(1, 0, 2).reshape(SEQ, FILTER_EMB)
    zp = np.zeros((SEQ, 40))
    zp[:, :FILTER_EMB] = z
    max_decay = math.log(1e-2) / 0.3
    min_decay = math.log(1e-2) / 1.5
    deltas = np.abs(np.linspace(min_decay, max_decay, HYENA_WIDTH))
    return zp.astype(np.float32), deltas.astype(np.float32)[None, :]


_F1, _F1I, _G, _GI, _G0, _G0I = _dft_tables()
_ZFEAT, _ABS_DELTAS = _filter_features()


def _shift_down(x):
    rows = lax.broadcasted_iota(jnp.int32, x.shape, 0)
    return jnp.where(rows == 0, 0.0, pltpu.roll(x, 1, axis=0))


def _shift_up(x):
    rows = lax.broadcasted_iota(jnp.int32, x.shape, 0)
    return jnp.where(rows == x.shape[0] - 1, 0.0, pltpu.roll(x, x.shape[0] - 1, axis=0))


def _pair(ref, i):
    return jnp.concatenate([ref[0, i], ref[1, i]], axis=-1).astype(F32)


def _dup(x):
    return jnp.concatenate([x, x], axis=-1)


def _silu(z):
    return z / (1.0 + jnp.exp(-z))


def _a_rows(n2):
    return pl.ds(pl.multiple_of(n2 * A_STRIDE, 8), 128)


def _slot_rows(r):
    return pl.ds(r, N_TILES, stride=A_STRIDE)


def _stage_load(a_ref, rows):
    return jnp.concatenate([a_ref[0, rows, :], a_ref[1, rows, :]], axis=-1)


def _stage_store(a_ref, rows, val):
    a_ref[0, rows, :] = val[:, :LANES]
    a_ref[1, rows, :] = val[:, LANES:]


def _short_conv_tile(chunk_ref, prev_ref, next_ref, i, tc, w_ref, b_ref):
    cur = _pair(chunk_ref, i)
    if i == 0:
        halo = _pair(prev_ref, 0)
        prev = jnp.where(tc == 0, _shift_down(halo), halo)
    else:
        prev = _pair(chunk_ref, i - 1)
    if i == CHUNK - 1:
        halo = _pair(next_ref, 0)
        nxt = jnp.where(tc == N_CHUNKS - 1, _shift_up(halo), halo)
    else:
        nxt = _pair(chunk_ref, i + 1)
    w = w_ref[...]
    return (prev * _dup(w[0:1]) + cur * _dup(w[1:2]) + nxt * _dup(w[2:3]) + _dup(b_ref[...]))


def _stage2_slots(a_ref, g_ref, g0_ref, body):
    def one(s, gmat):
        xr = _stage_load(a_ref, _slot_rows(s))
        xi = _stage_load(a_ref, _slot_rows(N_SLOTS + s))
        x = jnp.concatenate([xr, xi], axis=0).astype(BF16)
        return jnp.dot(gmat, x, preferred_element_type=F32)

    body(0, one(0, g0_ref[...]), True)

    def loop(s, carry):
        body(s, one(s, g_ref[...]), False)
        return carry

    lax.fori_loop(1, N_SLOTS, loop, 0)


def _in_proj_kernel(x_ref, g_ref, w_ref, o_ref, h_scr):
    x = x_ref[...].reshape(TILE_ROWS * IN_TILES, D_MODEL)
    ms = jnp.mean(x * x, axis=-1, keepdims=True)
    hn = x * lax.rsqrt(ms + EPS) * g_ref[...]
    for k in range(D_MODEL // LANES):
        h_scr[k] = hn[:, k * LANES:(k + 1) * LANES]
    h = jnp.concatenate(
        [jnp.concatenate([h_scr[k, pl.ds(j, TILE_ROWS, stride=IN_TILES), :] for k in range(D_MODEL // LANES)],
                         axis=-1).astype(BF16) for j in range(IN_TILES)], axis=0)
    ncol = 512
    for c in range(PROJ_WIDTH // ncol):
        p = jnp.dot(h, w_ref[:, c * ncol:(c + 1) * ncol], preferred_element_type=F32).astype(BF16)
        for cb in range(ncol // LANES):
            for j in range(IN_TILES):
                o_ref[c * (ncol // LANES) + cb, j] = p[j * TILE_ROWS:(j + 1) * TILE_ROWS,
                                                       cb * LANES:(cb + 1) * LANES]


def _in_proj(x4, pre_g, w_in_bf):
    grid = (BATCH, N_TILES // IN_TILES)
    return pl.pallas_call(
        _in_proj_kernel,
        grid=grid,
        in_specs=[
            pl.BlockSpec((None, TILE_ROWS, IN_TILES, D_MODEL), lambda b, g: (b, 0, g, 0)),
            pl.BlockSpec((1, D_MODEL), lambda b, g: (0, 0)),
            pl.BlockSpec((D_MODEL, PROJ_WIDTH), lambda b, g: (0, 0)),
        ],
        out_specs=pl.BlockSpec((None, PROJ_BLOCKS, IN_TILES, TILE_ROWS, LANES), lambda b, g: (b, 0, g, 0, 0)),
        out_shape=jax.ShapeDtypeStruct((BATCH, PROJ_BLOCKS, N_TILES, TILE_ROWS, LANES), BF16),
        scratch_shapes=[pltpu.VMEM((D_MODEL // LANES, TILE_ROWS * IN_TILES, LANES), F32)],
        compiler_params=pltpu.CompilerParams(
            dimension_semantics=("arbitrary", "arbitrary"), vmem_limit_bytes=VMEM_LIMIT),
        name="in_proj",
    )(x4, pre_g, w_in_bf)


def _filter_mlp_kernel(z_ref, w1_ref, b1_ref, w2_ref, b2_ref, w3_ref, b3_ref, fr_ref, wp_ref, dl_ref, o_ref):
    hp = lax.Precision.HIGHEST
    fr = fr_ref[...]
    h = jnp.sin(fr * (jnp.dot(z_ref[...], w1_ref[...], precision=hp, preferred_element_type=F32) + b1_ref[...]))
    h = jnp.sin(fr * (jnp.dot(h, w2_ref[...], precision=hp, preferred_element_type=F32) + b2_ref[...]))
    h = jnp.sin(fr * (jnp.dot(h, w3_ref[...], precision=hp, preferred_element_type=F32) + b3_ref[...]))
    rows = CHUNK * TILE_ROWS
    r = lax.broadcasted_iota(jnp.int32, (rows, HYENA_WIDTH), 0)
    pos = 128 * (r % TILE_ROWS) + CHUNK * pl.program_id(0) + r // TILE_ROWS
    t = pos.astype(F32) / float(SEQ - 1)
    decay = jnp.exp(-t * dl_ref[...])
    for od in range(4):
        taps = jnp.dot(h, wp_ref[:, od * HYENA_WIDTH:(od + 1) * HYENA_WIDTH], precision=hp,
                       preferred_element_type=F32) * decay
        if od % 2 == 1:
            taps = jnp.where(pos == 0, 0.0, taps)
        taps = taps.astype(BF16)
        for cb in range(N_CBLK):
            for i in range(CHUNK):
                o_ref[od * N_CBLK + cb, i] = taps[i * TILE_ROWS:(i + 1) * TILE_ROWS, cb * LANES:(cb + 1) * LANES]


def _filter_mlp(zfeat, w1p, b1, w2, b2, w3, b3, freq, w_proj, abs_deltas):
    full = lambda shape: pl.BlockSpec(shape, lambda c: (0,) * len(shape))
    return pl.pallas_call(
        _filter_mlp_kernel,
        grid=(N_CHUNKS,),
        in_specs=[
            pl.BlockSpec((CHUNK * TILE_ROWS, 40), lambda c: (c, 0)),
            full((40, FILTER_HIDDEN)), full((1, FILTER_HIDDEN)),
            full((FILTER_HIDDEN, FILTER_HIDDEN)), full((1, FILTER_HIDDEN)),
            full((FILTER_HIDDEN, FILTER_HIDDEN)), full((1, FILTER_HIDDEN)),
            full((1, FILTER_HIDDEN)),
            full((FILTER_HIDDEN, 4 * HYENA_WIDTH)),
            full((1, HYENA_WIDTH)),
        ],
        out_specs=pl.BlockSpec((4 * N_CBLK, CHUNK, TILE_ROWS, LANES), lambda c: (0, c, 0, 0)),
        out_shape=jax.ShapeDtypeStruct((4 * N_CBLK, N_TILES, TILE_ROWS, LANES), BF16),
        compiler_params=pltpu.CompilerParams(dimension_semantics=("arbitrary",), vmem_limit_bytes=VMEM_LIMIT),
        name="filter_mlp",
    )(zfeat, w1p, b1, w2, b2, w3, b3, freq, w_proj, abs_deltas)


def _filter_spec_kernel(hf_ref, hb_ref, f1_ref, g_ref, g0_ref, k_ref, a_scr):
    t = pl.program_id(2)

    @pl.when(t < N_CHUNKS)
    def _():
        for i in range(CHUNK):
            n2 = t * CHUNK + i
            taps = jnp.concatenate([hf_ref[i], hb_ref[i]], axis=-1)
            _stage_store(a_scr, _a_rows(n2), jnp.dot(f1_ref[n2], taps, preferred_element_type=F32))

    @pl.when(t == N_CHUNKS)
    def _():
        def body(s, y, first):
            yr, yi = y[:128], y[128:]
            kr = yr[:, :LANES] + yr[:, LANES:]
            ki = yi[:, :LANES] - yi[:, LANES:]
            if first:
                rows = lax.broadcasted_iota(jnp.int32, ki.shape, 0)
                ki = jnp.where(rows == 0, yi[:, :LANES] + yi[:, LANES:], ki)
            k_ref[s, 0:128, :] = kr
            k_ref[s, 128:256, :] = ki

        _stage2_slots(a_scr, g_ref, g0_ref, body)


def _filter_spec(taps, f1, g, g0):
    grid = (2, N_CBLK, N_CHUNKS + 1)
    chunk_idx = lambda t: jnp.minimum(t, N_CHUNKS - 1)
    const = lambda shape: pl.BlockSpec(shape, lambda o, c, t: (0,) * len(shape))
    return pl.pallas_call(
        _filter_spec_kernel,
        grid=grid,
        in_specs=[
            pl.BlockSpec((None, CHUNK, TILE_ROWS, LANES), lambda o, c, t: ((2 * o) * N_CBLK + c, chunk_idx(t), 0, 0)),
            pl.BlockSpec((None, CHUNK, TILE_ROWS, LANES), lambda o, c, t: ((2 * o + 1) * N_CBLK + c, chunk_idx(t), 0, 0)),
            const((N_TILES, 128, TILE_ROWS)), const((256, 256)), const((256, 256)),
        ],
        out_specs=pl.BlockSpec((None, None, N_SLOTS, 256, LANES), lambda o, c, t: (o, c, 0, 0, 0)),
        out_shape=jax.ShapeDtypeStruct((2, N_CBLK, N_SLOTS, 256, LANES), F32),
        scratch_shapes=[pltpu.VMEM((2, N_TILES * A_STRIDE, LANES), F32)],
        compiler_params=pltpu.CompilerParams(
            dimension_semantics=("arbitrary", "arbitrary", "arbitrary"), vmem_limit_bytes=VMEM_LIMIT),
        name="filter_spec",
    )(taps, taps, f1, g, g0)


_T_S1 = 0
_T_F0 = N_CHUNKS
_T_M = N_CHUNKS + 1
_T_F1 = 2 * N_CHUNKS + 1
_T_E = 2 * N_CHUNKS + 2
_T_END = 3 * N_CHUNKS + 2


def _hyena_kernel(pv_ref, pvp_ref, pvn_ref, px1_ref, px1p_ref, px1n_ref, px2_ref, px2p_ref, px2n_ref, pz_ref,
                  wv_ref, bv_ref, w1_ref, b1_ref, w2_ref, b2_ref, d_ref, k_ref,
                  f1_ref, f1i_ref, g_ref, g0_ref, gi_ref, g0i_ref, o_ref, a_scr, u_scr):
    t = pl.program_id(2)

    def stage1(n2, u_bf):
        _stage_store(a_scr, _a_rows(n2), jnp.dot(f1_ref[n2], u_bf, preferred_element_type=F32))

    def inv_stage1(n2):
        return jnp.dot(f1i_ref[n2], _stage_load(a_scr, _a_rows(n2)).astype(BF16), preferred_element_type=F32)

    def spectral_phase():
        def body(s, y, first):
            yr, yi = y[:128], y[128:]
            kr = k_ref[s, 0:128, :]
            ki = k_ref[s, 128:256, :]
            if first:
                rows = lax.broadcasted_iota(jnp.int32, kr.shape, 0)
                ka, kb, kd = kr, jnp.where(rows == 0, 0.0, ki), jnp.where(rows == 0, ki, kr)
            else:
                ka, kb, kd = kr, ki, kr
            ka, kb, kd = _dup(ka), _dup(kb), _dup(kd)
            z = jnp.concatenate([yr * ka - yi * kb, yr * kb + yi * kd], axis=0).astype(BF16)
            bm = jnp.dot(g0i_ref[...] if first else gi_ref[...], z, preferred_element_type=F32)
            _stage_store(a_scr, _slot_rows(s), bm[:128])
            _stage_store(a_scr, _slot_rows(N_SLOTS + s), bm[128:])

        _stage2_slots(a_scr, g_ref, g0_ref, body)

    @pl.when(t < _T_F0)
    def _():
        for i in range(CHUNK):
            n2 = t * CHUNK + i
            v = _short_conv_tile(pv_ref, pvp_ref, pvn_ref, i, t, wv_ref, bv_ref).astype(BF16)
            u_scr[n2] = v
            stage1(n2, v)

    @pl.when((t == _T_F0) | (t == _T_F1))
    def _():
        spectral_phase()

    @pl.when((t >= _T_M) & (t < _T_F1))
    def _():
        tc = t - _T_M
        for i in range(CHUNK):
            n2 = tc * CHUNK + i
            y = inv_stage1(n2)
            gate = _short_conv_tile(px1_ref, px1p_ref, px1n_ref, i, tc, w1_ref, b1_ref)
            u = (gate * (y + _dup(d_ref[0:1]) * u_scr[n2].astype(F32))).astype(BF16)
            u_scr[n2] = u
            stage1(n2, u)

    @pl.when(t >= _T_E)
    def _():
        tc = t - _T_E
        for i in range(CHUNK):
            n2 = tc * CHUNK + i
            y = inv_stage1(n2)
            gate = _short_conv_tile(px2_ref, px2p_ref, px2n_ref, i, tc, w2_ref, b2_ref)
            res = gate * (y + _dup(d_ref[1:2]) * u_scr[n2].astype(F32)) * _silu(_pair(pz_ref, i))
            res = res.astype(BF16)
            o_ref[0, i] = res[:, :LANES]
            o_ref[1, i] = res[:, LANES:]


def _hyena(p5, conv_w, conv_b, hyena_d, kspec, f1, f1i, g, g0, gi, g0i):
    grid = (N_CBLK, BATCH // 2, _T_END)

    def clampc(t, start):
        return jnp.clip(t - start, 0, N_CHUNKS - 1)

    def chunk_spec(col0, start):
        return pl.BlockSpec((2, None, CHUNK, TILE_ROWS, LANES),
                            lambda c, b, t: (b, col0 + c, clampc(t, start), 0, 0))

    def prev_spec(col0, start):
        return pl.BlockSpec((2, None, 1, TILE_ROWS, LANES),
                            lambda c, b, t: (b, col0 + c, (clampc(t, start) * CHUNK + N_TILES - 1) % N_TILES, 0, 0))

    def next_spec(col0, start):
        return pl.BlockSpec((2, None, 1, TILE_ROWS, LANES),
                            lambda c, b, t: (b, col0 + c, (clampc(t, start) * CHUNK + CHUNK) % N_TILES, 0, 0))

    def lane_spec(rows, col0):
        return pl.BlockSpec((rows, LANES), lambda c, b, t: (0, col0 + c))

    const = lambda shape: pl.BlockSpec(shape, lambda c, b, t: (0,) * len(shape))
    in_specs = [
        chunk_spec(0, _T_S1), prev_spec(0, _T_S1), next_spec(0, _T_S1),
        chunk_spec(N_CBLK, _T_M), prev_spec(N_CBLK, _T_M), next_spec(N_CBLK, _T_M),
        chunk_spec(2 * N_CBLK, _T_E), prev_spec(2 * N_CBLK, _T_E), next_spec(2 * N_CBLK, _T_E),
        chunk_spec(3 * N_CBLK, _T_E),
        lane_spec(3, 0), lane_spec(1, 0),
        lane_spec(3, N_CBLK), lane_spec(1, N_CBLK),
        lane_spec(3, 2 * N_CBLK), lane_spec(1, 2 * N_CBLK),
        lane_spec(2, 0),
        pl.BlockSpec((None, None, N_SLOTS, 256, LANES), lambda c, b, t: (jnp.where(t >= _T_M, 1, 0), c, 0, 0, 0)),
        const((N_TILES, 128, TILE_ROWS)), const((N_TILES, TILE_ROWS, 128)),
        const((256, 256)), const((256, 256)), const((256, 256)), const((256, 256)),
    ]
    return pl.pallas_call(
        _hyena_kernel,
        grid=grid,
        in_specs=in_specs,
        out_specs=pl.BlockSpec((2, None, CHUNK, TILE_ROWS, LANES), lambda c, b, t: (b, c, clampc(t, _T_E), 0, 0)),
        out_shape=jax.ShapeDtypeStruct((BATCH, N_CBLK, N_TILES, TILE_ROWS, LANES), BF16),
        scratch_shapes=[pltpu.VMEM((2, N_TILES * A_STRIDE, LANES), F32),
                        pltpu.VMEM((N_TILES, TILE_ROWS, 2 * LANES), BF16)],
        compiler_params=pltpu.CompilerParams(
            dimension_semantics=("arbitrary", "arbitrary", "arbitrary"), vmem_limit_bytes=VMEM_LIMIT),
        name="hyena",
    )(p5, p5, p5, p5, p5, p5, p5, p5, p5, p5,
      conv_w, conv_b, conv_w, conv_b, conv_w, conv_b, hyena_d, kspec, f1, f1i, g, g0, gi, g0i)


_POOL_HALO = 8


def _pool_kernel(u_ref, z_ref, pw_ref, ps_ref, o_ref, e_scr):
    g = pl.program_id(1)

    def fill(n2, carry):
        e_scr[_POOL_HALO + n2] = u_ref[n2].astype(F32)
        return carry

    lax.fori_loop(0, N_TILES, fill, 0)
    for j in range(1, _POOL_HALO + 1):
        e_scr[_POOL_HALO - j] = _shift_down(u_ref[N_TILES - j].astype(F32))
    for j in range(_POOL_HALO):
        e_scr[_POOL_HALO + N_TILES + j] = _shift_up(u_ref[j].astype(F32))

    n1 = lax.broadcasted_iota(jnp.int32, (TILE_ROWS, LANES), 0)
    group = 8

    for gi_, w in enumerate(POOL_WINDOWS):
        @pl.when(g == gi_)
        def _(w=w):
            lo_off, hi_off = w // 2, w - 1 - w // 2

            def step(c, carry):
                pooled = []
                for i in range(group):
                    n2 = c * group + i
                    total = e_scr[_POOL_HALO + n2 - lo_off]
                    for d in range(-lo_off + 1, hi_off + 1):
                        total = total + e_scr[_POOL_HALO + n2 + d]
                    pos = 128 * n1 + n2
                    cnt = (jnp.minimum(pos + hi_off, SEQ - 1) - jnp.maximum(pos - lo_off, 0) + 1).astype(F32)
                    pooled.append((total / cnt - e_scr[_POOL_HALO + n2]).astype(BF16))
                pooled = jnp.concatenate(pooled, axis=0)
                y = jnp.dot(pooled, pw_ref[...], preferred_element_type=F32) * ps_ref[...]
                for i in range(group):
                    n2 = c * group + i
                    o_ref[n2] = (y[i * TILE_ROWS:(i + 1) * TILE_ROWS] * _silu(z_ref[n2].astype(F32))).astype(BF16)
                return carry

            lax.fori_loop(0, N_TILES // group, step, 0)


def _pool(p5, pool_w_bf, pool_scale):
    u0 = 4 * N_CBLK
    z0 = 5 * N_CBLK
    return pl.pallas_call(
        _pool_kernel,
        grid=(BATCH, len(POOL_WINDOWS)),
        in_specs=[
            pl.BlockSpec((None, None, N_TILES, TILE_ROWS, LANES), lambda b, g: (b, u0 + g, 0, 0, 0)),
            pl.BlockSpec((None, None, N_TILES, TILE_ROWS, LANES), lambda b, g: (b, z0 + g, 0, 0, 0)),
            pl.BlockSpec((None, LANES, LANES), lambda b, g: (g, 0, 0)),
            pl.BlockSpec((1, LANES), lambda b, g: (0, g)),
        ],
        out_specs=pl.BlockSpec((None, None, N_TILES, TILE_ROWS, LANES), lambda b, g: (b, g, 0, 0, 0)),
        out_shape=jax.ShapeDtypeStruct((BATCH, N_CBLK, N_TILES, TILE_ROWS, LANES), BF16),
        scratch_shapes=[pltpu.VMEM((N_TILES + 2 * _POOL_HALO, TILE_ROWS, LANES), F32)],
        compiler_params=pltpu.CompilerParams(
            dimension_semantics=("arbitrary", "arbitrary"), vmem_limit_bytes=VMEM_LIMIT),
        name="pool",
    )(p5, p5, pool_w_bf, pool_scale)


def _out_kernel(yh_ref, yp_ref, x_ref, gh_ref, gp_ref, w_ref, gpost_ref, o_ref, r_scr):
    def group_norm(ref, gain_ref):
        rows = []
        for j in range(IN_TILES):
            rows.append(jnp.concatenate([ref[cb, j] for cb in range(N_CBLK)], axis=-1).astype(F32))
        y = jnp.concatenate(rows, axis=0)
        ms = jnp.mean(y * y, axis=-1, keepdims=True)
        return (y * lax.rsqrt(ms + EPS) * gain_ref[...]).astype(BF16)

    yc = jnp.concatenate([group_norm(yh_ref, gh_ref), group_norm(yp_ref, gp_ref)], axis=-1)
    out = jnp.dot(yc, w_ref[...], preferred_element_type=F32)
    ms = jnp.mean(out * out, axis=-1, keepdims=True)
    out = out * lax.rsqrt(ms + EPS) * gpost_ref[...]
    for j in range(IN_TILES):
        for k in range(D_MODEL // LANES):
            r_scr[k, pl.ds(j, TILE_ROWS, stride=IN_TILES), :] = out[j * TILE_ROWS:(j + 1) * TILE_ROWS,
                                                                    k * LANES:(k + 1) * LANES]
    r = jnp.concatenate([r_scr[k] for k in range(D_MODEL // LANES)], axis=-1)
    o_ref[...] = x_ref[...] + r.reshape(TILE_ROWS, IN_TILES, D_MODEL)


def _out_proj(yh, yp, x4, norm_h_g, norm_p_g, w_out_bf, post_g):
    grid = (BATCH, N_TILES // IN_TILES)
    y_spec = pl.BlockSpec((None, N_CBLK, IN_TILES, TILE_ROWS, LANES), lambda b, g: (b, 0, g, 0, 0))
    x_spec = pl.BlockSpec((None, TILE_ROWS, IN_TILES, D_MODEL), lambda b, g: (b, 0, g, 0))
    return pl.pallas_call(
        _out_kernel,
        grid=grid,
        in_specs=[
            y_spec, y_spec, x_spec,
            pl.BlockSpec((1, HYENA_WIDTH), lambda b, g: (0, 0)),
            pl.BlockSpec((1, POOL_WIDTH), lambda b, g: (0, 0)),
            pl.BlockSpec((D_MODEL, D_MODEL), lambda b, g: (0, 0)),
            pl.BlockSpec((1, D_MODEL), lambda b, g: (0, 0)),
        ],
        out_specs=x_spec,
        out_shape=jax.ShapeDtypeStruct((BATCH, TILE_ROWS, N_TILES, D_MODEL), F32),
        scratch_shapes=[pltpu.VMEM((D_MODEL // LANES, TILE_ROWS * IN_TILES, LANES), F32)],
        compiler_params=pltpu.CompilerParams(
            dimension_semantics=("arbitrary", "arbitrary"), vmem_limit_bytes=VMEM_LIMIT),
        name="out_proj",
    )(yh, yp, x4, norm_h_g, norm_p_g, w_out_bf, post_g)


def kernel(x, pre_norm_g, w_in, conv_w, conv_b, filt_w1, filt_b1, filt_w2, filt_b2, filt_w3, filt_b3,
           filt_freq, filt_w_out, hyena_d, pool_w, pool_scale, norm_h_g, norm_p_g, w_out, post_norm_g):
    assert x.shape == (BATCH, SEQ, D_MODEL) and pre_norm_g.shape[0] == 1
    f1, f1i, g, gi, g0, g0i = (jnp.asarray(m, F32).astype(BF16) for m in (_F1, _F1I, _G, _GI, _G0, _G0I))

    x4 = x.reshape(BATCH, TILE_ROWS, N_TILES, D_MODEL)
    p5 = _in_proj(x4, pre_norm_g, w_in[0].astype(BF16))

    w1p = jnp.pad(filt_w1[0], ((0, 40 - FILTER_EMB), (0, 0)))
    taps = _filter_mlp(jnp.asarray(_ZFEAT), w1p, filt_b1, filt_w2[0], filt_b2, filt_w3[0], filt_b3,
                       filt_freq, filt_w_out[0], jnp.asarray(_ABS_DELTAS))
    kspec = _filter_spec(taps, f1, g, g0)

    yh = _hyena(p5, conv_w[0], conv_b, hyena_d[0], kspec, f1, f1i, g, g0, gi, g0i)
    yp = _pool(p5, pool_w[0].astype(BF16), pool_scale)
    out4 = _out_proj(yh, yp, x4, norm_h_g, norm_p_g, w_out[0].astype(BF16), post_norm_g)
    return out4.reshape(BATCH, SEQ, D_MODEL)
```

```python
import functools
import math

import numpy as np
import jax
import jax.numpy as jnp
from jax import lax
from jax.experimental import pallas as pl
from jax.experimental.pallas import tpu as pltpu

F32 = jnp.float32
BF16 = jnp.bfloat16

D_MODEL = 1024
BATCH = 4
SEQ = 8192
HYENA_WIDTH = 512
POOL_WIDTH = 512
POOL_WINDOWS = (2, 4, 8, 16)
FILTER_EMB = 33
FILTER_BANDS = 16
FILTER_HIDDEN = 64
PROJ_WIDTH = 3072
EPS = 1e-6

LANES = 128
N_FFT = 2 * SEQ
N_TILES = 128
TILE_ROWS = SEQ // N_TILES
N_SLOTS = 64
CHUNK = 16
N_CHUNKS = N_TILES // CHUNK
A_STRIDE = 136
SLOT_BATCH = 4
N_CBLK = HYENA_WIDTH // LANES
PROJ_BLOCKS = PROJ_WIDTH // LANES
IN_TILES = 8
VMEM_LIMIT = 60 * 1024 * 1024


def _dft_tables():
    n1 = np.arange(TILE_ROWS)
    n2 = np.arange(N_TILES)
    s = np.arange(N_SLOTS)
    ph = 2 * np.pi * (n2[:, None, None] * s[None, :, None] / N_FFT
                      + n1[None, None, :] * s[None, :, None] / 128.0)
    f1 = np.zeros((N_TILES, 128, TILE_ROWS))
    f1[:, :64, :] = np.cos(ph)
    f1[:, 64:, :] = -np.sin(ph)
    f1[:, 0, :] = 1.0
    f1[:, 64, :] = (-1.0) ** n1
    php = np.transpose(ph, (0, 2, 1))
    f1i = np.zeros((N_TILES, TILE_ROWS, 128))
    f1i[:, :, :64] = 2 * np.cos(php) / N_FFT
    f1i[:, :, 64:] = -2 * np.sin(php) / N_FFT
    f1i[:, :, 0] = 1.0 / N_FFT
    f1i[:, :, 64] = ((-1.0) ** n1)[None, :] / N_FFT
    k2 = np.arange(128)
    th = 2 * np.pi * np.outer(k2, n2) / 128.0
    c, sn = np.cos(th), np.sin(th)
    g = np.block([[c, sn], [-sn, c]])
    gi = np.block([[c, -sn], [sn, c]])
    kk = np.arange(64)
    tha = 2 * np.pi * np.outer(kk, n2) / 128.0
    thb = 2 * np.pi * np.outer(64 + 128 * kk, n2) / N_FFT
    g0 = np.zeros((256, 256))
    g0[0:64, 0:128] = np.cos(tha)
    g0[64:128, 128:256] = np.cos(thb)
    g0[128:192, 0:128] = -np.sin(tha)
    g0[128, 0:128] = (-1.0) ** n2
    g0[192:256, 128:256] = -np.sin(thb)
    g0i = np.zeros((256, 256))
    g0i[0:128, 0:64] = 2 * np.cos(tha.T)
    g0i[0:128, 0] = 1.0
    g0i[0:128, 128:192] = -2 * np.sin(tha.T)
    g0i[0:128, 128] = (-1.0) ** n2
    g0i[128:256, 64:128] = 2 * np.cos(thb.T)
    g0i[128:256, 192:256] = -2 * np.sin(thb.T)
    q = np.arange(256)
    perm = np.where(q % 16 < 8, 8 * (q // 16) + q % 16, 128 + 8 * (q // 16) + q % 16 - 8)
    gf = np.stack([g0[perm, :], g[perm, :]])
    gb = np.stack([g0i[:, perm], gi[:, perm]])
    return f1, f1i, gf, gb


def _filter_features():
    pos = np.arange(SEQ, dtype=np.float64)
    t = pos / (SEQ - 1)
    ang = 2.0 * math.pi * pos / SEQ
    bands = np.linspace(1e-4, FILTER_BANDS - 1, FILTER_BANDS)
    z = np.concatenate([t[:, None], np.cos(bands[None, :] * ang[:, None]),
                        -np.sin(bands[None, :] * ang[:, None])], axis=-1)
    z = z.reshape(TILE_ROWS, N_TILES, FILTER_EMB).transpose(1, 0, 2).reshape(SEQ, FILTER_EMB)
    zp = np.zeros((SEQ, 40))
    zp[:, :FILTER_EMB] = z
    max_decay = math.log(1e-2) / 0.3
    min_decay = math.log(1e-2) / 1.5
    deltas = np.abs(np.linspace(min_decay, max_decay, HYENA_WIDTH))
    return zp.astype(np.float32), deltas.astype(np.float32)[None, :]


_F1, _F1I, _GF, _GB = _dft_tables()
_ZFEAT, _ABS_DELTAS = _filter_features()


def _shift_down(x):
    rows = lax.broadcasted_iota(jnp.int32, x.shape, 0)
    return jnp.where(rows == 0, 0.0, pltpu.roll(x, 1, axis=0))


def _shift_up(x):
    rows = lax.broadcasted_iota(jnp.int32, x.shape, 0)
    return jnp.where(rows == x.shape[0] - 1, 0.0, pltpu.roll(x, x.shape[0] - 1, axis=0))


def _pair(ref, i):
    return jnp.concatenate([ref[0, i], ref[1, i]], axis=-1).astype(F32)


def _dup(x):
    return jnp.concatenate([x, x], axis=-1)


def _silu(z):
    return z / (1.0 + jnp.exp(-z))


def _a_rows(n2):
    return pl.ds(pl.multiple_of(n2 * A_STRIDE, 8), 128)


def _slot_rows(r):
    return pl.ds(r, N_TILES, stride=A_STRIDE)


def _stage_load(a_ref, rows):
    return jnp.concatenate([a_ref[0, rows, :], a_ref[1, rows, :]], axis=-1)


def _stage_store(a_ref, rows, val):
    a_ref[0, rows, :] = val[:, :LANES]
    a_ref[1, rows, :] = val[:, LANES:]


def _short_conv_tile(chunk_ref, prev_ref, next_ref, i, tc, w_ref, b_ref):
    cur = _pair(chunk_ref, i)
    if i == 0:
        halo = _pair(prev_ref, 0)
        prev = jnp.where(tc == 0, _shift_down(halo), halo)
    else:
        prev = _pair(chunk_ref, i - 1)
    if i == CHUNK - 1:
        halo = _pair(next_ref, 0)
        nxt = jnp.where(tc == N_CHUNKS - 1, _shift_up(halo), halo)
    else:
        nxt = _pair(chunk_ref, i + 1)
    w = w_ref[...]
    return (prev * _dup(w[0:1]) + cur * _dup(w[1:2]) + nxt * _dup(w[2:3]) + _dup(b_ref[...]))


N_BATCHES = N_SLOTS // SLOT_BATCH
SPEC_BLOCKS = 16


def _batch_slots(i):
    return [i * SLOT_BATCH + j for j in range(SLOT_BATCH)]


def _stage2_forward(a_ref, gf_ref, slots, first):
    xs = []
    for s in slots:
        xr = _stage_load(a_ref, _slot_rows(s))
        xi = _stage_load(a_ref, _slot_rows(N_SLOTS + s))
        xs.append(jnp.concatenate([xr, xi], axis=0).astype(BF16))
    return [jnp.dot(gf_ref[0] if (first and j == 0) else gf_ref[1], x, preferred_element_type=F32)
            for j, x in enumerate(xs)]


def _re_im_blocks(y):
    return [(y[16 * i:16 * i + 8], y[16 * i + 8:16 * i + 16]) for i in range(SPEC_BLOCKS)]


def _in_proj_kernel(x_ref, g_ref, w_ref, o_ref, h_scr):
    x = x_ref[...].reshape(TILE_ROWS * IN_TILES, D_MODEL)
    ms = jnp.mean(x * x, axis=-1, keepdims=True)
    hn = x * lax.rsqrt(ms + EPS) * g_ref[...]
    for k in range(D_MODEL // LANES):
        h_scr[k] = hn[:, k * LANES:(k + 1) * LANES]
    h = jnp.concatenate(
        [jnp.concatenate([h_scr[k, pl.ds(j, TILE_ROWS, stride=IN_TILES), :] for k in range(D_MODEL // LANES)],
                         axis=-1).astype(BF16) for j in range(IN_TILES)], axis=0)
    ncol = 512
    for c in range(PROJ_WIDTH // ncol):
        p = jnp.dot(h, w_ref[:, c * ncol:(c + 1) * ncol], preferred_element_type=F32).astype(BF16)
        for cb in range(ncol // LANES):
            for j in range(IN_TILES):
                o_ref[c * (ncol // LANES) + cb, j] = p[j * TILE_ROWS:(j + 1) * TILE_ROWS,
                                                       cb * LANES:(cb + 1) * LANES]


def _in_proj(x4, pre_g, w_in_bf):
    grid = (BATCH, N_TILES // IN_TILES)
    return pl.pallas_call(
        _in_proj_kernel,
        grid=grid,
        in_specs=[
            pl.BlockSpec((None, TILE_ROWS, IN_TILES, D_MODEL), lambda b, g: (b, 0, g, 0)),
            pl.BlockSpec((1, D_MODEL), lambda b, g: (0, 0)),
            pl.BlockSpec((D_MODEL, PROJ_WIDTH), lambda b, g: (0, 0)),
        ],
        out_specs=pl.BlockSpec((None, PROJ_BLOCKS, IN_TILES, TILE_ROWS, LANES), lambda b, g: (b, 0, g, 0, 0)),
        out_shape=jax.ShapeDtypeStruct((BATCH, PROJ_BLOCKS, N_TILES, TILE_ROWS, LANES), BF16),
        scratch_shapes=[pltpu.VMEM((D_MODEL // LANES, TILE_ROWS * IN_TILES, LANES), F32)],
        compiler_params=pltpu.CompilerParams(
            dimension_semantics=("arbitrary", "arbitrary"), vmem_limit_bytes=VMEM_LIMIT),
        name="in_proj",
    )(x4, pre_g, w_in_bf)


def _filter_mlp_kernel(z_ref, w1_ref, b1_ref, w2_ref, b2_ref, w3_ref, b3_ref, fr_ref, wp_ref, dl_ref, o_ref):
    hp = lax.Precision.HIGHEST
    fr = fr_ref[...]
    h = jnp.sin(fr * (jnp.dot(z_ref[...], w1_ref[...], precision=hp, preferred_element_type=F32) + b1_ref[...]))
    h = jnp.sin(fr * (jnp.dot(h, w2_ref[...], precision=hp, preferred_element_type=F32) + b2_ref[...]))
    h = jnp.sin(fr * (jnp.dot(h, w3_ref[...], precision=hp, preferred_element_type=F32) + b3_ref[...]))
    rows = CHUNK * TILE_ROWS
    r = lax.broadcasted_iota(jnp.int32, (rows, HYENA_WIDTH), 0)
    pos = 128 * (r % TILE_ROWS) + CHUNK * pl.program_id(0) + r // TILE_ROWS
    t = pos.astype(F32) / float(SEQ - 1)
    decay = jnp.exp(-t * dl_ref[...])
    for od in range(4):
        taps = jnp.dot(h, wp_ref[:, od * HYENA_WIDTH:(od + 1) * HYENA_WIDTH], precision=hp,
                       preferred_element_type=F32) * decay
        if od % 2 == 1:
            taps = jnp.where(pos == 0, 0.0, taps)
        taps = taps.astype(BF16)
        for cb in range(N_CBLK):
            for i in range(CHUNK):
                o_ref[od * N_CBLK + cb, i] = taps[i * TILE_ROWS:(i + 1) * TILE_ROWS, cb * LANES:(cb + 1) * LANES]


def _filter_mlp(zfeat, w1p, b1, w2, b2, w3, b3, freq, w_proj, abs_deltas):
    full = lambda shape: pl.BlockSpec(shape, lambda c: (0,) * len(shape))
    return pl.pallas_call(
        _filter_mlp_kernel,
        grid=(N_CHUNKS,),
        in_specs=[
            pl.BlockSpec((CHUNK * TILE_ROWS, 40), lambda c: (c, 0)),
            full((40, FILTER_HIDDEN)), full((1, FILTER_HIDDEN)),
            full((FILTER_HIDDEN, FILTER_HIDDEN)), full((1, FILTER_HIDDEN)),
            full((FILTER_HIDDEN, FILTER_HIDDEN)), full((1, FILTER_HIDDEN)),
            full((1, FILTER_HIDDEN)),
            full((FILTER_HIDDEN, 4 * HYENA_WIDTH)),
            full((1, HYENA_WIDTH)),
        ],
        out_specs=pl.BlockSpec((4 * N_CBLK, CHUNK, TILE_ROWS, LANES), lambda c: (0, c, 0, 0)),
        out_shape=jax.ShapeDtypeStruct((4 * N_CBLK, N_TILES, TILE_ROWS, LANES), BF16),
        compiler_params=pltpu.CompilerParams(dimension_semantics=("arbitrary",), vmem_limit_bytes=VMEM_LIMIT),
        name="filter_mlp",
    )(zfeat, w1p, b1, w2, b2, w3, b3, freq, w_proj, abs_deltas)


def _filter_spec_kernel(hf_ref, hb_ref, f1_ref, gf_ref, k_ref, a_scr):
    t = pl.program_id(2)

    @pl.when(t < N_CHUNKS)
    def _():
        for i in range(CHUNK):
            n2 = t * CHUNK + i
            taps = jnp.concatenate([hf_ref[i], hb_ref[i]], axis=-1)
            _stage_store(a_scr, _a_rows(n2), jnp.dot(f1_ref[n2], taps, preferred_element_type=F32))

    @pl.when(t == N_CHUNKS)
    def _():
        def batch(slots, first):
            ys = _stage2_forward(a_scr, gf_ref, slots, first)
            for j, (s, y) in enumerate(zip(slots, ys)):
                for i, (yr, yi) in enumerate(_re_im_blocks(y)):
                    kr = yr[:, :LANES] + yr[:, LANES:]
                    ki = yi[:, :LANES] - yi[:, LANES:]
                    if first and j == 0 and i == 0:
                        rows = lax.broadcasted_iota(jnp.int32, ki.shape, 0)
                        ki = jnp.where(rows == 0, yi[:, :LANES] + yi[:, LANES:], ki)
                    k_ref[s, 16 * i:16 * i + 8, :] = kr
                    k_ref[s, 16 * i + 8:16 * i + 16, :] = ki

        batch(_batch_slots(0), True)

        def loop(i, carry):
            batch(_batch_slots(i), False)
            return carry

        lax.fori_loop(1, N_BATCHES, loop, 0)


def _filter_spec(taps, f1, gf):
    grid = (2, N_CBLK, N_CHUNKS + 1)
    chunk_idx = lambda t: jnp.minimum(t, N_CHUNKS - 1)
    const = lambda shape: pl.BlockSpec(shape, lambda o, c, t: (0,) * len(shape))
    return pl.pallas_call(
        _filter_spec_kernel,
        grid=grid,
        in_specs=[
            pl.BlockSpec((None, CHUNK, TILE_ROWS, LANES), lambda o, c, t: ((2 * o) * N_CBLK + c, chunk_idx(t), 0, 0)),
            pl.BlockSpec((None, CHUNK, TILE_ROWS, LANES), lambda o, c, t: ((2 * o + 1) * N_CBLK + c, chunk_idx(t), 0, 0)),
            const((N_TILES, 128, TILE_ROWS)), const((2, 256, 256)),
        ],
        out_specs=pl.BlockSpec((None, None, N_SLOTS, 256, LANES), lambda o, c, t: (o, c, 0, 0, 0)),
        out_shape=jax.ShapeDtypeStruct((2, N_CBLK, N_SLOTS, 256, LANES), F32),
        scratch_shapes=[pltpu.VMEM((2, N_TILES * A_STRIDE, LANES), F32)],
        compiler_params=pltpu.CompilerParams(
            dimension_semantics=("arbitrary", "arbitrary", "arbitrary"), vmem_limit_bytes=VMEM_LIMIT),
        name="filter_spec",
    )(taps, taps, f1, gf)


_T_S1 = 0
_T_F0 = N_CHUNKS
_T_M = N_CHUNKS + 1
_T_F1 = 2 * N_CHUNKS + 1
_T_E = 2 * N_CHUNKS + 2
_T_END = 3 * N_CHUNKS + 2


def _hyena_kernel(pv_ref, pvp_ref, pvn_ref, px1_ref, px1p_ref, px1n_ref, px2_ref, px2p_ref, px2n_ref, pz_ref,
                  wv_ref, bv_ref, w1_ref, b1_ref, w2_ref, b2_ref, d_ref, k_ref,
                  f1_ref, f1i_ref, gf_ref, gb_ref, o_ref, a_scr, u_scr, z_scr):
    t = pl.program_id(2)

    def stage1(n2, u_bf):
        _stage_store(a_scr, _a_rows(n2), jnp.dot(f1_ref[n2], u_bf, preferred_element_type=F32))

    def inv_stage1(n2):
        return jnp.dot(f1i_ref[n2], _stage_load(a_scr, _a_rows(n2)).astype(BF16), preferred_element_type=F32)

    def filter_multiply(slots, ys, first, buf):
        for j, (s, y) in enumerate(zip(slots, ys)):
            blocks = []
            for i, (yr, yi) in enumerate(_re_im_blocks(y)):
                kr = k_ref[s, 16 * i:16 * i + 8, :]
                ki = k_ref[s, 16 * i + 8:16 * i + 16, :]
                if first and j == 0 and i == 0:
                    rows = lax.broadcasted_iota(jnp.int32, kr.shape, 0)
                    ka, kb, kd = kr, jnp.where(rows == 0, 0.0, ki), jnp.where(rows == 0, ki, kr)
                else:
                    ka, kb, kd = kr, ki, kr
                ka, kb, kd = _dup(ka), _dup(kb), _dup(kd)
                blocks += [yr * ka - yi * kb, yr * kb + yi * kd]
            z_scr[buf, j] = jnp.concatenate(blocks, axis=0).astype(BF16)

    def stage2_inverse(slots, first_sel, buf):
        bms = [jnp.dot(gb_ref[first_sel] if j == 0 else gb_ref[1], z_scr[buf, j], preferred_element_type=F32)
               for j in range(SLOT_BATCH)]
        for s, bm in zip(slots, bms):
            _stage_store(a_scr, _slot_rows(s), bm[:128])
            _stage_store(a_scr, _slot_rows(N_SLOTS + s), bm[128:])

    def spectral_phase():
        slots0 = _batch_slots(0)
        filter_multiply(slots0, _stage2_forward(a_scr, gf_ref, slots0, True), True, 0)

        def loop(i, carry):
            slots = _batch_slots(i)
            ys = _stage2_forward(a_scr, gf_ref, slots, False)
            stage2_inverse(_batch_slots(i - 1), jnp.where(i == 1, 0, 1), (i - 1) % 2)
            filter_multiply(slots, ys, False, i % 2)
            return carry

        lax.fori_loop(1, N_BATCHES, loop, 0)
        stage2_inverse(_batch_slots(N_BATCHES - 1), 1, (N_BATCHES - 1) % 2)

    @pl.when(t < _T_F0)
    def _():
        for i in range(CHUNK):
            n2 = t * CHUNK + i
            v = _short_conv_tile(pv_ref, pvp_ref, pvn_ref, i, t, wv_ref, bv_ref).astype(BF16)
            u_scr[n2] = v
            stage1(n2, v)

    @pl.when((t == _T_F0) | (t == _T_F1))
    def _():
        spectral_phase()

    @pl.when((t >= _T_M) & (t < _T_F1))
    def _():
        tc = t - _T_M
        for i in range(CHUNK):
            n2 = tc * CHUNK + i
            y = inv_stage1(n2)
            gate = _short_conv_tile(px1_ref, px1p_ref, px1n_ref, i, tc, w1_ref, b1_ref)
            u = (gate * (y + _dup(d_ref[0:1]) * u_scr[n2].astype(F32))).astype(BF16)
            u_scr[n2] = u
            stage1(n2, u)

    @pl.when(t >= _T_E)
    def _():
        tc = t - _T_E
        for i in range(CHUNK):
            n2 = tc * CHUNK + i
            y = inv_stage1(n2)
            gate = _short_conv_tile(px2_ref, px2p_ref, px2n_ref, i, tc, w2_ref, b2_ref)
            res = gate * (y + _dup(d_ref[1:2]) * u_scr[n2].astype(F32)) * _silu(_pair(pz_ref, i))
            res = res.astype(BF16)
            o_ref[0, i] = res[:, :LANES]
            o_ref[1, i] = res[:, LANES:]


def _hyena(p5, conv_w, conv_b, hyena_d, kspec, f1, f1i, gf, gb):
    grid = (N_CBLK, BATCH // 2, _T_END)

    def clampc(t, start):
        return jnp.clip(t - start, 0, N_CHUNKS - 1)

    def chunk_spec(col0, start):
        return pl.BlockSpec((2, None, CHUNK, TILE_ROWS, LANES),
                            lambda c, b, t: (b, col0 + c, clampc(t, start), 0, 0))

    def prev_spec(col0, start):
        return pl.BlockSpec((2, None, 1, TILE_ROWS, LANES),
                            lambda c, b, t: (b, col0 + c, (clampc(t, start) * CHUNK + N_TILES - 1) % N_TILES, 0, 0))

    def next_spec(col0, start):
        return pl.BlockSpec((2, None, 1, TILE_ROWS, LANES),
                            lambda c, b, t: (b, col0 + c, (clampc(t, start) * CHUNK + CHUNK) % N_TILES, 0, 0))

    def lane_spec(rows, col0):
        return pl.BlockSpec((rows, LANES), lambda c, b, t: (0, col0 + c))

    const = lambda shape: pl.BlockSpec(shape, lambda c, b, t: (0,) * len(shape))
    in_specs = [
        chunk_spec(0, _T_S1), prev_spec(0, _T_S1), next_spec(0, _T_S1),
        chunk_spec(N_CBLK, _T_M), prev_spec(N_CBLK, _T_M), next_spec(N_CBLK, _T_M),
        chunk_spec(2 * N_CBLK, _T_E), prev_spec(2 * N_CBLK, _T_E), next_spec(2 * N_CBLK, _T_E),
        chunk_spec(3 * N_CBLK, _T_E),
        lane_spec(3, 0), lane_spec(1, 0),
        lane_spec(3, N_CBLK), lane_spec(1, N_CBLK),
        lane_spec(3, 2 * N_CBLK), lane_spec(1, 2 * N_CBLK),
        lane_spec(2, 0),
        pl.BlockSpec((None, None, N_SLOTS, 256, LANES), lambda c, b, t: (jnp.where(t >= _T_M, 1, 0), c, 0, 0, 0)),
        const((N_TILES, 128, TILE_ROWS)), const((N_TILES, TILE_ROWS, 128)),
        const((2, 256, 256)), const((2, 256, 256)),
    ]
    return pl.pallas_call(
        _hyena_kernel,
        grid=grid,
        in_specs=in_specs,
        out_specs=pl.BlockSpec((2, None, CHUNK, TILE_ROWS, LANES), lambda c, b, t: (b, c, clampc(t, _T_E), 0, 0)),
        out_shape=jax.ShapeDtypeStruct((BATCH, N_CBLK, N_TILES, TILE_ROWS, LANES), BF16),
        scratch_shapes=[pltpu.VMEM((2, N_TILES * A_STRIDE, LANES), F32),
                        pltpu.VMEM((N_TILES, TILE_ROWS, 2 * LANES), BF16),
                        pltpu.VMEM((2, SLOT_BATCH, 256, 2 * LANES), BF16)],
        compiler_params=pltpu.CompilerParams(
            dimension_semantics=("arbitrary", "arbitrary", "arbitrary"), vmem_limit_bytes=VMEM_LIMIT),
        name="hyena",
    )(p5, p5, p5, p5, p5, p5, p5, p5, p5, p5,
      conv_w, conv_b, conv_w, conv_b, conv_w, conv_b, hyena_d, kspec, f1, f1i, gf, gb)


_POOL_HALO = 8


def _pool_kernel(u_ref, z_ref, pw_ref, ps_ref, o_ref, e_scr):
    g = pl.program_id(1)

    def fill(n2, carry):
        e_scr[_POOL_HALO + n2] = u_ref[n2].astype(F32)
        return carry

    lax.fori_loop(0, N_TILES, fill, 0)
    for j in range(1, _POOL_HALO + 1):
        e_scr[_POOL_HALO - j] = _shift_down(u_ref[N_TILES - j].astype(F32))
    for j in range(_POOL_HALO):
        e_scr[_POOL_HALO + N_TILES + j] = _shift_up(u_ref[j].astype(F32))

    n1 = lax.broadcasted_iota(jnp.int32, (TILE_ROWS, LANES), 0)
    group = 8

    for gi_, w in enumerate(POOL_WINDOWS):
        @pl.when(g == gi_)
        def _(w=w):
            lo_off, hi_off = w // 2, w - 1 - w // 2

            def step(c, carry):
                pooled = []
                for i in range(group):
                    n2 = c * group + i
                    total = e_scr[_POOL_HALO + n2 - lo_off]
                    for d in range(-lo_off + 1, hi_off + 1):
                        total = total + e_scr[_POOL_HALO + n2 + d]
                    pos = 128 * n1 + n2
                    cnt = (jnp.minimum(pos + hi_off, SEQ - 1) - jnp.maximum(pos - lo_off, 0) + 1).astype(F32)
                    pooled.append((total / cnt - e_scr[_POOL_HALO + n2]).astype(BF16))
                pooled = jnp.concatenate(pooled, axis=0)
                y = jnp.dot(pooled, pw_ref[...], preferred_element_type=F32) * ps_ref[...]
                for i in range(group):
                    n2 = c * group + i
                    o_ref[n2] = (y[i * TILE_ROWS:(i + 1) * TILE_ROWS] * _silu(z_ref[n2].astype(F32))).astype(BF16)
                return carry

            lax.fori_loop(0, N_TILES // group, step, 0)


def _pool(p5, pool_w_bf, pool_scale):
    u0 = 4 * N_CBLK
    z0 = 5 * N_CBLK
    return pl.pallas_call(
        _pool_kernel,
        grid=(BATCH, len(POOL_WINDOWS)),
        in_specs=[
            pl.BlockSpec((None, None, N_TILES, TILE_ROWS, LANES), lambda b, g: (b, u0 + g, 0, 0, 0)),
            pl.BlockSpec((None, None, N_TILES, TILE_ROWS, LANES), lambda b, g: (b, z0 + g, 0, 0, 0)),
            pl.BlockSpec((None, LANES, LANES), lambda b, g: (g, 0, 0)),
            pl.BlockSpec((1, LANES), lambda b, g: (0, g)),
        ],
        out_specs=pl.BlockSpec((None, None, N_TILES, TILE_ROWS, LANES), lambda b, g: (b, g, 0, 0, 0)),
        out_shape=jax.ShapeDtypeStruct((BATCH, N_CBLK, N_TILES, TILE_ROWS, LANES), BF16),
        scratch_shapes=[pltpu.VMEM((N_TILES + 2 * _POOL_HALO, TILE_ROWS, LANES), F32)],
        compiler_params=pltpu.CompilerParams(
            dimension_semantics=("arbitrary", "arbitrary"), vmem_limit_bytes=VMEM_LIMIT),
        name="pool",
    )(p5, p5, pool_w_bf, pool_scale)


def _out_kernel(yh_ref, yp_ref, x_ref, gh_ref, gp_ref, w_ref, gpost_ref, o_ref, r_scr):
    def group_norm(ref, gain_ref):
        rows = []
        for j in range(IN_TILES):
            rows.append(jnp.concatenate([ref[cb, j] for cb in range(N_CBLK)], axis=-1).astype(F32))
        y = jnp.concatenate(rows, axis=0)
        ms = jnp.mean(y * y, axis=-1, keepdims=True)
        return (y * lax.rsqrt(ms + EPS) * gain_ref[...]).astype(BF16)

    yc = jnp.concatenate([group_norm(yh_ref, gh_ref), group_norm(yp_ref, gp_ref)], axis=-1)
    out = jnp.dot(yc, w_ref[...], preferred_element_type=F32)
    ms = jnp.mean(out * out, axis=-1, keepdims=True)
    out = out * lax.rsqrt(ms + EPS) * gpost_ref[...]
    for j in range(IN_TILES):
        for k in range(D_MODEL // LANES):
            r_scr[k, pl.ds(j, TILE_ROWS, stride=IN_TILES), :] = out[j * TILE_ROWS:(j + 1) * TILE_ROWS,
                                                                    k * LANES:(k + 1) * LANES]
    r = jnp.concatenate([r_scr[k] for k in range(D_MODEL // LANES)], axis=-1)
    o_ref[...] = x_ref[...] + r.reshape(TILE_ROWS, IN_TILES, D_MODEL)


def _out_proj(yh, yp, x4, norm_h_g, norm_p_g, w_out_bf, post_g):
    grid = (BATCH, N_TILES // IN_TILES)
    y_spec = pl.BlockSpec((None, N_CBLK, IN_TILES, TILE_ROWS, LANES), lambda b, g: (b, 0, g, 0, 0))
    x_spec = pl.BlockSpec((None, TILE_ROWS, IN_TILES, D_MODEL), lambda b, g: (b, 0, g, 0))
    return pl.pallas_call(
        _out_kernel,
        grid=grid,
        in_specs=[
            y_spec, y_spec, x_spec,
            pl.BlockSpec((1, HYENA_WIDTH), lambda b, g: (0, 0)),
            pl.BlockSpec((1, POOL_WIDTH), lambda b, g: (0, 0)),
            pl.BlockSpec((D_MODEL, D_MODEL), lambda b, g: (0, 0)),
            pl.BlockSpec((1, D_MODEL), lambda b, g: (0, 0)),
        ],
        out_specs=x_spec,
        out_shape=jax.ShapeDtypeStruct((BATCH, TILE_ROWS, N_TILES, D_MODEL), F32),
        scratch_shapes=[pltpu.VMEM((D_MODEL // LANES, TILE_ROWS * IN_TILES, LANES), F32)],
        compiler_params=pltpu.CompilerParams(
            dimension_semantics=("arbitrary", "arbitrary"), vmem_limit_bytes=VMEM_LIMIT),
        name="out_proj",
    )(yh, yp, x4, norm_h_g, norm_p_g, w_out_bf, post_g)


def kernel(x, pre_norm_g, w_in, conv_w, conv_b, filt_w1, filt_b1, filt_w2, filt_b2, filt_w3, filt_b3,
           filt_freq, filt_w_out, hyena_d, pool_w, pool_scale, norm_h_g, norm_p_g, w_out, post_norm_g):
    assert x.shape == (BATCH, SEQ, D_MODEL) and pre_norm_g.shape[0] == 1
    f1, f1i, gf, gb = (jnp.asarray(m, F32).astype(BF16) for m in (_F1, _F1I, _GF, _GB))

    x4 = x.reshape(BATCH, TILE_ROWS, N_TILES, D_MODEL)
    p5 = _in_proj(x4, pre_norm_g, w_in[0].astype(BF16))

    w1p = jnp.pad(filt_w1[0], ((0, 40 - FILTER_EMB), (0, 0)))
    taps = _filter_mlp(jnp.asarray(_ZFEAT), w1p, filt_b1, filt_w2[0], filt_b2, filt_w3[0], filt_b3,
                       filt_freq, filt_w_out[0], jnp.asarray(_ABS_DELTAS))
    kspec = _filter_spec(taps, f1, gf)

    yh = _hyena(p5, conv_w[0], conv_b, hyena_d[0], kspec, f1, f1i, gf, gb)
    yp = _pool(p5, pool_w[0].astype(BF16), pool_scale)
    out4 = _out_proj(yh, yp, x4, norm_h_g, norm_p_g, w_out[0].astype(BF16), post_norm_g)
    return out4.reshape(BATCH, SEQ, D_MODEL)
```

```python
import functools
import math

import numpy as np
import jax
import jax.numpy as jnp
from jax import lax
from jax.experimental import pallas as pl
from jax.experimental.pallas import tpu as pltpu

F32 = jnp.float32
BF16 = jnp.bfloat16

D_MODEL = 1024
BATCH = 4
SEQ = 8192
HYENA_WIDTH = 512
POOL_WIDTH = 512
POOL_WINDOWS = (2, 4, 8, 16)
FILTER_EMB = 33
FILTER_BANDS = 16
FILTER_HIDDEN = 64
PROJ_WIDTH = 3072
EPS = 1e-6

LANES = 128
N_FFT = 2 * SEQ
N_TILES = 128
TILE_ROWS = SEQ // N_TILES
N_SLOTS = 64
CHUNK = 32
N_CHUNKS = N_TILES // CHUNK
A_STRIDE = 136
SLOT_BATCH = 4
N_CBLK = HYENA_WIDTH // LANES
PROJ_BLOCKS = PROJ_WIDTH // LANES
IN_TILES = 8
VMEM_LIMIT = 60 * 1024 * 1024


def _dft_tables():
    n1 = np.arange(TILE_ROWS)
    n2 = np.arange(N_TILES)
    s = np.arange(N_SLOTS)
    ph = 2 * np.pi * (n2[:, None, None] * s[None, :, None] / N_FFT
                      + n1[None, None, :] * s[None, :, None] / 128.0)
    f1 = np.zeros((N_TILES, 128, TILE_ROWS))
    f1[:, :64, :] = np.cos(ph)
    f1[:, 64:, :] = -np.sin(ph)
    f1[:, 0, :] = 1.0
    f1[:, 64, :] = (-1.0) ** n1
    php = np.transpose(ph, (0, 2, 1))
    f1i = np.zeros((N_TILES, TILE_ROWS, 128))
    f1i[:, :, :64] = 2 * np.cos(php) / N_FFT
    f1i[:, :, 64:] = -2 * np.sin(php) / N_FFT
    f1i[:, :, 0] = 1.0 / N_FFT
    f1i[:, :, 64] = ((-1.0) ** n1)[None, :] / N_FFT
    k2 = np.arange(128)
    th = 2 * np.pi * np.outer(k2, n2) / 128.0
    c, sn = np.cos(th), np.sin(th)
    g = np.block([[c, sn], [-sn, c]])
    gi = np.block([[c, -sn], [sn, c]])
    kk = np.arange(64)
    tha = 2 * np.pi * np.outer(kk, n2) / 128.0
    thb = 2 * np.pi * np.outer(64 + 128 * kk, n2) / N_FFT
    g0 = np.zeros((256, 256))
    g0[0:64, 0:128] = np.cos(tha)
    g0[64:128, 128:256] = np.cos(thb)
    g0[128:192, 0:128] = -np.sin(tha)
    g0[128, 0:128] = (-1.0) ** n2
    g0[192:256, 128:256] = -np.sin(thb)
    g0i = np.zeros((256, 256))
    g0i[0:128, 0:64] = 2 * np.cos(tha.T)
    g0i[0:128, 0] = 1.0
    g0i[0:128, 128:192] = -2 * np.sin(tha.T)
    g0i[0:128, 128] = (-1.0) ** n2
    g0i[128:256, 64:128] = 2 * np.cos(thb.T)
    g0i[128:256, 192:256] = -2 * np.sin(thb.T)
    q = np.arange(256)
    perm = np.where(q % 16 < 8, 8 * (q // 16) + q % 16, 128 + 8 * (q // 16) + q % 16 - 8)
    gf = np.stack([g0[perm, :], g[perm, :]])
    gb = np.stack([g0i[:, perm], gi[:, perm]])
    return f1, f1i, gf, gb


def _filter_features():
    pos = np.arange(SEQ, dtype=np.float64)
    t = pos / (SEQ - 1)
    ang = 2.0 * math.pi * pos / SEQ
    bands = np.linspace(1e-4, FILTER_BANDS - 1, FILTER_BANDS)
    z = np.concatenate([t[:, None], np.cos(bands[None, :] * ang[:, None]),
                        -np.sin(bands[None, :] * ang[:, None])], axis=-1)
    z = z.reshape(TILE_ROWS, N_TILES, FILTER_EMB).transpose(1, 0, 2).reshape(SEQ, FILTER_EMB)
    zp = np.zeros((SEQ, 40))
    zp[:, :FILTER_EMB] = z
    max_decay = math.log(1e-2) / 0.3
    min_decay = math.log(1e-2) / 1.5
    deltas = np.abs(np.linspace(min_decay, max_decay, HYENA_WIDTH))
    return zp.astype(np.float32), deltas.astype(np.float32)[None, :]


_F1, _F1I, _GF, _GB = _dft_tables()
_ZFEAT, _ABS_DELTAS = _filter_features()


def _shift_down(x):
    rows = lax.broadcasted_iota(jnp.int32, x.shape, 0)
    return jnp.where(rows == 0, 0.0, pltpu.roll(x, 1, axis=0))


def _shift_up(x):
    rows = lax.broadcasted_iota(jnp.int32, x.shape, 0)
    return jnp.where(rows == x.shape[0] - 1, 0.0, pltpu.roll(x, x.shape[0] - 1, axis=0))


def _pair(ref, i):
    return jnp.concatenate([ref[0, i], ref[1, i]], axis=-1).astype(F32)


def _dup(x):
    return jnp.concatenate([x, x], axis=-1)


def _silu(z):
    hz = 0.5 * z
    return hz * (1.0 + jnp.tanh(hz))


def _a_rows(n2):
    return pl.ds(pl.multiple_of(n2 * A_STRIDE, 8), 128)


def _slot_rows(r):
    return pl.ds(r, N_TILES, stride=A_STRIDE)


def _stage_load(a_ref, rows):
    return jnp.concatenate([a_ref[0, rows, :], a_ref[1, rows, :]], axis=-1)


def _stage_store(a_ref, rows, val):
    a_ref[0, rows, :] = val[:, :LANES]
    a_ref[1, rows, :] = val[:, LANES:]


def _short_conv_chunk(chunk_ref, prev_ref, next_ref, tc, w_ref, b_ref):
    first = _pair(prev_ref, 0)
    first = jnp.where(tc == 0, _shift_down(first), first)
    last = _pair(next_ref, 0)
    last = jnp.where(tc == N_CHUNKS - 1, _shift_up(last), last)
    tiles = [first] + [_pair(chunk_ref, i) for i in range(CHUNK)] + [last]
    w = w_ref[...]
    w0, w1, w2, b = _dup(w[0:1]), _dup(w[1:2]), _dup(w[2:3]), _dup(b_ref[...])
    return [tiles[i] * w0 + tiles[i + 1] * w1 + tiles[i + 2] * w2 + b for i in range(CHUNK)]


N_BATCHES = N_SLOTS // SLOT_BATCH
SPEC_BLOCKS = 16


def _batch_slots(i):
    return [i * SLOT_BATCH + j for j in range(SLOT_BATCH)]


def _stage2_forward(a_ref, gf_ref, slots, first):
    xs = []
    for s in slots:
        xr = _stage_load(a_ref, _slot_rows(s))
        xi = _stage_load(a_ref, _slot_rows(N_SLOTS + s))
        xs.append(jnp.concatenate([xr, xi], axis=0).astype(BF16))
    return [jnp.dot(gf_ref[0] if (first and j == 0) else gf_ref[1], x, preferred_element_type=F32)
            for j, x in enumerate(xs)]


def _re_im_blocks(y):
    return [(y[16 * i:16 * i + 8], y[16 * i + 8:16 * i + 16]) for i in range(SPEC_BLOCKS)]


def _in_proj_kernel(x_ref, g_ref, w_ref, o_ref, h_scr):
    x = x_ref[...].reshape(TILE_ROWS * IN_TILES, D_MODEL)
    ms = jnp.mean(x * x, axis=-1, keepdims=True)
    hn = x * lax.rsqrt(ms + EPS) * g_ref[...]
    for k in range(D_MODEL // LANES):
        h_scr[k] = hn[:, k * LANES:(k + 1) * LANES]
    h = jnp.concatenate(
        [jnp.concatenate([h_scr[k, pl.ds(j, TILE_ROWS, stride=IN_TILES), :] for k in range(D_MODEL // LANES)],
                         axis=-1).astype(BF16) for j in range(IN_TILES)], axis=0)
    ncol = 512
    for c in range(PROJ_WIDTH // ncol):
        p = jnp.dot(h, w_ref[:, c * ncol:(c + 1) * ncol], preferred_element_type=F32).astype(BF16)
        for cb in range(ncol // LANES):
            for j in range(IN_TILES):
                o_ref[c * (ncol // LANES) + cb, j] = p[j * TILE_ROWS:(j + 1) * TILE_ROWS,
                                                       cb * LANES:(cb + 1) * LANES]


def _in_proj(x4, pre_g, w_in_bf):
    grid = (BATCH, N_TILES // IN_TILES)
    return pl.pallas_call(
        _in_proj_kernel,
        grid=grid,
        in_specs=[
            pl.BlockSpec((None, TILE_ROWS, IN_TILES, D_MODEL), lambda b, g: (b, 0, g, 0)),
            pl.BlockSpec((1, D_MODEL), lambda b, g: (0, 0)),
            pl.BlockSpec((D_MODEL, PROJ_WIDTH), lambda b, g: (0, 0)),
        ],
        out_specs=pl.BlockSpec((None, PROJ_BLOCKS, IN_TILES, TILE_ROWS, LANES), lambda b, g: (b, 0, g, 0, 0)),
        out_shape=jax.ShapeDtypeStruct((BATCH, PROJ_BLOCKS, N_TILES, TILE_ROWS, LANES), BF16),
        scratch_shapes=[pltpu.VMEM((D_MODEL // LANES, TILE_ROWS * IN_TILES, LANES), F32)],
        compiler_params=pltpu.CompilerParams(
            dimension_semantics=("arbitrary", "arbitrary"), vmem_limit_bytes=VMEM_LIMIT),
        name="in_proj",
    )(x4, pre_g, w_in_bf)


def _filter_mlp_kernel(z_ref, w1_ref, b1_ref, w2_ref, b2_ref, w3_ref, b3_ref, fr_ref, wp_ref, dl_ref, o_ref):
    hp = lax.Precision.HIGHEST
    fr = fr_ref[...]
    h = jnp.sin(fr * (jnp.dot(z_ref[...], w1_ref[...], precision=hp, preferred_element_type=F32) + b1_ref[...]))
    h = jnp.sin(fr * (jnp.dot(h, w2_ref[...], precision=hp, preferred_element_type=F32) + b2_ref[...]))
    h = jnp.sin(fr * (jnp.dot(h, w3_ref[...], precision=hp, preferred_element_type=F32) + b3_ref[...]))
    rows = CHUNK * TILE_ROWS
    r = lax.broadcasted_iota(jnp.int32, (rows, HYENA_WIDTH), 0)
    pos = 128 * (r % TILE_ROWS) + CHUNK * pl.program_id(0) + r // TILE_ROWS
    t = pos.astype(F32) / float(SEQ - 1)
    decay = jnp.exp(-t * dl_ref[...])
    for od in range(4):
        taps = jnp.dot(h, wp_ref[:, od * HYENA_WIDTH:(od + 1) * HYENA_WIDTH], precision=hp,
                       preferred_element_type=F32) * decay
        if od % 2 == 1:
            taps = jnp.where(pos == 0, 0.0, taps)
        taps = taps.astype(BF16)
        for cb in range(N_CBLK):
            for i in range(CHUNK):
                o_ref[od * N_CBLK + cb, i] = taps[i * TILE_ROWS:(i + 1) * TILE_ROWS, cb * LANES:(cb + 1) * LANES]


def _filter_mlp(zfeat, w1p, b1, w2, b2, w3, b3, freq, w_proj, abs_deltas):
    full = lambda shape: pl.BlockSpec(shape, lambda c: (0,) * len(shape))
    return pl.pallas_call(
        _filter_mlp_kernel,
        grid=(N_CHUNKS,),
        in_specs=[
            pl.BlockSpec((CHUNK * TILE_ROWS, 40), lambda c: (c, 0)),
            full((40, FILTER_HIDDEN)), full((1, FILTER_HIDDEN)),
            full((FILTER_HIDDEN, FILTER_HIDDEN)), full((1, FILTER_HIDDEN)),
            full((FILTER_HIDDEN, FILTER_HIDDEN)), full((1, FILTER_HIDDEN)),
            full((1, FILTER_HIDDEN)),
            full((FILTER_HIDDEN, 4 * HYENA_WIDTH)),
            full((1, HYENA_WIDTH)),
        ],
        out_specs=pl.BlockSpec((4 * N_CBLK, CHUNK, TILE_ROWS, LANES), lambda c: (0, c, 0, 0)),
        out_shape=jax.ShapeDtypeStruct((4 * N_CBLK, N_TILES, TILE_ROWS, LANES), BF16),
        compiler_params=pltpu.CompilerParams(dimension_semantics=("arbitrary",), vmem_limit_bytes=VMEM_LIMIT),
        name="filter_mlp",
    )(zfeat, w1p, b1, w2, b2, w3, b3, freq, w_proj, abs_deltas)


def _filter_spec_kernel(hf_ref, hb_ref, f1_ref, gf_ref, k_ref, a_scr):
    t = pl.program_id(2)

    @pl.when(t < N_CHUNKS)
    def _():
        for i in range(CHUNK):
            n2 = t * CHUNK + i
            taps = jnp.concatenate([hf_ref[i], hb_ref[i]], axis=-1)
            _stage_store(a_scr, _a_rows(n2), jnp.dot(f1_ref[n2], taps, preferred_element_type=F32))

    @pl.when(t == N_CHUNKS)
    def _():
        def batch(slots, first):
            ys = _stage2_forward(a_scr, gf_ref, slots, first)
            for j, (s, y) in enumerate(zip(slots, ys)):
                for i, (yr, yi) in enumerate(_re_im_blocks(y)):
                    kr = yr[:, :LANES] + yr[:, LANES:]
                    ki = yi[:, :LANES] - yi[:, LANES:]
                    if first and j == 0 and i == 0:
                        rows = lax.broadcasted_iota(jnp.int32, ki.shape, 0)
                        ki = jnp.where(rows == 0, yi[:, :LANES] + yi[:, LANES:], ki)
                    k_ref[s, 16 * i:16 * i + 8, :] = kr
                    k_ref[s, 16 * i + 8:16 * i + 16, :] = ki

        batch(_batch_slots(0), True)

        def loop(i, carry):
            batch(_batch_slots(i), False)
            return carry

        lax.fori_loop(1, N_BATCHES, loop, 0)


def _filter_spec(taps, f1, gf):
    grid = (2, N_CBLK, N_CHUNKS + 1)
    chunk_idx = lambda t: jnp.minimum(t, N_CHUNKS - 1)
    const = lambda shape: pl.BlockSpec(shape, lambda o, c, t: (0,) * len(shape), pipeline_mode=pl.Buffered(1))
    return pl.pallas_call(
        _filter_spec_kernel,
        grid=grid,
        in_specs=[
            pl.BlockSpec((None, CHUNK, TILE_ROWS, LANES), lambda o, c, t: ((2 * o) * N_CBLK + c, chunk_idx(t), 0, 0)),
            pl.BlockSpec((None, CHUNK, TILE_ROWS, LANES), lambda o, c, t: ((2 * o + 1) * N_CBLK + c, chunk_idx(t), 0, 0)),
            const((N_TILES, 128, TILE_ROWS)), const((2, 256, 256)),
        ],
        out_specs=pl.BlockSpec((None, None, N_SLOTS, 256, LANES), lambda o, c, t: (o, c, 0, 0, 0)),
        out_shape=jax.ShapeDtypeStruct((2, N_CBLK, N_SLOTS, 256, LANES), F32),
        scratch_shapes=[pltpu.VMEM((2, N_TILES * A_STRIDE, LANES), F32)],
        compiler_params=pltpu.CompilerParams(
            dimension_semantics=("arbitrary", "arbitrary", "arbitrary"), vmem_limit_bytes=VMEM_LIMIT),
        name="filter_spec",
    )(taps, taps, f1, gf)


_T_S1 = 0
_T_F0 = N_CHUNKS
_T_M = N_CHUNKS + 1
_T_F1 = 2 * N_CHUNKS + 1
_T_E = 2 * N_CHUNKS + 2
_T_END = 3 * N_CHUNKS + 2


def _hyena_kernel(pv_ref, pvp_ref, pvn_ref, px1_ref, px1p_ref, px1n_ref, px2_ref, px2p_ref, px2n_ref, pz_ref,
                  wv_ref, bv_ref, w1_ref, b1_ref, w2_ref, b2_ref, d_ref, k_ref,
                  f1_ref, f1i_ref, gf_ref, gb_ref, o_ref, a_scr, u_scr, z_scr):
    t = pl.program_id(2)

    def stage1(n2, u_bf):
        _stage_store(a_scr, _a_rows(n2), jnp.dot(f1_ref[n2], u_bf, preferred_element_type=F32))

    def inv_stage1(n2):
        return jnp.dot(f1i_ref[n2], _stage_load(a_scr, _a_rows(n2)).astype(BF16), preferred_element_type=F32)

    def filter_multiply(slots, ys, first, buf):
        for j, (s, y) in enumerate(zip(slots, ys)):
            blocks = []
            for i, (yr, yi) in enumerate(_re_im_blocks(y)):
                kr = k_ref[s, 16 * i:16 * i + 8, :]
                ki = k_ref[s, 16 * i + 8:16 * i + 16, :]
                if first and j == 0 and i == 0:
                    rows = lax.broadcasted_iota(jnp.int32, kr.shape, 0)
                    ka, kb, kd = kr, jnp.where(rows == 0, 0.0, ki), jnp.where(rows == 0, ki, kr)
                else:
                    ka, kb, kd = kr, ki, kr
                ka, kb, kd = _dup(ka), _dup(kb), _dup(kd)
                blocks += [yr * ka - yi * kb, yr * kb + yi * kd]
            z_scr[buf, j] = jnp.concatenate(blocks, axis=0).astype(BF16)

    def stage2_inverse(slots, first_sel, buf):
        bms = [jnp.dot(gb_ref[first_sel] if j == 0 else gb_ref[1], z_scr[buf, j], preferred_element_type=F32)
               for j in range(SLOT_BATCH)]
        for s, bm in zip(slots, bms):
            _stage_store(a_scr, _slot_rows(s), bm[:128])
            _stage_store(a_scr, _slot_rows(N_SLOTS + s), bm[128:])

    def spectral_phase():
        slots0 = _batch_slots(0)
        filter_multiply(slots0, _stage2_forward(a_scr, gf_ref, slots0, True), True, 0)

        def loop(i, carry):
            slots = _batch_slots(i)
            ys = _stage2_forward(a_scr, gf_ref, slots, False)
            stage2_inverse(_batch_slots(i - 1), jnp.where(i == 1, 0, 1), (i - 1) % 2)
            filter_multiply(slots, ys, False, i % 2)
            return carry

        lax.fori_loop(1, N_BATCHES, loop, 0)
        stage2_inverse(_batch_slots(N_BATCHES - 1), 1, (N_BATCHES - 1) % 2)

    @pl.when(t < _T_F0)
    def _():
        vs = _short_conv_chunk(pv_ref, pvp_ref, pvn_ref, t, wv_ref, bv_ref)
        for i in range(CHUNK):
            n2 = t * CHUNK + i
            v = vs[i].astype(BF16)
            u_scr[n2] = v
            stage1(n2, v)

    @pl.when((t == _T_F0) | (t == _T_F1))
    def _():
        spectral_phase()

    @pl.when((t >= _T_M) & (t < _T_F1))
    def _():
        tc = t - _T_M
        ys = [inv_stage1(tc * CHUNK + i) for i in range(CHUNK)]
        gates = _short_conv_chunk(px1_ref, px1p_ref, px1n_ref, tc, w1_ref, b1_ref)
        d0 = _dup(d_ref[0:1])
        us = [(gates[i] * (ys[i] + d0 * u_scr[tc * CHUNK + i].astype(F32))).astype(BF16) for i in range(CHUNK)]
        for i in range(CHUNK):
            n2 = tc * CHUNK + i
            u_scr[n2] = us[i]
            stage1(n2, us[i])

    @pl.when(t >= _T_E)
    def _():
        tc = t - _T_E
        ys = [inv_stage1(tc * CHUNK + i) for i in range(CHUNK)]
        gates = _short_conv_chunk(px2_ref, px2p_ref, px2n_ref, tc, w2_ref, b2_ref)
        d1 = _dup(d_ref[1:2])
        for i in range(CHUNK):
            n2 = tc * CHUNK + i
            res = gates[i] * (ys[i] + d1 * u_scr[n2].astype(F32)) * _silu(_pair(pz_ref, i))
            res = res.astype(BF16)
            o_ref[0, i] = res[:, :LANES]
            o_ref[1, i] = res[:, LANES:]


def _hyena(p5, conv_w, conv_b, hyena_d, kspec, f1, f1i, gf, gb):
    grid = (N_CBLK, BATCH // 2, _T_END)

    def clampc(t, start):
        return jnp.clip(t - start, 0, N_CHUNKS - 1)

    def chunk_spec(col0, start):
        return pl.BlockSpec((2, None, CHUNK, TILE_ROWS, LANES),
                            lambda c, b, t: (b, col0 + c, clampc(t, start), 0, 0))

    def prev_spec(col0, start):
        return pl.BlockSpec((2, None, 1, TILE_ROWS, LANES),
                            lambda c, b, t: (b, col0 + c, (clampc(t, start) * CHUNK + N_TILES - 1) % N_TILES, 0, 0))

    def next_spec(col0, start):
        return pl.BlockSpec((2, None, 1, TILE_ROWS, LANES),
                            lambda c, b, t: (b, col0 + c, (clampc(t, start) * CHUNK + CHUNK) % N_TILES, 0, 0))

    def lane_spec(rows, col0):
        return pl.BlockSpec((rows, LANES), lambda c, b, t: (0, col0 + c))

    const = lambda shape: pl.BlockSpec(shape, lambda c, b, t: (0,) * len(shape), pipeline_mode=pl.Buffered(1))
    in_specs = [
        chunk_spec(0, _T_S1), prev_spec(0, _T_S1), next_spec(0, _T_S1),
        chunk_spec(N_CBLK, _T_M), prev_spec(N_CBLK, _T_M), next_spec(N_CBLK, _T_M),
        chunk_spec(2 * N_CBLK, _T_E), prev_spec(2 * N_CBLK, _T_E), next_spec(2 * N_CBLK, _T_E),
        chunk_spec(3 * N_CBLK, _T_E),
        lane_spec(3, 0), lane_spec(1, 0),
        lane_spec(3, N_CBLK), lane_spec(1, N_CBLK),
        lane_spec(3, 2 * N_CBLK), lane_spec(1, 2 * N_CBLK),
        lane_spec(2, 0),
        pl.BlockSpec((None, None, N_SLOTS, 256, LANES), lambda c, b, t: (jnp.where(t >= _T_M, 1, 0), c, 0, 0, 0)),
        const((N_TILES, 128, TILE_ROWS)), const((N_TILES, TILE_ROWS, 128)),
        const((2, 256, 256)), const((2, 256, 256)),
    ]
    return pl.pallas_call(
        _hyena_kernel,
        grid=grid,
        in_specs=in_specs,
        out_specs=pl.BlockSpec((2, None, CHUNK, TILE_ROWS, LANES), lambda c, b, t: (b, c, clampc(t, _T_E), 0, 0)),
        out_shape=jax.ShapeDtypeStruct((BATCH, N_CBLK, N_TILES, TILE_ROWS, LANES), BF16),
        scratch_shapes=[pltpu.VMEM((2, N_TILES * A_STRIDE, LANES), F32),
                        pltpu.VMEM((N_TILES, TILE_ROWS, 2 * LANES), BF16),
                        pltpu.VMEM((2, SLOT_BATCH, 256, 2 * LANES), BF16)],
        compiler_params=pltpu.CompilerParams(
            dimension_semantics=("arbitrary", "arbitrary", "arbitrary"), vmem_limit_bytes=VMEM_LIMIT),
        name="hyena",
    )(p5, p5, p5, p5, p5, p5, p5, p5, p5, p5,
      conv_w, conv_b, conv_w, conv_b, conv_w, conv_b, hyena_d, kspec, f1, f1i, gf, gb)


_POOL_HALO = 8


def _pool_kernel(u_ref, z_ref, pw_ref, ps_ref, o_ref, e_scr):
    g = pl.program_id(1)

    def fill(n2, carry):
        e_scr[_POOL_HALO + n2] = u_ref[n2].astype(F32)
        return carry

    lax.fori_loop(0, N_TILES, fill, 0)
    for j in range(1, _POOL_HALO + 1):
        e_scr[_POOL_HALO - j] = _shift_down(u_ref[N_TILES - j].astype(F32))
    for j in range(_POOL_HALO):
        e_scr[_POOL_HALO + N_TILES + j] = _shift_up(u_ref[j].astype(F32))

    n1 = lax.broadcasted_iota(jnp.int32, (TILE_ROWS, LANES), 0)
    group = 8

    for gi_, w in enumerate(POOL_WINDOWS):
        @pl.when(g == gi_)
        def _(w=w):
            lo_off, hi_off = w // 2, w - 1 - w // 2

            def step(c, carry):
                pooled = []
                for i in range(group):
                    n2 = c * group + i
                    total = e_scr[_POOL_HALO + n2 - lo_off]
                    for d in range(-lo_off + 1, hi_off + 1):
                        total = total + e_scr[_POOL_HALO + n2 + d]
                    pos = 128 * n1 + n2
                    cnt = (jnp.minimum(pos + hi_off, SEQ - 1) - jnp.maximum(pos - lo_off, 0) + 1).astype(F32)
                    pooled.append((total / cnt - e_scr[_POOL_HALO + n2]).astype(BF16))
                pooled = jnp.concatenate(pooled, axis=0)
                y = jnp.dot(pooled, pw_ref[...], preferred_element_type=F32) * ps_ref[...]
                for i in range(group):
                    n2 = c * group + i
                    o_ref[n2] = (y[i * TILE_ROWS:(i + 1) * TILE_ROWS] * _silu(z_ref[n2].astype(F32))).astype(BF16)
                return carry

            lax.fori_loop(0, N_TILES // group, step, 0)


def _pool(p5, pool_w_bf, pool_scale):
    u0 = 4 * N_CBLK
    z0 = 5 * N_CBLK
    return pl.pallas_call(
        _pool_kernel,
        grid=(BATCH, len(POOL_WINDOWS)),
        in_specs=[
            pl.BlockSpec((None, None, N_TILES, TILE_ROWS, LANES), lambda b, g: (b, u0 + g, 0, 0, 0)),
            pl.BlockSpec((None, None, N_TILES, TILE_ROWS, LANES), lambda b, g: (b, z0 + g, 0, 0, 0)),
            pl.BlockSpec((None, LANES, LANES), lambda b, g: (g, 0, 0)),
            pl.BlockSpec((1, LANES), lambda b, g: (0, g)),
        ],
        out_specs=pl.BlockSpec((None, None, N_TILES, TILE_ROWS, LANES), lambda b, g: (b, g, 0, 0, 0)),
        out_shape=jax.ShapeDtypeStruct((BATCH, N_CBLK, N_TILES, TILE_ROWS, LANES), BF16),
        scratch_shapes=[pltpu.VMEM((N_TILES + 2 * _POOL_HALO, TILE_ROWS, LANES), F32)],
        compiler_params=pltpu.CompilerParams(
            dimension_semantics=("arbitrary", "arbitrary"), vmem_limit_bytes=VMEM_LIMIT),
        name="pool",
    )(p5, p5, pool_w_bf, pool_scale)


def _out_kernel(yh_ref, yp_ref, x_ref, gh_ref, gp_ref, w_ref, gpost_ref, o_ref, r_scr):
    def group_norm(ref, gain_ref):
        rows = []
        for j in range(IN_TILES):
            rows.append(jnp.concatenate([ref[cb, j] for cb in range(N_CBLK)], axis=-1).astype(F32))
        y = jnp.concatenate(rows, axis=0)
        ms = jnp.mean(y * y, axis=-1, keepdims=True)
        return (y * lax.rsqrt(ms + EPS) * gain_ref[...]).astype(BF16)

    yc = jnp.concatenate([group_norm(yh_ref, gh_ref), group_norm(yp_ref, gp_ref)], axis=-1)
    out = jnp.dot(yc, w_ref[...], preferred_element_type=F32)
    ms = jnp.mean(out * out, axis=-1, keepdims=True)
    out = out * lax.rsqrt(ms + EPS) * gpost_ref[...]
    for j in range(IN_TILES):
        for k in range(D_MODEL // LANES):
            r_scr[k, pl.ds(j, TILE_ROWS, stride=IN_TILES), :] = out[j * TILE_ROWS:(j + 1) * TILE_ROWS,
                                                                    k * LANES:(k + 1) * LANES]
    r = jnp.concatenate([r_scr[k] for k in range(D_MODEL // LANES)], axis=-1)
    o_ref[...] = x_ref[...] + r.reshape(TILE_ROWS, IN_TILES, D_MODEL)


def _out_proj(yh, yp, x4, norm_h_g, norm_p_g, w_out_bf, post_g):
    grid = (BATCH, N_TILES // IN_TILES)
    y_spec = pl.BlockSpec((None, N_CBLK, IN_TILES, TILE_ROWS, LANES), lambda b, g: (b, 0, g, 0, 0))
    x_spec = pl.BlockSpec((None, TILE_ROWS, IN_TILES, D_MODEL), lambda b, g: (b, 0, g, 0))
    return pl.pallas_call(
        _out_kernel,
        grid=grid,
        in_specs=[
            y_spec, y_spec, x_spec,
            pl.BlockSpec((1, HYENA_WIDTH), lambda b, g: (0, 0)),
            pl.BlockSpec((1, POOL_WIDTH), lambda b, g: (0, 0)),
            pl.BlockSpec((D_MODEL, D_MODEL), lambda b, g: (0, 0)),
            pl.BlockSpec((1, D_MODEL), lambda b, g: (0, 0)),
        ],
        out_specs=x_spec,
        out_shape=jax.ShapeDtypeStruct((BATCH, TILE_ROWS, N_TILES, D_MODEL), F32),
        scratch_shapes=[pltpu.VMEM((D_MODEL // LANES, TILE_ROWS * IN_TILES, LANES), F32)],
        compiler_params=pltpu.CompilerParams(
            dimension_semantics=("arbitrary", "arbitrary"), vmem_limit_bytes=VMEM_LIMIT),
        name="out_proj",
    )(yh, yp, x4, norm_h_g, norm_p_g, w_out_bf, post_g)


def kernel(x, pre_norm_g, w_in, conv_w, conv_b, filt_w1, filt_b1, filt_w2, filt_b2, filt_w3, filt_b3,
           filt_freq, filt_w_out, hyena_d, pool_w, pool_scale, norm_h_g, norm_p_g, w_out, post_norm_g):
    assert x.shape == (BATCH, SEQ, D_MODEL) and pre_norm_g.shape[0] == 1
    f1, f1i, gf, gb = (jnp.asarray(m, F32).astype(BF16) for m in (_F1, _F1I, _GF, _GB))

    x4 = x.reshape(BATCH, TILE_ROWS, N_TILES, D_MODEL)
    p5 = _in_proj(x4, pre_norm_g, w_in[0].astype(BF16))

    w1p = jnp.pad(filt_w1[0], ((0, 40 - FILTER_EMB), (0, 0)))
    taps = _filter_mlp(jnp.asarray(_ZFEAT), w1p, filt_b1, filt_w2[0], filt_b2, filt_w3[0], filt_b3,
                       filt_freq, filt_w_out[0], jnp.asarray(_ABS_DELTAS))
    kspec = _filter_spec(taps, f1, gf)

    yh = _hyena(p5, conv_w[0], conv_b, hyena_d[0], kspec, f1, f1i, gf, gb)
    yp = _pool(p5, pool_w[0].astype(BF16), pool_scale)
    out4 = _out_proj(yh, yp, x4, norm_h_g, norm_p_g, w_out[0].astype(BF16), post_norm_g)
    return out4.reshape(BATCH, SEQ, D_MODEL)
```

```python
import functools
import math

import numpy as np
import jax
import jax.numpy as jnp
from jax import lax
from jax.experimental import pallas as pl
from jax.experimental.pallas import tpu as pltpu

F32 = jnp.float32
BF16 = jnp.bfloat16

D_MODEL = 1024
BATCH = 4
SEQ = 8192
HYENA_WIDTH = 512
POOL_WIDTH = 512
POOL_WINDOWS = (2, 4, 8, 16)
FILTER_EMB = 33
FILTER_BANDS = 16
FILTER_HIDDEN = 64
PROJ_WIDTH = 3072
EPS = 1e-6

LANES = 128
N_FFT = 2 * SEQ
N_TILES = 128
TILE_ROWS = SEQ // N_TILES
N_SLOTS = 64
CHUNK = 32
N_CHUNKS = N_TILES // CHUNK
A_STRIDE = 136
SLOT_BATCH = 4
N_CBLK = HYENA_WIDTH // LANES
PROJ_BLOCKS = PROJ_WIDTH // LANES
IN_TILES = 8
VMEM_LIMIT = 60 * 1024 * 1024


def _dft_tables():
    n1 = np.arange(TILE_ROWS)
    n2 = np.arange(N_TILES)
    s = np.arange(N_SLOTS)
    ph = 2 * np.pi * (n2[:, None, None] * s[None, :, None] / N_FFT
                      + n1[None, None, :] * s[None, :, None] / 128.0)
    f1 = np.zeros((N_TILES, 128, TILE_ROWS))
    f1[:, :64, :] = np.cos(ph)
    f1[:, 64:, :] = -np.sin(ph)
    f1[:, 0, :] = 1.0
    f1[:, 64, :] = (-1.0) ** n1
    php = np.transpose(ph, (0, 2, 1))
    f1i = np.zeros((N_TILES, TILE_ROWS, 128))
    f1i[:, :, :64] = 2 * np.cos(php) / N_FFT
    f1i[:, :, 64:] = -2 * np.sin(php) / N_FFT
    f1i[:, :, 0] = 1.0 / N_FFT
    f1i[:, :, 64] = ((-1.0) ** n1)[None, :] / N_FFT
    k2 = np.arange(128)
    th = 2 * np.pi * np.outer(k2, n2) / 128.0
    c, sn = np.cos(th), np.sin(th)
    g = np.block([[c, sn], [-sn, c]])
    gi = np.block([[c, -sn], [sn, c]])
    kk = np.arange(64)
    tha = 2 * np.pi * np.outer(kk, n2) / 128.0
    thb = 2 * np.pi * np.outer(64 + 128 * kk, n2) / N_FFT
    g0 = np.zeros((256, 256))
    g0[0:64, 0:128] = np.cos(tha)
    g0[64:128, 128:256] = np.cos(thb)
    g0[128:192, 0:128] = -np.sin(tha)
    g0[128, 0:128] = (-1.0) ** n2
    g0[192:256, 128:256] = -np.sin(thb)
    g0i = np.zeros((256, 256))
    g0i[0:128, 0:64] = 2 * np.cos(tha.T)
    g0i[0:128, 0] = 1.0
    g0i[0:128, 128:192] = -2 * np.sin(tha.T)
    g0i[0:128, 128] = (-1.0) ** n2
    g0i[128:256, 64:128] = 2 * np.cos(thb.T)
    g0i[128:256, 192:256] = -2 * np.sin(thb.T)
    q = np.arange(256)
    perm = np.where(q % 16 < 8, 8 * (q // 16) + q % 16, 128 + 8 * (q // 16) + q % 16 - 8)
    gf = np.stack([g0[perm, :], g[perm, :]])
    gb = np.stack([g0i[:, perm], gi[:, perm]])
    n1f = np.arange(128)
    phf = 2 * np.pi * (n2[:, None, None] * s[None, :, None] / N_FFT
                       + n1f[None, None, :] * s[None, :, None] / 128.0)
    full = np.zeros((N_TILES, 128, 128))
    full[:, :64, :] = np.cos(phf)
    full[:, 64:, :] = -np.sin(phf)
    full[:, 0, :] = 1.0
    full[:, 64, :] = (-1.0) ** n1f
    rev = full[:, :, 127:63:-1].copy()
    rev[0, :, 1:] = full[0, :, 127:64:-1]
    rev[0, :, 0] = 0.0
    f1f = np.concatenate([full[:, :, :64], rev], axis=2)
    return f1, f1i, gf, gb, f1f


def _filter_features():
    pos = np.arange(SEQ, dtype=np.float64)
    t = pos / (SEQ - 1)
    ang = 2.0 * math.pi * pos / SEQ
    bands = np.linspace(1e-4, FILTER_BANDS - 1, FILTER_BANDS)
    z = np.concatenate([t[:, None], np.cos(bands[None, :] * ang[:, None]),
                        -np.sin(bands[None, :] * ang[:, None])], axis=-1)
    z = z.reshape(TILE_ROWS, N_TILES, FILTER_EMB).transpose(1, 0, 2).reshape(SEQ, FILTER_EMB)
    zp = np.zeros((FILTER_HIDDEN, SEQ))
    zp[:FILTER_EMB, :] = z.T
    max_decay = math.log(1e-2) / 0.3
    min_decay = math.log(1e-2) / 1.5
    deltas = np.abs(np.linspace(min_decay, max_decay, HYENA_WIDTH))
    return zp.astype(np.float32), deltas.astype(np.float32)[None, :]


_F1, _F1I, _GF, _GB, _F1F = _dft_tables()
_ZFEAT, _ABS_DELTAS = _filter_features()


def _shift_down(x):
    rows = lax.broadcasted_iota(jnp.int32, x.shape, 0)
    return jnp.where(rows == 0, 0.0, pltpu.roll(x, 1, axis=0))


def _shift_up(x):
    rows = lax.broadcasted_iota(jnp.int32, x.shape, 0)
    return jnp.where(rows == x.shape[0] - 1, 0.0, pltpu.roll(x, x.shape[0] - 1, axis=0))


def _pair(ref, i):
    return jnp.concatenate([ref[0, i], ref[1, i]], axis=-1).astype(F32)


def _dup(x):
    return jnp.concatenate([x, x], axis=-1)


def _silu(z):
    hz = 0.5 * z
    return hz * (1.0 + jnp.tanh(hz))


def _a_rows(n2):
    return pl.ds(pl.multiple_of(n2 * A_STRIDE, 8), 128)


def _slot_rows(r):
    return pl.ds(r, N_TILES, stride=A_STRIDE)


def _stage_load(a_ref, rows):
    return jnp.concatenate([a_ref[0, rows, :], a_ref[1, rows, :]], axis=-1)


def _stage_store(a_ref, rows, val):
    a_ref[0, rows, :] = val[:, :LANES]
    a_ref[1, rows, :] = val[:, LANES:]


def _short_conv_chunk(chunk_ref, prev_ref, next_ref, tc, w_ref, b_ref):
    first = _pair(prev_ref, 0)
    first = jnp.where(tc == 0, _shift_down(first), first)
    last = _pair(next_ref, 0)
    last = jnp.where(tc == N_CHUNKS - 1, _shift_up(last), last)
    tiles = [first] + [_pair(chunk_ref, i) for i in range(CHUNK)] + [last]
    w = w_ref[...]
    w0, w1, w2, b = _dup(w[0:1]), _dup(w[1:2]), _dup(w[2:3]), _dup(b_ref[...])
    return [tiles[i] * w0 + tiles[i + 1] * w1 + tiles[i + 2] * w2 + b for i in range(CHUNK)]


N_BATCHES = N_SLOTS // SLOT_BATCH
SPEC_BLOCKS = 16


def _batch_slots(i):
    return [i * SLOT_BATCH + j for j in range(SLOT_BATCH)]


def _stage2_forward(a_ref, gf_ref, slots, sel0):
    xs = []
    for s in slots:
        xr = _stage_load(a_ref, _slot_rows(s))
        xi = _stage_load(a_ref, _slot_rows(N_SLOTS + s))
        xs.append(jnp.concatenate([xr, xi], axis=0).astype(BF16))
    return [jnp.dot(gf_ref[sel0] if j == 0 else gf_ref[1], x, preferred_element_type=F32)
            for j, x in enumerate(xs)]


def _re_im_blocks(y):
    return [(y[16 * i:16 * i + 8], y[16 * i + 8:16 * i + 16]) for i in range(SPEC_BLOCKS)]


def _in_proj_kernel(x_ref, g_ref, w_ref, o_ref, h_scr):
    x = x_ref[...].reshape(TILE_ROWS * IN_TILES, D_MODEL)
    ms = jnp.mean(x * x, axis=-1, keepdims=True)
    hn = x * lax.rsqrt(ms + EPS) * g_ref[...]
    for k in range(D_MODEL // LANES):
        h_scr[k] = hn[:, k * LANES:(k + 1) * LANES]
    h = jnp.concatenate(
        [jnp.concatenate([h_scr[k, pl.ds(j, TILE_ROWS, stride=IN_TILES), :] for k in range(D_MODEL // LANES)],
                         axis=-1).astype(BF16) for j in range(IN_TILES)], axis=0)
    ncol = 512
    for c in range(PROJ_WIDTH // ncol):
        p = jnp.dot(h, w_ref[:, c * ncol:(c + 1) * ncol], preferred_element_type=F32).astype(BF16)
        for cb in range(ncol // LANES):
            for j in range(IN_TILES):
                o_ref[c * (ncol // LANES) + cb, j] = p[j * TILE_ROWS:(j + 1) * TILE_ROWS,
                                                       cb * LANES:(cb + 1) * LANES]


def _in_proj(x4, pre_g, w_in_bf):
    grid = (BATCH, N_TILES // IN_TILES)
    return pl.pallas_call(
        _in_proj_kernel,
        grid=grid,
        in_specs=[
            pl.BlockSpec((None, TILE_ROWS, IN_TILES, D_MODEL), lambda b, g: (b, 0, g, 0)),
            pl.BlockSpec((1, D_MODEL), lambda b, g: (0, 0)),
            pl.BlockSpec((D_MODEL, PROJ_WIDTH), lambda b, g: (0, 0)),
        ],
        out_specs=pl.BlockSpec((None, PROJ_BLOCKS, IN_TILES, TILE_ROWS, LANES), lambda b, g: (b, 0, g, 0, 0)),
        out_shape=jax.ShapeDtypeStruct((BATCH, PROJ_BLOCKS, N_TILES, TILE_ROWS, LANES), BF16),
        scratch_shapes=[pltpu.VMEM((D_MODEL // LANES, TILE_ROWS * IN_TILES, LANES), F32)],
        compiler_params=pltpu.CompilerParams(
            dimension_semantics=("arbitrary", "arbitrary"), vmem_limit_bytes=VMEM_LIMIT),
        name="in_proj",
    )(x4, pre_g, w_in_bf)


def _split_bf16(x):
    hi = x.astype(BF16)
    return hi, (x - hi.astype(F32)).astype(BF16)


def _stack_weight_rows(w):
    hi, lo = _split_bf16(w)
    return jnp.concatenate([hi, lo, hi], axis=1)


def _filter_mlp_kernel(z_ref, w1_ref, b1_ref, w2_ref, b2_ref, w3_ref, b3_ref, fr_ref, fr3_ref,
                       wp2_ref, wpl_ref, dl_ref, o_ref):
    def layer(w_ref, b_ref, f_ref, h):
        hi, lo = _split_bf16(h)
        pre = jnp.dot(w_ref[...], jnp.concatenate([hi, hi, lo], axis=0), preferred_element_type=F32)
        return jnp.sin(f_ref[...] * (pre + b_ref[...]))

    h = layer(w1_ref, b1_ref, fr_ref, z_ref[...])
    h = layer(w2_ref, b2_ref, fr_ref, h)
    h = layer(w3_ref, b3_ref, fr3_ref, h)
    hi, lo = _split_bf16(h.T)
    hs = jnp.concatenate([hi, lo], axis=1)
    rows = CHUNK * TILE_ROWS
    r = lax.broadcasted_iota(jnp.int32, (rows, HYENA_WIDTH), 0)
    pos = 128 * (r % TILE_ROWS) + CHUNK * pl.program_id(0) + r // TILE_ROWS
    t = pos.astype(F32) / float(SEQ - 1)
    decay = jnp.exp(-t * dl_ref[...])
    for od in range(4):
        cols = slice(od * HYENA_WIDTH, (od + 1) * HYENA_WIDTH)
        taps = (jnp.dot(hs, wp2_ref[:, cols], preferred_element_type=F32)
                + jnp.dot(hi, wpl_ref[:, cols], preferred_element_type=F32)) * decay
        taps = taps.astype(BF16)
        for cb in range(N_CBLK):
            for i in range(CHUNK):
                o_ref[od * N_CBLK + cb, i] = taps[i * TILE_ROWS:(i + 1) * TILE_ROWS, cb * LANES:(cb + 1) * LANES]


def _filter_mlp(zfeat_t, w1, b1, w2, b2, w3, b3, freq, w_proj, abs_deltas):
    hid = FILTER_HIDDEN
    col = lambda v, n: jnp.pad(v.reshape(-1, 1), ((0, n - v.size), (0, 0)))
    w1s = _stack_weight_rows(jnp.pad(w1, ((0, hid - FILTER_EMB), (0, 0))).T)
    w2s = _stack_weight_rows(w2.T)
    w3s = _stack_weight_rows(jnp.pad(w3.T, ((0, LANES - hid), (0, 0))))
    wp_hi, wp_lo = _split_bf16(jnp.pad(w_proj, ((0, LANES - hid), (0, 0))))
    wp2 = jnp.concatenate([wp_hi, wp_hi], axis=0)
    full = lambda shape: pl.BlockSpec(shape, lambda c: (0,) * len(shape))
    cols = CHUNK * TILE_ROWS
    return pl.pallas_call(
        _filter_mlp_kernel,
        grid=(N_CHUNKS,),
        in_specs=[
            pl.BlockSpec((hid, cols), lambda c: (0, c)),
            full((hid, 3 * hid)), full((hid, 1)),
            full((hid, 3 * hid)), full((hid, 1)),
            full((LANES, 3 * hid)), full((LANES, 1)),
            full((hid, 1)), full((LANES, 1)),
            full((2 * LANES, 4 * HYENA_WIDTH)), full((LANES, 4 * HYENA_WIDTH)),
            full((1, HYENA_WIDTH)),
        ],
        out_specs=pl.BlockSpec((4 * N_CBLK, CHUNK, TILE_ROWS, LANES), lambda c: (0, c, 0, 0)),
        out_shape=jax.ShapeDtypeStruct((4 * N_CBLK, N_TILES, TILE_ROWS, LANES), BF16),
        compiler_params=pltpu.CompilerParams(dimension_semantics=("arbitrary",), vmem_limit_bytes=VMEM_LIMIT),
        name="filter_mlp",
    )(zfeat_t, w1s, col(b1, hid), w2s, col(b2, hid), w3s, col(b3, LANES), col(freq, hid), col(freq, LANES),
      wp2, wp_lo, abs_deltas)


SPEC_SLOTS_PER_STEP = 16
N_SPEC_STEPS = N_SLOTS // SPEC_SLOTS_PER_STEP


def _filter_spec_kernel(fa_ref, fb_ref, ba_ref, bb_ref, bxa_ref, bxb_ref, f1f_ref, gf_ref, k_ref, a_scr):
    t = pl.program_id(2)

    @pl.when(t < N_CHUNKS)
    def _():
        for i in range(CHUNK):
            n2 = t * CHUNK + i
            fwd = jnp.concatenate([fa_ref[i], fb_ref[i]], axis=-1)
            if i == 0:
                bwd = jnp.concatenate([bxa_ref[0], bxb_ref[0]], axis=-1)
            else:
                bwd = jnp.concatenate([ba_ref[CHUNK - i], bb_ref[CHUNK - i]], axis=-1)
            taps = jnp.concatenate([fwd, bwd], axis=0)
            _stage_store(a_scr, _a_rows(n2), jnp.dot(f1f_ref[n2], taps, preferred_element_type=F32))

    @pl.when(t >= N_CHUNKS)
    def _():
        q = t - N_CHUNKS
        for bi in range(SPEC_SLOTS_PER_STEP // SLOT_BATCH):
            base = q * SPEC_SLOTS_PER_STEP + bi * SLOT_BATCH
            sel0 = jnp.where(q == 0, 0, 1) if bi == 0 else 1
            ys = _stage2_forward(a_scr, gf_ref, [base + j for j in range(SLOT_BATCH)], sel0)
            for j, y in enumerate(ys):
                k_ref[0, bi * SLOT_BATCH + j] = y[:, :LANES]
                k_ref[1, bi * SLOT_BATCH + j] = y[:, LANES:]


def _filter_spec(taps, f1f, gf):
    grid = (2, N_CBLK // 2, N_CHUNKS + N_SPEC_STEPS)
    fchunk = lambda t: jnp.minimum(t, N_CHUNKS - 1)
    const = lambda shape: pl.BlockSpec(shape, lambda o, c, t: (0,) * len(shape), pipeline_mode=pl.Buffered(1))

    def fwd_spec(k):
        return pl.BlockSpec((None, CHUNK, TILE_ROWS, LANES),
                            lambda o, c, t: ((2 * o) * N_CBLK + 2 * c + k, fchunk(t), 0, 0))

    def bwd_spec(k):
        return pl.BlockSpec((None, CHUNK, TILE_ROWS, LANES),
                            lambda o, c, t: ((2 * o + 1) * N_CBLK + 2 * c + k, N_CHUNKS - 1 - fchunk(t), 0, 0))

    def bwd_tile_spec(k):
        return pl.BlockSpec((None, 1, TILE_ROWS, LANES),
                            lambda o, c, t: ((2 * o + 1) * N_CBLK + 2 * c + k, (N_TILES - CHUNK * fchunk(t)) % N_TILES, 0, 0))

    return pl.pallas_call(
        _filter_spec_kernel,
        grid=grid,
        in_specs=[fwd_spec(0), fwd_spec(1), bwd_spec(0), bwd_spec(1), bwd_tile_spec(0), bwd_tile_spec(1),
                  const((N_TILES, 128, 128)), const((2, 256, 256))],
        out_specs=pl.BlockSpec((None, 2, SPEC_SLOTS_PER_STEP, 256, LANES),
                               lambda o, c, t: (o, c, jnp.maximum(t - N_CHUNKS, 0), 0, 0)),
        out_shape=jax.ShapeDtypeStruct((2, N_CBLK, N_SLOTS, 256, LANES), F32),
        scratch_shapes=[pltpu.VMEM((2, N_TILES * A_STRIDE, LANES), F32)],
        compiler_params=pltpu.CompilerParams(
            dimension_semantics=("arbitrary", "arbitrary", "arbitrary"), vmem_limit_bytes=VMEM_LIMIT),
        name="filter_spec",
    )(taps, taps, taps, taps, taps, taps, f1f, gf)


_T_S1 = 0
_T_F0 = N_CHUNKS
_T_M = N_CHUNKS + 1
_T_F1 = 2 * N_CHUNKS + 1
_T_E = 2 * N_CHUNKS + 2
_T_END = 3 * N_CHUNKS + 2


def _hyena_kernel(pv_ref, pvp_ref, pvn_ref, px1_ref, px1p_ref, px1n_ref, px2_ref, px2p_ref, px2n_ref, pz_ref,
                  wv_ref, bv_ref, w1_ref, b1_ref, w2_ref, b2_ref, d_ref, k_ref,
                  f1_ref, f1i_ref, gf_ref, gb_ref, o_ref, a_scr, u_scr, z_scr):
    t = pl.program_id(2)

    def stage1(n2, u_bf):
        _stage_store(a_scr, _a_rows(n2), jnp.dot(f1_ref[n2], u_bf, preferred_element_type=F32))

    def inv_stage1(n2):
        return jnp.dot(f1i_ref[n2], _stage_load(a_scr, _a_rows(n2)).astype(BF16), preferred_element_type=F32)

    def filter_multiply(slots, ys, first, buf):
        for j, (s, y) in enumerate(zip(slots, ys)):
            blocks = []
            for i, (yr, yi) in enumerate(_re_im_blocks(y)):
                kr = k_ref[s, 16 * i:16 * i + 8, :]
                ki = k_ref[s, 16 * i + 8:16 * i + 16, :]
                if first and j == 0 and i == 0:
                    rows = lax.broadcasted_iota(jnp.int32, kr.shape, 0)
                    ka, kb, kd = kr, jnp.where(rows == 0, 0.0, ki), jnp.where(rows == 0, ki, kr)
                else:
                    ka, kb, kd = kr, ki, kr
                ka, kb, kd = _dup(ka), _dup(kb), _dup(kd)
                blocks += [yr * ka - yi * kb, yr * kb + yi * kd]
            z_scr[buf, j] = jnp.concatenate(blocks, axis=0).astype(BF16)

    def stage2_inverse(slots, first_sel, buf):
        bms = [jnp.dot(gb_ref[first_sel] if j == 0 else gb_ref[1], z_scr[buf, j], preferred_element_type=F32)
               for j in range(SLOT_BATCH)]
        for s, bm in zip(slots, bms):
            _stage_store(a_scr, _slot_rows(s), bm[:128])
            _stage_store(a_scr, _slot_rows(N_SLOTS + s), bm[128:])

    def spectral_phase():
        slots0 = _batch_slots(0)
        filter_multiply(slots0, _stage2_forward(a_scr, gf_ref, slots0, 0), True, 0)

        def loop(i, carry):
            slots = _batch_slots(i)
            ys = _stage2_forward(a_scr, gf_ref, slots, 1)
            stage2_inverse(_batch_slots(i - 1), jnp.where(i == 1, 0, 1), (i - 1) % 2)
            filter_multiply(slots, ys, False, i % 2)
            return carry

        lax.fori_loop(1, N_BATCHES, loop, 0)
        stage2_inverse(_batch_slots(N_BATCHES - 1), 1, (N_BATCHES - 1) % 2)

    @pl.when(t < _T_F0)
    def _():
        vs = _short_conv_chunk(pv_ref, pvp_ref, pvn_ref, t, wv_ref, bv_ref)
        for i in range(CHUNK):
            n2 = t * CHUNK + i
            v = vs[i].astype(BF16)
            u_scr[n2] = v
            stage1(n2, v)

    @pl.when((t == _T_F0) | (t == _T_F1))
    def _():
        spectral_phase()

    @pl.when((t >= _T_M) & (t < _T_F1))
    def _():
        tc = t - _T_M
        ys = [inv_stage1(tc * CHUNK + i) for i in range(CHUNK)]
        gates = _short_conv_chunk(px1_ref, px1p_ref, px1n_ref, tc, w1_ref, b1_ref)
        d0 = _dup(d_ref[0:1])
        us = [(gates[i] * (ys[i] + d0 * u_scr[tc * CHUNK + i].astype(F32))).astype(BF16) for i in range(CHUNK)]
        for i in range(CHUNK):
            n2 = tc * CHUNK + i
            u_scr[n2] = us[i]
            stage1(n2, us[i])

    @pl.when(t >= _T_E)
    def _():
        tc = t - _T_E
        ys = [inv_stage1(tc * CHUNK + i) for i in range(CHUNK)]
        gates = _short_conv_chunk(px2_ref, px2p_ref, px2n_ref, tc, w2_ref, b2_ref)
        d1 = _dup(d_ref[1:2])
        for i in range(CHUNK):
            n2 = tc * CHUNK + i
            res = gates[i] * (ys[i] + d1 * u_scr[n2].astype(F32)) * _silu(_pair(pz_ref, i))
            res = res.astype(BF16)
            o_ref[0, i] = res[:, :LANES]
            o_ref[1, i] = res[:, LANES:]


def _hyena(p5, conv_w, conv_b, hyena_d, kspec, f1, f1i, gf, gb):
    grid = (N_CBLK, BATCH // 2, _T_END)

    def clampc(t, start):
        return jnp.clip(t - start, 0, N_CHUNKS - 1)

    def chunk_spec(col0, start):
        return pl.BlockSpec((2, None, CHUNK, TILE_ROWS, LANES),
                            lambda c, b, t: (b, col0 + c, clampc(t, start), 0, 0))

    def prev_spec(col0, start):
        return pl.BlockSpec((2, None, 1, TILE_ROWS, LANES),
                            lambda c, b, t: (b, col0 + c, (clampc(t, start) * CHUNK + N_TILES - 1) % N_TILES, 0, 0))

    def next_spec(col0, start):
        return pl.BlockSpec((2, None, 1, TILE_ROWS, LANES),
                            lambda c, b, t: (b, col0 + c, (clampc(t, start) * CHUNK + CHUNK) % N_TILES, 0, 0))

    def lane_spec(rows, col0):
        return pl.BlockSpec((rows, LANES), lambda c, b, t: (0, col0 + c))

    const = lambda shape: pl.BlockSpec(shape, lambda c, b, t: (0,) * len(shape), pipeline_mode=pl.Buffered(1))
    in_specs = [
        chunk_spec(0, _T_S1), prev_spec(0, _T_S1), next_spec(0, _T_S1),
        chunk_spec(N_CBLK, _T_M), prev_spec(N_CBLK, _T_M), next_spec(N_CBLK, _T_M),
        chunk_spec(2 * N_CBLK, _T_E), prev_spec(2 * N_CBLK, _T_E), next_spec(2 * N_CBLK, _T_E),
        chunk_spec(3 * N_CBLK, _T_E),
        lane_spec(3, 0), lane_spec(1, 0),
        lane_spec(3, N_CBLK), lane_spec(1, N_CBLK),
        lane_spec(3, 2 * N_CBLK), lane_spec(1, 2 * N_CBLK),
        lane_spec(2, 0),
        pl.BlockSpec((None, None, N_SLOTS, 256, LANES), lambda c, b, t: (jnp.where(t >= _T_M, 1, 0), c, 0, 0, 0)),
        const((N_TILES, 128, TILE_ROWS)), const((N_TILES, TILE_ROWS, 128)),
        const((2, 256, 256)), const((2, 256, 256)),
    ]
    return pl.pallas_call(
        _hyena_kernel,
        grid=grid,
        in_specs=in_specs,
        out_specs=pl.BlockSpec((2, None, CHUNK, TILE_ROWS, LANES), lambda c, b, t: (b, c, clampc(t, _T_E), 0, 0)),
        out_shape=jax.ShapeDtypeStruct((BATCH, N_CBLK, N_TILES, TILE_ROWS, LANES), BF16),
        scratch_shapes=[pltpu.VMEM((2, N_TILES * A_STRIDE, LANES), F32),
                        pltpu.VMEM((N_TILES, TILE_ROWS, 2 * LANES), BF16),
                        pltpu.VMEM((2, SLOT_BATCH, 256, 2 * LANES), BF16)],
        compiler_params=pltpu.CompilerParams(
            dimension_semantics=("arbitrary", "arbitrary", "arbitrary"), vmem_limit_bytes=VMEM_LIMIT),
        name="hyena",
    )(p5, p5, p5, p5, p5, p5, p5, p5, p5, p5,
      conv_w, conv_b, conv_w, conv_b, conv_w, conv_b, hyena_d, kspec, f1, f1i, gf, gb)


_POOL_HALO = 8


def _pool_kernel(u_ref, z_ref, pw_ref, ps_ref, o_ref, e_scr):
    g = pl.program_id(1)

    def fill(n2, carry):
        e_scr[_POOL_HALO + n2] = u_ref[n2].astype(F32)
        return carry

    lax.fori_loop(0, N_TILES, fill, 0)
    for j in range(1, _POOL_HALO + 1):
        e_scr[_POOL_HALO - j] = _shift_down(u_ref[N_TILES - j].astype(F32))
    for j in range(_POOL_HALO):
        e_scr[_POOL_HALO + N_TILES + j] = _shift_up(u_ref[j].astype(F32))

    n1 = lax.broadcasted_iota(jnp.int32, (TILE_ROWS, LANES), 0)
    group = 8

    for gi_, w in enumerate(POOL_WINDOWS):
        @pl.when(g == gi_)
        def _(w=w):
            lo_off, hi_off = w // 2, w - 1 - w // 2

            def step(c, carry):
                pooled = []
                for i in range(group):
                    n2 = c * group + i
                    total = e_scr[_POOL_HALO + n2 - lo_off]
                    for d in range(-lo_off + 1, hi_off + 1):
                        total = total + e_scr[_POOL_HALO + n2 + d]
                    pos = 128 * n1 + n2
                    cnt = (jnp.minimum(pos + hi_off, SEQ - 1) - jnp.maximum(pos - lo_off, 0) + 1).astype(F32)
                    pooled.append((total / cnt - e_scr[_POOL_HALO + n2]).astype(BF16))
                pooled = jnp.concatenate(pooled, axis=0)
                y = jnp.dot(pooled, pw_ref[...], preferred_element_type=F32) * ps_ref[...]
                for i in range(group):
                    n2 = c * group + i
                    o_ref[n2] = (y[i * TILE_ROWS:(i + 1) * TILE_ROWS] * _silu(z_ref[n2].astype(F32))).astype(BF16)
                return carry

            lax.fori_loop(0, N_TILES // group, step, 0)


def _pool(p5, pool_w_bf, pool_scale):
    u0 = 4 * N_CBLK
    z0 = 5 * N_CBLK
    return pl.pallas_call(
        _pool_kernel,
        grid=(BATCH, len(POOL_WINDOWS)),
        in_specs=[
            pl.BlockSpec((None, None, N_TILES, TILE_ROWS, LANES), lambda b, g: (b, u0 + g, 0, 0, 0)),
            pl.BlockSpec((None, None, N_TILES, TILE_ROWS, LANES), lambda b, g: (b, z0 + g, 0, 0, 0)),
            pl.BlockSpec((None, LANES, LANES), lambda b, g: (g, 0, 0)),
            pl.BlockSpec((1, LANES), lambda b, g: (0, g)),
        ],
        out_specs=pl.BlockSpec((None, None, N_TILES, TILE_ROWS, LANES), lambda b, g: (b, g, 0, 0, 0)),
        out_shape=jax.ShapeDtypeStruct((BATCH, N_CBLK, N_TILES, TILE_ROWS, LANES), BF16),
        scratch_shapes=[pltpu.VMEM((N_TILES + 2 * _POOL_HALO, TILE_ROWS, LANES), F32)],
        compiler_params=pltpu.CompilerParams(
            dimension_semantics=("arbitrary", "arbitrary"), vmem_limit_bytes=VMEM_LIMIT),
        name="pool",
    )(p5, p5, pool_w_bf, pool_scale)


def _out_kernel(yh_ref, yp_ref, x_ref, gh_ref, gp_ref, w_ref, gpost_ref, o_ref, r_scr):
    def group_norm(ref, gain_ref):
        rows = []
        for j in range(IN_TILES):
            rows.append(jnp.concatenate([ref[cb, j] for cb in range(N_CBLK)], axis=-1).astype(F32))
        y = jnp.concatenate(rows, axis=0)
        ms = jnp.mean(y * y, axis=-1, keepdims=True)
        return (y * lax.rsqrt(ms + EPS) * gain_ref[...]).astype(BF16)

    yc = jnp.concatenate([group_norm(yh_ref, gh_ref), group_norm(yp_ref, gp_ref)], axis=-1)
    out = jnp.dot(yc, w_ref[...], preferred_element_type=F32)
    ms = jnp.mean(out * out, axis=-1, keepdims=True)
    out = out * lax.rsqrt(ms + EPS) * gpost_ref[...]
    for j in range(IN_TILES):
        for k in range(D_MODEL // LANES):
            r_scr[k, pl.ds(j, TILE_ROWS, stride=IN_TILES), :] = out[j * TILE_ROWS:(j + 1) * TILE_ROWS,
                                                                    k * LANES:(k + 1) * LANES]
    r = jnp.concatenate([r_scr[k] for k in range(D_MODEL // LANES)], axis=-1)
    o_ref[...] = x_ref[...] + r.reshape(TILE_ROWS, IN_TILES, D_MODEL)


def _out_proj(yh, yp, x4, norm_h_g, norm_p_g, w_out_bf, post_g):
    grid = (BATCH, N_TILES // IN_TILES)
    y_spec = pl.BlockSpec((None, N_CBLK, IN_TILES, TILE_ROWS, LANES), lambda b, g: (b, 0, g, 0, 0))
    x_spec = pl.BlockSpec((None, TILE_ROWS, IN_TILES, D_MODEL), lambda b, g: (b, 0, g, 0))
    return pl.pallas_call(
        _out_kernel,
        grid=grid,
        in_specs=[
            y_spec, y_spec, x_spec,
            pl.BlockSpec((1, HYENA_WIDTH), lambda b, g: (0, 0)),
            pl.BlockSpec((1, POOL_WIDTH), lambda b, g: (0, 0)),
            pl.BlockSpec((D_MODEL, D_MODEL), lambda b, g: (0, 0)),
            pl.BlockSpec((1, D_MODEL), lambda b, g: (0, 0)),
        ],
        out_specs=x_spec,
        out_shape=jax.ShapeDtypeStruct((BATCH, TILE_ROWS, N_TILES, D_MODEL), F32),
        scratch_shapes=[pltpu.VMEM((D_MODEL // LANES, TILE_ROWS * IN_TILES, LANES), F32)],
        compiler_params=pltpu.CompilerParams(
            dimension_semantics=("arbitrary", "arbitrary"), vmem_limit_bytes=VMEM_LIMIT),
        name="out_proj",
    )(yh, yp, x4, norm_h_g, norm_p_g, w_out_bf, post_g)


def kernel(x, pre_norm_g, w_in, conv_w, conv_b, filt_w1, filt_b1, filt_w2, filt_b2, filt_w3, filt_b3,
           filt_freq, filt_w_out, hyena_d, pool_w, pool_scale, norm_h_g, norm_p_g, w_out, post_norm_g):
    assert x.shape == (BATCH, SEQ, D_MODEL) and pre_norm_g.shape[0] == 1
    f1, f1i, gf, gb, f1f = (jnp.asarray(m, F32).astype(BF16) for m in (_F1, _F1I, _GF, _GB, _F1F))

    x4 = x.reshape(BATCH, TILE_ROWS, N_TILES, D_MODEL)
    p5 = _in_proj(x4, pre_norm_g, w_in[0].astype(BF16))

    taps = _filter_mlp(jnp.asarray(_ZFEAT), filt_w1[0], filt_b1[0], filt_w2[0], filt_b2[0], filt_w3[0], filt_b3[0],
                       filt_freq[0], filt_w_out[0], jnp.asarray(_ABS_DELTAS))
    kspec = _filter_spec(taps, f1f, gf)

    yh = _hyena(p5, conv_w[0], conv_b, hyena_d[0], kspec, f1, f1i, gf, gb)
    yp = _pool(p5, pool_w[0].astype(BF16), pool_scale)
    out4 = _out_proj(yh, yp, x4, norm_h_g, norm_p_g, w_out[0].astype(BF16), post_norm_g)
    return out4.reshape(BATCH, SEQ, D_MODEL)
```

```python
import functools
import math

import numpy as np
import jax
import jax.numpy as jnp
from jax import lax
from jax.experimental import pallas as pl
from jax.experimental.pallas import tpu as pltpu

F32 = jnp.float32
BF16 = jnp.bfloat16

D_MODEL = 1024
BATCH = 4
SEQ = 8192
HYENA_WIDTH = 512
POOL_WIDTH = 512
POOL_WINDOWS = (2, 4, 8, 16)
FILTER_EMB = 33
FILTER_BANDS = 16
FILTER_HIDDEN = 64
PROJ_WIDTH = 3072
EPS = 1e-6

LANES = 128
N_FFT = 2 * SEQ
N_TILES = 128
TILE_ROWS = SEQ // N_TILES
N_SLOTS = 64
CHUNK = 32
N_CHUNKS = N_TILES // CHUNK
A_STRIDE = 136
SLOT_BATCH = 4
N_CBLK = HYENA_WIDTH // LANES
PROJ_BLOCKS = PROJ_WIDTH // LANES
IN_TILES = 16
SUB_TILES = 8
VMEM_LIMIT = 60 * 1024 * 1024


def _dft_tables():
    n1 = np.arange(TILE_ROWS)
    n2 = np.arange(N_TILES)
    s = np.arange(N_SLOTS)
    ph = 2 * np.pi * (n2[:, None, None] * s[None, :, None] / N_FFT
                      + n1[None, None, :] * s[None, :, None] / 128.0)
    f1 = np.zeros((N_TILES, 128, TILE_ROWS))
    f1[:, :64, :] = np.cos(ph)
    f1[:, 64:, :] = -np.sin(ph)
    f1[:, 0, :] = 1.0
    f1[:, 64, :] = (-1.0) ** n1
    php = np.transpose(ph, (0, 2, 1))
    f1i = np.zeros((N_TILES, TILE_ROWS, 128))
    f1i[:, :, :64] = 2 * np.cos(php) / N_FFT
    f1i[:, :, 64:] = -2 * np.sin(php) / N_FFT
    f1i[:, :, 0] = 1.0 / N_FFT
    f1i[:, :, 64] = ((-1.0) ** n1)[None, :] / N_FFT
    k2 = np.arange(128)
    th = 2 * np.pi * np.outer(k2, n2) / 128.0
    c, sn = np.cos(th), np.sin(th)
    g = np.block([[c, sn], [-sn, c]])
    gi = np.block([[c, -sn], [sn, c]])
    kk = np.arange(64)
    tha = 2 * np.pi * np.outer(kk, n2) / 128.0
    thb = 2 * np.pi * np.outer(64 + 128 * kk, n2) / N_FFT
    g0 = np.zeros((256, 256))
    g0[0:64, 0:128] = np.cos(tha)
    g0[64:128, 128:256] = np.cos(thb)
    g0[128:192, 0:128] = -np.sin(tha)
    g0[128, 0:128] = (-1.0) ** n2
    g0[192:256, 128:256] = -np.sin(thb)
    g0i = np.zeros((256, 256))
    g0i[0:128, 0:64] = 2 * np.cos(tha.T)
    g0i[0:128, 0] = 1.0
    g0i[0:128, 128:192] = -2 * np.sin(tha.T)
    g0i[0:128, 128] = (-1.0) ** n2
    g0i[128:256, 64:128] = 2 * np.cos(thb.T)
    g0i[128:256, 192:256] = -2 * np.sin(thb.T)
    q = np.arange(256)
    perm = np.where(q % 16 < 8, 8 * (q // 16) + q % 16, 128 + 8 * (q // 16) + q % 16 - 8)
    gf = np.stack([g0[perm, :], g[perm, :]])
    gb = np.stack([g0i[:, perm], gi[:, perm]])
    n1f = np.arange(128)
    phf = 2 * np.pi * (n2[:, None, None] * s[None, :, None] / N_FFT
                       + n1f[None, None, :] * s[None, :, None] / 128.0)
    full = np.zeros((N_TILES, 128, 128))
    full[:, :64, :] = np.cos(phf)
    full[:, 64:, :] = -np.sin(phf)
    full[:, 0, :] = 1.0
    full[:, 64, :] = (-1.0) ** n1f
    rev = full[:, :, 127:63:-1].copy()
    rev[0, :, 1:] = full[0, :, 127:64:-1]
    rev[0, :, 0] = 0.0
    f1f = np.concatenate([full[:, :, :64], rev], axis=2)
    return f1, f1i, gf, gb, f1f


def _filter_features():
    pos = np.arange(SEQ, dtype=np.float64)
    t = pos / (SEQ - 1)
    ang = 2.0 * math.pi * pos / SEQ
    bands = np.linspace(1e-4, FILTER_BANDS - 1, FILTER_BANDS)
    z = np.concatenate([t[:, None], np.cos(bands[None, :] * ang[:, None]),
                        -np.sin(bands[None, :] * ang[:, None])], axis=-1)
    z = z.reshape(TILE_ROWS, N_TILES, FILTER_EMB).transpose(1, 0, 2).reshape(SEQ, FILTER_EMB)
    zp = np.zeros((FILTER_HIDDEN, SEQ))
    zp[:FILTER_EMB, :] = z.T
    max_decay = math.log(1e-2) / 0.3
    min_decay = math.log(1e-2) / 1.5
    deltas = np.abs(np.linspace(min_decay, max_decay, HYENA_WIDTH))
    return zp.astype(np.float32), deltas.astype(np.float32)[None, :]


_F1, _F1I, _GF, _GB, _F1F = _dft_tables()
_ZFEAT, _ABS_DELTAS = _filter_features()


def _shift_down(x):
    rows = lax.broadcasted_iota(jnp.int32, x.shape, 0)
    return jnp.where(rows == 0, 0.0, pltpu.roll(x, 1, axis=0))


def _shift_up(x):
    rows = lax.broadcasted_iota(jnp.int32, x.shape, 0)
    return jnp.where(rows == x.shape[0] - 1, 0.0, pltpu.roll(x, x.shape[0] - 1, axis=0))


def _pair(ref, i):
    return jnp.concatenate([ref[0, i], ref[1, i]], axis=-1).astype(F32)


def _dup(x):
    return jnp.concatenate([x, x], axis=-1)


def _silu(z):
    hz = 0.5 * z
    return hz * (1.0 + jnp.tanh(hz))


def _a_rows(n2):
    return pl.ds(pl.multiple_of(n2 * A_STRIDE, 8), 128)


def _slot_rows(r):
    return pl.ds(r, N_TILES, stride=A_STRIDE)


def _stage_load(a_ref, rows):
    return jnp.concatenate([a_ref[0, rows, :], a_ref[1, rows, :]], axis=-1)


def _stage_store(a_ref, rows, val):
    a_ref[0, rows, :] = val[:, :LANES]
    a_ref[1, rows, :] = val[:, LANES:]


def _short_conv_chunk(chunk_ref, prev_ref, next_ref, tc, w_ref, b_ref):
    first = _pair(prev_ref, 0)
    first = jnp.where(tc == 0, _shift_down(first), first)
    last = _pair(next_ref, 0)
    last = jnp.where(tc == N_CHUNKS - 1, _shift_up(last), last)
    tiles = [first] + [_pair(chunk_ref, i) for i in range(CHUNK)] + [last]
    w = w_ref[...]
    w0, w1, w2, b = _dup(w[0:1]), _dup(w[1:2]), _dup(w[2:3]), _dup(b_ref[...])
    return [tiles[i] * w0 + tiles[i + 1] * w1 + tiles[i + 2] * w2 + b for i in range(CHUNK)]


N_BATCHES = N_SLOTS // SLOT_BATCH
SPEC_BLOCKS = 16


def _batch_slots(i):
    return [i * SLOT_BATCH + j for j in range(SLOT_BATCH)]


def _stage2_forward(a_ref, gf_ref, slots, sel0):
    xs = []
    for s in slots:
        xr = _stage_load(a_ref, _slot_rows(s))
        xi = _stage_load(a_ref, _slot_rows(N_SLOTS + s))
        xs.append(jnp.concatenate([xr, xi], axis=0).astype(BF16))
    return [jnp.dot(gf_ref[sel0] if j == 0 else gf_ref[1], x, preferred_element_type=F32)
            for j, x in enumerate(xs)]


def _re_im_blocks(y):
    return [(y[16 * i:16 * i + 8], y[16 * i + 8:16 * i + 16]) for i in range(SPEC_BLOCKS)]


def _in_proj_kernel(x_ref, g_ref, w_ref, o_ref, h_scr):
    ncol = 512
    for sb in range(IN_TILES // SUB_TILES):
        j0 = sb * SUB_TILES
        x = x_ref[:, j0:j0 + SUB_TILES, :].reshape(TILE_ROWS * SUB_TILES, D_MODEL)
        ms = jnp.mean(x * x, axis=-1, keepdims=True)
        hn = x * lax.rsqrt(ms + EPS) * g_ref[...]
        for k in range(D_MODEL // LANES):
            h_scr[sb, k] = hn[:, k * LANES:(k + 1) * LANES]
        h = jnp.concatenate(
            [jnp.concatenate([h_scr[sb, k, pl.ds(j, TILE_ROWS, stride=SUB_TILES), :]
                              for k in range(D_MODEL // LANES)], axis=-1).astype(BF16)
             for j in range(SUB_TILES)], axis=0)
        for c in range(PROJ_WIDTH // ncol):
            p = jnp.dot(h, w_ref[:, c * ncol:(c + 1) * ncol], preferred_element_type=F32).astype(BF16)
            for cb in range(ncol // LANES):
                for j in range(SUB_TILES):
                    o_ref[c * (ncol // LANES) + cb, j0 + j] = p[j * TILE_ROWS:(j + 1) * TILE_ROWS,
                                                                cb * LANES:(cb + 1) * LANES]


def _in_proj(x4, pre_g, w_in_bf):
    grid = (BATCH, N_TILES // IN_TILES)
    return pl.pallas_call(
        _in_proj_kernel,
        grid=grid,
        in_specs=[
            pl.BlockSpec((None, TILE_ROWS, IN_TILES, D_MODEL), lambda b, g: (b, 0, g, 0)),
            pl.BlockSpec((1, D_MODEL), lambda b, g: (0, 0)),
            pl.BlockSpec((D_MODEL, PROJ_WIDTH), lambda b, g: (0, 0)),
        ],
        out_specs=pl.BlockSpec((None, PROJ_BLOCKS, IN_TILES, TILE_ROWS, LANES), lambda b, g: (b, 0, g, 0, 0)),
        out_shape=jax.ShapeDtypeStruct((BATCH, PROJ_BLOCKS, N_TILES, TILE_ROWS, LANES), BF16),
        scratch_shapes=[pltpu.VMEM((IN_TILES // SUB_TILES, D_MODEL // LANES, TILE_ROWS * SUB_TILES, LANES), F32)],
        compiler_params=pltpu.CompilerParams(
            dimension_semantics=("arbitrary", "arbitrary"), vmem_limit_bytes=VMEM_LIMIT),
        name="in_proj",
    )(x4, pre_g, w_in_bf)


def _split_bf16(x):
    hi = x.astype(BF16)
    return hi, (x - hi.astype(F32)).astype(BF16)


def _stack_weight_rows(w):
    hi, lo = _split_bf16(w)
    return jnp.concatenate([hi, lo, hi], axis=1)


def _filter_mlp_kernel(z_ref, w1_ref, b1_ref, w2_ref, b2_ref, w3_ref, b3_ref, fr_ref, fr3_ref,
                       wp2_ref, wpl_ref, dl_ref, o_ref):
    def layer(w_ref, b_ref, f_ref, h):
        hi, lo = _split_bf16(h)
        pre = jnp.dot(w_ref[...], jnp.concatenate([hi, hi, lo], axis=0), preferred_element_type=F32)
        return jnp.sin(f_ref[...] * (pre + b_ref[...]))

    h = layer(w1_ref, b1_ref, fr_ref, z_ref[...])
    h = layer(w2_ref, b2_ref, fr_ref, h)
    h = layer(w3_ref, b3_ref, fr3_ref, h)
    hi, lo = _split_bf16(h.T)
    hs = jnp.concatenate([hi, lo], axis=1)
    rows = CHUNK * TILE_ROWS
    r = lax.broadcasted_iota(jnp.int32, (rows, HYENA_WIDTH), 0)
    pos = 128 * (r % TILE_ROWS) + CHUNK * pl.program_id(0) + r // TILE_ROWS
    t = pos.astype(F32) / float(SEQ - 1)
    decay = jnp.exp(-t * dl_ref[...])
    for od in range(4):
        cols = slice(od * HYENA_WIDTH, (od + 1) * HYENA_WIDTH)
        taps = (jnp.dot(hs, wp2_ref[:, cols], preferred_element_type=F32)
                + jnp.dot(hi, wpl_ref[:, cols], preferred_element_type=F32)) * decay
        taps = taps.astype(BF16)
        for cb in range(N_CBLK):
            for i in range(CHUNK):
                o_ref[od * N_CBLK + cb, i] = taps[i * TILE_ROWS:(i + 1) * TILE_ROWS, cb * LANES:(cb + 1) * LANES]


def _filter_mlp(zfeat_t, w1, b1, w2, b2, w3, b3, freq, w_proj, abs_deltas):
    hid = FILTER_HIDDEN
    col = lambda v, n: jnp.pad(v.reshape(-1, 1), ((0, n - v.size), (0, 0)))
    w1s = _stack_weight_rows(jnp.pad(w1, ((0, hid - FILTER_EMB), (0, 0))).T)
    w2s = _stack_weight_rows(w2.T)
    w3s = _stack_weight_rows(jnp.pad(w3.T, ((0, LANES - hid), (0, 0))))
    wp_hi, wp_lo = _split_bf16(jnp.pad(w_proj, ((0, LANES - hid), (0, 0))))
    wp2 = jnp.concatenate([wp_hi, wp_hi], axis=0)
    full = lambda shape: pl.BlockSpec(shape, lambda c: (0,) * len(shape))
    cols = CHUNK * TILE_ROWS
    return pl.pallas_call(
        _filter_mlp_kernel,
        grid=(N_CHUNKS,),
        in_specs=[
            pl.BlockSpec((hid, cols), lambda c: (0, c)),
            full((hid, 3 * hid)), full((hid, 1)),
            full((hid, 3 * hid)), full((hid, 1)),
            full((LANES, 3 * hid)), full((LANES, 1)),
            full((hid, 1)), full((LANES, 1)),
            full((2 * LANES, 4 * HYENA_WIDTH)), full((LANES, 4 * HYENA_WIDTH)),
            full((1, HYENA_WIDTH)),
        ],
        out_specs=pl.BlockSpec((4 * N_CBLK, CHUNK, TILE_ROWS, LANES), lambda c: (0, c, 0, 0)),
        out_shape=jax.ShapeDtypeStruct((4 * N_CBLK, N_TILES, TILE_ROWS, LANES), BF16),
        compiler_params=pltpu.CompilerParams(dimension_semantics=("arbitrary",), vmem_limit_bytes=VMEM_LIMIT),
        name="filter_mlp",
    )(zfeat_t, w1s, col(b1, hid), w2s, col(b2, hid), w3s, col(b3, LANES), col(freq, hid), col(freq, LANES),
      wp2, wp_lo, abs_deltas)


SPEC_SLOTS_PER_STEP = 16
N_SPEC_STEPS = N_SLOTS // SPEC_SLOTS_PER_STEP


def _filter_spec_kernel(fa_ref, fb_ref, ba_ref, bb_ref, bxa_ref, bxb_ref, f1f_ref, gf_ref, k_ref, a_scr):
    t = pl.program_id(2)

    @pl.when(t < N_CHUNKS)
    def _():
        for i in range(CHUNK):
            n2 = t * CHUNK + i
            fwd = jnp.concatenate([fa_ref[i], fb_ref[i]], axis=-1)
            if i == 0:
                bwd = jnp.concatenate([bxa_ref[0], bxb_ref[0]], axis=-1)
            else:
                bwd = jnp.concatenate([ba_ref[CHUNK - i], bb_ref[CHUNK - i]], axis=-1)
            taps = jnp.concatenate([fwd, bwd], axis=0)
            _stage_store(a_scr, _a_rows(n2), jnp.dot(f1f_ref[n2], taps, preferred_element_type=F32))

    @pl.when(t >= N_CHUNKS)
    def _():
        q = t - N_CHUNKS
        for bi in range(SPEC_SLOTS_PER_STEP // SLOT_BATCH):
            base = q * SPEC_SLOTS_PER_STEP + bi * SLOT_BATCH
            sel0 = jnp.where(q == 0, 0, 1) if bi == 0 else 1
            ys = _stage2_forward(a_scr, gf_ref, [base + j for j in range(SLOT_BATCH)], sel0)
            for j, y in enumerate(ys):
                k_ref[0, bi * SLOT_BATCH + j] = y[:, :LANES]
                k_ref[1, bi * SLOT_BATCH + j] = y[:, LANES:]


def _filter_spec(taps, f1f, gf):
    grid = (2, N_CBLK // 2, N_CHUNKS + N_SPEC_STEPS)
    fchunk = lambda t: jnp.minimum(t, N_CHUNKS - 1)
    const = lambda shape: pl.BlockSpec(shape, lambda o, c, t: (0,) * len(shape), pipeline_mode=pl.Buffered(1))

    def fwd_spec(k):
        return pl.BlockSpec((None, CHUNK, TILE_ROWS, LANES),
                            lambda o, c, t: ((2 * o) * N_CBLK + 2 * c + k, fchunk(t), 0, 0))

    def bwd_spec(k):
        return pl.BlockSpec((None, CHUNK, TILE_ROWS, LANES),
                            lambda o, c, t: ((2 * o + 1) * N_CBLK + 2 * c + k, N_CHUNKS - 1 - fchunk(t), 0, 0))

    def bwd_tile_spec(k):
        return pl.BlockSpec((None, 1, TILE_ROWS, LANES),
                            lambda o, c, t: ((2 * o + 1) * N_CBLK + 2 * c + k, (N_TILES - CHUNK * fchunk(t)) % N_TILES, 0, 0))

    return pl.pallas_call(
        _filter_spec_kernel,
        grid=grid,
        in_specs=[fwd_spec(0), fwd_spec(1), bwd_spec(0), bwd_spec(1), bwd_tile_spec(0), bwd_tile_spec(1),
                  const((N_TILES, 128, 128)), const((2, 256, 256))],
        out_specs=pl.BlockSpec((None, 2, SPEC_SLOTS_PER_STEP, 256, LANES),
                               lambda o, c, t: (o, c, jnp.maximum(t - N_CHUNKS, 0), 0, 0)),
        out_shape=jax.ShapeDtypeStruct((2, N_CBLK, N_SLOTS, 256, LANES), F32),
        scratch_shapes=[pltpu.VMEM((2, N_TILES * A_STRIDE, LANES), F32)],
        compiler_params=pltpu.CompilerParams(
            dimension_semantics=("arbitrary", "arbitrary", "arbitrary"), vmem_limit_bytes=VMEM_LIMIT),
        name="filter_spec",
    )(taps, taps, taps, taps, taps, taps, f1f, gf)


_T_S1 = 0
_T_F0 = N_CHUNKS
_T_M = N_CHUNKS + 1
_T_F1 = 2 * N_CHUNKS + 1
_T_E = 2 * N_CHUNKS + 2
_T_END = 3 * N_CHUNKS + 2


def _hyena_kernel(pv_ref, pvp_ref, pvn_ref, px1_ref, px1p_ref, px1n_ref, px2_ref, px2p_ref, px2n_ref, pz_ref,
                  wv_ref, bv_ref, w1_ref, b1_ref, w2_ref, b2_ref, d_ref, k_ref,
                  f1_ref, f1i_ref, gf_ref, gb_ref, o_ref, a_scr, u_scr, z_scr):
    t = pl.program_id(2)

    def stage1(n2, u_bf):
        _stage_store(a_scr, _a_rows(n2), jnp.dot(f1_ref[n2], u_bf, preferred_element_type=F32))

    def inv_stage1(n2):
        return jnp.dot(f1i_ref[n2], _stage_load(a_scr, _a_rows(n2)).astype(BF16), preferred_element_type=F32)

    def filter_multiply(slots, ys, first, buf):
        for j, (s, y) in enumerate(zip(slots, ys)):
            blocks = []
            for i, (yr, yi) in enumerate(_re_im_blocks(y)):
                kr = k_ref[s, 16 * i:16 * i + 8, :]
                ki = k_ref[s, 16 * i + 8:16 * i + 16, :]
                if first and j == 0 and i == 0:
                    rows = lax.broadcasted_iota(jnp.int32, kr.shape, 0)
                    ka, kb, kd = kr, jnp.where(rows == 0, 0.0, ki), jnp.where(rows == 0, ki, kr)
                else:
                    ka, kb, kd = kr, ki, kr
                ka, kb, kd = _dup(ka), _dup(kb), _dup(kd)
                blocks += [yr * ka - yi * kb, yr * kb + yi * kd]
            z_scr[buf, j] = jnp.concatenate(blocks, axis=0).astype(BF16)

    def stage2_inverse(slots, first_sel, buf):
        bms = [jnp.dot(gb_ref[first_sel] if j == 0 else gb_ref[1], z_scr[buf, j], preferred_element_type=F32)
               for j in range(SLOT_BATCH)]
        for s, bm in zip(slots, bms):
            _stage_store(a_scr, _slot_rows(s), bm[:128])
            _stage_store(a_scr, _slot_rows(N_SLOTS + s), bm[128:])

    def spectral_phase():
        slots0 = _batch_slots(0)
        filter_multiply(slots0, _stage2_forward(a_scr, gf_ref, slots0, 0), True, 0)

        def loop(i, carry):
            slots = _batch_slots(i)
            ys = _stage2_forward(a_scr, gf_ref, slots, 1)
            stage2_inverse(_batch_slots(i - 1), jnp.where(i == 1, 0, 1), (i - 1) % 2)
            filter_multiply(slots, ys, False, i % 2)
            return carry

        lax.fori_loop(1, N_BATCHES, loop, 0)
        stage2_inverse(_batch_slots(N_BATCHES - 1), 1, (N_BATCHES - 1) % 2)

    @pl.when(t < _T_F0)
    def _():
        vs = _short_conv_chunk(pv_ref, pvp_ref, pvn_ref, t, wv_ref, bv_ref)
        for i in range(CHUNK):
            n2 = t * CHUNK + i
            v = vs[i].astype(BF16)
            u_scr[n2] = v
            stage1(n2, v)

    @pl.when((t == _T_F0) | (t == _T_F1))
    def _():
        spectral_phase()

    @pl.when((t >= _T_M) & (t < _T_F1))
    def _():
        tc = t - _T_M
        ys = [inv_stage1(tc * CHUNK + i) for i in range(CHUNK)]
        gates = _short_conv_chunk(px1_ref, px1p_ref, px1n_ref, tc, w1_ref, b1_ref)
        d0 = _dup(d_ref[0:1])
        us = [(gates[i] * (ys[i] + d0 * u_scr[tc * CHUNK + i].astype(F32))).astype(BF16) for i in range(CHUNK)]
        for i in range(CHUNK):
            n2 = tc * CHUNK + i
            u_scr[n2] = us[i]
            stage1(n2, us[i])

    @pl.when(t >= _T_E)
    def _():
        tc = t - _T_E
        ys = [inv_stage1(tc * CHUNK + i) for i in range(CHUNK)]
        gates = _short_conv_chunk(px2_ref, px2p_ref, px2n_ref, tc, w2_ref, b2_ref)
        d1 = _dup(d_ref[1:2])
        for i in range(CHUNK):
            n2 = tc * CHUNK + i
            res = gates[i] * (ys[i] + d1 * u_scr[n2].astype(F32)) * _silu(_pair(pz_ref, i))
            res = res.astype(BF16)
            o_ref[0, i] = res[:, :LANES]
            o_ref[1, i] = res[:, LANES:]


def _hyena(p5, conv_w, conv_b, hyena_d, kspec, f1, f1i, gf, gb):
    grid = (N_CBLK, BATCH // 2, _T_END)

    def clampc(t, start):
        return jnp.clip(t - start, 0, N_CHUNKS - 1)

    def chunk_spec(col0, start):
        return pl.BlockSpec((2, None, CHUNK, TILE_ROWS, LANES),
                            lambda c, b, t: (b, col0 + c, clampc(t, start), 0, 0))

    def prev_spec(col0, start):
        return pl.BlockSpec((2, None, 1, TILE_ROWS, LANES),
                            lambda c, b, t: (b, col0 + c, (clampc(t, start) * CHUNK + N_TILES - 1) % N_TILES, 0, 0))

    def next_spec(col0, start):
        return pl.BlockSpec((2, None, 1, TILE_ROWS, LANES),
                            lambda c, b, t: (b, col0 + c, (clampc(t, start) * CHUNK + CHUNK) % N_TILES, 0, 0))

    def lane_spec(rows, col0):
        return pl.BlockSpec((rows, LANES), lambda c, b, t: (0, col0 + c))

    const = lambda shape: pl.BlockSpec(shape, lambda c, b, t: (0,) * len(shape), pipeline_mode=pl.Buffered(1))
    in_specs = [
        chunk_spec(0, _T_S1), prev_spec(0, _T_S1), next_spec(0, _T_S1),
        chunk_spec(N_CBLK, _T_M), prev_spec(N_CBLK, _T_M), next_spec(N_CBLK, _T_M),
        chunk_spec(2 * N_CBLK, _T_E), prev_spec(2 * N_CBLK, _T_E), next_spec(2 * N_CBLK, _T_E),
        chunk_spec(3 * N_CBLK, _T_E),
        lane_spec(3, 0), lane_spec(1, 0),
        lane_spec(3, N_CBLK), lane_spec(1, N_CBLK),
        lane_spec(3, 2 * N_CBLK), lane_spec(1, 2 * N_CBLK),
        lane_spec(2, 0),
        pl.BlockSpec((None, None, N_SLOTS, 256, LANES), lambda c, b, t: (jnp.where(t >= _T_M, 1, 0), c, 0, 0, 0)),
        const((N_TILES, 128, TILE_ROWS)), const((N_TILES, TILE_ROWS, 128)),
        const((2, 256, 256)), const((2, 256, 256)),
    ]
    return pl.pallas_call(
        _hyena_kernel,
        grid=grid,
        in_specs=in_specs,
        out_specs=pl.BlockSpec((2, None, CHUNK, TILE_ROWS, LANES), lambda c, b, t: (b, c, clampc(t, _T_E), 0, 0)),
        out_shape=jax.ShapeDtypeStruct((BATCH, N_CBLK, N_TILES, TILE_ROWS, LANES), BF16),
        scratch_shapes=[pltpu.VMEM((2, N_TILES * A_STRIDE, LANES), F32),
                        pltpu.VMEM((N_TILES, TILE_ROWS, 2 * LANES), BF16),
                        pltpu.VMEM((2, SLOT_BATCH, 256, 2 * LANES), BF16)],
        compiler_params=pltpu.CompilerParams(
            dimension_semantics=("arbitrary", "arbitrary", "arbitrary"), vmem_limit_bytes=VMEM_LIMIT),
        name="hyena",
    )(p5, p5, p5, p5, p5, p5, p5, p5, p5, p5,
      conv_w, conv_b, conv_w, conv_b, conv_w, conv_b, hyena_d, kspec, f1, f1i, gf, gb)


_POOL_GROUP = 8


def _pool_kernel(u_ref, z_ref, pw_ref, ps_ref, o_ref):
    g = pl.program_id(1)
    n_groups = N_TILES // _POOL_GROUP
    n1 = lax.broadcasted_iota(jnp.int32, (TILE_ROWS, LANES), 0)

    def tile(idx):
        if isinstance(idx, int) and idx < 0:
            return _shift_down(u_ref[idx + N_TILES].astype(F32))
        if isinstance(idx, int) and idx >= N_TILES:
            return _shift_up(u_ref[idx - N_TILES].astype(F32))
        return u_ref[idx].astype(F32)

    for gi_, w in enumerate(POOL_WINDOWS):
        @pl.when(g == gi_)
        def _(w=w):
            lo, hi = w // 2, w - 1 - w // 2

            def run_group(base, window_sum, edge):
                pooled = []
                for i in range(_POOL_GROUP):
                    n2 = base + i
                    if i > 0:
                        window_sum = window_sum + tile(n2 + hi) - tile(n2 - 1 - lo)
                    if edge:
                        pos = 128 * n1 + n2
                        cnt = (jnp.minimum(pos + hi, SEQ - 1) - jnp.maximum(pos - lo, 0) + 1).astype(F32)
                        mean = window_sum / cnt
                    else:
                        mean = window_sum * (1.0 / w)
                    pooled.append((mean - tile(n2)).astype(BF16))
                pooled = jnp.concatenate(pooled, axis=0)
                y = jnp.dot(pooled, pw_ref[...], preferred_element_type=F32) * ps_ref[...]
                for i in range(_POOL_GROUP):
                    n2 = base + i
                    o_ref[n2] = (y[i * TILE_ROWS:(i + 1) * TILE_ROWS] * _silu(z_ref[n2].astype(F32))).astype(BF16)
                nxt = base + _POOL_GROUP
                return window_sum + tile(nxt + hi) - tile(nxt - 1 - lo)

            first = tile(-lo)
            for d in range(-lo + 1, hi + 1):
                first = first + tile(d)
            carry = run_group(0, first, True)
            carry = lax.fori_loop(1, n_groups - 1,
                                  lambda c, ws: run_group(c * _POOL_GROUP, ws, False), carry)
            run_group(N_TILES - _POOL_GROUP, carry, True)


def _pool(p5, pool_w_bf, pool_scale):
    u0 = 4 * N_CBLK
    z0 = 5 * N_CBLK
    return pl.pallas_call(
        _pool_kernel,
        grid=(BATCH, len(POOL_WINDOWS)),
        in_specs=[
            pl.BlockSpec((None, None, N_TILES, TILE_ROWS, LANES), lambda b, g: (b, u0 + g, 0, 0, 0)),
            pl.BlockSpec((None, None, N_TILES, TILE_ROWS, LANES), lambda b, g: (b, z0 + g, 0, 0, 0)),
            pl.BlockSpec((None, LANES, LANES), lambda b, g: (g, 0, 0)),
            pl.BlockSpec((1, LANES), lambda b, g: (0, g)),
        ],
        out_specs=pl.BlockSpec((None, None, N_TILES, TILE_ROWS, LANES), lambda b, g: (b, g, 0, 0, 0)),
        out_shape=jax.ShapeDtypeStruct((BATCH, N_CBLK, N_TILES, TILE_ROWS, LANES), BF16),
        compiler_params=pltpu.CompilerParams(
            dimension_semantics=("arbitrary", "arbitrary"), vmem_limit_bytes=VMEM_LIMIT),
        name="pool",
    )(p5, p5, pool_w_bf, pool_scale)


def _out_kernel(yh_ref, yp_ref, x_ref, gh_ref, gp_ref, w_ref, gpost_ref, o_ref, r_scr):
    def group_norm(ref, gain_ref, j0):
        rows = []
        for j in range(j0, j0 + SUB_TILES):
            rows.append(jnp.concatenate([ref[cb, j] for cb in range(N_CBLK)], axis=-1).astype(F32))
        y = jnp.concatenate(rows, axis=0)
        ms = jnp.mean(y * y, axis=-1, keepdims=True)
        return (y * lax.rsqrt(ms + EPS) * gain_ref[...]).astype(BF16)

    for sb in range(IN_TILES // SUB_TILES):
        j0 = sb * SUB_TILES
        yc = jnp.concatenate([group_norm(yh_ref, gh_ref, j0), group_norm(yp_ref, gp_ref, j0)], axis=-1)
        out = jnp.dot(yc, w_ref[...], preferred_element_type=F32)
        ms = jnp.mean(out * out, axis=-1, keepdims=True)
        out = out * lax.rsqrt(ms + EPS) * gpost_ref[...]
        for j in range(SUB_TILES):
            for k in range(D_MODEL // LANES):
                r_scr[sb, k, pl.ds(j, TILE_ROWS, stride=SUB_TILES), :] = out[j * TILE_ROWS:(j + 1) * TILE_ROWS,
                                                                             k * LANES:(k + 1) * LANES]
        r = jnp.concatenate([r_scr[sb, k] for k in range(D_MODEL // LANES)], axis=-1)
        o_ref[:, j0:j0 + SUB_TILES, :] = (x_ref[:, j0:j0 + SUB_TILES, :]
                                          + r.reshape(TILE_ROWS, SUB_TILES, D_MODEL))


def _out_proj(yh, yp, x4, norm_h_g, norm_p_g, w_out_bf, post_g):
    grid = (BATCH, N_TILES // IN_TILES)
    y_spec = pl.BlockSpec((None, N_CBLK, IN_TILES, TILE_ROWS, LANES), lambda b, g: (b, 0, g, 0, 0))
    x_spec = pl.BlockSpec((None, TILE_ROWS, IN_TILES, D_MODEL), lambda b, g: (b, 0, g, 0))
    return pl.pallas_call(
        _out_kernel,
        grid=grid,
        in_specs=[
            y_spec, y_spec, x_spec,
            pl.BlockSpec((1, HYENA_WIDTH), lambda b, g: (0, 0)),
            pl.BlockSpec((1, POOL_WIDTH), lambda b, g: (0, 0)),
            pl.BlockSpec((D_MODEL, D_MODEL), lambda b, g: (0, 0)),
            pl.BlockSpec((1, D_MODEL), lambda b, g: (0, 0)),
        ],
        out_specs=x_spec,
        out_shape=jax.ShapeDtypeStruct((BATCH, TILE_ROWS, N_TILES, D_MODEL), F32),
        scratch_shapes=[pltpu.VMEM((IN_TILES // SUB_TILES, D_MODEL // LANES, TILE_ROWS * SUB_TILES, LANES), F32)],
        compiler_params=pltpu.CompilerParams(
            dimension_semantics=("arbitrary", "arbitrary"), vmem_limit_bytes=VMEM_LIMIT),
        name="out_proj",
    )(yh, yp, x4, norm_h_g, norm_p_g, w_out_bf, post_g)


def kernel(x, pre_norm_g, w_in, conv_w, conv_b, filt_w1, filt_b1, filt_w2, filt_b2, filt_w3, filt_b3,
           filt_freq, filt_w_out, hyena_d, pool_w, pool_scale, norm_h_g, norm_p_g, w_out, post_norm_g):
    assert x.shape == (BATCH, SEQ, D_MODEL) and pre_norm_g.shape[0] == 1
    f1, f1i, gf, gb, f1f = (jnp.asarray(m, F32).astype(BF16) for m in (_F1, _F1I, _GF, _GB, _F1F))

    x4 = x.reshape(BATCH, TILE_ROWS, N_TILES, D_MODEL)
    p5 = _in_proj(x4, pre_norm_g, w_in[0].astype(BF16))

    taps = _filter_mlp(jnp.asarray(_ZFEAT), filt_w1[0], filt_b1[0], filt_w2[0], filt_b2[0], filt_w3[0], filt_b3[0],
                       filt_freq[0], filt_w_out[0], jnp.asarray(_ABS_DELTAS))
    kspec = _filter_spec(taps, f1f, gf)

    yh = _hyena(p5, conv_w[0], conv_b, hyena_d[0], kspec, f1, f1i, gf, gb)
    yp = _pool(p5, pool_w[0].astype(BF16), pool_scale)
    out4 = _out_proj(yh, yp, x4, norm_h_g, norm_p_g, w_out[0].astype(BF16), post_norm_g)
    return out4.reshape(BATCH, SEQ, D_MODEL)
```

```python
import functools
import math

import numpy as np
import jax
import jax.numpy as jnp
from jax import lax
from jax.experimental import pallas as pl
from jax.experimental.pallas import tpu as pltpu

F32 = jnp.float32
BF16 = jnp.bfloat16

D_MODEL = 1024
BATCH = 4
SEQ = 8192
HYENA_WIDTH = 512
POOL_WIDTH = 512
POOL_WINDOWS = (2, 4, 8, 16)
FILTER_EMB = 33
FILTER_BANDS = 16
FILTER_HIDDEN = 64
PROJ_WIDTH = 3072
EPS = 1e-6

LANES = 128
N_FFT = 2 * SEQ
N_TILES = 128
TILE_ROWS = SEQ // N_TILES
N_SLOTS = 64
CHUNK = 32
N_CHUNKS = N_TILES // CHUNK
A_STRIDE = 136
SLOT_BATCH = 4
N_CBLK = HYENA_WIDTH // LANES
PROJ_BLOCKS = PROJ_WIDTH // LANES
IN_TILES = 16
SUB_TILES = 8
VMEM_LIMIT = 60 * 1024 * 1024


def _dft_tables():
    n1 = np.arange(TILE_ROWS)
    n2 = np.arange(N_TILES)
    s = np.arange(N_SLOTS)
    ph = 2 * np.pi * (n2[:, None, None] * s[None, :, None] / N_FFT
                      + n1[None, None, :] * s[None, :, None] / 128.0)
    f1 = np.zeros((N_TILES, 128, TILE_ROWS))
    f1[:, :64, :] = np.cos(ph)
    f1[:, 64:, :] = -np.sin(ph)
    f1[:, 0, :] = 1.0
    f1[:, 64, :] = (-1.0) ** n1
    php = np.transpose(ph, (0, 2, 1))
    f1i = np.zeros((N_TILES, TILE_ROWS, 128))
    f1i[:, :, :64] = 2 * np.cos(php) / N_FFT
    f1i[:, :, 64:] = -2 * np.sin(php) / N_FFT
    f1i[:, :, 0] = 1.0 / N_FFT
    f1i[:, :, 64] = ((-1.0) ** n1)[None, :] / N_FFT
    k2 = np.arange(128)
    th = 2 * np.pi * np.outer(k2, n2) / 128.0
    c, sn = np.cos(th), np.sin(th)
    g = np.block([[c, sn], [-sn, c]])
    gi = np.block([[c, -sn], [sn, c]])
    kk = np.arange(64)
    tha = 2 * np.pi * np.outer(kk, n2) / 128.0
    thb = 2 * np.pi * np.outer(64 + 128 * kk, n2) / N_FFT
    g0 = np.zeros((256, 256))
    g0[0:64, 0:128] = np.cos(tha)
    g0[64:128, 128:256] = np.cos(thb)
    g0[128:192, 0:128] = -np.sin(tha)
    g0[128, 0:128] = (-1.0) ** n2
    g0[192:256, 128:256] = -np.sin(thb)
    g0i = np.zeros((256, 256))
    g0i[0:128, 0:64] = 2 * np.cos(tha.T)
    g0i[0:128, 0] = 1.0
    g0i[0:128, 128:192] = -2 * np.sin(tha.T)
    g0i[0:128, 128] = (-1.0) ** n2
    g0i[128:256, 64:128] = 2 * np.cos(thb.T)
    g0i[128:256, 192:256] = -2 * np.sin(thb.T)
    q = np.arange(256)
    perm = np.where(q % 16 < 8, 8 * (q // 16) + q % 16, 128 + 8 * (q // 16) + q % 16 - 8)
    gf = np.stack([g0[perm, :], g[perm, :]])
    gb = np.stack([g0i[:, perm], gi[:, perm]])
    n1f = np.arange(128)
    phf = 2 * np.pi * (n2[:, None, None] * s[None, :, None] / N_FFT
                       + n1f[None, None, :] * s[None, :, None] / 128.0)
    full = np.zeros((N_TILES, 128, 128))
    full[:, :64, :] = np.cos(phf)
    full[:, 64:, :] = -np.sin(phf)
    full[:, 0, :] = 1.0
    full[:, 64, :] = (-1.0) ** n1f
    rev = full[:, :, 127:63:-1].copy()
    rev[0, :, 1:] = full[0, :, 127:64:-1]
    rev[0, :, 0] = 0.0
    f1f = np.concatenate([full[:, :, :64], rev], axis=2)
    return f1, f1i, gf, gb, f1f


def _filter_features():
    pos = np.arange(SEQ, dtype=np.float64)
    t = pos / (SEQ - 1)
    ang = 2.0 * math.pi * pos / SEQ
    bands = np.linspace(1e-4, FILTER_BANDS - 1, FILTER_BANDS)
    z = np.concatenate([t[:, None], np.cos(bands[None, :] * ang[:, None]),
                        -np.sin(bands[None, :] * ang[:, None])], axis=-1)
    z = z.reshape(TILE_ROWS, N_TILES, FILTER_EMB).transpose(1, 0, 2).reshape(SEQ, FILTER_EMB)
    zp = np.zeros((FILTER_HIDDEN, SEQ))
    zp[:FILTER_EMB, :] = z.T
    max_decay = math.log(1e-2) / 0.3
    min_decay = math.log(1e-2) / 1.5
    deltas = np.abs(np.linspace(min_decay, max_decay, HYENA_WIDTH))
    return zp.astype(np.float32), deltas.astype(np.float32)[None, :]


_F1, _F1I, _GF, _GB, _F1F = _dft_tables()
_ZFEAT, _ABS_DELTAS = _filter_features()


def _shift_down(x):
    rows = lax.broadcasted_iota(jnp.int32, x.shape, 0)
    return jnp.where(rows == 0, 0.0, pltpu.roll(x, 1, axis=0))


def _shift_up(x):
    rows = lax.broadcasted_iota(jnp.int32, x.shape, 0)
    return jnp.where(rows == x.shape[0] - 1, 0.0, pltpu.roll(x, x.shape[0] - 1, axis=0))


def _pair(ref, i):
    return jnp.concatenate([ref[0, i], ref[1, i]], axis=-1).astype(F32)


def _dup(x):
    return jnp.concatenate([x, x], axis=-1)


def _silu(z):
    hz = 0.5 * z
    return hz * (1.0 + jnp.tanh(hz))


def _a_rows(n2):
    return pl.ds(pl.multiple_of(n2 * A_STRIDE, 8), 128)


def _slot_rows(r):
    return pl.ds(r, N_TILES, stride=A_STRIDE)


def _stage_load(a_ref, rows):
    return jnp.concatenate([a_ref[0, rows, :], a_ref[1, rows, :]], axis=-1)


def _stage_store(a_ref, rows, val):
    a_ref[0, rows, :] = val[:, :LANES]
    a_ref[1, rows, :] = val[:, LANES:]


def _short_conv_chunk(chunk_ref, prev_ref, next_ref, tc, w_ref, b_ref):
    first = _pair(prev_ref, 0)
    first = jnp.where(tc == 0, _shift_down(first), first)
    last = _pair(next_ref, 0)
    last = jnp.where(tc == N_CHUNKS - 1, _shift_up(last), last)
    tiles = [first] + [_pair(chunk_ref, i) for i in range(CHUNK)] + [last]
    w = w_ref[...]
    w0, w1, w2, b = _dup(w[0:1]), _dup(w[1:2]), _dup(w[2:3]), _dup(b_ref[...])
    return [tiles[i] * w0 + tiles[i + 1] * w1 + tiles[i + 2] * w2 + b for i in range(CHUNK)]


N_BATCHES = N_SLOTS // SLOT_BATCH
SPEC_BLOCKS = 16


def _batch_slots(i):
    return [i * SLOT_BATCH + j for j in range(SLOT_BATCH)]


def _stage2_forward(a_ref, gf_ref, slots, sel0):
    xs = []
    for s in slots:
        xr = _stage_load(a_ref, _slot_rows(s))
        xi = _stage_load(a_ref, _slot_rows(N_SLOTS + s))
        xs.append(jnp.concatenate([xr, xi], axis=0).astype(BF16))
    return [jnp.dot(gf_ref[sel0] if j == 0 else gf_ref[1], x, preferred_element_type=F32)
            for j, x in enumerate(xs)]


def _re_im_blocks(y):
    return [(y[16 * i:16 * i + 8], y[16 * i + 8:16 * i + 16]) for i in range(SPEC_BLOCKS)]


def _in_proj_kernel(x_ref, g_ref, w_ref, o_ref, h_scr):
    ncol = 512
    for sb in range(IN_TILES // SUB_TILES):
        j0 = sb * SUB_TILES
        x = x_ref[:, j0:j0 + SUB_TILES, :].reshape(TILE_ROWS * SUB_TILES, D_MODEL)
        ms = jnp.mean(x * x, axis=-1, keepdims=True)
        hn = x * lax.rsqrt(ms + EPS) * g_ref[...]
        for k in range(D_MODEL // LANES):
            h_scr[sb, k] = hn[:, k * LANES:(k + 1) * LANES]
        h = jnp.concatenate(
            [jnp.concatenate([h_scr[sb, k, pl.ds(j, TILE_ROWS, stride=SUB_TILES), :]
                              for k in range(D_MODEL // LANES)], axis=-1).astype(BF16)
             for j in range(SUB_TILES)], axis=0)
        for c in range(PROJ_WIDTH // ncol):
            p = jnp.dot(h, w_ref[:, c * ncol:(c + 1) * ncol], preferred_element_type=F32).astype(BF16)
            for cb in range(ncol // LANES):
                for j in range(SUB_TILES):
                    o_ref[c * (ncol // LANES) + cb, j0 + j] = p[j * TILE_ROWS:(j + 1) * TILE_ROWS,
                                                                cb * LANES:(cb + 1) * LANES]


def _in_proj(x4, pre_g, w_in_bf):
    grid = (BATCH, N_TILES // IN_TILES)
    return pl.pallas_call(
        _in_proj_kernel,
        grid=grid,
        in_specs=[
            pl.BlockSpec((None, TILE_ROWS, IN_TILES, D_MODEL), lambda b, g: (b, 0, g, 0)),
            pl.BlockSpec((1, D_MODEL), lambda b, g: (0, 0)),
            pl.BlockSpec((D_MODEL, PROJ_WIDTH), lambda b, g: (0, 0)),
        ],
        out_specs=pl.BlockSpec((None, PROJ_BLOCKS, IN_TILES, TILE_ROWS, LANES), lambda b, g: (b, 0, g, 0, 0)),
        out_shape=jax.ShapeDtypeStruct((BATCH, PROJ_BLOCKS, N_TILES, TILE_ROWS, LANES), BF16),
        scratch_shapes=[pltpu.VMEM((IN_TILES // SUB_TILES, D_MODEL // LANES, TILE_ROWS * SUB_TILES, LANES), F32)],
        compiler_params=pltpu.CompilerParams(
            dimension_semantics=("arbitrary", "arbitrary"), vmem_limit_bytes=VMEM_LIMIT),
        name="in_proj",
    )(x4, pre_g, w_in_bf)


def _split_bf16(x):
    hi = x.astype(BF16)
    return hi, (x - hi.astype(F32)).astype(BF16)


def _stack_weight_rows(w):
    hi, lo = _split_bf16(w)
    return jnp.concatenate([hi, lo, hi], axis=1)


def _filter_mlp_kernel(z_ref, w1_ref, b1_ref, w2_ref, b2_ref, w3_ref, b3_ref, fr_ref, fr3_ref,
                       wp2_ref, wpl_ref, dl_ref, o_ref):
    def layer(w_ref, b_ref, f_ref, h):
        hi, lo = _split_bf16(h)
        pre = jnp.dot(w_ref[...], jnp.concatenate([hi, hi, lo], axis=0), preferred_element_type=F32)
        return jnp.sin(f_ref[...] * (pre + b_ref[...]))

    h = layer(w1_ref, b1_ref, fr_ref, z_ref[...])
    h = layer(w2_ref, b2_ref, fr_ref, h)
    h = layer(w3_ref, b3_ref, fr3_ref, h)
    hi, lo = _split_bf16(h.T)
    hs = jnp.concatenate([hi, lo], axis=1)
    rows = CHUNK * TILE_ROWS
    r = lax.broadcasted_iota(jnp.int32, (rows, HYENA_WIDTH), 0)
    pos = 128 * (r % TILE_ROWS) + CHUNK * pl.program_id(0) + r // TILE_ROWS
    t = pos.astype(F32) / float(SEQ - 1)
    decay = jnp.exp(-t * dl_ref[...])
    for od in range(4):
        cols = slice(od * HYENA_WIDTH, (od + 1) * HYENA_WIDTH)
        taps = (jnp.dot(hs, wp2_ref[:, cols], preferred_element_type=F32)
                + jnp.dot(hi, wpl_ref[:, cols], preferred_element_type=F32)) * decay
        taps = taps.astype(BF16)
        for cb in range(N_CBLK):
            for i in range(CHUNK):
                o_ref[od * N_CBLK + cb, i] = taps[i * TILE_ROWS:(i + 1) * TILE_ROWS, cb * LANES:(cb + 1) * LANES]


def _filter_mlp(zfeat_t, w1, b1, w2, b2, w3, b3, freq, w_proj, abs_deltas):
    hid = FILTER_HIDDEN
    col = lambda v, n: jnp.pad(v.reshape(-1, 1), ((0, n - v.size), (0, 0)))
    w1s = _stack_weight_rows(jnp.pad(w1, ((0, hid - FILTER_EMB), (0, 0))).T)
    w2s = _stack_weight_rows(w2.T)
    w3s = _stack_weight_rows(jnp.pad(w3.T, ((0, LANES - hid), (0, 0))))
    wp_hi, wp_lo = _split_bf16(jnp.pad(w_proj, ((0, LANES - hid), (0, 0))))
    wp2 = jnp.concatenate([wp_hi, wp_hi], axis=0)
    full = lambda shape: pl.BlockSpec(shape, lambda c: (0,) * len(shape))
    cols = CHUNK * TILE_ROWS
    return pl.pallas_call(
        _filter_mlp_kernel,
        grid=(N_CHUNKS,),
        in_specs=[
            pl.BlockSpec((hid, cols), lambda c: (0, c)),
            full((hid, 3 * hid)), full((hid, 1)),
            full((hid, 3 * hid)), full((hid, 1)),
            full((LANES, 3 * hid)), full((LANES, 1)),
            full((hid, 1)), full((LANES, 1)),
            full((2 * LANES, 4 * HYENA_WIDTH)), full((LANES, 4 * HYENA_WIDTH)),
            full((1, HYENA_WIDTH)),
        ],
        out_specs=pl.BlockSpec((4 * N_CBLK, CHUNK, TILE_ROWS, LANES), lambda c: (0, c, 0, 0)),
        out_shape=jax.ShapeDtypeStruct((4 * N_CBLK, N_TILES, TILE_ROWS, LANES), BF16),
        compiler_params=pltpu.CompilerParams(dimension_semantics=("arbitrary",), vmem_limit_bytes=VMEM_LIMIT),
        name="filter_mlp",
    )(zfeat_t, w1s, col(b1, hid), w2s, col(b2, hid), w3s, col(b3, LANES), col(freq, hid), col(freq, LANES),
      wp2, wp_lo, abs_deltas)


SPEC_SLOTS_PER_STEP = 16
N_SPEC_STEPS = N_SLOTS // SPEC_SLOTS_PER_STEP


def _filter_spec_kernel(fa_ref, fb_ref, ba_ref, bb_ref, bxa_ref, bxb_ref, d_ref, f1f_ref, gf_ref, k_ref, a_scr):
    t = pl.program_id(2)

    @pl.when(t < N_CHUNKS)
    def _():
        for i in range(CHUNK):
            n2 = t * CHUNK + i
            fwd = jnp.concatenate([fa_ref[i], fb_ref[i]], axis=-1)
            if i == 0:
                bwd = jnp.concatenate([bxa_ref[0], bxb_ref[0]], axis=-1)
            else:
                bwd = jnp.concatenate([ba_ref[CHUNK - i], bb_ref[CHUNK - i]], axis=-1)
            taps = jnp.concatenate([fwd, bwd], axis=0)
            _stage_store(a_scr, _a_rows(n2), jnp.dot(f1f_ref[n2], taps, preferred_element_type=F32))

    @pl.when(t >= N_CHUNKS)
    def _():
        q = t - N_CHUNKS
        for bi in range(SPEC_SLOTS_PER_STEP // SLOT_BATCH):
            base = q * SPEC_SLOTS_PER_STEP + bi * SLOT_BATCH
            sel0 = jnp.where(q == 0, 0, 1) if bi == 0 else 1
            ys = _stage2_forward(a_scr, gf_ref, [base + j for j in range(SLOT_BATCH)], sel0)
            d = d_ref[...]
            for j, y in enumerate(ys):
                blocks = []
                for i, (yr, yi) in enumerate(_re_im_blocks(y)):
                    yr = yr + d
                    if bi == 0 and j == 0 and i == 0:
                        rows = lax.broadcasted_iota(jnp.int32, yi.shape, 0)
                        yi = yi + jnp.where((rows == 0) & (q == 0), d, 0.0)
                    blocks += [yr, yi]
                spec = jnp.concatenate(blocks, axis=0).astype(BF16)
                k_ref[0, bi * SLOT_BATCH + j] = spec[:, :LANES]
                k_ref[1, bi * SLOT_BATCH + j] = spec[:, LANES:]


def _filter_spec(taps, hyena_d3, f1f, gf):
    grid = (2, N_CBLK // 2, N_CHUNKS + N_SPEC_STEPS)
    fchunk = lambda t: jnp.minimum(t, N_CHUNKS - 1)
    const = lambda shape: pl.BlockSpec(shape, lambda o, c, t: (0,) * len(shape), pipeline_mode=pl.Buffered(1))

    def fwd_spec(k):
        return pl.BlockSpec((None, CHUNK, TILE_ROWS, LANES),
                            lambda o, c, t: ((2 * o) * N_CBLK + 2 * c + k, fchunk(t), 0, 0))

    def bwd_spec(k):
        return pl.BlockSpec((None, CHUNK, TILE_ROWS, LANES),
                            lambda o, c, t: ((2 * o + 1) * N_CBLK + 2 * c + k, N_CHUNKS - 1 - fchunk(t), 0, 0))

    def bwd_tile_spec(k):
        return pl.BlockSpec((None, 1, TILE_ROWS, LANES),
                            lambda o, c, t: ((2 * o + 1) * N_CBLK + 2 * c + k, (N_TILES - CHUNK * fchunk(t)) % N_TILES, 0, 0))

    return pl.pallas_call(
        _filter_spec_kernel,
        grid=grid,
        in_specs=[fwd_spec(0), fwd_spec(1), bwd_spec(0), bwd_spec(1), bwd_tile_spec(0), bwd_tile_spec(1),
                  pl.BlockSpec((None, 1, 2 * LANES), lambda o, c, t: (o, 0, c)),
                  const((N_TILES, 128, 128)), const((2, 256, 256))],
        out_specs=pl.BlockSpec((None, 2, SPEC_SLOTS_PER_STEP, 256, LANES),
                               lambda o, c, t: (o, c, jnp.maximum(t - N_CHUNKS, 0), 0, 0)),
        out_shape=jax.ShapeDtypeStruct((2, N_CBLK, N_SLOTS, 256, LANES), BF16),
        scratch_shapes=[pltpu.VMEM((2, N_TILES * A_STRIDE, LANES), F32)],
        compiler_params=pltpu.CompilerParams(
            dimension_semantics=("arbitrary", "arbitrary", "arbitrary"), vmem_limit_bytes=VMEM_LIMIT),
        name="filter_spec",
    )(taps, taps, taps, taps, taps, taps, hyena_d3, f1f, gf)


_T_S1 = 0
_T_F0 = N_CHUNKS
_T_M = N_CHUNKS + 1
_T_F1 = 2 * N_CHUNKS + 1
_T_E = 2 * N_CHUNKS + 2
_T_END = 3 * N_CHUNKS + 2


def _hyena_kernel(pv_ref, pvp_ref, pvn_ref, px1_ref, px1p_ref, px1n_ref, px2_ref, px2p_ref, px2n_ref, pz_ref,
                  cw_ref, k_ref, f1_ref, f1i_ref, gf_ref, gb_ref, o_ref, a_scr, z_scr):
    t = pl.program_id(2)
    wv_ref, bv_ref = cw_ref.at[0:3], cw_ref.at[3:4]
    w1_ref, b1_ref = cw_ref.at[4:7], cw_ref.at[7:8]
    w2_ref, b2_ref = cw_ref.at[8:11], cw_ref.at[11:12]

    def stage1(n2, u_bf):
        _stage_store(a_scr, _a_rows(n2), jnp.dot(f1_ref[n2], u_bf, preferred_element_type=F32))

    def inv_stage1(n2):
        return jnp.dot(f1i_ref[n2], _stage_load(a_scr, _a_rows(n2)).astype(BF16), preferred_element_type=F32)

    def filter_multiply(slots, ys, first, buf):
        for j, (s, y) in enumerate(zip(slots, ys)):
            blocks = []
            for i, (yr, yi) in enumerate(_re_im_blocks(y)):
                kblk = k_ref[s, 16 * i:16 * i + 16, :].astype(F32)
                kr, ki = kblk[:8], kblk[8:]
                if first and j == 0 and i == 0:
                    rows = lax.broadcasted_iota(jnp.int32, kr.shape, 0)
                    ka, kb, kd = kr, jnp.where(rows == 0, 0.0, ki), jnp.where(rows == 0, ki, kr)
                else:
                    ka, kb, kd = kr, ki, kr
                ka, kb, kd = _dup(ka), _dup(kb), _dup(kd)
                blocks += [yr * ka - yi * kb, yr * kb + yi * kd]
            z_scr[buf, j] = jnp.concatenate(blocks, axis=0).astype(BF16)

    def stage2_inverse(slots, first_sel, buf):
        bms = [jnp.dot(gb_ref[first_sel] if j == 0 else gb_ref[1], z_scr[buf, j], preferred_element_type=F32)
               for j in range(SLOT_BATCH)]
        for s, bm in zip(slots, bms):
            _stage_store(a_scr, _slot_rows(s), bm[:128])
            _stage_store(a_scr, _slot_rows(N_SLOTS + s), bm[128:])

    def spectral_phase():
        slots0 = _batch_slots(0)
        filter_multiply(slots0, _stage2_forward(a_scr, gf_ref, slots0, 0), True, 0)

        def loop(i, carry):
            slots = _batch_slots(i)
            ys = _stage2_forward(a_scr, gf_ref, slots, 1)
            stage2_inverse(_batch_slots(i - 1), jnp.where(i == 1, 0, 1), (i - 1) % 2)
            filter_multiply(slots, ys, False, i % 2)
            return carry

        lax.fori_loop(1, N_BATCHES, loop, 0)
        stage2_inverse(_batch_slots(N_BATCHES - 1), 1, (N_BATCHES - 1) % 2)

    @pl.when(t < _T_F0)
    def _():
        vs = _short_conv_chunk(pv_ref, pvp_ref, pvn_ref, t, wv_ref, bv_ref)
        for i in range(CHUNK):
            n2 = t * CHUNK + i
            stage1(n2, vs[i].astype(BF16))

    @pl.when((t == _T_F0) | (t == _T_F1))
    def _():
        spectral_phase()

    @pl.when((t >= _T_M) & (t < _T_F1))
    def _():
        tc = t - _T_M
        ys = [inv_stage1(tc * CHUNK + i) for i in range(CHUNK)]
        gates = _short_conv_chunk(px1_ref, px1p_ref, px1n_ref, tc, w1_ref, b1_ref)
        us = [(gates[i] * ys[i]).astype(BF16) for i in range(CHUNK)]
        for i in range(CHUNK):
            stage1(tc * CHUNK + i, us[i])

    @pl.when(t >= _T_E)
    def _():
        tc = t - _T_E
        ys = [inv_stage1(tc * CHUNK + i) for i in range(CHUNK)]
        gates = _short_conv_chunk(px2_ref, px2p_ref, px2n_ref, tc, w2_ref, b2_ref)
        for i in range(CHUNK):
            res = (gates[i] * ys[i] * _silu(_pair(pz_ref, i))).astype(BF16)
            o_ref[0, i] = res[:, :LANES]
            o_ref[1, i] = res[:, LANES:]


def _hyena(p5, conv_w, conv_b, kspec, f1, f1i, gf, gb):
    grid = (N_CBLK, BATCH // 2, _T_END)
    cw = jnp.concatenate(
        [jnp.concatenate([conv_w[:, k * HYENA_WIDTH:(k + 1) * HYENA_WIDTH],
                          conv_b[:, k * HYENA_WIDTH:(k + 1) * HYENA_WIDTH]], axis=0) for k in range(3)], axis=0)

    def clampc(t, start):
        return jnp.clip(t - start, 0, N_CHUNKS - 1)

    def chunk_spec(col0, start):
        return pl.BlockSpec((2, None, CHUNK, TILE_ROWS, LANES),
                            lambda c, b, t: (b, col0 + c, clampc(t, start), 0, 0))

    def prev_spec(col0, start):
        return pl.BlockSpec((2, None, 1, TILE_ROWS, LANES),
                            lambda c, b, t: (b, col0 + c, (clampc(t, start) * CHUNK + N_TILES - 1) % N_TILES, 0, 0))

    def next_spec(col0, start):
        return pl.BlockSpec((2, None, 1, TILE_ROWS, LANES),
                            lambda c, b, t: (b, col0 + c, (clampc(t, start) * CHUNK + CHUNK) % N_TILES, 0, 0))

    const = lambda shape: pl.BlockSpec(shape, lambda c, b, t: (0,) * len(shape), pipeline_mode=pl.Buffered(1))
    in_specs = [
        chunk_spec(0, _T_S1), prev_spec(0, _T_S1), next_spec(0, _T_S1),
        chunk_spec(N_CBLK, _T_M), prev_spec(N_CBLK, _T_M), next_spec(N_CBLK, _T_M),
        chunk_spec(2 * N_CBLK, _T_E), prev_spec(2 * N_CBLK, _T_E), next_spec(2 * N_CBLK, _T_E),
        chunk_spec(3 * N_CBLK, _T_E),
        pl.BlockSpec((12, LANES), lambda c, b, t: (0, c)),
        pl.BlockSpec((None, None, N_SLOTS, 256, LANES), lambda c, b, t: (jnp.where(t >= _T_M, 1, 0), c, 0, 0, 0)),
        const((N_TILES, 128, TILE_ROWS)), const((N_TILES, TILE_ROWS, 128)),
        const((2, 256, 256)), const((2, 256, 256)),
    ]
    return pl.pallas_call(
        _hyena_kernel,
        grid=grid,
        in_specs=in_specs,
        out_specs=pl.BlockSpec((2, None, CHUNK, TILE_ROWS, LANES), lambda c, b, t: (b, c, clampc(t, _T_E), 0, 0)),
        out_shape=jax.ShapeDtypeStruct((BATCH, N_CBLK, N_TILES, TILE_ROWS, LANES), BF16),
        scratch_shapes=[pltpu.VMEM((2, N_TILES * A_STRIDE, LANES), F32),
                        pltpu.VMEM((2, SLOT_BATCH, 256, 2 * LANES), BF16)],
        compiler_params=pltpu.CompilerParams(
            dimension_semantics=("arbitrary", "arbitrary", "arbitrary"), vmem_limit_bytes=VMEM_LIMIT),
        name="hyena",
    )(p5, p5, p5, p5, p5, p5, p5, p5, p5, p5, cw, kspec, f1, f1i, gf, gb)


_POOL_GROUP = 8


def _pool_kernel(u_ref, z_ref, pw_ref, ps_ref, o_ref):
    g = pl.program_id(1)
    n_groups = N_TILES // _POOL_GROUP
    n1 = lax.broadcasted_iota(jnp.int32, (TILE_ROWS, LANES), 0)

    def tile(idx):
        if isinstance(idx, int) and idx < 0:
            return _shift_down(u_ref[idx + N_TILES].astype(F32))
        if isinstance(idx, int) and idx >= N_TILES:
            return _shift_up(u_ref[idx - N_TILES].astype(F32))
        return u_ref[idx].astype(F32)

    for gi_, w in enumerate(POOL_WINDOWS):
        @pl.when(g == gi_)
        def _(w=w):
            lo, hi = w // 2, w - 1 - w // 2

            def run_group(base, window_sum, edge):
                pooled = []
                for i in range(_POOL_GROUP):
                    n2 = base + i
                    if i > 0:
                        window_sum = window_sum + tile(n2 + hi) - tile(n2 - 1 - lo)
                    if edge:
                        pos = 128 * n1 + n2
                        cnt = (jnp.minimum(pos + hi, SEQ - 1) - jnp.maximum(pos - lo, 0) + 1).astype(F32)
                        mean = window_sum / cnt
                    else:
                        mean = window_sum * (1.0 / w)
                    pooled.append((mean - tile(n2)).astype(BF16))
                pooled = jnp.concatenate(pooled, axis=0)
                y = jnp.dot(pooled, pw_ref[...], preferred_element_type=F32) * ps_ref[...]
                for i in range(_POOL_GROUP):
                    n2 = base + i
                    o_ref[n2] = (y[i * TILE_ROWS:(i + 1) * TILE_ROWS] * _silu(z_ref[n2].astype(F32))).astype(BF16)
                nxt = base + _POOL_GROUP
                return window_sum + tile(nxt + hi) - tile(nxt - 1 - lo)

            first = tile(-lo)
            for d in range(-lo + 1, hi + 1):
                first = first + tile(d)
            carry = run_group(0, first, True)
            carry = lax.fori_loop(1, n_groups - 1,
                                  lambda c, ws: run_group(c * _POOL_GROUP, ws, False), carry)
            run_group(N_TILES - _POOL_GROUP, carry, True)


def _pool(p5, pool_w_bf, pool_scale):
    u0 = 4 * N_CBLK
    z0 = 5 * N_CBLK
    return pl.pallas_call(
        _pool_kernel,
        grid=(BATCH, len(POOL_WINDOWS)),
        in_specs=[
            pl.BlockSpec((None, None, N_TILES, TILE_ROWS, LANES), lambda b, g: (b, u0 + g, 0, 0, 0)),
            pl.BlockSpec((None, None, N_TILES, TILE_ROWS, LANES), lambda b, g: (b, z0 + g, 0, 0, 0)),
            pl.BlockSpec((None, LANES, LANES), lambda b, g: (g, 0, 0)),
            pl.BlockSpec((1, LANES), lambda b, g: (0, g)),
        ],
        out_specs=pl.BlockSpec((None, None, N_TILES, TILE_ROWS, LANES), lambda b, g: (b, g, 0, 0, 0)),
        out_shape=jax.ShapeDtypeStruct((BATCH, N_CBLK, N_TILES, TILE_ROWS, LANES), BF16),
        compiler_params=pltpu.CompilerParams(
            dimension_semantics=("arbitrary", "arbitrary"), vmem_limit_bytes=VMEM_LIMIT),
        name="pool",
    )(p5, p5, pool_w_bf, pool_scale)


def _out_kernel(yh_ref, yp_ref, x_ref, gh_ref, gp_ref, w_ref, gpost_ref, o_ref, r_scr):
    def group_norm(ref, gain_ref, j0):
        rows = []
        for j in range(j0, j0 + SUB_TILES):
            rows.append(jnp.concatenate([ref[cb, j] for cb in range(N_CBLK)], axis=-1).astype(F32))
        y = jnp.concatenate(rows, axis=0)
        ms = jnp.mean(y * y, axis=-1, keepdims=True)
        return (y * lax.rsqrt(ms + EPS) * gain_ref[...]).astype(BF16)

    for sb in range(IN_TILES // SUB_TILES):
        j0 = sb * SUB_TILES
        yc = jnp.concatenate([group_norm(yh_ref, gh_ref, j0), group_norm(yp_ref, gp_ref, j0)], axis=-1)
        out = jnp.dot(yc, w_ref[...], preferred_element_type=F32)
        ms = jnp.mean(out * out, axis=-1, keepdims=True)
        out = out * lax.rsqrt(ms + EPS) * gpost_ref[...]
        for j in range(SUB_TILES):
            for k in range(D_MODEL // LANES):
                r_scr[sb, k, pl.ds(j, TILE_ROWS, stride=SUB_TILES), :] = out[j * TILE_ROWS:(j + 1) * TILE_ROWS,
                                                                             k * LANES:(k + 1) * LANES]
        r = jnp.concatenate([r_scr[sb, k] for k in range(D_MODEL // LANES)], axis=-1)
        o_ref[:, j0:j0 + SUB_TILES, :] = (x_ref[:, j0:j0 + SUB_TILES, :]
                                          + r.reshape(TILE_ROWS, SUB_TILES, D_MODEL))


def _out_proj(yh, yp, x4, norm_h_g, norm_p_g, w_out_bf, post_g):
    grid = (BATCH, N_TILES // IN_TILES)
    y_spec = pl.BlockSpec((None, N_CBLK, IN_TILES, TILE_ROWS, LANES), lambda b, g: (b, 0, g, 0, 0))
    x_spec = pl.BlockSpec((None, TILE_ROWS, IN_TILES, D_MODEL), lambda b, g: (b, 0, g, 0))
    return pl.pallas_call(
        _out_kernel,
        grid=grid,
        in_specs=[
            y_spec, y_spec, x_spec,
            pl.BlockSpec((1, HYENA_WIDTH), lambda b, g: (0, 0)),
            pl.BlockSpec((1, POOL_WIDTH), lambda b, g: (0, 0)),
            pl.BlockSpec((D_MODEL, D_MODEL), lambda b, g: (0, 0)),
            pl.BlockSpec((1, D_MODEL), lambda b, g: (0, 0)),
        ],
        out_specs=x_spec,
        out_shape=jax.ShapeDtypeStruct((BATCH, TILE_ROWS, N_TILES, D_MODEL), F32),
        scratch_shapes=[pltpu.VMEM((IN_TILES // SUB_TILES, D_MODEL // LANES, TILE_ROWS * SUB_TILES, LANES), F32)],
        compiler_params=pltpu.CompilerParams(
            dimension_semantics=("arbitrary", "arbitrary"), vmem_limit_bytes=VMEM_LIMIT),
        name="out_proj",
    )(yh, yp, x4, norm_h_g, norm_p_g, w_out_bf, post_g)


def kernel(x, pre_norm_g, w_in, conv_w, conv_b, filt_w1, filt_b1, filt_w2, filt_b2, filt_w3, filt_b3,
           filt_freq, filt_w_out, hyena_d, pool_w, pool_scale, norm_h_g, norm_p_g, w_out, post_norm_g):
    assert x.shape == (BATCH, SEQ, D_MODEL) and pre_norm_g.shape[0] == 1
    f1, f1i, gf, gb, f1f = (jnp.asarray(m, F32).astype(BF16) for m in (_F1, _F1I, _GF, _GB, _F1F))

    x4 = x.reshape(BATCH, TILE_ROWS, N_TILES, D_MODEL)
    p5 = _in_proj(x4, pre_norm_g, w_in[0].astype(BF16))

    taps = _filter_mlp(jnp.asarray(_ZFEAT), filt_w1[0], filt_b1[0], filt_w2[0], filt_b2[0], filt_w3[0], filt_b3[0],
                       filt_freq[0], filt_w_out[0], jnp.asarray(_ABS_DELTAS))
    kspec = _filter_spec(taps, hyena_d[0].reshape(2, 1, HYENA_WIDTH), f1f, gf)

    yh = _hyena(p5, conv_w[0], conv_b, kspec, f1, f1i, gf, gb)
    yp = _pool(p5, pool_w[0].astype(BF16), pool_scale)
    out4 = _out_proj(yh, yp, x4, norm_h_g, norm_p_g, w_out[0].astype(BF16), post_norm_g)
    return out4.reshape(BATCH, SEQ, D_MODEL)
```

```python
import functools
import math

import numpy as np
import jax
import jax.numpy as jnp
from jax import lax
from jax.experimental import pallas as pl
from jax.experimental.pallas import tpu as pltpu

F32 = jnp.float32
BF16 = jnp.bfloat16

D_MODEL = 1024
BATCH = 4
SEQ = 8192
HYENA_WIDTH = 512
POOL_WIDTH = 512
POOL_WINDOWS = (2, 4, 8, 16)
FILTER_EMB = 33
FILTER_BANDS = 16
FILTER_HIDDEN = 64
PROJ_WIDTH = 3072
EPS = 1e-6

LANES = 128
N_FFT = 2 * SEQ
N_TILES = 128
TILE_ROWS = SEQ // N_TILES
N_SLOTS = 64
CHUNK = 32
HY_CHUNK = 64
HY_NCHUNKS = 128 // HY_CHUNK
HY_SUB = 32
N_CHUNKS = N_TILES // CHUNK
A_STRIDE = 136
SLOT_BATCH = 4
N_CBLK = HYENA_WIDTH // LANES
PROJ_BLOCKS = PROJ_WIDTH // LANES
IN_TILES = 16
SUB_TILES = 8
VMEM_LIMIT = 60 * 1024 * 1024


def _dft_tables():
    n1 = np.arange(TILE_ROWS)
    n2 = np.arange(N_TILES)
    s = np.arange(N_SLOTS)
    ph = 2 * np.pi * (n2[:, None, None] * s[None, :, None] / N_FFT
                      + n1[None, None, :] * s[None, :, None] / 128.0)
    f1 = np.zeros((N_TILES, 128, TILE_ROWS))
    f1[:, :64, :] = np.cos(ph)
    f1[:, 64:, :] = -np.sin(ph)
    f1[:, 0, :] = 1.0
    f1[:, 64, :] = (-1.0) ** n1
    php = np.transpose(ph, (0, 2, 1))
    f1i = np.zeros((N_TILES, TILE_ROWS, 128))
    f1i[:, :, :64] = 2 * np.cos(php) / N_FFT
    f1i[:, :, 64:] = -2 * np.sin(php) / N_FFT
    f1i[:, :, 0] = 1.0 / N_FFT
    f1i[:, :, 64] = ((-1.0) ** n1)[None, :] / N_FFT
    k2 = np.arange(128)
    th = 2 * np.pi * np.outer(k2, n2) / 128.0
    c, sn = np.cos(th), np.sin(th)
    g = np.block([[c, sn], [-sn, c]])
    gi = np.block([[c, -sn], [sn, c]])
    kk = np.arange(64)
    tha = 2 * np.pi * np.outer(kk, n2) / 128.0
    thb = 2 * np.pi * np.outer(64 + 128 * kk, n2) / N_FFT
    g0 = np.zeros((256, 256))
    g0[0:64, 0:128] = np.cos(tha)
    g0[64:128, 128:256] = np.cos(thb)
    g0[128:192, 0:128] = -np.sin(tha)
    g0[128, 0:128] = (-1.0) ** n2
    g0[192:256, 128:256] = -np.sin(thb)
    g0i = np.zeros((256, 256))
    g0i[0:128, 0:64] = 2 * np.cos(tha.T)
    g0i[0:128, 0] = 1.0
    g0i[0:128, 128:192] = -2 * np.sin(tha.T)
    g0i[0:128, 128] = (-1.0) ** n2
    g0i[128:256, 64:128] = 2 * np.cos(thb.T)
    g0i[128:256, 192:256] = -2 * np.sin(thb.T)
    q = np.arange(256)
    perm = np.where(q % 16 < 8, 8 * (q // 16) + q % 16, 128 + 8 * (q // 16) + q % 16 - 8)
    gf = np.stack([g0[perm, :], g[perm, :]])
    gb = np.stack([g0i[:, perm], gi[:, perm]])
    n1f = np.arange(128)
    phf = 2 * np.pi * (n2[:, None, None] * s[None, :, None] / N_FFT
                       + n1f[None, None, :] * s[None, :, None] / 128.0)
    full = np.zeros((N_TILES, 128, 128))
    full[:, :64, :] = np.cos(phf)
    full[:, 64:, :] = -np.sin(phf)
    full[:, 0, :] = 1.0
    full[:, 64, :] = (-1.0) ** n1f
    rev = full[:, :, 127:63:-1].copy()
    rev[0, :, 1:] = full[0, :, 127:64:-1]
    rev[0, :, 0] = 0.0
    f1f = np.concatenate([full[:, :, :64], rev], axis=2)
    return f1, f1i, gf, gb, f1f


def _filter_features():
    pos = np.arange(SEQ, dtype=np.float64)
    t = pos / (SEQ - 1)
    ang = 2.0 * math.pi * pos / SEQ
    bands = np.linspace(1e-4, FILTER_BANDS - 1, FILTER_BANDS)
    z = np.concatenate([t[:, None], np.cos(bands[None, :] * ang[:, None]),
                        -np.sin(bands[None, :] * ang[:, None])], axis=-1)
    z = z.reshape(TILE_ROWS, N_TILES, FILTER_EMB).transpose(1, 0, 2).reshape(SEQ, FILTER_EMB)
    zp = np.zeros((FILTER_HIDDEN, SEQ))
    zp[:FILTER_EMB, :] = z.T
    max_decay = math.log(1e-2) / 0.3
    min_decay = math.log(1e-2) / 1.5
    deltas = np.abs(np.linspace(min_decay, max_decay, HYENA_WIDTH))
    return zp.astype(np.float32), deltas.astype(np.float32)[None, :]


_F1, _F1I, _GF, _GB, _F1F = _dft_tables()
_ZFEAT, _ABS_DELTAS = _filter_features()


def _shift_down(x):
    rows = lax.broadcasted_iota(jnp.int32, x.shape, 0)
    return jnp.where(rows == 0, 0.0, pltpu.roll(x, 1, axis=0))


def _shift_up(x):
    rows = lax.broadcasted_iota(jnp.int32, x.shape, 0)
    return jnp.where(rows == x.shape[0] - 1, 0.0, pltpu.roll(x, x.shape[0] - 1, axis=0))


def _pair(ref, i):
    return jnp.concatenate([ref[0, i], ref[1, i]], axis=-1).astype(F32)


def _dup(x):
    return jnp.concatenate([x, x], axis=-1)


def _silu(z):
    hz = 0.5 * z
    return hz * (1.0 + jnp.tanh(hz))


def _a_rows(n2):
    return pl.ds(pl.multiple_of(n2 * A_STRIDE, 8), 128)


def _slot_rows(r):
    return pl.ds(r, N_TILES, stride=A_STRIDE)


def _stage_load(a_ref, rows):
    return jnp.concatenate([a_ref[0, rows, :], a_ref[1, rows, :]], axis=-1)


def _stage_store(a_ref, rows, val):
    a_ref[0, rows, :] = val[:, :LANES]
    a_ref[1, rows, :] = val[:, LANES:]


def _short_conv_chunk(chunk_ref, prev_ref, next_ref, tc, w_ref, b_ref):
    first = _pair(prev_ref, 0)
    first = jnp.where(tc == 0, _shift_down(first), first)
    last = _pair(next_ref, 0)
    last = jnp.where(tc == HY_NCHUNKS - 1, _shift_up(last), last)
    tiles = [first] + [_pair(chunk_ref, i) for i in range(HY_CHUNK)] + [last]
    w = w_ref[...]
    w0, w1, w2, b = _dup(w[0:1]), _dup(w[1:2]), _dup(w[2:3]), _dup(b_ref[...])
    return [tiles[i] * w0 + tiles[i + 1] * w1 + tiles[i + 2] * w2 + b for i in range(HY_CHUNK)]


N_BATCHES = N_SLOTS // SLOT_BATCH
SPEC_BLOCKS = 16


def _batch_slots(i):
    return [i * SLOT_BATCH + j for j in range(SLOT_BATCH)]


def _stage2_forward(a_ref, gf_ref, slots, sel0):
    xs = []
    for s in slots:
        xr = _stage_load(a_ref, _slot_rows(s))
        xi = _stage_load(a_ref, _slot_rows(N_SLOTS + s))
        xs.append(jnp.concatenate([xr, xi], axis=0).astype(BF16))
    return [jnp.dot(gf_ref[sel0] if j == 0 else gf_ref[1], x, preferred_element_type=F32)
            for j, x in enumerate(xs)]


def _re_im_blocks(y):
    return [(y[16 * i:16 * i + 8], y[16 * i + 8:16 * i + 16]) for i in range(SPEC_BLOCKS)]


def _in_proj_kernel(x_ref, g_ref, w_ref, o_ref, h_scr):
    ncol = 512
    for sb in range(IN_TILES // SUB_TILES):
        j0 = sb * SUB_TILES
        x = x_ref[:, j0:j0 + SUB_TILES, :].reshape(TILE_ROWS * SUB_TILES, D_MODEL)
        ms = jnp.mean(x * x, axis=-1, keepdims=True)
        hn = x * lax.rsqrt(ms + EPS) * g_ref[...]
        for k in range(D_MODEL // LANES):
            h_scr[sb, k] = hn[:, k * LANES:(k + 1) * LANES]
        h = jnp.concatenate(
            [jnp.concatenate([h_scr[sb, k, pl.ds(j, TILE_ROWS, stride=SUB_TILES), :]
                              for k in range(D_MODEL // LANES)], axis=-1).astype(BF16)
             for j in range(SUB_TILES)], axis=0)
        for c in range(PROJ_WIDTH // ncol):
            p = jnp.dot(h, w_ref[:, c * ncol:(c + 1) * ncol], preferred_element_type=F32).astype(BF16)
            for cb in range(ncol // LANES):
                for j in range(SUB_TILES):
                    o_ref[c * (ncol // LANES) + cb, j0 + j] = p[j * TILE_ROWS:(j + 1) * TILE_ROWS,
                                                                cb * LANES:(cb + 1) * LANES]


def _in_proj(x4, pre_g, w_in_bf):
    grid = (BATCH, N_TILES // IN_TILES)
    return pl.pallas_call(
        _in_proj_kernel,
        grid=grid,
        in_specs=[
            pl.BlockSpec((None, TILE_ROWS, IN_TILES, D_MODEL), lambda b, g: (b, 0, g, 0)),
            pl.BlockSpec((1, D_MODEL), lambda b, g: (0, 0)),
            pl.BlockSpec((D_MODEL, PROJ_WIDTH), lambda b, g: (0, 0)),
        ],
        out_specs=pl.BlockSpec((None, PROJ_BLOCKS, IN_TILES, TILE_ROWS, LANES), lambda b, g: (b, 0, g, 0, 0)),
        out_shape=jax.ShapeDtypeStruct((BATCH, PROJ_BLOCKS, N_TILES, TILE_ROWS, LANES), BF16),
        scratch_shapes=[pltpu.VMEM((IN_TILES // SUB_TILES, D_MODEL // LANES, TILE_ROWS * SUB_TILES, LANES), F32)],
        compiler_params=pltpu.CompilerParams(
            dimension_semantics=("arbitrary", "arbitrary"), vmem_limit_bytes=VMEM_LIMIT),
        name="in_proj",
    )(x4, pre_g, w_in_bf)


def _split_bf16(x):
    hi = x.astype(BF16)
    return hi, (x - hi.astype(F32)).astype(BF16)


def _stack_weight_rows(w):
    hi, lo = _split_bf16(w)
    return jnp.concatenate([hi, lo, hi], axis=1)


def _filter_mlp_kernel(z_ref, w1_ref, b1_ref, w2_ref, b2_ref, w3_ref, b3_ref, fr_ref, fr3_ref,
                       wp2_ref, wpl_ref, dl_ref, o_ref):
    def layer(w_ref, b_ref, f_ref, h):
        hi, lo = _split_bf16(h)
        pre = jnp.dot(w_ref[...], jnp.concatenate([hi, hi, lo], axis=0), preferred_element_type=F32)
        return jnp.sin(f_ref[...] * (pre + b_ref[...]))

    h = layer(w1_ref, b1_ref, fr_ref, z_ref[...])
    h = layer(w2_ref, b2_ref, fr_ref, h)
    h = layer(w3_ref, b3_ref, fr3_ref, h)
    hi, lo = _split_bf16(h.T)
    hs = jnp.concatenate([hi, lo], axis=1)
    rows = CHUNK * TILE_ROWS
    r = lax.broadcasted_iota(jnp.int32, (rows, HYENA_WIDTH), 0)
    pos = 128 * (r % TILE_ROWS) + CHUNK * pl.program_id(0) + r // TILE_ROWS
    t = pos.astype(F32) / float(SEQ - 1)
    decay = jnp.exp(-t * dl_ref[...])
    for od in range(4):
        cols = slice(od * HYENA_WIDTH, (od + 1) * HYENA_WIDTH)
        taps = (jnp.dot(hs, wp2_ref[:, cols], preferred_element_type=F32)
                + jnp.dot(hi, wpl_ref[:, cols], preferred_element_type=F32)) * decay
        taps = taps.astype(BF16)
        for cb in range(N_CBLK):
            for i in range(CHUNK):
                o_ref[od * N_CBLK + cb, i] = taps[i * TILE_ROWS:(i + 1) * TILE_ROWS, cb * LANES:(cb + 1) * LANES]


def _filter_mlp(zfeat_t, w1, b1, w2, b2, w3, b3, freq, w_proj, abs_deltas):
    hid = FILTER_HIDDEN
    col = lambda v, n: jnp.pad(v.reshape(-1, 1), ((0, n - v.size), (0, 0)))
    w1s = _stack_weight_rows(jnp.pad(w1, ((0, hid - FILTER_EMB), (0, 0))).T)
    w2s = _stack_weight_rows(w2.T)
    w3s = _stack_weight_rows(jnp.pad(w3.T, ((0, LANES - hid), (0, 0))))
    wp_hi, wp_lo = _split_bf16(jnp.pad(w_proj, ((0, LANES - hid), (0, 0))))
    wp2 = jnp.concatenate([wp_hi, wp_hi], axis=0)
    full = lambda shape: pl.BlockSpec(shape, lambda c: (0,) * len(shape))
    cols = CHUNK * TILE_ROWS
    return pl.pallas_call(
        _filter_mlp_kernel,
        grid=(N_CHUNKS,),
        in_specs=[
            pl.BlockSpec((hid, cols), lambda c: (0, c)),
            full((hid, 3 * hid)), full((hid, 1)),
            full((hid, 3 * hid)), full((hid, 1)),
            full((LANES, 3 * hid)), full((LANES, 1)),
            full((hid, 1)), full((LANES, 1)),
            full((2 * LANES, 4 * HYENA_WIDTH)), full((LANES, 4 * HYENA_WIDTH)),
            full((1, HYENA_WIDTH)),
        ],
        out_specs=pl.BlockSpec((4 * N_CBLK, CHUNK, TILE_ROWS, LANES), lambda c: (0, c, 0, 0)),
        out_shape=jax.ShapeDtypeStruct((4 * N_CBLK, N_TILES, TILE_ROWS, LANES), BF16),
        compiler_params=pltpu.CompilerParams(dimension_semantics=("arbitrary",), vmem_limit_bytes=VMEM_LIMIT),
        name="filter_mlp",
    )(zfeat_t, w1s, col(b1, hid), w2s, col(b2, hid), w3s, col(b3, LANES), col(freq, hid), col(freq, LANES),
      wp2, wp_lo, abs_deltas)


SPEC_SLOTS_PER_STEP = 16
N_SPEC_STEPS = N_SLOTS // SPEC_SLOTS_PER_STEP


def _filter_spec_kernel(fa_ref, fb_ref, ba_ref, bb_ref, bxa_ref, bxb_ref, d_ref, f1f_ref, gf_ref, k_ref, a_scr):
    t = pl.program_id(2)

    @pl.when(t < N_CHUNKS)
    def _():
        for i in range(CHUNK):
            n2 = t * CHUNK + i
            fwd = jnp.concatenate([fa_ref[i], fb_ref[i]], axis=-1)
            if i == 0:
                bwd = jnp.concatenate([bxa_ref[0], bxb_ref[0]], axis=-1)
            else:
                bwd = jnp.concatenate([ba_ref[CHUNK - i], bb_ref[CHUNK - i]], axis=-1)
            taps = jnp.concatenate([fwd, bwd], axis=0)
            _stage_store(a_scr, _a_rows(n2), jnp.dot(f1f_ref[n2], taps, preferred_element_type=F32))

    @pl.when(t >= N_CHUNKS)
    def _():
        q = t - N_CHUNKS
        for bi in range(SPEC_SLOTS_PER_STEP // SLOT_BATCH):
            base = q * SPEC_SLOTS_PER_STEP + bi * SLOT_BATCH
            sel0 = jnp.where(q == 0, 0, 1) if bi == 0 else 1
            ys = _stage2_forward(a_scr, gf_ref, [base + j for j in range(SLOT_BATCH)], sel0)
            d = d_ref[...]
            for j, y in enumerate(ys):
                blocks = []
                for i, (yr, yi) in enumerate(_re_im_blocks(y)):
                    yr = yr + d
                    if bi == 0 and j == 0 and i == 0:
                        rows = lax.broadcasted_iota(jnp.int32, yi.shape, 0)
                        yi = yi + jnp.where((rows == 0) & (q == 0), d, 0.0)
                    blocks += [yr, yi]
                spec = jnp.concatenate(blocks, axis=0).astype(BF16)
                k_ref[0, bi * SLOT_BATCH + j] = spec[:, :LANES]
                k_ref[1, bi * SLOT_BATCH + j] = spec[:, LANES:]


def _filter_spec(taps, hyena_d3, f1f, gf):
    grid = (2, N_CBLK // 2, N_CHUNKS + N_SPEC_STEPS)
    fchunk = lambda t: jnp.minimum(t, N_CHUNKS - 1)
    const = lambda shape: pl.BlockSpec(shape, lambda o, c, t: (0,) * len(shape), pipeline_mode=pl.Buffered(1))

    def fwd_spec(k):
        return pl.BlockSpec((None, CHUNK, TILE_ROWS, LANES),
                            lambda o, c, t: ((2 * o) * N_CBLK + 2 * c + k, fchunk(t), 0, 0))

    def bwd_spec(k):
        return pl.BlockSpec((None, CHUNK, TILE_ROWS, LANES),
                            lambda o, c, t: ((2 * o + 1) * N_CBLK + 2 * c + k, N_CHUNKS - 1 - fchunk(t), 0, 0))

    def bwd_tile_spec(k):
        return pl.BlockSpec((None, 1, TILE_ROWS, LANES),
                            lambda o, c, t: ((2 * o + 1) * N_CBLK + 2 * c + k, (N_TILES - CHUNK * fchunk(t)) % N_TILES, 0, 0))

    return pl.pallas_call(
        _filter_spec_kernel,
        grid=grid,
        in_specs=[fwd_spec(0), fwd_spec(1), bwd_spec(0), bwd_spec(1), bwd_tile_spec(0), bwd_tile_spec(1),
                  pl.BlockSpec((None, 1, 2 * LANES), lambda o, c, t: (o, 0, c)),
                  const((N_TILES, 128, 128)), const((2, 256, 256))],
        out_specs=pl.BlockSpec((None, 2, SPEC_SLOTS_PER_STEP, 256, LANES),
                               lambda o, c, t: (o, c, jnp.maximum(t - N_CHUNKS, 0), 0, 0)),
        out_shape=jax.ShapeDtypeStruct((2, N_CBLK, N_SLOTS, 256, LANES), BF16),
        scratch_shapes=[pltpu.VMEM((2, N_TILES * A_STRIDE, LANES), F32)],
        compiler_params=pltpu.CompilerParams(
            dimension_semantics=("arbitrary", "arbitrary", "arbitrary"), vmem_limit_bytes=VMEM_LIMIT),
        name="filter_spec",
    )(taps, taps, taps, taps, taps, taps, hyena_d3, f1f, gf)


_T_S1 = 0
_T_F0 = HY_NCHUNKS
_T_M = HY_NCHUNKS + 1
_T_F1 = 2 * HY_NCHUNKS + 1
_T_E = 2 * HY_NCHUNKS + 2
_T_END = 3 * HY_NCHUNKS + 2


def _hyena_kernel(pv_ref, pvp_ref, pvn_ref, px1_ref, px1p_ref, px1n_ref, px2_ref, px2p_ref, px2n_ref, pz_ref,
                  cw_ref, k_ref, f1_ref, f1i_ref, gf_ref, gb_ref, o_ref, a_scr, z_scr):
    t = pl.program_id(2)
    wv_ref, bv_ref = cw_ref.at[0:3], cw_ref.at[3:4]
    w1_ref, b1_ref = cw_ref.at[4:7], cw_ref.at[7:8]
    w2_ref, b2_ref = cw_ref.at[8:11], cw_ref.at[11:12]

    def stage1(n2, u_bf):
        _stage_store(a_scr, _a_rows(n2), jnp.dot(f1_ref[n2], u_bf, preferred_element_type=F32))

    def inv_stage1(n2):
        return jnp.dot(f1i_ref[n2], _stage_load(a_scr, _a_rows(n2)).astype(BF16), preferred_element_type=F32)

    def filter_multiply(slots, ys, first, buf):
        for j, (s, y) in enumerate(zip(slots, ys)):
            blocks = []
            for i, (yr, yi) in enumerate(_re_im_blocks(y)):
                kblk = k_ref[s, 16 * i:16 * i + 16, :].astype(F32)
                kr, ki = kblk[:8], kblk[8:]
                if first and j == 0 and i == 0:
                    rows = lax.broadcasted_iota(jnp.int32, kr.shape, 0)
                    ka, kb, kd = kr, jnp.where(rows == 0, 0.0, ki), jnp.where(rows == 0, ki, kr)
                else:
                    ka, kb, kd = kr, ki, kr
                ka, kb, kd = _dup(ka), _dup(kb), _dup(kd)
                blocks += [yr * ka - yi * kb, yr * kb + yi * kd]
            z_scr[buf, j] = jnp.concatenate(blocks, axis=0).astype(BF16)

    def stage2_inverse(slots, first_sel, buf):
        bms = [jnp.dot(gb_ref[first_sel] if j == 0 else gb_ref[1], z_scr[buf, j], preferred_element_type=F32)
               for j in range(SLOT_BATCH)]
        for s, bm in zip(slots, bms):
            _stage_store(a_scr, _slot_rows(s), bm[:128])
            _stage_store(a_scr, _slot_rows(N_SLOTS + s), bm[128:])

    def spectral_phase():
        slots0 = _batch_slots(0)
        filter_multiply(slots0, _stage2_forward(a_scr, gf_ref, slots0, 0), True, 0)

        def loop(i, carry):
            slots = _batch_slots(i)
            ys = _stage2_forward(a_scr, gf_ref, slots, 1)
            stage2_inverse(_batch_slots(i - 1), jnp.where(i == 1, 0, 1), (i - 1) % 2)
            filter_multiply(slots, ys, False, i % 2)
            return carry

        lax.fori_loop(1, N_BATCHES, loop, 0)
        stage2_inverse(_batch_slots(N_BATCHES - 1), 1, (N_BATCHES - 1) % 2)

    @pl.when(t < _T_F0)
    def _():
        vs = _short_conv_chunk(pv_ref, pvp_ref, pvn_ref, t, wv_ref, bv_ref)
        for i in range(HY_CHUNK):
            stage1(t * HY_CHUNK + i, vs[i].astype(BF16))

    @pl.when((t == _T_F0) | (t == _T_F1))
    def _():
        spectral_phase()

    @pl.when((t >= _T_M) & (t < _T_F1))
    def _():
        tc = t - _T_M
        gates = _short_conv_chunk(px1_ref, px1p_ref, px1n_ref, tc, w1_ref, b1_ref)
        for sb in range(0, HY_CHUNK, HY_SUB):
            ys = [inv_stage1(tc * HY_CHUNK + i) for i in range(sb, sb + HY_SUB)]
            us = [(gates[sb + k] * ys[k]).astype(BF16) for k in range(HY_SUB)]
            for k in range(HY_SUB):
                stage1(tc * HY_CHUNK + sb + k, us[k])

    @pl.when(t >= _T_E)
    def _():
        tc = t - _T_E
        gates = _short_conv_chunk(px2_ref, px2p_ref, px2n_ref, tc, w2_ref, b2_ref)
        for i in range(HY_CHUNK):
            res = (gates[i] * inv_stage1(tc * HY_CHUNK + i) * _silu(_pair(pz_ref, i))).astype(BF16)
            o_ref[0, i] = res[:, :LANES]
            o_ref[1, i] = res[:, LANES:]


def _hyena(p5, conv_w, conv_b, kspec, f1, f1i, gf, gb):
    grid = (N_CBLK, BATCH // 2, _T_END)
    cw = jnp.concatenate(
        [jnp.concatenate([conv_w[:, k * HYENA_WIDTH:(k + 1) * HYENA_WIDTH],
                          conv_b[:, k * HYENA_WIDTH:(k + 1) * HYENA_WIDTH]], axis=0) for k in range(3)], axis=0)

    def clampc(t, start):
        return jnp.clip(t - start, 0, HY_NCHUNKS - 1)

    def chunk_spec(col0, start):
        return pl.BlockSpec((2, None, HY_CHUNK, TILE_ROWS, LANES),
                            lambda c, b, t: (b, col0 + c, clampc(t, start), 0, 0))

    def prev_spec(col0, start):
        return pl.BlockSpec((2, None, 1, TILE_ROWS, LANES),
                            lambda c, b, t: (b, col0 + c, (clampc(t, start) * HY_CHUNK + N_TILES - 1) % N_TILES, 0, 0))

    def next_spec(col0, start):
        return pl.BlockSpec((2, None, 1, TILE_ROWS, LANES),
                            lambda c, b, t: (b, col0 + c, (clampc(t, start) * HY_CHUNK + HY_CHUNK) % N_TILES, 0, 0))

    const = lambda shape: pl.BlockSpec(shape, lambda c, b, t: (0,) * len(shape), pipeline_mode=pl.Buffered(1))
    in_specs = [
        chunk_spec(0, _T_S1), prev_spec(0, _T_S1), next_spec(0, _T_S1),
        chunk_spec(N_CBLK, _T_M), prev_spec(N_CBLK, _T_M), next_spec(N_CBLK, _T_M),
        chunk_spec(2 * N_CBLK, _T_E), prev_spec(2 * N_CBLK, _T_E), next_spec(2 * N_CBLK, _T_E),
        chunk_spec(3 * N_CBLK, _T_E),
        pl.BlockSpec((12, LANES), lambda c, b, t: (0, c)),
        pl.BlockSpec((None, None, N_SLOTS, 256, LANES), lambda c, b, t: (jnp.where(t >= _T_M, 1, 0), c, 0, 0, 0)),
        const((N_TILES, 128, TILE_ROWS)), const((N_TILES, TILE_ROWS, 128)),
        const((2, 256, 256)), const((2, 256, 256)),
    ]
    return pl.pallas_call(
        _hyena_kernel,
        grid=grid,
        in_specs=in_specs,
        out_specs=pl.BlockSpec((2, None, HY_CHUNK, TILE_ROWS, LANES), lambda c, b, t: (b, c, clampc(t, _T_E), 0, 0)),
        out_shape=jax.ShapeDtypeStruct((BATCH, N_CBLK, N_TILES, TILE_ROWS, LANES), BF16),
        scratch_shapes=[pltpu.VMEM((2, N_TILES * A_STRIDE, LANES), F32),
                        pltpu.VMEM((2, SLOT_BATCH, 256, 2 * LANES), BF16)],
        compiler_params=pltpu.CompilerParams(
            dimension_semantics=("arbitrary", "arbitrary", "arbitrary"), vmem_limit_bytes=VMEM_LIMIT),
        name="hyena",
    )(p5, p5, p5, p5, p5, p5, p5, p5, p5, p5, cw, kspec, f1, f1i, gf, gb)


_POOL_GROUP = 8


def _pool_kernel(u_ref, z_ref, pw_ref, ps_ref, o_ref):
    g = pl.program_id(1)
    n_groups = N_TILES // _POOL_GROUP
    n1 = lax.broadcasted_iota(jnp.int32, (TILE_ROWS, LANES), 0)

    def tile(idx):
        if isinstance(idx, int) and idx < 0:
            return _shift_down(u_ref[idx + N_TILES].astype(F32))
        if isinstance(idx, int) and idx >= N_TILES:
            return _shift_up(u_ref[idx - N_TILES].astype(F32))
        return u_ref[idx].astype(F32)

    for gi_, w in enumerate(POOL_WINDOWS):
        @pl.when(g == gi_)
        def _(w=w):
            lo, hi = w // 2, w - 1 - w // 2

            def run_group(base, window_sum, edge):
                pooled = []
                for i in range(_POOL_GROUP):
                    n2 = base + i
                    if i > 0:
                        window_sum = window_sum + tile(n2 + hi) - tile(n2 - 1 - lo)
                    if edge:
                        pos = 128 * n1 + n2
                        cnt = (jnp.minimum(pos + hi, SEQ - 1) - jnp.maximum(pos - lo, 0) + 1).astype(F32)
                        mean = window_sum / cnt
                    else:
                        mean = window_sum * (1.0 / w)
                    pooled.append((mean - tile(n2)).astype(BF16))
                pooled = jnp.concatenate(pooled, axis=0)
                y = jnp.dot(pooled, pw_ref[...], preferred_element_type=F32) * ps_ref[...]
                for i in range(_POOL_GROUP):
                    n2 = base + i
                    o_ref[n2] = (y[i * TILE_ROWS:(i + 1) * TILE_ROWS] * _silu(z_ref[n2].astype(F32))).astype(BF16)
                nxt = base + _POOL_GROUP
                return window_sum + tile(nxt + hi) - tile(nxt - 1 - lo)

            first = tile(-lo)
            for d in range(-lo + 1, hi + 1):
                first = first + tile(d)
            carry = run_group(0, first, True)
            carry = lax.fori_loop(1, n_groups - 1,
                                  lambda c, ws: run_group(c * _POOL_GROUP, ws, False), carry)
            run_group(N_TILES - _POOL_GROUP, carry, True)


def _pool(p5, pool_w_bf, pool_scale):
    u0 = 4 * N_CBLK
    z0 = 5 * N_CBLK
    return pl.pallas_call(
        _pool_kernel,
        grid=(BATCH, len(POOL_WINDOWS)),
        in_specs=[
            pl.BlockSpec((None, None, N_TILES, TILE_ROWS, LANES), lambda b, g: (b, u0 + g, 0, 0, 0)),
            pl.BlockSpec((None, None, N_TILES, TILE_ROWS, LANES), lambda b, g: (b, z0 + g, 0, 0, 0)),
            pl.BlockSpec((None, LANES, LANES), lambda b, g: (g, 0, 0)),
            pl.BlockSpec((1, LANES), lambda b, g: (0, g)),
        ],
        out_specs=pl.BlockSpec((None, None, N_TILES, TILE_ROWS, LANES), lambda b, g: (b, g, 0, 0, 0)),
        out_shape=jax.ShapeDtypeStruct((BATCH, N_CBLK, N_TILES, TILE_ROWS, LANES), BF16),
        compiler_params=pltpu.CompilerParams(
            dimension_semantics=("arbitrary", "arbitrary"), vmem_limit_bytes=VMEM_LIMIT),
        name="pool",
    )(p5, p5, pool_w_bf, pool_scale)


def _out_kernel(yh_ref, yp_ref, x_ref, gh_ref, gp_ref, w_ref, gpost_ref, o_ref, r_scr):
    def group_norm(ref, gain_ref, j0):
        rows = []
        for j in range(j0, j0 + SUB_TILES):
            rows.append(jnp.concatenate([ref[cb, j] for cb in range(N_CBLK)], axis=-1).astype(F32))
        y = jnp.concatenate(rows, axis=0)
        ms = jnp.mean(y * y, axis=-1, keepdims=True)
        return (y * lax.rsqrt(ms + EPS) * gain_ref[...]).astype(BF16)

    for sb in range(IN_TILES // SUB_TILES):
        j0 = sb * SUB_TILES
        yc = jnp.concatenate([group_norm(yh_ref, gh_ref, j0), group_norm(yp_ref, gp_ref, j0)], axis=-1)
        out = jnp.dot(yc, w_ref[...], preferred_element_type=F32)
        ms = jnp.mean(out * out, axis=-1, keepdims=True)
        out = out * lax.rsqrt(ms + EPS) * gpost_ref[...]
        for j in range(SUB_TILES):
            for k in range(D_MODEL // LANES):
                r_scr[sb, k, pl.ds(j, TILE_ROWS, stride=SUB_TILES), :] = out[j * TILE_ROWS:(j + 1) * TILE_ROWS,
                                                                             k * LANES:(k + 1) * LANES]
        r = jnp.concatenate([r_scr[sb, k] for k in range(D_MODEL // LANES)], axis=-1)
        o_ref[:, j0:j0 + SUB_TILES, :] = (x_ref[:, j0:j0 + SUB_TILES, :]
                                          + r.reshape(TILE_ROWS, SUB_TILES, D_MODEL))


def _out_proj(yh, yp, x4, norm_h_g, norm_p_g, w_out_bf, post_g):
    grid = (BATCH, N_TILES // IN_TILES)
    y_spec = pl.BlockSpec((None, N_CBLK, IN_TILES, TILE_ROWS, LANES), lambda b, g: (b, 0, g, 0, 0))
    x_spec = pl.BlockSpec((None, TILE_ROWS, IN_TILES, D_MODEL), lambda b, g: (b, 0, g, 0))
    return pl.pallas_call(
        _out_kernel,
        grid=grid,
        in_specs=[
            y_spec, y_spec, x_spec,
            pl.BlockSpec((1, HYENA_WIDTH), lambda b, g: (0, 0)),
            pl.BlockSpec((1, POOL_WIDTH), lambda b, g: (0, 0)),
            pl.BlockSpec((D_MODEL, D_MODEL), lambda b, g: (0, 0)),
            pl.BlockSpec((1, D_MODEL), lambda b, g: (0, 0)),
        ],
        out_specs=x_spec,
        out_shape=jax.ShapeDtypeStruct((BATCH, TILE_ROWS, N_TILES, D_MODEL), F32),
        scratch_shapes=[pltpu.VMEM((IN_TILES // SUB_TILES, D_MODEL // LANES, TILE_ROWS * SUB_TILES, LANES), F32)],
        compiler_params=pltpu.CompilerParams(
            dimension_semantics=("arbitrary", "arbitrary"), vmem_limit_bytes=VMEM_LIMIT),
        name="out_proj",
    )(yh, yp, x4, norm_h_g, norm_p_g, w_out_bf, post_g)


def kernel(x, pre_norm_g, w_in, conv_w, conv_b, filt_w1, filt_b1, filt_w2, filt_b2, filt_w3, filt_b3,
           filt_freq, filt_w_out, hyena_d, pool_w, pool_scale, norm_h_g, norm_p_g, w_out, post_norm_g):
    assert x.shape == (BATCH, SEQ, D_MODEL) and pre_norm_g.shape[0] == 1
    f1, f1i, gf, gb, f1f = (jnp.asarray(m, F32).astype(BF16) for m in (_F1, _F1I, _GF, _GB, _F1F))

    x4 = x.reshape(BATCH, TILE_ROWS, N_TILES, D_MODEL)
    p5 = _in_proj(x4, pre_norm_g, w_in[0].astype(BF16))

    taps = _filter_mlp(jnp.asarray(_ZFEAT), filt_w1[0], filt_b1[0], filt_w2[0], filt_b2[0], filt_w3[0], filt_b3[0],
                       filt_freq[0], filt_w_out[0], jnp.asarray(_ABS_DELTAS))
    kspec = _filter_spec(taps, hyena_d[0].reshape(2, 1, HYENA_WIDTH), f1f, gf)

    yh = _hyena(p5, conv_w[0], conv_b, kspec, f1, f1i, gf, gb)
    yp = _pool(p5, pool_w[0].astype(BF16), pool_scale)
    out4 = _out_proj(yh, yp, x4, norm_h_g, norm_p_g, w_out[0].astype(BF16), post_norm_g)
    return out4.reshape(BATCH, SEQ, D_MODEL)
```

```python
import functools
import math

import numpy as np
import jax
import jax.numpy as jnp
from jax import lax
from jax.experimental import pallas as pl
from jax.experimental.pallas import tpu as pltpu

F32 = jnp.float32
BF16 = jnp.bfloat16

D_MODEL = 1024
BATCH = 4
SEQ = 8192
HYENA_WIDTH = 512
POOL_WIDTH = 512
POOL_WINDOWS = (2, 4, 8, 16)
FILTER_EMB = 33
FILTER_BANDS = 16
FILTER_HIDDEN = 64
PROJ_WIDTH = 3072
EPS = 1e-6

LANES = 128
N_FFT = 2 * SEQ
N_TILES = 128
TILE_ROWS = SEQ // N_TILES
N_SLOTS = 64
CHUNK = 32
HY_SUB = 32
N_CHUNKS = N_TILES // CHUNK
A_STRIDE = 72
SLOT_BATCH = 4
N_CBLK = HYENA_WIDTH // LANES
PROJ_BLOCKS = PROJ_WIDTH // LANES
IN_TILES = 16
SUB_TILES = 8
VMEM_LIMIT = 60 * 1024 * 1024


def _dft_tables():
    n1 = np.arange(TILE_ROWS)
    n2 = np.arange(N_TILES)
    s = np.arange(N_SLOTS)
    ph = 2 * np.pi * (n2[:, None, None] * s[None, :, None] / N_FFT
                      + n1[None, None, :] * s[None, :, None] / 128.0)
    f1 = np.zeros((N_TILES, 128, TILE_ROWS))
    f1[:, :64, :] = np.cos(ph)
    f1[:, 64:, :] = -np.sin(ph)
    f1[:, 0, :] = 1.0
    f1[:, 64, :] = (-1.0) ** n1
    php = np.transpose(ph, (0, 2, 1))
    f1i = np.zeros((N_TILES, TILE_ROWS, 128))
    f1i[:, :, :64] = 2 * np.cos(php) / N_FFT
    f1i[:, :, 64:] = -2 * np.sin(php) / N_FFT
    f1i[:, :, 0] = 1.0 / N_FFT
    f1i[:, :, 64] = ((-1.0) ** n1)[None, :] / N_FFT
    k2 = np.arange(128)
    th = 2 * np.pi * np.outer(k2, n2) / 128.0
    c, sn = np.cos(th), np.sin(th)
    g = np.block([[c, sn], [-sn, c]])
    gi = np.block([[c, -sn], [sn, c]])
    kk = np.arange(64)
    tha = 2 * np.pi * np.outer(kk, n2) / 128.0
    thb = 2 * np.pi * np.outer(64 + 128 * kk, n2) / N_FFT
    g0 = np.zeros((256, 256))
    g0[0:64, 0:128] = np.cos(tha)
    g0[64:128, 128:256] = np.cos(thb)
    g0[128:192, 0:128] = -np.sin(tha)
    g0[128, 0:128] = (-1.0) ** n2
    g0[192:256, 128:256] = -np.sin(thb)
    g0i = np.zeros((256, 256))
    g0i[0:128, 0:64] = 2 * np.cos(tha.T)
    g0i[0:128, 0] = 1.0
    g0i[0:128, 128:192] = -2 * np.sin(tha.T)
    g0i[0:128, 128] = (-1.0) ** n2
    g0i[128:256, 64:128] = 2 * np.cos(thb.T)
    g0i[128:256, 192:256] = -2 * np.sin(thb.T)
    q = np.arange(256)
    perm = np.where(q % 16 < 8, 8 * (q // 16) + q % 16, 128 + 8 * (q // 16) + q % 16 - 8)
    gf = np.stack([g0[perm, :], g[perm, :]])
    gb = np.stack([g0i[:, perm], gi[:, perm]])
    n1f = np.arange(128)
    phf = 2 * np.pi * (n2[:, None, None] * s[None, :, None] / N_FFT
                       + n1f[None, None, :] * s[None, :, None] / 128.0)
    full = np.zeros((N_TILES, 128, 128))
    full[:, :64, :] = np.cos(phf)
    full[:, 64:, :] = -np.sin(phf)
    full[:, 0, :] = 1.0
    full[:, 64, :] = (-1.0) ** n1f
    rev = full[:, :, 127:63:-1].copy()
    rev[0, :, 1:] = full[0, :, 127:64:-1]
    rev[0, :, 0] = 0.0
    f1f = np.concatenate([full[:, :, :64], rev], axis=2)
    pair64 = np.arange(128) % 2 * 64 + np.arange(128) // 2
    pair128 = np.arange(256) % 2 * 128 + np.arange(256) // 2
    f1, f1f = f1[:, pair64, :], f1f[:, pair64, :]
    f1i = f1i[:, :, pair64]
    gf = gf[:, :, pair128]
    gb = gb[:, pair128, :]
    return f1, f1i, gf, gb, f1f


def _filter_features():
    pos = np.arange(SEQ, dtype=np.float64)
    t = pos / (SEQ - 1)
    ang = 2.0 * math.pi * pos / SEQ
    bands = np.linspace(1e-4, FILTER_BANDS - 1, FILTER_BANDS)
    z = np.concatenate([t[:, None], np.cos(bands[None, :] * ang[:, None]),
                        -np.sin(bands[None, :] * ang[:, None])], axis=-1)
    z = z.reshape(TILE_ROWS, N_TILES, FILTER_EMB).transpose(1, 0, 2).reshape(SEQ, FILTER_EMB)
    zp = np.zeros((FILTER_HIDDEN, SEQ))
    zp[:FILTER_EMB, :] = z.T
    max_decay = math.log(1e-2) / 0.3
    min_decay = math.log(1e-2) / 1.5
    deltas = np.abs(np.linspace(min_decay, max_decay, HYENA_WIDTH))
    return zp.astype(np.float32), deltas.astype(np.float32)[None, :]


_F1, _F1I, _GF, _GB, _F1F = _dft_tables()
_ZFEAT, _ABS_DELTAS = _filter_features()


def _shift_down(x):
    rows = lax.broadcasted_iota(jnp.int32, x.shape, 0)
    return jnp.where(rows == 0, 0.0, pltpu.roll(x, 1, axis=0))


def _shift_up(x):
    rows = lax.broadcasted_iota(jnp.int32, x.shape, 0)
    return jnp.where(rows == x.shape[0] - 1, 0.0, pltpu.roll(x, x.shape[0] - 1, axis=0))


def _pair(ref, i):
    return jnp.concatenate([ref[0, i], ref[1, i]], axis=-1).astype(F32)


def _dup(x):
    return jnp.concatenate([x, x], axis=-1)


def _silu(z):
    hz = 0.5 * z
    return hz * (1.0 + jnp.tanh(hz))


def _tile_rows(n2):
    return pl.ds(pl.multiple_of(n2 * A_STRIDE, 8), N_SLOTS)


def _slot_rows(s):
    return pl.ds(s, N_TILES, stride=A_STRIDE)


def _stage_load(a_ref, rows):
    words = jnp.concatenate([a_ref[0, rows, :], a_ref[1, rows, :]], axis=-1)
    return pltpu.bitcast(words, BF16)


def _stage_store(a_ref, rows, val):
    words = pltpu.bitcast(val.astype(BF16), jnp.uint32)
    a_ref[0, rows, :] = words[:, :LANES]
    a_ref[1, rows, :] = words[:, LANES:]


N_BATCHES = N_SLOTS // SLOT_BATCH
SPEC_BLOCKS = 16


def _batch_slots(i):
    return [i * SLOT_BATCH + j for j in range(SLOT_BATCH)]


def _stage2_forward(a_ref, gf_ref, slots, sel0):
    xs = [_stage_load(a_ref, _slot_rows(s)) for s in slots]
    return [jnp.dot(gf_ref[sel0] if j == 0 else gf_ref[1], x, preferred_element_type=F32)
            for j, x in enumerate(xs)]


def _re_im_blocks(y):
    return [(y[16 * i:16 * i + 8], y[16 * i + 8:16 * i + 16]) for i in range(SPEC_BLOCKS)]


def _in_proj_kernel(x_ref, g_ref, w_ref, o_ref, h_scr):
    ncol = 512
    for sb in range(IN_TILES // SUB_TILES):
        j0 = sb * SUB_TILES
        x = x_ref[:, j0:j0 + SUB_TILES, :].reshape(TILE_ROWS * SUB_TILES, D_MODEL)
        ms = jnp.mean(x * x, axis=-1, keepdims=True)
        hn = x * lax.rsqrt(ms + EPS) * g_ref[...]
        for k in range(D_MODEL // LANES):
            h_scr[sb, k] = hn[:, k * LANES:(k + 1) * LANES]
        h = jnp.concatenate(
            [jnp.concatenate([h_scr[sb, k, pl.ds(j, TILE_ROWS, stride=SUB_TILES), :]
                              for k in range(D_MODEL // LANES)], axis=-1).astype(BF16)
             for j in range(SUB_TILES)], axis=0)
        for c in range(PROJ_WIDTH // ncol):
            p = jnp.dot(h, w_ref[:, c * ncol:(c + 1) * ncol], preferred_element_type=F32).astype(BF16)
            for cb in range(ncol // LANES):
                for j in range(SUB_TILES):
                    o_ref[c * (ncol // LANES) + cb, j0 + j] = p[j * TILE_ROWS:(j + 1) * TILE_ROWS,
                                                                cb * LANES:(cb + 1) * LANES]


def _in_proj(x4, pre_g, w_in_bf):
    grid = (BATCH, N_TILES // IN_TILES)
    return pl.pallas_call(
        _in_proj_kernel,
        grid=grid,
        in_specs=[
            pl.BlockSpec((None, TILE_ROWS, IN_TILES, D_MODEL), lambda b, g: (b, 0, g, 0)),
            pl.BlockSpec((1, D_MODEL), lambda b, g: (0, 0)),
            pl.BlockSpec((D_MODEL, PROJ_WIDTH), lambda b, g: (0, 0)),
        ],
        out_specs=pl.BlockSpec((None, PROJ_BLOCKS, IN_TILES, TILE_ROWS, LANES), lambda b, g: (b, 0, g, 0, 0)),
        out_shape=jax.ShapeDtypeStruct((BATCH, PROJ_BLOCKS, N_TILES, TILE_ROWS, LANES), BF16),
        scratch_shapes=[pltpu.VMEM((IN_TILES // SUB_TILES, D_MODEL // LANES, TILE_ROWS * SUB_TILES, LANES), F32)],
        compiler_params=pltpu.CompilerParams(
            dimension_semantics=("arbitrary", "arbitrary"), vmem_limit_bytes=VMEM_LIMIT),
        name="in_proj",
    )(x4, pre_g, w_in_bf)


def _split_bf16(x):
    hi = x.astype(BF16)
    return hi, (x - hi.astype(F32)).astype(BF16)


def _stack_weight_rows(w):
    hi, lo = _split_bf16(w)
    return jnp.concatenate([hi, lo, hi], axis=1)


def _filter_mlp_kernel(z_ref, w1_ref, b1_ref, w2_ref, b2_ref, w3_ref, b3_ref, fr_ref, fr3_ref,
                       wp2_ref, wpl_ref, dl_ref, o_ref):
    def layer(w_ref, b_ref, f_ref, h):
        hi, lo = _split_bf16(h)
        pre = jnp.dot(w_ref[...], jnp.concatenate([hi, hi, lo], axis=0), preferred_element_type=F32)
        return jnp.sin(f_ref[...] * (pre + b_ref[...]))

    h = layer(w1_ref, b1_ref, fr_ref, z_ref[...])
    h = layer(w2_ref, b2_ref, fr_ref, h)
    h = layer(w3_ref, b3_ref, fr3_ref, h)
    hi, lo = _split_bf16(h.T)
    hs = jnp.concatenate([hi, lo], axis=1)
    rows = CHUNK * TILE_ROWS
    r = lax.broadcasted_iota(jnp.int32, (rows, HYENA_WIDTH), 0)
    pos = 128 * (r % TILE_ROWS) + CHUNK * pl.program_id(0) + r // TILE_ROWS
    t = pos.astype(F32) / float(SEQ - 1)
    decay = jnp.exp(-t * dl_ref[...])
    for od in range(4):
        cols = slice(od * HYENA_WIDTH, (od + 1) * HYENA_WIDTH)
        taps = (jnp.dot(hs, wp2_ref[:, cols], preferred_element_type=F32)
                + jnp.dot(hi, wpl_ref[:, cols], preferred_element_type=F32)) * decay
        taps = taps.astype(BF16)
        for cb in range(N_CBLK):
            for i in range(CHUNK):
                o_ref[od * N_CBLK + cb, i] = taps[i * TILE_ROWS:(i + 1) * TILE_ROWS, cb * LANES:(cb + 1) * LANES]


def _filter_mlp(zfeat_t, w1, b1, w2, b2, w3, b3, freq, w_proj, abs_deltas):
    hid = FILTER_HIDDEN
    col = lambda v, n: jnp.pad(v.reshape(-1, 1), ((0, n - v.size), (0, 0)))
    w1s = _stack_weight_rows(jnp.pad(w1, ((0, hid - FILTER_EMB), (0, 0))).T)
    w2s = _stack_weight_rows(w2.T)
    w3s = _stack_weight_rows(jnp.pad(w3.T, ((0, LANES - hid), (0, 0))))
    wp_hi, wp_lo = _split_bf16(jnp.pad(w_proj, ((0, LANES - hid), (0, 0))))
    wp2 = jnp.concatenate([wp_hi, wp_hi], axis=0)
    full = lambda shape: pl.BlockSpec(shape, lambda c: (0,) * len(shape))
    cols = CHUNK * TILE_ROWS
    return pl.pallas_call(
        _filter_mlp_kernel,
        grid=(N_CHUNKS,),
        in_specs=[
            pl.BlockSpec((hid, cols), lambda c: (0, c)),
            full((hid, 3 * hid)), full((hid, 1)),
            full((hid, 3 * hid)), full((hid, 1)),
            full((LANES, 3 * hid)), full((LANES, 1)),
            full((hid, 1)), full((LANES, 1)),
            full((2 * LANES, 4 * HYENA_WIDTH)), full((LANES, 4 * HYENA_WIDTH)),
            full((1, HYENA_WIDTH)),
        ],
        out_specs=pl.BlockSpec((4 * N_CBLK, CHUNK, TILE_ROWS, LANES), lambda c: (0, c, 0, 0)),
        out_shape=jax.ShapeDtypeStruct((4 * N_CBLK, N_TILES, TILE_ROWS, LANES), BF16),
        compiler_params=pltpu.CompilerParams(dimension_semantics=("arbitrary",), vmem_limit_bytes=VMEM_LIMIT),
        name="filter_mlp",
    )(zfeat_t, w1s, col(b1, hid), w2s, col(b2, hid), w3s, col(b3, LANES), col(freq, hid), col(freq, LANES),
      wp2, wp_lo, abs_deltas)


SPEC_SLOTS_PER_STEP = 16
N_SPEC_STEPS = N_SLOTS // SPEC_SLOTS_PER_STEP


def _filter_spec_kernel(fa_ref, fb_ref, ba_ref, bb_ref, bxa_ref, bxb_ref, d_ref, f1f_ref, gf_ref, k_ref, a_scr):
    t = pl.program_id(2)

    @pl.when(t < N_CHUNKS)
    def _():
        for i in range(CHUNK):
            n2 = t * CHUNK + i
            fwd = jnp.concatenate([fa_ref[i], fb_ref[i]], axis=-1)
            if i == 0:
                bwd = jnp.concatenate([bxa_ref[0], bxb_ref[0]], axis=-1)
            else:
                bwd = jnp.concatenate([ba_ref[CHUNK - i], bb_ref[CHUNK - i]], axis=-1)
            taps = jnp.concatenate([fwd, bwd], axis=0)
            _stage_store(a_scr, _tile_rows(n2), jnp.dot(f1f_ref[n2], taps, preferred_element_type=F32))

    @pl.when(t >= N_CHUNKS)
    def _():
        q = t - N_CHUNKS
        for bi in range(SPEC_SLOTS_PER_STEP // SLOT_BATCH):
            base = q * SPEC_SLOTS_PER_STEP + bi * SLOT_BATCH
            sel0 = jnp.where(q == 0, 0, 1) if bi == 0 else 1
            ys = _stage2_forward(a_scr, gf_ref, [base + j for j in range(SLOT_BATCH)], sel0)
            d = d_ref[...]
            for j, y in enumerate(ys):
                blocks = []
                for i, (yr, yi) in enumerate(_re_im_blocks(y)):
                    yr = yr + d
                    if bi == 0 and j == 0 and i == 0:
                        rows = lax.broadcasted_iota(jnp.int32, yi.shape, 0)
                        yi = yi + jnp.where((rows == 0) & (q == 0), d, 0.0)
                    blocks += [yr, yi]
                spec = jnp.concatenate(blocks, axis=0).astype(BF16)
                k_ref[0, bi * SLOT_BATCH + j] = spec[:, :LANES]
                k_ref[1, bi * SLOT_BATCH + j] = spec[:, LANES:]


def _filter_spec(taps, hyena_d3, f1f, gf):
    grid = (2, N_CBLK // 2, N_CHUNKS + N_SPEC_STEPS)
    fchunk = lambda t: jnp.minimum(t, N_CHUNKS - 1)
    const = lambda shape: pl.BlockSpec(shape, lambda o, c, t: (0,) * len(shape), pipeline_mode=pl.Buffered(1))

    def fwd_spec(k):
        return pl.BlockSpec((None, CHUNK, TILE_ROWS, LANES),
                            lambda o, c, t: ((2 * o) * N_CBLK + 2 * c + k, fchunk(t), 0, 0))

    def bwd_spec(k):
        return pl.BlockSpec((None, CHUNK, TILE_ROWS, LANES),
                            lambda o, c, t: ((2 * o + 1) * N_CBLK + 2 * c + k, N_CHUNKS - 1 - fchunk(t), 0, 0))

    def bwd_tile_spec(k):
        return pl.BlockSpec((None, 1, TILE_ROWS, LANES),
                            lambda o, c, t: ((2 * o + 1) * N_CBLK + 2 * c + k, (N_TILES - CHUNK * fchunk(t)) % N_TILES, 0, 0))

    return pl.pallas_call(
        _filter_spec_kernel,
        grid=grid,
        in_specs=[fwd_spec(0), fwd_spec(1), bwd_spec(0), bwd_spec(1), bwd_tile_spec(0), bwd_tile_spec(1),
                  pl.BlockSpec((None, 1, 2 * LANES), lambda o, c, t: (o, 0, c)),
                  const((N_TILES, 128, 128)), const((2, 256, 256))],
        out_specs=pl.BlockSpec((None, 2, SPEC_SLOTS_PER_STEP, 256, LANES),
                               lambda o, c, t: (o, c, jnp.maximum(t - N_CHUNKS, 0), 0, 0)),
        out_shape=jax.ShapeDtypeStruct((2, N_CBLK, N_SLOTS, 256, LANES), BF16),
        scratch_shapes=[pltpu.VMEM((2, N_TILES * A_STRIDE, LANES), jnp.uint32)],
        compiler_params=pltpu.CompilerParams(
            dimension_semantics=("arbitrary", "arbitrary", "arbitrary"), vmem_limit_bytes=VMEM_LIMIT),
        name="filter_spec",
    )(taps, taps, taps, taps, taps, taps, hyena_d3, f1f, gf)


_T_S1 = 0
_T_F0 = 1
_T_M = 2
_T_F1 = 3
_T_E = 4
_T_END = 5
N_SUB = N_TILES // HY_SUB


def _hyena_kernel(cin_ref, pz_ref, cw_ref, k_ref, f1_ref, f1i_ref, gf_ref, gb_ref, o_ref, a_scr, z_scr):
    t = pl.program_id(2)

    def stage1(n2, u_bf):
        _stage_store(a_scr, _tile_rows(n2), jnp.dot(f1_ref[n2], u_bf, preferred_element_type=F32))

    def inv_stage1(n2):
        return jnp.dot(f1i_ref[n2], _stage_load(a_scr, _tile_rows(n2)), preferred_element_type=F32)

    def short_conv(sb, row0):
        base = sb * HY_SUB
        first = _pair(cin_ref, (base + N_TILES - 1) % N_TILES)
        first = jnp.where(sb == 0, _shift_down(first), first)
        last = _pair(cin_ref, (base + HY_SUB) % N_TILES)
        last = jnp.where(sb == N_SUB - 1, _shift_up(last), last)
        tiles = [first] + [_pair(cin_ref, base + i) for i in range(HY_SUB)] + [last]
        w = cw_ref[row0:row0 + 4, :]
        w0, w1, w2, b = _dup(w[0:1]), _dup(w[1:2]), _dup(w[2:3]), _dup(w[3:4])
        return [tiles[i] * w0 + tiles[i + 1] * w1 + tiles[i + 2] * w2 + b for i in range(HY_SUB)]

    def filter_multiply(slots, ys, first, buf):
        for j, (s, y) in enumerate(zip(slots, ys)):
            blocks = []
            for i, (yr, yi) in enumerate(_re_im_blocks(y)):
                kblk = k_ref[s, 16 * i:16 * i + 16, :].astype(F32)
                kr, ki = kblk[:8], kblk[8:]
                if first and j == 0 and i == 0:
                    rows = lax.broadcasted_iota(jnp.int32, kr.shape, 0)
                    ka, kb, kd = kr, jnp.where(rows == 0, 0.0, ki), jnp.where(rows == 0, ki, kr)
                else:
                    ka, kb, kd = kr, ki, kr
                ka, kb, kd = _dup(ka), _dup(kb), _dup(kd)
                blocks += [yr * ka - yi * kb, yr * kb + yi * kd]
            z_scr[buf, j] = jnp.concatenate(blocks, axis=0).astype(BF16)

    def stage2_inverse(slots, first_sel, buf):
        bms = [jnp.dot(gb_ref[first_sel] if j == 0 else gb_ref[1], z_scr[buf, j], preferred_element_type=F32)
               for j in range(SLOT_BATCH)]
        for s, bm in zip(slots, bms):
            _stage_store(a_scr, _slot_rows(s), bm)

    def spectral_phase():
        slots0 = _batch_slots(0)
        filter_multiply(slots0, _stage2_forward(a_scr, gf_ref, slots0, 0), True, 0)

        def loop(i, carry):
            slots = _batch_slots(i)
            ys = _stage2_forward(a_scr, gf_ref, slots, 1)
            stage2_inverse(_batch_slots(i - 1), jnp.where(i == 1, 0, 1), (i - 1) % 2)
            filter_multiply(slots, ys, False, i % 2)
            return carry

        lax.fori_loop(1, N_BATCHES, loop, 0)
        stage2_inverse(_batch_slots(N_BATCHES - 1), 1, (N_BATCHES - 1) % 2)

    @pl.when(t == _T_S1)
    def _():
        def sub(sb, carry):
            vs = short_conv(sb, 0)
            for i in range(HY_SUB):
                stage1(sb * HY_SUB + i, vs[i].astype(BF16))
            return carry

        lax.fori_loop(0, N_SUB, sub, 0)

    @pl.when((t == _T_F0) | (t == _T_F1))
    def _():
        spectral_phase()

    @pl.when(t == _T_M)
    def _():
        def sub(sb, carry):
            gates = short_conv(sb, 4)
            ys = [inv_stage1(sb * HY_SUB + i) for i in range(HY_SUB)]
            us = [(gates[i] * ys[i]).astype(BF16) for i in range(HY_SUB)]
            for i in range(HY_SUB):
                stage1(sb * HY_SUB + i, us[i])
            return carry

        lax.fori_loop(0, N_SUB, sub, 0)

    @pl.when(t == _T_E)
    def _():
        def sub(sb, carry):
            gates = short_conv(sb, 8)
            for i in range(HY_SUB):
                n2 = sb * HY_SUB + i
                res = (gates[i] * inv_stage1(n2) * _silu(_pair(pz_ref, n2))).astype(BF16)
                o_ref[0, n2] = res[:, :LANES]
                o_ref[1, n2] = res[:, LANES:]
            return carry

        lax.fori_loop(0, N_SUB, sub, 0)


def _hyena(p5, conv_w, conv_b, kspec, f1, f1i, gf, gb):
    grid = (N_CBLK, BATCH // 2, _T_END)
    cw = jnp.concatenate(
        [jnp.concatenate([conv_w[:, k * HYENA_WIDTH:(k + 1) * HYENA_WIDTH],
                          conv_b[:, k * HYENA_WIDTH:(k + 1) * HYENA_WIDTH]], axis=0) for k in range(3)], axis=0)
    seq_block = (2, None, N_TILES, TILE_ROWS, LANES)

    def conv_in_map(c, b, t):
        return (b, jnp.where(t < _T_M, 0, jnp.where(t < _T_E, 1, 2)) * N_CBLK + c, 0, 0, 0)

    def z_map(c, b, t):
        flat = c * (BATCH // 2) + b
        sel = jnp.where(t >= _T_M, flat, jnp.maximum(flat - 1, 0))
        return (sel % (BATCH // 2), 3 * N_CBLK + sel // (BATCH // 2), 0, 0, 0)

    const = lambda shape: pl.BlockSpec(shape, lambda c, b, t: (0,) * len(shape), pipeline_mode=pl.Buffered(1))
    in_specs = [
        pl.BlockSpec(seq_block, conv_in_map),
        pl.BlockSpec(seq_block, z_map),
        pl.BlockSpec((12, LANES), lambda c, b, t: (0, c)),
        pl.BlockSpec((None, None, N_SLOTS, 256, LANES), lambda c, b, t: (jnp.where(t >= _T_M, 1, 0), c, 0, 0, 0)),
        const((N_TILES, 128, TILE_ROWS)), const((N_TILES, TILE_ROWS, 128)),
        const((2, 256, 256)), const((2, 256, 256)),
    ]
    return pl.pallas_call(
        _hyena_kernel,
        grid=grid,
        in_specs=in_specs,
        out_specs=pl.BlockSpec(seq_block, lambda c, b, t: (b, c, 0, 0, 0)),
        out_shape=jax.ShapeDtypeStruct((BATCH, N_CBLK, N_TILES, TILE_ROWS, LANES), BF16),
        scratch_shapes=[pltpu.VMEM((2, N_TILES * A_STRIDE, LANES), jnp.uint32),
                        pltpu.VMEM((2, SLOT_BATCH, 256, 2 * LANES), BF16)],
        compiler_params=pltpu.CompilerParams(
            dimension_semantics=("arbitrary", "arbitrary", "arbitrary"), vmem_limit_bytes=VMEM_LIMIT),
        name="hyena",
    )(p5, p5, cw, kspec, f1, f1i, gf, gb)


_POOL_GROUP = 8


def _pool_kernel(u_ref, z_ref, pw_ref, ps_ref, o_ref):
    g = pl.program_id(1)
    n_groups = N_TILES // _POOL_GROUP
    n1 = lax.broadcasted_iota(jnp.int32, (TILE_ROWS, LANES), 0)

    def tile(idx):
        if isinstance(idx, int) and idx < 0:
            return _shift_down(u_ref[idx + N_TILES].astype(F32))
        if isinstance(idx, int) and idx >= N_TILES:
            return _shift_up(u_ref[idx - N_TILES].astype(F32))
        return u_ref[idx].astype(F32)

    for gi_, w in enumerate(POOL_WINDOWS):
        @pl.when(g == gi_)
        def _(w=w):
            lo, hi = w // 2, w - 1 - w // 2

            def run_group(base, window_sum, edge):
                pooled = []
                for i in range(_POOL_GROUP):
                    n2 = base + i
                    if i > 0:
                        window_sum = window_sum + tile(n2 + hi) - tile(n2 - 1 - lo)
                    if edge:
                        pos = 128 * n1 + n2
                        cnt = (jnp.minimum(pos + hi, SEQ - 1) - jnp.maximum(pos - lo, 0) + 1).astype(F32)
                        mean = window_sum / cnt
                    else:
                        mean = window_sum * (1.0 / w)
                    pooled.append((mean - tile(n2)).astype(BF16))
                pooled = jnp.concatenate(pooled, axis=0)
                y = jnp.dot(pooled, pw_ref[...], preferred_element_type=F32) * ps_ref[...]
                for i in range(_POOL_GROUP):
                    n2 = base + i
                    o_ref[n2] = (y[i * TILE_ROWS:(i + 1) * TILE_ROWS] * _silu(z_ref[n2].astype(F32))).astype(BF16)
                nxt = base + _POOL_GROUP
                return window_sum + tile(nxt + hi) - tile(nxt - 1 - lo)

            first = tile(-lo)
            for d in range(-lo + 1, hi + 1):
                first = first + tile(d)
            carry = run_group(0, first, True)
            carry = lax.fori_loop(1, n_groups - 1,
                                  lambda c, ws: run_group(c * _POOL_GROUP, ws, False), carry)
            run_group(N_TILES - _POOL_GROUP, carry, True)


def _pool(p5, pool_w_bf, pool_scale):
    u0 = 4 * N_CBLK
    z0 = 5 * N_CBLK
    return pl.pallas_call(
        _pool_kernel,
        grid=(BATCH, len(POOL_WINDOWS)),
        in_specs=[
            pl.BlockSpec((None, None, N_TILES, TILE_ROWS, LANES), lambda b, g: (b, u0 + g, 0, 0, 0)),
            pl.BlockSpec((None, None, N_TILES, TILE_ROWS, LANES), lambda b, g: (b, z0 + g, 0, 0, 0)),
            pl.BlockSpec((None, LANES, LANES), lambda b, g: (g, 0, 0)),
            pl.BlockSpec((1, LANES), lambda b, g: (0, g)),
        ],
        out_specs=pl.BlockSpec((None, None, N_TILES, TILE_ROWS, LANES), lambda b, g: (b, g, 0, 0, 0)),
        out_shape=jax.ShapeDtypeStruct((BATCH, N_CBLK, N_TILES, TILE_ROWS, LANES), BF16),
        compiler_params=pltpu.CompilerParams(
            dimension_semantics=("arbitrary", "arbitrary"), vmem_limit_bytes=VMEM_LIMIT),
        name="pool",
    )(p5, p5, pool_w_bf, pool_scale)


def _out_kernel(yh_ref, yp_ref, x_ref, gh_ref, gp_ref, w_ref, gpost_ref, o_ref, r_scr):
    def group_norm(ref, gain_ref, j0):
        rows = []
        for j in range(j0, j0 + SUB_TILES):
            rows.append(jnp.concatenate([ref[cb, j] for cb in range(N_CBLK)], axis=-1).astype(F32))
        y = jnp.concatenate(rows, axis=0)
        ms = jnp.mean(y * y, axis=-1, keepdims=True)
        return (y * lax.rsqrt(ms + EPS) * gain_ref[...]).astype(BF16)

    for sb in range(IN_TILES // SUB_TILES):
        j0 = sb * SUB_TILES
        yc = jnp.concatenate([group_norm(yh_ref, gh_ref, j0), group_norm(yp_ref, gp_ref, j0)], axis=-1)
        out = jnp.dot(yc, w_ref[...], preferred_element_type=F32)
        ms = jnp.mean(out * out, axis=-1, keepdims=True)
        out = out * lax.rsqrt(ms + EPS) * gpost_ref[...]
        for j in range(SUB_TILES):
            for k in range(D_MODEL // LANES):
                r_scr[sb, k, pl.ds(j, TILE_ROWS, stride=SUB_TILES), :] = out[j * TILE_ROWS:(j + 1) * TILE_ROWS,
                                                                             k * LANES:(k + 1) * LANES]
        r = jnp.concatenate([r_scr[sb, k] for k in range(D_MODEL // LANES)], axis=-1)
        o_ref[:, j0:j0 + SUB_TILES, :] = (x_ref[:, j0:j0 + SUB_TILES, :]
                                          + r.reshape(TILE_ROWS, SUB_TILES, D_MODEL))


def _out_proj(yh, yp, x4, norm_h_g, norm_p_g, w_out_bf, post_g):
    grid = (BATCH, N_TILES // IN_TILES)
    y_spec = pl.BlockSpec((None, N_CBLK, IN_TILES, TILE_ROWS, LANES), lambda b, g: (b, 0, g, 0, 0))
    x_spec = pl.BlockSpec((None, TILE_ROWS, IN_TILES, D_MODEL), lambda b, g: (b, 0, g, 0))
    return pl.pallas_call(
        _out_kernel,
        grid=grid,
        in_specs=[
            y_spec, y_spec, x_spec,
            pl.BlockSpec((1, HYENA_WIDTH), lambda b, g: (0, 0)),
            pl.BlockSpec((1, POOL_WIDTH), lambda b, g: (0, 0)),
            pl.BlockSpec((D_MODEL, D_MODEL), lambda b, g: (0, 0)),
            pl.BlockSpec((1, D_MODEL), lambda b, g: (0, 0)),
        ],
        out_specs=x_spec,
        out_shape=jax.ShapeDtypeStruct((BATCH, TILE_ROWS, N_TILES, D_MODEL), F32),
        scratch_shapes=[pltpu.VMEM((IN_TILES // SUB_TILES, D_MODEL // LANES, TILE_ROWS * SUB_TILES, LANES), F32)],
        compiler_params=pltpu.CompilerParams(
            dimension_semantics=("arbitrary", "arbitrary"), vmem_limit_bytes=VMEM_LIMIT),
        name="out_proj",
    )(yh, yp, x4, norm_h_g, norm_p_g, w_out_bf, post_g)


def kernel(x, pre_norm_g, w_in, conv_w, conv_b, filt_w1, filt_b1, filt_w2, filt_b2, filt_w3, filt_b3,
           filt_freq, filt_w_out, hyena_d, pool_w, pool_scale, norm_h_g, norm_p_g, w_out, post_norm_g):
    assert x.shape == (BATCH, SEQ, D_MODEL) and pre_norm_g.shape[0] == 1
    f1, f1i, gf, gb, f1f = (jnp.asarray(m, F32).astype(BF16) for m in (_F1, _F1I, _GF, _GB, _F1F))

    x4 = x.reshape(BATCH, TILE_ROWS, N_TILES, D_MODEL)
    p5 = _in_proj(x4, pre_norm_g, w_in[0].astype(BF16))

    taps = _filter_mlp(jnp.asarray(_ZFEAT), filt_w1[0], filt_b1[0], filt_w2[0], filt_b2[0], filt_w3[0], filt_b3[0],
                       filt_freq[0], filt_w_out[0], jnp.asarray(_ABS_DELTAS))
    kspec = _filter_spec(taps, hyena_d[0].reshape(2, 1, HYENA_WIDTH), f1f, gf)

    yh = _hyena(p5, conv_w[0], conv_b, kspec, f1, f1i, gf, gb)
    yp = _pool(p5, pool_w[0].astype(BF16), pool_scale)
    out4 = _out_proj(yh, yp, x4, norm_h_g, norm_p_g, w_out[0].astype(BF16), post_norm_g)
    return out4.reshape(BATCH, SEQ, D_MODEL)
```

```python
import functools
import math

import numpy as np
import jax
import jax.numpy as jnp
from jax import lax
from jax.experimental import pallas as pl
from jax.experimental.pallas import tpu as pltpu

F32 = jnp.float32
BF16 = jnp.bfloat16

D_MODEL = 1024
BATCH = 4
SEQ = 8192
HYENA_WIDTH = 512
POOL_WIDTH = 512
POOL_WINDOWS = (2, 4, 8, 16)
FILTER_EMB = 33
FILTER_BANDS = 16
FILTER_HIDDEN = 64
PROJ_WIDTH = 3072
EPS = 1e-6

LANES = 128
N_FFT = 2 * SEQ
N_TILES = 128
TILE_ROWS = SEQ // N_TILES
N_SLOTS = 64
CHUNK = 32
HY_SUB = 32
N_CHUNKS = N_TILES // CHUNK
A_STRIDE = 72
SLOT_BATCH = 4
N_CBLK = HYENA_WIDTH // LANES
PROJ_BLOCKS = PROJ_WIDTH // LANES
IN_TILES = 16
OUT_TILES = 32
SUB_TILES = 8
VMEM_LIMIT = 60 * 1024 * 1024


def _dft_tables():
    n1 = np.arange(TILE_ROWS)
    n2 = np.arange(N_TILES)
    s = np.arange(N_SLOTS)
    ph = 2 * np.pi * (n2[:, None, None] * s[None, :, None] / N_FFT
                      + n1[None, None, :] * s[None, :, None] / 128.0)
    f1 = np.zeros((N_TILES, 128, TILE_ROWS))
    f1[:, :64, :] = np.cos(ph)
    f1[:, 64:, :] = -np.sin(ph)
    f1[:, 0, :] = 1.0
    f1[:, 64, :] = (-1.0) ** n1
    php = np.transpose(ph, (0, 2, 1))
    f1i = np.zeros((N_TILES, TILE_ROWS, 128))
    f1i[:, :, :64] = 2 * np.cos(php) / N_FFT
    f1i[:, :, 64:] = -2 * np.sin(php) / N_FFT
    f1i[:, :, 0] = 1.0 / N_FFT
    f1i[:, :, 64] = ((-1.0) ** n1)[None, :] / N_FFT
    k2 = np.arange(128)
    th = 2 * np.pi * np.outer(k2, n2) / 128.0
    c, sn = np.cos(th), np.sin(th)
    g = np.block([[c, sn], [-sn, c]])
    gi = np.block([[c, -sn], [sn, c]])
    kk = np.arange(64)
    tha = 2 * np.pi * np.outer(kk, n2) / 128.0
    thb = 2 * np.pi * np.outer(64 + 128 * kk, n2) / N_FFT
    g0 = np.zeros((256, 256))
    g0[0:64, 0:128] = np.cos(tha)
    g0[64:128, 128:256] = np.cos(thb)
    g0[128:192, 0:128] = -np.sin(tha)
    g0[128, 0:128] = (-1.0) ** n2
    g0[192:256, 128:256] = -np.sin(thb)
    g0i = np.zeros((256, 256))
    g0i[0:128, 0:64] = 2 * np.cos(tha.T)
    g0i[0:128, 0] = 1.0
    g0i[0:128, 128:192] = -2 * np.sin(tha.T)
    g0i[0:128, 128] = (-1.0) ** n2
    g0i[128:256, 64:128] = 2 * np.cos(thb.T)
    g0i[128:256, 192:256] = -2 * np.sin(thb.T)
    q = np.arange(256)
    perm = np.where(q % 16 < 8, 8 * (q // 16) + q % 16, 128 + 8 * (q // 16) + q % 16 - 8)
    gf = np.stack([g0[perm, :], g[perm, :]])
    gb = np.stack([g0i[:, perm], gi[:, perm]])
    n1f = np.arange(128)
    phf = 2 * np.pi * (n2[:, None, None] * s[None, :, None] / N_FFT
                       + n1f[None, None, :] * s[None, :, None] / 128.0)
    full = np.zeros((N_TILES, 128, 128))
    full[:, :64, :] = np.cos(phf)
    full[:, 64:, :] = -np.sin(phf)
    full[:, 0, :] = 1.0
    full[:, 64, :] = (-1.0) ** n1f
    rev = full[:, :, 127:63:-1].copy()
    rev[0, :, 1:] = full[0, :, 127:64:-1]
    rev[0, :, 0] = 0.0
    f1f = np.concatenate([full[:, :, :64], rev], axis=2)
    pair64 = np.arange(128) % 2 * 64 + np.arange(128) // 2
    pair128 = np.arange(256) % 2 * 128 + np.arange(256) // 2
    f1, f1f = f1[:, pair64, :], f1f[:, pair64, :]
    f1i = f1i[:, :, pair64]
    gf = gf[:, :, pair128]
    gb = gb[:, pair128, :]
    return f1, f1i, gf, gb, f1f


def _filter_features():
    pos = np.arange(SEQ, dtype=np.float64)
    t = pos / (SEQ - 1)
    ang = 2.0 * math.pi * pos / SEQ
    bands = np.linspace(1e-4, FILTER_BANDS - 1, FILTER_BANDS)
    z = np.concatenate([t[:, None], np.cos(bands[None, :] * ang[:, None]),
                        -np.sin(bands[None, :] * ang[:, None])], axis=-1)
    z = z.reshape(TILE_ROWS, N_TILES, FILTER_EMB).transpose(1, 0, 2).reshape(SEQ, FILTER_EMB)
    zp = np.zeros((FILTER_HIDDEN, SEQ))
    zp[:FILTER_EMB, :] = z.T
    max_decay = math.log(1e-2) / 0.3
    min_decay = math.log(1e-2) / 1.5
    deltas = np.abs(np.linspace(min_decay, max_decay, HYENA_WIDTH))
    return zp.astype(np.float32), deltas.astype(np.float32)[None, :]


_F1, _F1I, _GF, _GB, _F1F = _dft_tables()
_ZFEAT, _ABS_DELTAS = _filter_features()


def _shift_down(x):
    rows = lax.broadcasted_iota(jnp.int32, x.shape, 0)
    return jnp.where(rows == 0, 0.0, pltpu.roll(x, 1, axis=0))


def _shift_up(x):
    rows = lax.broadcasted_iota(jnp.int32, x.shape, 0)
    return jnp.where(rows == x.shape[0] - 1, 0.0, pltpu.roll(x, x.shape[0] - 1, axis=0))


def _pair(ref, i):
    return jnp.concatenate([ref[0, i], ref[1, i]], axis=-1).astype(F32)


def _dup(x):
    return jnp.concatenate([x, x], axis=-1)


def _silu(z):
    hz = 0.5 * z
    return hz * (1.0 + jnp.tanh(hz))


def _tile_rows(n2):
    return pl.ds(pl.multiple_of(n2 * A_STRIDE, 8), N_SLOTS)


def _slot_rows(s):
    return pl.ds(s, N_TILES, stride=A_STRIDE)


def _stage_load(a_ref, rows):
    words = jnp.concatenate([a_ref[0, rows, :], a_ref[1, rows, :]], axis=-1)
    return pltpu.bitcast(words, BF16)


def _stage_store(a_ref, rows, val):
    words = pltpu.bitcast(val.astype(BF16), jnp.uint32)
    a_ref[0, rows, :] = words[:, :LANES]
    a_ref[1, rows, :] = words[:, LANES:]


N_BATCHES = N_SLOTS // SLOT_BATCH
SPEC_BLOCKS = 16


def _batch_slots(i):
    return [i * SLOT_BATCH + j for j in range(SLOT_BATCH)]


def _stage2_forward(a_ref, gf_ref, slots, sel0):
    xs = [_stage_load(a_ref, _slot_rows(s)) for s in slots]
    return [jnp.dot(gf_ref[sel0] if j == 0 else gf_ref[1], x, preferred_element_type=F32)
            for j, x in enumerate(xs)]


def _re_im_blocks(y):
    return [(y[16 * i:16 * i + 8], y[16 * i + 8:16 * i + 16]) for i in range(SPEC_BLOCKS)]


def _in_proj_kernel(x_ref, g_ref, w_ref, o_ref, h_scr):
    ncol = 512
    for sb in range(IN_TILES // SUB_TILES):
        j0 = sb * SUB_TILES
        x = x_ref[:, j0:j0 + SUB_TILES, :].reshape(TILE_ROWS * SUB_TILES, D_MODEL)
        ms = jnp.mean(x * x, axis=-1, keepdims=True)
        hn = x * lax.rsqrt(ms + EPS) * g_ref[...]
        for k in range(D_MODEL // LANES):
            h_scr[sb, k] = hn[:, k * LANES:(k + 1) * LANES]
        h = jnp.concatenate(
            [jnp.concatenate([h_scr[sb, k, pl.ds(j, TILE_ROWS, stride=SUB_TILES), :]
                              for k in range(D_MODEL // LANES)], axis=-1).astype(BF16)
             for j in range(SUB_TILES)], axis=0)
        for c in range(PROJ_WIDTH // ncol):
            p = jnp.dot(h, w_ref[:, c * ncol:(c + 1) * ncol], preferred_element_type=F32).astype(BF16)
            for cb in range(ncol // LANES):
                for j in range(SUB_TILES):
                    o_ref[c * (ncol // LANES) + cb, j0 + j] = p[j * TILE_ROWS:(j + 1) * TILE_ROWS,
                                                                cb * LANES:(cb + 1) * LANES]


def _in_proj(x4, pre_g, w_in_bf):
    grid = (BATCH, N_TILES // IN_TILES)
    return pl.pallas_call(
        _in_proj_kernel,
        grid=grid,
        in_specs=[
            pl.BlockSpec((None, TILE_ROWS, IN_TILES, D_MODEL), lambda b, g: (b, 0, g, 0)),
            pl.BlockSpec((1, D_MODEL), lambda b, g: (0, 0)),
            pl.BlockSpec((D_MODEL, PROJ_WIDTH), lambda b, g: (0, 0)),
        ],
        out_specs=pl.BlockSpec((None, PROJ_BLOCKS, IN_TILES, TILE_ROWS, LANES), lambda b, g: (b, 0, g, 0, 0)),
        out_shape=jax.ShapeDtypeStruct((BATCH, PROJ_BLOCKS, N_TILES, TILE_ROWS, LANES), BF16),
        scratch_shapes=[pltpu.VMEM((IN_TILES // SUB_TILES, D_MODEL // LANES, TILE_ROWS * SUB_TILES, LANES), F32)],
        compiler_params=pltpu.CompilerParams(
            dimension_semantics=("arbitrary", "arbitrary"), vmem_limit_bytes=VMEM_LIMIT),
        name="in_proj",
    )(x4, pre_g, w_in_bf)


def _split_bf16(x):
    hi = x.astype(BF16)
    return hi, (x - hi.astype(F32)).astype(BF16)


def _stack_weight_rows(w):
    hi, lo = _split_bf16(w)
    return jnp.concatenate([hi, lo, hi], axis=1)


def _filter_mlp_kernel(z_ref, w1_ref, b1_ref, w2_ref, b2_ref, w3_ref, b3_ref, fr_ref, fr3_ref,
                       wp2_ref, wpl_ref, dl_ref, o_ref):
    def layer(w_ref, b_ref, f_ref, h):
        hi, lo = _split_bf16(h)
        pre = jnp.dot(w_ref[...], jnp.concatenate([hi, hi, lo], axis=0), preferred_element_type=F32)
        return jnp.sin(f_ref[...] * (pre + b_ref[...]))

    h = layer(w1_ref, b1_ref, fr_ref, z_ref[...])
    h = layer(w2_ref, b2_ref, fr_ref, h)
    h = layer(w3_ref, b3_ref, fr3_ref, h)
    hi, lo = _split_bf16(h.T)
    hs = jnp.concatenate([hi, lo], axis=1)
    rows = CHUNK * TILE_ROWS
    r = lax.broadcasted_iota(jnp.int32, (rows, HYENA_WIDTH), 0)
    pos = 128 * (r % TILE_ROWS) + CHUNK * pl.program_id(0) + r // TILE_ROWS
    t = pos.astype(F32) / float(SEQ - 1)
    decay = jnp.exp(-t * dl_ref[...])
    for od in range(4):
        cols = slice(od * HYENA_WIDTH, (od + 1) * HYENA_WIDTH)
        taps = (jnp.dot(hs, wp2_ref[:, cols], preferred_element_type=F32)
                + jnp.dot(hi, wpl_ref[:, cols], preferred_element_type=F32)) * decay
        taps = taps.astype(BF16)
        for cb in range(N_CBLK):
            for i in range(CHUNK):
                o_ref[od * N_CBLK + cb, i] = taps[i * TILE_ROWS:(i + 1) * TILE_ROWS, cb * LANES:(cb + 1) * LANES]


def _filter_mlp(zfeat_t, w1, b1, w2, b2, w3, b3, freq, w_proj, abs_deltas):
    hid = FILTER_HIDDEN
    col = lambda v, n: jnp.pad(v.reshape(-1, 1), ((0, n - v.size), (0, 0)))
    w1s = _stack_weight_rows(jnp.pad(w1, ((0, hid - FILTER_EMB), (0, 0))).T)
    w2s = _stack_weight_rows(w2.T)
    w3s = _stack_weight_rows(jnp.pad(w3.T, ((0, LANES - hid), (0, 0))))
    wp_hi, wp_lo = _split_bf16(jnp.pad(w_proj, ((0, LANES - hid), (0, 0))))
    wp2 = jnp.concatenate([wp_hi, wp_hi], axis=0)
    full = lambda shape: pl.BlockSpec(shape, lambda c: (0,) * len(shape))
    cols = CHUNK * TILE_ROWS
    return pl.pallas_call(
        _filter_mlp_kernel,
        grid=(N_CHUNKS,),
        in_specs=[
            pl.BlockSpec((hid, cols), lambda c: (0, c)),
            full((hid, 3 * hid)), full((hid, 1)),
            full((hid, 3 * hid)), full((hid, 1)),
            full((LANES, 3 * hid)), full((LANES, 1)),
            full((hid, 1)), full((LANES, 1)),
            full((2 * LANES, 4 * HYENA_WIDTH)), full((LANES, 4 * HYENA_WIDTH)),
            full((1, HYENA_WIDTH)),
        ],
        out_specs=pl.BlockSpec((4 * N_CBLK, CHUNK, TILE_ROWS, LANES), lambda c: (0, c, 0, 0)),
        out_shape=jax.ShapeDtypeStruct((4 * N_CBLK, N_TILES, TILE_ROWS, LANES), BF16),
        compiler_params=pltpu.CompilerParams(dimension_semantics=("arbitrary",), vmem_limit_bytes=VMEM_LIMIT),
        name="filter_mlp",
    )(zfeat_t, w1s, col(b1, hid), w2s, col(b2, hid), w3s, col(b3, LANES), col(freq, hid), col(freq, LANES),
      wp2, wp_lo, abs_deltas)


SPEC_SLOTS_PER_STEP = 16
N_SPEC_STEPS = N_SLOTS // SPEC_SLOTS_PER_STEP


def _filter_spec_kernel(fa_ref, fb_ref, ba_ref, bb_ref, bxa_ref, bxb_ref, d_ref, f1f_ref, gf_ref, k_ref, a_scr):
    t = pl.program_id(2)

    @pl.when(t < N_CHUNKS)
    def _():
        for i in range(CHUNK):
            n2 = t * CHUNK + i
            fwd = jnp.concatenate([fa_ref[i], fb_ref[i]], axis=-1)
            if i == 0:
                bwd = jnp.concatenate([bxa_ref[0], bxb_ref[0]], axis=-1)
            else:
                bwd = jnp.concatenate([ba_ref[CHUNK - i], bb_ref[CHUNK - i]], axis=-1)
            taps = jnp.concatenate([fwd, bwd], axis=0)
            _stage_store(a_scr, _tile_rows(n2), jnp.dot(f1f_ref[n2], taps, preferred_element_type=F32))

    @pl.when(t >= N_CHUNKS)
    def _():
        q = t - N_CHUNKS
        for bi in range(SPEC_SLOTS_PER_STEP // SLOT_BATCH):
            base = q * SPEC_SLOTS_PER_STEP + bi * SLOT_BATCH
            sel0 = jnp.where(q == 0, 0, 1) if bi == 0 else 1
            ys = _stage2_forward(a_scr, gf_ref, [base + j for j in range(SLOT_BATCH)], sel0)
            d = d_ref[...]
            for j, y in enumerate(ys):
                blocks = []
                for i, (yr, yi) in enumerate(_re_im_blocks(y)):
                    yr = yr + d
                    if bi == 0 and j == 0 and i == 0:
                        rows = lax.broadcasted_iota(jnp.int32, yi.shape, 0)
                        yi = yi + jnp.where((rows == 0) & (q == 0), d, 0.0)
                    blocks += [yr, yi]
                spec = jnp.concatenate(blocks, axis=0).astype(BF16)
                k_ref[0, bi * SLOT_BATCH + j] = spec[:, :LANES]
                k_ref[1, bi * SLOT_BATCH + j] = spec[:, LANES:]


def _filter_spec(taps, hyena_d3, f1f, gf):
    grid = (2, N_CBLK // 2, N_CHUNKS + N_SPEC_STEPS)
    fchunk = lambda t: jnp.minimum(t, N_CHUNKS - 1)
    const = lambda shape: pl.BlockSpec(shape, lambda o, c, t: (0,) * len(shape), pipeline_mode=pl.Buffered(1))

    def fwd_spec(k):
        return pl.BlockSpec((None, CHUNK, TILE_ROWS, LANES),
                            lambda o, c, t: ((2 * o) * N_CBLK + 2 * c + k, fchunk(t), 0, 0))

    def bwd_spec(k):
        return pl.BlockSpec((None, CHUNK, TILE_ROWS, LANES),
                            lambda o, c, t: ((2 * o + 1) * N_CBLK + 2 * c + k, N_CHUNKS - 1 - fchunk(t), 0, 0))

    def bwd_tile_spec(k):
        return pl.BlockSpec((None, 1, TILE_ROWS, LANES),
                            lambda o, c, t: ((2 * o + 1) * N_CBLK + 2 * c + k, (N_TILES - CHUNK * fchunk(t)) % N_TILES, 0, 0))

    return pl.pallas_call(
        _filter_spec_kernel,
        grid=grid,
        in_specs=[fwd_spec(0), fwd_spec(1), bwd_spec(0), bwd_spec(1), bwd_tile_spec(0), bwd_tile_spec(1),
                  pl.BlockSpec((None, 1, 2 * LANES), lambda o, c, t: (o, 0, c)),
                  const((N_TILES, 128, 128)), const((2, 256, 256))],
        out_specs=pl.BlockSpec((None, 2, SPEC_SLOTS_PER_STEP, 256, LANES),
                               lambda o, c, t: (o, c, jnp.maximum(t - N_CHUNKS, 0), 0, 0)),
        out_shape=jax.ShapeDtypeStruct((2, N_CBLK, N_SLOTS, 256, LANES), BF16),
        scratch_shapes=[pltpu.VMEM((2, N_TILES * A_STRIDE, LANES), jnp.uint32)],
        compiler_params=pltpu.CompilerParams(
            dimension_semantics=("arbitrary", "arbitrary", "arbitrary"), vmem_limit_bytes=VMEM_LIMIT),
        name="filter_spec",
    )(taps, taps, taps, taps, taps, taps, hyena_d3, f1f, gf)


_T_S1 = 0
_T_F0 = 1
_T_M = 2
_T_F1 = 3
_T_E = 4
_T_END = 5
N_SUB = N_TILES // HY_SUB


def _hyena_kernel(cin_ref, pz_ref, cw_ref, k_ref, f1_ref, f1i_ref, gf_ref, gb_ref, o_ref, a_scr, z_scr):
    t = pl.program_id(2)

    def stage1(n2, u_bf):
        _stage_store(a_scr, _tile_rows(n2), jnp.dot(f1_ref[n2], u_bf, preferred_element_type=F32))

    def inv_stage1(n2):
        return jnp.dot(f1i_ref[n2], _stage_load(a_scr, _tile_rows(n2)), preferred_element_type=F32)

    def short_conv(sb, row0):
        base = sb * HY_SUB
        first = _pair(cin_ref, (base + N_TILES - 1) % N_TILES)
        first = jnp.where(sb == 0, _shift_down(first), first)
        last = _pair(cin_ref, (base + HY_SUB) % N_TILES)
        last = jnp.where(sb == N_SUB - 1, _shift_up(last), last)
        tiles = [first] + [_pair(cin_ref, base + i) for i in range(HY_SUB)] + [last]
        w = cw_ref[row0:row0 + 4, :]
        w0, w1, w2, b = _dup(w[0:1]), _dup(w[1:2]), _dup(w[2:3]), _dup(w[3:4])
        return [tiles[i] * w0 + tiles[i + 1] * w1 + tiles[i + 2] * w2 + b for i in range(HY_SUB)]

    def filter_multiply(slots, ys, first, buf):
        for j, (s, y) in enumerate(zip(slots, ys)):
            blocks = []
            for i, (yr, yi) in enumerate(_re_im_blocks(y)):
                kblk = k_ref[s, 16 * i:16 * i + 16, :].astype(F32)
                kr, ki = kblk[:8], kblk[8:]
                if first and j == 0 and i == 0:
                    rows = lax.broadcasted_iota(jnp.int32, kr.shape, 0)
                    ka, kb, kd = kr, jnp.where(rows == 0, 0.0, ki), jnp.where(rows == 0, ki, kr)
                else:
                    ka, kb, kd = kr, ki, kr
                ka, kb, kd = _dup(ka), _dup(kb), _dup(kd)
                blocks += [yr * ka - yi * kb, yr * kb + yi * kd]
            z_scr[buf, j] = jnp.concatenate(blocks, axis=0).astype(BF16)

    def stage2_inverse(slots, first_sel, buf):
        bms = [jnp.dot(gb_ref[first_sel] if j == 0 else gb_ref[1], z_scr[buf, j], preferred_element_type=F32)
               for j in range(SLOT_BATCH)]
        for s, bm in zip(slots, bms):
            _stage_store(a_scr, _slot_rows(s), bm)

    def spectral_phase():
        slots0 = _batch_slots(0)
        filter_multiply(slots0, _stage2_forward(a_scr, gf_ref, slots0, 0), True, 0)

        def loop(i, carry):
            slots = _batch_slots(i)
            ys = _stage2_forward(a_scr, gf_ref, slots, 1)
            stage2_inverse(_batch_slots(i - 1), jnp.where(i == 1, 0, 1), (i - 1) % 2)
            filter_multiply(slots, ys, False, i % 2)
            return carry

        lax.fori_loop(1, N_BATCHES, loop, 0, unroll=5)
        stage2_inverse(_batch_slots(N_BATCHES - 1), 1, (N_BATCHES - 1) % 2)

    @pl.when(t == _T_S1)
    def _():
        def sub(sb, carry):
            vs = short_conv(sb, 0)
            for i in range(HY_SUB):
                stage1(sb * HY_SUB + i, vs[i].astype(BF16))
            return carry

        lax.fori_loop(0, N_SUB, sub, 0)

    @pl.when((t == _T_F0) | (t == _T_F1))
    def _():
        spectral_phase()

    @pl.when(t == _T_M)
    def _():
        def sub(sb, carry):
            gates = short_conv(sb, 4)
            ys = [inv_stage1(sb * HY_SUB + i) for i in range(HY_SUB)]
            us = [(gates[i] * ys[i]).astype(BF16) for i in range(HY_SUB)]
            for i in range(HY_SUB):
                stage1(sb * HY_SUB + i, us[i])
            return carry

        lax.fori_loop(0, N_SUB, sub, 0)

    @pl.when(t == _T_E)
    def _():
        def sub(sb, carry):
            gates = short_conv(sb, 8)
            for i in range(HY_SUB):
                n2 = sb * HY_SUB + i
                res = (gates[i] * inv_stage1(n2) * _silu(_pair(pz_ref, n2))).astype(BF16)
                o_ref[0, n2] = res[:, :LANES]
                o_ref[1, n2] = res[:, LANES:]
            return carry

        lax.fori_loop(0, N_SUB, sub, 0)


def _hyena(p5, conv_w, conv_b, kspec, f1, f1i, gf, gb):
    grid = (N_CBLK, BATCH // 2, _T_END)
    cw = jnp.concatenate(
        [jnp.concatenate([conv_w[:, k * HYENA_WIDTH:(k + 1) * HYENA_WIDTH],
                          conv_b[:, k * HYENA_WIDTH:(k + 1) * HYENA_WIDTH]], axis=0) for k in range(3)], axis=0)
    seq_block = (2, None, N_TILES, TILE_ROWS, LANES)

    def conv_in_map(c, b, t):
        return (b, jnp.where(t < _T_M, 0, jnp.where(t < _T_E, 1, 2)) * N_CBLK + c, 0, 0, 0)

    def z_map(c, b, t):
        flat = c * (BATCH // 2) + b
        sel = jnp.where(t >= _T_M, flat, jnp.maximum(flat - 1, 0))
        return (sel % (BATCH // 2), 3 * N_CBLK + sel // (BATCH // 2), 0, 0, 0)

    const = lambda shape: pl.BlockSpec(shape, lambda c, b, t: (0,) * len(shape), pipeline_mode=pl.Buffered(1))
    in_specs = [
        pl.BlockSpec(seq_block, conv_in_map),
        pl.BlockSpec(seq_block, z_map),
        pl.BlockSpec((12, LANES), lambda c, b, t: (0, c)),
        pl.BlockSpec((None, None, N_SLOTS, 256, LANES), lambda c, b, t: (jnp.where(t >= _T_M, 1, 0), c, 0, 0, 0)),
        const((N_TILES, 128, TILE_ROWS)), const((N_TILES, TILE_ROWS, 128)),
        const((2, 256, 256)), const((2, 256, 256)),
    ]
    return pl.pallas_call(
        _hyena_kernel,
        grid=grid,
        in_specs=in_specs,
        out_specs=pl.BlockSpec(seq_block, lambda c, b, t: (b, c, 0, 0, 0)),
        out_shape=jax.ShapeDtypeStruct((BATCH, N_CBLK, N_TILES, TILE_ROWS, LANES), BF16),
        scratch_shapes=[pltpu.VMEM((2, N_TILES * A_STRIDE, LANES), jnp.uint32),
                        pltpu.VMEM((2, SLOT_BATCH, 256, 2 * LANES), BF16)],
        compiler_params=pltpu.CompilerParams(
            dimension_semantics=("arbitrary", "arbitrary", "arbitrary"), vmem_limit_bytes=VMEM_LIMIT),
        name="hyena",
    )(p5, p5, cw, kspec, f1, f1i, gf, gb)


_POOL_GROUP = 8


def _pool_kernel(u_ref, z_ref, pw_ref, ps_ref, o_ref):
    g = pl.program_id(1)
    n_groups = N_TILES // _POOL_GROUP
    n1 = lax.broadcasted_iota(jnp.int32, (TILE_ROWS, LANES), 0)

    def tile(idx):
        if isinstance(idx, int) and idx < 0:
            return _shift_down(u_ref[idx + N_TILES].astype(F32))
        if isinstance(idx, int) and idx >= N_TILES:
            return _shift_up(u_ref[idx - N_TILES].astype(F32))
        return u_ref[idx].astype(F32)

    for gi_, w in enumerate(POOL_WINDOWS):
        @pl.when(g == gi_)
        def _(w=w):
            lo, hi = w // 2, w - 1 - w // 2

            def run_group(base, window_sum, edge):
                pooled = []
                for i in range(_POOL_GROUP):
                    n2 = base + i
                    if i > 0:
                        window_sum = window_sum + tile(n2 + hi) - tile(n2 - 1 - lo)
                    if edge:
                        pos = 128 * n1 + n2
                        cnt = (jnp.minimum(pos + hi, SEQ - 1) - jnp.maximum(pos - lo, 0) + 1).astype(F32)
                        mean = window_sum / cnt
                    else:
                        mean = window_sum * (1.0 / w)
                    pooled.append((mean - tile(n2)).astype(BF16))
                pooled = jnp.concatenate(pooled, axis=0)
                y = jnp.dot(pooled, pw_ref[...], preferred_element_type=F32) * ps_ref[...]
                for i in range(_POOL_GROUP):
                    n2 = base + i
                    o_ref[n2] = (y[i * TILE_ROWS:(i + 1) * TILE_ROWS] * _silu(z_ref[n2].astype(F32))).astype(BF16)
                nxt = base + _POOL_GROUP
                return window_sum + tile(nxt + hi) - tile(nxt - 1 - lo)

            first = tile(-lo)
            for d in range(-lo + 1, hi + 1):
                first = first + tile(d)
            carry = run_group(0, first, True)
            carry = lax.fori_loop(1, n_groups - 1,
                                  lambda c, ws: run_group(c * _POOL_GROUP, ws, False), carry)
            run_group(N_TILES - _POOL_GROUP, carry, True)


def _pool(p5, pool_w_bf, pool_scale):
    u0 = 4 * N_CBLK
    z0 = 5 * N_CBLK
    return pl.pallas_call(
        _pool_kernel,
        grid=(BATCH, len(POOL_WINDOWS)),
        in_specs=[
            pl.BlockSpec((None, None, N_TILES, TILE_ROWS, LANES), lambda b, g: (b, u0 + g, 0, 0, 0)),
            pl.BlockSpec((None, None, N_TILES, TILE_ROWS, LANES), lambda b, g: (b, z0 + g, 0, 0, 0)),
            pl.BlockSpec((None, LANES, LANES), lambda b, g: (g, 0, 0)),
            pl.BlockSpec((1, LANES), lambda b, g: (0, g)),
        ],
        out_specs=pl.BlockSpec((None, None, N_TILES, TILE_ROWS, LANES), lambda b, g: (b, g, 0, 0, 0)),
        out_shape=jax.ShapeDtypeStruct((BATCH, N_CBLK, N_TILES, TILE_ROWS, LANES), BF16),
        compiler_params=pltpu.CompilerParams(
            dimension_semantics=("arbitrary", "arbitrary"), vmem_limit_bytes=VMEM_LIMIT),
        name="pool",
    )(p5, p5, pool_w_bf, pool_scale)


def _out_kernel(yh_ref, yp_ref, x_ref, gh_ref, gp_ref, w_ref, gpost_ref, o_ref, r_scr):
    def group_norm(ref, gain_ref, j0):
        rows = []
        for j in range(j0, j0 + SUB_TILES):
            rows.append(jnp.concatenate([ref[cb, j] for cb in range(N_CBLK)], axis=-1).astype(F32))
        y = jnp.concatenate(rows, axis=0)
        ms = jnp.mean(y * y, axis=-1, keepdims=True)
        return (y * lax.rsqrt(ms + EPS) * gain_ref[...]).astype(BF16)

    for sb in range(OUT_TILES // SUB_TILES):
        j0 = sb * SUB_TILES
        yc = jnp.concatenate([group_norm(yh_ref, gh_ref, j0), group_norm(yp_ref, gp_ref, j0)], axis=-1)
        out = jnp.dot(yc, w_ref[...], preferred_element_type=F32)
        ms = jnp.mean(out * out, axis=-1, keepdims=True)
        out = out * lax.rsqrt(ms + EPS) * gpost_ref[...]
        for j in range(SUB_TILES):
            for k in range(D_MODEL // LANES):
                r_scr[sb, k, pl.ds(j, TILE_ROWS, stride=SUB_TILES), :] = out[j * TILE_ROWS:(j + 1) * TILE_ROWS,
                                                                             k * LANES:(k + 1) * LANES]
        r = jnp.concatenate([r_scr[sb, k] for k in range(D_MODEL // LANES)], axis=-1)
        o_ref[:, j0:j0 + SUB_TILES, :] = (x_ref[:, j0:j0 + SUB_TILES, :]
                                          + r.reshape(TILE_ROWS, SUB_TILES, D_MODEL))


def _out_proj(yh, yp, x4, norm_h_g, norm_p_g, w_out_bf, post_g):
    grid = (BATCH, N_TILES // OUT_TILES)
    y_spec = pl.BlockSpec((None, N_CBLK, OUT_TILES, TILE_ROWS, LANES), lambda b, g: (b, 0, g, 0, 0))
    x_spec = pl.BlockSpec((None, TILE_ROWS, OUT_TILES, D_MODEL), lambda b, g: (b, 0, g, 0))
    return pl.pallas_call(
        _out_kernel,
        grid=grid,
        in_specs=[
            y_spec, y_spec, x_spec,
            pl.BlockSpec((1, HYENA_WIDTH), lambda b, g: (0, 0)),
            pl.BlockSpec((1, POOL_WIDTH), lambda b, g: (0, 0)),
            pl.BlockSpec((D_MODEL, D_MODEL), lambda b, g: (0, 0)),
            pl.BlockSpec((1, D_MODEL), lambda b, g: (0, 0)),
        ],
        out_specs=x_spec,
        out_shape=jax.ShapeDtypeStruct((BATCH, TILE_ROWS, N_TILES, D_MODEL), F32),
        scratch_shapes=[pltpu.VMEM((OUT_TILES // SUB_TILES, D_MODEL // LANES, TILE_ROWS * SUB_TILES, LANES), F32)],
        compiler_params=pltpu.CompilerParams(
            dimension_semantics=("arbitrary", "arbitrary"), vmem_limit_bytes=VMEM_LIMIT),
        name="out_proj",
    )(yh, yp, x4, norm_h_g, norm_p_g, w_out_bf, post_g)


def kernel(x, pre_norm_g, w_in, conv_w, conv_b, filt_w1, filt_b1, filt_w2, filt_b2, filt_w3, filt_b3,
           filt_freq, filt_w_out, hyena_d, pool_w, pool_scale, norm_h_g, norm_p_g, w_out, post_norm_g):
    assert x.shape == (BATCH, SEQ, D_MODEL) and pre_norm_g.shape[0] == 1
    f1, f1i, gf, gb, f1f = (jnp.asarray(m, F32).astype(BF16) for m in (_F1, _F1I, _GF, _GB, _F1F))

    x4 = x.reshape(BATCH, TILE_ROWS, N_TILES, D_MODEL)
    p5 = _in_proj(x4, pre_norm_g, w_in[0].astype(BF16))

    taps = _filter_mlp(jnp.asarray(_ZFEAT), filt_w1[0], filt_b1[0], filt_w2[0], filt_b2[0], filt_w3[0], filt_b3[0],
                       filt_freq[0], filt_w_out[0], jnp.asarray(_ABS_DELTAS))
    kspec = _filter_spec(taps, hyena_d[0].reshape(2, 1, HYENA_WIDTH), f1f, gf)

    yh = _hyena(p5, conv_w[0], conv_b, kspec, f1, f1i, gf, gb)
    yp = _pool(p5, pool_w[0].astype(BF16), pool_scale)
    out4 = _out_proj(yh, yp, x4, norm_h_g, norm_p_g, w_out[0].astype(BF16), post_norm_g)
    return out4.reshape(BATCH, SEQ, D_MODEL)
```

```python
import functools
import math

import numpy as np
import jax
import jax.numpy as jnp
from jax import lax
from jax.experimental import pallas as pl
from jax.experimental.pallas import tpu as pltpu

F32 = jnp.float32
BF16 = jnp.bfloat16

D_MODEL = 1024
BATCH = 4
SEQ = 8192
HYENA_WIDTH = 512
POOL_WIDTH = 512
POOL_WINDOWS = (2, 4, 8, 16)
FILTER_EMB = 33
FILTER_BANDS = 16
FILTER_HIDDEN = 64
PROJ_WIDTH = 3072
EPS = 1e-6

LANES = 128
N_FFT = 2 * SEQ
N_TILES = 128
TILE_ROWS = SEQ // N_TILES
N_SLOTS = 64
CHUNK = 32
HY_SUB = 32
N_CHUNKS = N_TILES // CHUNK
A_STRIDE = 72
SLOT_BATCH = 4
N_CBLK = HYENA_WIDTH // LANES
PROJ_BLOCKS = PROJ_WIDTH // LANES
IN_TILES = 16
OUT_TILES = 32
SUB_TILES = 8
VMEM_LIMIT = 60 * 1024 * 1024


def _dft_tables():
    n1 = np.arange(TILE_ROWS)
    n2 = np.arange(N_TILES)
    s = np.arange(N_SLOTS)
    ph = 2 * np.pi * (n2[:, None, None] * s[None, :, None] / N_FFT
                      + n1[None, None, :] * s[None, :, None] / 128.0)
    f1 = np.zeros((N_TILES, 128, TILE_ROWS))
    f1[:, :64, :] = np.cos(ph)
    f1[:, 64:, :] = -np.sin(ph)
    f1[:, 0, :] = 1.0
    f1[:, 64, :] = (-1.0) ** n1
    php = np.transpose(ph, (0, 2, 1))
    f1i = np.zeros((N_TILES, TILE_ROWS, 128))
    f1i[:, :, :64] = 2 * np.cos(php) / N_FFT
    f1i[:, :, 64:] = -2 * np.sin(php) / N_FFT
    f1i[:, :, 0] = 1.0 / N_FFT
    f1i[:, :, 64] = ((-1.0) ** n1)[None, :] / N_FFT
    k2 = np.arange(128)
    th = 2 * np.pi * np.outer(k2, n2) / 128.0
    c, sn = np.cos(th), np.sin(th)
    g = np.block([[c, sn], [-sn, c]])
    gi = np.block([[c, -sn], [sn, c]])
    kk = np.arange(64)
    tha = 2 * np.pi * np.outer(kk, n2) / 128.0
    thb = 2 * np.pi * np.outer(64 + 128 * kk, n2) / N_FFT
    g0 = np.zeros((256, 256))
    g0[0:64, 0:128] = np.cos(tha)
    g0[64:128, 128:256] = np.cos(thb)
    g0[128:192, 0:128] = -np.sin(tha)
    g0[128, 0:128] = (-1.0) ** n2
    g0[192:256, 128:256] = -np.sin(thb)
    g0i = np.zeros((256, 256))
    g0i[0:128, 0:64] = 2 * np.cos(tha.T)
    g0i[0:128, 0] = 1.0
    g0i[0:128, 128:192] = -2 * np.sin(tha.T)
    g0i[0:128, 128] = (-1.0) ** n2
    g0i[128:256, 64:128] = 2 * np.cos(thb.T)
    g0i[128:256, 192:256] = -2 * np.sin(thb.T)
    q = np.arange(256)
    perm = np.where(q % 16 < 8, 8 * (q // 16) + q % 16, 128 + 8 * (q // 16) + q % 16 - 8)
    gf = np.stack([g0[perm, :], g[perm, :]])
    gb = np.stack([g0i[:, perm], gi[:, perm]])
    n1f = np.arange(128)
    phf = 2 * np.pi * (n2[:, None, None] * s[None, :, None] / N_FFT
                       + n1f[None, None, :] * s[None, :, None] / 128.0)
    full = np.zeros((N_TILES, 128, 128))
    full[:, :64, :] = np.cos(phf)
    full[:, 64:, :] = -np.sin(phf)
    full[:, 0, :] = 1.0
    full[:, 64, :] = (-1.0) ** n1f
    rev = full[:, :, 127:63:-1].copy()
    rev[0, :, 1:] = full[0, :, 127:64:-1]
    rev[0, :, 0] = 0.0
    f1f = np.concatenate([full[:, :, :64], rev], axis=2)
    pair64 = np.arange(128) % 2 * 64 + np.arange(128) // 2
    pair128 = np.arange(256) % 2 * 128 + np.arange(256) // 2
    f1, f1f = f1[:, pair64, :], f1f[:, pair64, :]
    f1i = f1i[:, :, pair64]
    gf = gf[:, :, pair128]
    gb = gb[:, pair128, :]
    return f1, f1i, gf, gb, f1f


def _filter_features():
    pos = np.arange(SEQ, dtype=np.float64)
    t = pos / (SEQ - 1)
    ang = 2.0 * math.pi * pos / SEQ
    bands = np.linspace(1e-4, FILTER_BANDS - 1, FILTER_BANDS)
    z = np.concatenate([t[:, None], np.cos(bands[None, :] * ang[:, None]),
                        -np.sin(bands[None, :] * ang[:, None])], axis=-1)
    z = z.reshape(TILE_ROWS, N_TILES, FILTER_EMB).transpose(1, 0, 2).reshape(SEQ, FILTER_EMB)
    zp = np.zeros((FILTER_HIDDEN, SEQ))
    zp[:FILTER_EMB, :] = z.T
    max_decay = math.log(1e-2) / 0.3
    min_decay = math.log(1e-2) / 1.5
    deltas = np.abs(np.linspace(min_decay, max_decay, HYENA_WIDTH))
    return zp.astype(np.float32), deltas.astype(np.float32)[None, :]


_F1, _F1I, _GF, _GB, _F1F = _dft_tables()
_ZFEAT, _ABS_DELTAS = _filter_features()


def _shift_down(x):
    rows = lax.broadcasted_iota(jnp.int32, x.shape, 0)
    return jnp.where(rows == 0, 0.0, pltpu.roll(x, 1, axis=0))


def _shift_up(x):
    rows = lax.broadcasted_iota(jnp.int32, x.shape, 0)
    return jnp.where(rows == x.shape[0] - 1, 0.0, pltpu.roll(x, x.shape[0] - 1, axis=0))


def _pair(ref, i):
    return jnp.concatenate([ref[0, i], ref[1, i]], axis=-1).astype(F32)


def _dup(x):
    return jnp.concatenate([x, x], axis=-1)


def _silu(z):
    hz = 0.5 * z
    return hz * (1.0 + jnp.tanh(hz))


def _tile_rows(n2):
    return pl.ds(pl.multiple_of(n2 * A_STRIDE, 8), N_SLOTS)


def _slot_rows(s):
    return pl.ds(s, N_TILES, stride=A_STRIDE)


def _stage_load(a_ref, rows):
    words = jnp.concatenate([a_ref[0, rows, :], a_ref[1, rows, :]], axis=-1)
    return pltpu.bitcast(words, BF16)


def _stage_store(a_ref, rows, val):
    words = pltpu.bitcast(val.astype(BF16), jnp.uint32)
    a_ref[0, rows, :] = words[:, :LANES]
    a_ref[1, rows, :] = words[:, LANES:]


N_BATCHES = N_SLOTS // SLOT_BATCH
SPEC_BLOCKS = 16


def _batch_slots(i):
    return [i * SLOT_BATCH + j for j in range(SLOT_BATCH)]


def _stage2_forward(a_ref, gf_ref, slots, sel0):
    xs = [_stage_load(a_ref, _slot_rows(s)) for s in slots]
    return [jnp.dot(gf_ref[sel0] if j == 0 else gf_ref[1], x, preferred_element_type=F32)
            for j, x in enumerate(xs)]


def _re_im_blocks(y):
    return [(y[16 * i:16 * i + 8], y[16 * i + 8:16 * i + 16]) for i in range(SPEC_BLOCKS)]


_GATE_CHUNKS = (3, 5)


def _in_proj_kernel(x_ref, g_ref, w_ref, o_ref, h_scr):
    ncol = 512
    for sb in range(IN_TILES // SUB_TILES):
        j0 = sb * SUB_TILES
        x = x_ref[:, j0:j0 + SUB_TILES, :].reshape(TILE_ROWS * SUB_TILES, D_MODEL)
        ms = jnp.mean(x * x, axis=-1, keepdims=True)
        hn = x * lax.rsqrt(ms + EPS) * g_ref[...]
        for k in range(D_MODEL // LANES):
            h_scr[sb, k] = hn[:, k * LANES:(k + 1) * LANES]
        h = jnp.concatenate(
            [jnp.concatenate([h_scr[sb, k, pl.ds(j, TILE_ROWS, stride=SUB_TILES), :]
                              for k in range(D_MODEL // LANES)], axis=-1).astype(BF16)
             for j in range(SUB_TILES)], axis=0)
        for c in range(PROJ_WIDTH // ncol):
            p = jnp.dot(h, w_ref[:, c * ncol:(c + 1) * ncol], preferred_element_type=F32)
            if c in _GATE_CHUNKS:
                p = _silu(p)
            p = p.astype(BF16)
            for cb in range(ncol // LANES):
                for j in range(SUB_TILES):
                    o_ref[c * (ncol // LANES) + cb, j0 + j] = p[j * TILE_ROWS:(j + 1) * TILE_ROWS,
                                                                cb * LANES:(cb + 1) * LANES]


def _in_proj(x4, pre_g, w_in_bf):
    grid = (BATCH, N_TILES // IN_TILES)
    return pl.pallas_call(
        _in_proj_kernel,
        grid=grid,
        in_specs=[
            pl.BlockSpec((None, TILE_ROWS, IN_TILES, D_MODEL), lambda b, g: (b, 0, g, 0)),
            pl.BlockSpec((1, D_MODEL), lambda b, g: (0, 0)),
            pl.BlockSpec((D_MODEL, PROJ_WIDTH), lambda b, g: (0, 0)),
        ],
        out_specs=pl.BlockSpec((None, PROJ_BLOCKS, IN_TILES, TILE_ROWS, LANES), lambda b, g: (b, 0, g, 0, 0)),
        out_shape=jax.ShapeDtypeStruct((BATCH, PROJ_BLOCKS, N_TILES, TILE_ROWS, LANES), BF16),
        scratch_shapes=[pltpu.VMEM((IN_TILES // SUB_TILES, D_MODEL // LANES, TILE_ROWS * SUB_TILES, LANES), F32)],
        compiler_params=pltpu.CompilerParams(
            dimension_semantics=("arbitrary", "arbitrary"), vmem_limit_bytes=VMEM_LIMIT),
        name="in_proj",
    )(x4, pre_g, w_in_bf)


def _split_bf16(x):
    hi = x.astype(BF16)
    return hi, (x - hi.astype(F32)).astype(BF16)


def _stack_weight_rows(w):
    hi, lo = _split_bf16(w)
    return jnp.concatenate([hi, lo, hi], axis=1)


def _filter_mlp_kernel(z_ref, w1_ref, b1_ref, w2_ref, b2_ref, w3_ref, b3_ref, fr_ref,
                       wp2_ref, wpl_ref, dl_ref, o_ref):
    def layer(w_ref, b_ref, f_ref, h):
        hi, lo = _split_bf16(h)
        pre = jnp.dot(w_ref[...], jnp.concatenate([hi, hi, lo], axis=0), preferred_element_type=F32)
        return jnp.sin(f_ref[...] * (pre + b_ref[...]))

    h = layer(w1_ref, b1_ref, fr_ref, z_ref[...])
    h = layer(w2_ref, b2_ref, fr_ref, h)
    h = layer(w3_ref, b3_ref, fr_ref, h)
    h = jnp.concatenate([h, jnp.zeros_like(h)], axis=0)
    hi, lo = _split_bf16(h.T)
    hs = jnp.concatenate([hi, lo], axis=1)
    rows = CHUNK * TILE_ROWS
    r = lax.broadcasted_iota(jnp.int32, (rows, HYENA_WIDTH), 0)
    pos = 128 * (r % TILE_ROWS) + CHUNK * pl.program_id(0) + r // TILE_ROWS
    t = pos.astype(F32) / float(SEQ - 1)
    decay = jnp.exp(-t * dl_ref[...])
    for od in range(4):
        cols = slice(od * HYENA_WIDTH, (od + 1) * HYENA_WIDTH)
        taps = (jnp.dot(hs, wp2_ref[:, cols], preferred_element_type=F32)
                + jnp.dot(hi, wpl_ref[:, cols], preferred_element_type=F32)) * decay
        taps = taps.astype(BF16)
        for cb in range(N_CBLK):
            for i in range(CHUNK):
                o_ref[od * N_CBLK + cb, i] = taps[i * TILE_ROWS:(i + 1) * TILE_ROWS, cb * LANES:(cb + 1) * LANES]


def _filter_mlp(zfeat_t, w1, b1, w2, b2, w3, b3, freq, w_proj, abs_deltas):
    hid = FILTER_HIDDEN
    col = lambda v, n: jnp.pad(v.reshape(-1, 1), ((0, n - v.size), (0, 0)))
    w1s = _stack_weight_rows(jnp.pad(w1, ((0, hid - FILTER_EMB), (0, 0))).T)
    w2s = _stack_weight_rows(w2.T)
    w3s = _stack_weight_rows(w3.T)
    wp_hi, wp_lo = _split_bf16(jnp.pad(w_proj, ((0, LANES - hid), (0, 0))))
    wp2 = jnp.concatenate([wp_hi, wp_hi], axis=0)
    full = lambda shape: pl.BlockSpec(shape, lambda c: (0,) * len(shape))
    cols = CHUNK * TILE_ROWS
    return pl.pallas_call(
        _filter_mlp_kernel,
        grid=(N_CHUNKS,),
        in_specs=[
            pl.BlockSpec((hid, cols), lambda c: (0, c)),
            full((hid, 3 * hid)), full((hid, 1)),
            full((hid, 3 * hid)), full((hid, 1)),
            full((hid, 3 * hid)), full((hid, 1)),
            full((hid, 1)),
            full((2 * LANES, 4 * HYENA_WIDTH)), full((LANES, 4 * HYENA_WIDTH)),
            full((1, HYENA_WIDTH)),
        ],
        out_specs=pl.BlockSpec((4 * N_CBLK, CHUNK, TILE_ROWS, LANES), lambda c: (0, c, 0, 0)),
        out_shape=jax.ShapeDtypeStruct((4 * N_CBLK, N_TILES, TILE_ROWS, LANES), BF16),
        compiler_params=pltpu.CompilerParams(dimension_semantics=("arbitrary",), vmem_limit_bytes=VMEM_LIMIT),
        name="filter_mlp",
    )(zfeat_t, w1s, col(b1, hid), w2s, col(b2, hid), w3s, col(b3, hid), col(freq, hid), wp2, wp_lo, abs_deltas)


SPEC_SLOTS_PER_STEP = 16
N_SPEC_STEPS = N_SLOTS // SPEC_SLOTS_PER_STEP


def _filter_spec_kernel(fa_ref, fb_ref, ba_ref, bb_ref, bxa_ref, bxb_ref, d_ref, f1f_ref, gf_ref, k_ref, a_scr):
    t = pl.program_id(2)

    @pl.when(t < N_CHUNKS)
    def _():
        for i in range(CHUNK):
            n2 = t * CHUNK + i
            fwd = jnp.concatenate([fa_ref[i], fb_ref[i]], axis=-1)
            if i == 0:
                bwd = jnp.concatenate([bxa_ref[0], bxb_ref[0]], axis=-1)
            else:
                bwd = jnp.concatenate([ba_ref[CHUNK - i], bb_ref[CHUNK - i]], axis=-1)
            taps = jnp.concatenate([fwd, bwd], axis=0)
            _stage_store(a_scr, _tile_rows(n2), jnp.dot(f1f_ref[n2], taps, preferred_element_type=F32))

    @pl.when(t >= N_CHUNKS)
    def _():
        q = t - N_CHUNKS
        for bi in range(SPEC_SLOTS_PER_STEP // SLOT_BATCH):
            base = q * SPEC_SLOTS_PER_STEP + bi * SLOT_BATCH
            sel0 = jnp.where(q == 0, 0, 1) if bi == 0 else 1
            ys = _stage2_forward(a_scr, gf_ref, [base + j for j in range(SLOT_BATCH)], sel0)
            d = d_ref[...]
            for j, y in enumerate(ys):
                blocks = []
                for i, (yr, yi) in enumerate(_re_im_blocks(y)):
                    yr = yr + d
                    if bi == 0 and j == 0 and i == 0:
                        rows = lax.broadcasted_iota(jnp.int32, yi.shape, 0)
                        yi = yi + jnp.where((rows == 0) & (q == 0), d, 0.0)
                    blocks += [yr, yi]
                spec = jnp.concatenate(blocks, axis=0).astype(BF16)
                k_ref[0, bi * SLOT_BATCH + j] = spec[:, :LANES]
                k_ref[1, bi * SLOT_BATCH + j] = spec[:, LANES:]


def _filter_spec(taps, hyena_d3, f1f, gf):
    grid = (2, N_CBLK // 2, N_CHUNKS + N_SPEC_STEPS)
    fchunk = lambda t: jnp.minimum(t, N_CHUNKS - 1)
    const = lambda shape: pl.BlockSpec(shape, lambda o, c, t: (0,) * len(shape), pipeline_mode=pl.Buffered(1))

    def fwd_spec(k):
        return pl.BlockSpec((None, CHUNK, TILE_ROWS, LANES),
                            lambda o, c, t: ((2 * o) * N_CBLK + 2 * c + k, fchunk(t), 0, 0))

    def bwd_spec(k):
        return pl.BlockSpec((None, CHUNK, TILE_ROWS, LANES),
                            lambda o, c, t: ((2 * o + 1) * N_CBLK + 2 * c + k, N_CHUNKS - 1 - fchunk(t), 0, 0))

    def bwd_tile_spec(k):
        return pl.BlockSpec((None, 1, TILE_ROWS, LANES),
                            lambda o, c, t: ((2 * o + 1) * N_CBLK + 2 * c + k, (N_TILES - CHUNK * fchunk(t)) % N_TILES, 0, 0))

    return pl.pallas_call(
        _filter_spec_kernel,
        grid=grid,
        in_specs=[fwd_spec(0), fwd_spec(1), bwd_spec(0), bwd_spec(1), bwd_tile_spec(0), bwd_tile_spec(1),
                  pl.BlockSpec((None, 1, 2 * LANES), lambda o, c, t: (o, 0, c)),
                  const((N_TILES, 128, 128)), const((2, 256, 256))],
        out_specs=pl.BlockSpec((None, 2, SPEC_SLOTS_PER_STEP, 256, LANES),
                               lambda o, c, t: (o, c, jnp.maximum(t - N_CHUNKS, 0), 0, 0)),
        out_shape=jax.ShapeDtypeStruct((2, N_CBLK, N_SLOTS, 256, LANES), BF16),
        scratch_shapes=[pltpu.VMEM((2, N_TILES * A_STRIDE, LANES), jnp.uint32)],
        compiler_params=pltpu.CompilerParams(
            dimension_semantics=("arbitrary", "arbitrary", "arbitrary"), vmem_limit_bytes=VMEM_LIMIT),
        name="filter_spec",
    )(taps, taps, taps, taps, taps, taps, hyena_d3, f1f, gf)


_T_S1 = 0
_T_F0 = 1
_T_M = 2
_T_F1 = 3
_T_E = 4
_T_END = 5
N_SUB = N_TILES // HY_SUB


def _hyena_kernel(cin_ref, pz_ref, cw_ref, k_ref, f1_ref, f1i_ref, gf_ref, gb_ref, o_ref, a_scr, z_scr):
    t = pl.program_id(2)

    def stage1(n2, u_bf):
        _stage_store(a_scr, _tile_rows(n2), jnp.dot(f1_ref[n2], u_bf, preferred_element_type=F32))

    def inv_stage1(n2):
        return jnp.dot(f1i_ref[n2], _stage_load(a_scr, _tile_rows(n2)), preferred_element_type=F32)

    def short_conv(sb, row0):
        base = sb * HY_SUB
        first = _pair(cin_ref, (base + N_TILES - 1) % N_TILES)
        first = jnp.where(sb == 0, _shift_down(first), first)
        last = _pair(cin_ref, (base + HY_SUB) % N_TILES)
        last = jnp.where(sb == N_SUB - 1, _shift_up(last), last)
        tiles = [first] + [_pair(cin_ref, base + i) for i in range(HY_SUB)] + [last]
        w = cw_ref[row0:row0 + 4, :]
        w0, w1, w2, b = _dup(w[0:1]), _dup(w[1:2]), _dup(w[2:3]), _dup(w[3:4])
        return [tiles[i] * w0 + tiles[i + 1] * w1 + tiles[i + 2] * w2 + b for i in range(HY_SUB)]

    def filter_multiply(slots, ys, first, buf):
        for j, (s, y) in enumerate(zip(slots, ys)):
            blocks = []
            for i, (yr, yi) in enumerate(_re_im_blocks(y)):
                kblk = k_ref[s, 16 * i:16 * i + 16, :].astype(F32)
                kr, ki = kblk[:8], kblk[8:]
                if first and j == 0 and i == 0:
                    rows = lax.broadcasted_iota(jnp.int32, kr.shape, 0)
                    ka, kb, kd = kr, jnp.where(rows == 0, 0.0, ki), jnp.where(rows == 0, ki, kr)
                else:
                    ka, kb, kd = kr, ki, kr
                ka, kb, kd = _dup(ka), _dup(kb), _dup(kd)
                blocks += [yr * ka - yi * kb, yr * kb + yi * kd]
            z_scr[buf, j] = jnp.concatenate(blocks, axis=0).astype(BF16)

    def stage2_inverse(slots, first_sel, buf):
        bms = [jnp.dot(gb_ref[first_sel] if j == 0 else gb_ref[1], z_scr[buf, j], preferred_element_type=F32)
               for j in range(SLOT_BATCH)]
        for s, bm in zip(slots, bms):
            _stage_store(a_scr, _slot_rows(s), bm)

    def spectral_phase():
        slots0 = _batch_slots(0)
        filter_multiply(slots0, _stage2_forward(a_scr, gf_ref, slots0, 0), True, 0)

        def loop(i, carry):
            slots = _batch_slots(i)
            ys = _stage2_forward(a_scr, gf_ref, slots, 1)
            stage2_inverse(_batch_slots(i - 1), jnp.where(i == 1, 0, 1), (i - 1) % 2)
            filter_multiply(slots, ys, False, i % 2)
            return carry

        lax.fori_loop(1, N_BATCHES, loop, 0, unroll=5)
        stage2_inverse(_batch_slots(N_BATCHES - 1), 1, (N_BATCHES - 1) % 2)

    @pl.when(t == _T_S1)
    def _():
        def sub(sb, carry):
            vs = short_conv(sb, 0)
            for i in range(HY_SUB):
                stage1(sb * HY_SUB + i, vs[i].astype(BF16))
            return carry

        lax.fori_loop(0, N_SUB, sub, 0)

    @pl.when((t == _T_F0) | (t == _T_F1))
    def _():
        spectral_phase()

    @pl.when(t == _T_M)
    def _():
        def sub(sb, carry):
            gates = short_conv(sb, 4)
            ys = [inv_stage1(sb * HY_SUB + i) for i in range(HY_SUB)]
            us = [(gates[i] * ys[i]).astype(BF16) for i in range(HY_SUB)]
            for i in range(HY_SUB):
                stage1(sb * HY_SUB + i, us[i])
            return carry

        lax.fori_loop(0, N_SUB, sub, 0)

    @pl.when(t == _T_E)
    def _():
        def sub(sb, carry):
            gates = short_conv(sb, 8)
            for i in range(HY_SUB):
                n2 = sb * HY_SUB + i
                res = (gates[i] * inv_stage1(n2) * _pair(pz_ref, n2)).astype(BF16)
                o_ref[0, n2] = res[:, :LANES]
                o_ref[1, n2] = res[:, LANES:]
            return carry

        lax.fori_loop(0, N_SUB, sub, 0)


def _hyena(p5, conv_w, conv_b, kspec, f1, f1i, gf, gb):
    grid = (N_CBLK, BATCH // 2, _T_END)
    cw = jnp.concatenate(
        [jnp.concatenate([conv_w[:, k * HYENA_WIDTH:(k + 1) * HYENA_WIDTH],
                          conv_b[:, k * HYENA_WIDTH:(k + 1) * HYENA_WIDTH]], axis=0) for k in range(3)], axis=0)
    seq_block = (2, None, N_TILES, TILE_ROWS, LANES)

    def conv_in_map(c, b, t):
        return (b, jnp.where(t < _T_M, 0, jnp.where(t < _T_E, 1, 2)) * N_CBLK + c, 0, 0, 0)

    def z_map(c, b, t):
        flat = c * (BATCH // 2) + b
        sel = jnp.where(t >= _T_M, flat, jnp.maximum(flat - 1, 0))
        return (sel % (BATCH // 2), 3 * N_CBLK + sel // (BATCH // 2), 0, 0, 0)

    const = lambda shape: pl.BlockSpec(shape, lambda c, b, t: (0,) * len(shape), pipeline_mode=pl.Buffered(1))
    in_specs = [
        pl.BlockSpec(seq_block, conv_in_map),
        pl.BlockSpec(seq_block, z_map),
        pl.BlockSpec((12, LANES), lambda c, b, t: (0, c)),
        pl.BlockSpec((None, None, N_SLOTS, 256, LANES), lambda c, b, t: (jnp.where(t >= _T_M, 1, 0), c, 0, 0, 0)),
        const((N_TILES, 128, TILE_ROWS)), const((N_TILES, TILE_ROWS, 128)),
        const((2, 256, 256)), const((2, 256, 256)),
    ]
    return pl.pallas_call(
        _hyena_kernel,
        grid=grid,
        in_specs=in_specs,
        out_specs=pl.BlockSpec(seq_block, lambda c, b, t: (b, c, 0, 0, 0)),
        out_shape=jax.ShapeDtypeStruct((BATCH, N_CBLK, N_TILES, TILE_ROWS, LANES), BF16),
        scratch_shapes=[pltpu.VMEM((2, N_TILES * A_STRIDE, LANES), jnp.uint32),
                        pltpu.VMEM((2, SLOT_BATCH, 256, 2 * LANES), BF16)],
        compiler_params=pltpu.CompilerParams(
            dimension_semantics=("arbitrary", "arbitrary", "arbitrary"), vmem_limit_bytes=VMEM_LIMIT),
        name="hyena",
    )(p5, p5, cw, kspec, f1, f1i, gf, gb)


_POOL_GROUP = 8


def _pool_kernel(u_ref, z_ref, pw_ref, ps_ref, o_ref):
    g = pl.program_id(1)
    n_groups = N_TILES // _POOL_GROUP
    n1 = lax.broadcasted_iota(jnp.int32, (TILE_ROWS, LANES), 0)

    def tile(idx):
        if isinstance(idx, int) and idx < 0:
            return _shift_down(u_ref[idx + N_TILES].astype(F32))
        if isinstance(idx, int) and idx >= N_TILES:
            return _shift_up(u_ref[idx - N_TILES].astype(F32))
        return u_ref[idx].astype(F32)

    for gi_, w in enumerate(POOL_WINDOWS):
        @pl.when(g == gi_)
        def _(w=w):
            lo, hi = w // 2, w - 1 - w // 2

            def run_group(base, window_sum, edge):
                pooled = []
                for i in range(_POOL_GROUP):
                    n2 = base + i
                    if i > 0:
                        window_sum = window_sum + tile(n2 + hi) - tile(n2 - 1 - lo)
                    if edge:
                        pos = 128 * n1 + n2
                        cnt = (jnp.minimum(pos + hi, SEQ - 1) - jnp.maximum(pos - lo, 0) + 1).astype(F32)
                        mean = window_sum / cnt
                    else:
                        mean = window_sum * (1.0 / w)
                    pooled.append((mean - tile(n2)).astype(BF16))
                pooled = jnp.concatenate(pooled, axis=0)
                y = jnp.dot(pooled, pw_ref[...], preferred_element_type=F32) * ps_ref[...]
                for i in range(_POOL_GROUP):
                    n2 = base + i
                    o_ref[n2] = (y[i * TILE_ROWS:(i + 1) * TILE_ROWS] * z_ref[n2].astype(F32)).astype(BF16)
                nxt = base + _POOL_GROUP
                return window_sum + tile(nxt + hi) - tile(nxt - 1 - lo)

            first = tile(-lo)
            for d in range(-lo + 1, hi + 1):
                first = first + tile(d)
            carry = run_group(0, first, True)
            carry = lax.fori_loop(1, n_groups - 1,
                                  lambda c, ws: run_group(c * _POOL_GROUP, ws, False), carry, unroll=2)
            run_group(N_TILES - _POOL_GROUP, carry, True)


def _pool(p5, pool_w_bf, pool_scale):
    u0 = 4 * N_CBLK
    z0 = 5 * N_CBLK
    return pl.pallas_call(
        _pool_kernel,
        grid=(BATCH, len(POOL_WINDOWS)),
        in_specs=[
            pl.BlockSpec((None, None, N_TILES, TILE_ROWS, LANES), lambda b, g: (b, u0 + g, 0, 0, 0)),
            pl.BlockSpec((None, None, N_TILES, TILE_ROWS, LANES), lambda b, g: (b, z0 + g, 0, 0, 0)),
            pl.BlockSpec((None, LANES, LANES), lambda b, g: (g, 0, 0)),
            pl.BlockSpec((1, LANES), lambda b, g: (0, g)),
        ],
        out_specs=pl.BlockSpec((None, None, N_TILES, TILE_ROWS, LANES), lambda b, g: (b, g, 0, 0, 0)),
        out_shape=jax.ShapeDtypeStruct((BATCH, N_CBLK, N_TILES, TILE_ROWS, LANES), BF16),
        compiler_params=pltpu.CompilerParams(
            dimension_semantics=("arbitrary", "arbitrary"), vmem_limit_bytes=VMEM_LIMIT),
        name="pool",
    )(p5, p5, pool_w_bf, pool_scale)


def _out_kernel(yh_ref, yp_ref, x_ref, gh_ref, gp_ref, w_ref, gpost_ref, o_ref, r_scr):
    def group_norm(ref, gain_ref, j0):
        rows = []
        for j in range(j0, j0 + SUB_TILES):
            rows.append(jnp.concatenate([ref[cb, j] for cb in range(N_CBLK)], axis=-1).astype(F32))
        y = jnp.concatenate(rows, axis=0)
        ms = jnp.mean(y * y, axis=-1, keepdims=True)
        return (y * lax.rsqrt(ms + EPS) * gain_ref[...]).astype(BF16)

    for sb in range(OUT_TILES // SUB_TILES):
        j0 = sb * SUB_TILES
        yc = jnp.concatenate([group_norm(yh_ref, gh_ref, j0), group_norm(yp_ref, gp_ref, j0)], axis=-1)
        out = jnp.dot(yc, w_ref[...], preferred_element_type=F32)
        ms = jnp.mean(out * out, axis=-1, keepdims=True)
        out = out * lax.rsqrt(ms + EPS) * gpost_ref[...]
        for j in range(SUB_TILES):
            for k in range(D_MODEL // LANES):
                r_scr[sb, k, pl.ds(j, TILE_ROWS, stride=SUB_TILES), :] = out[j * TILE_ROWS:(j + 1) * TILE_ROWS,
                                                                             k * LANES:(k + 1) * LANES]
        r = jnp.concatenate([r_scr[sb, k] for k in range(D_MODEL // LANES)], axis=-1)
        o_ref[:, j0:j0 + SUB_TILES, :] = (x_ref[:, j0:j0 + SUB_TILES, :]
                                          + r.reshape(TILE_ROWS, SUB_TILES, D_MODEL))


def _out_proj(yh, yp, x4, norm_h_g, norm_p_g, w_out_bf, post_g):
    grid = (BATCH, N_TILES // OUT_TILES)
    y_spec = pl.BlockSpec((None, N_CBLK, OUT_TILES, TILE_ROWS, LANES), lambda b, g: (b, 0, g, 0, 0))
    x_spec = pl.BlockSpec((None, TILE_ROWS, OUT_TILES, D_MODEL), lambda b, g: (b, 0, g, 0))
    return pl.pallas_call(
        _out_kernel,
        grid=grid,
        in_specs=[
            y_spec, y_spec, x_spec,
            pl.BlockSpec((1, HYENA_WIDTH), lambda b, g: (0, 0)),
            pl.BlockSpec((1, POOL_WIDTH), lambda b, g: (0, 0)),
            pl.BlockSpec((D_MODEL, D_MODEL), lambda b, g: (0, 0)),
            pl.BlockSpec((1, D_MODEL), lambda b, g: (0, 0)),
        ],
        out_specs=x_spec,
        out_shape=jax.ShapeDtypeStruct((BATCH, TILE_ROWS, N_TILES, D_MODEL), F32),
        scratch_shapes=[pltpu.VMEM((OUT_TILES // SUB_TILES, D_MODEL // LANES, TILE_ROWS * SUB_TILES, LANES), F32)],
        compiler_params=pltpu.CompilerParams(
            dimension_semantics=("arbitrary", "arbitrary"), vmem_limit_bytes=VMEM_LIMIT),
        name="out_proj",
    )(yh, yp, x4, norm_h_g, norm_p_g, w_out_bf, post_g)


def kernel(x, pre_norm_g, w_in, conv_w, conv_b, filt_w1, filt_b1, filt_w2, filt_b2, filt_w3, filt_b3,
           filt_freq, filt_w_out, hyena_d, pool_w, pool_scale, norm_h_g, norm_p_g, w_out, post_norm_g):
    assert x.shape == (BATCH, SEQ, D_MODEL) and pre_norm_g.shape[0] == 1
    f1, f1i, gf, gb, f1f = (jnp.asarray(m, F32).astype(BF16) for m in (_F1, _F1I, _GF, _GB, _F1F))

    x4 = x.reshape(BATCH, TILE_ROWS, N_TILES, D_MODEL)
    p5 = _in_proj(x4, pre_norm_g, w_in[0].astype(BF16))

    taps = _filter_mlp(jnp.asarray(_ZFEAT), filt_w1[0], filt_b1[0], filt_w2[0], filt_b2[0], filt_w3[0], filt_b3[0],
                       filt_freq[0], filt_w_out[0], jnp.asarray(_ABS_DELTAS))
    kspec = _filter_spec(taps, hyena_d[0].reshape(2, 1, HYENA_WIDTH), f1f, gf)

    yh = _hyena(p5, conv_w[0], conv_b, kspec, f1, f1i, gf, gb)
    yp = _pool(p5, pool_w[0].astype(BF16), pool_scale)
    out4 = _out_proj(yh, yp, x4, norm_h_g, norm_p_g, w_out[0].astype(BF16), post_norm_g)
    return out4.reshape(BATCH, SEQ, D_MODEL)
```

```python
import functools
import math

import numpy as np
import jax
import jax.numpy as jnp
from jax import lax
from jax.experimental import pallas as pl
from jax.experimental.pallas import tpu as pltpu

F32 = jnp.float32
BF16 = jnp.bfloat16

D_MODEL = 1024
BATCH = 4
SEQ = 8192
HYENA_WIDTH = 512
POOL_WIDTH = 512
POOL_WINDOWS = (2, 4, 8, 16)
FILTER_EMB = 33
FILTER_BANDS = 16
FILTER_HIDDEN = 64
PROJ_WIDTH = 3072
EPS = 1e-6

LANES = 128
N_FFT = 2 * SEQ
N_TILES = 128
TILE_ROWS = SEQ // N_TILES
N_SLOTS = 64
CHUNK = 32
HY_SUB = 32
N_CHUNKS = N_TILES // CHUNK
A_STRIDE = 72
SLOT_BATCH = 4
N_CBLK = HYENA_WIDTH // LANES
PROJ_BLOCKS = PROJ_WIDTH // LANES
IN_TILES = 16
OUT_TILES = 32
SUB_TILES = 8
VMEM_LIMIT = 60 * 1024 * 1024


def _dft_tables():
    n1 = np.arange(TILE_ROWS)
    n2 = np.arange(N_TILES)
    s = np.arange(N_SLOTS)
    ph = 2 * np.pi * (n2[:, None, None] * s[None, :, None] / N_FFT
                      + n1[None, None, :] * s[None, :, None] / 128.0)
    f1 = np.zeros((N_TILES, 128, TILE_ROWS))
    f1[:, :64, :] = np.cos(ph)
    f1[:, 64:, :] = -np.sin(ph)
    f1[:, 0, :] = 1.0
    f1[:, 64, :] = (-1.0) ** n1
    php = np.transpose(ph, (0, 2, 1))
    f1i = np.zeros((N_TILES, TILE_ROWS, 128))
    f1i[:, :, :64] = 2 * np.cos(php) / N_FFT
    f1i[:, :, 64:] = -2 * np.sin(php) / N_FFT
    f1i[:, :, 0] = 1.0 / N_FFT
    f1i[:, :, 64] = ((-1.0) ** n1)[None, :] / N_FFT
    k2 = np.arange(128)
    th = 2 * np.pi * np.outer(k2, n2) / 128.0
    c, sn = np.cos(th), np.sin(th)
    g = np.block([[c, sn], [-sn, c]])
    gi = np.block([[c, -sn], [sn, c]])
    kk = np.arange(64)
    tha = 2 * np.pi * np.outer(kk, n2) / 128.0
    thb = 2 * np.pi * np.outer(64 + 128 * kk, n2) / N_FFT
    g0 = np.zeros((256, 256))
    g0[0:64, 0:128] = np.cos(tha)
    g0[64:128, 128:256] = np.cos(thb)
    g0[128:192, 0:128] = -np.sin(tha)
    g0[128, 0:128] = (-1.0) ** n2
    g0[192:256, 128:256] = -np.sin(thb)
    g0i = np.zeros((256, 256))
    g0i[0:128, 0:64] = 2 * np.cos(tha.T)
    g0i[0:128, 0] = 1.0
    g0i[0:128, 128:192] = -2 * np.sin(tha.T)
    g0i[0:128, 128] = (-1.0) ** n2
    g0i[128:256, 64:128] = 2 * np.cos(thb.T)
    g0i[128:256, 192:256] = -2 * np.sin(thb.T)
    q = np.arange(256)
    perm = np.where(q % 16 < 8, 8 * (q // 16) + q % 16, 128 + 8 * (q // 16) + q % 16 - 8)
    gf = np.stack([g0[perm, :], g[perm, :]])
    gb = np.stack([g0i[:, perm], gi[:, perm]])
    n1f = np.arange(128)
    phf = 2 * np.pi * (n2[:, None, None] * s[None, :, None] / N_FFT
                       + n1f[None, None, :] * s[None, :, None] / 128.0)
    full = np.zeros((N_TILES, 128, 128))
    full[:, :64, :] = np.cos(phf)
    full[:, 64:, :] = -np.sin(phf)
    full[:, 0, :] = 1.0
    full[:, 64, :] = (-1.0) ** n1f
    rev = full[:, :, 127:63:-1].copy()
    rev[0, :, 1:] = full[0, :, 127:64:-1]
    rev[0, :, 0] = 0.0
    f1f = np.concatenate([full[:, :, :64], rev], axis=2)
    pair64 = np.arange(128) % 2 * 64 + np.arange(128) // 2
    pair128 = np.arange(256) % 2 * 128 + np.arange(256) // 2
    f1, f1f = f1[:, pair64, :], f1f[:, pair64, :]
    f1i = f1i[:, :, pair64]
    gf = gf[:, :, pair128]
    gb = gb[:, pair128, :]
    return f1, f1i, gf, gb, f1f


def _filter_features():
    pos = np.arange(SEQ, dtype=np.float64)
    t = pos / (SEQ - 1)
    ang = 2.0 * math.pi * pos / SEQ
    bands = np.linspace(1e-4, FILTER_BANDS - 1, FILTER_BANDS)
    z = np.concatenate([t[:, None], np.cos(bands[None, :] * ang[:, None]),
                        -np.sin(bands[None, :] * ang[:, None])], axis=-1)
    z = z.reshape(TILE_ROWS, N_TILES, FILTER_EMB).transpose(1, 0, 2).reshape(SEQ, FILTER_EMB)
    zp = np.zeros((FILTER_HIDDEN, SEQ))
    zp[:FILTER_EMB, :] = z.T
    max_decay = math.log(1e-2) / 0.3
    min_decay = math.log(1e-2) / 1.5
    deltas = np.abs(np.linspace(min_decay, max_decay, HYENA_WIDTH))
    return zp.astype(np.float32), deltas.astype(np.float32)[None, :]


_F1, _F1I, _GF, _GB, _F1F = _dft_tables()
_ZFEAT, _ABS_DELTAS = _filter_features()


def _shift_down(x):
    rows = lax.broadcasted_iota(jnp.int32, x.shape, 0)
    return jnp.where(rows == 0, 0.0, pltpu.roll(x, 1, axis=0))


def _shift_up(x):
    rows = lax.broadcasted_iota(jnp.int32, x.shape, 0)
    return jnp.where(rows == x.shape[0] - 1, 0.0, pltpu.roll(x, x.shape[0] - 1, axis=0))


def _pair(ref, i):
    return jnp.concatenate([ref[0, i], ref[1, i]], axis=-1).astype(F32)


def _dup(x):
    return jnp.concatenate([x, x], axis=-1)


def _silu(z):
    hz = 0.5 * z
    return hz * (1.0 + jnp.tanh(hz))


def _tile_rows(n2):
    return pl.ds(pl.multiple_of(n2 * A_STRIDE, 8), N_SLOTS)


def _slot_rows(s):
    return pl.ds(s, N_TILES, stride=A_STRIDE)


def _stage_load(a_ref, rows):
    words = jnp.concatenate([a_ref[0, rows, :], a_ref[1, rows, :]], axis=-1)
    return pltpu.bitcast(words, BF16)


def _stage_store(a_ref, rows, val):
    words = pltpu.bitcast(val.astype(BF16), jnp.uint32)
    a_ref[0, rows, :] = words[:, :LANES]
    a_ref[1, rows, :] = words[:, LANES:]


N_BATCHES = N_SLOTS // SLOT_BATCH
SPEC_BLOCKS = 16


def _batch_slots(i):
    return [i * SLOT_BATCH + j for j in range(SLOT_BATCH)]


def _stage2_forward(a_ref, gf_ref, slots, sel0):
    xs = [_stage_load(a_ref, _slot_rows(s)) for s in slots]
    return [jnp.dot(gf_ref[sel0] if j == 0 else gf_ref[1], x, preferred_element_type=F32)
            for j, x in enumerate(xs)]


def _re_im_blocks(y):
    return [(y[16 * i:16 * i + 8], y[16 * i + 8:16 * i + 16]) for i in range(SPEC_BLOCKS)]


_GATE_CHUNKS = (3, 5)


def _in_proj_kernel(x_ref, g_ref, w_ref, o_ref, h_scr):
    ncol = 512
    for sb in range(IN_TILES // SUB_TILES):
        j0 = sb * SUB_TILES
        x = x_ref[:, j0:j0 + SUB_TILES, :].reshape(TILE_ROWS * SUB_TILES, D_MODEL)
        ms = jnp.mean(x * x, axis=-1, keepdims=True)
        hn = x * lax.rsqrt(ms + EPS) * g_ref[...]
        for k in range(D_MODEL // LANES):
            h_scr[sb, k] = hn[:, k * LANES:(k + 1) * LANES]
        h = jnp.concatenate(
            [jnp.concatenate([h_scr[sb, k, pl.ds(j, TILE_ROWS, stride=SUB_TILES), :]
                              for k in range(D_MODEL // LANES)], axis=-1).astype(BF16)
             for j in range(SUB_TILES)], axis=0)
        for c in range(PROJ_WIDTH // ncol):
            p = jnp.dot(h, w_ref[:, c * ncol:(c + 1) * ncol], preferred_element_type=F32)
            if c in _GATE_CHUNKS:
                p = _silu(p)
            p = p.astype(BF16)
            for cb in range(ncol // LANES):
                for j in range(SUB_TILES):
                    o_ref[c * (ncol // LANES) + cb, j0 + j] = p[j * TILE_ROWS:(j + 1) * TILE_ROWS,
                                                                cb * LANES:(cb + 1) * LANES]


def _in_proj(x4, pre_g, w_in_bf):
    grid = (BATCH, N_TILES // IN_TILES)
    return pl.pallas_call(
        _in_proj_kernel,
        grid=grid,
        in_specs=[
            pl.BlockSpec((None, TILE_ROWS, IN_TILES, D_MODEL), lambda b, g: (b, 0, g, 0)),
            pl.BlockSpec((1, D_MODEL), lambda b, g: (0, 0)),
            pl.BlockSpec((D_MODEL, PROJ_WIDTH), lambda b, g: (0, 0)),
        ],
        out_specs=pl.BlockSpec((None, PROJ_BLOCKS, IN_TILES, TILE_ROWS, LANES), lambda b, g: (b, 0, g, 0, 0)),
        out_shape=jax.ShapeDtypeStruct((BATCH, PROJ_BLOCKS, N_TILES, TILE_ROWS, LANES), BF16),
        scratch_shapes=[pltpu.VMEM((IN_TILES // SUB_TILES, D_MODEL // LANES, TILE_ROWS * SUB_TILES, LANES), F32)],
        compiler_params=pltpu.CompilerParams(
            dimension_semantics=("arbitrary", "arbitrary"), vmem_limit_bytes=VMEM_LIMIT),
        name="in_proj",
    )(x4, pre_g, w_in_bf)


def _split_bf16(x):
    hi = x.astype(BF16)
    return hi, (x - hi.astype(F32)).astype(BF16)


def _stack_weight_rows(w):
    hi, lo = _split_bf16(w)
    return jnp.concatenate([hi, lo, hi], axis=1)


def _filter_mlp_kernel(z_ref, w1_ref, b1_ref, w2_ref, b2_ref, w3_ref, b3_ref, fr_ref,
                       wp2_ref, wpl_ref, dl_ref, o_ref):
    def layer(w_ref, b_ref, f_ref, h):
        hi, lo = _split_bf16(h)
        pre = jnp.dot(w_ref[...], jnp.concatenate([hi, hi, lo], axis=0), preferred_element_type=F32)
        return jnp.sin(f_ref[...] * (pre + b_ref[...]))

    h = layer(w1_ref, b1_ref, fr_ref, z_ref[...])
    h = layer(w2_ref, b2_ref, fr_ref, h)
    h = layer(w3_ref, b3_ref, fr_ref, h)
    h = jnp.concatenate([h, jnp.zeros_like(h)], axis=0)
    hi, lo = _split_bf16(h.T)
    hs = jnp.concatenate([hi, lo], axis=1)
    rows = CHUNK * TILE_ROWS
    r = lax.broadcasted_iota(jnp.int32, (rows, HYENA_WIDTH), 0)
    pos = 128 * (r % TILE_ROWS) + CHUNK * pl.program_id(0) + r // TILE_ROWS
    t = pos.astype(F32) / float(SEQ - 1)
    decay = jnp.exp(-t * dl_ref[...])
    for od in range(4):
        cols = slice(od * HYENA_WIDTH, (od + 1) * HYENA_WIDTH)
        taps = (jnp.dot(hs, wp2_ref[:, cols], preferred_element_type=F32)
                + jnp.dot(hi, wpl_ref[:, cols], preferred_element_type=F32)) * decay
        taps = taps.astype(BF16)
        for cb in range(N_CBLK):
            for i in range(CHUNK):
                o_ref[od * N_CBLK + cb, i] = taps[i * TILE_ROWS:(i + 1) * TILE_ROWS, cb * LANES:(cb + 1) * LANES]


def _filter_mlp(zfeat_t, w1, b1, w2, b2, w3, b3, freq, w_proj, abs_deltas):
    hid = FILTER_HIDDEN
    col = lambda v, n: jnp.pad(v.reshape(-1, 1), ((0, n - v.size), (0, 0)))
    w1s = _stack_weight_rows(jnp.pad(w1, ((0, hid - FILTER_EMB), (0, 0))).T)
    w2s = _stack_weight_rows(w2.T)
    w3s = _stack_weight_rows(w3.T)
    wp_hi, wp_lo = _split_bf16(jnp.pad(w_proj, ((0, LANES - hid), (0, 0))))
    wp2 = jnp.concatenate([wp_hi, wp_hi], axis=0)
    full = lambda shape: pl.BlockSpec(shape, lambda c: (0,) * len(shape))
    cols = CHUNK * TILE_ROWS
    return pl.pallas_call(
        _filter_mlp_kernel,
        grid=(N_CHUNKS,),
        in_specs=[
            pl.BlockSpec((hid, cols), lambda c: (0, c)),
            full((hid, 3 * hid)), full((hid, 1)),
            full((hid, 3 * hid)), full((hid, 1)),
            full((hid, 3 * hid)), full((hid, 1)),
            full((hid, 1)),
            full((2 * LANES, 4 * HYENA_WIDTH)), full((LANES, 4 * HYENA_WIDTH)),
            full((1, HYENA_WIDTH)),
        ],
        out_specs=pl.BlockSpec((4 * N_CBLK, CHUNK, TILE_ROWS, LANES), lambda c: (0, c, 0, 0)),
        out_shape=jax.ShapeDtypeStruct((4 * N_CBLK, N_TILES, TILE_ROWS, LANES), BF16),
        compiler_params=pltpu.CompilerParams(dimension_semantics=("arbitrary",), vmem_limit_bytes=VMEM_LIMIT),
        name="filter_mlp",
    )(zfeat_t, w1s, col(b1, hid), w2s, col(b2, hid), w3s, col(b3, hid), col(freq, hid), wp2, wp_lo, abs_deltas)


def _filter_spec_kernel(fa_ref, fb_ref, ba_ref, bb_ref, d_ref, f1f_ref, gf_ref, k_ref, a_scr):
    def stage1(sb, carry):
        for i in range(HY_SUB):
            n2 = sb * HY_SUB + i
            nb = (N_TILES - n2) % N_TILES
            fwd = jnp.concatenate([fa_ref[n2], fb_ref[n2]], axis=-1)
            bwd = jnp.concatenate([ba_ref[nb], bb_ref[nb]], axis=-1)
            taps = jnp.concatenate([fwd, bwd], axis=0)
            _stage_store(a_scr, _tile_rows(n2), jnp.dot(f1f_ref[n2], taps, preferred_element_type=F32))
        return carry

    lax.fori_loop(0, N_SUB, stage1, 0)

    d = d_ref[...]

    def batch(slots, first):
        ys = _stage2_forward(a_scr, gf_ref, slots, 0 if first else 1)
        for j, (s, y) in enumerate(zip(slots, ys)):
            blocks = []
            for i, (yr, yi) in enumerate(_re_im_blocks(y)):
                yr = yr + d
                if first and j == 0 and i == 0:
                    rows = lax.broadcasted_iota(jnp.int32, yi.shape, 0)
                    yi = yi + jnp.where(rows == 0, d, 0.0)
                blocks += [yr, yi]
            spec = jnp.concatenate(blocks, axis=0).astype(BF16)
            k_ref[0, s] = spec[:, :LANES]
            k_ref[1, s] = spec[:, LANES:]

    batch(_batch_slots(0), True)

    def loop(i, carry):
        batch(_batch_slots(i), False)
        return carry

    lax.fori_loop(1, N_BATCHES, loop, 0, unroll=3)


def _filter_spec(taps, hyena_d3, f1f, gf):
    grid = (2, N_CBLK // 2)
    const = lambda shape: pl.BlockSpec(shape, lambda o, c: (0,) * len(shape), pipeline_mode=pl.Buffered(1))

    def taps_spec(direction, k):
        return pl.BlockSpec((None, N_TILES, TILE_ROWS, LANES),
                            lambda o, c: ((2 * o + direction) * N_CBLK + 2 * c + k, 0, 0, 0))

    return pl.pallas_call(
        _filter_spec_kernel,
        grid=grid,
        in_specs=[taps_spec(0, 0), taps_spec(0, 1), taps_spec(1, 0), taps_spec(1, 1),
                  pl.BlockSpec((None, 1, 2 * LANES), lambda o, c: (o, 0, c)),
                  const((N_TILES, 128, 128)), const((2, 256, 256))],
        out_specs=pl.BlockSpec((None, 2, N_SLOTS, 256, LANES), lambda o, c: (o, c, 0, 0, 0)),
        out_shape=jax.ShapeDtypeStruct((2, N_CBLK, N_SLOTS, 256, LANES), BF16),
        scratch_shapes=[pltpu.VMEM((2, N_TILES * A_STRIDE, LANES), jnp.uint32)],
        compiler_params=pltpu.CompilerParams(
            dimension_semantics=("arbitrary", "arbitrary"), vmem_limit_bytes=VMEM_LIMIT),
        name="filter_spec",
    )(taps, taps, taps, taps, hyena_d3, f1f, gf)


_T_S1 = 0
_T_F0 = 1
_T_M = 2
_T_F1 = 3
_T_E = 4
_T_END = 5
N_SUB = N_TILES // HY_SUB
MID_GROUP = 8


def _hyena_kernel(cin_ref, pz_ref, cw_ref, k_ref, f1_ref, f1i_ref, gf_ref, gb_ref, o_ref, a_scr, z_scr):
    t = pl.program_id(2)

    def stage1(n2, u_bf):
        _stage_store(a_scr, _tile_rows(n2), jnp.dot(f1_ref[n2], u_bf, preferred_element_type=F32))

    def inv_stage1(n2):
        return jnp.dot(f1i_ref[n2], _stage_load(a_scr, _tile_rows(n2)), preferred_element_type=F32)

    def short_conv(sb, row0):
        base = sb * HY_SUB
        first = _pair(cin_ref, (base + N_TILES - 1) % N_TILES)
        first = jnp.where(sb == 0, _shift_down(first), first)
        last = _pair(cin_ref, (base + HY_SUB) % N_TILES)
        last = jnp.where(sb == N_SUB - 1, _shift_up(last), last)
        tiles = [first] + [_pair(cin_ref, base + i) for i in range(HY_SUB)] + [last]
        w = cw_ref[row0:row0 + 4, :]
        w0, w1, w2, b = _dup(w[0:1]), _dup(w[1:2]), _dup(w[2:3]), _dup(w[3:4])
        return [tiles[i] * w0 + tiles[i + 1] * w1 + tiles[i + 2] * w2 + b for i in range(HY_SUB)]

    def filter_multiply(slots, ys, first, buf):
        for j, (s, y) in enumerate(zip(slots, ys)):
            blocks = []
            for i, (yr, yi) in enumerate(_re_im_blocks(y)):
                kblk = k_ref[s, 16 * i:16 * i + 16, :].astype(F32)
                kr, ki = kblk[:8], kblk[8:]
                if first and j == 0 and i == 0:
                    rows = lax.broadcasted_iota(jnp.int32, kr.shape, 0)
                    ka, kb, kd = kr, jnp.where(rows == 0, 0.0, ki), jnp.where(rows == 0, ki, kr)
                else:
                    ka, kb, kd = kr, ki, kr
                ka, kb, kd = _dup(ka), _dup(kb), _dup(kd)
                blocks += [yr * ka - yi * kb, yr * kb + yi * kd]
            z_scr[buf, j] = jnp.concatenate(blocks, axis=0).astype(BF16)

    def stage2_inverse(slots, first_sel, buf):
        bms = [jnp.dot(gb_ref[first_sel] if j == 0 else gb_ref[1], z_scr[buf, j], preferred_element_type=F32)
               for j in range(SLOT_BATCH)]
        for s, bm in zip(slots, bms):
            _stage_store(a_scr, _slot_rows(s), bm)

    def spectral_phase():
        slots0 = _batch_slots(0)
        filter_multiply(slots0, _stage2_forward(a_scr, gf_ref, slots0, 0), True, 0)

        def loop(i, carry):
            slots = _batch_slots(i)
            ys = _stage2_forward(a_scr, gf_ref, slots, 1)
            stage2_inverse(_batch_slots(i - 1), jnp.where(i == 1, 0, 1), (i - 1) % 2)
            filter_multiply(slots, ys, False, i % 2)
            return carry

        lax.fori_loop(1, N_BATCHES, loop, 0, unroll=5)
        stage2_inverse(_batch_slots(N_BATCHES - 1), 1, (N_BATCHES - 1) % 2)

    @pl.when(t == _T_S1)
    def _():
        def sub(sb, carry):
            vs = short_conv(sb, 0)
            for i in range(HY_SUB):
                stage1(sb * HY_SUB + i, vs[i].astype(BF16))
            return carry

        lax.fori_loop(0, N_SUB, sub, 0)

    @pl.when((t == _T_F0) | (t == _T_F1))
    def _():
        spectral_phase()

    @pl.when(t == _T_M)
    def _():
        def sub(sb, carry):
            gates = short_conv(sb, 4)
            for q in range(0, HY_SUB, MID_GROUP):
                ys = [inv_stage1(sb * HY_SUB + q + i) for i in range(MID_GROUP)]
                us = [(gates[q + i] * ys[i]).astype(BF16) for i in range(MID_GROUP)]
                for i in range(MID_GROUP):
                    stage1(sb * HY_SUB + q + i, us[i])
            return carry

        lax.fori_loop(0, N_SUB, sub, 0)

    @pl.when(t == _T_E)
    def _():
        def sub(sb, carry):
            gates = short_conv(sb, 8)
            for i in range(HY_SUB):
                n2 = sb * HY_SUB + i
                res = (gates[i] * inv_stage1(n2) * _pair(pz_ref, n2)).astype(BF16)
                o_ref[0, n2] = res[:, :LANES]
                o_ref[1, n2] = res[:, LANES:]
            return carry

        lax.fori_loop(0, N_SUB, sub, 0)


def _hyena(p5, conv_w, conv_b, kspec, f1, f1i, gf, gb):
    grid = (N_CBLK, BATCH // 2, _T_END)
    cw = jnp.concatenate(
        [jnp.concatenate([conv_w[:, k * HYENA_WIDTH:(k + 1) * HYENA_WIDTH],
                          conv_b[:, k * HYENA_WIDTH:(k + 1) * HYENA_WIDTH]], axis=0) for k in range(3)], axis=0)
    seq_block = (2, None, N_TILES, TILE_ROWS, LANES)

    def conv_in_map(c, b, t):
        return (b, jnp.where(t < _T_M, 0, jnp.where(t < _T_E, 1, 2)) * N_CBLK + c, 0, 0, 0)

    def z_map(c, b, t):
        flat = c * (BATCH // 2) + b
        sel = jnp.where(t >= _T_M, flat, jnp.maximum(flat - 1, 0))
        return (sel % (BATCH // 2), 3 * N_CBLK + sel // (BATCH // 2), 0, 0, 0)

    const = lambda shape: pl.BlockSpec(shape, lambda c, b, t: (0,) * len(shape), pipeline_mode=pl.Buffered(1))
    in_specs = [
        pl.BlockSpec(seq_block, conv_in_map),
        pl.BlockSpec(seq_block, z_map),
        pl.BlockSpec((12, LANES), lambda c, b, t: (0, c)),
        pl.BlockSpec((None, None, N_SLOTS, 256, LANES), lambda c, b, t: (jnp.where(t >= _T_M, 1, 0), c, 0, 0, 0)),
        const((N_TILES, 128, TILE_ROWS)), const((N_TILES, TILE_ROWS, 128)),
        const((2, 256, 256)), const((2, 256, 256)),
    ]
    return pl.pallas_call(
        _hyena_kernel,
        grid=grid,
        in_specs=in_specs,
        out_specs=pl.BlockSpec(seq_block, lambda c, b, t: (b, c, 0, 0, 0)),
        out_shape=jax.ShapeDtypeStruct((BATCH, N_CBLK, N_TILES, TILE_ROWS, LANES), BF16),
        scratch_shapes=[pltpu.VMEM((2, N_TILES * A_STRIDE, LANES), jnp.uint32),
                        pltpu.VMEM((2, SLOT_BATCH, 256, 2 * LANES), BF16)],
        compiler_params=pltpu.CompilerParams(
            dimension_semantics=("arbitrary", "arbitrary", "arbitrary"), vmem_limit_bytes=VMEM_LIMIT),
        name="hyena",
    )(p5, p5, cw, kspec, f1, f1i, gf, gb)


_POOL_GROUP = 8


def _pool_kernel(u_ref, z_ref, pw_ref, ps_ref, o_ref):
    g = pl.program_id(1)
    n_groups = N_TILES // _POOL_GROUP
    n1 = lax.broadcasted_iota(jnp.int32, (TILE_ROWS, LANES), 0)

    def tile(idx):
        if isinstance(idx, int) and idx < 0:
            return _shift_down(u_ref[idx + N_TILES].astype(F32))
        if isinstance(idx, int) and idx >= N_TILES:
            return _shift_up(u_ref[idx - N_TILES].astype(F32))
        return u_ref[idx].astype(F32)

    for gi_, w in enumerate(POOL_WINDOWS):
        @pl.when(g == gi_)
        def _(w=w):
            lo, hi = w // 2, w - 1 - w // 2

            def run_group(base, window_sum, edge):
                pooled = []
                for i in range(_POOL_GROUP):
                    n2 = base + i
                    if i > 0:
                        window_sum = window_sum + tile(n2 + hi) - tile(n2 - 1 - lo)
                    if edge:
                        pos = 128 * n1 + n2
                        cnt = (jnp.minimum(pos + hi, SEQ - 1) - jnp.maximum(pos - lo, 0) + 1).astype(F32)
                        mean = window_sum / cnt
                    else:
                        mean = window_sum * (1.0 / w)
                    pooled.append((mean - tile(n2)).astype(BF16))
                pooled = jnp.concatenate(pooled, axis=0)
                y = jnp.dot(pooled, pw_ref[...], preferred_element_type=F32) * ps_ref[...]
                for i in range(_POOL_GROUP):
                    n2 = base + i
                    o_ref[n2] = (y[i * TILE_ROWS:(i + 1) * TILE_ROWS] * z_ref[n2].astype(F32)).astype(BF16)
                nxt = base + _POOL_GROUP
                return window_sum + tile(nxt + hi) - tile(nxt - 1 - lo)

            first = tile(-lo)
            for d in range(-lo + 1, hi + 1):
                first = first + tile(d)
            carry = run_group(0, first, True)
            carry = lax.fori_loop(1, n_groups - 1,
                                  lambda c, ws: run_group(c * _POOL_GROUP, ws, False), carry, unroll=2)
            run_group(N_TILES - _POOL_GROUP, carry, True)


def _pool(p5, pool_w_bf, pool_scale):
    u0 = 4 * N_CBLK
    z0 = 5 * N_CBLK
    return pl.pallas_call(
        _pool_kernel,
        grid=(BATCH, len(POOL_WINDOWS)),
        in_specs=[
            pl.BlockSpec((None, None, N_TILES, TILE_ROWS, LANES), lambda b, g: (b, u0 + g, 0, 0, 0)),
            pl.BlockSpec((None, None, N_TILES, TILE_ROWS, LANES), lambda b, g: (b, z0 + g, 0, 0, 0)),
            pl.BlockSpec((None, LANES, LANES), lambda b, g: (g, 0, 0)),
            pl.BlockSpec((1, LANES), lambda b, g: (0, g)),
        ],
        out_specs=pl.BlockSpec((None, None, N_TILES, TILE_ROWS, LANES), lambda b, g: (b, g, 0, 0, 0)),
        out_shape=jax.ShapeDtypeStruct((BATCH, N_CBLK, N_TILES, TILE_ROWS, LANES), BF16),
        compiler_params=pltpu.CompilerParams(
            dimension_semantics=("arbitrary", "arbitrary"), vmem_limit_bytes=VMEM_LIMIT),
        name="pool",
    )(p5, p5, pool_w_bf, pool_scale)


def _out_kernel(yh_ref, yp_ref, x_ref, gh_ref, gp_ref, w_ref, gpost_ref, o_ref, r_scr):
    def group_norm(ref, gain_ref, j0):
        rows = []
        for j in range(j0, j0 + SUB_TILES):
            rows.append(jnp.concatenate([ref[cb, j] for cb in range(N_CBLK)], axis=-1).astype(F32))
        y = jnp.concatenate(rows, axis=0)
        ms = jnp.mean(y * y, axis=-1, keepdims=True)
        return (y * lax.rsqrt(ms + EPS) * gain_ref[...]).astype(BF16)

    for sb in range(OUT_TILES // SUB_TILES):
        j0 = sb * SUB_TILES
        yc = jnp.concatenate([group_norm(yh_ref, gh_ref, j0), group_norm(yp_ref, gp_ref, j0)], axis=-1)
        out = jnp.dot(yc, w_ref[...], preferred_element_type=F32)
        ms = jnp.mean(out * out, axis=-1, keepdims=True)
        out = out * lax.rsqrt(ms + EPS) * gpost_ref[...]
        for j in range(SUB_TILES):
            for k in range(D_MODEL // LANES):
                r_scr[sb, k, pl.ds(j, TILE_ROWS, stride=SUB_TILES), :] = out[j * TILE_ROWS:(j + 1) * TILE_ROWS,
                                                                             k * LANES:(k + 1) * LANES]
        r = jnp.concatenate([r_scr[sb, k] for k in range(D_MODEL // LANES)], axis=-1)
        o_ref[:, j0:j0 + SUB_TILES, :] = (x_ref[:, j0:j0 + SUB_TILES, :]
                                          + r.reshape(TILE_ROWS, SUB_TILES, D_MODEL))


def _out_proj(yh, yp, x4, norm_h_g, norm_p_g, w_out_bf, post_g):
    grid = (BATCH, N_TILES // OUT_TILES)
    y_spec = pl.BlockSpec((None, N_CBLK, OUT_TILES, TILE_ROWS, LANES), lambda b, g: (b, 0, g, 0, 0))
    x_spec = pl.BlockSpec((None, TILE_ROWS, OUT_TILES, D_MODEL), lambda b, g: (b, 0, g, 0))
    return pl.pallas_call(
        _out_kernel,
        grid=grid,
        in_specs=[
            y_spec, y_spec, x_spec,
            pl.BlockSpec((1, HYENA_WIDTH), lambda b, g: (0, 0)),
            pl.BlockSpec((1, POOL_WIDTH), lambda b, g: (0, 0)),
            pl.BlockSpec((D_MODEL, D_MODEL), lambda b, g: (0, 0)),
            pl.BlockSpec((1, D_MODEL), lambda b, g: (0, 0)),
        ],
        out_specs=x_spec,
        out_shape=jax.ShapeDtypeStruct((BATCH, TILE_ROWS, N_TILES, D_MODEL), F32),
        scratch_shapes=[pltpu.VMEM((OUT_TILES // SUB_TILES, D_MODEL // LANES, TILE_ROWS * SUB_TILES, LANES), F32)],
        compiler_params=pltpu.CompilerParams(
            dimension_semantics=("arbitrary", "arbitrary"), vmem_limit_bytes=VMEM_LIMIT),
        name="out_proj",
    )(yh, yp, x4, norm_h_g, norm_p_g, w_out_bf, post_g)


def kernel(x, pre_norm_g, w_in, conv_w, conv_b, filt_w1, filt_b1, filt_w2, filt_b2, filt_w3, filt_b3,
           filt_freq, filt_w_out, hyena_d, pool_w, pool_scale, norm_h_g, norm_p_g, w_out, post_norm_g):
    assert x.shape == (BATCH, SEQ, D_MODEL) and pre_norm_g.shape[0] == 1
    f1, f1i, gf, gb, f1f = (jnp.asarray(m, F32).astype(BF16) for m in (_F1, _F1I, _GF, _GB, _F1F))

    x4 = x.reshape(BATCH, TILE_ROWS, N_TILES, D_MODEL)
    p5 = _in_proj(x4, pre_norm_g, w_in[0].astype(BF16))

    taps = _filter_mlp(jnp.asarray(_ZFEAT), filt_w1[0], filt_b1[0], filt_w2[0], filt_b2[0], filt_w3[0], filt_b3[0],
                       filt_freq[0], filt_w_out[0], jnp.asarray(_ABS_DELTAS))
    kspec = _filter_spec(taps, hyena_d[0].reshape(2, 1, HYENA_WIDTH), f1f, gf)

    yh = _hyena(p5, conv_w[0], conv_b, kspec, f1, f1i, gf, gb)
    yp = _pool(p5, pool_w[0].astype(BF16), pool_scale)
    out4 = _out_proj(yh, yp, x4, norm_h_g, norm_p_g, w_out[0].astype(BF16), post_norm_g)
    return out4.reshape(BATCH, SEQ, D_MODEL)
```

```python
import functools
import math

import numpy as np
import jax
import jax.numpy as jnp
from jax import lax
from jax.experimental import pallas as pl
from jax.experimental.pallas import tpu as pltpu

F32 = jnp.float32
BF16 = jnp.bfloat16

D_MODEL = 1024
BATCH = 4
SEQ = 8192
HYENA_WIDTH = 512
POOL_WIDTH = 512
POOL_WINDOWS = (2, 4, 8, 16)
FILTER_EMB = 33
FILTER_BANDS = 16
FILTER_HIDDEN = 64
PROJ_WIDTH = 3072
EPS = 1e-6

LANES = 128
N_FFT = 2 * SEQ
N_TILES = 128
TILE_ROWS = SEQ // N_TILES
N_SLOTS = 64
CHUNK = 32
HY_SUB = 32
N_CHUNKS = N_TILES // CHUNK
A_STRIDE = 72
SLOT_BATCH = 4
N_CBLK = HYENA_WIDTH // LANES
PROJ_BLOCKS = PROJ_WIDTH // LANES
IN_TILES = 16
OUT_TILES = 16
SUB_TILES = 8
VMEM_LIMIT = 60 * 1024 * 1024


def _dft_tables():
    n1 = np.arange(TILE_ROWS)
    n2 = np.arange(N_TILES)
    s = np.arange(N_SLOTS)
    ph = 2 * np.pi * (n2[:, None, None] * s[None, :, None] / N_FFT
                      + n1[None, None, :] * s[None, :, None] / 128.0)
    f1 = np.zeros((N_TILES, 128, TILE_ROWS))
    f1[:, :64, :] = np.cos(ph)
    f1[:, 64:, :] = -np.sin(ph)
    f1[:, 0, :] = 1.0
    f1[:, 64, :] = (-1.0) ** n1
    php = np.transpose(ph, (0, 2, 1))
    f1i = np.zeros((N_TILES, TILE_ROWS, 128))
    f1i[:, :, :64] = 2 * np.cos(php) / N_FFT
    f1i[:, :, 64:] = -2 * np.sin(php) / N_FFT
    f1i[:, :, 0] = 1.0 / N_FFT
    f1i[:, :, 64] = ((-1.0) ** n1)[None, :] / N_FFT
    k2 = np.arange(128)
    th = 2 * np.pi * np.outer(k2, n2) / 128.0
    c, sn = np.cos(th), np.sin(th)
    g = np.block([[c, sn], [-sn, c]])
    gi = np.block([[c, -sn], [sn, c]])
    kk = np.arange(64)
    tha = 2 * np.pi * np.outer(kk, n2) / 128.0
    thb = 2 * np.pi * np.outer(64 + 128 * kk, n2) / N_FFT
    g0 = np.zeros((256, 256))
    g0[0:64, 0:128] = np.cos(tha)
    g0[64:128, 128:256] = np.cos(thb)
    g0[128:192, 0:128] = -np.sin(tha)
    g0[128, 0:128] = (-1.0) ** n2
    g0[192:256, 128:256] = -np.sin(thb)
    g0i = np.zeros((256, 256))
    g0i[0:128, 0:64] = 2 * np.cos(tha.T)
    g0i[0:128, 0] = 1.0
    g0i[0:128, 128:192] = -2 * np.sin(tha.T)
    g0i[0:128, 128] = (-1.0) ** n2
    g0i[128:256, 64:128] = 2 * np.cos(thb.T)
    g0i[128:256, 192:256] = -2 * np.sin(thb.T)
    q = np.arange(256)
    perm = np.where(q % 16 < 8, 8 * (q // 16) + q % 16, 128 + 8 * (q // 16) + q % 16 - 8)
    gf = np.stack([g0[perm, :], g[perm, :]])
    gb = np.stack([g0i[:, perm], gi[:, perm]])
    n1f = np.arange(128)
    phf = 2 * np.pi * (n2[:, None, None] * s[None, :, None] / N_FFT
                       + n1f[None, None, :] * s[None, :, None] / 128.0)
    full = np.zeros((N_TILES, 128, 128))
    full[:, :64, :] = np.cos(phf)
    full[:, 64:, :] = -np.sin(phf)
    full[:, 0, :] = 1.0
    full[:, 64, :] = (-1.0) ** n1f
    rev = full[:, :, 127:63:-1].copy()
    rev[0, :, 1:] = full[0, :, 127:64:-1]
    rev[0, :, 0] = 0.0
    f1f = np.concatenate([full[:, :, :64], rev], axis=2)
    pair64 = np.arange(128) % 2 * 64 + np.arange(128) // 2
    pair128 = np.arange(256) % 2 * 128 + np.arange(256) // 2
    f1, f1f = f1[:, pair64, :], f1f[:, pair64, :]
    f1i = f1i[:, :, pair64]
    gf = gf[:, :, pair128]
    gb = gb[:, pair128, :]
    return f1, f1i, gf, gb, f1f


def _filter_features():
    pos = np.arange(SEQ, dtype=np.float64)
    t = pos / (SEQ - 1)
    ang = 2.0 * math.pi * pos / SEQ
    bands = np.linspace(1e-4, FILTER_BANDS - 1, FILTER_BANDS)
    z = np.concatenate([t[:, None], np.cos(bands[None, :] * ang[:, None]),
                        -np.sin(bands[None, :] * ang[:, None])], axis=-1)
    z = z.reshape(TILE_ROWS, N_TILES, FILTER_EMB).transpose(1, 0, 2).reshape(SEQ, FILTER_EMB)
    zp = np.zeros((FILTER_HIDDEN, SEQ))
    zp[:FILTER_EMB, :] = z.T
    max_decay = math.log(1e-2) / 0.3
    min_decay = math.log(1e-2) / 1.5
    deltas = np.abs(np.linspace(min_decay, max_decay, HYENA_WIDTH))
    return zp.astype(np.float32), deltas.astype(np.float32)[None, :]


_F1, _F1I, _GF, _GB, _F1F = _dft_tables()
_ZFEAT, _ABS_DELTAS = _filter_features()


def _shift_down(x):
    rows = lax.broadcasted_iota(jnp.int32, x.shape, 0)
    return jnp.where(rows == 0, 0.0, pltpu.roll(x, 1, axis=0))


def _shift_up(x):
    rows = lax.broadcasted_iota(jnp.int32, x.shape, 0)
    return jnp.where(rows == x.shape[0] - 1, 0.0, pltpu.roll(x, x.shape[0] - 1, axis=0))


def _pair(ref, i):
    return jnp.concatenate([ref[0, i], ref[1, i]], axis=-1).astype(F32)


def _dup(x):
    return jnp.concatenate([x, x], axis=-1)


def _silu(z):
    hz = 0.5 * z
    return hz * (1.0 + jnp.tanh(hz))


def _tile_rows(n2):
    return pl.ds(pl.multiple_of(n2 * A_STRIDE, 8), N_SLOTS)


def _slot_rows(s):
    return pl.ds(s, N_TILES, stride=A_STRIDE)


def _stage_load(a_ref, rows):
    words = jnp.concatenate([a_ref[0, rows, :], a_ref[1, rows, :]], axis=-1)
    return pltpu.bitcast(words, BF16)


def _stage_store(a_ref, rows, val):
    words = pltpu.bitcast(val.astype(BF16), jnp.uint32)
    a_ref[0, rows, :] = words[:, :LANES]
    a_ref[1, rows, :] = words[:, LANES:]


N_BATCHES = N_SLOTS // SLOT_BATCH
SPEC_BLOCKS = 16


def _batch_slots(i):
    return [i * SLOT_BATCH + j for j in range(SLOT_BATCH)]


def _stage2_forward(a_ref, gf_ref, slots, sel0):
    xs = [_stage_load(a_ref, _slot_rows(s)) for s in slots]
    return [jnp.dot(gf_ref[sel0] if j == 0 else gf_ref[1], x, preferred_element_type=F32)
            for j, x in enumerate(xs)]


def _re_im_blocks(y):
    return [(y[16 * i:16 * i + 8], y[16 * i + 8:16 * i + 16]) for i in range(SPEC_BLOCKS)]


_GATE_CHUNKS = (3, 5)


def _in_proj_kernel(x_ref, g_ref, w_ref, o_ref, h_scr):
    ncol = 512
    for sb in range(IN_TILES // SUB_TILES):
        j0 = sb * SUB_TILES
        x = x_ref[:, j0:j0 + SUB_TILES, :].reshape(TILE_ROWS * SUB_TILES, D_MODEL)
        ms = jnp.mean(x * x, axis=-1, keepdims=True)
        hn = x * lax.rsqrt(ms + EPS) * g_ref[...]
        for k in range(D_MODEL // LANES):
            h_scr[sb, k] = hn[:, k * LANES:(k + 1) * LANES]
        h = jnp.concatenate(
            [jnp.concatenate([h_scr[sb, k, pl.ds(j, TILE_ROWS, stride=SUB_TILES), :]
                              for k in range(D_MODEL // LANES)], axis=-1).astype(BF16)
             for j in range(SUB_TILES)], axis=0)
        for c in range(PROJ_WIDTH // ncol):
            p = jnp.dot(h, w_ref[:, c * ncol:(c + 1) * ncol], preferred_element_type=F32)
            if c in _GATE_CHUNKS:
                p = _silu(p)
            p = p.astype(BF16)
            for cb in range(ncol // LANES):
                for j in range(SUB_TILES):
                    o_ref[c * (ncol // LANES) + cb, j0 + j] = p[j * TILE_ROWS:(j + 1) * TILE_ROWS,
                                                                cb * LANES:(cb + 1) * LANES]


def _in_proj(x4, pre_g, w_in_bf):
    grid = (BATCH, N_TILES // IN_TILES)
    return pl.pallas_call(
        _in_proj_kernel,
        grid=grid,
        in_specs=[
            pl.BlockSpec((None, TILE_ROWS, IN_TILES, D_MODEL), lambda b, g: (b, 0, g, 0)),
            pl.BlockSpec((1, D_MODEL), lambda b, g: (0, 0)),
            pl.BlockSpec((D_MODEL, PROJ_WIDTH), lambda b, g: (0, 0)),
        ],
        out_specs=pl.BlockSpec((None, PROJ_BLOCKS, IN_TILES, TILE_ROWS, LANES), lambda b, g: (b, 0, g, 0, 0)),
        out_shape=jax.ShapeDtypeStruct((BATCH, PROJ_BLOCKS, N_TILES, TILE_ROWS, LANES), BF16),
        scratch_shapes=[pltpu.VMEM((IN_TILES // SUB_TILES, D_MODEL // LANES, TILE_ROWS * SUB_TILES, LANES), F32)],
        compiler_params=pltpu.CompilerParams(
            dimension_semantics=("arbitrary", "arbitrary"), vmem_limit_bytes=VMEM_LIMIT),
        name="in_proj",
    )(x4, pre_g, w_in_bf)


def _split_bf16(x):
    hi = x.astype(BF16)
    return hi, (x - hi.astype(F32)).astype(BF16)


def _stack_weight_rows(w):
    hi, lo = _split_bf16(w)
    return jnp.concatenate([hi, lo, hi], axis=1)


def _filter_mlp_kernel(z_ref, w1_ref, b1_ref, w2_ref, b2_ref, w3_ref, b3_ref, fr_ref,
                       wp2_ref, wpl_ref, dl_ref, o_ref):
    def layer(w_ref, b_ref, f_ref, h):
        hi, lo = _split_bf16(h)
        pre = jnp.dot(w_ref[...], jnp.concatenate([hi, hi, lo], axis=0), preferred_element_type=F32)
        return jnp.sin(f_ref[...] * (pre + b_ref[...]))

    h = layer(w1_ref, b1_ref, fr_ref, z_ref[...])
    h = layer(w2_ref, b2_ref, fr_ref, h)
    h = layer(w3_ref, b3_ref, fr_ref, h)
    h = jnp.concatenate([h, jnp.zeros_like(h)], axis=0)
    hi, lo = _split_bf16(h.T)
    hs = jnp.concatenate([hi, lo], axis=1)
    rows = CHUNK * TILE_ROWS
    r = lax.broadcasted_iota(jnp.int32, (rows, HYENA_WIDTH), 0)
    pos = 128 * (r % TILE_ROWS) + CHUNK * pl.program_id(0) + r // TILE_ROWS
    t = pos.astype(F32) / float(SEQ - 1)
    decay = jnp.exp(-t * dl_ref[...])
    for od in range(4):
        cols = slice(od * HYENA_WIDTH, (od + 1) * HYENA_WIDTH)
        taps = (jnp.dot(hs, wp2_ref[:, cols], preferred_element_type=F32)
                + jnp.dot(hi, wpl_ref[:, cols], preferred_element_type=F32)) * decay
        taps = taps.astype(BF16)
        for cb in range(N_CBLK):
            for i in range(CHUNK):
                o_ref[od * N_CBLK + cb, i] = taps[i * TILE_ROWS:(i + 1) * TILE_ROWS, cb * LANES:(cb + 1) * LANES]


def _filter_mlp(zfeat_t, w1, b1, w2, b2, w3, b3, freq, w_proj, abs_deltas):
    hid = FILTER_HIDDEN
    col = lambda v, n: jnp.pad(v.reshape(-1, 1), ((0, n - v.size), (0, 0)))
    w1s = _stack_weight_rows(jnp.pad(w1, ((0, hid - FILTER_EMB), (0, 0))).T)
    w2s = _stack_weight_rows(w2.T)
    w3s = _stack_weight_rows(w3.T)
    wp_hi, wp_lo = _split_bf16(jnp.pad(w_proj, ((0, LANES - hid), (0, 0))))
    wp2 = jnp.concatenate([wp_hi, wp_hi], axis=0)
    full = lambda shape: pl.BlockSpec(shape, lambda c: (0,) * len(shape))
    cols = CHUNK * TILE_ROWS
    return pl.pallas_call(
        _filter_mlp_kernel,
        grid=(N_CHUNKS,),
        in_specs=[
            pl.BlockSpec((hid, cols), lambda c: (0, c)),
            full((hid, 3 * hid)), full((hid, 1)),
            full((hid, 3 * hid)), full((hid, 1)),
            full((hid, 3 * hid)), full((hid, 1)),
            full((hid, 1)),
            full((2 * LANES, 4 * HYENA_WIDTH)), full((LANES, 4 * HYENA_WIDTH)),
            full((1, HYENA_WIDTH)),
        ],
        out_specs=pl.BlockSpec((4 * N_CBLK, CHUNK, TILE_ROWS, LANES), lambda c: (0, c, 0, 0)),
        out_shape=jax.ShapeDtypeStruct((4 * N_CBLK, N_TILES, TILE_ROWS, LANES), BF16),
        compiler_params=pltpu.CompilerParams(dimension_semantics=("arbitrary",), vmem_limit_bytes=VMEM_LIMIT),
        name="filter_mlp",
    )(zfeat_t, w1s, col(b1, hid), w2s, col(b2, hid), w3s, col(b3, hid), col(freq, hid), wp2, wp_lo, abs_deltas)


def _filter_spec_kernel(fa_ref, fb_ref, ba_ref, bb_ref, d_ref, f1f_ref, gf_ref, k_ref, a_scr):
    def stage1(sb, carry):
        for i in range(HY_SUB):
            n2 = sb * HY_SUB + i
            nb = (N_TILES - n2) % N_TILES
            fwd = jnp.concatenate([fa_ref[n2], fb_ref[n2]], axis=-1)
            bwd = jnp.concatenate([ba_ref[nb], bb_ref[nb]], axis=-1)
            taps = jnp.concatenate([fwd, bwd], axis=0)
            _stage_store(a_scr, _tile_rows(n2), jnp.dot(f1f_ref[n2], taps, preferred_element_type=F32))
        return carry

    lax.fori_loop(0, N_SUB, stage1, 0)

    d = d_ref[...]

    def batch(slots, first):
        ys = _stage2_forward(a_scr, gf_ref, slots, 0 if first else 1)
        for j, (s, y) in enumerate(zip(slots, ys)):
            blocks = []
            for i, (yr, yi) in enumerate(_re_im_blocks(y)):
                yr = yr + d
                if first and j == 0 and i == 0:
                    rows = lax.broadcasted_iota(jnp.int32, yi.shape, 0)
                    yi = yi + jnp.where(rows == 0, d, 0.0)
                blocks += [yr, yi]
            spec = jnp.concatenate(blocks, axis=0).astype(BF16)
            k_ref[0, s] = spec[:, :LANES]
            k_ref[1, s] = spec[:, LANES:]

    batch(_batch_slots(0), True)

    def loop(i, carry):
        batch(_batch_slots(i), False)
        return carry

    lax.fori_loop(1, N_BATCHES, loop, 0, unroll=3)


def _filter_spec(taps, hyena_d3, f1f, gf):
    grid = (2, N_CBLK // 2)
    const = lambda shape: pl.BlockSpec(shape, lambda o, c: (0,) * len(shape), pipeline_mode=pl.Buffered(1))

    def taps_spec(direction, k):
        return pl.BlockSpec((None, N_TILES, TILE_ROWS, LANES),
                            lambda o, c: ((2 * o + direction) * N_CBLK + 2 * c + k, 0, 0, 0))

    return pl.pallas_call(
        _filter_spec_kernel,
        grid=grid,
        in_specs=[taps_spec(0, 0), taps_spec(0, 1), taps_spec(1, 0), taps_spec(1, 1),
                  pl.BlockSpec((None, 1, 2 * LANES), lambda o, c: (o, 0, c)),
                  const((N_TILES, 128, 128)), const((2, 256, 256))],
        out_specs=pl.BlockSpec((None, 2, N_SLOTS, 256, LANES), lambda o, c: (o, c, 0, 0, 0)),
        out_shape=jax.ShapeDtypeStruct((2, N_CBLK, N_SLOTS, 256, LANES), BF16),
        scratch_shapes=[pltpu.VMEM((2, N_TILES * A_STRIDE, LANES), jnp.uint32)],
        compiler_params=pltpu.CompilerParams(
            dimension_semantics=("arbitrary", "arbitrary"), vmem_limit_bytes=VMEM_LIMIT),
        name="filter_spec",
    )(taps, taps, taps, taps, hyena_d3, f1f, gf)


_T_S1 = 0
_T_F0 = 1
_T_M = 2
_T_F1 = 3
_T_E = 4
_T_END = 5
N_SUB = N_TILES // HY_SUB
MID_GROUP = 8


def _hyena_kernel(cin_ref, pz_ref, cw_ref, k_ref, f1_ref, f1i_ref, gf_ref, gb_ref, o_ref, a_scr, z_scr):
    t = pl.program_id(2)

    def stage1(n2, u_bf):
        _stage_store(a_scr, _tile_rows(n2), jnp.dot(f1_ref[n2], u_bf, preferred_element_type=F32))

    def inv_stage1(n2):
        return jnp.dot(f1i_ref[n2], _stage_load(a_scr, _tile_rows(n2)), preferred_element_type=F32)

    def short_conv(sb, row0):
        base = sb * HY_SUB
        first = _pair(cin_ref, (base + N_TILES - 1) % N_TILES)
        first = jnp.where(sb == 0, _shift_down(first), first)
        last = _pair(cin_ref, (base + HY_SUB) % N_TILES)
        last = jnp.where(sb == N_SUB - 1, _shift_up(last), last)
        tiles = [first] + [_pair(cin_ref, base + i) for i in range(HY_SUB)] + [last]
        w = cw_ref[row0:row0 + 4, :]
        w0, w1, w2, b = _dup(w[0:1]), _dup(w[1:2]), _dup(w[2:3]), _dup(w[3:4])
        return [tiles[i] * w0 + tiles[i + 1] * w1 + tiles[i + 2] * w2 + b for i in range(HY_SUB)]

    def filter_multiply(slots, ys, first, buf):
        for j, (s, y) in enumerate(zip(slots, ys)):
            blocks = []
            for i, (yr, yi) in enumerate(_re_im_blocks(y)):
                kblk = k_ref[s, 16 * i:16 * i + 16, :].astype(F32)
                kr, ki = kblk[:8], kblk[8:]
                if first and j == 0 and i == 0:
                    rows = lax.broadcasted_iota(jnp.int32, kr.shape, 0)
                    ka, kb, kd = kr, jnp.where(rows == 0, 0.0, ki), jnp.where(rows == 0, ki, kr)
                else:
                    ka, kb, kd = kr, ki, kr
                ka, kb, kd = _dup(ka), _dup(kb), _dup(kd)
                blocks += [yr * ka - yi * kb, yr * kb + yi * kd]
            z_scr[buf, j] = jnp.concatenate(blocks, axis=0).astype(BF16)

    def stage2_inverse(slots, first_sel, buf):
        bms = [jnp.dot(gb_ref[first_sel] if j == 0 else gb_ref[1], z_scr[buf, j], preferred_element_type=F32)
               for j in range(SLOT_BATCH)]
        for s, bm in zip(slots, bms):
            _stage_store(a_scr, _slot_rows(s), bm)

    def spectral_phase():
        slots0 = _batch_slots(0)
        filter_multiply(slots0, _stage2_forward(a_scr, gf_ref, slots0, 0), True, 0)

        def loop(i, carry):
            slots = _batch_slots(i)
            ys = _stage2_forward(a_scr, gf_ref, slots, 1)
            stage2_inverse(_batch_slots(i - 1), jnp.where(i == 1, 0, 1), (i - 1) % 2)
            filter_multiply(slots, ys, False, i % 2)
            return carry

        lax.fori_loop(1, N_BATCHES, loop, 0, unroll=5)
        stage2_inverse(_batch_slots(N_BATCHES - 1), 1, (N_BATCHES - 1) % 2)

    @pl.when(t == _T_S1)
    def _():
        def sub(sb, carry):
            vs = short_conv(sb, 0)
            for i in range(HY_SUB):
                stage1(sb * HY_SUB + i, vs[i].astype(BF16))
            return carry

        lax.fori_loop(0, N_SUB, sub, 0)

    @pl.when((t == _T_F0) | (t == _T_F1))
    def _():
        spectral_phase()

    @pl.when(t == _T_M)
    def _():
        def sub(sb, carry):
            gates = short_conv(sb, 4)
            for q in range(0, HY_SUB, MID_GROUP):
                ys = [inv_stage1(sb * HY_SUB + q + i) for i in range(MID_GROUP)]
                us = [(gates[q + i] * ys[i]).astype(BF16) for i in range(MID_GROUP)]
                for i in range(MID_GROUP):
                    stage1(sb * HY_SUB + q + i, us[i])
            return carry

        lax.fori_loop(0, N_SUB, sub, 0)

    @pl.when(t == _T_E)
    def _():
        def sub(sb, carry):
            gates = short_conv(sb, 8)
            for i in range(HY_SUB):
                n2 = sb * HY_SUB + i
                res = (gates[i] * inv_stage1(n2) * _pair(pz_ref, n2)).astype(BF16)
                o_ref[0, n2] = res[:, :LANES]
                o_ref[1, n2] = res[:, LANES:]
            return carry

        lax.fori_loop(0, N_SUB, sub, 0)


def _hyena(p5, conv_w, conv_b, kspec, f1, f1i, gf, gb):
    grid = (N_CBLK, BATCH // 2, _T_END)
    cw = jnp.concatenate(
        [jnp.concatenate([conv_w[:, k * HYENA_WIDTH:(k + 1) * HYENA_WIDTH],
                          conv_b[:, k * HYENA_WIDTH:(k + 1) * HYENA_WIDTH]], axis=0) for k in range(3)], axis=0)
    seq_block = (2, None, N_TILES, TILE_ROWS, LANES)

    def conv_in_map(c, b, t):
        return (b, jnp.where(t < _T_M, 0, jnp.where(t < _T_E, 1, 2)) * N_CBLK + c, 0, 0, 0)

    def z_map(c, b, t):
        flat = c * (BATCH // 2) + b
        sel = jnp.where(t >= _T_M, flat, jnp.maximum(flat - 1, 0))
        return (sel % (BATCH // 2), 3 * N_CBLK + sel // (BATCH // 2), 0, 0, 0)

    const = lambda shape: pl.BlockSpec(shape, lambda c, b, t: (0,) * len(shape), pipeline_mode=pl.Buffered(1))
    in_specs = [
        pl.BlockSpec(seq_block, conv_in_map),
        pl.BlockSpec(seq_block, z_map),
        pl.BlockSpec((12, LANES), lambda c, b, t: (0, c)),
        pl.BlockSpec((None, None, N_SLOTS, 256, LANES), lambda c, b, t: (jnp.where(t >= _T_M, 1, 0), c, 0, 0, 0)),
        const((N_TILES, 128, TILE_ROWS)), const((N_TILES, TILE_ROWS, 128)),
        const((2, 256, 256)), const((2, 256, 256)),
    ]
    return pl.pallas_call(
        _hyena_kernel,
        grid=grid,
        in_specs=in_specs,
        out_specs=pl.BlockSpec(seq_block, lambda c, b, t: (b, c, 0, 0, 0)),
        out_shape=jax.ShapeDtypeStruct((BATCH, N_CBLK, N_TILES, TILE_ROWS, LANES), BF16),
        scratch_shapes=[pltpu.VMEM((2, N_TILES * A_STRIDE, LANES), jnp.uint32),
                        pltpu.VMEM((2, SLOT_BATCH, 256, 2 * LANES), BF16)],
        compiler_params=pltpu.CompilerParams(
            dimension_semantics=("arbitrary", "arbitrary", "arbitrary"), vmem_limit_bytes=VMEM_LIMIT),
        name="hyena",
    )(p5, p5, cw, kspec, f1, f1i, gf, gb)


POOL_HALO = 8


def _out_kernel(yh_ref, up_ref, upp_ref, upn_ref, zp_ref, x_ref, pw_ref, ps_ref, gh_ref, gp_ref, w_ref, gpost_ref,
                o_ref, r_scr, yp_scr):
    g = pl.program_id(1)
    last_step = N_TILES // OUT_TILES - 1
    row = lax.broadcasted_iota(jnp.int32, (TILE_ROWS, LANES), 0)

    def pool_tile(ci, l):
        if l < 0:
            t = upp_ref[ci, POOL_HALO + l].astype(F32)
            return jnp.where(g == 0, _shift_down(t), t)
        if l >= OUT_TILES:
            t = upn_ref[ci, l - OUT_TILES].astype(F32)
            return jnp.where(g == last_step, _shift_up(t), t)
        return up_ref[ci, l].astype(F32)

    for ci, w in enumerate(POOL_WINDOWS):
        lo, hi = w // 2, w - 1 - w // 2
        window_sum = pool_tile(ci, -lo)
        for d in range(-lo + 1, hi + 1):
            window_sum = window_sum + pool_tile(ci, d)
        for sb in range(OUT_TILES // SUB_TILES):
            pooled = []
            for i in range(SUB_TILES):
                l = sb * SUB_TILES + i
                if l > 0:
                    window_sum = window_sum + pool_tile(ci, l + hi) - pool_tile(ci, l - 1 - lo)
                inv_cnt = 1.0 / w
                if l < lo:
                    inv_cnt = jnp.where((g == 0) & (row == 0), 1.0 / (hi + l + 1), inv_cnt)
                elif l > OUT_TILES - 1 - hi:
                    inv_cnt = jnp.where((g == last_step) & (row == TILE_ROWS - 1),
                                        1.0 / (lo + OUT_TILES - l), inv_cnt)
                pooled.append((window_sum * inv_cnt - pool_tile(ci, l)).astype(BF16))
            y = jnp.dot(jnp.concatenate(pooled, axis=0), pw_ref[ci], preferred_element_type=F32)
            y = y * ps_ref[:, ci * LANES:(ci + 1) * LANES]
            gate = jnp.concatenate([zp_ref[ci, sb * SUB_TILES + i] for i in range(SUB_TILES)], axis=0).astype(F32)
            yp_scr[sb, ci] = y * gate

    def rms(y, gain_ref):
        ms = jnp.mean(y * y, axis=-1, keepdims=True)
        return (y * lax.rsqrt(ms + EPS) * gain_ref[...]).astype(BF16)

    for sb in range(OUT_TILES // SUB_TILES):
        j0 = sb * SUB_TILES
        yh = jnp.concatenate(
            [jnp.concatenate([yh_ref[cb, j] for cb in range(N_CBLK)], axis=-1) for j in range(j0, j0 + SUB_TILES)],
            axis=0).astype(F32)
        yp = jnp.concatenate([yp_scr[sb, ci] for ci in range(N_CBLK)], axis=-1)
        yc = jnp.concatenate([rms(yh, gh_ref), rms(yp, gp_ref)], axis=-1)
        out = jnp.dot(yc, w_ref[...], preferred_element_type=F32)
        ms = jnp.mean(out * out, axis=-1, keepdims=True)
        out = out * lax.rsqrt(ms + EPS) * gpost_ref[...]
        for j in range(SUB_TILES):
            for k in range(D_MODEL // LANES):
                r_scr[sb, k, pl.ds(j, TILE_ROWS, stride=SUB_TILES), :] = out[j * TILE_ROWS:(j + 1) * TILE_ROWS,
                                                                             k * LANES:(k + 1) * LANES]
        r = jnp.concatenate([r_scr[sb, k] for k in range(D_MODEL // LANES)], axis=-1)
        o_ref[:, j0:j0 + SUB_TILES, :] = (x_ref[:, j0:j0 + SUB_TILES, :]
                                          + r.reshape(TILE_ROWS, SUB_TILES, D_MODEL))


def _out_proj(yh, p5, x4, pool_w_bf, pool_scale, norm_h_g, norm_p_g, w_out_bf, post_g):
    n_steps = N_TILES // OUT_TILES
    halo_blocks = N_TILES // POOL_HALO
    per_step = OUT_TILES // POOL_HALO
    u_group, z_group = 4, 5
    y_spec = pl.BlockSpec((None, N_CBLK, OUT_TILES, TILE_ROWS, LANES), lambda b, g: (b, 0, g, 0, 0))
    x_spec = pl.BlockSpec((None, TILE_ROWS, OUT_TILES, D_MODEL), lambda b, g: (b, 0, g, 0))
    full = lambda shape: pl.BlockSpec(shape, lambda b, g: (0,) * len(shape))
    return pl.pallas_call(
        _out_kernel,
        grid=(BATCH, n_steps),
        in_specs=[
            y_spec,
            pl.BlockSpec((None, N_CBLK, OUT_TILES, TILE_ROWS, LANES), lambda b, g: (b, u_group, g, 0, 0)),
            pl.BlockSpec((None, N_CBLK, POOL_HALO, TILE_ROWS, LANES),
                         lambda b, g: (b, u_group, (g * per_step + halo_blocks - 1) % halo_blocks, 0, 0)),
            pl.BlockSpec((None, N_CBLK, POOL_HALO, TILE_ROWS, LANES),
                         lambda b, g: (b, u_group, (g * per_step + per_step) % halo_blocks, 0, 0)),
            pl.BlockSpec((None, N_CBLK, OUT_TILES, TILE_ROWS, LANES), lambda b, g: (b, z_group, g, 0, 0)),
            x_spec,
            full((N_CBLK, LANES, LANES)), full((1, POOL_WIDTH)),
            full((1, HYENA_WIDTH)), full((1, POOL_WIDTH)),
            full((D_MODEL, D_MODEL)), full((1, D_MODEL)),
        ],
        out_specs=x_spec,
        out_shape=jax.ShapeDtypeStruct((BATCH, TILE_ROWS, N_TILES, D_MODEL), F32),
        scratch_shapes=[pltpu.VMEM((OUT_TILES // SUB_TILES, D_MODEL // LANES, TILE_ROWS * SUB_TILES, LANES), F32),
                        pltpu.VMEM((OUT_TILES // SUB_TILES, N_CBLK, TILE_ROWS * SUB_TILES, LANES), F32)],
        compiler_params=pltpu.CompilerParams(
            dimension_semantics=("arbitrary", "arbitrary"), vmem_limit_bytes=VMEM_LIMIT),
        name="out_proj",
    )(yh, p5, p5, p5, p5, x4, pool_w_bf, pool_scale, norm_h_g, norm_p_g, w_out_bf, post_g)


def kernel(x, pre_norm_g, w_in, conv_w, conv_b, filt_w1, filt_b1, filt_w2, filt_b2, filt_w3, filt_b3,
           filt_freq, filt_w_out, hyena_d, pool_w, pool_scale, norm_h_g, norm_p_g, w_out, post_norm_g):
    assert x.shape == (BATCH, SEQ, D_MODEL) and pre_norm_g.shape[0] == 1
    f1, f1i, gf, gb, f1f = (jnp.asarray(m, F32).astype(BF16) for m in (_F1, _F1I, _GF, _GB, _F1F))

    x4 = x.reshape(BATCH, TILE_ROWS, N_TILES, D_MODEL)
    p5 = _in_proj(x4, pre_norm_g, w_in[0].astype(BF16))

    taps = _filter_mlp(jnp.asarray(_ZFEAT), filt_w1[0], filt_b1[0], filt_w2[0], filt_b2[0], filt_w3[0], filt_b3[0],
                       filt_freq[0], filt_w_out[0], jnp.asarray(_ABS_DELTAS))
    kspec = _filter_spec(taps, hyena_d[0].reshape(2, 1, HYENA_WIDTH), f1f, gf)

    yh = _hyena(p5, conv_w[0], conv_b, kspec, f1, f1i, gf, gb)
    out4 = _out_proj(yh, p5, x4, pool_w[0].astype(BF16), pool_scale, norm_h_g, norm_p_g,
                     w_out[0].astype(BF16), post_norm_g)
    return out4.reshape(BATCH, SEQ, D_MODEL)
```

```python
import functools
import math

import numpy as np
import jax
import jax.numpy as jnp
from jax import lax
from jax.experimental import pallas as pl
from jax.experimental.pallas import tpu as pltpu

F32 = jnp.float32
BF16 = jnp.bfloat16

D_MODEL = 1024
BATCH = 4
SEQ = 8192
HYENA_WIDTH = 512
POOL_WIDTH = 512
POOL_WINDOWS = (2, 4, 8, 16)
FILTER_EMB = 33
FILTER_BANDS = 16
FILTER_HIDDEN = 64
PROJ_WIDTH = 3072
EPS = 1e-6

LANES = 128
N_FFT = 2 * SEQ
N_TILES = 128
TILE_ROWS = SEQ // N_TILES
N_SLOTS = 64
CHUNK = 32
HY_SUB = 32
N_CHUNKS = N_TILES // CHUNK
A_STRIDE = 72
SLOT_BATCH = 4
N_CBLK = HYENA_WIDTH // LANES
PROJ_BLOCKS = PROJ_WIDTH // LANES
IN_TILES = 16
OUT_TILES = 16
SUB_TILES = 8
VMEM_LIMIT = 60 * 1024 * 1024


def _dft_tables():
    n1 = np.arange(TILE_ROWS)
    n2 = np.arange(N_TILES)
    s = np.arange(N_SLOTS)
    ph = 2 * np.pi * (n2[:, None, None] * s[None, :, None] / N_FFT
                      + n1[None, None, :] * s[None, :, None] / 128.0)
    f1 = np.zeros((N_TILES, 128, TILE_ROWS))
    f1[:, :64, :] = np.cos(ph)
    f1[:, 64:, :] = -np.sin(ph)
    f1[:, 0, :] = 1.0
    f1[:, 64, :] = (-1.0) ** n1
    php = np.transpose(ph, (0, 2, 1))
    f1i = np.zeros((N_TILES, TILE_ROWS, 128))
    f1i[:, :, :64] = 2 * np.cos(php) / N_FFT
    f1i[:, :, 64:] = -2 * np.sin(php) / N_FFT
    f1i[:, :, 0] = 1.0 / N_FFT
    f1i[:, :, 64] = ((-1.0) ** n1)[None, :] / N_FFT
    k2 = np.arange(128)
    th = 2 * np.pi * np.outer(k2, n2) / 128.0
    c, sn = np.cos(th), np.sin(th)
    g = np.block([[c, sn], [-sn, c]])
    gi = np.block([[c, -sn], [sn, c]])
    kk = np.arange(64)
    tha = 2 * np.pi * np.outer(kk, n2) / 128.0
    thb = 2 * np.pi * np.outer(64 + 128 * kk, n2) / N_FFT
    g0 = np.zeros((256, 256))
    g0[0:64, 0:128] = np.cos(tha)
    g0[64:128, 128:256] = np.cos(thb)
    g0[128:192, 0:128] = -np.sin(tha)
    g0[128, 0:128] = (-1.0) ** n2
    g0[192:256, 128:256] = -np.sin(thb)
    g0i = np.zeros((256, 256))
    g0i[0:128, 0:64] = 2 * np.cos(tha.T)
    g0i[0:128, 0] = 1.0
    g0i[0:128, 128:192] = -2 * np.sin(tha.T)
    g0i[0:128, 128] = (-1.0) ** n2
    g0i[128:256, 64:128] = 2 * np.cos(thb.T)
    g0i[128:256, 192:256] = -2 * np.sin(thb.T)
    q = np.arange(256)
    perm = np.where(q % 16 < 8, 8 * (q // 16) + q % 16, 128 + 8 * (q // 16) + q % 16 - 8)
    gf = np.stack([g0[perm, :], g[perm, :]])
    gb = np.stack([g0i[:, perm], gi[:, perm]])
    n1f = np.arange(128)
    phf = 2 * np.pi * (n2[:, None, None] * s[None, :, None] / N_FFT
                       + n1f[None, None, :] * s[None, :, None] / 128.0)
    full = np.zeros((N_TILES, 128, 128))
    full[:, :64, :] = np.cos(phf)
    full[:, 64:, :] = -np.sin(phf)
    full[:, 0, :] = 1.0
    full[:, 64, :] = (-1.0) ** n1f
    rev = full[:, :, 127:63:-1].copy()
    rev[0, :, 1:] = full[0, :, 127:64:-1]
    rev[0, :, 0] = 0.0
    f1f = np.concatenate([full[:, :, :64], rev], axis=2)
    pair64 = np.arange(128) % 2 * 64 + np.arange(128) // 2
    pair128 = np.arange(256) % 2 * 128 + np.arange(256) // 2
    f1, f1f = f1[:, pair64, :], f1f[:, pair64, :]
    f1i = f1i[:, :, pair64]
    gf = gf[:, :, pair128]
    gb = gb[:, pair128, :]
    return f1, f1i, gf, gb, f1f


def _filter_features():
    pos = np.arange(SEQ, dtype=np.float64)
    t = pos / (SEQ - 1)
    ang = 2.0 * math.pi * pos / SEQ
    bands = np.linspace(1e-4, FILTER_BANDS - 1, FILTER_BANDS)
    z = np.concatenate([t[:, None], np.cos(bands[None, :] * ang[:, None]),
                        -np.sin(bands[None, :] * ang[:, None])], axis=-1)
    z = z.reshape(TILE_ROWS, N_TILES, FILTER_EMB).transpose(1, 0, 2).reshape(SEQ, FILTER_EMB)
    zp = np.zeros((FILTER_HIDDEN, SEQ))
    zp[:FILTER_EMB, :] = z.T
    max_decay = math.log(1e-2) / 0.3
    min_decay = math.log(1e-2) / 1.5
    deltas = np.abs(np.linspace(min_decay, max_decay, HYENA_WIDTH))
    return zp.astype(np.float32), deltas.astype(np.float32)[None, :]


_F1, _F1I, _GF, _GB, _F1F = _dft_tables()
_ZFEAT, _ABS_DELTAS = _filter_features()


def _shift_down(x):
    rows = lax.broadcasted_iota(jnp.int32, x.shape, 0)
    return jnp.where(rows == 0, 0.0, pltpu.roll(x, 1, axis=0))


def _shift_up(x):
    rows = lax.broadcasted_iota(jnp.int32, x.shape, 0)
    return jnp.where(rows == x.shape[0] - 1, 0.0, pltpu.roll(x, x.shape[0] - 1, axis=0))


def _pair(ref, i):
    return jnp.concatenate([ref[0, i], ref[1, i]], axis=-1).astype(F32)


def _dup(x):
    return jnp.concatenate([x, x], axis=-1)


def _silu(z):
    hz = 0.5 * z
    return hz * (1.0 + jnp.tanh(hz))


def _tile_rows(n2):
    return pl.ds(pl.multiple_of(n2 * A_STRIDE, 8), N_SLOTS)


def _slot_rows(s):
    return pl.ds(s, N_TILES, stride=A_STRIDE)


def _stage_load(a_ref, rows):
    words = jnp.concatenate([a_ref[0, rows, :], a_ref[1, rows, :]], axis=-1)
    return pltpu.bitcast(words, BF16)


def _stage_store(a_ref, rows, val):
    words = pltpu.bitcast(val.astype(BF16), jnp.uint32)
    a_ref[0, rows, :] = words[:, :LANES]
    a_ref[1, rows, :] = words[:, LANES:]


N_BATCHES = N_SLOTS // SLOT_BATCH
SPEC_BLOCKS = 16


def _batch_slots(i):
    return [i * SLOT_BATCH + j for j in range(SLOT_BATCH)]


def _stage2_forward(a_ref, gf_ref, slots, sel0):
    xs = [_stage_load(a_ref, _slot_rows(s)) for s in slots]
    return [jnp.dot(gf_ref[sel0] if j == 0 else gf_ref[1], x, preferred_element_type=F32)
            for j, x in enumerate(xs)]


def _re_im_blocks(y):
    return [(y[16 * i:16 * i + 8], y[16 * i + 8:16 * i + 16]) for i in range(SPEC_BLOCKS)]


_GATE_CHUNKS = (3, 5)


def _cast_weights_once(w_ref, w_scr, ncol):
    @pl.when((pl.program_id(0) == 0) & (pl.program_id(1) == 0))
    def _():
        for c in range(w_ref.shape[1] // ncol):
            w_scr[:, c * ncol:(c + 1) * ncol] = w_ref[:, c * ncol:(c + 1) * ncol].astype(BF16)


def _in_proj_kernel(x_ref, g_ref, w32_ref, o_ref, h_scr, w_ref):
    _cast_weights_once(w32_ref, w_ref, 512)
    ncol = 512
    for sb in range(IN_TILES // SUB_TILES):
        j0 = sb * SUB_TILES
        x = x_ref[:, j0:j0 + SUB_TILES, :].reshape(TILE_ROWS * SUB_TILES, D_MODEL)
        ms = jnp.mean(x * x, axis=-1, keepdims=True)
        hn = x * lax.rsqrt(ms + EPS) * g_ref[...]
        for k in range(D_MODEL // LANES):
            h_scr[sb, k] = hn[:, k * LANES:(k + 1) * LANES]
        h = jnp.concatenate(
            [jnp.concatenate([h_scr[sb, k, pl.ds(j, TILE_ROWS, stride=SUB_TILES), :]
                              for k in range(D_MODEL // LANES)], axis=-1).astype(BF16)
             for j in range(SUB_TILES)], axis=0)
        for c in range(PROJ_WIDTH // ncol):
            p = jnp.dot(h, w_ref[:, c * ncol:(c + 1) * ncol], preferred_element_type=F32)
            if c in _GATE_CHUNKS:
                p = _silu(p)
            p = p.astype(BF16)
            for cb in range(ncol // LANES):
                for j in range(SUB_TILES):
                    o_ref[c * (ncol // LANES) + cb, j0 + j] = p[j * TILE_ROWS:(j + 1) * TILE_ROWS,
                                                                cb * LANES:(cb + 1) * LANES]


def _in_proj(x4, pre_g, w_in):
    grid = (BATCH, N_TILES // IN_TILES)
    return pl.pallas_call(
        _in_proj_kernel,
        grid=grid,
        in_specs=[
            pl.BlockSpec((None, TILE_ROWS, IN_TILES, D_MODEL), lambda b, g: (b, 0, g, 0)),
            pl.BlockSpec((1, D_MODEL), lambda b, g: (0, 0)),
            pl.BlockSpec((D_MODEL, PROJ_WIDTH), lambda b, g: (0, 0), pipeline_mode=pl.Buffered(1)),
        ],
        out_specs=pl.BlockSpec((None, PROJ_BLOCKS, IN_TILES, TILE_ROWS, LANES), lambda b, g: (b, 0, g, 0, 0)),
        out_shape=jax.ShapeDtypeStruct((BATCH, PROJ_BLOCKS, N_TILES, TILE_ROWS, LANES), BF16),
        scratch_shapes=[pltpu.VMEM((IN_TILES // SUB_TILES, D_MODEL // LANES, TILE_ROWS * SUB_TILES, LANES), F32),
                        pltpu.VMEM((D_MODEL, PROJ_WIDTH), BF16)],
        compiler_params=pltpu.CompilerParams(
            dimension_semantics=("arbitrary", "arbitrary"), vmem_limit_bytes=VMEM_LIMIT),
        name="in_proj",
    )(x4, pre_g, w_in)


def _split_bf16(x):
    hi = x.astype(BF16)
    return hi, (x - hi.astype(F32)).astype(BF16)


def _stack_weight_rows(w):
    hi, lo = _split_bf16(w)
    return jnp.concatenate([hi, lo, hi], axis=1)


def _filter_mlp_kernel(z_ref, w1_ref, b1_ref, w2_ref, b2_ref, w3_ref, b3_ref, fr_ref,
                       wp2_ref, wpl_ref, dl_ref, o_ref):
    def layer(w_ref, b_ref, f_ref, h):
        hi, lo = _split_bf16(h)
        pre = jnp.dot(w_ref[...], jnp.concatenate([hi, hi, lo], axis=0), preferred_element_type=F32)
        return jnp.sin(f_ref[...] * (pre + b_ref[...]))

    h = layer(w1_ref, b1_ref, fr_ref, z_ref[...])
    h = layer(w2_ref, b2_ref, fr_ref, h)
    h = layer(w3_ref, b3_ref, fr_ref, h)
    h = jnp.concatenate([h, jnp.zeros_like(h)], axis=0)
    hi, lo = _split_bf16(h.T)
    hs = jnp.concatenate([hi, lo], axis=1)
    rows = CHUNK * TILE_ROWS
    r = lax.broadcasted_iota(jnp.int32, (rows, HYENA_WIDTH), 0)
    pos = 128 * (r % TILE_ROWS) + CHUNK * pl.program_id(0) + r // TILE_ROWS
    t = pos.astype(F32) / float(SEQ - 1)
    decay = jnp.exp(-t * dl_ref[...])
    for od in range(4):
        cols = slice(od * HYENA_WIDTH, (od + 1) * HYENA_WIDTH)
        taps = (jnp.dot(hs, wp2_ref[:, cols], preferred_element_type=F32)
                + jnp.dot(hi, wpl_ref[:, cols], preferred_element_type=F32)) * decay
        taps = taps.astype(BF16)
        for cb in range(N_CBLK):
            for i in range(CHUNK):
                o_ref[od * N_CBLK + cb, i] = taps[i * TILE_ROWS:(i + 1) * TILE_ROWS, cb * LANES:(cb + 1) * LANES]


def _filter_mlp(zfeat_t, w1, b1, w2, b2, w3, b3, freq, w_proj, abs_deltas):
    hid = FILTER_HIDDEN
    col = lambda v, n: jnp.pad(v.reshape(-1, 1), ((0, n - v.size), (0, 0)))
    w1s = _stack_weight_rows(jnp.pad(w1, ((0, hid - FILTER_EMB), (0, 0))).T)
    w2s = _stack_weight_rows(w2.T)
    w3s = _stack_weight_rows(w3.T)
    wp_hi, wp_lo = _split_bf16(jnp.pad(w_proj, ((0, LANES - hid), (0, 0))))
    wp2 = jnp.concatenate([wp_hi, wp_hi], axis=0)
    full = lambda shape: pl.BlockSpec(shape, lambda c: (0,) * len(shape))
    cols = CHUNK * TILE_ROWS
    return pl.pallas_call(
        _filter_mlp_kernel,
        grid=(N_CHUNKS,),
        in_specs=[
            pl.BlockSpec((hid, cols), lambda c: (0, c)),
            full((hid, 3 * hid)), full((hid, 1)),
            full((hid, 3 * hid)), full((hid, 1)),
            full((hid, 3 * hid)), full((hid, 1)),
            full((hid, 1)),
            full((2 * LANES, 4 * HYENA_WIDTH)), full((LANES, 4 * HYENA_WIDTH)),
            full((1, HYENA_WIDTH)),
        ],
        out_specs=pl.BlockSpec((4 * N_CBLK, CHUNK, TILE_ROWS, LANES), lambda c: (0, c, 0, 0)),
        out_shape=jax.ShapeDtypeStruct((4 * N_CBLK, N_TILES, TILE_ROWS, LANES), BF16),
        compiler_params=pltpu.CompilerParams(dimension_semantics=("arbitrary",), vmem_limit_bytes=VMEM_LIMIT),
        name="filter_mlp",
    )(zfeat_t, w1s, col(b1, hid), w2s, col(b2, hid), w3s, col(b3, hid), col(freq, hid), wp2, wp_lo, abs_deltas)


def _filter_spec_kernel(fa_ref, fb_ref, ba_ref, bb_ref, d_ref, f1f_ref, gf_ref, k_ref, a_scr):
    def stage1(sb, carry):
        for i in range(HY_SUB):
            n2 = sb * HY_SUB + i
            nb = (N_TILES - n2) % N_TILES
            fwd = jnp.concatenate([fa_ref[n2], fb_ref[n2]], axis=-1)
            bwd = jnp.concatenate([ba_ref[nb], bb_ref[nb]], axis=-1)
            taps = jnp.concatenate([fwd, bwd], axis=0)
            _stage_store(a_scr, _tile_rows(n2), jnp.dot(f1f_ref[n2], taps, preferred_element_type=F32))
        return carry

    lax.fori_loop(0, N_SUB, stage1, 0)

    d = d_ref[...]

    def batch(slots, first):
        ys = _stage2_forward(a_scr, gf_ref, slots, 0 if first else 1)
        for j, (s, y) in enumerate(zip(slots, ys)):
            blocks = []
            for i, (yr, yi) in enumerate(_re_im_blocks(y)):
                yr = yr + d
                if first and j == 0 and i == 0:
                    rows = lax.broadcasted_iota(jnp.int32, yi.shape, 0)
                    yi = yi + jnp.where(rows == 0, d, 0.0)
                blocks += [yr, yi]
            spec = jnp.concatenate(blocks, axis=0).astype(BF16)
            k_ref[0, s] = spec[:, :LANES]
            k_ref[1, s] = spec[:, LANES:]

    batch(_batch_slots(0), True)

    def loop(i, carry):
        batch(_batch_slots(i), False)
        return carry

    lax.fori_loop(1, N_BATCHES, loop, 0, unroll=3)


def _filter_spec(taps, hyena_d3, f1f, gf):
    grid = (2, N_CBLK // 2)
    const = lambda shape: pl.BlockSpec(shape, lambda o, c: (0,) * len(shape), pipeline_mode=pl.Buffered(1))

    def taps_spec(direction, k):
        return pl.BlockSpec((None, N_TILES, TILE_ROWS, LANES),
                            lambda o, c: ((2 * o + direction) * N_CBLK + 2 * c + k, 0, 0, 0))

    return pl.pallas_call(
        _filter_spec_kernel,
        grid=grid,
        in_specs=[taps_spec(0, 0), taps_spec(0, 1), taps_spec(1, 0), taps_spec(1, 1),
                  pl.BlockSpec((None, 1, 2 * LANES), lambda o, c: (o, 0, c)),
                  const((N_TILES, 128, 128)), const((2, 256, 256))],
        out_specs=pl.BlockSpec((None, 2, N_SLOTS, 256, LANES), lambda o, c: (o, c, 0, 0, 0)),
        out_shape=jax.ShapeDtypeStruct((2, N_CBLK, N_SLOTS, 256, LANES), BF16),
        scratch_shapes=[pltpu.VMEM((2, N_TILES * A_STRIDE, LANES), jnp.uint32)],
        compiler_params=pltpu.CompilerParams(
            dimension_semantics=("arbitrary", "arbitrary"), vmem_limit_bytes=VMEM_LIMIT),
        name="filter_spec",
    )(taps, taps, taps, taps, hyena_d3, f1f, gf)


_T_S1 = 0
_T_F0 = 1
_T_M = 2
_T_F1 = 3
_T_E = 4
_T_END = 5
N_SUB = N_TILES // HY_SUB
MID_GROUP = 8


def _hyena_kernel(cin_ref, pz_ref, cw_ref, k_ref, f1_ref, f1i_ref, gf_ref, gb_ref, o_ref, a_scr, z_scr):
    t = pl.program_id(2)

    def stage1(n2, u_bf):
        _stage_store(a_scr, _tile_rows(n2), jnp.dot(f1_ref[n2], u_bf, preferred_element_type=F32))

    def inv_stage1(n2):
        return jnp.dot(f1i_ref[n2], _stage_load(a_scr, _tile_rows(n2)), preferred_element_type=F32)

    def short_conv(sb, row0):
        base = sb * HY_SUB
        first = _pair(cin_ref, (base + N_TILES - 1) % N_TILES)
        first = jnp.where(sb == 0, _shift_down(first), first)
        last = _pair(cin_ref, (base + HY_SUB) % N_TILES)
        last = jnp.where(sb == N_SUB - 1, _shift_up(last), last)
        tiles = [first] + [_pair(cin_ref, base + i) for i in range(HY_SUB)] + [last]
        w = cw_ref[row0:row0 + 4, :]
        w0, w1, w2, b = _dup(w[0:1]), _dup(w[1:2]), _dup(w[2:3]), _dup(w[3:4])
        return [tiles[i] * w0 + tiles[i + 1] * w1 + tiles[i + 2] * w2 + b for i in range(HY_SUB)]

    def filter_multiply(slots, ys, first, buf):
        for j, (s, y) in enumerate(zip(slots, ys)):
            blocks = []
            for i, (yr, yi) in enumerate(_re_im_blocks(y)):
                kblk = k_ref[s, 16 * i:16 * i + 16, :].astype(F32)
                kr, ki = kblk[:8], kblk[8:]
                if first and j == 0 and i == 0:
                    rows = lax.broadcasted_iota(jnp.int32, kr.shape, 0)
                    ka, kb, kd = kr, jnp.where(rows == 0, 0.0, ki), jnp.where(rows == 0, ki, kr)
                else:
                    ka, kb, kd = kr, ki, kr
                ka, kb, kd = _dup(ka), _dup(kb), _dup(kd)
                blocks += [yr * ka - yi * kb, yr * kb + yi * kd]
            z_scr[buf, j] = jnp.concatenate(blocks, axis=0).astype(BF16)

    def stage2_inverse(slots, first_sel, buf):
        bms = [jnp.dot(gb_ref[first_sel] if j == 0 else gb_ref[1], z_scr[buf, j], preferred_element_type=F32)
               for j in range(SLOT_BATCH)]
        for s, bm in zip(slots, bms):
            _stage_store(a_scr, _slot_rows(s), bm)

    def spectral_phase():
        slots0 = _batch_slots(0)
        filter_multiply(slots0, _stage2_forward(a_scr, gf_ref, slots0, 0), True, 0)

        def loop(i, carry):
            slots = _batch_slots(i)
            ys = _stage2_forward(a_scr, gf_ref, slots, 1)
            stage2_inverse(_batch_slots(i - 1), jnp.where(i == 1, 0, 1), (i - 1) % 2)
            filter_multiply(slots, ys, False, i % 2)
            return carry

        lax.fori_loop(1, N_BATCHES, loop, 0, unroll=5)
        stage2_inverse(_batch_slots(N_BATCHES - 1), 1, (N_BATCHES - 1) % 2)

    @pl.when(t == _T_S1)
    def _():
        def sub(sb, carry):
            vs = short_conv(sb, 0)
            for i in range(HY_SUB):
                stage1(sb * HY_SUB + i, vs[i].astype(BF16))
            return carry

        lax.fori_loop(0, N_SUB, sub, 0)

    @pl.when((t == _T_F0) | (t == _T_F1))
    def _():
        spectral_phase()

    @pl.when(t == _T_M)
    def _():
        def sub(sb, carry):
            gates = short_conv(sb, 4)
            for q in range(0, HY_SUB, MID_GROUP):
                ys = [inv_stage1(sb * HY_SUB + q + i) for i in range(MID_GROUP)]
                us = [(gates[q + i] * ys[i]).astype(BF16) for i in range(MID_GROUP)]
                for i in range(MID_GROUP):
                    stage1(sb * HY_SUB + q + i, us[i])
            return carry

        lax.fori_loop(0, N_SUB, sub, 0)

    @pl.when(t == _T_E)
    def _():
        def sub(sb, carry):
            gates = short_conv(sb, 8)
            for i in range(HY_SUB):
                n2 = sb * HY_SUB + i
                res = (gates[i] * inv_stage1(n2) * _pair(pz_ref, n2)).astype(BF16)
                o_ref[0, n2] = res[:, :LANES]
                o_ref[1, n2] = res[:, LANES:]
            return carry

        lax.fori_loop(0, N_SUB, sub, 0)


def _hyena(p5, conv_w, conv_b, kspec, f1, f1i, gf, gb):
    grid = (N_CBLK, BATCH // 2, _T_END)
    cw = jnp.concatenate(
        [jnp.concatenate([conv_w[:, k * HYENA_WIDTH:(k + 1) * HYENA_WIDTH],
                          conv_b[:, k * HYENA_WIDTH:(k + 1) * HYENA_WIDTH]], axis=0) for k in range(3)], axis=0)
    seq_block = (2, None, N_TILES, TILE_ROWS, LANES)

    def conv_in_map(c, b, t):
        return (b, jnp.where(t < _T_M, 0, jnp.where(t < _T_E, 1, 2)) * N_CBLK + c, 0, 0, 0)

    def z_map(c, b, t):
        flat = c * (BATCH // 2) + b
        sel = jnp.where(t >= _T_M, flat, jnp.maximum(flat - 1, 0))
        return (sel % (BATCH // 2), 3 * N_CBLK + sel // (BATCH // 2), 0, 0, 0)

    const = lambda shape: pl.BlockSpec(shape, lambda c, b, t: (0,) * len(shape), pipeline_mode=pl.Buffered(1))
    in_specs = [
        pl.BlockSpec(seq_block, conv_in_map),
        pl.BlockSpec(seq_block, z_map),
        pl.BlockSpec((12, LANES), lambda c, b, t: (0, c)),
        pl.BlockSpec((None, None, N_SLOTS, 256, LANES), lambda c, b, t: (jnp.where(t >= _T_M, 1, 0), c, 0, 0, 0)),
        const((N_TILES, 128, TILE_ROWS)), const((N_TILES, TILE_ROWS, 128)),
        const((2, 256, 256)), const((2, 256, 256)),
    ]
    return pl.pallas_call(
        _hyena_kernel,
        grid=grid,
        in_specs=in_specs,
        out_specs=pl.BlockSpec(seq_block, lambda c, b, t: (b, c, 0, 0, 0)),
        out_shape=jax.ShapeDtypeStruct((BATCH, N_CBLK, N_TILES, TILE_ROWS, LANES), BF16),
        scratch_shapes=[pltpu.VMEM((2, N_TILES * A_STRIDE, LANES), jnp.uint32),
                        pltpu.VMEM((2, SLOT_BATCH, 256, 2 * LANES), BF16)],
        compiler_params=pltpu.CompilerParams(
            dimension_semantics=("arbitrary", "arbitrary", "arbitrary"), vmem_limit_bytes=VMEM_LIMIT),
        name="hyena",
    )(p5, p5, cw, kspec, f1, f1i, gf, gb)


POOL_HALO = 8


def _out_kernel(yh_ref, up_ref, upp_ref, upn_ref, zp_ref, x_ref, pw_ref, ps_ref, gh_ref, gp_ref, w32_ref, gpost_ref,
                o_ref, r_scr, yp_scr, w_ref):
    _cast_weights_once(w32_ref, w_ref, 512)
    g = pl.program_id(1)
    last_step = N_TILES // OUT_TILES - 1
    row = lax.broadcasted_iota(jnp.int32, (TILE_ROWS, LANES), 0)

    def pool_tile(ci, l):
        if l < 0:
            t = upp_ref[ci, POOL_HALO + l].astype(F32)
            return jnp.where(g == 0, _shift_down(t), t)
        if l >= OUT_TILES:
            t = upn_ref[ci, l - OUT_TILES].astype(F32)
            return jnp.where(g == last_step, _shift_up(t), t)
        return up_ref[ci, l].astype(F32)

    for ci, w in enumerate(POOL_WINDOWS):
        lo, hi = w // 2, w - 1 - w // 2
        window_sum = pool_tile(ci, -lo)
        for d in range(-lo + 1, hi + 1):
            window_sum = window_sum + pool_tile(ci, d)
        for sb in range(OUT_TILES // SUB_TILES):
            pooled = []
            for i in range(SUB_TILES):
                l = sb * SUB_TILES + i
                if l > 0:
                    window_sum = window_sum + pool_tile(ci, l + hi) - pool_tile(ci, l - 1 - lo)
                inv_cnt = 1.0 / w
                if l < lo:
                    inv_cnt = jnp.where((g == 0) & (row == 0), 1.0 / (hi + l + 1), inv_cnt)
                elif l > OUT_TILES - 1 - hi:
                    inv_cnt = jnp.where((g == last_step) & (row == TILE_ROWS - 1),
                                        1.0 / (lo + OUT_TILES - l), inv_cnt)
                pooled.append((window_sum * inv_cnt - pool_tile(ci, l)).astype(BF16))
            y = jnp.dot(jnp.concatenate(pooled, axis=0), pw_ref[ci], preferred_element_type=F32)
            y = y * ps_ref[:, ci * LANES:(ci + 1) * LANES]
            gate = jnp.concatenate([zp_ref[ci, sb * SUB_TILES + i] for i in range(SUB_TILES)], axis=0).astype(F32)
            yp_scr[sb, ci] = y * gate

    def rms(y, gain_ref):
        ms = jnp.mean(y * y, axis=-1, keepdims=True)
        return (y * lax.rsqrt(ms + EPS) * gain_ref[...]).astype(BF16)

    for sb in range(OUT_TILES // SUB_TILES):
        j0 = sb * SUB_TILES
        yh = jnp.concatenate(
            [jnp.concatenate([yh_ref[cb, j] for cb in range(N_CBLK)], axis=-1) for j in range(j0, j0 + SUB_TILES)],
            axis=0).astype(F32)
        yp = jnp.concatenate([yp_scr[sb, ci] for ci in range(N_CBLK)], axis=-1)
        yc = jnp.concatenate([rms(yh, gh_ref), rms(yp, gp_ref)], axis=-1)
        out = jnp.dot(yc, w_ref[...], preferred_element_type=F32)
        ms = jnp.mean(out * out, axis=-1, keepdims=True)
        out = out * lax.rsqrt(ms + EPS) * gpost_ref[...]
        for j in range(SUB_TILES):
            for k in range(D_MODEL // LANES):
                r_scr[sb, k, pl.ds(j, TILE_ROWS, stride=SUB_TILES), :] = out[j * TILE_ROWS:(j + 1) * TILE_ROWS,
                                                                             k * LANES:(k + 1) * LANES]
        r = jnp.concatenate([r_scr[sb, k] for k in range(D_MODEL // LANES)], axis=-1)
        o_ref[:, j0:j0 + SUB_TILES, :] = (x_ref[:, j0:j0 + SUB_TILES, :]
                                          + r.reshape(TILE_ROWS, SUB_TILES, D_MODEL))


def _out_proj(yh, p5, x4, pool_w_bf, pool_scale, norm_h_g, norm_p_g, w_out, post_g):
    n_steps = N_TILES // OUT_TILES
    halo_blocks = N_TILES // POOL_HALO
    per_step = OUT_TILES // POOL_HALO
    u_group, z_group = 4, 5
    y_spec = pl.BlockSpec((None, N_CBLK, OUT_TILES, TILE_ROWS, LANES), lambda b, g: (b, 0, g, 0, 0))
    x_spec = pl.BlockSpec((None, TILE_ROWS, OUT_TILES, D_MODEL), lambda b, g: (b, 0, g, 0))
    full = lambda shape: pl.BlockSpec(shape, lambda b, g: (0,) * len(shape))
    return pl.pallas_call(
        _out_kernel,
        grid=(BATCH, n_steps),
        in_specs=[
            y_spec,
            pl.BlockSpec((None, N_CBLK, OUT_TILES, TILE_ROWS, LANES), lambda b, g: (b, u_group, g, 0, 0)),
            pl.BlockSpec((None, N_CBLK, POOL_HALO, TILE_ROWS, LANES),
                         lambda b, g: (b, u_group, (g * per_step + halo_blocks - 1) % halo_blocks, 0, 0)),
            pl.BlockSpec((None, N_CBLK, POOL_HALO, TILE_ROWS, LANES),
                         lambda b, g: (b, u_group, (g * per_step + per_step) % halo_blocks, 0, 0)),
            pl.BlockSpec((None, N_CBLK, OUT_TILES, TILE_ROWS, LANES), lambda b, g: (b, z_group, g, 0, 0)),
            x_spec,
            full((N_CBLK, LANES, LANES)), full((1, POOL_WIDTH)),
            full((1, HYENA_WIDTH)), full((1, POOL_WIDTH)),
            pl.BlockSpec((D_MODEL, D_MODEL), lambda b, g: (0, 0), pipeline_mode=pl.Buffered(1)), full((1, D_MODEL)),
        ],
        out_specs=x_spec,
        out_shape=jax.ShapeDtypeStruct((BATCH, TILE_ROWS, N_TILES, D_MODEL), F32),
        scratch_shapes=[pltpu.VMEM((OUT_TILES // SUB_TILES, D_MODEL // LANES, TILE_ROWS * SUB_TILES, LANES), F32),
                        pltpu.VMEM((OUT_TILES // SUB_TILES, N_CBLK, TILE_ROWS * SUB_TILES, LANES), F32),
                        pltpu.VMEM((D_MODEL, D_MODEL), BF16)],
        compiler_params=pltpu.CompilerParams(
            dimension_semantics=("arbitrary", "arbitrary"), vmem_limit_bytes=VMEM_LIMIT),
        name="out_proj",
    )(yh, p5, p5, p5, p5, x4, pool_w_bf, pool_scale, norm_h_g, norm_p_g, w_out, post_g)


def kernel(x, pre_norm_g, w_in, conv_w, conv_b, filt_w1, filt_b1, filt_w2, filt_b2, filt_w3, filt_b3,
           filt_freq, filt_w_out, hyena_d, pool_w, pool_scale, norm_h_g, norm_p_g, w_out, post_norm_g):
    assert x.shape == (BATCH, SEQ, D_MODEL) and pre_norm_g.shape[0] == 1
    f1, f1i, gf, gb, f1f = (jnp.asarray(m, F32).astype(BF16) for m in (_F1, _F1I, _GF, _GB, _F1F))

    x4 = x.reshape(BATCH, TILE_ROWS, N_TILES, D_MODEL)
    p5 = _in_proj(x4, pre_norm_g, w_in[0])

    taps = _filter_mlp(jnp.asarray(_ZFEAT), filt_w1[0], filt_b1[0], filt_w2[0], filt_b2[0], filt_w3[0], filt_b3[0],
                       filt_freq[0], filt_w_out[0], jnp.asarray(_ABS_DELTAS))
    kspec = _filter_spec(taps, hyena_d[0].reshape(2, 1, HYENA_WIDTH), f1f, gf)

    yh = _hyena(p5, conv_w[0], conv_b, kspec, f1, f1i, gf, gb)
    out4 = _out_proj(yh, p5, x4, pool_w[0].astype(BF16), pool_scale, norm_h_g, norm_p_g,
                     w_out[0], post_norm_g)
    return out4.reshape(BATCH, SEQ, D_MODEL)
```

```python
import functools
import math

import numpy as np
import jax
import jax.numpy as jnp
from jax import lax
from jax.experimental import pallas as pl
from jax.experimental.pallas import tpu as pltpu

F32 = jnp.float32
BF16 = jnp.bfloat16

D_MODEL = 1024
BATCH = 4
SEQ = 8192
HYENA_WIDTH = 512
POOL_WIDTH = 512
POOL_WINDOWS = (2, 4, 8, 16)
FILTER_EMB = 33
FILTER_BANDS = 16
FILTER_HIDDEN = 64
PROJ_WIDTH = 3072
EPS = 1e-6

LANES = 128
N_FFT = 2 * SEQ
N_TILES = 128
TILE_ROWS = SEQ // N_TILES
N_SLOTS = 64
CHUNK = 32
HY_SUB = 32
N_CHUNKS = N_TILES // CHUNK
A_STRIDE = 72
SLOT_BATCH = 4
N_CBLK = HYENA_WIDTH // LANES
PROJ_BLOCKS = PROJ_WIDTH // LANES
IN_TILES = 16
OUT_TILES = 16
SUB_TILES = 8
VMEM_LIMIT = 60 * 1024 * 1024


def _dft_tables():
    n1 = np.arange(TILE_ROWS)
    n2 = np.arange(N_TILES)
    s = np.arange(N_SLOTS)
    ph = 2 * np.pi * (n2[:, None, None] * s[None, :, None] / N_FFT
                      + n1[None, None, :] * s[None, :, None] / 128.0)
    f1 = np.zeros((N_TILES, 128, TILE_ROWS))
    f1[:, :64, :] = np.cos(ph)
    f1[:, 64:, :] = -np.sin(ph)
    f1[:, 0, :] = 1.0
    f1[:, 64, :] = (-1.0) ** n1
    php = np.transpose(ph, (0, 2, 1))
    f1i = np.zeros((N_TILES, TILE_ROWS, 128))
    f1i[:, :, :64] = 2 * np.cos(php) / N_FFT
    f1i[:, :, 64:] = -2 * np.sin(php) / N_FFT
    f1i[:, :, 0] = 1.0 / N_FFT
    f1i[:, :, 64] = ((-1.0) ** n1)[None, :] / N_FFT
    k2 = np.arange(128)
    th = 2 * np.pi * np.outer(k2, n2) / 128.0
    c, sn = np.cos(th), np.sin(th)
    g = np.block([[c, sn], [-sn, c]])
    gi = np.block([[c, -sn], [sn, c]])
    kk = np.arange(64)
    tha = 2 * np.pi * np.outer(kk, n2) / 128.0
    thb = 2 * np.pi * np.outer(64 + 128 * kk, n2) / N_FFT
    g0 = np.zeros((256, 256))
    g0[0:64, 0:128] = np.cos(tha)
    g0[64:128, 128:256] = np.cos(thb)
    g0[128:192, 0:128] = -np.sin(tha)
    g0[128, 0:128] = (-1.0) ** n2
    g0[192:256, 128:256] = -np.sin(thb)
    g0i = np.zeros((256, 256))
    g0i[0:128, 0:64] = 2 * np.cos(tha.T)
    g0i[0:128, 0] = 1.0
    g0i[0:128, 128:192] = -2 * np.sin(tha.T)
    g0i[0:128, 128] = (-1.0) ** n2
    g0i[128:256, 64:128] = 2 * np.cos(thb.T)
    g0i[128:256, 192:256] = -2 * np.sin(thb.T)
    q = np.arange(256)
    perm = np.where(q % 16 < 8, 8 * (q // 16) + q % 16, 128 + 8 * (q // 16) + q % 16 - 8)
    gf = np.stack([g0[perm, :], g[perm, :]])
    gb = np.stack([g0i[:, perm], gi[:, perm]])
    n1f = np.arange(128)
    phf = 2 * np.pi * (n2[:, None, None] * s[None, :, None] / N_FFT
                       + n1f[None, None, :] * s[None, :, None] / 128.0)
    full = np.zeros((N_TILES, 128, 128))
    full[:, :64, :] = np.cos(phf)
    full[:, 64:, :] = -np.sin(phf)
    full[:, 0, :] = 1.0
    full[:, 64, :] = (-1.0) ** n1f
    rev = full[:, :, 127:63:-1].copy()
    rev[0, :, 1:] = full[0, :, 127:64:-1]
    rev[0, :, 0] = 0.0
    f1f = np.concatenate([full[:, :, :64], rev], axis=2)
    pair64 = np.arange(128) % 2 * 64 + np.arange(128) // 2
    pair128 = np.arange(256) % 2 * 128 + np.arange(256) // 2
    f1, f1f = f1[:, pair64, :], f1f[:, pair64, :]
    f1i = f1i[:, :, pair64]
    gf = gf[:, :, pair128]
    gb = gb[:, pair128, :]
    return f1, f1i, gf, gb, f1f


def _filter_features():
    pos = np.arange(SEQ, dtype=np.float64)
    t = pos / (SEQ - 1)
    ang = 2.0 * math.pi * pos / SEQ
    bands = np.linspace(1e-4, FILTER_BANDS - 1, FILTER_BANDS)
    z = np.concatenate([t[:, None], np.cos(bands[None, :] * ang[:, None]),
                        -np.sin(bands[None, :] * ang[:, None])], axis=-1)
    z = z.reshape(TILE_ROWS, N_TILES, FILTER_EMB).transpose(1, 0, 2).reshape(SEQ, FILTER_EMB)
    zp = np.zeros((FILTER_HIDDEN, SEQ))
    zp[:FILTER_EMB, :] = z.T
    max_decay = math.log(1e-2) / 0.3
    min_decay = math.log(1e-2) / 1.5
    deltas = np.abs(np.linspace(min_decay, max_decay, HYENA_WIDTH))
    return zp.astype(np.float32), deltas.astype(np.float32)[None, :]


_F1, _F1I, _GF, _GB, _F1F = _dft_tables()
_ZFEAT, _ABS_DELTAS = _filter_features()


def _shift_down(x):
    rows = lax.broadcasted_iota(jnp.int32, x.shape, 0)
    return jnp.where(rows == 0, 0.0, pltpu.roll(x, 1, axis=0))


def _shift_up(x):
    rows = lax.broadcasted_iota(jnp.int32, x.shape, 0)
    return jnp.where(rows == x.shape[0] - 1, 0.0, pltpu.roll(x, x.shape[0] - 1, axis=0))


def _pair(ref, i):
    return jnp.concatenate([ref[0, i], ref[1, i]], axis=-1).astype(F32)


def _dup(x):
    return jnp.concatenate([x, x], axis=-1)


def _silu(z):
    hz = 0.5 * z
    return hz * (1.0 + jnp.tanh(hz))


def _tile_rows(n2):
    return pl.ds(pl.multiple_of(n2 * A_STRIDE, 8), N_SLOTS)


def _slot_rows(s):
    return pl.ds(s, N_TILES, stride=A_STRIDE)


def _stage_load(a_ref, rows):
    words = jnp.concatenate([a_ref[0, rows, :], a_ref[1, rows, :]], axis=-1)
    return pltpu.bitcast(words, BF16)


def _stage_store(a_ref, rows, val):
    words = pltpu.bitcast(val.astype(BF16), jnp.uint32)
    a_ref[0, rows, :] = words[:, :LANES]
    a_ref[1, rows, :] = words[:, LANES:]


N_BATCHES = N_SLOTS // SLOT_BATCH
SPEC_BLOCKS = 16


def _batch_slots(i):
    return [i * SLOT_BATCH + j for j in range(SLOT_BATCH)]


def _stage2_forward(a_ref, gf_ref, slots, sel0):
    xs = [_stage_load(a_ref, _slot_rows(s)) for s in slots]
    return [jnp.dot(gf_ref[sel0] if j == 0 else gf_ref[1], x, preferred_element_type=F32)
            for j, x in enumerate(xs)]


def _re_im_blocks(y):
    return [(y[16 * i:16 * i + 8], y[16 * i + 8:16 * i + 16]) for i in range(SPEC_BLOCKS)]


_GATE_CHUNKS = (3, 5)


def _cast_weights_once(w_ref, w_scr, ncol):
    @pl.when((pl.program_id(0) == 0) & (pl.program_id(1) == 0))
    def _():
        for c in range(w_ref.shape[1] // ncol):
            w_scr[:, c * ncol:(c + 1) * ncol] = w_ref[:, c * ncol:(c + 1) * ncol].astype(BF16)


def _in_proj_kernel(x_ref, g_ref, w32_ref, o_ref, h_scr, w_ref):
    _cast_weights_once(w32_ref, w_ref, 512)
    ncol = 512
    for sb in range(IN_TILES // SUB_TILES):
        j0 = sb * SUB_TILES
        x = x_ref[:, j0:j0 + SUB_TILES, :].reshape(TILE_ROWS * SUB_TILES, D_MODEL)
        ms = jnp.mean(x * x, axis=-1, keepdims=True)
        hn = x * lax.rsqrt(ms + EPS) * g_ref[...]
        for k in range(D_MODEL // LANES):
            h_scr[sb, k] = hn[:, k * LANES:(k + 1) * LANES]
        h = jnp.concatenate(
            [jnp.concatenate([h_scr[sb, k, pl.ds(j, TILE_ROWS, stride=SUB_TILES), :]
                              for k in range(D_MODEL // LANES)], axis=-1).astype(BF16)
             for j in range(SUB_TILES)], axis=0)
        for c in range(PROJ_WIDTH // ncol):
            p = jnp.dot(h, w_ref[:, c * ncol:(c + 1) * ncol], preferred_element_type=F32)
            if c in _GATE_CHUNKS:
                p = _silu(p)
            p = p.astype(BF16)
            for cb in range(ncol // LANES):
                for j in range(SUB_TILES):
                    o_ref[c * (ncol // LANES) + cb, j0 + j] = p[j * TILE_ROWS:(j + 1) * TILE_ROWS,
                                                                cb * LANES:(cb + 1) * LANES]


def _in_proj(x4, pre_g, w_in):
    grid = (BATCH, N_TILES // IN_TILES)
    return pl.pallas_call(
        _in_proj_kernel,
        grid=grid,
        in_specs=[
            pl.BlockSpec((None, TILE_ROWS, IN_TILES, D_MODEL), lambda b, g: (b, 0, g, 0)),
            pl.BlockSpec((1, D_MODEL), lambda b, g: (0, 0)),
            pl.BlockSpec((D_MODEL, PROJ_WIDTH), lambda b, g: (0, 0), pipeline_mode=pl.Buffered(1)),
        ],
        out_specs=pl.BlockSpec((None, PROJ_BLOCKS, IN_TILES, TILE_ROWS, LANES), lambda b, g: (b, 0, g, 0, 0)),
        out_shape=jax.ShapeDtypeStruct((BATCH, PROJ_BLOCKS, N_TILES, TILE_ROWS, LANES), BF16),
        scratch_shapes=[pltpu.VMEM((IN_TILES // SUB_TILES, D_MODEL // LANES, TILE_ROWS * SUB_TILES, LANES), F32),
                        pltpu.VMEM((D_MODEL, PROJ_WIDTH), BF16)],
        compiler_params=pltpu.CompilerParams(
            dimension_semantics=("arbitrary", "arbitrary"), vmem_limit_bytes=VMEM_LIMIT),
        name="in_proj",
    )(x4, pre_g, w_in)


def _split_bf16(x):
    hi = x.astype(BF16)
    return hi, (x - hi.astype(F32)).astype(BF16)


def _dot_split(a, b):
    a_hi, a_lo = _split_bf16(a)
    b_hi, b_lo = _split_bf16(b)
    dot = lambda u, v: jnp.dot(u, v, preferred_element_type=F32)
    return dot(a_hi, b_hi) + dot(a_lo, b_hi) + dot(a_hi, b_lo)


def _filter_mlp_kernel(z_ref, p_ref, wp_ref, dl_ref, o_ref):
    hid = FILTER_HIDDEN
    cols_blk = p_ref[3]
    freq = cols_blk[:, 3:4]
    h = z_ref[...]
    for k in range(3):
        h = jnp.sin(freq * (_dot_split(p_ref[k][:, :hid], h) + cols_blk[:, k:k + 1]))
    h = jnp.concatenate([h, jnp.zeros_like(h)], axis=0)
    hi, lo = _split_bf16(h.T)
    hs = jnp.concatenate([hi, lo], axis=1)
    wp_hi, wp_lo = _split_bf16(wp_ref[...])
    zpad = jnp.zeros((LANES - hid, 4 * HYENA_WIDTH), BF16)
    wp_hi, wp_lo = jnp.concatenate([wp_hi, zpad], axis=0), jnp.concatenate([wp_lo, zpad], axis=0)
    wp2 = jnp.concatenate([wp_hi, wp_hi], axis=0)
    rows = CHUNK * TILE_ROWS
    r = lax.broadcasted_iota(jnp.int32, (rows, HYENA_WIDTH), 0)
    pos = 128 * (r % TILE_ROWS) + CHUNK * pl.program_id(0) + r // TILE_ROWS
    t = pos.astype(F32) / float(SEQ - 1)
    decay = jnp.exp(-t * dl_ref[...])
    for od in range(4):
        cols = slice(od * HYENA_WIDTH, (od + 1) * HYENA_WIDTH)
        taps = (jnp.dot(hs, wp2[:, cols], preferred_element_type=F32)
                + jnp.dot(hi, wp_lo[:, cols], preferred_element_type=F32)) * decay
        taps = taps.astype(BF16)
        for cb in range(N_CBLK):
            for i in range(CHUNK):
                o_ref[od * N_CBLK + cb, i] = taps[i * TILE_ROWS:(i + 1) * TILE_ROWS, cb * LANES:(cb + 1) * LANES]


def _filter_mlp(zfeat_t, w1, b1, w2, b2, w3, b3, freq, w_proj, abs_deltas):
    hid = FILTER_HIDDEN
    lane_pad = lambda m: jnp.pad(m, ((0, 0), (0, LANES - m.shape[1])))
    params = jnp.stack([lane_pad(jnp.pad(w1, ((0, hid - FILTER_EMB), (0, 0))).T), lane_pad(w2.T), lane_pad(w3.T),
                        lane_pad(jnp.stack([b1, b2, b3, freq], axis=1))])
    full = lambda shape: pl.BlockSpec(shape, lambda c: (0,) * len(shape))
    cols = CHUNK * TILE_ROWS
    return pl.pallas_call(
        _filter_mlp_kernel,
        grid=(N_CHUNKS,),
        in_specs=[
            pl.BlockSpec((hid, cols), lambda c: (0, c)),
            full((4, hid, LANES)),
            full((hid, 4 * HYENA_WIDTH)),
            full((1, HYENA_WIDTH)),
        ],
        out_specs=pl.BlockSpec((4 * N_CBLK, CHUNK, TILE_ROWS, LANES), lambda c: (0, c, 0, 0)),
        out_shape=jax.ShapeDtypeStruct((4 * N_CBLK, N_TILES, TILE_ROWS, LANES), BF16),
        compiler_params=pltpu.CompilerParams(dimension_semantics=("arbitrary",), vmem_limit_bytes=VMEM_LIMIT),
        name="filter_mlp",
    )(zfeat_t, params, w_proj, abs_deltas)


def _filter_spec_kernel(fa_ref, fb_ref, ba_ref, bb_ref, d_ref, f1f_ref, gf_ref, k_ref, a_scr):
    def stage1(sb, carry):
        for i in range(HY_SUB):
            n2 = sb * HY_SUB + i
            nb = (N_TILES - n2) % N_TILES
            fwd = jnp.concatenate([fa_ref[n2], fb_ref[n2]], axis=-1)
            bwd = jnp.concatenate([ba_ref[nb], bb_ref[nb]], axis=-1)
            taps = jnp.concatenate([fwd, bwd], axis=0)
            _stage_store(a_scr, _tile_rows(n2), jnp.dot(f1f_ref[n2], taps, preferred_element_type=F32))
        return carry

    lax.fori_loop(0, N_SUB, stage1, 0)

    d = d_ref[pl.ds(pl.program_id(0), 1), :]

    def batch(slots, first):
        ys = _stage2_forward(a_scr, gf_ref, slots, 0 if first else 1)
        for j, (s, y) in enumerate(zip(slots, ys)):
            blocks = []
            for i, (yr, yi) in enumerate(_re_im_blocks(y)):
                yr = yr + d
                if first and j == 0 and i == 0:
                    rows = lax.broadcasted_iota(jnp.int32, yi.shape, 0)
                    yi = yi + jnp.where(rows == 0, d, 0.0)
                blocks += [yr, yi]
            spec = jnp.concatenate(blocks, axis=0).astype(BF16)
            k_ref[0, s] = spec[:, :LANES]
            k_ref[1, s] = spec[:, LANES:]

    batch(_batch_slots(0), True)

    def loop(i, carry):
        batch(_batch_slots(i), False)
        return carry

    lax.fori_loop(1, N_BATCHES, loop, 0, unroll=3)


def _filter_spec(taps, hyena_d, f1f, gf):
    grid = (2, N_CBLK // 2)
    const = lambda shape: pl.BlockSpec(shape, lambda o, c: (0,) * len(shape), pipeline_mode=pl.Buffered(1))

    def taps_spec(direction, k):
        return pl.BlockSpec((None, N_TILES, TILE_ROWS, LANES),
                            lambda o, c: ((2 * o + direction) * N_CBLK + 2 * c + k, 0, 0, 0))

    return pl.pallas_call(
        _filter_spec_kernel,
        grid=grid,
        in_specs=[taps_spec(0, 0), taps_spec(0, 1), taps_spec(1, 0), taps_spec(1, 1),
                  pl.BlockSpec((2, 2 * LANES), lambda o, c: (0, c)),
                  const((N_TILES, 128, 128)), const((2, 256, 256))],
        out_specs=pl.BlockSpec((None, 2, N_SLOTS, 256, LANES), lambda o, c: (o, c, 0, 0, 0)),
        out_shape=jax.ShapeDtypeStruct((2, N_CBLK, N_SLOTS, 256, LANES), BF16),
        scratch_shapes=[pltpu.VMEM((2, N_TILES * A_STRIDE, LANES), jnp.uint32)],
        compiler_params=pltpu.CompilerParams(
            dimension_semantics=("arbitrary", "arbitrary"), vmem_limit_bytes=VMEM_LIMIT),
        name="filter_spec",
    )(taps, taps, taps, taps, hyena_d, f1f, gf)


_T_S1 = 0
_T_F0 = 1
_T_M = 2
_T_F1 = 3
_T_E = 4
_T_END = 5
N_SUB = N_TILES // HY_SUB
MID_GROUP = 8


def _hyena_kernel(cin_ref, pz_ref, cw_ref, k_ref, f1_ref, f1i_ref, gf_ref, gb_ref, o_ref, a_scr, z_scr):
    t = pl.program_id(2)

    def stage1(n2, u_bf):
        _stage_store(a_scr, _tile_rows(n2), jnp.dot(f1_ref[n2], u_bf, preferred_element_type=F32))

    def inv_stage1(n2):
        return jnp.dot(f1i_ref[n2], _stage_load(a_scr, _tile_rows(n2)), preferred_element_type=F32)

    def short_conv(sb, row0):
        base = sb * HY_SUB
        first = _pair(cin_ref, (base + N_TILES - 1) % N_TILES)
        first = jnp.where(sb == 0, _shift_down(first), first)
        last = _pair(cin_ref, (base + HY_SUB) % N_TILES)
        last = jnp.where(sb == N_SUB - 1, _shift_up(last), last)
        tiles = [first] + [_pair(cin_ref, base + i) for i in range(HY_SUB)] + [last]
        w = cw_ref[row0:row0 + 4, :]
        w0, w1, w2, b = _dup(w[0:1]), _dup(w[1:2]), _dup(w[2:3]), _dup(w[3:4])
        return [tiles[i] * w0 + tiles[i + 1] * w1 + tiles[i + 2] * w2 + b for i in range(HY_SUB)]

    def filter_multiply(slots, ys, first, buf):
        for j, (s, y) in enumerate(zip(slots, ys)):
            blocks = []
            for i, (yr, yi) in enumerate(_re_im_blocks(y)):
                kblk = k_ref[s, 16 * i:16 * i + 16, :].astype(F32)
                kr, ki = kblk[:8], kblk[8:]
                if first and j == 0 and i == 0:
                    rows = lax.broadcasted_iota(jnp.int32, kr.shape, 0)
                    ka, kb, kd = kr, jnp.where(rows == 0, 0.0, ki), jnp.where(rows == 0, ki, kr)
                else:
                    ka, kb, kd = kr, ki, kr
                ka, kb, kd = _dup(ka), _dup(kb), _dup(kd)
                blocks += [yr * ka - yi * kb, yr * kb + yi * kd]
            z_scr[buf, j] = jnp.concatenate(blocks, axis=0).astype(BF16)

    def stage2_inverse(slots, first_sel, buf):
        bms = [jnp.dot(gb_ref[first_sel] if j == 0 else gb_ref[1], z_scr[buf, j], preferred_element_type=F32)
               for j in range(SLOT_BATCH)]
        for s, bm in zip(slots, bms):
            _stage_store(a_scr, _slot_rows(s), bm)

    def spectral_phase():
        slots0 = _batch_slots(0)
        filter_multiply(slots0, _stage2_forward(a_scr, gf_ref, slots0, 0), True, 0)

        def loop(i, carry):
            slots = _batch_slots(i)
            ys = _stage2_forward(a_scr, gf_ref, slots, 1)
            stage2_inverse(_batch_slots(i - 1), jnp.where(i == 1, 0, 1), (i - 1) % 2)
            filter_multiply(slots, ys, False, i % 2)
            return carry

        lax.fori_loop(1, N_BATCHES, loop, 0, unroll=5)
        stage2_inverse(_batch_slots(N_BATCHES - 1), 1, (N_BATCHES - 1) % 2)

    @pl.when(t == _T_S1)
    def _():
        def sub(sb, carry):
            vs = short_conv(sb, 0)
            for i in range(HY_SUB):
                stage1(sb * HY_SUB + i, vs[i].astype(BF16))
            return carry

        lax.fori_loop(0, N_SUB, sub, 0)

    @pl.when((t == _T_F0) | (t == _T_F1))
    def _():
        spectral_phase()

    @pl.when(t == _T_M)
    def _():
        def sub(sb, carry):
            gates = short_conv(sb, 4)
            for q in range(0, HY_SUB, MID_GROUP):
                ys = [inv_stage1(sb * HY_SUB + q + i) for i in range(MID_GROUP)]
                us = [(gates[q + i] * ys[i]).astype(BF16) for i in range(MID_GROUP)]
                for i in range(MID_GROUP):
                    stage1(sb * HY_SUB + q + i, us[i])
            return carry

        lax.fori_loop(0, N_SUB, sub, 0)

    @pl.when(t == _T_E)
    def _():
        def sub(sb, carry):
            gates = short_conv(sb, 8)
            for i in range(HY_SUB):
                n2 = sb * HY_SUB + i
                res = (gates[i] * inv_stage1(n2) * _pair(pz_ref, n2)).astype(BF16)
                o_ref[0, n2] = res[:, :LANES]
                o_ref[1, n2] = res[:, LANES:]
            return carry

        lax.fori_loop(0, N_SUB, sub, 0)


def _hyena(p5, conv_w, conv_b, kspec, f1, f1i, gf, gb):
    grid = (N_CBLK, BATCH // 2, _T_END)
    cw = jnp.concatenate(
        [jnp.concatenate([conv_w[:, k * HYENA_WIDTH:(k + 1) * HYENA_WIDTH],
                          conv_b[:, k * HYENA_WIDTH:(k + 1) * HYENA_WIDTH]], axis=0) for k in range(3)], axis=0)
    seq_block = (2, None, N_TILES, TILE_ROWS, LANES)

    def conv_in_map(c, b, t):
        return (b, jnp.where(t < _T_M, 0, jnp.where(t < _T_E, 1, 2)) * N_CBLK + c, 0, 0, 0)

    def z_map(c, b, t):
        flat = c * (BATCH // 2) + b
        sel = jnp.where(t >= _T_M, flat, jnp.maximum(flat - 1, 0))
        return (sel % (BATCH // 2), 3 * N_CBLK + sel // (BATCH // 2), 0, 0, 0)

    const = lambda shape: pl.BlockSpec(shape, lambda c, b, t: (0,) * len(shape), pipeline_mode=pl.Buffered(1))
    in_specs = [
        pl.BlockSpec(seq_block, conv_in_map),
        pl.BlockSpec(seq_block, z_map),
        pl.BlockSpec((12, LANES), lambda c, b, t: (0, c)),
        pl.BlockSpec((None, None, N_SLOTS, 256, LANES), lambda c, b, t: (jnp.where(t >= _T_M, 1, 0), c, 0, 0, 0)),
        const((N_TILES, 128, TILE_ROWS)), const((N_TILES, TILE_ROWS, 128)),
        const((2, 256, 256)), const((2, 256, 256)),
    ]
    return pl.pallas_call(
        _hyena_kernel,
        grid=grid,
        in_specs=in_specs,
        out_specs=pl.BlockSpec(seq_block, lambda c, b, t: (b, c, 0, 0, 0)),
        out_shape=jax.ShapeDtypeStruct((BATCH, N_CBLK, N_TILES, TILE_ROWS, LANES), BF16),
        scratch_shapes=[pltpu.VMEM((2, N_TILES * A_STRIDE, LANES), jnp.uint32),
                        pltpu.VMEM((2, SLOT_BATCH, 256, 2 * LANES), BF16)],
        compiler_params=pltpu.CompilerParams(
            dimension_semantics=("arbitrary", "arbitrary", "arbitrary"), vmem_limit_bytes=VMEM_LIMIT),
        name="hyena",
    )(p5, p5, cw, kspec, f1, f1i, gf, gb)


POOL_HALO = 8


def _out_kernel(yh_ref, up_ref, upp_ref, upn_ref, zp_ref, x_ref, pw_ref, ps_ref, gh_ref, gp_ref, w32_ref, gpost_ref,
                o_ref, r_scr, yp_scr, w_ref):
    _cast_weights_once(w32_ref, w_ref, 512)
    g = pl.program_id(1)
    last_step = N_TILES // OUT_TILES - 1
    row = lax.broadcasted_iota(jnp.int32, (TILE_ROWS, LANES), 0)

    def pool_tile(ci, l):
        if l < 0:
            t = upp_ref[ci, POOL_HALO + l].astype(F32)
            return jnp.where(g == 0, _shift_down(t), t)
        if l >= OUT_TILES:
            t = upn_ref[ci, l - OUT_TILES].astype(F32)
            return jnp.where(g == last_step, _shift_up(t), t)
        return up_ref[ci, l].astype(F32)

    for ci, w in enumerate(POOL_WINDOWS):
        lo, hi = w // 2, w - 1 - w // 2
        window_sum = pool_tile(ci, -lo)
        for d in range(-lo + 1, hi + 1):
            window_sum = window_sum + pool_tile(ci, d)
        for sb in range(OUT_TILES // SUB_TILES):
            pooled = []
            for i in range(SUB_TILES):
                l = sb * SUB_TILES + i
                if l > 0:
                    window_sum = window_sum + pool_tile(ci, l + hi) - pool_tile(ci, l - 1 - lo)
                inv_cnt = 1.0 / w
                if l < lo:
                    inv_cnt = jnp.where((g == 0) & (row == 0), 1.0 / (hi + l + 1), inv_cnt)
                elif l > OUT_TILES - 1 - hi:
                    inv_cnt = jnp.where((g == last_step) & (row == TILE_ROWS - 1),
                                        1.0 / (lo + OUT_TILES - l), inv_cnt)
                pooled.append((window_sum * inv_cnt - pool_tile(ci, l)).astype(BF16))
            y = jnp.dot(jnp.concatenate(pooled, axis=0), pw_ref[ci].astype(BF16), preferred_element_type=F32)
            y = y * ps_ref[:, ci * LANES:(ci + 1) * LANES]
            gate = jnp.concatenate([zp_ref[ci, sb * SUB_TILES + i] for i in range(SUB_TILES)], axis=0).astype(F32)
            yp_scr[sb, ci] = y * gate

    def rms(y, gain_ref):
        ms = jnp.mean(y * y, axis=-1, keepdims=True)
        return (y * lax.rsqrt(ms + EPS) * gain_ref[...]).astype(BF16)

    for sb in range(OUT_TILES // SUB_TILES):
        j0 = sb * SUB_TILES
        yh = jnp.concatenate(
            [jnp.concatenate([yh_ref[cb, j] for cb in range(N_CBLK)], axis=-1) for j in range(j0, j0 + SUB_TILES)],
            axis=0).astype(F32)
        yp = jnp.concatenate([yp_scr[sb, ci] for ci in range(N_CBLK)], axis=-1)
        yc = jnp.concatenate([rms(yh, gh_ref), rms(yp, gp_ref)], axis=-1)
        out = jnp.dot(yc, w_ref[...], preferred_element_type=F32)
        ms = jnp.mean(out * out, axis=-1, keepdims=True)
        out = out * lax.rsqrt(ms + EPS) * gpost_ref[...]
        for j in range(SUB_TILES):
            for k in range(D_MODEL // LANES):
                r_scr[sb, k, pl.ds(j, TILE_ROWS, stride=SUB_TILES), :] = out[j * TILE_ROWS:(j + 1) * TILE_ROWS,
                                                                             k * LANES:(k + 1) * LANES]
        r = jnp.concatenate([r_scr[sb, k] for k in range(D_MODEL // LANES)], axis=-1)
        o_ref[:, j0:j0 + SUB_TILES, :] = (x_ref[:, j0:j0 + SUB_TILES, :]
                                          + r.reshape(TILE_ROWS, SUB_TILES, D_MODEL))


def _out_proj(yh, p5, x4, pool_w, pool_scale, norm_h_g, norm_p_g, w_out, post_g):
    n_steps = N_TILES // OUT_TILES
    halo_blocks = N_TILES // POOL_HALO
    per_step = OUT_TILES // POOL_HALO
    u_group, z_group = 4, 5
    y_spec = pl.BlockSpec((None, N_CBLK, OUT_TILES, TILE_ROWS, LANES), lambda b, g: (b, 0, g, 0, 0))
    x_spec = pl.BlockSpec((None, TILE_ROWS, OUT_TILES, D_MODEL), lambda b, g: (b, 0, g, 0))
    full = lambda shape: pl.BlockSpec(shape, lambda b, g: (0,) * len(shape))
    return pl.pallas_call(
        _out_kernel,
        grid=(BATCH, n_steps),
        in_specs=[
            y_spec,
            pl.BlockSpec((None, N_CBLK, OUT_TILES, TILE_ROWS, LANES), lambda b, g: (b, u_group, g, 0, 0)),
            pl.BlockSpec((None, N_CBLK, POOL_HALO, TILE_ROWS, LANES),
                         lambda b, g: (b, u_group, (g * per_step + halo_blocks - 1) % halo_blocks, 0, 0)),
            pl.BlockSpec((None, N_CBLK, POOL_HALO, TILE_ROWS, LANES),
                         lambda b, g: (b, u_group, (g * per_step + per_step) % halo_blocks, 0, 0)),
            pl.BlockSpec((None, N_CBLK, OUT_TILES, TILE_ROWS, LANES), lambda b, g: (b, z_group, g, 0, 0)),
            x_spec,
            full((N_CBLK, LANES, LANES)), full((1, POOL_WIDTH)),
            full((1, HYENA_WIDTH)), full((1, POOL_WIDTH)),
            pl.BlockSpec((D_MODEL, D_MODEL), lambda b, g: (0, 0), pipeline_mode=pl.Buffered(1)), full((1, D_MODEL)),
        ],
        out_specs=x_spec,
        out_shape=jax.ShapeDtypeStruct((BATCH, TILE_ROWS, N_TILES, D_MODEL), F32),
        scratch_shapes=[pltpu.VMEM((OUT_TILES // SUB_TILES, D_MODEL // LANES, TILE_ROWS * SUB_TILES, LANES), F32),
                        pltpu.VMEM((OUT_TILES // SUB_TILES, N_CBLK, TILE_ROWS * SUB_TILES, LANES), F32),
                        pltpu.VMEM((D_MODEL, D_MODEL), BF16)],
        compiler_params=pltpu.CompilerParams(
            dimension_semantics=("arbitrary", "arbitrary"), vmem_limit_bytes=VMEM_LIMIT),
        name="out_proj",
    )(yh, p5, p5, p5, p5, x4, pool_w, pool_scale, norm_h_g, norm_p_g, w_out, post_g)


def kernel(x, pre_norm_g, w_in, conv_w, conv_b, filt_w1, filt_b1, filt_w2, filt_b2, filt_w3, filt_b3,
           filt_freq, filt_w_out, hyena_d, pool_w, pool_scale, norm_h_g, norm_p_g, w_out, post_norm_g):
    assert x.shape == (BATCH, SEQ, D_MODEL) and pre_norm_g.shape[0] == 1
    f1, f1i, gf, gb, f1f = (jnp.asarray(m, F32).astype(BF16) for m in (_F1, _F1I, _GF, _GB, _F1F))

    x4 = x.reshape(BATCH, TILE_ROWS, N_TILES, D_MODEL)
    p5 = _in_proj(x4, pre_norm_g, w_in[0])

    taps = _filter_mlp(jnp.asarray(_ZFEAT), filt_w1[0], filt_b1[0], filt_w2[0], filt_b2[0], filt_w3[0], filt_b3[0],
                       filt_freq[0], filt_w_out[0], jnp.asarray(_ABS_DELTAS))
    kspec = _filter_spec(taps, hyena_d[0], f1f, gf)

    yh = _hyena(p5, conv_w[0], conv_b, kspec, f1, f1i, gf, gb)
    out4 = _out_proj(yh, p5, x4, pool_w[0], pool_scale, norm_h_g, norm_p_g,
                     w_out[0], post_norm_g)
    return out4.reshape(BATCH, SEQ, D_MODEL)
```

```python
import functools
import math

import numpy as np
import jax
import jax.numpy as jnp
from jax import lax
from jax.experimental import pallas as pl
from jax.experimental.pallas import tpu as pltpu

F32 = jnp.float32
BF16 = jnp.bfloat16

D_MODEL = 1024
BATCH = 4
SEQ = 8192
HYENA_WIDTH = 512
POOL_WIDTH = 512
POOL_WINDOWS = (2, 4, 8, 16)
FILTER_EMB = 33
FILTER_BANDS = 16
FILTER_HIDDEN = 64
PROJ_WIDTH = 3072
EPS = 1e-6

LANES = 128
N_FFT = 2 * SEQ
N_TILES = 128
TILE_ROWS = SEQ // N_TILES
N_SLOTS = 64
CHUNK = 32
HY_SUB = 32
N_CHUNKS = N_TILES // CHUNK
A_STRIDE = 72
SLOT_BATCH = 4
N_CBLK = HYENA_WIDTH // LANES
PROJ_BLOCKS = PROJ_WIDTH // LANES
IN_TILES = 16
OUT_TILES = 16
SUB_TILES = 8
VMEM_LIMIT = 60 * 1024 * 1024


def _dft_tables():
    n1 = np.arange(TILE_ROWS)
    n2 = np.arange(N_TILES)
    s = np.arange(N_SLOTS)
    ph = 2 * np.pi * (n2[:, None, None] * s[None, :, None] / N_FFT
                      + n1[None, None, :] * s[None, :, None] / 128.0)
    f1 = np.zeros((N_TILES, 128, TILE_ROWS))
    f1[:, :64, :] = np.cos(ph)
    f1[:, 64:, :] = -np.sin(ph)
    f1[:, 0, :] = 1.0
    f1[:, 64, :] = (-1.0) ** n1
    php = np.transpose(ph, (0, 2, 1))
    f1i = np.zeros((N_TILES, TILE_ROWS, 128))
    f1i[:, :, :64] = 2 * np.cos(php) / N_FFT
    f1i[:, :, 64:] = -2 * np.sin(php) / N_FFT
    f1i[:, :, 0] = 1.0 / N_FFT
    f1i[:, :, 64] = ((-1.0) ** n1)[None, :] / N_FFT
    k2 = np.arange(128)
    th = 2 * np.pi * np.outer(k2, n2) / 128.0
    c, sn = np.cos(th), np.sin(th)
    g = np.block([[c, sn], [-sn, c]])
    gi = np.block([[c, -sn], [sn, c]])
    kk = np.arange(64)
    tha = 2 * np.pi * np.outer(kk, n2) / 128.0
    thb = 2 * np.pi * np.outer(64 + 128 * kk, n2) / N_FFT
    g0 = np.zeros((256, 256))
    g0[0:64, 0:128] = np.cos(tha)
    g0[64:128, 128:256] = np.cos(thb)
    g0[128:192, 0:128] = -np.sin(tha)
    g0[128, 0:128] = (-1.0) ** n2
    g0[192:256, 128:256] = -np.sin(thb)
    g0i = np.zeros((256, 256))
    g0i[0:128, 0:64] = 2 * np.cos(tha.T)
    g0i[0:128, 0] = 1.0
    g0i[0:128, 128:192] = -2 * np.sin(tha.T)
    g0i[0:128, 128] = (-1.0) ** n2
    g0i[128:256, 64:128] = 2 * np.cos(thb.T)
    g0i[128:256, 192:256] = -2 * np.sin(thb.T)
    q = np.arange(256)
    perm = np.where(q % 16 < 8, 8 * (q // 16) + q % 16, 128 + 8 * (q // 16) + q % 16 - 8)
    gf = np.stack([g0[perm, :], g[perm, :]])
    gb = np.stack([g0i[:, perm], gi[:, perm]])
    n1f = np.arange(128)
    phf = 2 * np.pi * (n2[:, None, None] * s[None, :, None] / N_FFT
                       + n1f[None, None, :] * s[None, :, None] / 128.0)
    full = np.zeros((N_TILES, 128, 128))
    full[:, :64, :] = np.cos(phf)
    full[:, 64:, :] = -np.sin(phf)
    full[:, 0, :] = 1.0
    full[:, 64, :] = (-1.0) ** n1f
    rev = full[:, :, 127:63:-1].copy()
    rev[0, :, 1:] = full[0, :, 127:64:-1]
    rev[0, :, 0] = 0.0
    f1f = np.concatenate([full[:, :, :64], rev], axis=2)
    pair64 = np.arange(128) % 2 * 64 + np.arange(128) // 2
    pair128 = np.arange(256) % 2 * 128 + np.arange(256) // 2
    f1, f1f = f1[:, pair64, :], f1f[:, pair64, :]
    f1i = f1i[:, :, pair64]
    gf = gf[:, :, pair128]
    gb = gb[:, pair128, :]
    return f1, f1i, gf, gb, f1f


def _filter_features():
    pos = np.arange(SEQ, dtype=np.float64)
    t = pos / (SEQ - 1)
    ang = 2.0 * math.pi * pos / SEQ
    bands = np.linspace(1e-4, FILTER_BANDS - 1, FILTER_BANDS)
    z = np.concatenate([t[:, None], np.cos(bands[None, :] * ang[:, None]),
                        -np.sin(bands[None, :] * ang[:, None])], axis=-1)
    z = z.reshape(TILE_ROWS, N_TILES, FILTER_EMB).transpose(1, 0, 2).reshape(SEQ, FILTER_EMB)
    zp = np.zeros((FILTER_HIDDEN, SEQ))
    zp[:FILTER_EMB, :] = z.T
    max_decay = math.log(1e-2) / 0.3
    min_decay = math.log(1e-2) / 1.5
    deltas = np.abs(np.linspace(min_decay, max_decay, HYENA_WIDTH))
    return zp.astype(np.float32), deltas.astype(np.float32)[None, :]


_F1, _F1I, _GF, _GB, _F1F = _dft_tables()
_ZFEAT, _ABS_DELTAS = _filter_features()


def _shift_down(x):
    rows = lax.broadcasted_iota(jnp.int32, x.shape, 0)
    return jnp.where(rows == 0, 0.0, pltpu.roll(x, 1, axis=0))


def _shift_up(x):
    rows = lax.broadcasted_iota(jnp.int32, x.shape, 0)
    return jnp.where(rows == x.shape[0] - 1, 0.0, pltpu.roll(x, x.shape[0] - 1, axis=0))


def _pair(ref, i):
    return jnp.concatenate([ref[0, i], ref[1, i]], axis=-1).astype(F32)


def _dup(x):
    return jnp.concatenate([x, x], axis=-1)


def _silu(z):
    hz = 0.5 * z
    return hz * (1.0 + jnp.tanh(hz))


def _tile_rows(n2):
    return pl.ds(pl.multiple_of(n2 * A_STRIDE, 8), N_SLOTS)


def _slot_rows(s):
    return pl.ds(s, N_TILES, stride=A_STRIDE)


def _stage_load(a_ref, rows):
    words = jnp.concatenate([a_ref[0, rows, :], a_ref[1, rows, :]], axis=-1)
    return pltpu.bitcast(words, BF16)


def _stage_store(a_ref, rows, val):
    words = pltpu.bitcast(val.astype(BF16), jnp.uint32)
    a_ref[0, rows, :] = words[:, :LANES]
    a_ref[1, rows, :] = words[:, LANES:]


N_BATCHES = N_SLOTS // SLOT_BATCH
SPEC_BLOCKS = 16


def _batch_slots(i):
    return [i * SLOT_BATCH + j for j in range(SLOT_BATCH)]


def _stage2_forward(a_ref, gf_ref, slots, sel0):
    xs = [_stage_load(a_ref, _slot_rows(s)) for s in slots]
    return [jnp.dot(gf_ref[sel0] if j == 0 else gf_ref[1], x, preferred_element_type=F32)
            for j, x in enumerate(xs)]


def _re_im_blocks(y):
    return [(y[16 * i:16 * i + 8], y[16 * i + 8:16 * i + 16]) for i in range(SPEC_BLOCKS)]


_GATE_GROUPS = (3, 5)


def _cast_weights_once(w_ref, w_scr, ncol):
    @pl.when((pl.program_id(0) == 0) & (pl.program_id(1) == 0))
    def _():
        for c in range(w_ref.shape[1] // ncol):
            w_scr[:, c * ncol:(c + 1) * ncol] = w_ref[:, c * ncol:(c + 1) * ncol].astype(BF16)


def _in_proj_kernel(x_ref, g_ref, w32_ref, o_ref, h_scr, w_ref):
    _cast_weights_once(w32_ref, w_ref, 512)
    ncol = 512
    for sb in range(IN_TILES // SUB_TILES):
        j0 = sb * SUB_TILES
        x = x_ref[:, j0:j0 + SUB_TILES, :].reshape(TILE_ROWS * SUB_TILES, D_MODEL)
        ms = jnp.mean(x * x, axis=-1, keepdims=True)
        hn = x * lax.rsqrt(ms + EPS) * g_ref[...]
        for k in range(D_MODEL // LANES):
            h_scr[sb, k] = hn[:, k * LANES:(k + 1) * LANES]
        h = jnp.concatenate(
            [jnp.concatenate([h_scr[sb, k, pl.ds(j, TILE_ROWS, stride=SUB_TILES), :]
                              for k in range(D_MODEL // LANES)], axis=-1).astype(BF16)
             for j in range(SUB_TILES)], axis=0)
        for c in range(PROJ_WIDTH // ncol):
            p = jnp.dot(h, w_ref[:, c * ncol:(c + 1) * ncol], preferred_element_type=F32)
            if (c * ncol) // HYENA_WIDTH in _GATE_GROUPS:
                p = _silu(p)
            p = p.astype(BF16)
            for cb in range(ncol // LANES):
                for j in range(SUB_TILES):
                    o_ref[c * (ncol // LANES) + cb, j0 + j] = p[j * TILE_ROWS:(j + 1) * TILE_ROWS,
                                                                cb * LANES:(cb + 1) * LANES]


def _in_proj(x4, pre_g, w_in):
    grid = (BATCH, N_TILES // IN_TILES)
    return pl.pallas_call(
        _in_proj_kernel,
        grid=grid,
        in_specs=[
            pl.BlockSpec((None, TILE_ROWS, IN_TILES, D_MODEL), lambda b, g: (b, 0, g, 0)),
            pl.BlockSpec((1, D_MODEL), lambda b, g: (0, 0)),
            pl.BlockSpec((D_MODEL, PROJ_WIDTH), lambda b, g: (0, 0), pipeline_mode=pl.Buffered(1)),
        ],
        out_specs=pl.BlockSpec((None, PROJ_BLOCKS, IN_TILES, TILE_ROWS, LANES), lambda b, g: (b, 0, g, 0, 0)),
        out_shape=jax.ShapeDtypeStruct((BATCH, PROJ_BLOCKS, N_TILES, TILE_ROWS, LANES), BF16),
        scratch_shapes=[pltpu.VMEM((IN_TILES // SUB_TILES, D_MODEL // LANES, TILE_ROWS * SUB_TILES, LANES), F32),
                        pltpu.VMEM((D_MODEL, PROJ_WIDTH), BF16)],
        compiler_params=pltpu.CompilerParams(
            dimension_semantics=("arbitrary", "arbitrary"), vmem_limit_bytes=VMEM_LIMIT),
        name="in_proj",
    )(x4, pre_g, w_in)


def _split_bf16(x):
    hi = x.astype(BF16)
    return hi, (x - hi.astype(F32)).astype(BF16)


def _dot_split(a, b):
    a_hi, a_lo = _split_bf16(a)
    b_hi, b_lo = _split_bf16(b)
    dot = lambda u, v: jnp.dot(u, v, preferred_element_type=F32)
    return dot(a_hi, b_hi) + dot(a_lo, b_hi) + dot(a_hi, b_lo)


def _filter_mlp_kernel(z_ref, p_ref, wp_ref, dl_ref, o_ref):
    hid = FILTER_HIDDEN
    cols_blk = p_ref[3].T[:hid]
    freq = cols_blk[:, 3:4]
    h = z_ref[...]
    for k in range(3):
        w_t = p_ref[k].T[:hid, :hid]
        h = jnp.sin(freq * (_dot_split(w_t, h) + cols_blk[:, k:k + 1]))
    h = jnp.concatenate([h, jnp.zeros_like(h)], axis=0)
    hi, lo = _split_bf16(h.T)
    hs = jnp.concatenate([hi, lo], axis=1)
    wp_hi, wp_lo = _split_bf16(wp_ref[...])
    zpad = jnp.zeros((LANES - hid, 4 * HYENA_WIDTH), BF16)
    wp_hi, wp_lo = jnp.concatenate([wp_hi, zpad], axis=0), jnp.concatenate([wp_lo, zpad], axis=0)
    wp2 = jnp.concatenate([wp_hi, wp_hi], axis=0)
    rows = CHUNK * TILE_ROWS
    r = lax.broadcasted_iota(jnp.int32, (rows, HYENA_WIDTH), 0)
    pos = 128 * (r % TILE_ROWS) + CHUNK * pl.program_id(0) + r // TILE_ROWS
    t = pos.astype(F32) / float(SEQ - 1)
    decay = jnp.exp(-t * dl_ref[...])
    for od in range(4):
        cols = slice(od * HYENA_WIDTH, (od + 1) * HYENA_WIDTH)
        taps = (jnp.dot(hs, wp2[:, cols], preferred_element_type=F32)
                + jnp.dot(hi, wp_lo[:, cols], preferred_element_type=F32)) * decay
        taps = taps.astype(BF16)
        for cb in range(N_CBLK):
            for i in range(CHUNK):
                o_ref[od * N_CBLK + cb, i] = taps[i * TILE_ROWS:(i + 1) * TILE_ROWS, cb * LANES:(cb + 1) * LANES]


def _filter_mlp(zfeat_t, w1, b1, w2, b2, w3, b3, freq, w_proj, abs_deltas):
    hid = FILTER_HIDDEN
    pad2 = lambda m: jnp.pad(m, ((0, LANES - m.shape[0]), (0, LANES - m.shape[1])))
    params = jnp.stack([pad2(w1), pad2(w2), pad2(w3), pad2(jnp.stack([b1, b2, b3, freq], axis=0))])
    full = lambda shape: pl.BlockSpec(shape, lambda c: (0,) * len(shape))
    cols = CHUNK * TILE_ROWS
    return pl.pallas_call(
        _filter_mlp_kernel,
        grid=(N_CHUNKS,),
        in_specs=[
            pl.BlockSpec((hid, cols), lambda c: (0, c)),
            full((4, LANES, LANES)),
            full((hid, 4 * HYENA_WIDTH)),
            full((1, HYENA_WIDTH)),
        ],
        out_specs=pl.BlockSpec((4 * N_CBLK, CHUNK, TILE_ROWS, LANES), lambda c: (0, c, 0, 0)),
        out_shape=jax.ShapeDtypeStruct((4 * N_CBLK, N_TILES, TILE_ROWS, LANES), BF16),
        compiler_params=pltpu.CompilerParams(dimension_semantics=("arbitrary",), vmem_limit_bytes=VMEM_LIMIT),
        name="filter_mlp",
    )(zfeat_t, params, w_proj, abs_deltas)


def _filter_spec_kernel(fa_ref, fb_ref, ba_ref, bb_ref, d_ref, f1f_ref, gf_ref, k_ref, a_scr):
    def stage1(sb, carry):
        for i in range(HY_SUB):
            n2 = sb * HY_SUB + i
            nb = (N_TILES - n2) % N_TILES
            fwd = jnp.concatenate([fa_ref[n2], fb_ref[n2]], axis=-1)
            bwd = jnp.concatenate([ba_ref[nb], bb_ref[nb]], axis=-1)
            taps = jnp.concatenate([fwd, bwd], axis=0)
            _stage_store(a_scr, _tile_rows(n2), jnp.dot(f1f_ref[n2], taps, preferred_element_type=F32))
        return carry

    lax.fori_loop(0, N_SUB, stage1, 0)

    d = d_ref[pl.ds(pl.program_id(0), 1), :]

    def batch(slots, first):
        ys = _stage2_forward(a_scr, gf_ref, slots, 0 if first else 1)
        for j, (s, y) in enumerate(zip(slots, ys)):
            blocks = []
            for i, (yr, yi) in enumerate(_re_im_blocks(y)):
                yr = yr + d
                if first and j == 0 and i == 0:
                    rows = lax.broadcasted_iota(jnp.int32, yi.shape, 0)
                    yi = yi + jnp.where(rows == 0, d, 0.0)
                blocks += [yr, yi]
            spec = jnp.concatenate(blocks, axis=0).astype(BF16)
            k_ref[0, s] = spec[:, :LANES]
            k_ref[1, s] = spec[:, LANES:]

    batch(_batch_slots(0), True)

    def loop(i, carry):
        batch(_batch_slots(i), False)
        return carry

    lax.fori_loop(1, N_BATCHES, loop, 0, unroll=3)


def _filter_spec(taps, hyena_d, f1f, gf):
    grid = (2, N_CBLK // 2)
    const = lambda shape: pl.BlockSpec(shape, lambda o, c: (0,) * len(shape), pipeline_mode=pl.Buffered(1))

    def taps_spec(direction, k):
        return pl.BlockSpec((None, N_TILES, TILE_ROWS, LANES),
                            lambda o, c: ((2 * o + direction) * N_CBLK + 2 * c + k, 0, 0, 0))

    return pl.pallas_call(
        _filter_spec_kernel,
        grid=grid,
        in_specs=[taps_spec(0, 0), taps_spec(0, 1), taps_spec(1, 0), taps_spec(1, 1),
                  pl.BlockSpec((2, 2 * LANES), lambda o, c: (0, c)),
                  const((N_TILES, 128, 128)), const((2, 256, 256))],
        out_specs=pl.BlockSpec((None, 2, N_SLOTS, 256, LANES), lambda o, c: (o, c, 0, 0, 0)),
        out_shape=jax.ShapeDtypeStruct((2, N_CBLK, N_SLOTS, 256, LANES), BF16),
        scratch_shapes=[pltpu.VMEM((2, N_TILES * A_STRIDE, LANES), jnp.uint32)],
        compiler_params=pltpu.CompilerParams(
            dimension_semantics=("arbitrary", "arbitrary"), vmem_limit_bytes=VMEM_LIMIT),
        name="filter_spec",
    )(taps, taps, taps, taps, hyena_d, f1f, gf)


_T_S1 = 0
_T_F0 = 1
_T_M = 2
_T_F1 = 3
_T_E = 4
_T_END = 5
N_SUB = N_TILES // HY_SUB
MID_GROUP = 8

def _hyena_kernel(cin_ref, pz_ref, cw_ref, k_ref, f1_ref, f1i_ref, gf_ref, gb_ref, o_ref, a_scr, z_scr):
    t = pl.program_id(2)

    def stage1(n2, u_bf):
        _stage_store(a_scr, _tile_rows(n2), jnp.dot(f1_ref[n2], u_bf, preferred_element_type=F32))

    def inv_stage1(n2):
        return jnp.dot(f1i_ref[n2], _stage_load(a_scr, _tile_rows(n2)), preferred_element_type=F32)

    def short_conv(sb, row0):
        base = sb * HY_SUB
        first = _pair(cin_ref, (base + N_TILES - 1) % N_TILES)
        first = jnp.where(sb == 0, _shift_down(first), first)
        last = _pair(cin_ref, (base + HY_SUB) % N_TILES)
        last = jnp.where(sb == N_SUB - 1, _shift_up(last), last)
        tiles = [first] + [_pair(cin_ref, base + i) for i in range(HY_SUB)] + [last]
        w = cw_ref[row0:row0 + 4, :]
        w0, w1, w2, b = _dup(w[0:1]), _dup(w[1:2]), _dup(w[2:3]), _dup(w[3:4])
        return [tiles[i] * w0 + tiles[i + 1] * w1 + tiles[i + 2] * w2 + b for i in range(HY_SUB)]

    def filter_multiply(slots, ys, first, buf):
        for j, (s, y) in enumerate(zip(slots, ys)):
            blocks = []
            for i, (yr, yi) in enumerate(_re_im_blocks(y)):
                kblk = k_ref[s, 16 * i:16 * i + 16, :].astype(F32)
                kr, ki = kblk[:8], kblk[8:]
                if first and j == 0 and i == 0:
                    rows = lax.broadcasted_iota(jnp.int32, kr.shape, 0)
                    ka, kb, kd = kr, jnp.where(rows == 0, 0.0, ki), jnp.where(rows == 0, ki, kr)
                else:
                    ka, kb, kd = kr, ki, kr
                ka, kb, kd = _dup(ka), _dup(kb), _dup(kd)
                blocks += [yr * ka - yi * kb, yr * kb + yi * kd]
            z_scr[buf, j] = jnp.concatenate(blocks, axis=0).astype(BF16)

    def stage2_inverse(slots, first_sel, buf):
        bms = [jnp.dot(gb_ref[first_sel] if j == 0 else gb_ref[1], z_scr[buf, j], preferred_element_type=F32)
               for j in range(SLOT_BATCH)]
        for s, bm in zip(slots, bms):
            _stage_store(a_scr, _slot_rows(s), bm)

    def spectral_phase():
        slots0 = _batch_slots(0)
        filter_multiply(slots0, _stage2_forward(a_scr, gf_ref, slots0, 0), True, 0)

        def loop(i, carry):
            slots = _batch_slots(i)
            ys = _stage2_forward(a_scr, gf_ref, slots, 1)
            stage2_inverse(_batch_slots(i - 1), jnp.where(i == 1, 0, 1), (i - 1) % 2)
            filter_multiply(slots, ys, False, i % 2)
            return carry

        lax.fori_loop(1, N_BATCHES, loop, 0, unroll=5)
        stage2_inverse(_batch_slots(N_BATCHES - 1), 1, (N_BATCHES - 1) % 2)

    @pl.when(t == _T_S1)
    def _():
        def sub(sb, carry):
            vs = short_conv(sb, 0)
            for i in range(HY_SUB):
                stage1(sb * HY_SUB + i, vs[i].astype(BF16))
            return carry

        lax.fori_loop(0, N_SUB, sub, 0)

    @pl.when((t == _T_F0) | (t == _T_F1))
    def _():
        spectral_phase()

    @pl.when(t == _T_M)
    def _():
        def sub(sb, carry):
            gates = short_conv(sb, 4)
            for q in range(0, HY_SUB, MID_GROUP):
                ys = [inv_stage1(sb * HY_SUB + q + i) for i in range(MID_GROUP)]
                us = [(gates[q + i] * ys[i]).astype(BF16) for i in range(MID_GROUP)]
                for i in range(MID_GROUP):
                    stage1(sb * HY_SUB + q + i, us[i])
            return carry

        lax.fori_loop(0, N_SUB, sub, 0)

    @pl.when(t == _T_E)
    def _():
        def sub(sb, carry):
            gates = short_conv(sb, 8)
            for i in range(HY_SUB):
                n2 = sb * HY_SUB + i
                res = (gates[i] * inv_stage1(n2) * _pair(pz_ref, n2)).astype(BF16)
                o_ref[0, n2] = res[:, :LANES]
                o_ref[1, n2] = res[:, LANES:]
            return carry

        lax.fori_loop(0, N_SUB, sub, 0)


def _hyena(p5, conv_w, conv_b, kspec, f1, f1i, gf, gb):
    grid = (N_CBLK, BATCH // 2, _T_END)
    cw = jnp.concatenate(
        [jnp.concatenate([conv_w[:, k * HYENA_WIDTH:(k + 1) * HYENA_WIDTH],
                          conv_b[:, k * HYENA_WIDTH:(k + 1) * HYENA_WIDTH]], axis=0) for k in range(3)], axis=0)
    seq_block = (2, None, N_TILES, TILE_ROWS, LANES)

    def conv_in_map(c, b, t):
        return (b, jnp.where(t < _T_M, 0, jnp.where(t < _T_E, 1, 2)) * N_CBLK + c, 0, 0, 0)

    def z_map(c, b, t):
        flat = c * (BATCH // 2) + b
        sel = jnp.where(t >= _T_M, flat, jnp.maximum(flat - 1, 0))
        return (sel % (BATCH // 2), 3 * N_CBLK + sel // (BATCH // 2), 0, 0, 0)

    const = lambda shape: pl.BlockSpec(shape, lambda c, b, t: (0,) * len(shape), pipeline_mode=pl.Buffered(1))
    in_specs = [
        pl.BlockSpec(seq_block, conv_in_map),
        pl.BlockSpec(seq_block, z_map),
        pl.BlockSpec((12, LANES), lambda c, b, t: (0, c)),
        pl.BlockSpec((None, None, N_SLOTS, 256, LANES), lambda c, b, t: (jnp.where(t >= _T_M, 1, 0), c, 0, 0, 0)),
        const((N_TILES, 128, TILE_ROWS)), const((N_TILES, TILE_ROWS, 128)),
        const((2, 256, 256)), const((2, 256, 256)),
    ]
    return pl.pallas_call(
        _hyena_kernel,
        grid=grid,
        in_specs=in_specs,
        out_specs=pl.BlockSpec(seq_block, lambda c, b, t: (b, c, 0, 0, 0)),
        out_shape=jax.ShapeDtypeStruct((BATCH, N_CBLK, N_TILES, TILE_ROWS, LANES), BF16),
        scratch_shapes=[pltpu.VMEM((2, N_TILES * A_STRIDE, LANES), jnp.uint32),
                        pltpu.VMEM((2, SLOT_BATCH, 256, 2 * LANES), BF16)],
        compiler_params=pltpu.CompilerParams(
            dimension_semantics=("arbitrary", "arbitrary", "arbitrary"), vmem_limit_bytes=VMEM_LIMIT),
        name="hyena",
    )(p5, p5, cw, kspec, f1, f1i, gf, gb)


def _out_kernel(yh_ref, up_ref, zp_ref, x_ref, pw_ref, ps_ref, gh_ref, gp_ref, w32_ref, gpost_ref, *rest):
    halo_refs, (o_ref, r_scr, yp_scr, w_ref) = rest[:2 * len(POOL_WINDOWS)], rest[2 * len(POOL_WINDOWS):]
    _cast_weights_once(w32_ref, w_ref, 512)
    g = pl.program_id(1)
    last_step = N_TILES // OUT_TILES - 1
    row = lax.broadcasted_iota(jnp.int32, (TILE_ROWS, LANES), 0)

    def pool_tile(ci, l):
        if l < 0:
            t = halo_refs[2 * ci][POOL_WINDOWS[ci] // 2 + l].astype(F32)
            return jnp.where(g == 0, _shift_down(t), t)
        if l >= OUT_TILES:
            t = halo_refs[2 * ci + 1][l - OUT_TILES].astype(F32)
            return jnp.where(g == last_step, _shift_up(t), t)
        return up_ref[ci, l].astype(F32)

    for ci, w in enumerate(POOL_WINDOWS):
        lo, hi = w // 2, w - 1 - w // 2
        window_sum = pool_tile(ci, -lo)
        for d in range(-lo + 1, hi + 1):
            window_sum = window_sum + pool_tile(ci, d)
        for sb in range(OUT_TILES // SUB_TILES):
            pooled = []
            for i in range(SUB_TILES):
                l = sb * SUB_TILES + i
                if l > 0:
                    window_sum = window_sum + pool_tile(ci, l + hi) - pool_tile(ci, l - 1 - lo)
                inv_cnt = 1.0 / w
                if l < lo:
                    inv_cnt = jnp.where((g == 0) & (row == 0), 1.0 / (hi + l + 1), inv_cnt)
                elif l > OUT_TILES - 1 - hi:
                    inv_cnt = jnp.where((g == last_step) & (row == TILE_ROWS - 1),
                                        1.0 / (lo + OUT_TILES - l), inv_cnt)
                pooled.append((window_sum * inv_cnt - pool_tile(ci, l)).astype(BF16))
            y = jnp.dot(jnp.concatenate(pooled, axis=0), pw_ref[ci].astype(BF16), preferred_element_type=F32)
            y = y * ps_ref[:, ci * LANES:(ci + 1) * LANES]
            gate = jnp.concatenate([zp_ref[ci, sb * SUB_TILES + i] for i in range(SUB_TILES)], axis=0).astype(F32)
            yp_scr[sb, ci] = y * gate

    def rms(y, gain_ref):
        ms = jnp.mean(y * y, axis=-1, keepdims=True)
        return (y * lax.rsqrt(ms + EPS) * gain_ref[...]).astype(BF16)

    for sb in range(OUT_TILES // SUB_TILES):
        j0 = sb * SUB_TILES
        yh = jnp.concatenate(
            [jnp.concatenate([yh_ref[cb, j] for cb in range(N_CBLK)], axis=-1) for j in range(j0, j0 + SUB_TILES)],
            axis=0).astype(F32)
        yp = jnp.concatenate([yp_scr[sb, ci] for ci in range(N_CBLK)], axis=-1)
        yc = jnp.concatenate([rms(yh, gh_ref), rms(yp, gp_ref)], axis=-1)
        out = jnp.dot(yc, w_ref[...], preferred_element_type=F32)
        ms = jnp.mean(out * out, axis=-1, keepdims=True)
        out = out * lax.rsqrt(ms + EPS) * gpost_ref[...]
        for j in range(SUB_TILES):
            for k in range(D_MODEL // LANES):
                r_scr[sb, k, pl.ds(j, TILE_ROWS, stride=SUB_TILES), :] = out[j * TILE_ROWS:(j + 1) * TILE_ROWS,
                                                                             k * LANES:(k + 1) * LANES]
        r = jnp.concatenate([r_scr[sb, k] for k in range(D_MODEL // LANES)], axis=-1)
        o_ref[:, j0:j0 + SUB_TILES, :] = (x_ref[:, j0:j0 + SUB_TILES, :]
                                          + r.reshape(TILE_ROWS, SUB_TILES, D_MODEL))


def _out_proj(yh, p5, x4, pool_w, pool_scale, norm_h_g, norm_p_g, w_out, post_g):
    n_steps = N_TILES // OUT_TILES
    u_group, z_group = 4, 5
    y_spec = pl.BlockSpec((None, N_CBLK, OUT_TILES, TILE_ROWS, LANES), lambda b, g: (b, 0, g, 0, 0))
    x_spec = pl.BlockSpec((None, TILE_ROWS, OUT_TILES, D_MODEL), lambda b, g: (b, 0, g, 0))
    full = lambda shape: pl.BlockSpec(shape, lambda b, g: (0,) * len(shape))

    def halo_specs(ci, w):
        h = w // 2
        n_blocks, per_step = N_TILES // h, OUT_TILES // h
        col = u_group * N_CBLK + ci
        return [pl.BlockSpec((None, None, h, TILE_ROWS, LANES),
                             lambda b, g: (b, col, (g * per_step + n_blocks - 1) % n_blocks, 0, 0)),
                pl.BlockSpec((None, None, h, TILE_ROWS, LANES),
                             lambda b, g: (b, col, (g * per_step + per_step) % n_blocks, 0, 0))]

    halos = [s for ci, w in enumerate(POOL_WINDOWS) for s in halo_specs(ci, w)]
    return pl.pallas_call(
        _out_kernel,
        grid=(BATCH, n_steps),
        in_specs=[
            y_spec,
            pl.BlockSpec((None, N_CBLK, OUT_TILES, TILE_ROWS, LANES), lambda b, g: (b, u_group, g, 0, 0)),
            pl.BlockSpec((None, N_CBLK, OUT_TILES, TILE_ROWS, LANES), lambda b, g: (b, z_group, g, 0, 0)),
            x_spec,
            full((N_CBLK, LANES, LANES)), full((1, POOL_WIDTH)),
            full((1, HYENA_WIDTH)), full((1, POOL_WIDTH)),
            pl.BlockSpec((D_MODEL, D_MODEL), lambda b, g: (0, 0), pipeline_mode=pl.Buffered(1)), full((1, D_MODEL)),
        ] + halos,
        out_specs=x_spec,
        out_shape=jax.ShapeDtypeStruct((BATCH, TILE_ROWS, N_TILES, D_MODEL), F32),
        scratch_shapes=[pltpu.VMEM((OUT_TILES // SUB_TILES, D_MODEL // LANES, TILE_ROWS * SUB_TILES, LANES), F32),
                        pltpu.VMEM((OUT_TILES // SUB_TILES, N_CBLK, TILE_ROWS * SUB_TILES, LANES), F32),
                        pltpu.VMEM((D_MODEL, D_MODEL), BF16)],
        compiler_params=pltpu.CompilerParams(
            dimension_semantics=("arbitrary", "arbitrary"), vmem_limit_bytes=VMEM_LIMIT),
        name="out_proj",
    )(yh, p5, p5, x4, pool_w, pool_scale, norm_h_g, norm_p_g, w_out, post_g, *([p5] * len(halos)))


def kernel(x, pre_norm_g, w_in, conv_w, conv_b, filt_w1, filt_b1, filt_w2, filt_b2, filt_w3, filt_b3,
           filt_freq, filt_w_out, hyena_d, pool_w, pool_scale, norm_h_g, norm_p_g, w_out, post_norm_g):
    assert x.shape == (BATCH, SEQ, D_MODEL) and pre_norm_g.shape[0] == 1
    f1, f1i, gf, gb, f1f = (jnp.asarray(m, F32).astype(BF16) for m in (_F1, _F1I, _GF, _GB, _F1F))

    x4 = x.reshape(BATCH, TILE_ROWS, N_TILES, D_MODEL)
    p5 = _in_proj(x4, pre_norm_g, w_in[0])

    taps = _filter_mlp(jnp.asarray(_ZFEAT), filt_w1[0], filt_b1[0], filt_w2[0], filt_b2[0], filt_w3[0], filt_b3[0],
                       filt_freq[0], filt_w_out[0], jnp.asarray(_ABS_DELTAS))
    kspec = _filter_spec(taps, hyena_d[0], f1f, gf)

    yh = _hyena(p5, conv_w[0], conv_b, kspec, f1, f1i, gf, gb)
    out4 = _out_proj(yh, p5, x4, pool_w[0], pool_scale, norm_h_g, norm_p_g,
                     w_out[0], post_norm_g)
    return out4.reshape(BATCH, SEQ, D_MODEL)
```

```python
import functools
import math

import numpy as np
import jax
import jax.numpy as jnp
from jax import lax
from jax.experimental import pallas as pl
from jax.experimental.pallas import tpu as pltpu

F32 = jnp.float32
BF16 = jnp.bfloat16

D_MODEL = 1024
BATCH = 4
SEQ = 8192
HYENA_WIDTH = 512
POOL_WIDTH = 512
POOL_WINDOWS = (2, 4, 8, 16)
FILTER_EMB = 33
FILTER_BANDS = 16
FILTER_HIDDEN = 64
PROJ_WIDTH = 3072
EPS = 1e-6

LANES = 128
N_FFT = 2 * SEQ
N_TILES = 128
TILE_ROWS = SEQ // N_TILES
N_SLOTS = 64
HY_SUB = 32
A_STRIDE = 72
SLOT_BATCH = 4
N_CBLK = HYENA_WIDTH // LANES
PROJ_BLOCKS = PROJ_WIDTH // LANES
IN_TILES = 16
OUT_TILES = 16
SUB_TILES = 8
VMEM_LIMIT = 60 * 1024 * 1024


def _dft_tables():
    n1 = np.arange(TILE_ROWS)
    n2 = np.arange(N_TILES)
    s = np.arange(N_SLOTS)
    ph = 2 * np.pi * (n2[:, None, None] * s[None, :, None] / N_FFT
                      + n1[None, None, :] * s[None, :, None] / 128.0)
    f1 = np.zeros((N_TILES, 128, TILE_ROWS))
    f1[:, :64, :] = np.cos(ph)
    f1[:, 64:, :] = -np.sin(ph)
    f1[:, 0, :] = 1.0
    f1[:, 64, :] = (-1.0) ** n1
    php = np.transpose(ph, (0, 2, 1))
    f1i = np.zeros((N_TILES, TILE_ROWS, 128))
    f1i[:, :, :64] = 2 * np.cos(php) / N_FFT
    f1i[:, :, 64:] = -2 * np.sin(php) / N_FFT
    f1i[:, :, 0] = 1.0 / N_FFT
    f1i[:, :, 64] = ((-1.0) ** n1)[None, :] / N_FFT
    k2 = np.arange(128)
    th = 2 * np.pi * np.outer(k2, n2) / 128.0
    c, sn = np.cos(th), np.sin(th)
    g = np.block([[c, sn], [-sn, c]])
    gi = np.block([[c, -sn], [sn, c]])
    kk = np.arange(64)
    tha = 2 * np.pi * np.outer(kk, n2) / 128.0
    thb = 2 * np.pi * np.outer(64 + 128 * kk, n2) / N_FFT
    g0 = np.zeros((256, 256))
    g0[0:64, 0:128] = np.cos(tha)
    g0[64:128, 128:256] = np.cos(thb)
    g0[128:192, 0:128] = -np.sin(tha)
    g0[128, 0:128] = (-1.0) ** n2
    g0[192:256, 128:256] = -np.sin(thb)
    g0i = np.zeros((256, 256))
    g0i[0:128, 0:64] = 2 * np.cos(tha.T)
    g0i[0:128, 0] = 1.0
    g0i[0:128, 128:192] = -2 * np.sin(tha.T)
    g0i[0:128, 128] = (-1.0) ** n2
    g0i[128:256, 64:128] = 2 * np.cos(thb.T)
    g0i[128:256, 192:256] = -2 * np.sin(thb.T)
    q = np.arange(256)
    perm = np.where(q % 16 < 8, 8 * (q // 16) + q % 16, 128 + 8 * (q // 16) + q % 16 - 8)
    gf = np.stack([g0[perm, :], g[perm, :]])
    gb = np.stack([g0i[:, perm], gi[:, perm]])
    n1f = np.arange(128)
    phf = 2 * np.pi * (n2[:, None, None] * s[None, :, None] / N_FFT
                       + n1f[None, None, :] * s[None, :, None] / 128.0)
    full = np.zeros((N_TILES, 128, 128))
    full[:, :64, :] = np.cos(phf)
    full[:, 64:, :] = -np.sin(phf)
    full[:, 0, :] = 1.0
    full[:, 64, :] = (-1.0) ** n1f
    rev = full[:, :, 127:63:-1].copy()
    rev[0, :, 1:] = full[0, :, 127:64:-1]
    rev[0, :, 0] = 0.0
    f1f = np.concatenate([full[:, :, :64], rev], axis=2)
    pair64 = np.arange(128) % 2 * 64 + np.arange(128) // 2
    pair128 = np.arange(256) % 2 * 128 + np.arange(256) // 2
    f1, f1f = f1[:, pair64, :], f1f[:, pair64, :]
    f1i = f1i[:, :, pair64]
    gf = gf[:, :, pair128]
    gb = gb[:, pair128, :]
    return f1, f1i, gf, gb, f1f


def _filter_features():
    pos = np.arange(SEQ, dtype=np.float64)
    t = pos / (SEQ - 1)
    ang = 2.0 * math.pi * pos / SEQ
    bands = np.linspace(1e-4, FILTER_BANDS - 1, FILTER_BANDS)
    z = np.concatenate([t[:, None], np.cos(bands[None, :] * ang[:, None]),
                        -np.sin(bands[None, :] * ang[:, None])], axis=-1)
    z = z.reshape(TILE_ROWS, N_TILES, FILTER_EMB).transpose(1, 0, 2).reshape(SEQ, FILTER_EMB)
    zp = np.zeros((FILTER_HIDDEN, SEQ))
    zp[:FILTER_EMB, :] = z.T
    max_decay = math.log(1e-2) / 0.3
    min_decay = math.log(1e-2) / 1.5
    deltas = np.abs(np.linspace(min_decay, max_decay, HYENA_WIDTH))
    return zp.astype(np.float32), deltas.astype(np.float32)[None, :]


_F1, _F1I, _GF, _GB, _F1F = _dft_tables()
_ZFEAT, _ABS_DELTAS = _filter_features()


def _shift_down(x):
    rows = lax.broadcasted_iota(jnp.int32, x.shape, 0)
    return jnp.where(rows == 0, 0.0, pltpu.roll(x, 1, axis=0))


def _shift_up(x):
    rows = lax.broadcasted_iota(jnp.int32, x.shape, 0)
    return jnp.where(rows == x.shape[0] - 1, 0.0, pltpu.roll(x, x.shape[0] - 1, axis=0))


def _pair(ref, i):
    return jnp.concatenate([ref[0, i], ref[1, i]], axis=-1).astype(F32)


def _dup(x):
    return jnp.concatenate([x, x], axis=-1)


def _silu(z):
    hz = 0.5 * z
    return hz * (1.0 + jnp.tanh(hz))


def _tile_rows(n2):
    return pl.ds(pl.multiple_of(n2 * A_STRIDE, 8), N_SLOTS)


def _slot_rows(s):
    return pl.ds(s, N_TILES, stride=A_STRIDE)


def _stage_load(a_ref, rows):
    words = jnp.concatenate([a_ref[0, rows, :], a_ref[1, rows, :]], axis=-1)
    return pltpu.bitcast(words, BF16)


def _stage_store(a_ref, rows, val):
    words = pltpu.bitcast(val.astype(BF16), jnp.uint32)
    a_ref[0, rows, :] = words[:, :LANES]
    a_ref[1, rows, :] = words[:, LANES:]


N_BATCHES = N_SLOTS // SLOT_BATCH
SPEC_BLOCKS = 16


def _batch_slots(i):
    return [i * SLOT_BATCH + j for j in range(SLOT_BATCH)]


def _stage2_forward(a_ref, gf_ref, slots, sel0):
    xs = [_stage_load(a_ref, _slot_rows(s)) for s in slots]
    return [jnp.dot(gf_ref[sel0] if j == 0 else gf_ref[1], x, preferred_element_type=F32)
            for j, x in enumerate(xs)]


def _re_im_blocks(y):
    return [(y[16 * i:16 * i + 8], y[16 * i + 8:16 * i + 16]) for i in range(SPEC_BLOCKS)]


_GATE_GROUPS = (3, 5)
N_IN_STEPS = BATCH * (N_TILES // IN_TILES)
FILT_TILES = N_TILES // N_IN_STEPS


def _split_bf16(x):
    hi = x.astype(BF16)
    return hi, (x - hi.astype(F32)).astype(BF16)


def _dot_split(a, b):
    a_hi, a_lo = _split_bf16(a)
    b_hi, b_lo = _split_bf16(b)
    dot = lambda u, v: jnp.dot(u, v, preferred_element_type=F32)
    return dot(a_hi, b_hi) + dot(a_lo, b_hi) + dot(a_hi, b_lo)


def _first_step():
    return (pl.program_id(0) == 0) & (pl.program_id(1) == 0)


def _cast_weights_once(w_ref, w_scr, ncol):
    @pl.when(_first_step())
    def _():
        for c in range(w_ref.shape[1] // ncol):
            w_scr[:, c * ncol:(c + 1) * ncol] = w_ref[:, c * ncol:(c + 1) * ncol].astype(BF16)


class _FilterTaps:
    def __init__(self, z_ref, p_ref, wp_scr, dl_ref, first_tile, o_ref):
        self.p_ref, self.wp_scr, self.dl_ref, self.first_tile, self.o_ref = p_ref, wp_scr, dl_ref, first_tile, o_ref
        self.cols_blk = p_ref[3].T[:FILTER_HIDDEN]
        self.h = z_ref[...]
        self.n_stages = 5

    def stage(self, k):
        hid = FILTER_HIDDEN
        if k < 3:
            w_t = self.p_ref[k].T[:hid, :hid]
            freq = self.cols_blk[:, 3:4]
            self.h = jnp.sin(freq * (_dot_split(w_t, self.h) + self.cols_blk[:, k:k + 1]))
            return
        if k == 3:
            h_hi = self.h.astype(BF16).astype(F32)
            stack = jnp.concatenate([h_hi, self.h - h_hi, h_hi, jnp.zeros_like(h_hi)], axis=0)
            self.lhs = stack.T.astype(BF16)
            rows = FILT_TILES * TILE_ROWS
            r = lax.broadcasted_iota(jnp.int32, (rows, HYENA_WIDTH), 0)
            pos = 128 * (r % TILE_ROWS) + self.first_tile + r // TILE_ROWS
            t = pos.astype(F32) / float(SEQ - 1)
            self.decay = jnp.exp(-t * self.dl_ref[...])
        for od in ((0, 1) if k == 3 else (2, 3)):
            cols = slice(od * HYENA_WIDTH, (od + 1) * HYENA_WIDTH)
            taps = jnp.dot(self.lhs, self.wp_scr[:, cols], preferred_element_type=F32) * self.decay
            taps = taps.astype(BF16)
            for cb in range(N_CBLK):
                for i in range(FILT_TILES):
                    self.o_ref[od * N_CBLK + cb, i] = taps[i * TILE_ROWS:(i + 1) * TILE_ROWS,
                                                           cb * LANES:(cb + 1) * LANES]


def _in_proj_kernel(x_ref, g_ref, w32_ref, z_ref, p_ref, wp32_ref, dl_ref, o_ref, taps_ref, h_scr, w_ref, wp_scr):
    _cast_weights_once(w32_ref, w_ref, 512)

    @pl.when(_first_step())
    def _():
        wp_hi, wp_lo = _split_bf16(wp32_ref[...])
        wp_scr[...] = jnp.concatenate([wp_hi, wp_hi, wp_lo, jnp.zeros_like(wp_lo)], axis=0)

    step = pl.program_id(0) * (N_TILES // IN_TILES) + pl.program_id(1)
    filt = _FilterTaps(z_ref, p_ref, wp_scr, dl_ref, step * FILT_TILES, taps_ref)
    n_dots = 0

    ncol = 512
    for sb in range(IN_TILES // SUB_TILES):
        j0 = sb * SUB_TILES
        x = x_ref[:, j0:j0 + SUB_TILES, :].reshape(TILE_ROWS * SUB_TILES, D_MODEL)
        ms = jnp.mean(x * x, axis=-1, keepdims=True)
        hn = x * lax.rsqrt(ms + EPS) * g_ref[...]
        for k in range(D_MODEL // LANES):
            h_scr[sb, k] = hn[:, k * LANES:(k + 1) * LANES]
        h = jnp.concatenate(
            [jnp.concatenate([h_scr[sb, k, pl.ds(j, TILE_ROWS, stride=SUB_TILES), :]
                              for k in range(D_MODEL // LANES)], axis=-1).astype(BF16)
             for j in range(SUB_TILES)], axis=0)
        for c in range(PROJ_WIDTH // ncol):
            p = jnp.dot(h, w_ref[:, c * ncol:(c + 1) * ncol], preferred_element_type=F32)
            if (c * ncol) // HYENA_WIDTH in _GATE_GROUPS:
                p = _silu(p)
            p = p.astype(BF16)
            for cb in range(ncol // LANES):
                for j in range(SUB_TILES):
                    o_ref[c * (ncol // LANES) + cb, j0 + j] = p[j * TILE_ROWS:(j + 1) * TILE_ROWS,
                                                                cb * LANES:(cb + 1) * LANES]
            if n_dots % 2 == 0 and n_dots // 2 < filt.n_stages:
                filt.stage(n_dots // 2)
            n_dots += 1


def _in_proj(x4, pre_g, w_in, zfeat_t, w1, b1, w2, b2, w3, b3, freq, w_proj, abs_deltas):
    steps_per_batch = N_TILES // IN_TILES
    pad2 = lambda m: jnp.pad(m, ((0, LANES - m.shape[0]), (0, LANES - m.shape[1])))
    params = jnp.stack([pad2(w1), pad2(w2), pad2(w3), pad2(jnp.stack([b1, b2, b3, freq], axis=0))])
    full = lambda shape: pl.BlockSpec(shape, lambda b, g: (0,) * len(shape))
    once = lambda shape: pl.BlockSpec(shape, lambda b, g: (0,) * len(shape), pipeline_mode=pl.Buffered(1))
    return pl.pallas_call(
        _in_proj_kernel,
        grid=(BATCH, steps_per_batch),
        in_specs=[
            pl.BlockSpec((None, TILE_ROWS, IN_TILES, D_MODEL), lambda b, g: (b, 0, g, 0)),
            full((1, D_MODEL)),
            once((D_MODEL, PROJ_WIDTH)),
            pl.BlockSpec((FILTER_HIDDEN, FILT_TILES * TILE_ROWS), lambda b, g: (0, b * steps_per_batch + g)),
            full((4, LANES, LANES)),
            once((FILTER_HIDDEN, 4 * HYENA_WIDTH)),
            full((1, HYENA_WIDTH)),
        ],
        out_specs=[
            pl.BlockSpec((None, PROJ_BLOCKS, IN_TILES, TILE_ROWS, LANES), lambda b, g: (b, 0, g, 0, 0)),
            pl.BlockSpec((4 * N_CBLK, FILT_TILES, TILE_ROWS, LANES), lambda b, g: (0, b * steps_per_batch + g, 0, 0)),
        ],
        out_shape=[jax.ShapeDtypeStruct((BATCH, PROJ_BLOCKS, N_TILES, TILE_ROWS, LANES), BF16),
                   jax.ShapeDtypeStruct((4 * N_CBLK, N_TILES, TILE_ROWS, LANES), BF16)],
        scratch_shapes=[pltpu.VMEM((IN_TILES // SUB_TILES, D_MODEL // LANES, TILE_ROWS * SUB_TILES, LANES), F32),
                        pltpu.VMEM((D_MODEL, PROJ_WIDTH), BF16),
                        pltpu.VMEM((2 * LANES, 4 * HYENA_WIDTH), BF16)],
        compiler_params=pltpu.CompilerParams(
            dimension_semantics=("arbitrary", "arbitrary"), vmem_limit_bytes=VMEM_LIMIT),
        name="in_proj",
    )(x4, pre_g, w_in, zfeat_t, params, w_proj, abs_deltas)


def _filter_spec_kernel(fa_ref, fb_ref, ba_ref, bb_ref, d_ref, f1f_ref, gf_ref, k_ref, a_scr):
    def stage1(sb, carry):
        for i in range(HY_SUB):
            n2 = sb * HY_SUB + i
            nb = (N_TILES - n2) % N_TILES
            fwd = jnp.concatenate([fa_ref[n2], fb_ref[n2]], axis=-1)
            bwd = jnp.concatenate([ba_ref[nb], bb_ref[nb]], axis=-1)
            taps = jnp.concatenate([fwd, bwd], axis=0)
            _stage_store(a_scr, _tile_rows(n2), jnp.dot(f1f_ref[n2], taps, preferred_element_type=F32))
        return carry

    lax.fori_loop(0, N_SUB, stage1, 0)

    d = d_ref[pl.ds(pl.program_id(0), 1), :]

    def batch(slots, first):
        ys = _stage2_forward(a_scr, gf_ref, slots, 0 if first else 1)
        for j, (s, y) in enumerate(zip(slots, ys)):
            blocks = []
            for i, (yr, yi) in enumerate(_re_im_blocks(y)):
                yr = yr + d
                if first and j == 0 and i == 0:
                    rows = lax.broadcasted_iota(jnp.int32, yi.shape, 0)
                    yi = yi + jnp.where(rows == 0, d, 0.0)
                blocks += [yr, yi]
            spec = jnp.concatenate(blocks, axis=0).astype(BF16)
            k_ref[0, s] = spec[:, :LANES]
            k_ref[1, s] = spec[:, LANES:]

    batch(_batch_slots(0), True)

    def loop(i, carry):
        batch(_batch_slots(i), False)
        return carry

    lax.fori_loop(1, N_BATCHES, loop, 0, unroll=3)


def _filter_spec(taps, hyena_d, f1f, gf):
    grid = (2, N_CBLK // 2)
    const = lambda shape: pl.BlockSpec(shape, lambda o, c: (0,) * len(shape), pipeline_mode=pl.Buffered(1))

    def taps_spec(direction, k):
        return pl.BlockSpec((None, N_TILES, TILE_ROWS, LANES),
                            lambda o, c: ((2 * o + direction) * N_CBLK + 2 * c + k, 0, 0, 0))

    return pl.pallas_call(
        _filter_spec_kernel,
        grid=grid,
        in_specs=[taps_spec(0, 0), taps_spec(0, 1), taps_spec(1, 0), taps_spec(1, 1),
                  pl.BlockSpec((2, 2 * LANES), lambda o, c: (0, c)),
                  const((N_TILES, 128, 128)), const((2, 256, 256))],
        out_specs=pl.BlockSpec((None, 2, N_SLOTS, 256, LANES), lambda o, c: (o, c, 0, 0, 0)),
        out_shape=jax.ShapeDtypeStruct((2, N_CBLK, N_SLOTS, 256, LANES), BF16),
        scratch_shapes=[pltpu.VMEM((2, N_TILES * A_STRIDE, LANES), jnp.uint32)],
        compiler_params=pltpu.CompilerParams(
            dimension_semantics=("arbitrary", "arbitrary"), vmem_limit_bytes=VMEM_LIMIT),
        name="filter_spec",
    )(taps, taps, taps, taps, hyena_d, f1f, gf)


_T_S1 = 0
_T_F0 = 1
_T_M = 2
_T_F1 = 3
_T_E = 4
_T_END = 5
N_SUB = N_TILES // HY_SUB
MID_GROUP = 8

def _hyena_kernel(cin_ref, pz_ref, cw_ref, k_ref, f1_ref, f1i_ref, gf_ref, gb_ref, o_ref, a_scr, z_scr):
    t = pl.program_id(2)

    def stage1(n2, u_bf):
        _stage_store(a_scr, _tile_rows(n2), jnp.dot(f1_ref[n2], u_bf, preferred_element_type=F32))

    def inv_stage1(n2):
        return jnp.dot(f1i_ref[n2], _stage_load(a_scr, _tile_rows(n2)), preferred_element_type=F32)

    def short_conv(sb, row0):
        base = sb * HY_SUB
        first = _pair(cin_ref, (base + N_TILES - 1) % N_TILES)
        first = jnp.where(sb == 0, _shift_down(first), first)
        last = _pair(cin_ref, (base + HY_SUB) % N_TILES)
        last = jnp.where(sb == N_SUB - 1, _shift_up(last), last)
        tiles = [first] + [_pair(cin_ref, base + i) for i in range(HY_SUB)] + [last]
        w = cw_ref[row0:row0 + 4, :]
        w0, w1, w2, b = _dup(w[0:1]), _dup(w[1:2]), _dup(w[2:3]), _dup(w[3:4])
        return [tiles[i] * w0 + tiles[i + 1] * w1 + tiles[i + 2] * w2 + b for i in range(HY_SUB)]

    def filter_multiply(slots, ys, first, buf):
        for j, (s, y) in enumerate(zip(slots, ys)):
            blocks = []
            for i, (yr, yi) in enumerate(_re_im_blocks(y)):
                kblk = k_ref[s, 16 * i:16 * i + 16, :].astype(F32)
                kr, ki = kblk[:8], kblk[8:]
                if first and j == 0 and i == 0:
                    rows = lax.broadcasted_iota(jnp.int32, kr.shape, 0)
                    ka, kb, kd = kr, jnp.where(rows == 0, 0.0, ki), jnp.where(rows == 0, ki, kr)
                else:
                    ka, kb, kd = kr, ki, kr
                ka, kb, kd = _dup(ka), _dup(kb), _dup(kd)
                blocks += [yr * ka - yi * kb, yr * kb + yi * kd]
            z_scr[buf, j] = jnp.concatenate(blocks, axis=0).astype(BF16)

    def stage2_inverse(slots, first_sel, buf):
        bms = [jnp.dot(gb_ref[first_sel] if j == 0 else gb_ref[1], z_scr[buf, j], preferred_element_type=F32)
               for j in range(SLOT_BATCH)]
        for s, bm in zip(slots, bms):
            _stage_store(a_scr, _slot_rows(s), bm)

    def spectral_phase():
        slots0 = _batch_slots(0)
        filter_multiply(slots0, _stage2_forward(a_scr, gf_ref, slots0, 0), True, 0)

        def loop(i, carry):
            slots = _batch_slots(i)
            ys = _stage2_forward(a_scr, gf_ref, slots, 1)
            stage2_inverse(_batch_slots(i - 1), jnp.where(i == 1, 0, 1), (i - 1) % 2)
            filter_multiply(slots, ys, False, i % 2)
            return carry

        lax.fori_loop(1, N_BATCHES, loop, 0, unroll=5)
        stage2_inverse(_batch_slots(N_BATCHES - 1), 1, (N_BATCHES - 1) % 2)

    @pl.when(t == _T_S1)
    def _():
        def sub(sb, carry):
            vs = short_conv(sb, 0)
            for i in range(HY_SUB):
                stage1(sb * HY_SUB + i, vs[i].astype(BF16))
            return carry

        lax.fori_loop(0, N_SUB, sub, 0)

    @pl.when((t == _T_F0) | (t == _T_F1))
    def _():
        spectral_phase()

    @pl.when(t == _T_M)
    def _():
        def sub(sb, carry):
            gates = short_conv(sb, 4)
            for q in range(0, HY_SUB, MID_GROUP):
                ys = [inv_stage1(sb * HY_SUB + q + i) for i in range(MID_GROUP)]
                us = [(gates[q + i] * ys[i]).astype(BF16) for i in range(MID_GROUP)]
                for i in range(MID_GROUP):
                    stage1(sb * HY_SUB + q + i, us[i])
            return carry

        lax.fori_loop(0, N_SUB, sub, 0)

    @pl.when(t == _T_E)
    def _():
        def sub(sb, carry):
            gates = short_conv(sb, 8)
            for i in range(HY_SUB):
                n2 = sb * HY_SUB + i
                res = (gates[i] * inv_stage1(n2) * _pair(pz_ref, n2)).astype(BF16)
                o_ref[0, n2] = res[:, :LANES]
                o_ref[1, n2] = res[:, LANES:]
            return carry

        lax.fori_loop(0, N_SUB, sub, 0)


def _hyena(p5, conv_w, conv_b, kspec, f1, f1i, gf, gb):
    grid = (N_CBLK, BATCH // 2, _T_END)
    cw = jnp.concatenate(
        [jnp.concatenate([conv_w[:, k * HYENA_WIDTH:(k + 1) * HYENA_WIDTH],
                          conv_b[:, k * HYENA_WIDTH:(k + 1) * HYENA_WIDTH]], axis=0) for k in range(3)], axis=0)
    seq_block = (2, None, N_TILES, TILE_ROWS, LANES)

    def conv_in_map(c, b, t):
        return (b, jnp.where(t < _T_M, 0, jnp.where(t < _T_E, 1, 2)) * N_CBLK + c, 0, 0, 0)

    def z_map(c, b, t):
        flat = c * (BATCH // 2) + b
        sel = jnp.where(t >= _T_M, flat, jnp.maximum(flat - 1, 0))
        return (sel % (BATCH // 2), 3 * N_CBLK + sel // (BATCH // 2), 0, 0, 0)

    const = lambda shape: pl.BlockSpec(shape, lambda c, b, t: (0,) * len(shape), pipeline_mode=pl.Buffered(1))
    in_specs = [
        pl.BlockSpec(seq_block, conv_in_map),
        pl.BlockSpec(seq_block, z_map),
        pl.BlockSpec((12, LANES), lambda c, b, t: (0, c)),
        pl.BlockSpec((None, None, N_SLOTS, 256, LANES), lambda c, b, t: (jnp.where(t >= _T_M, 1, 0), c, 0, 0, 0)),
        const((N_TILES, 128, TILE_ROWS)), const((N_TILES, TILE_ROWS, 128)),
        const((2, 256, 256)), const((2, 256, 256)),
    ]
    return pl.pallas_call(
        _hyena_kernel,
        grid=grid,
        in_specs=in_specs,
        out_specs=pl.BlockSpec(seq_block, lambda c, b, t: (b, c, 0, 0, 0)),
        out_shape=jax.ShapeDtypeStruct((BATCH, N_CBLK, N_TILES, TILE_ROWS, LANES), BF16),
        scratch_shapes=[pltpu.VMEM((2, N_TILES * A_STRIDE, LANES), jnp.uint32),
                        pltpu.VMEM((2, SLOT_BATCH, 256, 2 * LANES), BF16)],
        compiler_params=pltpu.CompilerParams(
            dimension_semantics=("arbitrary", "arbitrary", "arbitrary"), vmem_limit_bytes=VMEM_LIMIT),
        name="hyena",
    )(p5, p5, cw, kspec, f1, f1i, gf, gb)


POOL_HALO = 8


def _out_kernel(yh_ref, up_ref, upp_ref, upn_ref, zp_ref, x_ref, pw_ref, ps_ref, gh_ref, gp_ref, w32_ref, gpost_ref,
                o_ref, r_scr, yp_scr, w_ref):
    _cast_weights_once(w32_ref, w_ref, 512)
    g = pl.program_id(1)
    last_step = N_TILES // OUT_TILES - 1
    row = lax.broadcasted_iota(jnp.int32, (TILE_ROWS, LANES), 0)

    def pool_tile(ci, l):
        if l < 0:
            t = upp_ref[ci, POOL_HALO + l].astype(F32)
            return jnp.where(g == 0, _shift_down(t), t)
        if l >= OUT_TILES:
            t = upn_ref[ci, l - OUT_TILES].astype(F32)
            return jnp.where(g == last_step, _shift_up(t), t)
        return up_ref[ci, l].astype(F32)

    for ci, w in enumerate(POOL_WINDOWS):
        lo, hi = w // 2, w - 1 - w // 2
        window_sum = pool_tile(ci, -lo)
        for d in range(-lo + 1, hi + 1):
            window_sum = window_sum + pool_tile(ci, d)
        for sb in range(OUT_TILES // SUB_TILES):
            pooled = []
            for i in range(SUB_TILES):
                l = sb * SUB_TILES + i
                if l > 0:
                    window_sum = window_sum + pool_tile(ci, l + hi) - pool_tile(ci, l - 1 - lo)
                inv_cnt = 1.0 / w
                if l < lo:
                    inv_cnt = jnp.where((g == 0) & (row == 0), 1.0 / (hi + l + 1), inv_cnt)
                elif l > OUT_TILES - 1 - hi:
                    inv_cnt = jnp.where((g == last_step) & (row == TILE_ROWS - 1),
                                        1.0 / (lo + OUT_TILES - l), inv_cnt)
                pooled.append((window_sum * inv_cnt - pool_tile(ci, l)).astype(BF16))
            y = jnp.dot(jnp.concatenate(pooled, axis=0), pw_ref[ci].astype(BF16), preferred_element_type=F32)
            y = y * ps_ref[:, ci * LANES:(ci + 1) * LANES]
            gate = jnp.concatenate([zp_ref[ci, sb * SUB_TILES + i] for i in range(SUB_TILES)], axis=0).astype(F32)
            yp_scr[sb, ci] = y * gate

    def rms(y, gain_ref):
        ms = jnp.mean(y * y, axis=-1, keepdims=True)
        return (y * lax.rsqrt(ms + EPS) * gain_ref[...]).astype(BF16)

    for sb in range(OUT_TILES // SUB_TILES):
        j0 = sb * SUB_TILES
        yh = jnp.concatenate(
            [jnp.concatenate([yh_ref[cb, j] for cb in range(N_CBLK)], axis=-1) for j in range(j0, j0 + SUB_TILES)],
            axis=0).astype(F32)
        yp = jnp.concatenate([yp_scr[sb, ci] for ci in range(N_CBLK)], axis=-1)
        yc = jnp.concatenate([rms(yh, gh_ref), rms(yp, gp_ref)], axis=-1)
        out = jnp.dot(yc, w_ref[...], preferred_element_type=F32)
        ms = jnp.mean(out * out, axis=-1, keepdims=True)
        out = out * lax.rsqrt(ms + EPS) * gpost_ref[...]
        for j in range(SUB_TILES):
            for k in range(D_MODEL // LANES):
                r_scr[sb, k, pl.ds(j, TILE_ROWS, stride=SUB_TILES), :] = out[j * TILE_ROWS:(j + 1) * TILE_ROWS,
                                                                             k * LANES:(k + 1) * LANES]
        r = jnp.concatenate([r_scr[sb, k] for k in range(D_MODEL // LANES)], axis=-1)
        o_ref[:, j0:j0 + SUB_TILES, :] = (x_ref[:, j0:j0 + SUB_TILES, :]
                                          + r.reshape(TILE_ROWS, SUB_TILES, D_MODEL))


def _out_proj(yh, p5, x4, pool_w, pool_scale, norm_h_g, norm_p_g, w_out, post_g):
    n_steps = N_TILES // OUT_TILES
    halo_blocks = N_TILES // POOL_HALO
    per_step = OUT_TILES // POOL_HALO
    u_group, z_group = 4, 5
    y_spec = pl.BlockSpec((None, N_CBLK, OUT_TILES, TILE_ROWS, LANES), lambda b, g: (b, 0, g, 0, 0))
    x_spec = pl.BlockSpec((None, TILE_ROWS, OUT_TILES, D_MODEL), lambda b, g: (b, 0, g, 0))
    full = lambda shape: pl.BlockSpec(shape, lambda b, g: (0,) * len(shape))
    return pl.pallas_call(
        _out_kernel,
        grid=(BATCH, n_steps),
        in_specs=[
            y_spec,
            pl.BlockSpec((None, N_CBLK, OUT_TILES, TILE_ROWS, LANES), lambda b, g: (b, u_group, g, 0, 0)),
            pl.BlockSpec((None, N_CBLK, POOL_HALO, TILE_ROWS, LANES),
                         lambda b, g: (b, u_group, (g * per_step + halo_blocks - 1) % halo_blocks, 0, 0)),
            pl.BlockSpec((None, N_CBLK, POOL_HALO, TILE_ROWS, LANES),
                         lambda b, g: (b, u_group, (g * per_step + per_step) % halo_blocks, 0, 0)),
            pl.BlockSpec((None, N_CBLK, OUT_TILES, TILE_ROWS, LANES), lambda b, g: (b, z_group, g, 0, 0)),
            x_spec,
            full((N_CBLK, LANES, LANES)), full((1, POOL_WIDTH)),
            full((1, HYENA_WIDTH)), full((1, POOL_WIDTH)),
            pl.BlockSpec((D_MODEL, D_MODEL), lambda b, g: (0, 0), pipeline_mode=pl.Buffered(1)), full((1, D_MODEL)),
        ],
        out_specs=x_spec,
        out_shape=jax.ShapeDtypeStruct((BATCH, TILE_ROWS, N_TILES, D_MODEL), F32),
        scratch_shapes=[pltpu.VMEM((OUT_TILES // SUB_TILES, D_MODEL // LANES, TILE_ROWS * SUB_TILES, LANES), F32),
                        pltpu.VMEM((OUT_TILES // SUB_TILES, N_CBLK, TILE_ROWS * SUB_TILES, LANES), F32),
                        pltpu.VMEM((D_MODEL, D_MODEL), BF16)],
        compiler_params=pltpu.CompilerParams(
            dimension_semantics=("arbitrary", "arbitrary"), vmem_limit_bytes=VMEM_LIMIT),
        name="out_proj",
    )(yh, p5, p5, p5, p5, x4, pool_w, pool_scale, norm_h_g, norm_p_g, w_out, post_g)


def kernel(x, pre_norm_g, w_in, conv_w, conv_b, filt_w1, filt_b1, filt_w2, filt_b2, filt_w3, filt_b3,
           filt_freq, filt_w_out, hyena_d, pool_w, pool_scale, norm_h_g, norm_p_g, w_out, post_norm_g):
    assert x.shape == (BATCH, SEQ, D_MODEL) and pre_norm_g.shape[0] == 1
    f1, f1i, gf, gb, f1f = (jnp.asarray(m, F32).astype(BF16) for m in (_F1, _F1I, _GF, _GB, _F1F))

    x4 = x.reshape(BATCH, TILE_ROWS, N_TILES, D_MODEL)
    p5, taps = _in_proj(x4, pre_norm_g, w_in[0], jnp.asarray(_ZFEAT), filt_w1[0], filt_b1[0], filt_w2[0], filt_b2[0],
                        filt_w3[0], filt_b3[0], filt_freq[0], filt_w_out[0], jnp.asarray(_ABS_DELTAS))
    kspec = _filter_spec(taps, hyena_d[0], f1f, gf)

    yh = _hyena(p5, conv_w[0], conv_b, kspec, f1, f1i, gf, gb)
    out4 = _out_proj(yh, p5, x4, pool_w[0], pool_scale, norm_h_g, norm_p_g,
                     w_out[0], post_norm_g)
    return out4.reshape(BATCH, SEQ, D_MODEL)
```

```python
import functools
import math

import numpy as np
import jax
import jax.numpy as jnp
from jax import lax
from jax.experimental import pallas as pl
from jax.experimental.pallas import tpu as pltpu

F32 = jnp.float32
BF16 = jnp.bfloat16

D_MODEL = 1024
BATCH = 4
SEQ = 8192
HYENA_WIDTH = 512
POOL_WIDTH = 512
POOL_WINDOWS = (2, 4, 8, 16)
FILTER_EMB = 33
FILTER_BANDS = 16
FILTER_HIDDEN = 64
PROJ_WIDTH = 3072
EPS = 1e-6

LANES = 128
N_FFT = 2 * SEQ
N_TILES = 128
TILE_ROWS = SEQ // N_TILES
N_SLOTS = 64
HY_SUB = 32
A_STRIDE = 72
SLOT_BATCH = 4
N_CBLK = HYENA_WIDTH // LANES
PROJ_BLOCKS = PROJ_WIDTH // LANES
IN_TILES = 16
OUT_TILES = 16
SUB_TILES = 8
VMEM_LIMIT = 60 * 1024 * 1024


def _dft_tables():
    n1 = np.arange(TILE_ROWS)
    n2 = np.arange(N_TILES)
    s = np.arange(N_SLOTS)
    ph = 2 * np.pi * (n2[:, None, None] * s[None, :, None] / N_FFT
                      + n1[None, None, :] * s[None, :, None] / 128.0)
    f1 = np.zeros((N_TILES, 128, TILE_ROWS))
    f1[:, :64, :] = np.cos(ph)
    f1[:, 64:, :] = -np.sin(ph)
    f1[:, 0, :] = 1.0
    f1[:, 64, :] = (-1.0) ** n1
    php = np.transpose(ph, (0, 2, 1))
    f1i = np.zeros((N_TILES, TILE_ROWS, 128))
    f1i[:, :, :64] = 2 * np.cos(php) / N_FFT
    f1i[:, :, 64:] = -2 * np.sin(php) / N_FFT
    f1i[:, :, 0] = 1.0 / N_FFT
    f1i[:, :, 64] = ((-1.0) ** n1)[None, :] / N_FFT
    k2 = np.arange(128)
    th = 2 * np.pi * np.outer(k2, n2) / 128.0
    c, sn = np.cos(th), np.sin(th)
    g = np.block([[c, sn], [-sn, c]])
    gi = np.block([[c, -sn], [sn, c]])
    kk = np.arange(64)
    tha = 2 * np.pi * np.outer(kk, n2) / 128.0
    thb = 2 * np.pi * np.outer(64 + 128 * kk, n2) / N_FFT
    g0 = np.zeros((256, 256))
    g0[0:64, 0:128] = np.cos(tha)
    g0[64:128, 128:256] = np.cos(thb)
    g0[128:192, 0:128] = -np.sin(tha)
    g0[128, 0:128] = (-1.0) ** n2
    g0[192:256, 128:256] = -np.sin(thb)
    g0i = np.zeros((256, 256))
    g0i[0:128, 0:64] = 2 * np.cos(tha.T)
    g0i[0:128, 0] = 1.0
    g0i[0:128, 128:192] = -2 * np.sin(tha.T)
    g0i[0:128, 128] = (-1.0) ** n2
    g0i[128:256, 64:128] = 2 * np.cos(thb.T)
    g0i[128:256, 192:256] = -2 * np.sin(thb.T)
    q = np.arange(256)
    perm = np.where(q % 16 < 8, 8 * (q // 16) + q % 16, 128 + 8 * (q // 16) + q % 16 - 8)
    gf = np.stack([g0[perm, :], g[perm, :]])
    gb = np.stack([g0i[:, perm], gi[:, perm]])
    n1f = np.arange(128)
    phf = 2 * np.pi * (n2[:, None, None] * s[None, :, None] / N_FFT
                       + n1f[None, None, :] * s[None, :, None] / 128.0)
    full = np.zeros((N_TILES, 128, 128))
    full[:, :64, :] = np.cos(phf)
    full[:, 64:, :] = -np.sin(phf)
    full[:, 0, :] = 1.0
    full[:, 64, :] = (-1.0) ** n1f
    rev = full[:, :, 127:63:-1].copy()
    rev[0, :, 1:] = full[0, :, 127:64:-1]
    rev[0, :, 0] = 0.0
    f1f = np.concatenate([full[:, :, :64], rev], axis=2)
    pair64 = np.arange(128) % 2 * 64 + np.arange(128) // 2
    pair128 = np.arange(256) % 2 * 128 + np.arange(256) // 2
    f1, f1f = f1[:, pair64, :], f1f[:, pair64, :]
    f1i = f1i[:, :, pair64]
    gf = gf[:, :, pair128]
    gb = gb[:, pair128, :]
    return f1, f1i, gf, gb, f1f


def _filter_features():
    pos = np.arange(SEQ, dtype=np.float64)
    t = pos / (SEQ - 1)
    ang = 2.0 * math.pi * pos / SEQ
    bands = np.linspace(1e-4, FILTER_BANDS - 1, FILTER_BANDS)
    z = np.concatenate([t[:, None], np.cos(bands[None, :] * ang[:, None]),
                        -np.sin(bands[None, :] * ang[:, None])], axis=-1)
    z = z.reshape(TILE_ROWS, N_TILES, FILTER_EMB).transpose(1, 0, 2).reshape(SEQ, FILTER_EMB)
    zp = np.zeros((FILTER_HIDDEN, SEQ))
    zp[:FILTER_EMB, :] = z.T
    max_decay = math.log(1e-2) / 0.3
    min_decay = math.log(1e-2) / 1.5
    deltas = np.abs(np.linspace(min_decay, max_decay, HYENA_WIDTH))
    return zp.astype(np.float32), deltas.astype(np.float32)[None, :]


_F1, _F1I, _GF, _GB, _F1F = _dft_tables()
_ZFEAT, _ABS_DELTAS = _filter_features()


def _shift_down(x):
    rows = lax.broadcasted_iota(jnp.int32, x.shape, 0)
    return jnp.where(rows == 0, 0.0, pltpu.roll(x, 1, axis=0))


def _shift_up(x):
    rows = lax.broadcasted_iota(jnp.int32, x.shape, 0)
    return jnp.where(rows == x.shape[0] - 1, 0.0, pltpu.roll(x, x.shape[0] - 1, axis=0))


def _pair(ref, i):
    return jnp.concatenate([ref[0, i], ref[1, i]], axis=-1).astype(F32)


def _dup(x):
    return jnp.concatenate([x, x], axis=-1)


def _silu(z):
    hz = 0.5 * z
    return hz * (1.0 + jnp.tanh(hz))


def _tile_rows(n2):
    return pl.ds(pl.multiple_of(n2 * A_STRIDE, 8), N_SLOTS)


def _slot_rows(s):
    return pl.ds(s, N_TILES, stride=A_STRIDE)


def _stage_load(a_ref, rows):
    words = jnp.concatenate([a_ref[0, rows, :], a_ref[1, rows, :]], axis=-1)
    return pltpu.bitcast(words, BF16)


def _stage_store(a_ref, rows, val):
    words = pltpu.bitcast(val.astype(BF16), jnp.uint32)
    a_ref[0, rows, :] = words[:, :LANES]
    a_ref[1, rows, :] = words[:, LANES:]


N_BATCHES = N_SLOTS // SLOT_BATCH
SPEC_BLOCKS = 16


def _batch_slots(i):
    return [i * SLOT_BATCH + j for j in range(SLOT_BATCH)]


def _stage2_forward(a_ref, gf_ref, slots, sel0):
    xs = [_stage_load(a_ref, _slot_rows(s)) for s in slots]
    return [jnp.dot(gf_ref[sel0] if j == 0 else gf_ref[1], x, preferred_element_type=F32)
            for j, x in enumerate(xs)]


def _re_im_blocks(y):
    return [(y[16 * i:16 * i + 8], y[16 * i + 8:16 * i + 16]) for i in range(SPEC_BLOCKS)]


_GATE_GROUPS = (3, 5)
N_IN_STEPS = BATCH * (N_TILES // IN_TILES)
FILT_TILES = N_TILES // N_IN_STEPS


def _split_bf16(x):
    hi = x.astype(BF16)
    return hi, (x - hi.astype(F32)).astype(BF16)


def _dot_split(a, b):
    a_hi, a_lo = _split_bf16(a)
    b_hi, b_lo = _split_bf16(b)
    dot = lambda u, v: jnp.dot(u, v, preferred_element_type=F32)
    return dot(a_hi, b_hi) + dot(a_lo, b_hi) + dot(a_hi, b_lo)


def _first_step():
    return (pl.program_id(0) == 0) & (pl.program_id(1) == 0)


def _cast_weights_once(w_ref, w_scr, ncol):
    @pl.when(_first_step())
    def _():
        for c in range(w_ref.shape[1] // ncol):
            w_scr[:, c * ncol:(c + 1) * ncol] = w_ref[:, c * ncol:(c + 1) * ncol].astype(BF16)


class _FilterTaps:
    def __init__(self, z_ref, p_ref, wp_scr, dl_ref, first_tile, o_ref):
        self.p_ref, self.wp_scr, self.dl_ref, self.first_tile, self.o_ref = p_ref, wp_scr, dl_ref, first_tile, o_ref
        self.cols_blk = p_ref[3].T[:FILTER_HIDDEN]
        self.h = z_ref[...]
        self.n_stages = 5

    def stage(self, k):
        hid = FILTER_HIDDEN
        if k < 3:
            w_t = self.p_ref[k].T[:hid, :hid]
            freq = self.cols_blk[:, 3:4]
            self.h = jnp.sin(freq * (_dot_split(w_t, self.h) + self.cols_blk[:, k:k + 1]))
            return
        if k == 3:
            h_hi = self.h.astype(BF16).astype(F32)
            stack = jnp.concatenate([h_hi, self.h - h_hi, h_hi, jnp.zeros_like(h_hi)], axis=0)
            self.lhs = stack.T.astype(BF16)
            rows = FILT_TILES * TILE_ROWS
            r = lax.broadcasted_iota(jnp.int32, (rows, HYENA_WIDTH), 0)
            pos = 128 * (r % TILE_ROWS) + self.first_tile + r // TILE_ROWS
            t = pos.astype(F32) / float(SEQ - 1)
            self.decay = jnp.exp(-t * self.dl_ref[...])
        for od in ((0, 1) if k == 3 else (2, 3)):
            cols = slice(od * HYENA_WIDTH, (od + 1) * HYENA_WIDTH)
            taps = jnp.dot(self.lhs, self.wp_scr[:, cols], preferred_element_type=F32) * self.decay
            taps = taps.astype(BF16)
            for cb in range(N_CBLK):
                for i in range(FILT_TILES):
                    self.o_ref[od * N_CBLK + cb, i] = taps[i * TILE_ROWS:(i + 1) * TILE_ROWS,
                                                           cb * LANES:(cb + 1) * LANES]


def _in_proj_kernel(x_ref, g_ref, w32_ref, z_ref, p_ref, wp32_ref, dl_ref, o_ref, taps_ref, h_scr, w_ref, wp_scr):
    _cast_weights_once(w32_ref, w_ref, 512)

    @pl.when(_first_step())
    def _():
        wp_hi, wp_lo = _split_bf16(wp32_ref[...])
        wp_scr[...] = jnp.concatenate([wp_hi, wp_hi, wp_lo, jnp.zeros_like(wp_lo)], axis=0)

    step = pl.program_id(0) * (N_TILES // IN_TILES) + pl.program_id(1)
    filt = _FilterTaps(z_ref, p_ref, wp_scr, dl_ref, step * FILT_TILES, taps_ref)
    n_dots = 0

    ncol = 512
    for sb in range(IN_TILES // SUB_TILES):
        j0 = sb * SUB_TILES
        x = x_ref[:, j0:j0 + SUB_TILES, :].reshape(TILE_ROWS * SUB_TILES, D_MODEL)
        ms = jnp.mean(x * x, axis=-1, keepdims=True)
        hn = x * lax.rsqrt(ms + EPS) * g_ref[...]
        for k in range(D_MODEL // LANES):
            h_scr[sb, k] = hn[:, k * LANES:(k + 1) * LANES]
        h = jnp.concatenate(
            [jnp.concatenate([h_scr[sb, k, pl.ds(j, TILE_ROWS, stride=SUB_TILES), :]
                              for k in range(D_MODEL // LANES)], axis=-1).astype(BF16)
             for j in range(SUB_TILES)], axis=0)
        for c in range(PROJ_WIDTH // ncol):
            p = jnp.dot(h, w_ref[:, c * ncol:(c + 1) * ncol], preferred_element_type=F32)
            if (c * ncol) // HYENA_WIDTH in _GATE_GROUPS:
                p = _silu(p)
            p = p.astype(BF16)
            for cb in range(ncol // LANES):
                for j in range(SUB_TILES):
                    o_ref[c * (ncol // LANES) + cb, j0 + j] = p[j * TILE_ROWS:(j + 1) * TILE_ROWS,
                                                                cb * LANES:(cb + 1) * LANES]
            if n_dots % 2 == 0 and n_dots // 2 < filt.n_stages:
                filt.stage(n_dots // 2)
            n_dots += 1


def _in_proj(x4, pre_g, w_in, zfeat_t, w1, b1, w2, b2, w3, b3, freq, w_proj, abs_deltas):
    steps_per_batch = N_TILES // IN_TILES
    pad2 = lambda m: jnp.pad(m, ((0, LANES - m.shape[0]), (0, LANES - m.shape[1])))
    params = jnp.stack([pad2(w1), pad2(w2), pad2(w3), pad2(jnp.stack([b1, b2, b3, freq], axis=0))])
    full = lambda shape: pl.BlockSpec(shape, lambda b, g: (0,) * len(shape))
    once = lambda shape: pl.BlockSpec(shape, lambda b, g: (0,) * len(shape), pipeline_mode=pl.Buffered(1))
    return pl.pallas_call(
        _in_proj_kernel,
        grid=(BATCH, steps_per_batch),
        in_specs=[
            pl.BlockSpec((None, TILE_ROWS, IN_TILES, D_MODEL), lambda b, g: (b, 0, g, 0)),
            full((1, D_MODEL)),
            once((D_MODEL, PROJ_WIDTH)),
            pl.BlockSpec((FILTER_HIDDEN, FILT_TILES * TILE_ROWS), lambda b, g: (0, b * steps_per_batch + g)),
            full((4, LANES, LANES)),
            once((FILTER_HIDDEN, 4 * HYENA_WIDTH)),
            full((1, HYENA_WIDTH)),
        ],
        out_specs=[
            pl.BlockSpec((None, PROJ_BLOCKS, IN_TILES, TILE_ROWS, LANES), lambda b, g: (b, 0, g, 0, 0)),
            pl.BlockSpec((4 * N_CBLK, FILT_TILES, TILE_ROWS, LANES), lambda b, g: (0, b * steps_per_batch + g, 0, 0)),
        ],
        out_shape=[jax.ShapeDtypeStruct((BATCH, PROJ_BLOCKS, N_TILES, TILE_ROWS, LANES), BF16),
                   jax.ShapeDtypeStruct((4 * N_CBLK, N_TILES, TILE_ROWS, LANES), BF16)],
        scratch_shapes=[pltpu.VMEM((IN_TILES // SUB_TILES, D_MODEL // LANES, TILE_ROWS * SUB_TILES, LANES), F32),
                        pltpu.VMEM((D_MODEL, PROJ_WIDTH), BF16),
                        pltpu.VMEM((2 * LANES, 4 * HYENA_WIDTH), BF16)],
        compiler_params=pltpu.CompilerParams(
            dimension_semantics=("arbitrary", "arbitrary"), vmem_limit_bytes=VMEM_LIMIT),
        name="in_proj",
    )(x4, pre_g, w_in, zfeat_t, params, w_proj, abs_deltas)


def _filter_spec_kernel(fa_ref, fb_ref, ba_ref, bb_ref, d_ref, f1f_ref, gf_ref, k_ref, a_scr):
    def stage1(sb, carry):
        for i in range(HY_SUB):
            n2 = sb * HY_SUB + i
            nb = (N_TILES - n2) % N_TILES
            fwd = jnp.concatenate([fa_ref[n2], fb_ref[n2]], axis=-1)
            bwd = jnp.concatenate([ba_ref[nb], bb_ref[nb]], axis=-1)
            taps = jnp.concatenate([fwd, bwd], axis=0)
            _stage_store(a_scr, _tile_rows(n2), jnp.dot(f1f_ref[n2], taps, preferred_element_type=F32))
        return carry

    lax.fori_loop(0, N_SUB, stage1, 0)

    d = d_ref[pl.ds(pl.program_id(0), 1), :]

    def batch(slots, first):
        ys = _stage2_forward(a_scr, gf_ref, slots, 0 if first else 1)
        for j, (s, y) in enumerate(zip(slots, ys)):
            blocks = []
            for i, (yr, yi) in enumerate(_re_im_blocks(y)):
                yr = yr + d
                if first and j == 0 and i == 0:
                    rows = lax.broadcasted_iota(jnp.int32, yi.shape, 0)
                    yi = yi + jnp.where(rows == 0, d, 0.0)
                blocks += [yr, yi]
            spec = jnp.concatenate(blocks, axis=0).astype(BF16)
            k_ref[0, s] = spec[:, :LANES]
            k_ref[1, s] = spec[:, LANES:]

    batch(_batch_slots(0), True)

    def loop(i, carry):
        batch(_batch_slots(i), False)
        return carry

    lax.fori_loop(1, N_BATCHES, loop, 0, unroll=3)


def _filter_spec(taps, hyena_d, f1f, gf):
    grid = (2, N_CBLK // 2)
    const = lambda shape: pl.BlockSpec(shape, lambda o, c: (0,) * len(shape), pipeline_mode=pl.Buffered(1))

    def taps_spec(direction, k):
        return pl.BlockSpec((None, N_TILES, TILE_ROWS, LANES),
                            lambda o, c: ((2 * o + direction) * N_CBLK + 2 * c + k, 0, 0, 0))

    return pl.pallas_call(
        _filter_spec_kernel,
        grid=grid,
        in_specs=[taps_spec(0, 0), taps_spec(0, 1), taps_spec(1, 0), taps_spec(1, 1),
                  pl.BlockSpec((2, 2 * LANES), lambda o, c: (0, c)),
                  const((N_TILES, 128, 128)), const((2, 256, 256))],
        out_specs=pl.BlockSpec((None, 2, N_SLOTS, 256, LANES), lambda o, c: (o, c, 0, 0, 0)),
        out_shape=jax.ShapeDtypeStruct((2, N_CBLK, N_SLOTS, 256, LANES), BF16),
        scratch_shapes=[pltpu.VMEM((2, N_TILES * A_STRIDE, LANES), jnp.uint32)],
        compiler_params=pltpu.CompilerParams(
            dimension_semantics=("arbitrary", "arbitrary"), vmem_limit_bytes=VMEM_LIMIT),
        name="filter_spec",
    )(taps, taps, taps, taps, hyena_d, f1f, gf)


_T_S1 = 0
_T_F0 = 1
_T_M = 2
_T_F1 = 3
_T_E = 4
_T_END = 5
N_SUB = N_TILES // HY_SUB
MID_GROUP = 8

def _hyena_kernel(cin_ref, pz_ref, cw_ref, k_ref, f1_ref, f1i_ref, gf_ref, gb_ref, o_ref, a_scr, z_scr):
    t = pl.program_id(2)

    def stage1(n2, u_bf):
        _stage_store(a_scr, _tile_rows(n2), jnp.dot(f1_ref[n2], u_bf, preferred_element_type=F32))

    def inv_stage1(n2):
        return jnp.dot(f1i_ref[n2], _stage_load(a_scr, _tile_rows(n2)), preferred_element_type=F32)

    def short_conv(sb, row0):
        base = sb * HY_SUB
        first = _pair(cin_ref, (base + N_TILES - 1) % N_TILES)
        first = jnp.where(sb == 0, _shift_down(first), first)
        last = _pair(cin_ref, (base + HY_SUB) % N_TILES)
        last = jnp.where(sb == N_SUB - 1, _shift_up(last), last)
        tiles = [first] + [_pair(cin_ref, base + i) for i in range(HY_SUB)] + [last]
        w = cw_ref[row0:row0 + 4, :]
        w0, w1, w2, b = _dup(w[0:1]), _dup(w[1:2]), _dup(w[2:3]), _dup(w[3:4])
        return [tiles[i] * w0 + tiles[i + 1] * w1 + tiles[i + 2] * w2 + b for i in range(HY_SUB)]

    def filter_multiply(slots, ys, first, buf):
        for j, (s, y) in enumerate(zip(slots, ys)):
            blocks = []
            for i, (yr, yi) in enumerate(_re_im_blocks(y)):
                kblk = k_ref[s, 16 * i:16 * i + 16, :].astype(F32)
                kr, ki = kblk[:8], kblk[8:]
                if first and j == 0 and i == 0:
                    rows = lax.broadcasted_iota(jnp.int32, kr.shape, 0)
                    ka, kb, kd = kr, jnp.where(rows == 0, 0.0, ki), jnp.where(rows == 0, ki, kr)
                else:
                    ka, kb, kd = kr, ki, kr
                ka, kb, kd = _dup(ka), _dup(kb), _dup(kd)
                blocks += [yr * ka - yi * kb, yr * kb + yi * kd]
            z_scr[buf, j] = jnp.concatenate(blocks, axis=0).astype(BF16)

    def stage2_inverse(slots, first_sel, buf):
        bms = [jnp.dot(gb_ref[first_sel] if j == 0 else gb_ref[1], z_scr[buf, j], preferred_element_type=F32)
               for j in range(SLOT_BATCH)]
        for s, bm in zip(slots, bms):
            _stage_store(a_scr, _slot_rows(s), bm)

    def spectral_phase():
        slots0 = _batch_slots(0)
        filter_multiply(slots0, _stage2_forward(a_scr, gf_ref, slots0, 0), True, 0)

        def loop(i, carry):
            slots = _batch_slots(i)
            ys = _stage2_forward(a_scr, gf_ref, slots, 1)
            stage2_inverse(_batch_slots(i - 1), jnp.where(i == 1, 0, 1), (i - 1) % 2)
            filter_multiply(slots, ys, False, i % 2)
            return carry

        lax.fori_loop(1, N_BATCHES, loop, 0, unroll=5)
        stage2_inverse(_batch_slots(N_BATCHES - 1), 1, (N_BATCHES - 1) % 2)

    @pl.when(t == _T_S1)
    def _():
        def sub(sb, carry):
            vs = short_conv(sb, 0)
            for i in range(HY_SUB):
                stage1(sb * HY_SUB + i, vs[i].astype(BF16))
            return carry

        lax.fori_loop(0, N_SUB, sub, 0)

    @pl.when((t == _T_F0) | (t == _T_F1))
    def _():
        spectral_phase()

    @pl.when(t == _T_M)
    def _():
        def sub(sb, carry):
            gates = short_conv(sb, 4)
            for q in range(0, HY_SUB, MID_GROUP):
                ys = [inv_stage1(sb * HY_SUB + q + i) for i in range(MID_GROUP)]
                us = [(gates[q + i] * ys[i]).astype(BF16) for i in range(MID_GROUP)]
                for i in range(MID_GROUP):
                    stage1(sb * HY_SUB + q + i, us[i])
            return carry

        lax.fori_loop(0, N_SUB, sub, 0)

    @pl.when(t == _T_E)
    def _():
        def sub(sb, carry):
            gates = short_conv(sb, 8)
            for i in range(HY_SUB):
                n2 = sb * HY_SUB + i
                res = (gates[i] * inv_stage1(n2) * _pair(pz_ref, n2)).astype(BF16)
                o_ref[0, n2] = res[:, :LANES]
                o_ref[1, n2] = res[:, LANES:]
            return carry

        lax.fori_loop(0, N_SUB, sub, 0)


def _hyena(p5, conv_w, conv_b, kspec, f1, f1i, gf, gb):
    grid = (N_CBLK, BATCH // 2, _T_END)
    cw = jnp.concatenate(
        [jnp.concatenate([conv_w[:, k * HYENA_WIDTH:(k + 1) * HYENA_WIDTH],
                          conv_b[:, k * HYENA_WIDTH:(k + 1) * HYENA_WIDTH]], axis=0) for k in range(3)], axis=0)
    seq_block = (2, None, N_TILES, TILE_ROWS, LANES)

    def conv_in_map(c, b, t):
        return (b, jnp.where(t < _T_M, 0, jnp.where(t < _T_E, 1, 2)) * N_CBLK + c, 0, 0, 0)

    def z_map(c, b, t):
        flat = c * (BATCH // 2) + b
        sel = jnp.where(t >= _T_M, flat, jnp.maximum(flat - 1, 0))
        return (sel % (BATCH // 2), 3 * N_CBLK + sel // (BATCH // 2), 0, 0, 0)

    const = lambda shape: pl.BlockSpec(shape, lambda c, b, t: (0,) * len(shape), pipeline_mode=pl.Buffered(1))
    in_specs = [
        pl.BlockSpec(seq_block, conv_in_map),
        pl.BlockSpec(seq_block, z_map),
        pl.BlockSpec((12, LANES), lambda c, b, t: (0, c)),
        pl.BlockSpec((None, None, N_SLOTS, 256, LANES), lambda c, b, t: (jnp.where(t >= _T_M, 1, 0), c, 0, 0, 0)),
        const((N_TILES, 128, TILE_ROWS)), const((N_TILES, TILE_ROWS, 128)),
        const((2, 256, 256)), const((2, 256, 256)),
    ]
    return pl.pallas_call(
        _hyena_kernel,
        grid=grid,
        in_specs=in_specs,
        out_specs=pl.BlockSpec(seq_block, lambda c, b, t: (b, c, 0, 0, 0)),
        out_shape=jax.ShapeDtypeStruct((BATCH, N_CBLK, N_TILES, TILE_ROWS, LANES), BF16),
        scratch_shapes=[pltpu.VMEM((2, N_TILES * A_STRIDE, LANES), jnp.uint32),
                        pltpu.VMEM((2, SLOT_BATCH, 256, 2 * LANES), BF16)],
        compiler_params=pltpu.CompilerParams(
            dimension_semantics=("arbitrary", "arbitrary", "arbitrary"), vmem_limit_bytes=VMEM_LIMIT),
        name="hyena",
    )(p5, p5, cw, kspec, f1, f1i, gf, gb)


POOL_HALO = 8


def _out_kernel(yh_ref, up_ref, upw_ref, upn_ref, zp_ref, x_ref, pw_ref, ps_ref, gh_ref, gp_ref, w32_ref, gpost_ref,
                o_ref, r_scr, yp_scr, w_ref, prev_scr):
    _cast_weights_once(w32_ref, w_ref, 512)

    @pl.when(_first_step())
    def _():
        prev_scr[...] = jnp.zeros_like(prev_scr)

    g = pl.program_id(1)
    last_step = N_TILES // OUT_TILES - 1
    row = lax.broadcasted_iota(jnp.int32, (TILE_ROWS, LANES), 0)

    def pool_tile(ci, l):
        if l < 0:
            wrapped = _shift_down(upw_ref[ci, POOL_HALO + l].astype(F32))
            return jnp.where(g == 0, wrapped, prev_scr[ci, POOL_HALO + l].astype(F32))
        if l >= OUT_TILES:
            t = upn_ref[ci, l - OUT_TILES].astype(F32)
            return jnp.where(g == last_step, _shift_up(t), t)
        return up_ref[ci, l].astype(F32)

    for ci, w in enumerate(POOL_WINDOWS):
        lo, hi = w // 2, w - 1 - w // 2
        window_sum = pool_tile(ci, -lo)
        for d in range(-lo + 1, hi + 1):
            window_sum = window_sum + pool_tile(ci, d)
        for sb in range(OUT_TILES // SUB_TILES):
            pooled = []
            for i in range(SUB_TILES):
                l = sb * SUB_TILES + i
                if l > 0:
                    window_sum = window_sum + pool_tile(ci, l + hi) - pool_tile(ci, l - 1 - lo)
                inv_cnt = 1.0 / w
                if l < lo:
                    inv_cnt = jnp.where((g == 0) & (row == 0), 1.0 / (hi + l + 1), inv_cnt)
                elif l > OUT_TILES - 1 - hi:
                    inv_cnt = jnp.where((g == last_step) & (row == TILE_ROWS - 1),
                                        1.0 / (lo + OUT_TILES - l), inv_cnt)
                pooled.append((window_sum * inv_cnt - pool_tile(ci, l)).astype(BF16))
            y = jnp.dot(jnp.concatenate(pooled, axis=0), pw_ref[ci].astype(BF16), preferred_element_type=F32)
            y = y * ps_ref[:, ci * LANES:(ci + 1) * LANES]
            gate = jnp.concatenate([zp_ref[ci, sb * SUB_TILES + i] for i in range(SUB_TILES)], axis=0).astype(F32)
            yp_scr[sb, ci] = y * gate
        prev_scr[ci] = up_ref[ci, OUT_TILES - POOL_HALO:OUT_TILES]

    def rms(y, gain_ref):
        ms = jnp.mean(y * y, axis=-1, keepdims=True)
        return (y * lax.rsqrt(ms + EPS) * gain_ref[...]).astype(BF16)

    for sb in range(OUT_TILES // SUB_TILES):
        j0 = sb * SUB_TILES
        yh = jnp.concatenate(
            [jnp.concatenate([yh_ref[cb, j] for cb in range(N_CBLK)], axis=-1) for j in range(j0, j0 + SUB_TILES)],
            axis=0).astype(F32)
        yp = jnp.concatenate([yp_scr[sb, ci] for ci in range(N_CBLK)], axis=-1)
        yc = jnp.concatenate([rms(yh, gh_ref), rms(yp, gp_ref)], axis=-1)
        out = jnp.dot(yc, w_ref[...], preferred_element_type=F32)
        ms = jnp.mean(out * out, axis=-1, keepdims=True)
        out = out * lax.rsqrt(ms + EPS) * gpost_ref[...]
        for j in range(SUB_TILES):
            for k in range(D_MODEL // LANES):
                r_scr[sb, k, pl.ds(j, TILE_ROWS, stride=SUB_TILES), :] = out[j * TILE_ROWS:(j + 1) * TILE_ROWS,
                                                                             k * LANES:(k + 1) * LANES]
        r = jnp.concatenate([r_scr[sb, k] for k in range(D_MODEL // LANES)], axis=-1)
        o_ref[:, j0:j0 + SUB_TILES, :] = (x_ref[:, j0:j0 + SUB_TILES, :]
                                          + r.reshape(TILE_ROWS, SUB_TILES, D_MODEL))


def _out_proj(yh, p5, x4, pool_w, pool_scale, norm_h_g, norm_p_g, w_out, post_g):
    n_steps = N_TILES // OUT_TILES
    halo_blocks = N_TILES // POOL_HALO
    per_step = OUT_TILES // POOL_HALO
    u_group, z_group = 4, 5
    y_spec = pl.BlockSpec((None, N_CBLK, OUT_TILES, TILE_ROWS, LANES), lambda b, g: (b, 0, g, 0, 0))
    x_spec = pl.BlockSpec((None, TILE_ROWS, OUT_TILES, D_MODEL), lambda b, g: (b, 0, g, 0))
    full = lambda shape: pl.BlockSpec(shape, lambda b, g: (0,) * len(shape))
    return pl.pallas_call(
        _out_kernel,
        grid=(BATCH, n_steps),
        in_specs=[
            y_spec,
            pl.BlockSpec((None, N_CBLK, OUT_TILES, TILE_ROWS, LANES), lambda b, g: (b, u_group, g, 0, 0)),
            pl.BlockSpec((None, N_CBLK, POOL_HALO, TILE_ROWS, LANES),
                         lambda b, g: (b, u_group, halo_blocks - 1, 0, 0)),
            pl.BlockSpec((None, N_CBLK, POOL_HALO, TILE_ROWS, LANES),
                         lambda b, g: (b, u_group, (g * per_step + per_step) % halo_blocks, 0, 0)),
            pl.BlockSpec((None, N_CBLK, OUT_TILES, TILE_ROWS, LANES), lambda b, g: (b, z_group, g, 0, 0)),
            x_spec,
            full((N_CBLK, LANES, LANES)), full((1, POOL_WIDTH)),
            full((1, HYENA_WIDTH)), full((1, POOL_WIDTH)),
            pl.BlockSpec((D_MODEL, D_MODEL), lambda b, g: (0, 0), pipeline_mode=pl.Buffered(1)), full((1, D_MODEL)),
        ],
        out_specs=x_spec,
        out_shape=jax.ShapeDtypeStruct((BATCH, TILE_ROWS, N_TILES, D_MODEL), F32),
        scratch_shapes=[pltpu.VMEM((OUT_TILES // SUB_TILES, D_MODEL // LANES, TILE_ROWS * SUB_TILES, LANES), F32),
                        pltpu.VMEM((OUT_TILES // SUB_TILES, N_CBLK, TILE_ROWS * SUB_TILES, LANES), F32),
                        pltpu.VMEM((D_MODEL, D_MODEL), BF16),
                        pltpu.VMEM((N_CBLK, POOL_HALO, TILE_ROWS, LANES), BF16)],
        compiler_params=pltpu.CompilerParams(
            dimension_semantics=("arbitrary", "arbitrary"), vmem_limit_bytes=VMEM_LIMIT),
        name="out_proj",
    )(yh, p5, p5, p5, p5, x4, pool_w, pool_scale, norm_h_g, norm_p_g, w_out, post_g)


def kernel(x, pre_norm_g, w_in, conv_w, conv_b, filt_w1, filt_b1, filt_w2, filt_b2, filt_w3, filt_b3,
           filt_freq, filt_w_out, hyena_d, pool_w, pool_scale, norm_h_g, norm_p_g, w_out, post_norm_g):
    assert x.shape == (BATCH, SEQ, D_MODEL) and pre_norm_g.shape[0] == 1
    f1, f1i, gf, gb, f1f = (jnp.asarray(m, F32).astype(BF16) for m in (_F1, _F1I, _GF, _GB, _F1F))

    x4 = x.reshape(BATCH, TILE_ROWS, N_TILES, D_MODEL)
    p5, taps = _in_proj(x4, pre_norm_g, w_in[0], jnp.asarray(_ZFEAT), filt_w1[0], filt_b1[0], filt_w2[0], filt_b2[0],
                        filt_w3[0], filt_b3[0], filt_freq[0], filt_w_out[0], jnp.asarray(_ABS_DELTAS))
    kspec = _filter_spec(taps, hyena_d[0], f1f, gf)

    yh = _hyena(p5, conv_w[0], conv_b, kspec, f1, f1i, gf, gb)
    out4 = _out_proj(yh, p5, x4, pool_w[0], pool_scale, norm_h_g, norm_p_g,
                     w_out[0], post_norm_g)
    return out4.reshape(BATCH, SEQ, D_MODEL)
```

```python
import functools
import math

import numpy as np
import jax
import jax.numpy as jnp
from jax import lax
from jax.experimental import pallas as pl
from jax.experimental.pallas import tpu as pltpu

F32 = jnp.float32
BF16 = jnp.bfloat16

D_MODEL = 1024
BATCH = 4
SEQ = 8192
HYENA_WIDTH = 512
POOL_WIDTH = 512
POOL_WINDOWS = (2, 4, 8, 16)
FILTER_EMB = 33
FILTER_BANDS = 16
FILTER_HIDDEN = 64
PROJ_WIDTH = 3072
EPS = 1e-6

LANES = 128
N_FFT = 2 * SEQ
N_TILES = 128
TILE_ROWS = SEQ // N_TILES
N_SLOTS = 64
HY_SUB = 32
A_STRIDE = 72
SLOT_BATCH = 4
N_CBLK = HYENA_WIDTH // LANES
PROJ_BLOCKS = PROJ_WIDTH // LANES
IN_TILES = 16
OUT_TILES = 32
SUB_TILES = 8
VMEM_LIMIT = 60 * 1024 * 1024
OUT_VMEM_LIMIT = 63 * 1024 * 1024


def _dft_tables():
    n1 = np.arange(TILE_ROWS)
    n2 = np.arange(N_TILES)
    s = np.arange(N_SLOTS)
    ph = 2 * np.pi * (n2[:, None, None] * s[None, :, None] / N_FFT
                      + n1[None, None, :] * s[None, :, None] / 128.0)
    f1 = np.zeros((N_TILES, 128, TILE_ROWS))
    f1[:, :64, :] = np.cos(ph)
    f1[:, 64:, :] = -np.sin(ph)
    f1[:, 0, :] = 1.0
    f1[:, 64, :] = (-1.0) ** n1
    php = np.transpose(ph, (0, 2, 1))
    f1i = np.zeros((N_TILES, TILE_ROWS, 128))
    f1i[:, :, :64] = 2 * np.cos(php) / N_FFT
    f1i[:, :, 64:] = -2 * np.sin(php) / N_FFT
    f1i[:, :, 0] = 1.0 / N_FFT
    f1i[:, :, 64] = ((-1.0) ** n1)[None, :] / N_FFT
    k2 = np.arange(128)
    th = 2 * np.pi * np.outer(k2, n2) / 128.0
    c, sn = np.cos(th), np.sin(th)
    g = np.block([[c, sn], [-sn, c]])
    gi = np.block([[c, -sn], [sn, c]])
    kk = np.arange(64)
    tha = 2 * np.pi * np.outer(kk, n2) / 128.0
    thb = 2 * np.pi * np.outer(64 + 128 * kk, n2) / N_FFT
    g0 = np.zeros((256, 256))
    g0[0:64, 0:128] = np.cos(tha)
    g0[64:128, 128:256] = np.cos(thb)
    g0[128:192, 0:128] = -np.sin(tha)
    g0[128, 0:128] = (-1.0) ** n2
    g0[192:256, 128:256] = -np.sin(thb)
    g0i = np.zeros((256, 256))
    g0i[0:128, 0:64] = 2 * np.cos(tha.T)
    g0i[0:128, 0] = 1.0
    g0i[0:128, 128:192] = -2 * np.sin(tha.T)
    g0i[0:128, 128] = (-1.0) ** n2
    g0i[128:256, 64:128] = 2 * np.cos(thb.T)
    g0i[128:256, 192:256] = -2 * np.sin(thb.T)
    q = np.arange(256)
    perm = np.where(q % 16 < 8, 8 * (q // 16) + q % 16, 128 + 8 * (q // 16) + q % 16 - 8)
    gf = np.stack([g0[perm, :], g[perm, :]])
    gb = np.stack([g0i[:, perm], gi[:, perm]])
    n1f = np.arange(128)
    phf = 2 * np.pi * (n2[:, None, None] * s[None, :, None] / N_FFT
                       + n1f[None, None, :] * s[None, :, None] / 128.0)
    full = np.zeros((N_TILES, 128, 128))
    full[:, :64, :] = np.cos(phf)
    full[:, 64:, :] = -np.sin(phf)
    full[:, 0, :] = 1.0
    full[:, 64, :] = (-1.0) ** n1f
    rev = full[:, :, 127:63:-1].copy()
    rev[0, :, 1:] = full[0, :, 127:64:-1]
    rev[0, :, 0] = 0.0
    f1f = np.concatenate([full[:, :, :64], rev], axis=2)
    pair64 = np.arange(128) % 2 * 64 + np.arange(128) // 2
    pair128 = np.arange(256) % 2 * 128 + np.arange(256) // 2
    f1, f1f = f1[:, pair64, :], f1f[:, pair64, :]
    f1i = f1i[:, :, pair64]
    gf = gf[:, :, pair128]
    gb = gb[:, pair128, :]
    return f1, f1i, gf, gb, f1f


def _filter_features():
    pos = np.arange(SEQ, dtype=np.float64)
    t = pos / (SEQ - 1)
    ang = 2.0 * math.pi * pos / SEQ
    bands = np.linspace(1e-4, FILTER_BANDS - 1, FILTER_BANDS)
    z = np.concatenate([t[:, None], np.cos(bands[None, :] * ang[:, None]),
                        -np.sin(bands[None, :] * ang[:, None])], axis=-1)
    z = z.reshape(TILE_ROWS, N_TILES, FILTER_EMB).transpose(1, 0, 2).reshape(SEQ, FILTER_EMB)
    zp = np.zeros((FILTER_HIDDEN, SEQ))
    zp[:FILTER_EMB, :] = z.T
    max_decay = math.log(1e-2) / 0.3
    min_decay = math.log(1e-2) / 1.5
    deltas = np.abs(np.linspace(min_decay, max_decay, HYENA_WIDTH))
    return zp.astype(np.float32), deltas.astype(np.float32)[None, :]


_F1, _F1I, _GF, _GB, _F1F = _dft_tables()
_ZFEAT, _ABS_DELTAS = _filter_features()


def _shift_down(x):
    rows = lax.broadcasted_iota(jnp.int32, x.shape, 0)
    return jnp.where(rows == 0, 0.0, pltpu.roll(x, 1, axis=0))


def _shift_up(x):
    rows = lax.broadcasted_iota(jnp.int32, x.shape, 0)
    return jnp.where(rows == x.shape[0] - 1, 0.0, pltpu.roll(x, x.shape[0] - 1, axis=0))


def _pair(ref, i):
    return jnp.concatenate([ref[0, i], ref[1, i]], axis=-1).astype(F32)


def _dup(x):
    return jnp.concatenate([x, x], axis=-1)


def _silu(z):
    hz = 0.5 * z
    return hz * (1.0 + jnp.tanh(hz))


def _tile_rows(n2):
    return pl.ds(pl.multiple_of(n2 * A_STRIDE, 8), N_SLOTS)


def _slot_rows(s):
    return pl.ds(s, N_TILES, stride=A_STRIDE)


def _stage_load(a_ref, rows):
    words = jnp.concatenate([a_ref[0, rows, :], a_ref[1, rows, :]], axis=-1)
    return pltpu.bitcast(words, BF16)


def _stage_store(a_ref, rows, val):
    words = pltpu.bitcast(val.astype(BF16), jnp.uint32)
    a_ref[0, rows, :] = words[:, :LANES]
    a_ref[1, rows, :] = words[:, LANES:]


N_BATCHES = N_SLOTS // SLOT_BATCH
SPEC_BLOCKS = 16


def _batch_slots(i):
    return [i * SLOT_BATCH + j for j in range(SLOT_BATCH)]


def _stage2_forward(a_ref, gf_ref, slots, sel0):
    xs = [_stage_load(a_ref, _slot_rows(s)) for s in slots]
    return [jnp.dot(gf_ref[sel0] if j == 0 else gf_ref[1], x, preferred_element_type=F32)
            for j, x in enumerate(xs)]


def _re_im_blocks(y):
    return [(y[16 * i:16 * i + 8], y[16 * i + 8:16 * i + 16]) for i in range(SPEC_BLOCKS)]


_GATE_GROUPS = (3, 5)
N_IN_STEPS = BATCH * (N_TILES // IN_TILES)
FILT_TILES = N_TILES // N_IN_STEPS


def _split_bf16(x):
    hi = x.astype(BF16)
    return hi, (x - hi.astype(F32)).astype(BF16)


def _dot_split(a, b):
    a_hi, a_lo = _split_bf16(a)
    b_hi, b_lo = _split_bf16(b)
    dot = lambda u, v: jnp.dot(u, v, preferred_element_type=F32)
    return dot(a_hi, b_hi) + dot(a_lo, b_hi) + dot(a_hi, b_lo)


def _first_step():
    return (pl.program_id(0) == 0) & (pl.program_id(1) == 0)


def _cast_weights_once(w_ref, w_scr, ncol):
    @pl.when(_first_step())
    def _():
        for c in range(w_ref.shape[1] // ncol):
            w_scr[:, c * ncol:(c + 1) * ncol] = w_ref[:, c * ncol:(c + 1) * ncol].astype(BF16)


class _FilterTaps:
    def __init__(self, z_ref, p_ref, wp_scr, dl_ref, first_tile, o_ref):
        self.p_ref, self.wp_scr, self.dl_ref, self.first_tile, self.o_ref = p_ref, wp_scr, dl_ref, first_tile, o_ref
        self.cols_blk = p_ref[3].T[:FILTER_HIDDEN]
        self.h = z_ref[...]
        self.n_stages = 5

    def stage(self, k):
        hid = FILTER_HIDDEN
        if k < 3:
            w_t = self.p_ref[k].T[:hid, :hid]
            freq = self.cols_blk[:, 3:4]
            self.h = jnp.sin(freq * (_dot_split(w_t, self.h) + self.cols_blk[:, k:k + 1]))
            return
        if k == 3:
            h_hi = self.h.astype(BF16).astype(F32)
            stack = jnp.concatenate([h_hi, self.h - h_hi, h_hi, jnp.zeros_like(h_hi)], axis=0)
            self.lhs = stack.T.astype(BF16)
            rows = FILT_TILES * TILE_ROWS
            r = lax.broadcasted_iota(jnp.int32, (rows, HYENA_WIDTH), 0)
            pos = 128 * (r % TILE_ROWS) + self.first_tile + r // TILE_ROWS
            t = pos.astype(F32) / float(SEQ - 1)
            self.decay = jnp.exp(-t * self.dl_ref[...])
        for od in ((0, 1) if k == 3 else (2, 3)):
            cols = slice(od * HYENA_WIDTH, (od + 1) * HYENA_WIDTH)
            taps = jnp.dot(self.lhs, self.wp_scr[:, cols], preferred_element_type=F32) * self.decay
            taps = taps.astype(BF16)
            for cb in range(N_CBLK):
                for i in range(FILT_TILES):
                    self.o_ref[od * N_CBLK + cb, i] = taps[i * TILE_ROWS:(i + 1) * TILE_ROWS,
                                                           cb * LANES:(cb + 1) * LANES]


def _in_proj_kernel(x_ref, g_ref, w32_ref, z_ref, p_ref, wp32_ref, dl_ref, o_ref, taps_ref, h_scr, w_ref, wp_scr):
    _cast_weights_once(w32_ref, w_ref, 512)

    @pl.when(_first_step())
    def _():
        wp_hi, wp_lo = _split_bf16(wp32_ref[...])
        wp_scr[...] = jnp.concatenate([wp_hi, wp_hi, wp_lo, jnp.zeros_like(wp_lo)], axis=0)

    step = pl.program_id(0) * (N_TILES // IN_TILES) + pl.program_id(1)
    filt = _FilterTaps(z_ref, p_ref, wp_scr, dl_ref, step * FILT_TILES, taps_ref)
    n_dots = 0

    ncol = 512
    for sb in range(IN_TILES // SUB_TILES):
        j0 = sb * SUB_TILES
        x = x_ref[:, j0:j0 + SUB_TILES, :].reshape(TILE_ROWS * SUB_TILES, D_MODEL)
        ms = jnp.mean(x * x, axis=-1, keepdims=True)
        hn = x * lax.rsqrt(ms + EPS) * g_ref[...]
        for k in range(D_MODEL // LANES):
            h_scr[sb, k] = hn[:, k * LANES:(k + 1) * LANES]
        h = jnp.concatenate(
            [jnp.concatenate([h_scr[sb, k, pl.ds(j, TILE_ROWS, stride=SUB_TILES), :]
                              for k in range(D_MODEL // LANES)], axis=-1).astype(BF16)
             for j in range(SUB_TILES)], axis=0)
        for c in range(PROJ_WIDTH // ncol):
            p = jnp.dot(h, w_ref[:, c * ncol:(c + 1) * ncol], preferred_element_type=F32)
            if (c * ncol) // HYENA_WIDTH in _GATE_GROUPS:
                p = _silu(p)
            p = p.astype(BF16)
            for cb in range(ncol // LANES):
                for j in range(SUB_TILES):
                    o_ref[c * (ncol // LANES) + cb, j0 + j] = p[j * TILE_ROWS:(j + 1) * TILE_ROWS,
                                                                cb * LANES:(cb + 1) * LANES]
            if n_dots % 2 == 0 and n_dots // 2 < filt.n_stages:
                filt.stage(n_dots // 2)
            n_dots += 1


def _in_proj(x4, pre_g, w_in, zfeat_t, w1, b1, w2, b2, w3, b3, freq, w_proj, abs_deltas):
    steps_per_batch = N_TILES // IN_TILES
    pad2 = lambda m: jnp.pad(m, ((0, LANES - m.shape[0]), (0, LANES - m.shape[1])))
    params = jnp.stack([pad2(w1), pad2(w2), pad2(w3), pad2(jnp.stack([b1, b2, b3, freq], axis=0))])
    full = lambda shape: pl.BlockSpec(shape, lambda b, g: (0,) * len(shape))
    once = lambda shape: pl.BlockSpec(shape, lambda b, g: (0,) * len(shape), pipeline_mode=pl.Buffered(1))
    return pl.pallas_call(
        _in_proj_kernel,
        grid=(BATCH, steps_per_batch),
        in_specs=[
            pl.BlockSpec((None, TILE_ROWS, IN_TILES, D_MODEL), lambda b, g: (b, 0, g, 0)),
            full((1, D_MODEL)),
            once((D_MODEL, PROJ_WIDTH)),
            pl.BlockSpec((FILTER_HIDDEN, FILT_TILES * TILE_ROWS), lambda b, g: (0, b * steps_per_batch + g)),
            full((4, LANES, LANES)),
            once((FILTER_HIDDEN, 4 * HYENA_WIDTH)),
            full((1, HYENA_WIDTH)),
        ],
        out_specs=[
            pl.BlockSpec((None, PROJ_BLOCKS, IN_TILES, TILE_ROWS, LANES), lambda b, g: (b, 0, g, 0, 0)),
            pl.BlockSpec((4 * N_CBLK, FILT_TILES, TILE_ROWS, LANES), lambda b, g: (0, b * steps_per_batch + g, 0, 0)),
        ],
        out_shape=[jax.ShapeDtypeStruct((BATCH, PROJ_BLOCKS, N_TILES, TILE_ROWS, LANES), BF16),
                   jax.ShapeDtypeStruct((4 * N_CBLK, N_TILES, TILE_ROWS, LANES), BF16)],
        scratch_shapes=[pltpu.VMEM((IN_TILES // SUB_TILES, D_MODEL // LANES, TILE_ROWS * SUB_TILES, LANES), F32),
                        pltpu.VMEM((D_MODEL, PROJ_WIDTH), BF16),
                        pltpu.VMEM((2 * LANES, 4 * HYENA_WIDTH), BF16)],
        compiler_params=pltpu.CompilerParams(
            dimension_semantics=("arbitrary", "arbitrary"), vmem_limit_bytes=VMEM_LIMIT),
        name="in_proj",
    )(x4, pre_g, w_in, zfeat_t, params, w_proj, abs_deltas)


def _filter_spec_kernel(fa_ref, fb_ref, ba_ref, bb_ref, d_ref, f1f_ref, gf_ref, k_ref, a_scr):
    def stage1(sb, carry):
        for i in range(HY_SUB):
            n2 = sb * HY_SUB + i
            nb = (N_TILES - n2) % N_TILES
            fwd = jnp.concatenate([fa_ref[n2], fb_ref[n2]], axis=-1)
            bwd = jnp.concatenate([ba_ref[nb], bb_ref[nb]], axis=-1)
            taps = jnp.concatenate([fwd, bwd], axis=0)
            _stage_store(a_scr, _tile_rows(n2), jnp.dot(f1f_ref[n2], taps, preferred_element_type=F32))
        return carry

    lax.fori_loop(0, N_SUB, stage1, 0)

    d = d_ref[pl.ds(pl.program_id(0), 1), :]

    def batch(slots, first):
        ys = _stage2_forward(a_scr, gf_ref, slots, 0 if first else 1)
        for j, (s, y) in enumerate(zip(slots, ys)):
            blocks = []
            for i, (yr, yi) in enumerate(_re_im_blocks(y)):
                yr = yr + d
                if first and j == 0 and i == 0:
                    rows = lax.broadcasted_iota(jnp.int32, yi.shape, 0)
                    yi = yi + jnp.where(rows == 0, d, 0.0)
                blocks += [yr, yi]
            spec = jnp.concatenate(blocks, axis=0).astype(BF16)
            k_ref[0, s] = spec[:, :LANES]
            k_ref[1, s] = spec[:, LANES:]

    batch(_batch_slots(0), True)

    def loop(i, carry):
        batch(_batch_slots(i), False)
        return carry

    lax.fori_loop(1, N_BATCHES, loop, 0, unroll=3)


def _filter_spec(taps, hyena_d, f1f, gf):
    grid = (2, N_CBLK // 2)
    const = lambda shape: pl.BlockSpec(shape, lambda o, c: (0,) * len(shape), pipeline_mode=pl.Buffered(1))

    def taps_spec(direction, k):
        return pl.BlockSpec((None, N_TILES, TILE_ROWS, LANES),
                            lambda o, c: ((2 * o + direction) * N_CBLK + 2 * c + k, 0, 0, 0))

    return pl.pallas_call(
        _filter_spec_kernel,
        grid=grid,
        in_specs=[taps_spec(0, 0), taps_spec(0, 1), taps_spec(1, 0), taps_spec(1, 1),
                  pl.BlockSpec((2, 2 * LANES), lambda o, c: (0, c)),
                  const((N_TILES, 128, 128)), const((2, 256, 256))],
        out_specs=pl.BlockSpec((None, 2, N_SLOTS, 256, LANES), lambda o, c: (o, c, 0, 0, 0)),
        out_shape=jax.ShapeDtypeStruct((2, N_CBLK, N_SLOTS, 256, LANES), BF16),
        scratch_shapes=[pltpu.VMEM((2, N_TILES * A_STRIDE, LANES), jnp.uint32)],
        compiler_params=pltpu.CompilerParams(
            dimension_semantics=("arbitrary", "arbitrary"), vmem_limit_bytes=VMEM_LIMIT),
        name="filter_spec",
    )(taps, taps, taps, taps, hyena_d, f1f, gf)


_T_S1 = 0
_T_F0 = 1
_T_M = 2
_T_F1 = 3
_T_E = 4
_T_END = 5
N_SUB = N_TILES // HY_SUB
MID_GROUP = 8

def _hyena_kernel(cin_ref, pz_ref, cw_ref, k_ref, f1_ref, f1i_ref, gf_ref, gb_ref, o_ref, a_scr, z_scr):
    t = pl.program_id(2)

    def stage1(n2, u_bf):
        _stage_store(a_scr, _tile_rows(n2), jnp.dot(f1_ref[n2], u_bf, preferred_element_type=F32))

    def inv_stage1(n2):
        return jnp.dot(f1i_ref[n2], _stage_load(a_scr, _tile_rows(n2)), preferred_element_type=F32)

    def short_conv(sb, row0):
        base = sb * HY_SUB
        first = _pair(cin_ref, (base + N_TILES - 1) % N_TILES)
        first = jnp.where(sb == 0, _shift_down(first), first)
        last = _pair(cin_ref, (base + HY_SUB) % N_TILES)
        last = jnp.where(sb == N_SUB - 1, _shift_up(last), last)
        tiles = [first] + [_pair(cin_ref, base + i) for i in range(HY_SUB)] + [last]
        w = cw_ref[row0:row0 + 4, :]
        w0, w1, w2, b = _dup(w[0:1]), _dup(w[1:2]), _dup(w[2:3]), _dup(w[3:4])
        return [tiles[i] * w0 + tiles[i + 1] * w1 + tiles[i + 2] * w2 + b for i in range(HY_SUB)]

    def filter_multiply(slots, ys, first, buf):
        for j, (s, y) in enumerate(zip(slots, ys)):
            blocks = []
            for i, (yr, yi) in enumerate(_re_im_blocks(y)):
                kblk = k_ref[s, 16 * i:16 * i + 16, :].astype(F32)
                kr, ki = kblk[:8], kblk[8:]
                if first and j == 0 and i == 0:
                    rows = lax.broadcasted_iota(jnp.int32, kr.shape, 0)
                    ka, kb, kd = kr, jnp.where(rows == 0, 0.0, ki), jnp.where(rows == 0, ki, kr)
                else:
                    ka, kb, kd = kr, ki, kr
                ka, kb, kd = _dup(ka), _dup(kb), _dup(kd)
                blocks += [yr * ka - yi * kb, yr * kb + yi * kd]
            z_scr[buf, j] = jnp.concatenate(blocks, axis=0).astype(BF16)

    def stage2_inverse(slots, first_sel, buf):
        bms = [jnp.dot(gb_ref[first_sel] if j == 0 else gb_ref[1], z_scr[buf, j], preferred_element_type=F32)
               for j in range(SLOT_BATCH)]
        for s, bm in zip(slots, bms):
            _stage_store(a_scr, _slot_rows(s), bm)

    def spectral_phase():
        slots0 = _batch_slots(0)
        filter_multiply(slots0, _stage2_forward(a_scr, gf_ref, slots0, 0), True, 0)

        def loop(i, carry):
            slots = _batch_slots(i)
            ys = _stage2_forward(a_scr, gf_ref, slots, 1)
            stage2_inverse(_batch_slots(i - 1), jnp.where(i == 1, 0, 1), (i - 1) % 2)
            filter_multiply(slots, ys, False, i % 2)
            return carry

        lax.fori_loop(1, N_BATCHES, loop, 0, unroll=5)
        stage2_inverse(_batch_slots(N_BATCHES - 1), 1, (N_BATCHES - 1) % 2)

    @pl.when(t == _T_S1)
    def _():
        def sub(sb, carry):
            vs = short_conv(sb, 0)
            for i in range(HY_SUB):
                stage1(sb * HY_SUB + i, vs[i].astype(BF16))
            return carry

        lax.fori_loop(0, N_SUB, sub, 0)

    @pl.when((t == _T_F0) | (t == _T_F1))
    def _():
        spectral_phase()

    @pl.when(t == _T_M)
    def _():
        def sub(sb, carry):
            gates = short_conv(sb, 4)
            for q in range(0, HY_SUB, MID_GROUP):
                ys = [inv_stage1(sb * HY_SUB + q + i) for i in range(MID_GROUP)]
                us = [(gates[q + i] * ys[i]).astype(BF16) for i in range(MID_GROUP)]
                for i in range(MID_GROUP):
                    stage1(sb * HY_SUB + q + i, us[i])
            return carry

        lax.fori_loop(0, N_SUB, sub, 0)

    @pl.when(t == _T_E)
    def _():
        def sub(sb, carry):
            gates = short_conv(sb, 8)
            for i in range(HY_SUB):
                n2 = sb * HY_SUB + i
                res = (gates[i] * inv_stage1(n2) * _pair(pz_ref, n2)).astype(BF16)
                o_ref[0, n2] = res[:, :LANES]
                o_ref[1, n2] = res[:, LANES:]
            return carry

        lax.fori_loop(0, N_SUB, sub, 0)


def _hyena(p5, conv_w, conv_b, kspec, f1, f1i, gf, gb):
    grid = (N_CBLK, BATCH // 2, _T_END)
    cw = jnp.concatenate(
        [jnp.concatenate([conv_w[:, k * HYENA_WIDTH:(k + 1) * HYENA_WIDTH],
                          conv_b[:, k * HYENA_WIDTH:(k + 1) * HYENA_WIDTH]], axis=0) for k in range(3)], axis=0)
    seq_block = (2, None, N_TILES, TILE_ROWS, LANES)

    def conv_in_map(c, b, t):
        return (b, jnp.where(t < _T_M, 0, jnp.where(t < _T_E, 1, 2)) * N_CBLK + c, 0, 0, 0)

    def z_map(c, b, t):
        flat = c * (BATCH // 2) + b
        sel = jnp.where(t >= _T_M, flat, jnp.maximum(flat - 1, 0))
        return (sel % (BATCH // 2), 3 * N_CBLK + sel // (BATCH // 2), 0, 0, 0)

    const = lambda shape: pl.BlockSpec(shape, lambda c, b, t: (0,) * len(shape), pipeline_mode=pl.Buffered(1))
    in_specs = [
        pl.BlockSpec(seq_block, conv_in_map),
        pl.BlockSpec(seq_block, z_map),
        pl.BlockSpec((12, LANES), lambda c, b, t: (0, c)),
        pl.BlockSpec((None, None, N_SLOTS, 256, LANES), lambda c, b, t: (jnp.where(t >= _T_M, 1, 0), c, 0, 0, 0)),
        const((N_TILES, 128, TILE_ROWS)), const((N_TILES, TILE_ROWS, 128)),
        const((2, 256, 256)), const((2, 256, 256)),
    ]
    return pl.pallas_call(
        _hyena_kernel,
        grid=grid,
        in_specs=in_specs,
        out_specs=pl.BlockSpec(seq_block, lambda c, b, t: (b, c, 0, 0, 0)),
        out_shape=jax.ShapeDtypeStruct((BATCH, N_CBLK, N_TILES, TILE_ROWS, LANES), BF16),
        scratch_shapes=[pltpu.VMEM((2, N_TILES * A_STRIDE, LANES), jnp.uint32),
                        pltpu.VMEM((2, SLOT_BATCH, 256, 2 * LANES), BF16)],
        compiler_params=pltpu.CompilerParams(
            dimension_semantics=("arbitrary", "arbitrary", "arbitrary"), vmem_limit_bytes=VMEM_LIMIT),
        name="hyena",
    )(p5, p5, cw, kspec, f1, f1i, gf, gb)


POOL_HALO = 8


def _out_kernel(yh_ref, up_ref, upw_ref, upn_ref, zp_ref, x_ref, pw_ref, ps_ref, gh_ref, gp_ref, w_ref, gpost_ref,
                o_ref, r_scr, yp_scr, prev_scr):
    @pl.when(_first_step())
    def _():
        prev_scr[...] = jnp.zeros_like(prev_scr)

    g = pl.program_id(1)
    last_step = N_TILES // OUT_TILES - 1
    row = lax.broadcasted_iota(jnp.int32, (TILE_ROWS, LANES), 0)

    def pool_tile(ci, l):
        if l < 0:
            wrapped = _shift_down(upw_ref[ci, POOL_HALO + l].astype(F32))
            return jnp.where(g == 0, wrapped, prev_scr[ci, POOL_HALO + l].astype(F32))
        if l >= OUT_TILES:
            t = upn_ref[ci, l - OUT_TILES].astype(F32)
            return jnp.where(g == last_step, _shift_up(t), t)
        return up_ref[ci, l].astype(F32)

    n_sub = OUT_TILES // SUB_TILES
    window_sums = [None] * len(POOL_WINDOWS)

    def pool_group(sb, ci):
        w = POOL_WINDOWS[ci]
        lo, hi = w // 2, w - 1 - w // 2
        if sb == 0:
            window_sum = pool_tile(ci, -lo)
            for d in range(-lo + 1, hi + 1):
                window_sum = window_sum + pool_tile(ci, d)
        else:
            window_sum = window_sums[ci]
        pooled = []
        for i in range(SUB_TILES):
            l = sb * SUB_TILES + i
            if l > 0:
                window_sum = window_sum + pool_tile(ci, l + hi) - pool_tile(ci, l - 1 - lo)
            inv_cnt = 1.0 / w
            if l < lo:
                inv_cnt = jnp.where((g == 0) & (row == 0), 1.0 / (hi + l + 1), inv_cnt)
            elif l > OUT_TILES - 1 - hi:
                inv_cnt = jnp.where((g == last_step) & (row == TILE_ROWS - 1),
                                    1.0 / (lo + OUT_TILES - l), inv_cnt)
            pooled.append((window_sum * inv_cnt - pool_tile(ci, l)).astype(BF16))
        window_sums[ci] = window_sum
        y = jnp.dot(jnp.concatenate(pooled, axis=0), pw_ref[ci].astype(BF16), preferred_element_type=F32)
        y = y * ps_ref[:, ci * LANES:(ci + 1) * LANES]
        gate = jnp.concatenate([zp_ref[ci, sb * SUB_TILES + i] for i in range(SUB_TILES)], axis=0).astype(F32)
        yp_scr[sb % 2, ci] = y * gate

    def rms(y, gain_ref):
        ms = jnp.mean(y * y, axis=-1, keepdims=True)
        return (y * lax.rsqrt(ms + EPS) * gain_ref[...]).astype(BF16)

    for ci in range(N_CBLK):
        pool_group(0, ci)
    ncol = D_MODEL // N_CBLK
    for sb in range(n_sub):
        j0 = sb * SUB_TILES
        yh = jnp.concatenate(
            [jnp.concatenate([yh_ref[cb, j] for cb in range(N_CBLK)], axis=-1) for j in range(j0, j0 + SUB_TILES)],
            axis=0).astype(F32)
        yp = jnp.concatenate([yp_scr[sb % 2, ci] for ci in range(N_CBLK)], axis=-1)
        yc = jnp.concatenate([rms(yh, gh_ref), rms(yp, gp_ref)], axis=-1)
        chunks = []
        for c in range(N_CBLK):
            chunks.append(jnp.dot(yc, w_ref[:, c * ncol:(c + 1) * ncol], preferred_element_type=F32))
            if sb + 1 < n_sub:
                pool_group(sb + 1, c)
        out = jnp.concatenate(chunks, axis=-1)
        ms = jnp.mean(out * out, axis=-1, keepdims=True)
        out = out * lax.rsqrt(ms + EPS) * gpost_ref[...]
        for j in range(SUB_TILES):
            for k in range(D_MODEL // LANES):
                r_scr[sb % 2, k, pl.ds(j, TILE_ROWS, stride=SUB_TILES), :] = out[j * TILE_ROWS:(j + 1) * TILE_ROWS,
                                                                             k * LANES:(k + 1) * LANES]
        r = jnp.concatenate([r_scr[sb % 2, k] for k in range(D_MODEL // LANES)], axis=-1)
        o_ref[:, j0:j0 + SUB_TILES, :] = (x_ref[:, j0:j0 + SUB_TILES, :]
                                          + r.reshape(TILE_ROWS, SUB_TILES, D_MODEL))
    for ci in range(N_CBLK):
        prev_scr[ci] = up_ref[ci, OUT_TILES - POOL_HALO:OUT_TILES]


def _out_proj(yh, p5, x4, pool_w, pool_scale, norm_h_g, norm_p_g, w_out, post_g):
    n_steps = N_TILES // OUT_TILES
    halo_blocks = N_TILES // POOL_HALO
    per_step = OUT_TILES // POOL_HALO
    u_group, z_group = 4, 5
    y_spec = pl.BlockSpec((None, N_CBLK, OUT_TILES, TILE_ROWS, LANES), lambda b, g: (b, 0, g, 0, 0))
    x_spec = pl.BlockSpec((None, TILE_ROWS, OUT_TILES, D_MODEL), lambda b, g: (b, 0, g, 0))
    full = lambda shape: pl.BlockSpec(shape, lambda b, g: (0,) * len(shape))
    return pl.pallas_call(
        _out_kernel,
        grid=(BATCH, n_steps),
        in_specs=[
            y_spec,
            pl.BlockSpec((None, N_CBLK, OUT_TILES, TILE_ROWS, LANES), lambda b, g: (b, u_group, g, 0, 0)),
            pl.BlockSpec((None, N_CBLK, POOL_HALO, TILE_ROWS, LANES),
                         lambda b, g: (b, u_group, halo_blocks - 1, 0, 0)),
            pl.BlockSpec((None, N_CBLK, POOL_HALO, TILE_ROWS, LANES),
                         lambda b, g: (b, u_group, (g * per_step + per_step) % halo_blocks, 0, 0)),
            pl.BlockSpec((None, N_CBLK, OUT_TILES, TILE_ROWS, LANES), lambda b, g: (b, z_group, g, 0, 0)),
            x_spec,
            full((N_CBLK, LANES, LANES)), full((1, POOL_WIDTH)),
            full((1, HYENA_WIDTH)), full((1, POOL_WIDTH)),
            pl.BlockSpec((D_MODEL, D_MODEL), lambda b, g: (0, 0), pipeline_mode=pl.Buffered(1)), full((1, D_MODEL)),
        ],
        out_specs=x_spec,
        out_shape=jax.ShapeDtypeStruct((BATCH, TILE_ROWS, N_TILES, D_MODEL), F32),
        scratch_shapes=[pltpu.VMEM((2, D_MODEL // LANES, TILE_ROWS * SUB_TILES, LANES), F32),
                        pltpu.VMEM((2, N_CBLK, TILE_ROWS * SUB_TILES, LANES), F32),
                        pltpu.VMEM((N_CBLK, POOL_HALO, TILE_ROWS, LANES), BF16)],
        compiler_params=pltpu.CompilerParams(
            dimension_semantics=("arbitrary", "arbitrary"), vmem_limit_bytes=OUT_VMEM_LIMIT),
        name="out_proj",
    )(yh, p5, p5, p5, p5, x4, pool_w, pool_scale, norm_h_g, norm_p_g, w_out, post_g)


def kernel(x, pre_norm_g, w_in, conv_w, conv_b, filt_w1, filt_b1, filt_w2, filt_b2, filt_w3, filt_b3,
           filt_freq, filt_w_out, hyena_d, pool_w, pool_scale, norm_h_g, norm_p_g, w_out, post_norm_g):
    assert x.shape == (BATCH, SEQ, D_MODEL) and pre_norm_g.shape[0] == 1
    f1, f1i, gf, gb, f1f = (jnp.asarray(m, F32).astype(BF16) for m in (_F1, _F1I, _GF, _GB, _F1F))

    x4 = x.reshape(BATCH, TILE_ROWS, N_TILES, D_MODEL)
    p5, taps = _in_proj(x4, pre_norm_g, w_in[0], jnp.asarray(_ZFEAT), filt_w1[0], filt_b1[0], filt_w2[0], filt_b2[0],
                        filt_w3[0], filt_b3[0], filt_freq[0], filt_w_out[0], jnp.asarray(_ABS_DELTAS))
    kspec = _filter_spec(taps, hyena_d[0], f1f, gf)

    yh = _hyena(p5, conv_w[0], conv_b, kspec, f1, f1i, gf, gb)
    out4 = _out_proj(yh, p5, x4, pool_w[0], pool_scale, norm_h_g, norm_p_g,
                     w_out[0].astype(BF16), post_norm_g)
    return out4.reshape(BATCH, SEQ, D_MODEL)
```

```python
import functools
import math

import numpy as np
import jax
import jax.numpy as jnp
from jax import lax
from jax.experimental import pallas as pl
from jax.experimental.pallas import tpu as pltpu

F32 = jnp.float32
BF16 = jnp.bfloat16

D_MODEL = 1024
BATCH = 4
SEQ = 8192
HYENA_WIDTH = 512
POOL_WIDTH = 512
POOL_WINDOWS = (2, 4, 8, 16)
FILTER_EMB = 33
FILTER_BANDS = 16
FILTER_HIDDEN = 64
PROJ_WIDTH = 3072
EPS = 1e-6

LANES = 128
N_FFT = 2 * SEQ
N_TILES = 128
TILE_ROWS = SEQ // N_TILES
N_SLOTS = 64
HY_SUB = 64
A_STRIDE = 72
SLOT_BATCH = 4
N_CBLK = HYENA_WIDTH // LANES
PROJ_BLOCKS = PROJ_WIDTH // LANES
IN_TILES = 16
OUT_TILES = 32
SUB_TILES = 8
VMEM_LIMIT = 60 * 1024 * 1024
OUT_VMEM_LIMIT = 63 * 1024 * 1024


def _dft_tables():
    n1 = np.arange(TILE_ROWS)
    n2 = np.arange(N_TILES)
    s = np.arange(N_SLOTS)
    ph = 2 * np.pi * (n2[:, None, None] * s[None, :, None] / N_FFT
                      + n1[None, None, :] * s[None, :, None] / 128.0)
    f1 = np.zeros((N_TILES, 128, TILE_ROWS))
    f1[:, :64, :] = np.cos(ph)
    f1[:, 64:, :] = -np.sin(ph)
    f1[:, 0, :] = 1.0
    f1[:, 64, :] = (-1.0) ** n1
    php = np.transpose(ph, (0, 2, 1))
    f1i = np.zeros((N_TILES, TILE_ROWS, 128))
    f1i[:, :, :64] = 2 * np.cos(php) / N_FFT
    f1i[:, :, 64:] = -2 * np.sin(php) / N_FFT
    f1i[:, :, 0] = 1.0 / N_FFT
    f1i[:, :, 64] = ((-1.0) ** n1)[None, :] / N_FFT
    k2 = np.arange(128)
    th = 2 * np.pi * np.outer(k2, n2) / 128.0
    c, sn = np.cos(th), np.sin(th)
    g = np.block([[c, sn], [-sn, c]])
    gi = np.block([[c, -sn], [sn, c]])
    kk = np.arange(64)
    tha = 2 * np.pi * np.outer(kk, n2) / 128.0
    thb = 2 * np.pi * np.outer(64 + 128 * kk, n2) / N_FFT
    g0 = np.zeros((256, 256))
    g0[0:64, 0:128] = np.cos(tha)
    g0[64:128, 128:256] = np.cos(thb)
    g0[128:192, 0:128] = -np.sin(tha)
    g0[128, 0:128] = (-1.0) ** n2
    g0[192:256, 128:256] = -np.sin(thb)
    g0i = np.zeros((256, 256))
    g0i[0:128, 0:64] = 2 * np.cos(tha.T)
    g0i[0:128, 0] = 1.0
    g0i[0:128, 128:192] = -2 * np.sin(tha.T)
    g0i[0:128, 128] = (-1.0) ** n2
    g0i[128:256, 64:128] = 2 * np.cos(thb.T)
    g0i[128:256, 192:256] = -2 * np.sin(thb.T)
    q = np.arange(256)
    perm = np.where(q % 16 < 8, 8 * (q // 16) + q % 16, 128 + 8 * (q // 16) + q % 16 - 8)
    gf = np.stack([g0[perm, :], g[perm, :]])
    gb = np.stack([g0i[:, perm], gi[:, perm]])
    n1f = np.arange(128)
    phf = 2 * np.pi * (n2[:, None, None] * s[None, :, None] / N_FFT
                       + n1f[None, None, :] * s[None, :, None] / 128.0)
    full = np.zeros((N_TILES, 128, 128))
    full[:, :64, :] = np.cos(phf)
    full[:, 64:, :] = -np.sin(phf)
    full[:, 0, :] = 1.0
    full[:, 64, :] = (-1.0) ** n1f
    rev = full[:, :, 127:63:-1].copy()
    rev[0, :, 1:] = full[0, :, 127:64:-1]
    rev[0, :, 0] = 0.0
    f1f = np.concatenate([full[:, :, :64], rev], axis=2)
    pair64 = np.arange(128) % 2 * 64 + np.arange(128) // 2
    pair128 = np.arange(256) % 2 * 128 + np.arange(256) // 2
    f1, f1f = f1[:, pair64, :], f1f[:, pair64, :]
    f1i = f1i[:, :, pair64]
    gf = gf[:, :, pair128]
    gb = gb[:, pair128, :]
    return f1, f1i, gf, gb, f1f


def _filter_features():
    pos = np.arange(SEQ, dtype=np.float64)
    t = pos / (SEQ - 1)
    ang = 2.0 * math.pi * pos / SEQ
    bands = np.linspace(1e-4, FILTER_BANDS - 1, FILTER_BANDS)
    z = np.concatenate([t[:, None], np.cos(bands[None, :] * ang[:, None]),
                        -np.sin(bands[None, :] * ang[:, None])], axis=-1)
    z = z.reshape(TILE_ROWS, N_TILES, FILTER_EMB).transpose(1, 0, 2).reshape(SEQ, FILTER_EMB)
    zp = np.zeros((FILTER_HIDDEN, SEQ))
    zp[:FILTER_EMB, :] = z.T
    max_decay = math.log(1e-2) / 0.3
    min_decay = math.log(1e-2) / 1.5
    deltas = np.abs(np.linspace(min_decay, max_decay, HYENA_WIDTH))
    return zp.astype(np.float32), deltas.astype(np.float32)[None, :]


_F1, _F1I, _GF, _GB, _F1F = _dft_tables()
_ZFEAT, _ABS_DELTAS = _filter_features()


def _shift_down(x):
    rows = lax.broadcasted_iota(jnp.int32, x.shape, 0)
    return jnp.where(rows == 0, 0.0, pltpu.roll(x, 1, axis=0))


def _shift_up(x):
    rows = lax.broadcasted_iota(jnp.int32, x.shape, 0)
    return jnp.where(rows == x.shape[0] - 1, 0.0, pltpu.roll(x, x.shape[0] - 1, axis=0))


def _pair(ref, i):
    return jnp.concatenate([ref[0, i], ref[1, i]], axis=-1).astype(F32)


def _dup(x):
    return jnp.concatenate([x, x], axis=-1)


def _silu(z):
    hz = 0.5 * z
    return hz * (1.0 + jnp.tanh(hz))


def _tile_rows(n2):
    return pl.ds(pl.multiple_of(n2 * A_STRIDE, 8), N_SLOTS)


def _slot_rows(s):
    return pl.ds(s, N_TILES, stride=A_STRIDE)


def _stage_load(a_ref, rows):
    words = jnp.concatenate([a_ref[0, rows, :], a_ref[1, rows, :]], axis=-1)
    return pltpu.bitcast(words, BF16)


def _stage_store(a_ref, rows, val):
    words = pltpu.bitcast(val.astype(BF16), jnp.uint32)
    a_ref[0, rows, :] = words[:, :LANES]
    a_ref[1, rows, :] = words[:, LANES:]


N_BATCHES = N_SLOTS // SLOT_BATCH
SPEC_BLOCKS = 16


def _batch_slots(i):
    return [i * SLOT_BATCH + j for j in range(SLOT_BATCH)]


def _stage2_forward(a_ref, gf_ref, slots, sel0):
    xs = [_stage_load(a_ref, _slot_rows(s)) for s in slots]
    return [jnp.dot(gf_ref[sel0] if j == 0 else gf_ref[1], x, preferred_element_type=F32)
            for j, x in enumerate(xs)]


def _re_im_blocks(y):
    return [(y[16 * i:16 * i + 8], y[16 * i + 8:16 * i + 16]) for i in range(SPEC_BLOCKS)]


_GATE_GROUPS = (3, 5)
N_IN_STEPS = BATCH * (N_TILES // IN_TILES)
FILT_TILES = N_TILES // N_IN_STEPS


def _split_bf16(x):
    hi = x.astype(BF16)
    return hi, (x - hi.astype(F32)).astype(BF16)


def _dot_split(a, b):
    a_hi, a_lo = _split_bf16(a)
    b_hi, b_lo = _split_bf16(b)
    dot = lambda u, v: jnp.dot(u, v, preferred_element_type=F32)
    return dot(a_hi, b_hi) + dot(a_lo, b_hi) + dot(a_hi, b_lo)


def _first_step():
    return (pl.program_id(0) == 0) & (pl.program_id(1) == 0)


def _cast_weights_once(w_ref, w_scr, ncol):
    @pl.when(_first_step())
    def _():
        for c in range(w_ref.shape[1] // ncol):
            w_scr[:, c * ncol:(c + 1) * ncol] = w_ref[:, c * ncol:(c + 1) * ncol].astype(BF16)


class _FilterTaps:
    def __init__(self, z_ref, p_ref, wp_scr, dl_ref, first_tile, o_ref):
        self.p_ref, self.wp_scr, self.dl_ref, self.first_tile, self.o_ref = p_ref, wp_scr, dl_ref, first_tile, o_ref
        self.cols_blk = p_ref[3].T[:FILTER_HIDDEN]
        self.h = z_ref[...]
        self.n_stages = 5

    def stage(self, k):
        hid = FILTER_HIDDEN
        if k < 3:
            w_t = self.p_ref[k].T[:hid, :hid]
            freq = self.cols_blk[:, 3:4]
            self.h = jnp.sin(freq * (_dot_split(w_t, self.h) + self.cols_blk[:, k:k + 1]))
            return
        if k == 3:
            h_hi = self.h.astype(BF16).astype(F32)
            stack = jnp.concatenate([h_hi, self.h - h_hi, h_hi, jnp.zeros_like(h_hi)], axis=0)
            self.lhs = stack.T.astype(BF16)
            rows = FILT_TILES * TILE_ROWS
            r = lax.broadcasted_iota(jnp.int32, (rows, HYENA_WIDTH), 0)
            pos = 128 * (r % TILE_ROWS) + self.first_tile + r // TILE_ROWS
            t = pos.astype(F32) / float(SEQ - 1)
            self.decay = jnp.exp(-t * self.dl_ref[...])
        for od in ((0, 1) if k == 3 else (2, 3)):
            cols = slice(od * HYENA_WIDTH, (od + 1) * HYENA_WIDTH)
            taps = jnp.dot(self.lhs, self.wp_scr[:, cols], preferred_element_type=F32) * self.decay
            taps = taps.astype(BF16)
            for cb in range(N_CBLK):
                for i in range(FILT_TILES):
                    self.o_ref[od * N_CBLK + cb, i] = taps[i * TILE_ROWS:(i + 1) * TILE_ROWS,
                                                           cb * LANES:(cb + 1) * LANES]


def _in_proj_kernel(x_ref, g_ref, w32_ref, z_ref, p_ref, wp32_ref, dl_ref, o_ref, taps_ref, h_scr, w_ref, wp_scr):
    _cast_weights_once(w32_ref, w_ref, 512)

    @pl.when(_first_step())
    def _():
        wp_hi, wp_lo = _split_bf16(wp32_ref[...])
        wp_scr[...] = jnp.concatenate([wp_hi, wp_hi, wp_lo, jnp.zeros_like(wp_lo)], axis=0)

    step = pl.program_id(0) * (N_TILES // IN_TILES) + pl.program_id(1)
    filt = _FilterTaps(z_ref, p_ref, wp_scr, dl_ref, step * FILT_TILES, taps_ref)
    n_dots = 0

    ncol = 512
    for sb in range(IN_TILES // SUB_TILES):
        j0 = sb * SUB_TILES
        x = x_ref[:, j0:j0 + SUB_TILES, :].reshape(TILE_ROWS * SUB_TILES, D_MODEL)
        ms = jnp.mean(x * x, axis=-1, keepdims=True)
        hn = x * lax.rsqrt(ms + EPS) * g_ref[...]
        for k in range(D_MODEL // LANES):
            h_scr[sb, k] = hn[:, k * LANES:(k + 1) * LANES]
        h = jnp.concatenate(
            [jnp.concatenate([h_scr[sb, k, pl.ds(j, TILE_ROWS, stride=SUB_TILES), :]
                              for k in range(D_MODEL // LANES)], axis=-1).astype(BF16)
             for j in range(SUB_TILES)], axis=0)
        for c in range(PROJ_WIDTH // ncol):
            p = jnp.dot(h, w_ref[:, c * ncol:(c + 1) * ncol], preferred_element_type=F32)
            if (c * ncol) // HYENA_WIDTH in _GATE_GROUPS:
                p = _silu(p)
            p = p.astype(BF16)
            for cb in range(ncol // LANES):
                for j in range(SUB_TILES):
                    o_ref[c * (ncol // LANES) + cb, j0 + j] = p[j * TILE_ROWS:(j + 1) * TILE_ROWS,
                                                                cb * LANES:(cb + 1) * LANES]
            if n_dots % 2 == 0 and n_dots // 2 < filt.n_stages:
                filt.stage(n_dots // 2)
            n_dots += 1


def _in_proj(x4, pre_g, w_in, zfeat_t, w1, b1, w2, b2, w3, b3, freq, w_proj, abs_deltas):
    steps_per_batch = N_TILES // IN_TILES
    pad2 = lambda m: jnp.pad(m, ((0, LANES - m.shape[0]), (0, LANES - m.shape[1])))
    params = jnp.stack([pad2(w1), pad2(w2), pad2(w3), pad2(jnp.stack([b1, b2, b3, freq], axis=0))])
    full = lambda shape: pl.BlockSpec(shape, lambda b, g: (0,) * len(shape))
    once = lambda shape: pl.BlockSpec(shape, lambda b, g: (0,) * len(shape), pipeline_mode=pl.Buffered(1))
    return pl.pallas_call(
        _in_proj_kernel,
        grid=(BATCH, steps_per_batch),
        in_specs=[
            pl.BlockSpec((None, TILE_ROWS, IN_TILES, D_MODEL), lambda b, g: (b, 0, g, 0)),
            full((1, D_MODEL)),
            once((D_MODEL, PROJ_WIDTH)),
            pl.BlockSpec((FILTER_HIDDEN, FILT_TILES * TILE_ROWS), lambda b, g: (0, b * steps_per_batch + g)),
            full((4, LANES, LANES)),
            once((FILTER_HIDDEN, 4 * HYENA_WIDTH)),
            full((1, HYENA_WIDTH)),
        ],
        out_specs=[
            pl.BlockSpec((None, PROJ_BLOCKS, IN_TILES, TILE_ROWS, LANES), lambda b, g: (b, 0, g, 0, 0)),
            pl.BlockSpec((4 * N_CBLK, FILT_TILES, TILE_ROWS, LANES), lambda b, g: (0, b * steps_per_batch + g, 0, 0)),
        ],
        out_shape=[jax.ShapeDtypeStruct((BATCH, PROJ_BLOCKS, N_TILES, TILE_ROWS, LANES), BF16),
                   jax.ShapeDtypeStruct((4 * N_CBLK, N_TILES, TILE_ROWS, LANES), BF16)],
        scratch_shapes=[pltpu.VMEM((IN_TILES // SUB_TILES, D_MODEL // LANES, TILE_ROWS * SUB_TILES, LANES), F32),
                        pltpu.VMEM((D_MODEL, PROJ_WIDTH), BF16),
                        pltpu.VMEM((2 * LANES, 4 * HYENA_WIDTH), BF16)],
        compiler_params=pltpu.CompilerParams(
            dimension_semantics=("arbitrary", "arbitrary"), vmem_limit_bytes=VMEM_LIMIT),
        name="in_proj",
    )(x4, pre_g, w_in, zfeat_t, params, w_proj, abs_deltas)


def _filter_spec_kernel(fa_ref, fb_ref, ba_ref, bb_ref, d_ref, f1f_ref, gf_ref, k_ref, a_scr):
    def stage1(sb, carry):
        for i in range(HY_SUB):
            n2 = sb * HY_SUB + i
            nb = (N_TILES - n2) % N_TILES
            fwd = jnp.concatenate([fa_ref[n2], fb_ref[n2]], axis=-1)
            bwd = jnp.concatenate([ba_ref[nb], bb_ref[nb]], axis=-1)
            taps = jnp.concatenate([fwd, bwd], axis=0)
            _stage_store(a_scr, _tile_rows(n2), jnp.dot(f1f_ref[n2], taps, preferred_element_type=F32))
        return carry

    lax.fori_loop(0, N_SUB, stage1, 0)

    d = d_ref[pl.ds(pl.program_id(0), 1), :]

    def batch(slots, first):
        ys = _stage2_forward(a_scr, gf_ref, slots, 0 if first else 1)
        for j, (s, y) in enumerate(zip(slots, ys)):
            blocks = []
            for i, (yr, yi) in enumerate(_re_im_blocks(y)):
                yr = yr + d
                if first and j == 0 and i == 0:
                    rows = lax.broadcasted_iota(jnp.int32, yi.shape, 0)
                    yi = yi + jnp.where(rows == 0, d, 0.0)
                blocks += [yr, yi]
            spec = jnp.concatenate(blocks, axis=0).astype(BF16)
            k_ref[0, s] = spec[:, :LANES]
            k_ref[1, s] = spec[:, LANES:]

    batch(_batch_slots(0), True)

    def loop(i, carry):
        batch(_batch_slots(i), False)
        return carry

    lax.fori_loop(1, N_BATCHES, loop, 0, unroll=3)


def _filter_spec(taps, hyena_d, f1f, gf):
    grid = (2, N_CBLK // 2)
    const = lambda shape: pl.BlockSpec(shape, lambda o, c: (0,) * len(shape), pipeline_mode=pl.Buffered(1))

    def taps_spec(direction, k):
        return pl.BlockSpec((None, N_TILES, TILE_ROWS, LANES),
                            lambda o, c: ((2 * o + direction) * N_CBLK + 2 * c + k, 0, 0, 0))

    return pl.pallas_call(
        _filter_spec_kernel,
        grid=grid,
        in_specs=[taps_spec(0, 0), taps_spec(0, 1), taps_spec(1, 0), taps_spec(1, 1),
                  pl.BlockSpec((2, 2 * LANES), lambda o, c: (0, c)),
                  const((N_TILES, 128, 128)), const((2, 256, 256))],
        out_specs=pl.BlockSpec((None, 2, N_SLOTS, 256, LANES), lambda o, c: (o, c, 0, 0, 0)),
        out_shape=jax.ShapeDtypeStruct((2, N_CBLK, N_SLOTS, 256, LANES), BF16),
        scratch_shapes=[pltpu.VMEM((2, N_TILES * A_STRIDE, LANES), jnp.uint32)],
        compiler_params=pltpu.CompilerParams(
            dimension_semantics=("arbitrary", "arbitrary"), vmem_limit_bytes=VMEM_LIMIT),
        name="filter_spec",
    )(taps, taps, taps, taps, hyena_d, f1f, gf)


_T_S1 = 0
_T_F0 = 1
_T_M = 2
_T_F1 = 3
_T_E = 4
_T_END = 5
N_SUB = N_TILES // HY_SUB
MID_GROUP = 8

def _hyena_kernel(cin_ref, pz_ref, cw_ref, k_ref, f1_ref, f1i_ref, gf_ref, gb_ref, o_ref, a_scr, z_scr):
    t = pl.program_id(2)

    def stage1(n2, u_bf):
        _stage_store(a_scr, _tile_rows(n2), jnp.dot(f1_ref[n2], u_bf, preferred_element_type=F32))

    def inv_stage1(n2):
        return jnp.dot(f1i_ref[n2], _stage_load(a_scr, _tile_rows(n2)), preferred_element_type=F32)

    def short_conv(sb, row0):
        base = sb * HY_SUB
        first = _pair(cin_ref, (base + N_TILES - 1) % N_TILES)
        first = jnp.where(sb == 0, _shift_down(first), first)
        last = _pair(cin_ref, (base + HY_SUB) % N_TILES)
        last = jnp.where(sb == N_SUB - 1, _shift_up(last), last)
        tiles = [first] + [_pair(cin_ref, base + i) for i in range(HY_SUB)] + [last]
        w = cw_ref[row0:row0 + 4, :]
        w0, w1, w2, b = _dup(w[0:1]), _dup(w[1:2]), _dup(w[2:3]), _dup(w[3:4])
        return [tiles[i] * w0 + tiles[i + 1] * w1 + tiles[i + 2] * w2 + b for i in range(HY_SUB)]

    def filter_multiply(slots, ys, first, buf):
        for j, (s, y) in enumerate(zip(slots, ys)):
            blocks = []
            for i, (yr, yi) in enumerate(_re_im_blocks(y)):
                kblk = k_ref[s, 16 * i:16 * i + 16, :].astype(F32)
                kr, ki = kblk[:8], kblk[8:]
                if first and j == 0 and i == 0:
                    rows = lax.broadcasted_iota(jnp.int32, kr.shape, 0)
                    ka, kb, kd = kr, jnp.where(rows == 0, 0.0, ki), jnp.where(rows == 0, ki, kr)
                else:
                    ka, kb, kd = kr, ki, kr
                ka, kb, kd = _dup(ka), _dup(kb), _dup(kd)
                blocks += [yr * ka - yi * kb, yr * kb + yi * kd]
            z_scr[buf, j] = jnp.concatenate(blocks, axis=0).astype(BF16)

    def stage2_inverse(slots, first_sel, buf):
        bms = [jnp.dot(gb_ref[first_sel] if j == 0 else gb_ref[1], z_scr[buf, j], preferred_element_type=F32)
               for j in range(SLOT_BATCH)]
        for s, bm in zip(slots, bms):
            _stage_store(a_scr, _slot_rows(s), bm)

    def spectral_phase():
        slots0 = _batch_slots(0)
        filter_multiply(slots0, _stage2_forward(a_scr, gf_ref, slots0, 0), True, 0)

        def loop(i, carry):
            slots = _batch_slots(i)
            ys = _stage2_forward(a_scr, gf_ref, slots, 1)
            stage2_inverse(_batch_slots(i - 1), jnp.where(i == 1, 0, 1), (i - 1) % 2)
            filter_multiply(slots, ys, False, i % 2)
            return carry

        lax.fori_loop(1, N_BATCHES, loop, 0, unroll=5)
        stage2_inverse(_batch_slots(N_BATCHES - 1), 1, (N_BATCHES - 1) % 2)

    @pl.when(t == _T_S1)
    def _():
        def sub(sb, carry):
            vs = short_conv(sb, 0)
            for i in range(HY_SUB):
                stage1(sb * HY_SUB + i, vs[i].astype(BF16))
            return carry

        lax.fori_loop(0, N_SUB, sub, 0)

    @pl.when((t == _T_F0) | (t == _T_F1))
    def _():
        spectral_phase()

    @pl.when(t == _T_M)
    def _():
        def sub(sb, carry):
            gates = short_conv(sb, 4)
            for q in range(0, HY_SUB, MID_GROUP):
                ys = [inv_stage1(sb * HY_SUB + q + i) for i in range(MID_GROUP)]
                us = [(gates[q + i] * ys[i]).astype(BF16) for i in range(MID_GROUP)]
                for i in range(MID_GROUP):
                    stage1(sb * HY_SUB + q + i, us[i])
            return carry

        lax.fori_loop(0, N_SUB, sub, 0)

    @pl.when(t == _T_E)
    def _():
        def sub(sb, carry):
            gates = short_conv(sb, 8)
            for i in range(HY_SUB):
                n2 = sb * HY_SUB + i
                res = (gates[i] * inv_stage1(n2) * _pair(pz_ref, n2)).astype(BF16)
                o_ref[0, n2] = res[:, :LANES]
                o_ref[1, n2] = res[:, LANES:]
            return carry

        lax.fori_loop(0, N_SUB, sub, 0)


def _hyena(p5, conv_w, conv_b, kspec, f1, f1i, gf, gb):
    grid = (N_CBLK, BATCH // 2, _T_END)
    cw = jnp.concatenate(
        [jnp.concatenate([conv_w[:, k * HYENA_WIDTH:(k + 1) * HYENA_WIDTH],
                          conv_b[:, k * HYENA_WIDTH:(k + 1) * HYENA_WIDTH]], axis=0) for k in range(3)], axis=0)
    seq_block = (2, None, N_TILES, TILE_ROWS, LANES)

    def conv_in_map(c, b, t):
        return (b, jnp.where(t < _T_M, 0, jnp.where(t < _T_E, 1, 2)) * N_CBLK + c, 0, 0, 0)

    def z_map(c, b, t):
        flat = c * (BATCH // 2) + b
        sel = jnp.where(t >= _T_M, flat, jnp.maximum(flat - 1, 0))
        return (sel % (BATCH // 2), 3 * N_CBLK + sel // (BATCH // 2), 0, 0, 0)

    const = lambda shape: pl.BlockSpec(shape, lambda c, b, t: (0,) * len(shape), pipeline_mode=pl.Buffered(1))
    in_specs = [
        pl.BlockSpec(seq_block, conv_in_map),
        pl.BlockSpec(seq_block, z_map),
        pl.BlockSpec((12, LANES), lambda c, b, t: (0, c)),
        pl.BlockSpec((None, None, N_SLOTS, 256, LANES), lambda c, b, t: (jnp.where(t >= _T_M, 1, 0), c, 0, 0, 0)),
        const((N_TILES, 128, TILE_ROWS)), const((N_TILES, TILE_ROWS, 128)),
        const((2, 256, 256)), const((2, 256, 256)),
    ]
    return pl.pallas_call(
        _hyena_kernel,
        grid=grid,
        in_specs=in_specs,
        out_specs=pl.BlockSpec(seq_block, lambda c, b, t: (b, c, 0, 0, 0)),
        out_shape=jax.ShapeDtypeStruct((BATCH, N_CBLK, N_TILES, TILE_ROWS, LANES), BF16),
        scratch_shapes=[pltpu.VMEM((2, N_TILES * A_STRIDE, LANES), jnp.uint32),
                        pltpu.VMEM((2, SLOT_BATCH, 256, 2 * LANES), BF16)],
        compiler_params=pltpu.CompilerParams(
            dimension_semantics=("arbitrary", "arbitrary", "arbitrary"), vmem_limit_bytes=VMEM_LIMIT),
        name="hyena",
    )(p5, p5, cw, kspec, f1, f1i, gf, gb)


POOL_HALO = 8


def _out_kernel(yh_ref, up_ref, upw_ref, upn_ref, zp_ref, x_ref, pw_ref, ps_ref, gh_ref, gp_ref, w_ref, gpost_ref,
                o_ref, r_scr, yp_scr, prev_scr):
    @pl.when(_first_step())
    def _():
        prev_scr[...] = jnp.zeros_like(prev_scr)

    g = pl.program_id(1)
    last_step = N_TILES // OUT_TILES - 1
    row = lax.broadcasted_iota(jnp.int32, (TILE_ROWS, LANES), 0)

    def pool_tile(ci, l):
        if l < 0:
            wrapped = _shift_down(upw_ref[ci, POOL_HALO + l].astype(F32))
            return jnp.where(g == 0, wrapped, prev_scr[ci, POOL_HALO + l].astype(F32))
        if l >= OUT_TILES:
            t = upn_ref[ci, l - OUT_TILES].astype(F32)
            return jnp.where(g == last_step, _shift_up(t), t)
        return up_ref[ci, l].astype(F32)

    n_sub = OUT_TILES // SUB_TILES
    window_sums = [None] * len(POOL_WINDOWS)

    def pool_group(sb, ci):
        w = POOL_WINDOWS[ci]
        lo, hi = w // 2, w - 1 - w // 2
        if sb == 0:
            window_sum = pool_tile(ci, -lo)
            for d in range(-lo + 1, hi + 1):
                window_sum = window_sum + pool_tile(ci, d)
        else:
            window_sum = window_sums[ci]
        pooled = []
        for i in range(SUB_TILES):
            l = sb * SUB_TILES + i
            if l > 0:
                window_sum = window_sum + pool_tile(ci, l + hi) - pool_tile(ci, l - 1 - lo)
            inv_cnt = 1.0 / w
            if l < lo:
                inv_cnt = jnp.where((g == 0) & (row == 0), 1.0 / (hi + l + 1), inv_cnt)
            elif l > OUT_TILES - 1 - hi:
                inv_cnt = jnp.where((g == last_step) & (row == TILE_ROWS - 1),
                                    1.0 / (lo + OUT_TILES - l), inv_cnt)
            pooled.append((window_sum * inv_cnt - pool_tile(ci, l)).astype(BF16))
        window_sums[ci] = window_sum
        y = jnp.dot(jnp.concatenate(pooled, axis=0), pw_ref[ci].astype(BF16), preferred_element_type=F32)
        y = y * ps_ref[:, ci * LANES:(ci + 1) * LANES]
        gate = jnp.concatenate([zp_ref[ci, sb * SUB_TILES + i] for i in range(SUB_TILES)], axis=0).astype(F32)
        yp_scr[sb % 2, ci] = y * gate

    def rms(y, gain_ref):
        ms = jnp.mean(y * y, axis=-1, keepdims=True)
        return (y * lax.rsqrt(ms + EPS) * gain_ref[...]).astype(BF16)

    for ci in range(N_CBLK):
        pool_group(0, ci)
    ncol = D_MODEL // N_CBLK
    for sb in range(n_sub):
        j0 = sb * SUB_TILES
        yh = jnp.concatenate(
            [jnp.concatenate([yh_ref[cb, j] for cb in range(N_CBLK)], axis=-1) for j in range(j0, j0 + SUB_TILES)],
            axis=0).astype(F32)
        yp = jnp.concatenate([yp_scr[sb % 2, ci] for ci in range(N_CBLK)], axis=-1)
        yc = jnp.concatenate([rms(yh, gh_ref), rms(yp, gp_ref)], axis=-1)
        chunks = []
        for c in range(N_CBLK):
            chunks.append(jnp.dot(yc, w_ref[:, c * ncol:(c + 1) * ncol], preferred_element_type=F32))
            if sb + 1 < n_sub:
                pool_group(sb + 1, c)
        out = jnp.concatenate(chunks, axis=-1)
        ms = jnp.mean(out * out, axis=-1, keepdims=True)
        out = out * lax.rsqrt(ms + EPS) * gpost_ref[...]
        for j in range(SUB_TILES):
            for k in range(D_MODEL // LANES):
                r_scr[sb % 2, k, pl.ds(j, TILE_ROWS, stride=SUB_TILES), :] = out[j * TILE_ROWS:(j + 1) * TILE_ROWS,
                                                                             k * LANES:(k + 1) * LANES]
        r = jnp.concatenate([r_scr[sb % 2, k] for k in range(D_MODEL // LANES)], axis=-1)
        o_ref[:, j0:j0 + SUB_TILES, :] = (x_ref[:, j0:j0 + SUB_TILES, :]
                                          + r.reshape(TILE_ROWS, SUB_TILES, D_MODEL))
    for ci in range(N_CBLK):
        prev_scr[ci] = up_ref[ci, OUT_TILES - POOL_HALO:OUT_TILES]


def _out_proj(yh, p5, x4, pool_w, pool_scale, norm_h_g, norm_p_g, w_out, post_g):
    n_steps = N_TILES // OUT_TILES
    halo_blocks = N_TILES // POOL_HALO
    per_step = OUT_TILES // POOL_HALO
    u_group, z_group = 4, 5
    y_spec = pl.BlockSpec((None, N_CBLK, OUT_TILES, TILE_ROWS, LANES), lambda b, g: (b, 0, g, 0, 0))
    x_spec = pl.BlockSpec((None, TILE_ROWS, OUT_TILES, D_MODEL), lambda b, g: (b, 0, g, 0))
    full = lambda shape: pl.BlockSpec(shape, lambda b, g: (0,) * len(shape))
    return pl.pallas_call(
        _out_kernel,
        grid=(BATCH, n_steps),
        in_specs=[
            y_spec,
            pl.BlockSpec((None, N_CBLK, OUT_TILES, TILE_ROWS, LANES), lambda b, g: (b, u_group, g, 0, 0)),
            pl.BlockSpec((None, N_CBLK, POOL_HALO, TILE_ROWS, LANES),
                         lambda b, g: (b, u_group, halo_blocks - 1, 0, 0)),
            pl.BlockSpec((None, N_CBLK, POOL_HALO, TILE_ROWS, LANES),
                         lambda b, g: (b, u_group, (g * per_step + per_step) % halo_blocks, 0, 0)),
            pl.BlockSpec((None, N_CBLK, OUT_TILES, TILE_ROWS, LANES), lambda b, g: (b, z_group, g, 0, 0)),
            x_spec,
            full((N_CBLK, LANES, LANES)), full((1, POOL_WIDTH)),
            full((1, HYENA_WIDTH)), full((1, POOL_WIDTH)),
            pl.BlockSpec((D_MODEL, D_MODEL), lambda b, g: (0, 0), pipeline_mode=pl.Buffered(1)), full((1, D_MODEL)),
        ],
        out_specs=x_spec,
        out_shape=jax.ShapeDtypeStruct((BATCH, TILE_ROWS, N_TILES, D_MODEL), F32),
        scratch_shapes=[pltpu.VMEM((2, D_MODEL // LANES, TILE_ROWS * SUB_TILES, LANES), F32),
                        pltpu.VMEM((2, N_CBLK, TILE_ROWS * SUB_TILES, LANES), F32),
                        pltpu.VMEM((N_CBLK, POOL_HALO, TILE_ROWS, LANES), BF16)],
        compiler_params=pltpu.CompilerParams(
            dimension_semantics=("arbitrary", "arbitrary"), vmem_limit_bytes=OUT_VMEM_LIMIT),
        name="out_proj",
    )(yh, p5, p5, p5, p5, x4, pool_w, pool_scale, norm_h_g, norm_p_g, w_out, post_g)


def kernel(x, pre_norm_g, w_in, conv_w, conv_b, filt_w1, filt_b1, filt_w2, filt_b2, filt_w3, filt_b3,
           filt_freq, filt_w_out, hyena_d, pool_w, pool_scale, norm_h_g, norm_p_g, w_out, post_norm_g):
    assert x.shape == (BATCH, SEQ, D_MODEL) and pre_norm_g.shape[0] == 1
    f1, f1i, gf, gb, f1f = (jnp.asarray(m, F32).astype(BF16) for m in (_F1, _F1I, _GF, _GB, _F1F))

    x4 = x.reshape(BATCH, TILE_ROWS, N_TILES, D_MODEL)
    p5, taps = _in_proj(x4, pre_norm_g, w_in[0], jnp.asarray(_ZFEAT), filt_w1[0], filt_b1[0], filt_w2[0], filt_b2[0],
                        filt_w3[0], filt_b3[0], filt_freq[0], filt_w_out[0], jnp.asarray(_ABS_DELTAS))
    kspec = _filter_spec(taps, hyena_d[0], f1f, gf)

    yh = _hyena(p5, conv_w[0], conv_b, kspec, f1, f1i, gf, gb)
    out4 = _out_proj(yh, p5, x4, pool_w[0], pool_scale, norm_h_g, norm_p_g,
                     w_out[0].astype(BF16), post_norm_g)
    return out4.reshape(BATCH, SEQ, D_MODEL)
```

```python
import math

import numpy as np
import jax
import jax.numpy as jnp
from jax import lax
from jax.experimental import pallas as pl
from jax.experimental.pallas import tpu as pltpu

F32 = jnp.float32
BF16 = jnp.bfloat16

D_MODEL = 1024
BATCH = 4
SEQ = 8192
HYENA_WIDTH = 512
POOL_WIDTH = 512
POOL_WINDOWS = (2, 4, 8, 16)
FILTER_EMB = 33
FILTER_BANDS = 16
FILTER_HIDDEN = 64
PROJ_WIDTH = 3072
EPS = 1e-6

LANES = 128
N_FFT = 2 * SEQ
N_TILES = 128
TILE_ROWS = SEQ // N_TILES
N_SLOTS = 64
HY_SUB = 64
N_SUB = N_TILES // HY_SUB
A_STRIDE = 72
SLOT_BATCH = 4
N_CBLK = HYENA_WIDTH // LANES
PROJ_BLOCKS = PROJ_WIDTH // LANES
IN_TILES = 16
OUT_TILES = 32
SUB_TILES = 8
VMEM_LIMIT = 60 * 1024 * 1024
OUT_VMEM_LIMIT = 63 * 1024 * 1024


def _dft_tables():
    n1 = np.arange(TILE_ROWS)
    n2 = np.arange(N_TILES)
    s = np.arange(N_SLOTS)
    ph = 2 * np.pi * (n2[:, None, None] * s[None, :, None] / N_FFT
                      + n1[None, None, :] * s[None, :, None] / 128.0)
    f1 = np.zeros((N_TILES, 128, TILE_ROWS))
    f1[:, :64, :] = np.cos(ph)
    f1[:, 64:, :] = -np.sin(ph)
    f1[:, 0, :] = 1.0
    f1[:, 64, :] = (-1.0) ** n1
    php = np.transpose(ph, (0, 2, 1))
    f1i = np.zeros((N_TILES, TILE_ROWS, 128))
    f1i[:, :, :64] = 2 * np.cos(php) / N_FFT
    f1i[:, :, 64:] = -2 * np.sin(php) / N_FFT
    f1i[:, :, 0] = 1.0 / N_FFT
    f1i[:, :, 64] = ((-1.0) ** n1)[None, :] / N_FFT
    k2 = np.arange(128)
    th = 2 * np.pi * np.outer(k2, n2) / 128.0
    c, sn = np.cos(th), np.sin(th)
    g = np.block([[c, sn], [-sn, c]])
    gi = np.block([[c, -sn], [sn, c]])
    kk = np.arange(64)
    tha = 2 * np.pi * np.outer(kk, n2) / 128.0
    thb = 2 * np.pi * np.outer(64 + 128 * kk, n2) / N_FFT
    g0 = np.zeros((256, 256))
    g0[0:64, 0:128] = np.cos(tha)
    g0[64:128, 128:256] = np.cos(thb)
    g0[128:192, 0:128] = -np.sin(tha)
    g0[128, 0:128] = (-1.0) ** n2
    g0[192:256, 128:256] = -np.sin(thb)
    g0i = np.zeros((256, 256))
    g0i[0:128, 0:64] = 2 * np.cos(tha.T)
    g0i[0:128, 0] = 1.0
    g0i[0:128, 128:192] = -2 * np.sin(tha.T)
    g0i[0:128, 128] = (-1.0) ** n2
    g0i[128:256, 64:128] = 2 * np.cos(thb.T)
    g0i[128:256, 192:256] = -2 * np.sin(thb.T)
    q = np.arange(256)
    perm = np.where(q % 16 < 8, 8 * (q // 16) + q % 16, 128 + 8 * (q // 16) + q % 16 - 8)
    gf = np.stack([g0[perm, :], g[perm, :]])
    gb = np.stack([g0i[:, perm], gi[:, perm]])
    n1f = np.arange(128)
    phf = 2 * np.pi * (n2[:, None, None] * s[None, :, None] / N_FFT
                       + n1f[None, None, :] * s[None, :, None] / 128.0)
    full = np.zeros((N_TILES, 128, 128))
    full[:, :64, :] = np.cos(phf)
    full[:, 64:, :] = -np.sin(phf)
    full[:, 0, :] = 1.0
    full[:, 64, :] = (-1.0) ** n1f
    rev = full[:, :, 127:63:-1].copy()
    rev[0, :, 1:] = full[0, :, 127:64:-1]
    rev[0, :, 0] = 0.0
    f1f = np.concatenate([full[:, :, :64], rev], axis=2)
    pair64 = np.arange(128) % 2 * 64 + np.arange(128) // 2
    pair128 = np.arange(256) % 2 * 128 + np.arange(256) // 2
    f1, f1f = f1[:, pair64, :], f1f[:, pair64, :]
    f1i = f1i[:, :, pair64]
    gf = gf[:, :, pair128]
    gb = gb[:, pair128, :]
    return f1, f1i, gf, gb, f1f


def _filter_features():
    pos = np.arange(SEQ, dtype=np.float64)
    t = pos / (SEQ - 1)
    ang = 2.0 * math.pi * pos / SEQ
    bands = np.linspace(1e-4, FILTER_BANDS - 1, FILTER_BANDS)
    z = np.concatenate([t[:, None], np.cos(bands[None, :] * ang[:, None]),
                        -np.sin(bands[None, :] * ang[:, None])], axis=-1)
    z = z.reshape(TILE_ROWS, N_TILES, FILTER_EMB).transpose(1, 0, 2).reshape(SEQ, FILTER_EMB)
    zp = np.zeros((FILTER_HIDDEN, SEQ))
    zp[:FILTER_EMB, :] = z.T
    max_decay = math.log(1e-2) / 0.3
    min_decay = math.log(1e-2) / 1.5
    deltas = np.abs(np.linspace(min_decay, max_decay, HYENA_WIDTH))
    return zp.astype(np.float32), deltas.astype(np.float32)[None, :]


_F1, _F1I, _GF, _GB, _F1F = _dft_tables()
_ZFEAT, _ABS_DELTAS = _filter_features()


def _shift_down(x):
    rows = lax.broadcasted_iota(jnp.int32, x.shape, 0)
    return jnp.where(rows == 0, 0.0, pltpu.roll(x, 1, axis=0))


def _shift_up(x):
    rows = lax.broadcasted_iota(jnp.int32, x.shape, 0)
    return jnp.where(rows == x.shape[0] - 1, 0.0, pltpu.roll(x, x.shape[0] - 1, axis=0))


def _pair(ref, i):
    return jnp.concatenate([ref[0, i], ref[1, i]], axis=-1).astype(F32)


def _dup(x):
    return jnp.concatenate([x, x], axis=-1)


def _silu(z):
    hz = 0.5 * z
    return hz * (1.0 + jnp.tanh(hz))


def _tile_rows(n2):
    return pl.ds(pl.multiple_of(n2 * A_STRIDE, 8), N_SLOTS)


def _slot_rows(s):
    return pl.ds(s, N_TILES, stride=A_STRIDE)


def _stage_load(a_ref, rows):
    words = jnp.concatenate([a_ref[0, rows, :], a_ref[1, rows, :]], axis=-1)
    return pltpu.bitcast(words, BF16)


def _stage_store(a_ref, rows, val):
    words = pltpu.bitcast(val.astype(BF16), jnp.uint32)
    a_ref[0, rows, :] = words[:, :LANES]
    a_ref[1, rows, :] = words[:, LANES:]


N_BATCHES = N_SLOTS // SLOT_BATCH
SPEC_BLOCKS = 16


def _batch_slots(i):
    return [i * SLOT_BATCH + j for j in range(SLOT_BATCH)]


def _stage2_forward(a_ref, gf_ref, slots, sel0):
    xs = [_stage_load(a_ref, _slot_rows(s)) for s in slots]
    return [jnp.dot(gf_ref[sel0] if j == 0 else gf_ref[1], x, preferred_element_type=F32)
            for j, x in enumerate(xs)]


def _re_im_blocks(y):
    return [(y[16 * i:16 * i + 8], y[16 * i + 8:16 * i + 16]) for i in range(SPEC_BLOCKS)]


_GATE_GROUPS = (3, 5)
N_IN_STEPS = BATCH * (N_TILES // IN_TILES)
FILT_TILES = N_TILES // N_IN_STEPS


def _split_bf16(x):
    hi = x.astype(BF16)
    return hi, (x - hi.astype(F32)).astype(BF16)


def _dot_split(a, b):
    a_hi, a_lo = _split_bf16(a)
    b_hi, b_lo = _split_bf16(b)
    dot = lambda u, v: jnp.dot(u, v, preferred_element_type=F32)
    return dot(a_hi, b_hi) + dot(a_lo, b_hi) + dot(a_hi, b_lo)


def _first_step():
    return (pl.program_id(0) == 0) & (pl.program_id(1) == 0)


def _cast_weights_once(w_ref, w_scr, ncol):
    @pl.when(_first_step())
    def _():
        for c in range(w_ref.shape[1] // ncol):
            w_scr[:, c * ncol:(c + 1) * ncol] = w_ref[:, c * ncol:(c + 1) * ncol].astype(BF16)


class _FilterTaps:
    def __init__(self, z_ref, p_ref, wp_scr, dl_ref, first_tile, o_ref):
        self.p_ref, self.wp_scr, self.dl_ref, self.first_tile, self.o_ref = p_ref, wp_scr, dl_ref, first_tile, o_ref
        self.cols_blk = p_ref[3].T[:FILTER_HIDDEN]
        self.h = z_ref[...]
        self.n_stages = 5

    def stage(self, k):
        hid = FILTER_HIDDEN
        if k < 3:
            w_t = self.p_ref[k].T[:hid, :hid]
            freq = self.cols_blk[:, 3:4]
            self.h = jnp.sin(freq * (_dot_split(w_t, self.h) + self.cols_blk[:, k:k + 1]))
            return
        if k == 3:
            h_hi = self.h.astype(BF16).astype(F32)
            stack = jnp.concatenate([h_hi, self.h - h_hi, h_hi, jnp.zeros_like(h_hi)], axis=0)
            self.lhs = stack.T.astype(BF16)
            rows = FILT_TILES * TILE_ROWS
            r = lax.broadcasted_iota(jnp.int32, (rows, HYENA_WIDTH), 0)
            pos = 128 * (r % TILE_ROWS) + self.first_tile + r // TILE_ROWS
            t = pos.astype(F32) / float(SEQ - 1)
            self.decay = jnp.exp(-t * self.dl_ref[...])
        for od in ((0, 1) if k == 3 else (2, 3)):
            cols = slice(od * HYENA_WIDTH, (od + 1) * HYENA_WIDTH)
            taps = jnp.dot(self.lhs, self.wp_scr[:, cols], preferred_element_type=F32) * self.decay
            taps = taps.astype(BF16)
            for cb in range(N_CBLK):
                for i in range(FILT_TILES):
                    self.o_ref[od * N_CBLK + cb, i] = taps[i * TILE_ROWS:(i + 1) * TILE_ROWS,
                                                           cb * LANES:(cb + 1) * LANES]


def _in_proj_kernel(x_ref, g_ref, w32_ref, z_ref, p_ref, wp32_ref, dl_ref, o_ref, taps_ref, h_scr, w_ref, wp_scr):
    _cast_weights_once(w32_ref, w_ref, 512)

    @pl.when(_first_step())
    def _():
        wp_hi, wp_lo = _split_bf16(wp32_ref[...])
        wp_scr[...] = jnp.concatenate([wp_hi, wp_hi, wp_lo, jnp.zeros_like(wp_lo)], axis=0)

    step = pl.program_id(0) * (N_TILES // IN_TILES) + pl.program_id(1)
    filt = _FilterTaps(z_ref, p_ref, wp_scr, dl_ref, step * FILT_TILES, taps_ref)
    n_dots = 0

    ncol = 512
    for sb in range(IN_TILES // SUB_TILES):
        j0 = sb * SUB_TILES
        x = x_ref[:, j0:j0 + SUB_TILES, :].reshape(TILE_ROWS * SUB_TILES, D_MODEL)
        ms = jnp.mean(x * x, axis=-1, keepdims=True)
        hn = x * lax.rsqrt(ms + EPS) * g_ref[...]
        for k in range(D_MODEL // LANES):
            h_scr[sb, k] = hn[:, k * LANES:(k + 1) * LANES]
        h = jnp.concatenate(
            [jnp.concatenate([h_scr[sb, k, pl.ds(j, TILE_ROWS, stride=SUB_TILES), :]
                              for k in range(D_MODEL // LANES)], axis=-1).astype(BF16)
             for j in range(SUB_TILES)], axis=0)
        for c in range(PROJ_WIDTH // ncol):
            p = jnp.dot(h, w_ref[:, c * ncol:(c + 1) * ncol], preferred_element_type=F32)
            if (c * ncol) // HYENA_WIDTH in _GATE_GROUPS:
                p = _silu(p)
            p = p.astype(BF16)
            for cb in range(ncol // LANES):
                for j in range(SUB_TILES):
                    o_ref[c * (ncol // LANES) + cb, j0 + j] = p[j * TILE_ROWS:(j + 1) * TILE_ROWS,
                                                                cb * LANES:(cb + 1) * LANES]
            if n_dots % 2 == 0 and n_dots // 2 < filt.n_stages:
                filt.stage(n_dots // 2)
            n_dots += 1


def _in_proj(x4, pre_g, w_in, zfeat_t, w1, b1, w2, b2, w3, b3, freq, w_proj, abs_deltas):
    steps_per_batch = N_TILES // IN_TILES
    pad2 = lambda m: jnp.pad(m, ((0, LANES - m.shape[0]), (0, LANES - m.shape[1])))
    params = jnp.stack([pad2(w1), pad2(w2), pad2(w3), pad2(jnp.stack([b1, b2, b3, freq], axis=0))])
    full = lambda shape: pl.BlockSpec(shape, lambda b, g: (0,) * len(shape))
    once = lambda shape: pl.BlockSpec(shape, lambda b, g: (0,) * len(shape), pipeline_mode=pl.Buffered(1))
    return pl.pallas_call(
        _in_proj_kernel,
        grid=(BATCH, steps_per_batch),
        in_specs=[
            pl.BlockSpec((None, TILE_ROWS, IN_TILES, D_MODEL), lambda b, g: (b, 0, g, 0)),
            full((1, D_MODEL)),
            once((D_MODEL, PROJ_WIDTH)),
            pl.BlockSpec((FILTER_HIDDEN, FILT_TILES * TILE_ROWS), lambda b, g: (0, b * steps_per_batch + g)),
            full((4, LANES, LANES)),
            once((FILTER_HIDDEN, 4 * HYENA_WIDTH)),
            full((1, HYENA_WIDTH)),
        ],
        out_specs=[
            pl.BlockSpec((None, PROJ_BLOCKS, IN_TILES, TILE_ROWS, LANES), lambda b, g: (b, 0, g, 0, 0)),
            pl.BlockSpec((4 * N_CBLK, FILT_TILES, TILE_ROWS, LANES), lambda b, g: (0, b * steps_per_batch + g, 0, 0)),
        ],
        out_shape=[jax.ShapeDtypeStruct((BATCH, PROJ_BLOCKS, N_TILES, TILE_ROWS, LANES), BF16),
                   jax.ShapeDtypeStruct((4 * N_CBLK, N_TILES, TILE_ROWS, LANES), BF16)],
        scratch_shapes=[pltpu.VMEM((IN_TILES // SUB_TILES, D_MODEL // LANES, TILE_ROWS * SUB_TILES, LANES), F32),
                        pltpu.VMEM((D_MODEL, PROJ_WIDTH), BF16),
                        pltpu.VMEM((2 * LANES, 4 * HYENA_WIDTH), BF16)],
        compiler_params=pltpu.CompilerParams(
            dimension_semantics=("arbitrary", "arbitrary"), vmem_limit_bytes=VMEM_LIMIT),
        name="in_proj",
    )(x4, pre_g, w_in, zfeat_t, params, w_proj, abs_deltas)


def _filter_spec_kernel(fa_ref, fb_ref, ba_ref, bb_ref, d_ref, f1f_ref, gf_ref, k_ref, a_scr):
    def stage1(sb, carry):
        for i in range(HY_SUB):
            n2 = sb * HY_SUB + i
            nb = (N_TILES - n2) % N_TILES
            fwd = jnp.concatenate([fa_ref[n2], fb_ref[n2]], axis=-1)
            bwd = jnp.concatenate([ba_ref[nb], bb_ref[nb]], axis=-1)
            taps = jnp.concatenate([fwd, bwd], axis=0)
            _stage_store(a_scr, _tile_rows(n2), jnp.dot(f1f_ref[n2], taps, preferred_element_type=F32))
        return carry

    lax.fori_loop(0, N_SUB, stage1, 0)

    d = d_ref[pl.ds(pl.program_id(0), 1), :]

    def batch(slots, first):
        ys = _stage2_forward(a_scr, gf_ref, slots, 0 if first else 1)
        for j, (s, y) in enumerate(zip(slots, ys)):
            blocks = []
            for i, (yr, yi) in enumerate(_re_im_blocks(y)):
                yr = yr + d
                if first and j == 0 and i == 0:
                    rows = lax.broadcasted_iota(jnp.int32, yi.shape, 0)
                    yi = yi + jnp.where(rows == 0, d, 0.0)
                blocks += [yr, yi]
            spec = jnp.concatenate(blocks, axis=0).astype(BF16)
            k_ref[0, s] = spec[:, :LANES]
            k_ref[1, s] = spec[:, LANES:]

    batch(_batch_slots(0), True)

    def loop(i, carry):
        batch(_batch_slots(i), False)
        return carry

    lax.fori_loop(1, N_BATCHES, loop, 0, unroll=3)


def _filter_spec(taps, hyena_d, f1f, gf):
    grid = (2, N_CBLK // 2)
    const = lambda shape: pl.BlockSpec(shape, lambda o, c: (0,) * len(shape), pipeline_mode=pl.Buffered(1))

    def taps_spec(direction, k):
        return pl.BlockSpec((None, N_TILES, TILE_ROWS, LANES),
                            lambda o, c: ((2 * o + direction) * N_CBLK + 2 * c + k, 0, 0, 0))

    return pl.pallas_call(
        _filter_spec_kernel,
        grid=grid,
        in_specs=[taps_spec(0, 0), taps_spec(0, 1), taps_spec(1, 0), taps_spec(1, 1),
                  pl.BlockSpec((2, 2 * LANES), lambda o, c: (0, c)),
                  const((N_TILES, 128, 128)), const((2, 256, 256))],
        out_specs=pl.BlockSpec((None, 2, N_SLOTS, 256, LANES), lambda o, c: (o, c, 0, 0, 0)),
        out_shape=jax.ShapeDtypeStruct((2, N_CBLK, N_SLOTS, 256, LANES), BF16),
        scratch_shapes=[pltpu.VMEM((2, N_TILES * A_STRIDE, LANES), jnp.uint32)],
        compiler_params=pltpu.CompilerParams(
            dimension_semantics=("arbitrary", "arbitrary"), vmem_limit_bytes=VMEM_LIMIT),
        name="filter_spec",
    )(taps, taps, taps, taps, hyena_d, f1f, gf)


_T_S1 = 0
_T_F0 = 1
_T_M = 2
_T_F1 = 3
_T_E = 4
_T_END = 5
MID_GROUP = 8


def _hyena_kernel(cin_ref, pz_ref, cw_ref, k_ref, f1_ref, f1i_ref, gf_ref, gb_ref, o_ref, a_scr, z_scr):
    t = pl.program_id(2)

    def stage1(n2, u_bf):
        _stage_store(a_scr, _tile_rows(n2), jnp.dot(f1_ref[n2], u_bf, preferred_element_type=F32))

    def inv_stage1(n2):
        return jnp.dot(f1i_ref[n2], _stage_load(a_scr, _tile_rows(n2)), preferred_element_type=F32)

    def short_conv(sb, row0):
        base = sb * HY_SUB
        first = _pair(cin_ref, (base + N_TILES - 1) % N_TILES)
        first = jnp.where(sb == 0, _shift_down(first), first)
        last = _pair(cin_ref, (base + HY_SUB) % N_TILES)
        last = jnp.where(sb == N_SUB - 1, _shift_up(last), last)
        tiles = [first] + [_pair(cin_ref, base + i) for i in range(HY_SUB)] + [last]
        w = cw_ref[row0:row0 + 4, :]
        w0, w1, w2, b = _dup(w[0:1]), _dup(w[1:2]), _dup(w[2:3]), _dup(w[3:4])
        return [tiles[i] * w0 + tiles[i + 1] * w1 + tiles[i + 2] * w2 + b for i in range(HY_SUB)]

    def filter_multiply(slots, ys, first, buf):
        for j, (s, y) in enumerate(zip(slots, ys)):
            blocks = []
            for i, (yr, yi) in enumerate(_re_im_blocks(y)):
                kblk = k_ref[s, 16 * i:16 * i + 16, :].astype(F32)
                kr, ki = kblk[:8], kblk[8:]
                if first and j == 0 and i == 0:
                    rows = lax.broadcasted_iota(jnp.int32, kr.shape, 0)
                    ka, kb, kd = kr, jnp.where(rows == 0, 0.0, ki), jnp.where(rows == 0, ki, kr)
                else:
                    ka, kb, kd = kr, ki, kr
                ka, kb, kd = _dup(ka), _dup(kb), _dup(kd)
                blocks += [yr * ka - yi * kb, yr * kb + yi * kd]
            z_scr[buf, j] = jnp.concatenate(blocks, axis=0).astype(BF16)

    def stage2_inverse(slots, first_sel, buf):
        bms = [jnp.dot(gb_ref[first_sel] if j == 0 else gb_ref[1], z_scr[buf, j], preferred_element_type=F32)
               for j in range(SLOT_BATCH)]
        for s, bm in zip(slots, bms):
            _stage_store(a_scr, _slot_rows(s), bm)

    def spectral_phase():
        slots0 = _batch_slots(0)
        filter_multiply(slots0, _stage2_forward(a_scr, gf_ref, slots0, 0), True, 0)

        def loop(i, carry):
            slots = _batch_slots(i)
            ys = _stage2_forward(a_scr, gf_ref, slots, 1)
            stage2_inverse(_batch_slots(i - 1), jnp.where(i == 1, 0, 1), (i - 1) % 2)
            filter_multiply(slots, ys, False, i % 2)
            return carry

        lax.fori_loop(1, N_BATCHES, loop, 0, unroll=5)
        stage2_inverse(_batch_slots(N_BATCHES - 1), 1, (N_BATCHES - 1) % 2)

    @pl.when(t == _T_S1)
    def _():
        def sub(sb, carry):
            vs = short_conv(sb, 0)
            for i in range(HY_SUB):
                stage1(sb * HY_SUB + i, vs[i].astype(BF16))
            return carry

        lax.fori_loop(0, N_SUB, sub, 0)

    @pl.when((t == _T_F0) | (t == _T_F1))
    def _():
        spectral_phase()

    @pl.when(t == _T_M)
    def _():
        def sub(sb, carry):
            gates = short_conv(sb, 4)
            for q in range(0, HY_SUB, MID_GROUP):
                ys = [inv_stage1(sb * HY_SUB + q + i) for i in range(MID_GROUP)]
                us = [(gates[q + i] * ys[i]).astype(BF16) for i in range(MID_GROUP)]
                for i in range(MID_GROUP):
                    stage1(sb * HY_SUB + q + i, us[i])
            return carry

        lax.fori_loop(0, N_SUB, sub, 0)

    @pl.when(t == _T_E)
    def _():
        def sub(sb, carry):
            gates = short_conv(sb, 8)
            for i in range(HY_SUB):
                n2 = sb * HY_SUB + i
                res = (gates[i] * inv_stage1(n2) * _pair(pz_ref, n2)).astype(BF16)
                o_ref[0, n2] = res[:, :LANES]
                o_ref[1, n2] = res[:, LANES:]
            return carry

        lax.fori_loop(0, N_SUB, sub, 0)


def _hyena(p5, conv_w, conv_b, kspec, f1, f1i, gf, gb):
    grid = (N_CBLK, BATCH // 2, _T_END)
    cw = jnp.concatenate(
        [jnp.concatenate([conv_w[:, k * HYENA_WIDTH:(k + 1) * HYENA_WIDTH],
                          conv_b[:, k * HYENA_WIDTH:(k + 1) * HYENA_WIDTH]], axis=0) for k in range(3)], axis=0)
    seq_block = (2, None, N_TILES, TILE_ROWS, LANES)

    def conv_in_map(c, b, t):
        return (b, jnp.where(t < _T_M, 0, jnp.where(t < _T_E, 1, 2)) * N_CBLK + c, 0, 0, 0)

    def z_map(c, b, t):
        flat = c * (BATCH // 2) + b
        sel = jnp.where(t >= _T_M, flat, jnp.maximum(flat - 1, 0))
        return (sel % (BATCH // 2), 3 * N_CBLK + sel // (BATCH // 2), 0, 0, 0)

    const = lambda shape: pl.BlockSpec(shape, lambda c, b, t: (0,) * len(shape), pipeline_mode=pl.Buffered(1))
    in_specs = [
        pl.BlockSpec(seq_block, conv_in_map),
        pl.BlockSpec(seq_block, z_map),
        pl.BlockSpec((12, LANES), lambda c, b, t: (0, c)),
        pl.BlockSpec((None, None, N_SLOTS, 256, LANES), lambda c, b, t: (jnp.where(t >= _T_M, 1, 0), c, 0, 0, 0)),
        const((N_TILES, 128, TILE_ROWS)), const((N_TILES, TILE_ROWS, 128)),
        const((2, 256, 256)), const((2, 256, 256)),
    ]
    return pl.pallas_call(
        _hyena_kernel,
        grid=grid,
        in_specs=in_specs,
        out_specs=pl.BlockSpec(seq_block, lambda c, b, t: (b, c, 0, 0, 0)),
        out_shape=jax.ShapeDtypeStruct((BATCH, N_CBLK, N_TILES, TILE_ROWS, LANES), BF16),
        scratch_shapes=[pltpu.VMEM((2, N_TILES * A_STRIDE, LANES), jnp.uint32),
                        pltpu.VMEM((2, SLOT_BATCH, 256, 2 * LANES), BF16)],
        compiler_params=pltpu.CompilerParams(
            dimension_semantics=("arbitrary", "arbitrary", "arbitrary"), vmem_limit_bytes=VMEM_LIMIT),
        name="hyena",
    )(p5, p5, cw, kspec, f1, f1i, gf, gb)


POOL_HALO = 8


def _out_kernel(yh_ref, up_ref, upw_ref, upn_ref, zp_ref, x_ref, pw_ref, ps_ref, gh_ref, gp_ref, w_ref, gpost_ref,
                o_ref, r_scr, yp_scr, prev_scr):
    @pl.when(_first_step())
    def _():
        prev_scr[...] = jnp.zeros_like(prev_scr)

    g = pl.program_id(1)
    last_step = N_TILES // OUT_TILES - 1
    row = lax.broadcasted_iota(jnp.int32, (TILE_ROWS, LANES), 0)

    def pool_tile(ci, l):
        if l < 0:
            wrapped = _shift_down(upw_ref[ci, POOL_HALO + l].astype(F32))
            return jnp.where(g == 0, wrapped, prev_scr[ci, POOL_HALO + l].astype(F32))
        if l >= OUT_TILES:
            t = upn_ref[ci, l - OUT_TILES].astype(F32)
            return jnp.where(g == last_step, _shift_up(t), t)
        return up_ref[ci, l].astype(F32)

    n_sub = OUT_TILES // SUB_TILES
    window_sums = [None] * len(POOL_WINDOWS)

    def pool_group(sb, ci):
        w = POOL_WINDOWS[ci]
        lo, hi = w // 2, w - 1 - w // 2
        if sb == 0:
            window_sum = pool_tile(ci, -lo)
            for d in range(-lo + 1, hi + 1):
                window_sum = window_sum + pool_tile(ci, d)
        else:
            window_sum = window_sums[ci]
        pooled = []
        for i in range(SUB_TILES):
            l = sb * SUB_TILES + i
            if l > 0:
                window_sum = window_sum + pool_tile(ci, l + hi) - pool_tile(ci, l - 1 - lo)
            inv_cnt = 1.0 / w
            if l < lo:
                inv_cnt = jnp.where((g == 0) & (row == 0), 1.0 / (hi + l + 1), inv_cnt)
            elif l > OUT_TILES - 1 - hi:
                inv_cnt = jnp.where((g == last_step) & (row == TILE_ROWS - 1),
                                    1.0 / (lo + OUT_TILES - l), inv_cnt)
            pooled.append((window_sum * inv_cnt - pool_tile(ci, l)).astype(BF16))
        window_sums[ci] = window_sum
        y = jnp.dot(jnp.concatenate(pooled, axis=0), pw_ref[ci].astype(BF16), preferred_element_type=F32)
        y = y * ps_ref[:, ci * LANES:(ci + 1) * LANES]
        gate = jnp.concatenate([zp_ref[ci, sb * SUB_TILES + i] for i in range(SUB_TILES)], axis=0).astype(F32)
        yp_scr[sb % 2, ci] = y * gate

    def rms(y, gain_ref):
        ms = jnp.mean(y * y, axis=-1, keepdims=True)
        return (y * lax.rsqrt(ms + EPS) * gain_ref[...]).astype(BF16)

    for ci in range(N_CBLK):
        pool_group(0, ci)
    ncol = D_MODEL // N_CBLK
    for sb in range(n_sub):
        j0 = sb * SUB_TILES
        yh = jnp.concatenate(
            [jnp.concatenate([yh_ref[cb, j] for cb in range(N_CBLK)], axis=-1) for j in range(j0, j0 + SUB_TILES)],
            axis=0).astype(F32)
        yp = jnp.concatenate([yp_scr[sb % 2, ci] for ci in range(N_CBLK)], axis=-1)
        yc = jnp.concatenate([rms(yh, gh_ref), rms(yp, gp_ref)], axis=-1)
        chunks = []
        for c in range(N_CBLK):
            chunks.append(jnp.dot(yc, w_ref[:, c * ncol:(c + 1) * ncol], preferred_element_type=F32))
            if sb + 1 < n_sub:
                pool_group(sb + 1, c)
        out = jnp.concatenate(chunks, axis=-1)
        ms = jnp.mean(out * out, axis=-1, keepdims=True)
        out = out * lax.rsqrt(ms + EPS) * gpost_ref[...]
        for j in range(SUB_TILES):
            for k in range(D_MODEL // LANES):
                r_scr[sb % 2, k, pl.ds(j, TILE_ROWS, stride=SUB_TILES), :] = out[j * TILE_ROWS:(j + 1) * TILE_ROWS,
                                                                             k * LANES:(k + 1) * LANES]
        r = jnp.concatenate([r_scr[sb % 2, k] for k in range(D_MODEL // LANES)], axis=-1)
        o_ref[:, j0:j0 + SUB_TILES, :] = (x_ref[:, j0:j0 + SUB_TILES, :]
                                          + r.reshape(TILE_ROWS, SUB_TILES, D_MODEL))
    for ci in range(N_CBLK):
        prev_scr[ci] = up_ref[ci, OUT_TILES - POOL_HALO:OUT_TILES]


def _out_proj(yh, p5, x4, pool_w, pool_scale, norm_h_g, norm_p_g, w_out, post_g):
    n_steps = N_TILES // OUT_TILES
    halo_blocks = N_TILES // POOL_HALO
    per_step = OUT_TILES // POOL_HALO
    u_group, z_group = 4, 5
    y_spec = pl.BlockSpec((None, N_CBLK, OUT_TILES, TILE_ROWS, LANES), lambda b, g: (b, 0, g, 0, 0))
    x_spec = pl.BlockSpec((None, TILE_ROWS, OUT_TILES, D_MODEL), lambda b, g: (b, 0, g, 0))
    full = lambda shape: pl.BlockSpec(shape, lambda b, g: (0,) * len(shape))
    return pl.pallas_call(
        _out_kernel,
        grid=(BATCH, n_steps),
        in_specs=[
            y_spec,
            pl.BlockSpec((None, N_CBLK, OUT_TILES, TILE_ROWS, LANES), lambda b, g: (b, u_group, g, 0, 0)),
            pl.BlockSpec((None, N_CBLK, POOL_HALO, TILE_ROWS, LANES),
                         lambda b, g: (b, u_group, halo_blocks - 1, 0, 0)),
            pl.BlockSpec((None, N_CBLK, POOL_HALO, TILE_ROWS, LANES),
                         lambda b, g: (b, u_group, (g * per_step + per_step) % halo_blocks, 0, 0)),
            pl.BlockSpec((None, N_CBLK, OUT_TILES, TILE_ROWS, LANES), lambda b, g: (b, z_group, g, 0, 0)),
            x_spec,
            full((N_CBLK, LANES, LANES)), full((1, POOL_WIDTH)),
            full((1, HYENA_WIDTH)), full((1, POOL_WIDTH)),
            pl.BlockSpec((D_MODEL, D_MODEL), lambda b, g: (0, 0), pipeline_mode=pl.Buffered(1)), full((1, D_MODEL)),
        ],
        out_specs=x_spec,
        out_shape=jax.ShapeDtypeStruct((BATCH, TILE_ROWS, N_TILES, D_MODEL), F32),
        scratch_shapes=[pltpu.VMEM((2, D_MODEL // LANES, TILE_ROWS * SUB_TILES, LANES), F32),
                        pltpu.VMEM((2, N_CBLK, TILE_ROWS * SUB_TILES, LANES), F32),
                        pltpu.VMEM((N_CBLK, POOL_HALO, TILE_ROWS, LANES), BF16)],
        compiler_params=pltpu.CompilerParams(
            dimension_semantics=("arbitrary", "arbitrary"), vmem_limit_bytes=OUT_VMEM_LIMIT),
        name="out_proj",
    )(yh, p5, p5, p5, p5, x4, pool_w, pool_scale, norm_h_g, norm_p_g, w_out, post_g)


def kernel(x, pre_norm_g, w_in, conv_w, conv_b, filt_w1, filt_b1, filt_w2, filt_b2, filt_w3, filt_b3,
           filt_freq, filt_w_out, hyena_d, pool_w, pool_scale, norm_h_g, norm_p_g, w_out, post_norm_g):
    assert x.shape == (BATCH, SEQ, D_MODEL) and pre_norm_g.shape[0] == 1
    f1, f1i, gf, gb, f1f = (jnp.asarray(m, F32).astype(BF16) for m in (_F1, _F1I, _GF, _GB, _F1F))

    x4 = x.reshape(BATCH, TILE_ROWS, N_TILES, D_MODEL)
    p5, taps = _in_proj(x4, pre_norm_g, w_in[0], jnp.asarray(_ZFEAT), filt_w1[0], filt_b1[0], filt_w2[0], filt_b2[0],
                        filt_w3[0], filt_b3[0], filt_freq[0], filt_w_out[0], jnp.asarray(_ABS_DELTAS))
    kspec = _filter_spec(taps, hyena_d[0], f1f, gf)

    yh = _hyena(p5, conv_w[0], conv_b, kspec, f1, f1i, gf, gb)
    out4 = _out_proj(yh, p5, x4, pool_w[0], pool_scale, norm_h_g, norm_p_g,
                     w_out[0].astype(BF16), post_norm_g)
    return out4.reshape(BATCH, SEQ, D_MODEL)
```

```python
import math

import numpy as np
import jax
import jax.numpy as jnp
from jax import lax
from jax.experimental import pallas as pl
from jax.experimental.pallas import tpu as pltpu

F32 = jnp.float32
BF16 = jnp.bfloat16

D_MODEL = 1024
BATCH = 4
SEQ = 8192
HYENA_WIDTH = 512
POOL_WIDTH = 512
POOL_WINDOWS = (2, 4, 8, 16)
FILTER_EMB = 33
FILTER_BANDS = 16
FILTER_HIDDEN = 64
PROJ_WIDTH = 3072
EPS = 1e-6

LANES = 128
N_FFT = 2 * SEQ
N_TILES = 128
TILE_ROWS = SEQ // N_TILES
N_SLOTS = 64
HY_SUB = 64
N_SUB = N_TILES // HY_SUB
A_STRIDE = 72
SLOT_BATCH = 4
N_CBLK = HYENA_WIDTH // LANES
PROJ_BLOCKS = PROJ_WIDTH // LANES
IN_TILES = 16
OUT_TILES = 32
SUB_TILES = 8
VMEM_LIMIT = 60 * 1024 * 1024
OUT_VMEM_LIMIT = 63 * 1024 * 1024


def _dft_tables():
    n1 = np.arange(TILE_ROWS)
    n2 = np.arange(N_TILES)
    s = np.arange(N_SLOTS)
    ph = 2 * np.pi * (n2[:, None, None] * s[None, :, None] / N_FFT
                      + n1[None, None, :] * s[None, :, None] / 128.0)
    f1 = np.zeros((N_TILES, 128, TILE_ROWS))
    f1[:, :64, :] = np.cos(ph)
    f1[:, 64:, :] = -np.sin(ph)
    f1[:, 0, :] = 1.0
    f1[:, 64, :] = (-1.0) ** n1
    php = np.transpose(ph, (0, 2, 1))
    f1i = np.zeros((N_TILES, TILE_ROWS, 128))
    f1i[:, :, :64] = 2 * np.cos(php) / N_FFT
    f1i[:, :, 64:] = -2 * np.sin(php) / N_FFT
    f1i[:, :, 0] = 1.0 / N_FFT
    f1i[:, :, 64] = ((-1.0) ** n1)[None, :] / N_FFT
    k2 = np.arange(128)
    th = 2 * np.pi * np.outer(k2, n2) / 128.0
    c, sn = np.cos(th), np.sin(th)
    g = np.block([[c, sn], [-sn, c]])
    gi = np.block([[c, -sn], [sn, c]])
    kk = np.arange(64)
    tha = 2 * np.pi * np.outer(kk, n2) / 128.0
    thb = 2 * np.pi * np.outer(64 + 128 * kk, n2) / N_FFT
    g0 = np.zeros((256, 256))
    g0[0:64, 0:128] = np.cos(tha)
    g0[64:128, 128:256] = np.cos(thb)
    g0[128:192, 0:128] = -np.sin(tha)
    g0[128, 0:128] = (-1.0) ** n2
    g0[192:256, 128:256] = -np.sin(thb)
    g0i = np.zeros((256, 256))
    g0i[0:128, 0:64] = 2 * np.cos(tha.T)
    g0i[0:128, 0] = 1.0
    g0i[0:128, 128:192] = -2 * np.sin(tha.T)
    g0i[0:128, 128] = (-1.0) ** n2
    g0i[128:256, 64:128] = 2 * np.cos(thb.T)
    g0i[128:256, 192:256] = -2 * np.sin(thb.T)
    q = np.arange(256)
    perm = np.where(q % 16 < 8, 8 * (q // 16) + q % 16, 128 + 8 * (q // 16) + q % 16 - 8)
    gf = np.stack([g0[perm, :], g[perm, :]])
    gb = np.stack([g0i[:, perm], gi[:, perm]])
    n1f = np.arange(128)
    phf = 2 * np.pi * (n2[:, None, None] * s[None, :, None] / N_FFT
                       + n1f[None, None, :] * s[None, :, None] / 128.0)
    full = np.zeros((N_TILES, 128, 128))
    full[:, :64, :] = np.cos(phf)
    full[:, 64:, :] = -np.sin(phf)
    full[:, 0, :] = 1.0
    full[:, 64, :] = (-1.0) ** n1f
    rev = full[:, :, 127:63:-1].copy()
    rev[0, :, 1:] = full[0, :, 127:64:-1]
    rev[0, :, 0] = 0.0
    f1f = np.concatenate([full[:, :, :64], rev], axis=2)
    pair64 = np.arange(128) % 2 * 64 + np.arange(128) // 2
    pair128 = np.arange(256) % 2 * 128 + np.arange(256) // 2
    f1, f1f = f1[:, pair64, :], f1f[:, pair64, :]
    f1i = f1i[:, :, pair64]
    gf = gf[:, :, pair128]
    gb = gb[:, pair128, :]
    return f1, f1i, gf, gb, f1f


def _filter_features():
    pos = np.arange(SEQ, dtype=np.float64)
    t = pos / (SEQ - 1)
    ang = 2.0 * math.pi * pos / SEQ
    bands = np.linspace(1e-4, FILTER_BANDS - 1, FILTER_BANDS)
    z = np.concatenate([t[:, None], np.cos(bands[None, :] * ang[:, None]),
                        -np.sin(bands[None, :] * ang[:, None])], axis=-1)
    z = z.reshape(TILE_ROWS, N_TILES, FILTER_EMB).transpose(1, 0, 2).reshape(SEQ, FILTER_EMB)
    zp = np.zeros((FILTER_HIDDEN, SEQ))
    zp[:FILTER_EMB, :] = z.T
    max_decay = math.log(1e-2) / 0.3
    min_decay = math.log(1e-2) / 1.5
    deltas = np.abs(np.linspace(min_decay, max_decay, HYENA_WIDTH))
    return zp.astype(np.float32), deltas.astype(np.float32)[None, :]


_F1, _F1I, _GF, _GB, _F1F = _dft_tables()
_ZFEAT, _ABS_DELTAS = _filter_features()


def _shift_down(x):
    rows = lax.broadcasted_iota(jnp.int32, x.shape, 0)
    return jnp.where(rows == 0, 0.0, pltpu.roll(x, 1, axis=0))


def _shift_up(x):
    rows = lax.broadcasted_iota(jnp.int32, x.shape, 0)
    return jnp.where(rows == x.shape[0] - 1, 0.0, pltpu.roll(x, x.shape[0] - 1, axis=0))


def _pair(ref, i):
    return jnp.concatenate([ref[0, i], ref[1, i]], axis=-1).astype(F32)


def _dup(x):
    return jnp.concatenate([x, x], axis=-1)


def _silu(z):
    hz = 0.5 * z
    return hz * (1.0 + jnp.tanh(hz))


def _tile_rows(n2):
    return pl.ds(pl.multiple_of(n2 * A_STRIDE, 8), N_SLOTS)


def _slot_rows(s):
    return pl.ds(s, N_TILES, stride=A_STRIDE)


def _stage_load(a_ref, rows):
    words = jnp.concatenate([a_ref[0, rows, :], a_ref[1, rows, :]], axis=-1)
    return pltpu.bitcast(words, BF16)


def _stage_store(a_ref, rows, val):
    words = pltpu.bitcast(val.astype(BF16), jnp.uint32)
    a_ref[0, rows, :] = words[:, :LANES]
    a_ref[1, rows, :] = words[:, LANES:]


N_BATCHES = N_SLOTS // SLOT_BATCH
SPEC_BLOCKS = 16


def _batch_slots(i):
    return [i * SLOT_BATCH + j for j in range(SLOT_BATCH)]


def _stage2_forward(a_ref, gf_ref, slots, sel0):
    xs = [_stage_load(a_ref, _slot_rows(s)) for s in slots]
    return [jnp.dot(gf_ref[sel0] if j == 0 else gf_ref[1], x, preferred_element_type=F32)
            for j, x in enumerate(xs)]


def _re_im_blocks(y):
    return [(y[16 * i:16 * i + 8], y[16 * i + 8:16 * i + 16]) for i in range(SPEC_BLOCKS)]


_GATE_GROUPS = (3, 5)
N_IN_STEPS = BATCH * (N_TILES // IN_TILES)
FILT_TILES = N_TILES // N_IN_STEPS


def _split_bf16(x):
    hi = x.astype(BF16)
    return hi, (x - hi.astype(F32)).astype(BF16)


def _dot_split(a, b):
    a_hi, a_lo = _split_bf16(a)
    b_hi, b_lo = _split_bf16(b)
    dot = lambda u, v: jnp.dot(u, v, preferred_element_type=F32)
    return dot(a_hi, b_hi) + dot(a_lo, b_hi) + dot(a_hi, b_lo)


def _first_step():
    return (pl.program_id(0) == 0) & (pl.program_id(1) == 0)


def _cast_weights_once(w_ref, w_scr, ncol):
    @pl.when(_first_step())
    def _():
        for c in range(w_ref.shape[1] // ncol):
            w_scr[:, c * ncol:(c + 1) * ncol] = w_ref[:, c * ncol:(c + 1) * ncol].astype(BF16)


class _FilterTaps:
    def __init__(self, z_ref, p_ref, wp_scr, dl_ref, first_tile, o_ref):
        self.p_ref, self.wp_scr, self.dl_ref, self.first_tile, self.o_ref = p_ref, wp_scr, dl_ref, first_tile, o_ref
        self.cols_blk = p_ref[3].T[:FILTER_HIDDEN]
        self.h = z_ref[...]
        self.n_stages = 5

    def stage(self, k):
        hid = FILTER_HIDDEN
        if k < 3:
            w_t = self.p_ref[k].T[:hid, :hid]
            freq = self.cols_blk[:, 3:4]
            self.h = jnp.sin(freq * (_dot_split(w_t, self.h) + self.cols_blk[:, k:k + 1]))
            return
        if k == 3:
            h_hi = self.h.astype(BF16).astype(F32)
            stack = jnp.concatenate([h_hi, self.h - h_hi, h_hi, jnp.zeros_like(h_hi)], axis=0)
            self.lhs = stack.T.astype(BF16)
            rows = FILT_TILES * TILE_ROWS
            r = lax.broadcasted_iota(jnp.int32, (rows, HYENA_WIDTH), 0)
            pos = 128 * (r % TILE_ROWS) + self.first_tile + r // TILE_ROWS
            t = pos.astype(F32) / float(SEQ - 1)
            self.decay = jnp.exp(-t * self.dl_ref[...])
        for od in ((0, 1) if k == 3 else (2, 3)):
            cols = slice(od * HYENA_WIDTH, (od + 1) * HYENA_WIDTH)
            taps = jnp.dot(self.lhs, self.wp_scr[:, cols], preferred_element_type=F32) * self.decay
            taps = taps.astype(BF16)
            for cb in range(N_CBLK):
                for i in range(FILT_TILES):
                    self.o_ref[od * N_CBLK + cb, i] = taps[i * TILE_ROWS:(i + 1) * TILE_ROWS,
                                                           cb * LANES:(cb + 1) * LANES]


def _in_proj_kernel(x_ref, g_ref, w32_ref, z_ref, p_ref, wp32_ref, dl_ref, wout32_ref,
                    o_ref, taps_ref, wout_ref, h_scr, w_ref, wp_scr):
    _cast_weights_once(w32_ref, w_ref, 512)
    _cast_weights_once(wout32_ref, wout_ref, 512)

    @pl.when(_first_step())
    def _():
        wp_hi, wp_lo = _split_bf16(wp32_ref[...])
        wp_scr[...] = jnp.concatenate([wp_hi, wp_hi, wp_lo, jnp.zeros_like(wp_lo)], axis=0)

    step = pl.program_id(0) * (N_TILES // IN_TILES) + pl.program_id(1)
    filt = _FilterTaps(z_ref, p_ref, wp_scr, dl_ref, step * FILT_TILES, taps_ref)
    n_dots = 0

    ncol = 512
    for sb in range(IN_TILES // SUB_TILES):
        j0 = sb * SUB_TILES
        x = x_ref[:, j0:j0 + SUB_TILES, :].reshape(TILE_ROWS * SUB_TILES, D_MODEL)
        ms = jnp.mean(x * x, axis=-1, keepdims=True)
        hn = x * lax.rsqrt(ms + EPS) * g_ref[...]
        for k in range(D_MODEL // LANES):
            h_scr[sb, k] = hn[:, k * LANES:(k + 1) * LANES]
        h = jnp.concatenate(
            [jnp.concatenate([h_scr[sb, k, pl.ds(j, TILE_ROWS, stride=SUB_TILES), :]
                              for k in range(D_MODEL // LANES)], axis=-1).astype(BF16)
             for j in range(SUB_TILES)], axis=0)
        for c in range(PROJ_WIDTH // ncol):
            p = jnp.dot(h, w_ref[:, c * ncol:(c + 1) * ncol], preferred_element_type=F32)
            if (c * ncol) // HYENA_WIDTH in _GATE_GROUPS:
                p = _silu(p)
            p = p.astype(BF16)
            for cb in range(ncol // LANES):
                for j in range(SUB_TILES):
                    o_ref[c * (ncol // LANES) + cb, j0 + j] = p[j * TILE_ROWS:(j + 1) * TILE_ROWS,
                                                                cb * LANES:(cb + 1) * LANES]
            if n_dots % 2 == 0 and n_dots // 2 < filt.n_stages:
                filt.stage(n_dots // 2)
            n_dots += 1


def _in_proj(x4, pre_g, w_in, zfeat_t, w1, b1, w2, b2, w3, b3, freq, w_proj, abs_deltas, w_out):
    steps_per_batch = N_TILES // IN_TILES
    pad2 = lambda m: jnp.pad(m, ((0, LANES - m.shape[0]), (0, LANES - m.shape[1])))
    params = jnp.stack([pad2(w1), pad2(w2), pad2(w3), pad2(jnp.stack([b1, b2, b3, freq], axis=0))])
    full = lambda shape: pl.BlockSpec(shape, lambda b, g: (0,) * len(shape))
    once = lambda shape: pl.BlockSpec(shape, lambda b, g: (0,) * len(shape), pipeline_mode=pl.Buffered(1))
    return pl.pallas_call(
        _in_proj_kernel,
        grid=(BATCH, steps_per_batch),
        in_specs=[
            pl.BlockSpec((None, TILE_ROWS, IN_TILES, D_MODEL), lambda b, g: (b, 0, g, 0)),
            full((1, D_MODEL)),
            once((D_MODEL, PROJ_WIDTH)),
            pl.BlockSpec((FILTER_HIDDEN, FILT_TILES * TILE_ROWS), lambda b, g: (0, b * steps_per_batch + g)),
            full((4, LANES, LANES)),
            once((FILTER_HIDDEN, 4 * HYENA_WIDTH)),
            full((1, HYENA_WIDTH)),
            once((D_MODEL, D_MODEL)),
        ],
        out_specs=[
            pl.BlockSpec((None, PROJ_BLOCKS, IN_TILES, TILE_ROWS, LANES), lambda b, g: (b, 0, g, 0, 0)),
            pl.BlockSpec((4 * N_CBLK, FILT_TILES, TILE_ROWS, LANES), lambda b, g: (0, b * steps_per_batch + g, 0, 0)),
            full((D_MODEL, D_MODEL)),
        ],
        out_shape=[jax.ShapeDtypeStruct((BATCH, PROJ_BLOCKS, N_TILES, TILE_ROWS, LANES), BF16),
                   jax.ShapeDtypeStruct((4 * N_CBLK, N_TILES, TILE_ROWS, LANES), BF16),
                   jax.ShapeDtypeStruct((D_MODEL, D_MODEL), BF16)],
        scratch_shapes=[pltpu.VMEM((IN_TILES // SUB_TILES, D_MODEL // LANES, TILE_ROWS * SUB_TILES, LANES), F32),
                        pltpu.VMEM((D_MODEL, PROJ_WIDTH), BF16),
                        pltpu.VMEM((2 * LANES, 4 * HYENA_WIDTH), BF16)],
        compiler_params=pltpu.CompilerParams(
            dimension_semantics=("arbitrary", "arbitrary"), vmem_limit_bytes=VMEM_LIMIT),
        name="in_proj",
    )(x4, pre_g, w_in, zfeat_t, params, w_proj, abs_deltas, w_out)


def _filter_spec_kernel(fa_ref, fb_ref, ba_ref, bb_ref, d_ref, f1f_ref, gf_ref, k_ref, a_scr):
    def stage1(sb, carry):
        for i in range(HY_SUB):
            n2 = sb * HY_SUB + i
            nb = (N_TILES - n2) % N_TILES
            fwd = jnp.concatenate([fa_ref[n2], fb_ref[n2]], axis=-1)
            bwd = jnp.concatenate([ba_ref[nb], bb_ref[nb]], axis=-1)
            taps = jnp.concatenate([fwd, bwd], axis=0)
            _stage_store(a_scr, _tile_rows(n2), jnp.dot(f1f_ref[n2], taps, preferred_element_type=F32))
        return carry

    lax.fori_loop(0, N_SUB, stage1, 0)

    d = d_ref[pl.ds(pl.program_id(0), 1), :]

    def batch(slots, first):
        ys = _stage2_forward(a_scr, gf_ref, slots, 0 if first else 1)
        for j, (s, y) in enumerate(zip(slots, ys)):
            blocks = []
            for i, (yr, yi) in enumerate(_re_im_blocks(y)):
                yr = yr + d
                if first and j == 0 and i == 0:
                    rows = lax.broadcasted_iota(jnp.int32, yi.shape, 0)
                    yi = yi + jnp.where(rows == 0, d, 0.0)
                blocks += [yr, yi]
            spec = jnp.concatenate(blocks, axis=0).astype(BF16)
            k_ref[0, s] = spec[:, :LANES]
            k_ref[1, s] = spec[:, LANES:]

    batch(_batch_slots(0), True)

    def loop(i, carry):
        batch(_batch_slots(i), False)
        return carry

    lax.fori_loop(1, N_BATCHES, loop, 0, unroll=3)


def _filter_spec(taps, hyena_d, f1f, gf):
    grid = (2, N_CBLK // 2)
    const = lambda shape: pl.BlockSpec(shape, lambda o, c: (0,) * len(shape), pipeline_mode=pl.Buffered(1))

    def taps_spec(direction, k):
        return pl.BlockSpec((None, N_TILES, TILE_ROWS, LANES),
                            lambda o, c: ((2 * o + direction) * N_CBLK + 2 * c + k, 0, 0, 0))

    return pl.pallas_call(
        _filter_spec_kernel,
        grid=grid,
        in_specs=[taps_spec(0, 0), taps_spec(0, 1), taps_spec(1, 0), taps_spec(1, 1),
                  pl.BlockSpec((2, 2 * LANES), lambda o, c: (0, c)),
                  const((N_TILES, 128, 128)), const((2, 256, 256))],
        out_specs=pl.BlockSpec((None, 2, N_SLOTS, 256, LANES), lambda o, c: (o, c, 0, 0, 0)),
        out_shape=jax.ShapeDtypeStruct((2, N_CBLK, N_SLOTS, 256, LANES), BF16),
        scratch_shapes=[pltpu.VMEM((2, N_TILES * A_STRIDE, LANES), jnp.uint32)],
        compiler_params=pltpu.CompilerParams(
            dimension_semantics=("arbitrary", "arbitrary"), vmem_limit_bytes=VMEM_LIMIT),
        name="filter_spec",
    )(taps, taps, taps, taps, hyena_d, f1f, gf)


_T_S1 = 0
_T_F0 = 1
_T_M = 2
_T_F1 = 3
_T_E = 4
_T_END = 5
MID_GROUP = 8


def _hyena_kernel(cin_ref, pz_ref, cw_ref, cb_ref, k_ref, f1_ref, f1i_ref, gf_ref, gb_ref, o_ref, a_scr, z_scr):
    t = pl.program_id(2)

    def stage1(n2, u_bf):
        _stage_store(a_scr, _tile_rows(n2), jnp.dot(f1_ref[n2], u_bf, preferred_element_type=F32))

    def inv_stage1(n2):
        return jnp.dot(f1i_ref[n2], _stage_load(a_scr, _tile_rows(n2)), preferred_element_type=F32)

    def short_conv(sb):
        base = sb * HY_SUB
        first = _pair(cin_ref, (base + N_TILES - 1) % N_TILES)
        first = jnp.where(sb == 0, _shift_down(first), first)
        last = _pair(cin_ref, (base + HY_SUB) % N_TILES)
        last = jnp.where(sb == N_SUB - 1, _shift_up(last), last)
        tiles = [first] + [_pair(cin_ref, base + i) for i in range(HY_SUB)] + [last]
        w = cw_ref[...]
        w0, w1, w2, b = _dup(w[0:1]), _dup(w[1:2]), _dup(w[2:3]), _dup(cb_ref[...])
        return [tiles[i] * w0 + tiles[i + 1] * w1 + tiles[i + 2] * w2 + b for i in range(HY_SUB)]

    def filter_multiply(slots, ys, first, buf):
        for j, (s, y) in enumerate(zip(slots, ys)):
            blocks = []
            for i, (yr, yi) in enumerate(_re_im_blocks(y)):
                kblk = k_ref[s, 16 * i:16 * i + 16, :].astype(F32)
                kr, ki = kblk[:8], kblk[8:]
                if first and j == 0 and i == 0:
                    rows = lax.broadcasted_iota(jnp.int32, kr.shape, 0)
                    ka, kb, kd = kr, jnp.where(rows == 0, 0.0, ki), jnp.where(rows == 0, ki, kr)
                else:
                    ka, kb, kd = kr, ki, kr
                ka, kb, kd = _dup(ka), _dup(kb), _dup(kd)
                blocks += [yr * ka - yi * kb, yr * kb + yi * kd]
            z_scr[buf, j] = jnp.concatenate(blocks, axis=0).astype(BF16)

    def stage2_inverse(slots, first_sel, buf):
        bms = [jnp.dot(gb_ref[first_sel] if j == 0 else gb_ref[1], z_scr[buf, j], preferred_element_type=F32)
               for j in range(SLOT_BATCH)]
        for s, bm in zip(slots, bms):
            _stage_store(a_scr, _slot_rows(s), bm)

    def spectral_phase():
        slots0 = _batch_slots(0)
        filter_multiply(slots0, _stage2_forward(a_scr, gf_ref, slots0, 0), True, 0)

        def loop(i, carry):
            slots = _batch_slots(i)
            ys = _stage2_forward(a_scr, gf_ref, slots, 1)
            stage2_inverse(_batch_slots(i - 1), jnp.where(i == 1, 0, 1), (i - 1) % 2)
            filter_multiply(slots, ys, False, i % 2)
            return carry

        lax.fori_loop(1, N_BATCHES, loop, 0, unroll=5)
        stage2_inverse(_batch_slots(N_BATCHES - 1), 1, (N_BATCHES - 1) % 2)

    @pl.when(t == _T_S1)
    def _():
        def sub(sb, carry):
            vs = short_conv(sb)
            for i in range(HY_SUB):
                stage1(sb * HY_SUB + i, vs[i].astype(BF16))
            return carry

        lax.fori_loop(0, N_SUB, sub, 0)

    @pl.when((t == _T_F0) | (t == _T_F1))
    def _():
        spectral_phase()

    @pl.when(t == _T_M)
    def _():
        def sub(sb, carry):
            gates = short_conv(sb)
            for q in range(0, HY_SUB, MID_GROUP):
                ys = [inv_stage1(sb * HY_SUB + q + i) for i in range(MID_GROUP)]
                us = [(gates[q + i] * ys[i]).astype(BF16) for i in range(MID_GROUP)]
                for i in range(MID_GROUP):
                    stage1(sb * HY_SUB + q + i, us[i])
            return carry

        lax.fori_loop(0, N_SUB, sub, 0)

    @pl.when(t == _T_E)
    def _():
        def sub(sb, carry):
            gates = short_conv(sb)
            for i in range(HY_SUB):
                n2 = sb * HY_SUB + i
                res = (gates[i] * inv_stage1(n2) * _pair(pz_ref, n2)).astype(BF16)
                o_ref[0, n2] = res[:, :LANES]
                o_ref[1, n2] = res[:, LANES:]
            return carry

        lax.fori_loop(0, N_SUB, sub, 0)


def _hyena(p5, conv_w, conv_b, kspec, f1, f1i, gf, gb):
    grid = (N_CBLK, BATCH // 2, _T_END)
    seq_block = (2, None, N_TILES, TILE_ROWS, LANES)

    def conv_col(c, t):
        return jnp.where(t < _T_M, 0, jnp.where(t < _T_E, 1, 2)) * N_CBLK + c

    def conv_in_map(c, b, t):
        return (b, conv_col(c, t), 0, 0, 0)

    def z_map(c, b, t):
        flat = c * (BATCH // 2) + b
        sel = jnp.where(t >= _T_M, flat, jnp.maximum(flat - 1, 0))
        return (sel % (BATCH // 2), 3 * N_CBLK + sel // (BATCH // 2), 0, 0, 0)

    const = lambda shape: pl.BlockSpec(shape, lambda c, b, t: (0,) * len(shape), pipeline_mode=pl.Buffered(1))
    in_specs = [
        pl.BlockSpec(seq_block, conv_in_map),
        pl.BlockSpec(seq_block, z_map),
        pl.BlockSpec((3, LANES), lambda c, b, t: (0, conv_col(c, t))),
        pl.BlockSpec((1, LANES), lambda c, b, t: (0, conv_col(c, t))),
        pl.BlockSpec((None, None, N_SLOTS, 256, LANES), lambda c, b, t: (jnp.where(t >= _T_M, 1, 0), c, 0, 0, 0)),
        const((N_TILES, 128, TILE_ROWS)), const((N_TILES, TILE_ROWS, 128)),
        const((2, 256, 256)), const((2, 256, 256)),
    ]
    return pl.pallas_call(
        _hyena_kernel,
        grid=grid,
        in_specs=in_specs,
        out_specs=pl.BlockSpec(seq_block, lambda c, b, t: (b, c, 0, 0, 0)),
        out_shape=jax.ShapeDtypeStruct((BATCH, N_CBLK, N_TILES, TILE_ROWS, LANES), BF16),
        scratch_shapes=[pltpu.VMEM((2, N_TILES * A_STRIDE, LANES), jnp.uint32),
                        pltpu.VMEM((2, SLOT_BATCH, 256, 2 * LANES), BF16)],
        compiler_params=pltpu.CompilerParams(
            dimension_semantics=("arbitrary", "arbitrary", "arbitrary"), vmem_limit_bytes=VMEM_LIMIT),
        name="hyena",
    )(p5, p5, conv_w, conv_b, kspec, f1, f1i, gf, gb)


POOL_HALO = 8


def _out_kernel(yh_ref, up_ref, upw_ref, upn_ref, zp_ref, x_ref, pw_ref, ps_ref, gh_ref, gp_ref, w_ref, gpost_ref,
                o_ref, r_scr, yp_scr, prev_scr):
    @pl.when(_first_step())
    def _():
        prev_scr[...] = jnp.zeros_like(prev_scr)

    g = pl.program_id(1)
    last_step = N_TILES // OUT_TILES - 1
    row = lax.broadcasted_iota(jnp.int32, (TILE_ROWS, LANES), 0)

    def pool_tile(ci, l):
        if l < 0:
            wrapped = _shift_down(upw_ref[ci, POOL_HALO + l].astype(F32))
            return jnp.where(g == 0, wrapped, prev_scr[ci, POOL_HALO + l].astype(F32))
        if l >= OUT_TILES:
            t = upn_ref[ci, l - OUT_TILES].astype(F32)
            return jnp.where(g == last_step, _shift_up(t), t)
        return up_ref[ci, l].astype(F32)

    n_sub = OUT_TILES // SUB_TILES
    window_sums = [None] * len(POOL_WINDOWS)

    def pool_group(sb, ci):
        w = POOL_WINDOWS[ci]
        lo, hi = w // 2, w - 1 - w // 2
        if sb == 0:
            window_sum = pool_tile(ci, -lo)
            for d in range(-lo + 1, hi + 1):
                window_sum = window_sum + pool_tile(ci, d)
        else:
            window_sum = window_sums[ci]
        pooled = []
        for i in range(SUB_TILES):
            l = sb * SUB_TILES + i
            if l > 0:
                window_sum = window_sum + pool_tile(ci, l + hi) - pool_tile(ci, l - 1 - lo)
            inv_cnt = 1.0 / w
            if l < lo:
                inv_cnt = jnp.where((g == 0) & (row == 0), 1.0 / (hi + l + 1), inv_cnt)
            elif l > OUT_TILES - 1 - hi:
                inv_cnt = jnp.where((g == last_step) & (row == TILE_ROWS - 1),
                                    1.0 / (lo + OUT_TILES - l), inv_cnt)
            pooled.append((window_sum * inv_cnt - pool_tile(ci, l)).astype(BF16))
        window_sums[ci] = window_sum
        y = jnp.dot(jnp.concatenate(pooled, axis=0), pw_ref[ci].astype(BF16), preferred_element_type=F32)
        y = y * ps_ref[:, ci * LANES:(ci + 1) * LANES]
        gate = jnp.concatenate([zp_ref[ci, sb * SUB_TILES + i] for i in range(SUB_TILES)], axis=0).astype(F32)
        yp_scr[sb % 2, ci] = y * gate

    def rms(y, gain_ref):
        ms = jnp.mean(y * y, axis=-1, keepdims=True)
        return (y * lax.rsqrt(ms + EPS) * gain_ref[...]).astype(BF16)

    for ci in range(N_CBLK):
        pool_group(0, ci)
    ncol = D_MODEL // N_CBLK
    for sb in range(n_sub):
        j0 = sb * SUB_TILES
        yh = jnp.concatenate(
            [jnp.concatenate([yh_ref[cb, j] for cb in range(N_CBLK)], axis=-1) for j in range(j0, j0 + SUB_TILES)],
            axis=0).astype(F32)
        yp = jnp.concatenate([yp_scr[sb % 2, ci] for ci in range(N_CBLK)], axis=-1)
        yc = jnp.concatenate([rms(yh, gh_ref), rms(yp, gp_ref)], axis=-1)
        chunks = []
        for c in range(N_CBLK):
            chunks.append(jnp.dot(yc, w_ref[:, c * ncol:(c + 1) * ncol], preferred_element_type=F32))
            if sb + 1 < n_sub:
                pool_group(sb + 1, c)
        out = jnp.concatenate(chunks, axis=-1)
        ms = jnp.mean(out * out, axis=-1, keepdims=True)
        out = out * lax.rsqrt(ms + EPS) * gpost_ref[...]
        for j in range(SUB_TILES):
            for k in range(D_MODEL // LANES):
                r_scr[sb % 2, k, pl.ds(j, TILE_ROWS, stride=SUB_TILES), :] = out[j * TILE_ROWS:(j + 1) * TILE_ROWS,
                                                                             k * LANES:(k + 1) * LANES]
        r = jnp.concatenate([r_scr[sb % 2, k] for k in range(D_MODEL // LANES)], axis=-1)
        o_ref[:, j0:j0 + SUB_TILES, :] = (x_ref[:, j0:j0 + SUB_TILES, :]
                                          + r.reshape(TILE_ROWS, SUB_TILES, D_MODEL))
    for ci in range(N_CBLK):
        prev_scr[ci] = up_ref[ci, OUT_TILES - POOL_HALO:OUT_TILES]


def _out_proj(yh, p5, x4, pool_w, pool_scale, norm_h_g, norm_p_g, w_out, post_g):
    n_steps = N_TILES // OUT_TILES
    halo_blocks = N_TILES // POOL_HALO
    per_step = OUT_TILES // POOL_HALO
    u_group, z_group = 4, 5
    y_spec = pl.BlockSpec((None, N_CBLK, OUT_TILES, TILE_ROWS, LANES), lambda b, g: (b, 0, g, 0, 0))
    x_spec = pl.BlockSpec((None, TILE_ROWS, OUT_TILES, D_MODEL), lambda b, g: (b, 0, g, 0))
    full = lambda shape: pl.BlockSpec(shape, lambda b, g: (0,) * len(shape))
    return pl.pallas_call(
        _out_kernel,
        grid=(BATCH, n_steps),
        in_specs=[
            y_spec,
            pl.BlockSpec((None, N_CBLK, OUT_TILES, TILE_ROWS, LANES), lambda b, g: (b, u_group, g, 0, 0)),
            pl.BlockSpec((None, N_CBLK, POOL_HALO, TILE_ROWS, LANES),
                         lambda b, g: (b, u_group, halo_blocks - 1, 0, 0)),
            pl.BlockSpec((None, N_CBLK, POOL_HALO, TILE_ROWS, LANES),
                         lambda b, g: (b, u_group, (g * per_step + per_step) % halo_blocks, 0, 0)),
            pl.BlockSpec((None, N_CBLK, OUT_TILES, TILE_ROWS, LANES), lambda b, g: (b, z_group, g, 0, 0)),
            x_spec,
            full((N_CBLK, LANES, LANES)), full((1, POOL_WIDTH)),
            full((1, HYENA_WIDTH)), full((1, POOL_WIDTH)),
            pl.BlockSpec((D_MODEL, D_MODEL), lambda b, g: (0, 0), pipeline_mode=pl.Buffered(1)), full((1, D_MODEL)),
        ],
        out_specs=x_spec,
        out_shape=jax.ShapeDtypeStruct((BATCH, TILE_ROWS, N_TILES, D_MODEL), F32),
        scratch_shapes=[pltpu.VMEM((2, D_MODEL // LANES, TILE_ROWS * SUB_TILES, LANES), F32),
                        pltpu.VMEM((2, N_CBLK, TILE_ROWS * SUB_TILES, LANES), F32),
                        pltpu.VMEM((N_CBLK, POOL_HALO, TILE_ROWS, LANES), BF16)],
        compiler_params=pltpu.CompilerParams(
            dimension_semantics=("arbitrary", "arbitrary"), vmem_limit_bytes=OUT_VMEM_LIMIT),
        name="out_proj",
    )(yh, p5, p5, p5, p5, x4, pool_w, pool_scale, norm_h_g, norm_p_g, w_out, post_g)


def kernel(x, pre_norm_g, w_in, conv_w, conv_b, filt_w1, filt_b1, filt_w2, filt_b2, filt_w3, filt_b3,
           filt_freq, filt_w_out, hyena_d, pool_w, pool_scale, norm_h_g, norm_p_g, w_out, post_norm_g):
    assert x.shape == (BATCH, SEQ, D_MODEL) and pre_norm_g.shape[0] == 1
    f1, f1i, gf, gb, f1f = (jnp.asarray(m, F32).astype(BF16) for m in (_F1, _F1I, _GF, _GB, _F1F))

    x4 = x.reshape(BATCH, TILE_ROWS, N_TILES, D_MODEL)
    p5, taps, w_out_bf = _in_proj(x4, pre_norm_g, w_in[0], jnp.asarray(_ZFEAT), filt_w1[0], filt_b1[0], filt_w2[0],
                                  filt_b2[0], filt_w3[0], filt_b3[0], filt_freq[0], filt_w_out[0],
                                  jnp.asarray(_ABS_DELTAS), w_out[0])
    kspec = _filter_spec(taps, hyena_d[0], f1f, gf)

    yh = _hyena(p5, conv_w[0], conv_b, kspec, f1, f1i, gf, gb)
    out4 = _out_proj(yh, p5, x4, pool_w[0], pool_scale, norm_h_g, norm_p_g,
                     w_out_bf, post_norm_g)
    return out4.reshape(BATCH, SEQ, D_MODEL)
```

```python
import math

import numpy as np
import jax
import jax.numpy as jnp
from jax import lax
from jax.experimental import pallas as pl
from jax.experimental.pallas import tpu as pltpu

F32 = jnp.float32
BF16 = jnp.bfloat16

D_MODEL = 1024
BATCH = 4
SEQ = 8192
HYENA_WIDTH = 512
POOL_WIDTH = 512
POOL_WINDOWS = (2, 4, 8, 16)
FILTER_EMB = 33
FILTER_BANDS = 16
FILTER_HIDDEN = 64
PROJ_WIDTH = 3072
EPS = 1e-6

LANES = 128
N_FFT = 2 * SEQ
N_TILES = 128
TILE_ROWS = SEQ // N_TILES
N_SLOTS = 64
HY_SUB = 128
N_SUB = N_TILES // HY_SUB
A_STRIDE = 72
SLOT_BATCH = 4
N_CBLK = HYENA_WIDTH // LANES
PROJ_BLOCKS = PROJ_WIDTH // LANES
IN_TILES = 16
OUT_TILES = 32
SUB_TILES = 8
VMEM_LIMIT = 60 * 1024 * 1024
OUT_VMEM_LIMIT = 63 * 1024 * 1024


def _dft_tables():
    n1 = np.arange(TILE_ROWS)
    n2 = np.arange(N_TILES)
    s = np.arange(N_SLOTS)
    ph = 2 * np.pi * (n2[:, None, None] * s[None, :, None] / N_FFT
                      + n1[None, None, :] * s[None, :, None] / 128.0)
    f1 = np.zeros((N_TILES, 128, TILE_ROWS))
    f1[:, :64, :] = np.cos(ph)
    f1[:, 64:, :] = -np.sin(ph)
    f1[:, 0, :] = 1.0
    f1[:, 64, :] = (-1.0) ** n1
    php = np.transpose(ph, (0, 2, 1))
    f1i = np.zeros((N_TILES, TILE_ROWS, 128))
    f1i[:, :, :64] = 2 * np.cos(php) / N_FFT
    f1i[:, :, 64:] = -2 * np.sin(php) / N_FFT
    f1i[:, :, 0] = 1.0 / N_FFT
    f1i[:, :, 64] = ((-1.0) ** n1)[None, :] / N_FFT
    k2 = np.arange(128)
    th = 2 * np.pi * np.outer(k2, n2) / 128.0
    c, sn = np.cos(th), np.sin(th)
    g = np.block([[c, sn], [-sn, c]])
    gi = np.block([[c, -sn], [sn, c]])
    kk = np.arange(64)
    tha = 2 * np.pi * np.outer(kk, n2) / 128.0
    thb = 2 * np.pi * np.outer(64 + 128 * kk, n2) / N_FFT
    g0 = np.zeros((256, 256))
    g0[0:64, 0:128] = np.cos(tha)
    g0[64:128, 128:256] = np.cos(thb)
    g0[128:192, 0:128] = -np.sin(tha)
    g0[128, 0:128] = (-1.0) ** n2
    g0[192:256, 128:256] = -np.sin(thb)
    g0i = np.zeros((256, 256))
    g0i[0:128, 0:64] = 2 * np.cos(tha.T)
    g0i[0:128, 0] = 1.0
    g0i[0:128, 128:192] = -2 * np.sin(tha.T)
    g0i[0:128, 128] = (-1.0) ** n2
    g0i[128:256, 64:128] = 2 * np.cos(thb.T)
    g0i[128:256, 192:256] = -2 * np.sin(thb.T)
    q = np.arange(256)
    perm = np.where(q % 16 < 8, 8 * (q // 16) + q % 16, 128 + 8 * (q // 16) + q % 16 - 8)
    gf = np.stack([g0[perm, :], g[perm, :]])
    gb = np.stack([g0i[:, perm], gi[:, perm]])
    n1f = np.arange(128)
    phf = 2 * np.pi * (n2[:, None, None] * s[None, :, None] / N_FFT
                       + n1f[None, None, :] * s[None, :, None] / 128.0)
    full = np.zeros((N_TILES, 128, 128))
    full[:, :64, :] = np.cos(phf)
    full[:, 64:, :] = -np.sin(phf)
    full[:, 0, :] = 1.0
    full[:, 64, :] = (-1.0) ** n1f
    rev = full[:, :, 127:63:-1].copy()
    rev[0, :, 1:] = full[0, :, 127:64:-1]
    rev[0, :, 0] = 0.0
    f1f = np.concatenate([full[:, :, :64], rev], axis=2)
    pair64 = np.arange(128) % 2 * 64 + np.arange(128) // 2
    pair128 = np.arange(256) % 2 * 128 + np.arange(256) // 2
    f1, f1f = f1[:, pair64, :], f1f[:, pair64, :]
    f1i = f1i[:, :, pair64]
    gf = gf[:, :, pair128]
    gb = gb[:, pair128, :]
    return f1, f1i, gf, gb, f1f


def _filter_features():
    pos = np.arange(SEQ, dtype=np.float64)
    t = pos / (SEQ - 1)
    ang = 2.0 * math.pi * pos / SEQ
    bands = np.linspace(1e-4, FILTER_BANDS - 1, FILTER_BANDS)
    z = np.concatenate([t[:, None], np.cos(bands[None, :] * ang[:, None]),
                        -np.sin(bands[None, :] * ang[:, None])], axis=-1)
    z = z.reshape(TILE_ROWS, N_TILES, FILTER_EMB).transpose(1, 0, 2).reshape(SEQ, FILTER_EMB)
    zp = np.zeros((FILTER_HIDDEN, SEQ))
    zp[:FILTER_EMB, :] = z.T
    max_decay = math.log(1e-2) / 0.3
    min_decay = math.log(1e-2) / 1.5
    deltas = np.abs(np.linspace(min_decay, max_decay, HYENA_WIDTH))
    return zp.astype(np.float32), deltas.astype(np.float32)[None, :]


_F1, _F1I, _GF, _GB, _F1F = _dft_tables()
_ZFEAT, _ABS_DELTAS = _filter_features()


def _shift_down(x):
    rows = lax.broadcasted_iota(jnp.int32, x.shape, 0)
    return jnp.where(rows == 0, 0.0, pltpu.roll(x, 1, axis=0))


def _shift_up(x):
    rows = lax.broadcasted_iota(jnp.int32, x.shape, 0)
    return jnp.where(rows == x.shape[0] - 1, 0.0, pltpu.roll(x, x.shape[0] - 1, axis=0))


def _pair(ref, i):
    return jnp.concatenate([ref[0, i], ref[1, i]], axis=-1).astype(F32)


def _dup(x):
    return jnp.concatenate([x, x], axis=-1)


def _silu(z):
    hz = 0.5 * z
    return hz * (1.0 + jnp.tanh(hz))


def _tile_rows(n2):
    return pl.ds(pl.multiple_of(n2 * A_STRIDE, 8), N_SLOTS)


def _slot_rows(s):
    return pl.ds(s, N_TILES, stride=A_STRIDE)


def _stage_load(a_ref, rows):
    words = jnp.concatenate([a_ref[0, rows, :], a_ref[1, rows, :]], axis=-1)
    return pltpu.bitcast(words, BF16)


def _stage_store(a_ref, rows, val):
    words = pltpu.bitcast(val.astype(BF16), jnp.uint32)
    a_ref[0, rows, :] = words[:, :LANES]
    a_ref[1, rows, :] = words[:, LANES:]


N_BATCHES = N_SLOTS // SLOT_BATCH
SPEC_BLOCKS = 16


def _batch_slots(i):
    return [i * SLOT_BATCH + j for j in range(SLOT_BATCH)]


def _stage2_forward(a_ref, gf_ref, slots, sel0):
    xs = [_stage_load(a_ref, _slot_rows(s)) for s in slots]
    return [jnp.dot(gf_ref[sel0] if j == 0 else gf_ref[1], x, preferred_element_type=F32)
            for j, x in enumerate(xs)]


def _re_im_blocks(y):
    return [(y[16 * i:16 * i + 8], y[16 * i + 8:16 * i + 16]) for i in range(SPEC_BLOCKS)]


_GATE_GROUPS = (3, 5)
N_IN_STEPS = BATCH * (N_TILES // IN_TILES)
FILT_TILES = N_TILES // N_IN_STEPS


def _split_bf16(x):
    hi = x.astype(BF16)
    return hi, (x - hi.astype(F32)).astype(BF16)


def _dot_split(a, b):
    a_hi, a_lo = _split_bf16(a)
    b_hi, b_lo = _split_bf16(b)
    dot = lambda u, v: jnp.dot(u, v, preferred_element_type=F32)
    return dot(a_hi, b_hi) + dot(a_lo, b_hi) + dot(a_hi, b_lo)


def _first_step():
    return (pl.program_id(0) == 0) & (pl.program_id(1) == 0)


def _cast_weights_once(w_ref, w_scr, ncol):
    @pl.when(_first_step())
    def _():
        for c in range(w_ref.shape[1] // ncol):
            w_scr[:, c * ncol:(c + 1) * ncol] = w_ref[:, c * ncol:(c + 1) * ncol].astype(BF16)


class _FilterTaps:
    def __init__(self, z_ref, p_ref, wp_scr, dl_ref, first_tile, o_ref):
        self.p_ref, self.wp_scr, self.dl_ref, self.first_tile, self.o_ref = p_ref, wp_scr, dl_ref, first_tile, o_ref
        self.cols_blk = p_ref[3].T[:FILTER_HIDDEN]
        self.h = z_ref[...]
        self.n_stages = 5

    def stage(self, k):
        hid = FILTER_HIDDEN
        if k < 3:
            w_t = self.p_ref[k].T[:hid, :hid]
            freq = self.cols_blk[:, 3:4]
            self.h = jnp.sin(freq * (_dot_split(w_t, self.h) + self.cols_blk[:, k:k + 1]))
            return
        if k == 3:
            h_hi = self.h.astype(BF16).astype(F32)
            stack = jnp.concatenate([h_hi, self.h - h_hi, h_hi, jnp.zeros_like(h_hi)], axis=0)
            self.lhs = stack.T.astype(BF16)
            rows = FILT_TILES * TILE_ROWS
            r = lax.broadcasted_iota(jnp.int32, (rows, HYENA_WIDTH), 0)
            pos = 128 * (r % TILE_ROWS) + self.first_tile + r // TILE_ROWS
            t = pos.astype(F32) / float(SEQ - 1)
            self.decay = jnp.exp(-t * self.dl_ref[...])
        for od in ((0, 1) if k == 3 else (2, 3)):
            cols = slice(od * HYENA_WIDTH, (od + 1) * HYENA_WIDTH)
            taps = jnp.dot(self.lhs, self.wp_scr[:, cols], preferred_element_type=F32) * self.decay
            taps = taps.astype(BF16)
            for cb in range(N_CBLK):
                for i in range(FILT_TILES):
                    self.o_ref[od * N_CBLK + cb, i] = taps[i * TILE_ROWS:(i + 1) * TILE_ROWS,
                                                           cb * LANES:(cb + 1) * LANES]


def _in_proj_kernel(x_ref, g_ref, w32_ref, z_ref, p_ref, wp32_ref, dl_ref, o_ref, taps_ref, h_scr, w_ref, wp_scr):
    _cast_weights_once(w32_ref, w_ref, 512)

    @pl.when(_first_step())
    def _():
        wp_hi, wp_lo = _split_bf16(wp32_ref[...])
        wp_scr[...] = jnp.concatenate([wp_hi, wp_hi, wp_lo, jnp.zeros_like(wp_lo)], axis=0)

    step = pl.program_id(0) * (N_TILES // IN_TILES) + pl.program_id(1)
    filt = _FilterTaps(z_ref, p_ref, wp_scr, dl_ref, step * FILT_TILES, taps_ref)
    n_dots = 0

    ncol = 512
    for sb in range(IN_TILES // SUB_TILES):
        j0 = sb * SUB_TILES
        x = x_ref[:, j0:j0 + SUB_TILES, :].reshape(TILE_ROWS * SUB_TILES, D_MODEL)
        ms = jnp.mean(x * x, axis=-1, keepdims=True)
        hn = x * lax.rsqrt(ms + EPS) * g_ref[...]
        for k in range(D_MODEL // LANES):
            h_scr[sb, k] = hn[:, k * LANES:(k + 1) * LANES]
        h = jnp.concatenate(
            [jnp.concatenate([h_scr[sb, k, pl.ds(j, TILE_ROWS, stride=SUB_TILES), :]
                              for k in range(D_MODEL // LANES)], axis=-1).astype(BF16)
             for j in range(SUB_TILES)], axis=0)
        for c in range(PROJ_WIDTH // ncol):
            p = jnp.dot(h, w_ref[:, c * ncol:(c + 1) * ncol], preferred_element_type=F32)
            if (c * ncol) // HYENA_WIDTH in _GATE_GROUPS:
                p = _silu(p)
            p = p.astype(BF16)
            for cb in range(ncol // LANES):
                for j in range(SUB_TILES):
                    o_ref[c * (ncol // LANES) + cb, j0 + j] = p[j * TILE_ROWS:(j + 1) * TILE_ROWS,
                                                                cb * LANES:(cb + 1) * LANES]
            if n_dots % 2 == 0 and n_dots // 2 < filt.n_stages:
                filt.stage(n_dots // 2)
            n_dots += 1


def _in_proj(x4, pre_g, w_in, zfeat_t, w1, b1, w2, b2, w3, b3, freq, w_proj, abs_deltas):
    steps_per_batch = N_TILES // IN_TILES
    pad2 = lambda m: jnp.pad(m, ((0, LANES - m.shape[0]), (0, LANES - m.shape[1])))
    params = jnp.stack([pad2(w1), pad2(w2), pad2(w3), pad2(jnp.stack([b1, b2, b3, freq], axis=0))])
    full = lambda shape: pl.BlockSpec(shape, lambda b, g: (0,) * len(shape))
    once = lambda shape: pl.BlockSpec(shape, lambda b, g: (0,) * len(shape), pipeline_mode=pl.Buffered(1))
    return pl.pallas_call(
        _in_proj_kernel,
        grid=(BATCH, steps_per_batch),
        in_specs=[
            pl.BlockSpec((None, TILE_ROWS, IN_TILES, D_MODEL), lambda b, g: (b, 0, g, 0)),
            full((1, D_MODEL)),
            once((D_MODEL, PROJ_WIDTH)),
            pl.BlockSpec((FILTER_HIDDEN, FILT_TILES * TILE_ROWS), lambda b, g: (0, b * steps_per_batch + g)),
            full((4, LANES, LANES)),
            once((FILTER_HIDDEN, 4 * HYENA_WIDTH)),
            full((1, HYENA_WIDTH)),
        ],
        out_specs=[
            pl.BlockSpec((None, PROJ_BLOCKS, IN_TILES, TILE_ROWS, LANES), lambda b, g: (b, 0, g, 0, 0)),
            pl.BlockSpec((4 * N_CBLK, FILT_TILES, TILE_ROWS, LANES), lambda b, g: (0, b * steps_per_batch + g, 0, 0)),
        ],
        out_shape=[jax.ShapeDtypeStruct((BATCH, PROJ_BLOCKS, N_TILES, TILE_ROWS, LANES), BF16),
                   jax.ShapeDtypeStruct((4 * N_CBLK, N_TILES, TILE_ROWS, LANES), BF16)],
        scratch_shapes=[pltpu.VMEM((IN_TILES // SUB_TILES, D_MODEL // LANES, TILE_ROWS * SUB_TILES, LANES), F32),
                        pltpu.VMEM((D_MODEL, PROJ_WIDTH), BF16),
                        pltpu.VMEM((2 * LANES, 4 * HYENA_WIDTH), BF16)],
        compiler_params=pltpu.CompilerParams(
            dimension_semantics=("arbitrary", "arbitrary"), vmem_limit_bytes=VMEM_LIMIT),
        name="in_proj",
    )(x4, pre_g, w_in, zfeat_t, params, w_proj, abs_deltas)


def _filter_spec_kernel(fa_ref, fb_ref, ba_ref, bb_ref, d_ref, f1f_ref, gf_ref, k_ref, a_scr):
    def stage1(sb, carry):
        for i in range(HY_SUB):
            n2 = sb * HY_SUB + i
            nb = (N_TILES - n2) % N_TILES
            fwd = jnp.concatenate([fa_ref[n2], fb_ref[n2]], axis=-1)
            bwd = jnp.concatenate([ba_ref[nb], bb_ref[nb]], axis=-1)
            taps = jnp.concatenate([fwd, bwd], axis=0)
            _stage_store(a_scr, _tile_rows(n2), jnp.dot(f1f_ref[n2], taps, preferred_element_type=F32))
        return carry

    lax.fori_loop(0, N_SUB, stage1, 0)

    d = d_ref[pl.ds(pl.program_id(0), 1), :]

    def batch(slots, first):
        ys = _stage2_forward(a_scr, gf_ref, slots, 0 if first else 1)
        for j, (s, y) in enumerate(zip(slots, ys)):
            blocks = []
            for i, (yr, yi) in enumerate(_re_im_blocks(y)):
                yr = yr + d
                if first and j == 0 and i == 0:
                    rows = lax.broadcasted_iota(jnp.int32, yi.shape, 0)
                    yi = yi + jnp.where(rows == 0, d, 0.0)
                blocks += [yr, yi]
            spec = jnp.concatenate(blocks, axis=0).astype(BF16)
            k_ref[0, s] = spec[:, :LANES]
            k_ref[1, s] = spec[:, LANES:]

    batch(_batch_slots(0), True)

    def loop(i, carry):
        batch(_batch_slots(i), False)
        return carry

    lax.fori_loop(1, N_BATCHES, loop, 0, unroll=3)


def _filter_spec(taps, hyena_d, f1f, gf):
    grid = (2, N_CBLK // 2)
    const = lambda shape: pl.BlockSpec(shape, lambda o, c: (0,) * len(shape), pipeline_mode=pl.Buffered(1))

    def taps_spec(direction, k):
        return pl.BlockSpec((None, N_TILES, TILE_ROWS, LANES),
                            lambda o, c: ((2 * o + direction) * N_CBLK + 2 * c + k, 0, 0, 0))

    return pl.pallas_call(
        _filter_spec_kernel,
        grid=grid,
        in_specs=[taps_spec(0, 0), taps_spec(0, 1), taps_spec(1, 0), taps_spec(1, 1),
                  pl.BlockSpec((2, 2 * LANES), lambda o, c: (0, c)),
                  const((N_TILES, 128, 128)), const((2, 256, 256))],
        out_specs=pl.BlockSpec((None, 2, N_SLOTS, 256, LANES), lambda o, c: (o, c, 0, 0, 0)),
        out_shape=jax.ShapeDtypeStruct((2, N_CBLK, N_SLOTS, 256, LANES), BF16),
        scratch_shapes=[pltpu.VMEM((2, N_TILES * A_STRIDE, LANES), jnp.uint32)],
        compiler_params=pltpu.CompilerParams(
            dimension_semantics=("arbitrary", "arbitrary"), vmem_limit_bytes=VMEM_LIMIT),
        name="filter_spec",
    )(taps, taps, taps, taps, hyena_d, f1f, gf)


_T_S1 = 0
_T_F0 = 1
_T_M = 2
_T_F1 = 3
_T_E = 4
_T_END = 5
MID_GROUP = 8


def _hyena_kernel(cin_ref, pz_ref, cw_ref, k_ref, f1_ref, f1i_ref, gf_ref, gb_ref, o_ref, a_scr, z_scr):
    t = pl.program_id(2)

    def stage1(n2, u_bf):
        _stage_store(a_scr, _tile_rows(n2), jnp.dot(f1_ref[n2], u_bf, preferred_element_type=F32))

    def inv_stage1(n2):
        return jnp.dot(f1i_ref[n2], _stage_load(a_scr, _tile_rows(n2)), preferred_element_type=F32)

    def short_conv(sb, row0):
        base = sb * HY_SUB
        first = _pair(cin_ref, (base + N_TILES - 1) % N_TILES)
        first = jnp.where(sb == 0, _shift_down(first), first)
        last = _pair(cin_ref, (base + HY_SUB) % N_TILES)
        last = jnp.where(sb == N_SUB - 1, _shift_up(last), last)
        tiles = [first] + [_pair(cin_ref, base + i) for i in range(HY_SUB)] + [last]
        w = cw_ref[row0:row0 + 4, :]
        w0, w1, w2, b = _dup(w[0:1]), _dup(w[1:2]), _dup(w[2:3]), _dup(w[3:4])
        return [tiles[i] * w0 + tiles[i + 1] * w1 + tiles[i + 2] * w2 + b for i in range(HY_SUB)]

    def filter_multiply(slots, ys, first, buf):
        for j, (s, y) in enumerate(zip(slots, ys)):
            blocks = []
            for i, (yr, yi) in enumerate(_re_im_blocks(y)):
                kblk = k_ref[s, 16 * i:16 * i + 16, :].astype(F32)
                kr, ki = kblk[:8], kblk[8:]
                if first and j == 0 and i == 0:
                    rows = lax.broadcasted_iota(jnp.int32, kr.shape, 0)
                    ka, kb, kd = kr, jnp.where(rows == 0, 0.0, ki), jnp.where(rows == 0, ki, kr)
                else:
                    ka, kb, kd = kr, ki, kr
                ka, kb, kd = _dup(ka), _dup(kb), _dup(kd)
                blocks += [yr * ka - yi * kb, yr * kb + yi * kd]
            z_scr[buf, j] = jnp.concatenate(blocks, axis=0).astype(BF16)

    def stage2_inverse(slots, first_sel, buf):
        bms = [jnp.dot(gb_ref[first_sel] if j == 0 else gb_ref[1], z_scr[buf, j], preferred_element_type=F32)
               for j in range(SLOT_BATCH)]
        for s, bm in zip(slots, bms):
            _stage_store(a_scr, _slot_rows(s), bm)

    def spectral_phase():
        slots0 = _batch_slots(0)
        filter_multiply(slots0, _stage2_forward(a_scr, gf_ref, slots0, 0), True, 0)

        def loop(i, carry):
            slots = _batch_slots(i)
            ys = _stage2_forward(a_scr, gf_ref, slots, 1)
            stage2_inverse(_batch_slots(i - 1), jnp.where(i == 1, 0, 1), (i - 1) % 2)
            filter_multiply(slots, ys, False, i % 2)
            return carry

        lax.fori_loop(1, N_BATCHES, loop, 0, unroll=5)
        stage2_inverse(_batch_slots(N_BATCHES - 1), 1, (N_BATCHES - 1) % 2)

    @pl.when(t == _T_S1)
    def _():
        def sub(sb, carry):
            vs = short_conv(sb, 0)
            for i in range(HY_SUB):
                stage1(sb * HY_SUB + i, vs[i].astype(BF16))
            return carry

        lax.fori_loop(0, N_SUB, sub, 0)

    @pl.when((t == _T_F0) | (t == _T_F1))
    def _():
        spectral_phase()

    @pl.when(t == _T_M)
    def _():
        def sub(sb, carry):
            gates = short_conv(sb, 4)
            for q in range(0, HY_SUB, MID_GROUP):
                ys = [inv_stage1(sb * HY_SUB + q + i) for i in range(MID_GROUP)]
                us = [(gates[q + i] * ys[i]).astype(BF16) for i in range(MID_GROUP)]
                for i in range(MID_GROUP):
                    stage1(sb * HY_SUB + q + i, us[i])
            return carry

        lax.fori_loop(0, N_SUB, sub, 0)

    @pl.when(t == _T_E)
    def _():
        def sub(sb, carry):
            gates = short_conv(sb, 8)
            for i in range(HY_SUB):
                n2 = sb * HY_SUB + i
                res = (gates[i] * inv_stage1(n2) * _pair(pz_ref, n2)).astype(BF16)
                o_ref[0, n2] = res[:, :LANES]
                o_ref[1, n2] = res[:, LANES:]
            return carry

        lax.fori_loop(0, N_SUB, sub, 0)


def _hyena(p5, conv_w, conv_b, kspec, f1, f1i, gf, gb):
    grid = (N_CBLK, BATCH // 2, _T_END)
    cw = jnp.concatenate(
        [jnp.concatenate([conv_w[:, k * HYENA_WIDTH:(k + 1) * HYENA_WIDTH],
                          conv_b[:, k * HYENA_WIDTH:(k + 1) * HYENA_WIDTH]], axis=0) for k in range(3)], axis=0)
    seq_block = (2, None, N_TILES, TILE_ROWS, LANES)

    def conv_in_map(c, b, t):
        return (b, jnp.where(t < _T_M, 0, jnp.where(t < _T_E, 1, 2)) * N_CBLK + c, 0, 0, 0)

    def z_map(c, b, t):
        flat = c * (BATCH // 2) + b
        sel = jnp.where(t >= _T_M, flat, jnp.maximum(flat - 1, 0))
        return (sel % (BATCH // 2), 3 * N_CBLK + sel // (BATCH // 2), 0, 0, 0)

    const = lambda shape: pl.BlockSpec(shape, lambda c, b, t: (0,) * len(shape), pipeline_mode=pl.Buffered(1))
    in_specs = [
        pl.BlockSpec(seq_block, conv_in_map),
        pl.BlockSpec(seq_block, z_map),
        pl.BlockSpec((12, LANES), lambda c, b, t: (0, c)),
        pl.BlockSpec((None, None, N_SLOTS, 256, LANES), lambda c, b, t: (jnp.where(t >= _T_M, 1, 0), c, 0, 0, 0)),
        const((N_TILES, 128, TILE_ROWS)), const((N_TILES, TILE_ROWS, 128)),
        const((2, 256, 256)), const((2, 256, 256)),
    ]
    return pl.pallas_call(
        _hyena_kernel,
        grid=grid,
        in_specs=in_specs,
        out_specs=pl.BlockSpec(seq_block, lambda c, b, t: (b, c, 0, 0, 0)),
        out_shape=jax.ShapeDtypeStruct((BATCH, N_CBLK, N_TILES, TILE_ROWS, LANES), BF16),
        scratch_shapes=[pltpu.VMEM((2, N_TILES * A_STRIDE, LANES), jnp.uint32),
                        pltpu.VMEM((2, SLOT_BATCH, 256, 2 * LANES), BF16)],
        compiler_params=pltpu.CompilerParams(
            dimension_semantics=("arbitrary", "arbitrary", "arbitrary"), vmem_limit_bytes=VMEM_LIMIT),
        name="hyena",
    )(p5, p5, cw, kspec, f1, f1i, gf, gb)


POOL_HALO = 8


def _out_kernel(yh_ref, up_ref, upw_ref, upn_ref, zp_ref, x_ref, pw_ref, ps_ref, gh_ref, gp_ref, w_ref, gpost_ref,
                o_ref, r_scr, yp_scr, prev_scr):
    @pl.when(_first_step())
    def _():
        prev_scr[...] = jnp.zeros_like(prev_scr)

    g = pl.program_id(1)
    last_step = N_TILES // OUT_TILES - 1
    row = lax.broadcasted_iota(jnp.int32, (TILE_ROWS, LANES), 0)

    def pool_tile(ci, l):
        if l < 0:
            wrapped = _shift_down(upw_ref[ci, POOL_HALO + l].astype(F32))
            return jnp.where(g == 0, wrapped, prev_scr[ci, POOL_HALO + l].astype(F32))
        if l >= OUT_TILES:
            t = upn_ref[ci, l - OUT_TILES].astype(F32)
            return jnp.where(g == last_step, _shift_up(t), t)
        return up_ref[ci, l].astype(F32)

    n_sub = OUT_TILES // SUB_TILES
    window_sums = [None] * len(POOL_WINDOWS)

    def pool_group(sb, ci):
        w = POOL_WINDOWS[ci]
        lo, hi = w // 2, w - 1 - w // 2
        if sb == 0:
            window_sum = pool_tile(ci, -lo)
            for d in range(-lo + 1, hi + 1):
                window_sum = window_sum + pool_tile(ci, d)
        else:
            window_sum = window_sums[ci]
        pooled = []
        for i in range(SUB_TILES):
            l = sb * SUB_TILES + i
            if l > 0:
                window_sum = window_sum + pool_tile(ci, l + hi) - pool_tile(ci, l - 1 - lo)
            inv_cnt = 1.0 / w
            if l < lo:
                inv_cnt = jnp.where((g == 0) & (row == 0), 1.0 / (hi + l + 1), inv_cnt)
            elif l > OUT_TILES - 1 - hi:
                inv_cnt = jnp.where((g == last_step) & (row == TILE_ROWS - 1),
                                    1.0 / (lo + OUT_TILES - l), inv_cnt)
            pooled.append((window_sum * inv_cnt - pool_tile(ci, l)).astype(BF16))
        window_sums[ci] = window_sum
        y = jnp.dot(jnp.concatenate(pooled, axis=0), pw_ref[ci].astype(BF16), preferred_element_type=F32)
        y = y * ps_ref[:, ci * LANES:(ci + 1) * LANES]
        gate = jnp.concatenate([zp_ref[ci, sb * SUB_TILES + i] for i in range(SUB_TILES)], axis=0).astype(F32)
        yp_scr[sb % 2, ci] = y * gate

    def rms(y, gain_ref):
        ms = jnp.mean(y * y, axis=-1, keepdims=True)
        return (y * lax.rsqrt(ms + EPS) * gain_ref[...]).astype(BF16)

    for ci in range(N_CBLK):
        pool_group(0, ci)
    ncol = D_MODEL // N_CBLK
    for sb in range(n_sub):
        j0 = sb * SUB_TILES
        yh = jnp.concatenate(
            [jnp.concatenate([yh_ref[cb, j] for cb in range(N_CBLK)], axis=-1) for j in range(j0, j0 + SUB_TILES)],
            axis=0).astype(F32)
        yp = jnp.concatenate([yp_scr[sb % 2, ci] for ci in range(N_CBLK)], axis=-1)
        yc = jnp.concatenate([rms(yh, gh_ref), rms(yp, gp_ref)], axis=-1)
        chunks = []
        for c in range(N_CBLK):
            chunks.append(jnp.dot(yc, w_ref[:, c * ncol:(c + 1) * ncol], preferred_element_type=F32))
            if sb + 1 < n_sub:
                pool_group(sb + 1, c)
        out = jnp.concatenate(chunks, axis=-1)
        ms = jnp.mean(out * out, axis=-1, keepdims=True)
        out = out * lax.rsqrt(ms + EPS) * gpost_ref[...]
        for j in range(SUB_TILES):
            for k in range(D_MODEL // LANES):
                r_scr[sb % 2, k, pl.ds(j, TILE_ROWS, stride=SUB_TILES), :] = out[j * TILE_ROWS:(j + 1) * TILE_ROWS,
                                                                             k * LANES:(k + 1) * LANES]
        r = jnp.concatenate([r_scr[sb % 2, k] for k in range(D_MODEL // LANES)], axis=-1)
        o_ref[:, j0:j0 + SUB_TILES, :] = (x_ref[:, j0:j0 + SUB_TILES, :]
                                          + r.reshape(TILE_ROWS, SUB_TILES, D_MODEL))
    for ci in range(N_CBLK):
        prev_scr[ci] = up_ref[ci, OUT_TILES - POOL_HALO:OUT_TILES]


def _out_proj(yh, p5, x4, pool_w, pool_scale, norm_h_g, norm_p_g, w_out, post_g):
    n_steps = N_TILES // OUT_TILES
    halo_blocks = N_TILES // POOL_HALO
    per_step = OUT_TILES // POOL_HALO
    u_group, z_group = 4, 5
    y_spec = pl.BlockSpec((None, N_CBLK, OUT_TILES, TILE_ROWS, LANES), lambda b, g: (b, 0, g, 0, 0))
    x_spec = pl.BlockSpec((None, TILE_ROWS, OUT_TILES, D_MODEL), lambda b, g: (b, 0, g, 0))
    full = lambda shape: pl.BlockSpec(shape, lambda b, g: (0,) * len(shape))
    return pl.pallas_call(
        _out_kernel,
        grid=(BATCH, n_steps),
        in_specs=[
            y_spec,
            pl.BlockSpec((None, N_CBLK, OUT_TILES, TILE_ROWS, LANES), lambda b, g: (b, u_group, g, 0, 0)),
            pl.BlockSpec((None, N_CBLK, POOL_HALO, TILE_ROWS, LANES),
                         lambda b, g: (b, u_group, halo_blocks - 1, 0, 0)),
            pl.BlockSpec((None, N_CBLK, POOL_HALO, TILE_ROWS, LANES),
                         lambda b, g: (b, u_group, (g * per_step + per_step) % halo_blocks, 0, 0)),
            pl.BlockSpec((None, N_CBLK, OUT_TILES, TILE_ROWS, LANES), lambda b, g: (b, z_group, g, 0, 0)),
            x_spec,
            full((N_CBLK, LANES, LANES)), full((1, POOL_WIDTH)),
            full((1, HYENA_WIDTH)), full((1, POOL_WIDTH)),
            pl.BlockSpec((D_MODEL, D_MODEL), lambda b, g: (0, 0), pipeline_mode=pl.Buffered(1)), full((1, D_MODEL)),
        ],
        out_specs=x_spec,
        out_shape=jax.ShapeDtypeStruct((BATCH, TILE_ROWS, N_TILES, D_MODEL), F32),
        scratch_shapes=[pltpu.VMEM((2, D_MODEL // LANES, TILE_ROWS * SUB_TILES, LANES), F32),
                        pltpu.VMEM((2, N_CBLK, TILE_ROWS * SUB_TILES, LANES), F32),
                        pltpu.VMEM((N_CBLK, POOL_HALO, TILE_ROWS, LANES), BF16)],
        compiler_params=pltpu.CompilerParams(
            dimension_semantics=("arbitrary", "arbitrary"), vmem_limit_bytes=OUT_VMEM_LIMIT),
        name="out_proj",
    )(yh, p5, p5, p5, p5, x4, pool_w, pool_scale, norm_h_g, norm_p_g, w_out, post_g)


def kernel(x, pre_norm_g, w_in, conv_w, conv_b, filt_w1, filt_b1, filt_w2, filt_b2, filt_w3, filt_b3,
           filt_freq, filt_w_out, hyena_d, pool_w, pool_scale, norm_h_g, norm_p_g, w_out, post_norm_g):
    assert x.shape == (BATCH, SEQ, D_MODEL) and pre_norm_g.shape[0] == 1
    f1, f1i, gf, gb, f1f = (jnp.asarray(m, F32).astype(BF16) for m in (_F1, _F1I, _GF, _GB, _F1F))

    x4 = x.reshape(BATCH, TILE_ROWS, N_TILES, D_MODEL)
    p5, taps = _in_proj(x4, pre_norm_g, w_in[0], jnp.asarray(_ZFEAT), filt_w1[0], filt_b1[0], filt_w2[0], filt_b2[0],
                        filt_w3[0], filt_b3[0], filt_freq[0], filt_w_out[0], jnp.asarray(_ABS_DELTAS))
    kspec = _filter_spec(taps, hyena_d[0], f1f, gf)

    yh = _hyena(p5, conv_w[0], conv_b, kspec, f1, f1i, gf, gb)
    out4 = _out_proj(yh, p5, x4, pool_w[0], pool_scale, norm_h_g, norm_p_g,
                     w_out[0].astype(BF16), post_norm_g)
    return out4.reshape(BATCH, SEQ, D_MODEL)
```

```python
import math

import numpy as np
import jax
import jax.numpy as jnp
from jax import lax
from jax.experimental import pallas as pl
from jax.experimental.pallas import tpu as pltpu

F32 = jnp.float32
BF16 = jnp.bfloat16

D_MODEL = 1024
BATCH = 4
SEQ = 8192
HYENA_WIDTH = 512
POOL_WIDTH = 512
POOL_WINDOWS = (2, 4, 8, 16)
FILTER_EMB = 33
FILTER_BANDS = 16
FILTER_HIDDEN = 64
PROJ_WIDTH = 3072
EPS = 1e-6

LANES = 128
N_FFT = 2 * SEQ
N_TILES = 128
TILE_ROWS = SEQ // N_TILES
N_SLOTS = 64
HY_SUB = 128
N_SUB = N_TILES // HY_SUB
A_STRIDE = 72
SLOT_BATCH = 4
N_CBLK = HYENA_WIDTH // LANES
PROJ_BLOCKS = PROJ_WIDTH // LANES
IN_TILES = 16
OUT_TILES = 32
SUB_TILES = 8
VMEM_LIMIT = 60 * 1024 * 1024
OUT_VMEM_LIMIT = 63 * 1024 * 1024


def _dft_tables():
    n1 = np.arange(TILE_ROWS)
    n2 = np.arange(N_TILES)
    s = np.arange(N_SLOTS)
    ph = 2 * np.pi * (n2[:, None, None] * s[None, :, None] / N_FFT
                      + n1[None, None, :] * s[None, :, None] / 128.0)
    f1 = np.zeros((N_TILES, 128, TILE_ROWS))
    f1[:, :64, :] = np.cos(ph)
    f1[:, 64:, :] = -np.sin(ph)
    f1[:, 0, :] = 1.0
    f1[:, 64, :] = (-1.0) ** n1
    php = np.transpose(ph, (0, 2, 1))
    f1i = np.zeros((N_TILES, TILE_ROWS, 128))
    f1i[:, :, :64] = 2 * np.cos(php) / N_FFT
    f1i[:, :, 64:] = -2 * np.sin(php) / N_FFT
    f1i[:, :, 0] = 1.0 / N_FFT
    f1i[:, :, 64] = ((-1.0) ** n1)[None, :] / N_FFT
    k2 = np.arange(128)
    th = 2 * np.pi * np.outer(k2, n2) / 128.0
    c, sn = np.cos(th), np.sin(th)
    g = np.block([[c, sn], [-sn, c]])
    gi = np.block([[c, -sn], [sn, c]])
    kk = np.arange(64)
    tha = 2 * np.pi * np.outer(kk, n2) / 128.0
    thb = 2 * np.pi * np.outer(64 + 128 * kk, n2) / N_FFT
    g0 = np.zeros((256, 256))
    g0[0:64, 0:128] = np.cos(tha)
    g0[64:128, 128:256] = np.cos(thb)
    g0[128:192, 0:128] = -np.sin(tha)
    g0[128, 0:128] = (-1.0) ** n2
    g0[192:256, 128:256] = -np.sin(thb)
    g0i = np.zeros((256, 256))
    g0i[0:128, 0:64] = 2 * np.cos(tha.T)
    g0i[0:128, 0] = 1.0
    g0i[0:128, 128:192] = -2 * np.sin(tha.T)
    g0i[0:128, 128] = (-1.0) ** n2
    g0i[128:256, 64:128] = 2 * np.cos(thb.T)
    g0i[128:256, 192:256] = -2 * np.sin(thb.T)
    q = np.arange(256)
    perm = np.where(q % 16 < 8, 8 * (q // 16) + q % 16, 128 + 8 * (q // 16) + q % 16 - 8)
    gf = np.stack([g0[perm, :], g[perm, :]])
    gb = np.stack([g0i[:, perm], gi[:, perm]])
    n1f = np.arange(128)
    phf = 2 * np.pi * (n2[:, None, None] * s[None, :, None] / N_FFT
                       + n1f[None, None, :] * s[None, :, None] / 128.0)
    full = np.zeros((N_TILES, 128, 128))
    full[:, :64, :] = np.cos(phf)
    full[:, 64:, :] = -np.sin(phf)
    full[:, 0, :] = 1.0
    full[:, 64, :] = (-1.0) ** n1f
    rev = full[:, :, 127:63:-1].copy()
    rev[0, :, 1:] = full[0, :, 127:64:-1]
    rev[0, :, 0] = 0.0
    f1f = np.concatenate([full[:, :, :64], rev], axis=2)
    pair64 = np.arange(128) % 2 * 64 + np.arange(128) // 2
    pair128 = np.arange(256) % 2 * 128 + np.arange(256) // 2
    f1, f1f = f1[:, pair64, :], f1f[:, pair64, :]
    f1i = f1i[:, :, pair64]
    gf = gf[:, :, pair128]
    gb = gb[:, pair128, :]
    return f1, f1i, gf, gb, f1f


def _filter_features():
    pos = np.arange(SEQ, dtype=np.float64)
    t = pos / (SEQ - 1)
    ang = 2.0 * math.pi * pos / SEQ
    bands = np.linspace(1e-4, FILTER_BANDS - 1, FILTER_BANDS)
    z = np.concatenate([t[:, None], np.cos(bands[None, :] * ang[:, None]),
                        -np.sin(bands[None, :] * ang[:, None])], axis=-1)
    z = z.reshape(TILE_ROWS, N_TILES, FILTER_EMB).transpose(1, 0, 2).reshape(SEQ, FILTER_EMB)
    zp = np.zeros((FILTER_HIDDEN, SEQ))
    zp[:FILTER_EMB, :] = z.T
    max_decay = math.log(1e-2) / 0.3
    min_decay = math.log(1e-2) / 1.5
    deltas = np.abs(np.linspace(min_decay, max_decay, HYENA_WIDTH))
    return zp.astype(np.float32), deltas.astype(np.float32)[None, :]


_F1, _F1I, _GF, _GB, _F1F = _dft_tables()
_ZFEAT, _ABS_DELTAS = _filter_features()


def _shift_down(x):
    rows = lax.broadcasted_iota(jnp.int32, x.shape, 0)
    return jnp.where(rows == 0, 0.0, pltpu.roll(x, 1, axis=0))


def _shift_up(x):
    rows = lax.broadcasted_iota(jnp.int32, x.shape, 0)
    return jnp.where(rows == x.shape[0] - 1, 0.0, pltpu.roll(x, x.shape[0] - 1, axis=0))


def _pair(ref, i):
    return jnp.concatenate([ref[0, i], ref[1, i]], axis=-1).astype(F32)


def _dup(x):
    return jnp.concatenate([x, x], axis=-1)


def _silu(z):
    hz = 0.5 * z
    return hz * (1.0 + jnp.tanh(hz))


def _tile_rows(n2):
    return pl.ds(pl.multiple_of(n2 * A_STRIDE, 8), N_SLOTS)


def _slot_rows(s):
    return pl.ds(s, N_TILES, stride=A_STRIDE)


def _stage_load(a_ref, rows):
    words = jnp.concatenate([a_ref[0, rows, :], a_ref[1, rows, :]], axis=-1)
    return pltpu.bitcast(words, BF16)


def _stage_store(a_ref, rows, val):
    words = pltpu.bitcast(val.astype(BF16), jnp.uint32)
    a_ref[0, rows, :] = words[:, :LANES]
    a_ref[1, rows, :] = words[:, LANES:]


N_BATCHES = N_SLOTS // SLOT_BATCH
SPEC_BLOCKS = 16


def _batch_slots(i):
    return [i * SLOT_BATCH + j for j in range(SLOT_BATCH)]


def _stage2_forward(a_ref, gf_ref, slots, sel0):
    xs = [_stage_load(a_ref, _slot_rows(s)) for s in slots]
    return [jnp.dot(gf_ref[sel0] if j == 0 else gf_ref[1], x, preferred_element_type=F32)
            for j, x in enumerate(xs)]


def _re_im_blocks(y):
    return [(y[16 * i:16 * i + 8], y[16 * i + 8:16 * i + 16]) for i in range(SPEC_BLOCKS)]


_GATE_GROUPS = (3, 5)
N_IN_STEPS = BATCH * (N_TILES // IN_TILES)
FILT_TILES = N_TILES // N_IN_STEPS


def _split_bf16(x):
    hi = x.astype(BF16)
    return hi, (x - hi.astype(F32)).astype(BF16)


def _dot_split(a, b):
    a_hi, a_lo = _split_bf16(a)
    b_hi, b_lo = _split_bf16(b)
    dot = lambda u, v: jnp.dot(u, v, preferred_element_type=F32)
    return dot(a_hi, b_hi) + dot(a_lo, b_hi) + dot(a_hi, b_lo)


def _first_step():
    return (pl.program_id(0) == 0) & (pl.program_id(1) == 0)


def _cast_weights_once(w_ref, w_scr, ncol):
    @pl.when(_first_step())
    def _():
        for c in range(w_ref.shape[1] // ncol):
            w_scr[:, c * ncol:(c + 1) * ncol] = w_ref[:, c * ncol:(c + 1) * ncol].astype(BF16)


class _FilterTaps:
    def __init__(self, z_ref, p_ref, wp_scr, dl_ref, first_tile, o_ref):
        self.p_ref, self.wp_scr, self.dl_ref, self.first_tile, self.o_ref = p_ref, wp_scr, dl_ref, first_tile, o_ref
        self.cols_blk = p_ref[3].T[:FILTER_HIDDEN]
        self.h = z_ref[...]
        self.n_stages = 5

    def stage(self, k):
        hid = FILTER_HIDDEN
        if k < 3:
            w_t = self.p_ref[k].T[:hid, :hid]
            freq = self.cols_blk[:, 3:4]
            self.h = jnp.sin(freq * (_dot_split(w_t, self.h) + self.cols_blk[:, k:k + 1]))
            return
        if k == 3:
            h_hi = self.h.astype(BF16).astype(F32)
            stack = jnp.concatenate([h_hi, self.h - h_hi, h_hi, jnp.zeros_like(h_hi)], axis=0)
            self.lhs = stack.T.astype(BF16)
            rows = FILT_TILES * TILE_ROWS
            r = lax.broadcasted_iota(jnp.int32, (rows, HYENA_WIDTH), 0)
            pos = 128 * (r % TILE_ROWS) + self.first_tile + r // TILE_ROWS
            t = pos.astype(F32) / float(SEQ - 1)
            self.decay = jnp.exp(-t * self.dl_ref[...])
        for od in ((0, 1) if k == 3 else (2, 3)):
            cols = slice(od * HYENA_WIDTH, (od + 1) * HYENA_WIDTH)
            taps = jnp.dot(self.lhs, self.wp_scr[:, cols], preferred_element_type=F32) * self.decay
            taps = taps.astype(BF16)
            for cb in range(N_CBLK):
                for i in range(FILT_TILES):
                    self.o_ref[od * N_CBLK + cb, i] = taps[i * TILE_ROWS:(i + 1) * TILE_ROWS,
                                                           cb * LANES:(cb + 1) * LANES]


def _in_proj_kernel(x_ref, g_ref, w32_ref, z_ref, p_ref, wp32_ref, dl_ref, o_ref, taps_ref, h_scr, w_ref, wp_scr):
    _cast_weights_once(w32_ref, w_ref, 512)

    @pl.when(_first_step())
    def _():
        wp_hi, wp_lo = _split_bf16(wp32_ref[...])
        wp_scr[...] = jnp.concatenate([wp_hi, wp_hi, wp_lo, jnp.zeros_like(wp_lo)], axis=0)

    step = pl.program_id(0) * (N_TILES // IN_TILES) + pl.program_id(1)
    filt = _FilterTaps(z_ref, p_ref, wp_scr, dl_ref, step * FILT_TILES, taps_ref)
    n_dots = 0

    ncol = 512
    for sb in range(IN_TILES // SUB_TILES):
        j0 = sb * SUB_TILES
        x = x_ref[:, j0:j0 + SUB_TILES, :].reshape(TILE_ROWS * SUB_TILES, D_MODEL)
        ms = jnp.mean(x * x, axis=-1, keepdims=True)
        hn = x * lax.rsqrt(ms + EPS) * g_ref[...]
        for k in range(D_MODEL // LANES):
            h_scr[sb, k] = hn[:, k * LANES:(k + 1) * LANES]
        h = jnp.concatenate(
            [jnp.concatenate([h_scr[sb, k, pl.ds(j, TILE_ROWS, stride=SUB_TILES), :]
                              for k in range(D_MODEL // LANES)], axis=-1).astype(BF16)
             for j in range(SUB_TILES)], axis=0)
        for c in range(PROJ_WIDTH // ncol):
            p = jnp.dot(h, w_ref[:, c * ncol:(c + 1) * ncol], preferred_element_type=F32)
            if (c * ncol) // HYENA_WIDTH in _GATE_GROUPS:
                p = _silu(p)
            p = p.astype(BF16)
            for cb in range(ncol // LANES):
                for j in range(SUB_TILES):
                    o_ref[c * (ncol // LANES) + cb, j0 + j] = p[j * TILE_ROWS:(j + 1) * TILE_ROWS,
                                                                cb * LANES:(cb + 1) * LANES]
            if n_dots % 2 == 0 and n_dots // 2 < filt.n_stages:
                filt.stage(n_dots // 2)
            n_dots += 1


def _in_proj(x4, pre_g, w_in, zfeat_t, w1, b1, w2, b2, w3, b3, freq, w_proj, abs_deltas):
    steps_per_batch = N_TILES // IN_TILES
    pad2 = lambda m: jnp.pad(m, ((0, LANES - m.shape[0]), (0, LANES - m.shape[1])))
    params = jnp.stack([pad2(w1), pad2(w2), pad2(w3), pad2(jnp.stack([b1, b2, b3, freq], axis=0))])
    full = lambda shape: pl.BlockSpec(shape, lambda b, g: (0,) * len(shape))
    once = lambda shape: pl.BlockSpec(shape, lambda b, g: (0,) * len(shape), pipeline_mode=pl.Buffered(1))
    return pl.pallas_call(
        _in_proj_kernel,
        grid=(BATCH, steps_per_batch),
        in_specs=[
            pl.BlockSpec((None, TILE_ROWS, IN_TILES, D_MODEL), lambda b, g: (b, 0, g, 0)),
            full((1, D_MODEL)),
            once((D_MODEL, PROJ_WIDTH)),
            pl.BlockSpec((FILTER_HIDDEN, FILT_TILES * TILE_ROWS), lambda b, g: (0, b * steps_per_batch + g)),
            full((4, LANES, LANES)),
            once((FILTER_HIDDEN, 4 * HYENA_WIDTH)),
            full((1, HYENA_WIDTH)),
        ],
        out_specs=[
            pl.BlockSpec((None, PROJ_BLOCKS, IN_TILES, TILE_ROWS, LANES), lambda b, g: (b, 0, g, 0, 0)),
            pl.BlockSpec((4 * N_CBLK, FILT_TILES, TILE_ROWS, LANES), lambda b, g: (0, b * steps_per_batch + g, 0, 0)),
        ],
        out_shape=[jax.ShapeDtypeStruct((BATCH, PROJ_BLOCKS, N_TILES, TILE_ROWS, LANES), BF16),
                   jax.ShapeDtypeStruct((4 * N_CBLK, N_TILES, TILE_ROWS, LANES), BF16)],
        scratch_shapes=[pltpu.VMEM((IN_TILES // SUB_TILES, D_MODEL // LANES, TILE_ROWS * SUB_TILES, LANES), F32),
                        pltpu.VMEM((D_MODEL, PROJ_WIDTH), BF16),
                        pltpu.VMEM((2 * LANES, 4 * HYENA_WIDTH), BF16)],
        compiler_params=pltpu.CompilerParams(
            dimension_semantics=("arbitrary", "arbitrary"), vmem_limit_bytes=VMEM_LIMIT),
        name="in_proj",
    )(x4, pre_g, w_in, zfeat_t, params, w_proj, abs_deltas)


def _filter_spec_kernel(fa_ref, fb_ref, ba_ref, bb_ref, d_ref, f1f_ref, gf_ref, k_ref, a_scr):
    def stage1(sb, carry):
        for i in range(HY_SUB):
            n2 = sb * HY_SUB + i
            nb = (N_TILES - n2) % N_TILES
            fwd = jnp.concatenate([fa_ref[n2], fb_ref[n2]], axis=-1)
            bwd = jnp.concatenate([ba_ref[nb], bb_ref[nb]], axis=-1)
            taps = jnp.concatenate([fwd, bwd], axis=0)
            _stage_store(a_scr, _tile_rows(n2), jnp.dot(f1f_ref[n2], taps, preferred_element_type=F32))
        return carry

    lax.fori_loop(0, N_SUB, stage1, 0)

    d = d_ref[pl.ds(pl.program_id(0), 1), :]

    def batch(slots, first):
        ys = _stage2_forward(a_scr, gf_ref, slots, 0 if first else 1)
        for j, (s, y) in enumerate(zip(slots, ys)):
            blocks = []
            for i, (yr, yi) in enumerate(_re_im_blocks(y)):
                yr = yr + d
                if first and j == 0 and i == 0:
                    rows = lax.broadcasted_iota(jnp.int32, yi.shape, 0)
                    yi = yi + jnp.where(rows == 0, d, 0.0)
                blocks += [yr, yi]
            spec = jnp.concatenate(blocks, axis=0).astype(BF16)
            k_ref[0, s] = spec[:, :LANES]
            k_ref[1, s] = spec[:, LANES:]

    batch(_batch_slots(0), True)

    def loop(i, carry):
        batch(_batch_slots(i), False)
        return carry

    lax.fori_loop(1, N_BATCHES, loop, 0, unroll=3)


def _filter_spec(taps, hyena_d, f1f, gf):
    grid = (2, N_CBLK // 2)
    const = lambda shape: pl.BlockSpec(shape, lambda o, c: (0,) * len(shape), pipeline_mode=pl.Buffered(1))

    def taps_spec(direction, k):
        return pl.BlockSpec((None, N_TILES, TILE_ROWS, LANES),
                            lambda o, c: ((2 * o + direction) * N_CBLK + 2 * c + k, 0, 0, 0))

    return pl.pallas_call(
        _filter_spec_kernel,
        grid=grid,
        in_specs=[taps_spec(0, 0), taps_spec(0, 1), taps_spec(1, 0), taps_spec(1, 1),
                  pl.BlockSpec((2, 2 * LANES), lambda o, c: (0, c)),
                  const((N_TILES, 128, 128)), const((2, 256, 256))],
        out_specs=pl.BlockSpec((None, 2, N_SLOTS, 256, LANES), lambda o, c: (o, c, 0, 0, 0)),
        out_shape=jax.ShapeDtypeStruct((2, N_CBLK, N_SLOTS, 256, LANES), BF16),
        scratch_shapes=[pltpu.VMEM((2, N_TILES * A_STRIDE, LANES), jnp.uint32)],
        compiler_params=pltpu.CompilerParams(
            dimension_semantics=("arbitrary", "arbitrary"), vmem_limit_bytes=VMEM_LIMIT),
        name="filter_spec",
    )(taps, taps, taps, taps, hyena_d, f1f, gf)


_T_S1 = 0
_T_F0 = 1
_T_M = 2
_T_F1 = 3
_T_E = 4
_T_END = 5
MID_GROUP = 8


def _hyena_kernel(cin_ref, pz_ref, cw_ref, k_ref, f1_ref, f1i_ref, gf_ref, gb_ref, o_ref, a_scr, z_scr):
    t = pl.program_id(2)

    def stage1(n2, u_bf):
        _stage_store(a_scr, _tile_rows(n2), jnp.dot(f1_ref[n2], u_bf, preferred_element_type=F32))

    def inv_stage1(n2):
        return jnp.dot(f1i_ref[n2], _stage_load(a_scr, _tile_rows(n2)), preferred_element_type=F32)

    def short_conv(sb, row0):
        base = sb * HY_SUB
        first = _pair(cin_ref, (base + N_TILES - 1) % N_TILES)
        first = jnp.where(sb == 0, _shift_down(first), first)
        last = _pair(cin_ref, (base + HY_SUB) % N_TILES)
        last = jnp.where(sb == N_SUB - 1, _shift_up(last), last)
        tiles = [first] + [_pair(cin_ref, base + i) for i in range(HY_SUB)] + [last]
        w = cw_ref[row0:row0 + 4, :]
        w0, w1, w2, b = _dup(w[0:1]), _dup(w[1:2]), _dup(w[2:3]), _dup(w[3:4])
        return [tiles[i] * w0 + tiles[i + 1] * w1 + tiles[i + 2] * w2 + b for i in range(HY_SUB)]

    def filter_multiply(slots, ys, first, buf):
        for j, (s, y) in enumerate(zip(slots, ys)):
            blocks = []
            for i, (yr, yi) in enumerate(_re_im_blocks(y)):
                kblk = k_ref[s, 16 * i:16 * i + 16, :].astype(F32)
                kr, ki = kblk[:8], kblk[8:]
                if first and j == 0 and i == 0:
                    rows = lax.broadcasted_iota(jnp.int32, kr.shape, 0)
                    ka, kb, kd = kr, jnp.where(rows == 0, 0.0, ki), jnp.where(rows == 0, ki, kr)
                else:
                    ka, kb, kd = kr, ki, kr
                ka, kb, kd = _dup(ka), _dup(kb), _dup(kd)
                blocks += [yr * ka - yi * kb, yr * kb + yi * kd]
            z_scr[buf, j] = jnp.concatenate(blocks, axis=0).astype(BF16)

    def stage2_inverse(slots, first_sel, buf):
        bms = [jnp.dot(gb_ref[first_sel] if j == 0 else gb_ref[1], z_scr[buf, j], preferred_element_type=F32)
               for j in range(SLOT_BATCH)]
        for s, bm in zip(slots, bms):
            _stage_store(a_scr, _slot_rows(s), bm)

    def spectral_phase():
        slots0 = _batch_slots(0)
        filter_multiply(slots0, _stage2_forward(a_scr, gf_ref, slots0, 0), True, 0)

        def loop(i, carry):
            slots = _batch_slots(i)
            ys = _stage2_forward(a_scr, gf_ref, slots, 1)
            stage2_inverse(_batch_slots(i - 1), jnp.where(i == 1, 0, 1), (i - 1) % 2)
            filter_multiply(slots, ys, False, i % 2)
            return carry

        lax.fori_loop(1, N_BATCHES, loop, 0, unroll=8)
        stage2_inverse(_batch_slots(N_BATCHES - 1), 1, (N_BATCHES - 1) % 2)

    @pl.when(t == _T_S1)
    def _():
        def sub(sb, carry):
            vs = short_conv(sb, 0)
            for i in range(HY_SUB):
                stage1(sb * HY_SUB + i, vs[i].astype(BF16))
            return carry

        lax.fori_loop(0, N_SUB, sub, 0)

    @pl.when((t == _T_F0) | (t == _T_F1))
    def _():
        spectral_phase()

    @pl.when(t == _T_M)
    def _():
        def sub(sb, carry):
            gates = short_conv(sb, 4)
            for q in range(0, HY_SUB, MID_GROUP):
                ys = [inv_stage1(sb * HY_SUB + q + i) for i in range(MID_GROUP)]
                us = [(gates[q + i] * ys[i]).astype(BF16) for i in range(MID_GROUP)]
                for i in range(MID_GROUP):
                    stage1(sb * HY_SUB + q + i, us[i])
            return carry

        lax.fori_loop(0, N_SUB, sub, 0)

    @pl.when(t == _T_E)
    def _():
        def sub(sb, carry):
            gates = short_conv(sb, 8)
            for i in range(HY_SUB):
                n2 = sb * HY_SUB + i
                res = (gates[i] * inv_stage1(n2) * _pair(pz_ref, n2)).astype(BF16)
                o_ref[0, n2] = res[:, :LANES]
                o_ref[1, n2] = res[:, LANES:]
            return carry

        lax.fori_loop(0, N_SUB, sub, 0)


def _hyena(p5, conv_w, conv_b, kspec, f1, f1i, gf, gb):
    grid = (N_CBLK, BATCH // 2, _T_END)
    cw = jnp.concatenate(
        [jnp.concatenate([conv_w[:, k * HYENA_WIDTH:(k + 1) * HYENA_WIDTH],
                          conv_b[:, k * HYENA_WIDTH:(k + 1) * HYENA_WIDTH]], axis=0) for k in range(3)], axis=0)
    seq_block = (2, None, N_TILES, TILE_ROWS, LANES)

    def conv_in_map(c, b, t):
        return (b, jnp.where(t < _T_M, 0, jnp.where(t < _T_E, 1, 2)) * N_CBLK + c, 0, 0, 0)

    def z_map(c, b, t):
        flat = c * (BATCH // 2) + b
        sel = jnp.where(t >= _T_M, flat, jnp.maximum(flat - 1, 0))
        return (sel % (BATCH // 2), 3 * N_CBLK + sel // (BATCH // 2), 0, 0, 0)

    const = lambda shape: pl.BlockSpec(shape, lambda c, b, t: (0,) * len(shape), pipeline_mode=pl.Buffered(1))
    in_specs = [
        pl.BlockSpec(seq_block, conv_in_map),
        pl.BlockSpec(seq_block, z_map),
        pl.BlockSpec((12, LANES), lambda c, b, t: (0, c)),
        pl.BlockSpec((None, None, N_SLOTS, 256, LANES), lambda c, b, t: (jnp.where(t >= _T_M, 1, 0), c, 0, 0, 0)),
        const((N_TILES, 128, TILE_ROWS)), const((N_TILES, TILE_ROWS, 128)),
        const((2, 256, 256)), const((2, 256, 256)),
    ]
    return pl.pallas_call(
        _hyena_kernel,
        grid=grid,
        in_specs=in_specs,
        out_specs=pl.BlockSpec(seq_block, lambda c, b, t: (b, c, 0, 0, 0)),
        out_shape=jax.ShapeDtypeStruct((BATCH, N_CBLK, N_TILES, TILE_ROWS, LANES), BF16),
        scratch_shapes=[pltpu.VMEM((2, N_TILES * A_STRIDE, LANES), jnp.uint32),
                        pltpu.VMEM((2, SLOT_BATCH, 256, 2 * LANES), BF16)],
        compiler_params=pltpu.CompilerParams(
            dimension_semantics=("arbitrary", "arbitrary", "arbitrary"), vmem_limit_bytes=VMEM_LIMIT),
        name="hyena",
    )(p5, p5, cw, kspec, f1, f1i, gf, gb)


POOL_HALO = 8


def _out_kernel(yh_ref, up_ref, upw_ref, upn_ref, zp_ref, x_ref, pw_ref, ps_ref, gh_ref, gp_ref, w_ref, gpost_ref,
                o_ref, r_scr, yp_scr, prev_scr):
    @pl.when(_first_step())
    def _():
        prev_scr[...] = jnp.zeros_like(prev_scr)

    g = pl.program_id(1)
    last_step = N_TILES // OUT_TILES - 1
    row = lax.broadcasted_iota(jnp.int32, (TILE_ROWS, LANES), 0)

    def pool_tile(ci, l):
        if l < 0:
            wrapped = _shift_down(upw_ref[ci, POOL_HALO + l].astype(F32))
            return jnp.where(g == 0, wrapped, prev_scr[ci, POOL_HALO + l].astype(F32))
        if l >= OUT_TILES:
            t = upn_ref[ci, l - OUT_TILES].astype(F32)
            return jnp.where(g == last_step, _shift_up(t), t)
        return up_ref[ci, l].astype(F32)

    n_sub = OUT_TILES // SUB_TILES
    window_sums = [None] * len(POOL_WINDOWS)

    def pool_group(sb, ci):
        w = POOL_WINDOWS[ci]
        lo, hi = w // 2, w - 1 - w // 2
        if sb == 0:
            window_sum = pool_tile(ci, -lo)
            for d in range(-lo + 1, hi + 1):
                window_sum = window_sum + pool_tile(ci, d)
        else:
            window_sum = window_sums[ci]
        pooled = []
        for i in range(SUB_TILES):
            l = sb * SUB_TILES + i
            if l > 0:
                window_sum = window_sum + pool_tile(ci, l + hi) - pool_tile(ci, l - 1 - lo)
            inv_cnt = 1.0 / w
            if l < lo:
                inv_cnt = jnp.where((g == 0) & (row == 0), 1.0 / (hi + l + 1), inv_cnt)
            elif l > OUT_TILES - 1 - hi:
                inv_cnt = jnp.where((g == last_step) & (row == TILE_ROWS - 1),
                                    1.0 / (lo + OUT_TILES - l), inv_cnt)
            pooled.append((window_sum * inv_cnt - pool_tile(ci, l)).astype(BF16))
        window_sums[ci] = window_sum
        y = jnp.dot(jnp.concatenate(pooled, axis=0), pw_ref[ci].astype(BF16), preferred_element_type=F32)
        y = y * ps_ref[:, ci * LANES:(ci + 1) * LANES]
        gate = jnp.concatenate([zp_ref[ci, sb * SUB_TILES + i] for i in range(SUB_TILES)], axis=0).astype(F32)
        yp_scr[sb % 2, ci] = y * gate

    def rms(y, gain_ref):
        ms = jnp.mean(y * y, axis=-1, keepdims=True)
        return (y * lax.rsqrt(ms + EPS) * gain_ref[...]).astype(BF16)

    for ci in range(N_CBLK):
        pool_group(0, ci)
    ncol = D_MODEL // N_CBLK
    for sb in range(n_sub):
        j0 = sb * SUB_TILES
        yh = jnp.concatenate(
            [jnp.concatenate([yh_ref[cb, j] for cb in range(N_CBLK)], axis=-1) for j in range(j0, j0 + SUB_TILES)],
            axis=0).astype(F32)
        yp = jnp.concatenate([yp_scr[sb % 2, ci] for ci in range(N_CBLK)], axis=-1)
        yc = jnp.concatenate([rms(yh, gh_ref), rms(yp, gp_ref)], axis=-1)
        chunks = []
        for c in range(N_CBLK):
            chunks.append(jnp.dot(yc, w_ref[:, c * ncol:(c + 1) * ncol], preferred_element_type=F32))
            if sb + 1 < n_sub:
                pool_group(sb + 1, c)
        out = jnp.concatenate(chunks, axis=-1)
        ms = jnp.mean(out * out, axis=-1, keepdims=True)
        out = out * lax.rsqrt(ms + EPS) * gpost_ref[...]
        for j in range(SUB_TILES):
            for k in range(D_MODEL // LANES):
                r_scr[sb % 2, k, pl.ds(j, TILE_ROWS, stride=SUB_TILES), :] = out[j * TILE_ROWS:(j + 1) * TILE_ROWS,
                                                                             k * LANES:(k + 1) * LANES]
        r = jnp.concatenate([r_scr[sb % 2, k] for k in range(D_MODEL // LANES)], axis=-1)
        o_ref[:, j0:j0 + SUB_TILES, :] = (x_ref[:, j0:j0 + SUB_TILES, :]
                                          + r.reshape(TILE_ROWS, SUB_TILES, D_MODEL))
    for ci in range(N_CBLK):
        prev_scr[ci] = up_ref[ci, OUT_TILES - POOL_HALO:OUT_TILES]


def _out_proj(yh, p5, x4, pool_w, pool_scale, norm_h_g, norm_p_g, w_out, post_g):
    n_steps = N_TILES // OUT_TILES
    halo_blocks = N_TILES // POOL_HALO
    per_step = OUT_TILES // POOL_HALO
    u_group, z_group = 4, 5
    y_spec = pl.BlockSpec((None, N_CBLK, OUT_TILES, TILE_ROWS, LANES), lambda b, g: (b, 0, g, 0, 0))
    x_spec = pl.BlockSpec((None, TILE_ROWS, OUT_TILES, D_MODEL), lambda b, g: (b, 0, g, 0))
    full = lambda shape: pl.BlockSpec(shape, lambda b, g: (0,) * len(shape))
    return pl.pallas_call(
        _out_kernel,
        grid=(BATCH, n_steps),
        in_specs=[
            y_spec,
            pl.BlockSpec((None, N_CBLK, OUT_TILES, TILE_ROWS, LANES), lambda b, g: (b, u_group, g, 0, 0)),
            pl.BlockSpec((None, N_CBLK, POOL_HALO, TILE_ROWS, LANES),
                         lambda b, g: (b, u_group, halo_blocks - 1, 0, 0)),
            pl.BlockSpec((None, N_CBLK, POOL_HALO, TILE_ROWS, LANES),
                         lambda b, g: (b, u_group, (g * per_step + per_step) % halo_blocks, 0, 0)),
            pl.BlockSpec((None, N_CBLK, OUT_TILES, TILE_ROWS, LANES), lambda b, g: (b, z_group, g, 0, 0)),
            x_spec,
            full((N_CBLK, LANES, LANES)), full((1, POOL_WIDTH)),
            full((1, HYENA_WIDTH)), full((1, POOL_WIDTH)),
            pl.BlockSpec((D_MODEL, D_MODEL), lambda b, g: (0, 0), pipeline_mode=pl.Buffered(1)), full((1, D_MODEL)),
        ],
        out_specs=x_spec,
        out_shape=jax.ShapeDtypeStruct((BATCH, TILE_ROWS, N_TILES, D_MODEL), F32),
        scratch_shapes=[pltpu.VMEM((2, D_MODEL // LANES, TILE_ROWS * SUB_TILES, LANES), F32),
                        pltpu.VMEM((2, N_CBLK, TILE_ROWS * SUB_TILES, LANES), F32),
                        pltpu.VMEM((N_CBLK, POOL_HALO, TILE_ROWS, LANES), BF16)],
        compiler_params=pltpu.CompilerParams(
            dimension_semantics=("arbitrary", "arbitrary"), vmem_limit_bytes=OUT_VMEM_LIMIT),
        name="out_proj",
    )(yh, p5, p5, p5, p5, x4, pool_w, pool_scale, norm_h_g, norm_p_g, w_out, post_g)


def kernel(x, pre_norm_g, w_in, conv_w, conv_b, filt_w1, filt_b1, filt_w2, filt_b2, filt_w3, filt_b3,
           filt_freq, filt_w_out, hyena_d, pool_w, pool_scale, norm_h_g, norm_p_g, w_out, post_norm_g):
    assert x.shape == (BATCH, SEQ, D_MODEL) and pre_norm_g.shape[0] == 1
    f1, f1i, gf, gb, f1f = (jnp.asarray(m, F32).astype(BF16) for m in (_F1, _F1I, _GF, _GB, _F1F))

    x4 = x.reshape(BATCH, TILE_ROWS, N_TILES, D_MODEL)
    p5, taps = _in_proj(x4, pre_norm_g, w_in[0], jnp.asarray(_ZFEAT), filt_w1[0], filt_b1[0], filt_w2[0], filt_b2[0],
                        filt_w3[0], filt_b3[0], filt_freq[0], filt_w_out[0], jnp.asarray(_ABS_DELTAS))
    kspec = _filter_spec(taps, hyena_d[0], f1f, gf)

    yh = _hyena(p5, conv_w[0], conv_b, kspec, f1, f1i, gf, gb)
    out4 = _out_proj(yh, p5, x4, pool_w[0], pool_scale, norm_h_g, norm_p_g,
                     w_out[0].astype(BF16), post_norm_g)
    return out4.reshape(BATCH, SEQ, D_MODEL)
```

```python
import math

import numpy as np
import jax
import jax.numpy as jnp
from jax import lax
from jax.experimental import pallas as pl
from jax.experimental.pallas import tpu as pltpu

F32 = jnp.float32
BF16 = jnp.bfloat16

D_MODEL = 1024
BATCH = 4
SEQ = 8192
HYENA_WIDTH = 512
POOL_WIDTH = 512
POOL_WINDOWS = (2, 4, 8, 16)
FILTER_EMB = 33
FILTER_BANDS = 16
FILTER_HIDDEN = 64
PROJ_WIDTH = 3072
EPS = 1e-6

LANES = 128
N_FFT = 2 * SEQ
N_TILES = 128
TILE_ROWS = SEQ // N_TILES
N_SLOTS = 64
HY_SUB = 128
N_SUB = N_TILES // HY_SUB
A_STRIDE = 72
SLOT_BATCH = 2
N_CBLK = HYENA_WIDTH // LANES
PROJ_BLOCKS = PROJ_WIDTH // LANES
IN_TILES = 16
OUT_TILES = 32
SUB_TILES = 8
VMEM_LIMIT = 60 * 1024 * 1024
OUT_VMEM_LIMIT = 63 * 1024 * 1024


def _dft_tables():
    n1 = np.arange(TILE_ROWS)
    n2 = np.arange(N_TILES)
    s = np.arange(N_SLOTS)
    ph = 2 * np.pi * (n2[:, None, None] * s[None, :, None] / N_FFT
                      + n1[None, None, :] * s[None, :, None] / 128.0)
    f1 = np.zeros((N_TILES, 128, TILE_ROWS))
    f1[:, :64, :] = np.cos(ph)
    f1[:, 64:, :] = -np.sin(ph)
    f1[:, 0, :] = 1.0
    f1[:, 64, :] = (-1.0) ** n1
    php = np.transpose(ph, (0, 2, 1))
    f1i = np.zeros((N_TILES, TILE_ROWS, 128))
    f1i[:, :, :64] = 2 * np.cos(php) / N_FFT
    f1i[:, :, 64:] = -2 * np.sin(php) / N_FFT
    f1i[:, :, 0] = 1.0 / N_FFT
    f1i[:, :, 64] = ((-1.0) ** n1)[None, :] / N_FFT
    k2 = np.arange(128)
    th = 2 * np.pi * np.outer(k2, n2) / 128.0
    c, sn = np.cos(th), np.sin(th)
    g = np.block([[c, sn], [-sn, c]])
    gi = np.block([[c, -sn], [sn, c]])
    kk = np.arange(64)
    tha = 2 * np.pi * np.outer(kk, n2) / 128.0
    thb = 2 * np.pi * np.outer(64 + 128 * kk, n2) / N_FFT
    g0 = np.zeros((256, 256))
    g0[0:64, 0:128] = np.cos(tha)
    g0[64:128, 128:256] = np.cos(thb)
    g0[128:192, 0:128] = -np.sin(tha)
    g0[128, 0:128] = (-1.0) ** n2
    g0[192:256, 128:256] = -np.sin(thb)
    g0i = np.zeros((256, 256))
    g0i[0:128, 0:64] = 2 * np.cos(tha.T)
    g0i[0:128, 0] = 1.0
    g0i[0:128, 128:192] = -2 * np.sin(tha.T)
    g0i[0:128, 128] = (-1.0) ** n2
    g0i[128:256, 64:128] = 2 * np.cos(thb.T)
    g0i[128:256, 192:256] = -2 * np.sin(thb.T)
    q = np.arange(256)
    perm = np.where(q % 16 < 8, 8 * (q // 16) + q % 16, 128 + 8 * (q // 16) + q % 16 - 8)
    gf = np.stack([g0[perm, :], g[perm, :]])
    gb = np.stack([g0i[:, perm], gi[:, perm]])
    n1f = np.arange(128)
    phf = 2 * np.pi * (n2[:, None, None] * s[None, :, None] / N_FFT
                       + n1f[None, None, :] * s[None, :, None] / 128.0)
    full = np.zeros((N_TILES, 128, 128))
    full[:, :64, :] = np.cos(phf)
    full[:, 64:, :] = -np.sin(phf)
    full[:, 0, :] = 1.0
    full[:, 64, :] = (-1.0) ** n1f
    rev = full[:, :, 127:63:-1].copy()
    rev[0, :, 1:] = full[0, :, 127:64:-1]
    rev[0, :, 0] = 0.0
    f1f = np.concatenate([full[:, :, :64], rev], axis=2)
    pair64 = np.arange(128) % 2 * 64 + np.arange(128) // 2
    pair128 = np.arange(256) % 2 * 128 + np.arange(256) // 2
    f1, f1f = f1[:, pair64, :], f1f[:, pair64, :]
    f1i = f1i[:, :, pair64]
    gf = gf[:, :, pair128]
    gb = gb[:, pair128, :]
    return f1, f1i, gf, gb, f1f


def _filter_features():
    pos = np.arange(SEQ, dtype=np.float64)
    t = pos / (SEQ - 1)
    ang = 2.0 * math.pi * pos / SEQ
    bands = np.linspace(1e-4, FILTER_BANDS - 1, FILTER_BANDS)
    z = np.concatenate([t[:, None], np.cos(bands[None, :] * ang[:, None]),
                        -np.sin(bands[None, :] * ang[:, None])], axis=-1)
    z = z.reshape(TILE_ROWS, N_TILES, FILTER_EMB).transpose(1, 0, 2).reshape(SEQ, FILTER_EMB)
    zp = np.zeros((FILTER_HIDDEN, SEQ))
    zp[:FILTER_EMB, :] = z.T
    max_decay = math.log(1e-2) / 0.3
    min_decay = math.log(1e-2) / 1.5
    deltas = np.abs(np.linspace(min_decay, max_decay, HYENA_WIDTH))
    return zp.astype(np.float32), deltas.astype(np.float32)[None, :]


_F1, _F1I, _GF, _GB, _F1F = _dft_tables()
_ZFEAT, _ABS_DELTAS = _filter_features()


def _shift_down(x):
    rows = lax.broadcasted_iota(jnp.int32, x.shape, 0)
    return jnp.where(rows == 0, 0.0, pltpu.roll(x, 1, axis=0))


def _shift_up(x):
    rows = lax.broadcasted_iota(jnp.int32, x.shape, 0)
    return jnp.where(rows == x.shape[0] - 1, 0.0, pltpu.roll(x, x.shape[0] - 1, axis=0))


def _pair(ref, i):
    return jnp.concatenate([ref[0, i], ref[1, i]], axis=-1).astype(F32)


def _dup(x):
    return jnp.concatenate([x, x], axis=-1)


def _silu(z):
    hz = 0.5 * z
    return hz * (1.0 + jnp.tanh(hz))


def _tile_rows(n2):
    return pl.ds(pl.multiple_of(n2 * A_STRIDE, 8), N_SLOTS)


def _slot_rows(s):
    return pl.ds(s, N_TILES, stride=A_STRIDE)


def _stage_load(a_ref, rows):
    words = jnp.concatenate([a_ref[0, rows, :], a_ref[1, rows, :]], axis=-1)
    return pltpu.bitcast(words, BF16)


def _stage_store(a_ref, rows, val):
    words = pltpu.bitcast(val.astype(BF16), jnp.uint32)
    a_ref[0, rows, :] = words[:, :LANES]
    a_ref[1, rows, :] = words[:, LANES:]


N_BATCHES = N_SLOTS // SLOT_BATCH
SPEC_BLOCKS = 16


def _batch_slots(i):
    return [i * SLOT_BATCH + j for j in range(SLOT_BATCH)]


def _stage2_forward(a_ref, gf_ref, slots, sel0):
    xs = [_stage_load(a_ref, _slot_rows(s)) for s in slots]
    return [jnp.dot(gf_ref[sel0] if j == 0 else gf_ref[1], x, preferred_element_type=F32)
            for j, x in enumerate(xs)]


def _re_im_blocks(y):
    return [(y[16 * i:16 * i + 8], y[16 * i + 8:16 * i + 16]) for i in range(SPEC_BLOCKS)]


_GATE_GROUPS = (3, 5)
N_IN_STEPS = BATCH * (N_TILES // IN_TILES)
FILT_TILES = N_TILES // N_IN_STEPS


def _split_bf16(x):
    hi = x.astype(BF16)
    return hi, (x - hi.astype(F32)).astype(BF16)


def _dot_split(a, b):
    a_hi, a_lo = _split_bf16(a)
    b_hi, b_lo = _split_bf16(b)
    dot = lambda u, v: jnp.dot(u, v, preferred_element_type=F32)
    return dot(a_hi, b_hi) + dot(a_lo, b_hi) + dot(a_hi, b_lo)


def _first_step():
    return (pl.program_id(0) == 0) & (pl.program_id(1) == 0)


def _cast_weights_once(w_ref, w_scr, ncol):
    @pl.when(_first_step())
    def _():
        for c in range(w_ref.shape[1] // ncol):
            w_scr[:, c * ncol:(c + 1) * ncol] = w_ref[:, c * ncol:(c + 1) * ncol].astype(BF16)


class _FilterTaps:
    def __init__(self, z_ref, p_ref, wp_scr, dl_ref, first_tile, o_ref):
        self.p_ref, self.wp_scr, self.dl_ref, self.first_tile, self.o_ref = p_ref, wp_scr, dl_ref, first_tile, o_ref
        self.cols_blk = p_ref[3].T[:FILTER_HIDDEN]
        self.h = z_ref[...]
        self.n_stages = 5

    def stage(self, k):
        hid = FILTER_HIDDEN
        if k < 3:
            w_t = self.p_ref[k].T[:hid, :hid]
            freq = self.cols_blk[:, 3:4]
            self.h = jnp.sin(freq * (_dot_split(w_t, self.h) + self.cols_blk[:, k:k + 1]))
            return
        if k == 3:
            h_hi = self.h.astype(BF16).astype(F32)
            stack = jnp.concatenate([h_hi, self.h - h_hi, h_hi, jnp.zeros_like(h_hi)], axis=0)
            self.lhs = stack.T.astype(BF16)
            rows = FILT_TILES * TILE_ROWS
            r = lax.broadcasted_iota(jnp.int32, (rows, HYENA_WIDTH), 0)
            pos = 128 * (r % TILE_ROWS) + self.first_tile + r // TILE_ROWS
            t = pos.astype(F32) / float(SEQ - 1)
            self.decay = jnp.exp(-t * self.dl_ref[...])
        for od in ((0, 1) if k == 3 else (2, 3)):
            cols = slice(od * HYENA_WIDTH, (od + 1) * HYENA_WIDTH)
            taps = jnp.dot(self.lhs, self.wp_scr[:, cols], preferred_element_type=F32) * self.decay
            taps = taps.astype(BF16)
            for cb in range(N_CBLK):
                for i in range(FILT_TILES):
                    self.o_ref[od * N_CBLK + cb, i] = taps[i * TILE_ROWS:(i + 1) * TILE_ROWS,
                                                           cb * LANES:(cb + 1) * LANES]


def _in_proj_kernel(x_ref, g_ref, w32_ref, z_ref, p_ref, wp32_ref, dl_ref, o_ref, taps_ref, h_scr, w_ref, wp_scr):
    _cast_weights_once(w32_ref, w_ref, 512)

    @pl.when(_first_step())
    def _():
        wp_hi, wp_lo = _split_bf16(wp32_ref[...])
        wp_scr[...] = jnp.concatenate([wp_hi, wp_hi, wp_lo, jnp.zeros_like(wp_lo)], axis=0)

    step = pl.program_id(0) * (N_TILES // IN_TILES) + pl.program_id(1)
    filt = _FilterTaps(z_ref, p_ref, wp_scr, dl_ref, step * FILT_TILES, taps_ref)
    n_dots = 0

    ncol = 512
    for sb in range(IN_TILES // SUB_TILES):
        j0 = sb * SUB_TILES
        x = x_ref[:, j0:j0 + SUB_TILES, :].reshape(TILE_ROWS * SUB_TILES, D_MODEL)
        ms = jnp.mean(x * x, axis=-1, keepdims=True)
        hn = x * lax.rsqrt(ms + EPS) * g_ref[...]
        for k in range(D_MODEL // LANES):
            h_scr[sb, k] = hn[:, k * LANES:(k + 1) * LANES]
        h = jnp.concatenate(
            [jnp.concatenate([h_scr[sb, k, pl.ds(j, TILE_ROWS, stride=SUB_TILES), :]
                              for k in range(D_MODEL // LANES)], axis=-1).astype(BF16)
             for j in range(SUB_TILES)], axis=0)
        for c in range(PROJ_WIDTH // ncol):
            p = jnp.dot(h, w_ref[:, c * ncol:(c + 1) * ncol], preferred_element_type=F32)
            if (c * ncol) // HYENA_WIDTH in _GATE_GROUPS:
                p = _silu(p)
            p = p.astype(BF16)
            for cb in range(ncol // LANES):
                for j in range(SUB_TILES):
                    o_ref[c * (ncol // LANES) + cb, j0 + j] = p[j * TILE_ROWS:(j + 1) * TILE_ROWS,
                                                                cb * LANES:(cb + 1) * LANES]
            if n_dots % 2 == 0 and n_dots // 2 < filt.n_stages:
                filt.stage(n_dots // 2)
            n_dots += 1


def _in_proj(x4, pre_g, w_in, zfeat_t, w1, b1, w2, b2, w3, b3, freq, w_proj, abs_deltas):
    steps_per_batch = N_TILES // IN_TILES
    pad2 = lambda m: jnp.pad(m, ((0, LANES - m.shape[0]), (0, LANES - m.shape[1])))
    params = jnp.stack([pad2(w1), pad2(w2), pad2(w3), pad2(jnp.stack([b1, b2, b3, freq], axis=0))])
    full = lambda shape: pl.BlockSpec(shape, lambda b, g: (0,) * len(shape))
    once = lambda shape: pl.BlockSpec(shape, lambda b, g: (0,) * len(shape), pipeline_mode=pl.Buffered(1))
    return pl.pallas_call(
        _in_proj_kernel,
        grid=(BATCH, steps_per_batch),
        in_specs=[
            pl.BlockSpec((None, TILE_ROWS, IN_TILES, D_MODEL), lambda b, g: (b, 0, g, 0)),
            full((1, D_MODEL)),
            once((D_MODEL, PROJ_WIDTH)),
            pl.BlockSpec((FILTER_HIDDEN, FILT_TILES * TILE_ROWS), lambda b, g: (0, b * steps_per_batch + g)),
            full((4, LANES, LANES)),
            once((FILTER_HIDDEN, 4 * HYENA_WIDTH)),
            full((1, HYENA_WIDTH)),
        ],
        out_specs=[
            pl.BlockSpec((None, PROJ_BLOCKS, IN_TILES, TILE_ROWS, LANES), lambda b, g: (b, 0, g, 0, 0)),
            pl.BlockSpec((4 * N_CBLK, FILT_TILES, TILE_ROWS, LANES), lambda b, g: (0, b * steps_per_batch + g, 0, 0)),
        ],
        out_shape=[jax.ShapeDtypeStruct((BATCH, PROJ_BLOCKS, N_TILES, TILE_ROWS, LANES), BF16),
                   jax.ShapeDtypeStruct((4 * N_CBLK, N_TILES, TILE_ROWS, LANES), BF16)],
        scratch_shapes=[pltpu.VMEM((IN_TILES // SUB_TILES, D_MODEL // LANES, TILE_ROWS * SUB_TILES, LANES), F32),
                        pltpu.VMEM((D_MODEL, PROJ_WIDTH), BF16),
                        pltpu.VMEM((2 * LANES, 4 * HYENA_WIDTH), BF16)],
        compiler_params=pltpu.CompilerParams(
            dimension_semantics=("arbitrary", "arbitrary"), vmem_limit_bytes=VMEM_LIMIT),
        name="in_proj",
    )(x4, pre_g, w_in, zfeat_t, params, w_proj, abs_deltas)


def _filter_spec_kernel(fa_ref, fb_ref, ba_ref, bb_ref, d_ref, f1f_ref, gf_ref, k_ref, a_scr):
    def stage1(sb, carry):
        for i in range(HY_SUB):
            n2 = sb * HY_SUB + i
            nb = (N_TILES - n2) % N_TILES
            fwd = jnp.concatenate([fa_ref[n2], fb_ref[n2]], axis=-1)
            bwd = jnp.concatenate([ba_ref[nb], bb_ref[nb]], axis=-1)
            taps = jnp.concatenate([fwd, bwd], axis=0)
            _stage_store(a_scr, _tile_rows(n2), jnp.dot(f1f_ref[n2], taps, preferred_element_type=F32))
        return carry

    lax.fori_loop(0, N_SUB, stage1, 0)

    d = d_ref[pl.ds(pl.program_id(0), 1), :]

    def batch(slots, first):
        ys = _stage2_forward(a_scr, gf_ref, slots, 0 if first else 1)
        for j, (s, y) in enumerate(zip(slots, ys)):
            blocks = []
            for i, (yr, yi) in enumerate(_re_im_blocks(y)):
                yr = yr + d
                if first and j == 0 and i == 0:
                    rows = lax.broadcasted_iota(jnp.int32, yi.shape, 0)
                    yi = yi + jnp.where(rows == 0, d, 0.0)
                blocks += [yr, yi]
            spec = jnp.concatenate(blocks, axis=0).astype(BF16)
            k_ref[0, s] = spec[:, :LANES]
            k_ref[1, s] = spec[:, LANES:]

    batch(_batch_slots(0), True)

    def loop(i, carry):
        batch(_batch_slots(i), False)
        return carry

    lax.fori_loop(1, N_BATCHES, loop, 0, unroll=3)


def _filter_spec(taps, hyena_d, f1f, gf):
    grid = (2, N_CBLK // 2)
    const = lambda shape: pl.BlockSpec(shape, lambda o, c: (0,) * len(shape), pipeline_mode=pl.Buffered(1))

    def taps_spec(direction, k):
        return pl.BlockSpec((None, N_TILES, TILE_ROWS, LANES),
                            lambda o, c: ((2 * o + direction) * N_CBLK + 2 * c + k, 0, 0, 0))

    return pl.pallas_call(
        _filter_spec_kernel,
        grid=grid,
        in_specs=[taps_spec(0, 0), taps_spec(0, 1), taps_spec(1, 0), taps_spec(1, 1),
                  pl.BlockSpec((2, 2 * LANES), lambda o, c: (0, c)),
                  const((N_TILES, 128, 128)), const((2, 256, 256))],
        out_specs=pl.BlockSpec((None, 2, N_SLOTS, 256, LANES), lambda o, c: (o, c, 0, 0, 0)),
        out_shape=jax.ShapeDtypeStruct((2, N_CBLK, N_SLOTS, 256, LANES), BF16),
        scratch_shapes=[pltpu.VMEM((2, N_TILES * A_STRIDE, LANES), jnp.uint32)],
        compiler_params=pltpu.CompilerParams(
            dimension_semantics=("arbitrary", "arbitrary"), vmem_limit_bytes=VMEM_LIMIT),
        name="filter_spec",
    )(taps, taps, taps, taps, hyena_d, f1f, gf)


_T_S1 = 0
_T_F0 = 1
_T_M = 2
_T_F1 = 3
_T_E = 4
_T_END = 5
MID_GROUP = 8


def _hyena_kernel(cin_ref, pz_ref, cw_ref, k_ref, f1_ref, f1i_ref, gf_ref, gb_ref, o_ref, a_scr, z_scr):
    t = pl.program_id(2)

    def stage1(n2, u_bf):
        _stage_store(a_scr, _tile_rows(n2), jnp.dot(f1_ref[n2], u_bf, preferred_element_type=F32))

    def inv_stage1(n2):
        return jnp.dot(f1i_ref[n2], _stage_load(a_scr, _tile_rows(n2)), preferred_element_type=F32)

    def short_conv(sb, row0):
        base = sb * HY_SUB
        first = _pair(cin_ref, (base + N_TILES - 1) % N_TILES)
        first = jnp.where(sb == 0, _shift_down(first), first)
        last = _pair(cin_ref, (base + HY_SUB) % N_TILES)
        last = jnp.where(sb == N_SUB - 1, _shift_up(last), last)
        tiles = [first] + [_pair(cin_ref, base + i) for i in range(HY_SUB)] + [last]
        w = cw_ref[row0:row0 + 4, :]
        w0, w1, w2, b = _dup(w[0:1]), _dup(w[1:2]), _dup(w[2:3]), _dup(w[3:4])
        return [tiles[i] * w0 + tiles[i + 1] * w1 + tiles[i + 2] * w2 + b for i in range(HY_SUB)]

    def filter_multiply(slots, ys, first, buf):
        for j, (s, y) in enumerate(zip(slots, ys)):
            blocks = []
            for i, (yr, yi) in enumerate(_re_im_blocks(y)):
                kblk = k_ref[s, 16 * i:16 * i + 16, :].astype(F32)
                kr, ki = kblk[:8], kblk[8:]
                if first and j == 0 and i == 0:
                    rows = lax.broadcasted_iota(jnp.int32, kr.shape, 0)
                    ka, kb, kd = kr, jnp.where(rows == 0, 0.0, ki), jnp.where(rows == 0, ki, kr)
                else:
                    ka, kb, kd = kr, ki, kr
                ka, kb, kd = _dup(ka), _dup(kb), _dup(kd)
                blocks += [yr * ka - yi * kb, yr * kb + yi * kd]
            z_scr[buf, j] = jnp.concatenate(blocks, axis=0).astype(BF16)

    def stage2_inverse(slots, first_sel, buf):
        bms = [jnp.dot(gb_ref[first_sel] if j == 0 else gb_ref[1], z_scr[buf, j], preferred_element_type=F32)
               for j in range(SLOT_BATCH)]
        for s, bm in zip(slots, bms):
            _stage_store(a_scr, _slot_rows(s), bm)

    def spectral_phase():
        slots0 = _batch_slots(0)
        filter_multiply(slots0, _stage2_forward(a_scr, gf_ref, slots0, 0), True, 0)

        def loop(i, carry):
            slots = _batch_slots(i)
            ys = _stage2_forward(a_scr, gf_ref, slots, 1)
            stage2_inverse(_batch_slots(i - 1), jnp.where(i == 1, 0, 1), (i - 1) % 2)
            filter_multiply(slots, ys, False, i % 2)
            return carry

        lax.fori_loop(1, N_BATCHES, loop, 0, unroll=16)
        stage2_inverse(_batch_slots(N_BATCHES - 1), 1, (N_BATCHES - 1) % 2)

    @pl.when(t == _T_S1)
    def _():
        def sub(sb, carry):
            vs = short_conv(sb, 0)
            for i in range(HY_SUB):
                stage1(sb * HY_SUB + i, vs[i].astype(BF16))
            return carry

        lax.fori_loop(0, N_SUB, sub, 0)

    @pl.when((t == _T_F0) | (t == _T_F1))
    def _():
        spectral_phase()

    @pl.when(t == _T_M)
    def _():
        def sub(sb, carry):
            gates = short_conv(sb, 4)
            for q in range(0, HY_SUB, MID_GROUP):
                ys = [inv_stage1(sb * HY_SUB + q + i) for i in range(MID_GROUP)]
                us = [(gates[q + i] * ys[i]).astype(BF16) for i in range(MID_GROUP)]
                for i in range(MID_GROUP):
                    stage1(sb * HY_SUB + q + i, us[i])
            return carry

        lax.fori_loop(0, N_SUB, sub, 0)

    @pl.when(t == _T_E)
    def _():
        def sub(sb, carry):
            gates = short_conv(sb, 8)
            for i in range(HY_SUB):
                n2 = sb * HY_SUB + i
                res = (gates[i] * inv_stage1(n2) * _pair(pz_ref, n2)).astype(BF16)
                o_ref[0, n2] = res[:, :LANES]
                o_ref[1, n2] = res[:, LANES:]
            return carry

        lax.fori_loop(0, N_SUB, sub, 0)


def _hyena(p5, conv_w, conv_b, kspec, f1, f1i, gf, gb):
    grid = (N_CBLK, BATCH // 2, _T_END)
    cw = jnp.concatenate(
        [jnp.concatenate([conv_w[:, k * HYENA_WIDTH:(k + 1) * HYENA_WIDTH],
                          conv_b[:, k * HYENA_WIDTH:(k + 1) * HYENA_WIDTH]], axis=0) for k in range(3)], axis=0)
    seq_block = (2, None, N_TILES, TILE_ROWS, LANES)

    def conv_in_map(c, b, t):
        return (b, jnp.where(t < _T_M, 0, jnp.where(t < _T_E, 1, 2)) * N_CBLK + c, 0, 0, 0)

    def z_map(c, b, t):
        flat = c * (BATCH // 2) + b
        sel = jnp.where(t >= _T_M, flat, jnp.maximum(flat - 1, 0))
        return (sel % (BATCH // 2), 3 * N_CBLK + sel // (BATCH // 2), 0, 0, 0)

    const = lambda shape: pl.BlockSpec(shape, lambda c, b, t: (0,) * len(shape), pipeline_mode=pl.Buffered(1))
    in_specs = [
        pl.BlockSpec(seq_block, conv_in_map),
        pl.BlockSpec(seq_block, z_map),
        pl.BlockSpec((12, LANES), lambda c, b, t: (0, c)),
        pl.BlockSpec((None, None, N_SLOTS, 256, LANES), lambda c, b, t: (jnp.where(t >= _T_M, 1, 0), c, 0, 0, 0)),
        const((N_TILES, 128, TILE_ROWS)), const((N_TILES, TILE_ROWS, 128)),
        const((2, 256, 256)), const((2, 256, 256)),
    ]
    return pl.pallas_call(
        _hyena_kernel,
        grid=grid,
        in_specs=in_specs,
        out_specs=pl.BlockSpec(seq_block, lambda c, b, t: (b, c, 0, 0, 0)),
        out_shape=jax.ShapeDtypeStruct((BATCH, N_CBLK, N_TILES, TILE_ROWS, LANES), BF16),
        scratch_shapes=[pltpu.VMEM((2, N_TILES * A_STRIDE, LANES), jnp.uint32),
                        pltpu.VMEM((2, SLOT_BATCH, 256, 2 * LANES), BF16)],
        compiler_params=pltpu.CompilerParams(
            dimension_semantics=("arbitrary", "arbitrary", "arbitrary"), vmem_limit_bytes=VMEM_LIMIT),
        name="hyena",
    )(p5, p5, cw, kspec, f1, f1i, gf, gb)


POOL_HALO = 8


def _out_kernel(yh_ref, up_ref, upw_ref, upn_ref, zp_ref, x_ref, pw_ref, ps_ref, gh_ref, gp_ref, w_ref, gpost_ref,
                o_ref, r_scr, yp_scr, prev_scr):
    @pl.when(_first_step())
    def _():
        prev_scr[...] = jnp.zeros_like(prev_scr)

    g = pl.program_id(1)
    last_step = N_TILES // OUT_TILES - 1
    row = lax.broadcasted_iota(jnp.int32, (TILE_ROWS, LANES), 0)

    def pool_tile(ci, l):
        if l < 0:
            wrapped = _shift_down(upw_ref[ci, POOL_HALO + l].astype(F32))
            return jnp.where(g == 0, wrapped, prev_scr[ci, POOL_HALO + l].astype(F32))
        if l >= OUT_TILES:
            t = upn_ref[ci, l - OUT_TILES].astype(F32)
            return jnp.where(g == last_step, _shift_up(t), t)
        return up_ref[ci, l].astype(F32)

    n_sub = OUT_TILES // SUB_TILES
    window_sums = [None] * len(POOL_WINDOWS)

    def pool_group(sb, ci):
        w = POOL_WINDOWS[ci]
        lo, hi = w // 2, w - 1 - w // 2
        if sb == 0:
            window_sum = pool_tile(ci, -lo)
            for d in range(-lo + 1, hi + 1):
                window_sum = window_sum + pool_tile(ci, d)
        else:
            window_sum = window_sums[ci]
        pooled = []
        for i in range(SUB_TILES):
            l = sb * SUB_TILES + i
            if l > 0:
                window_sum = window_sum + pool_tile(ci, l + hi) - pool_tile(ci, l - 1 - lo)
            inv_cnt = 1.0 / w
            if l < lo:
                inv_cnt = jnp.where((g == 0) & (row == 0), 1.0 / (hi + l + 1), inv_cnt)
            elif l > OUT_TILES - 1 - hi:
                inv_cnt = jnp.where((g == last_step) & (row == TILE_ROWS - 1),
                                    1.0 / (lo + OUT_TILES - l), inv_cnt)
            pooled.append((window_sum * inv_cnt - pool_tile(ci, l)).astype(BF16))
        window_sums[ci] = window_sum
        y = jnp.dot(jnp.concatenate(pooled, axis=0), pw_ref[ci].astype(BF16), preferred_element_type=F32)
        y = y * ps_ref[:, ci * LANES:(ci + 1) * LANES]
        gate = jnp.concatenate([zp_ref[ci, sb * SUB_TILES + i] for i in range(SUB_TILES)], axis=0).astype(F32)
        yp_scr[sb % 2, ci] = y * gate

    def rms(y, gain_ref):
        ms = jnp.mean(y * y, axis=-1, keepdims=True)
        return (y * lax.rsqrt(ms + EPS) * gain_ref[...]).astype(BF16)

    for ci in range(N_CBLK):
        pool_group(0, ci)
    ncol = D_MODEL // N_CBLK
    for sb in range(n_sub):
        j0 = sb * SUB_TILES
        yh = jnp.concatenate(
            [jnp.concatenate([yh_ref[cb, j] for cb in range(N_CBLK)], axis=-1) for j in range(j0, j0 + SUB_TILES)],
            axis=0).astype(F32)
        yp = jnp.concatenate([yp_scr[sb % 2, ci] for ci in range(N_CBLK)], axis=-1)
        yc = jnp.concatenate([rms(yh, gh_ref), rms(yp, gp_ref)], axis=-1)
        chunks = []
        for c in range(N_CBLK):
            chunks.append(jnp.dot(yc, w_ref[:, c * ncol:(c + 1) * ncol], preferred_element_type=F32))
            if sb + 1 < n_sub:
                pool_group(sb + 1, c)
        out = jnp.concatenate(chunks, axis=-1)
        ms = jnp.mean(out * out, axis=-1, keepdims=True)
        out = out * lax.rsqrt(ms + EPS) * gpost_ref[...]
        for j in range(SUB_TILES):
            for k in range(D_MODEL // LANES):
                r_scr[sb % 2, k, pl.ds(j, TILE_ROWS, stride=SUB_TILES), :] = out[j * TILE_ROWS:(j + 1) * TILE_ROWS,
                                                                             k * LANES:(k + 1) * LANES]
        r = jnp.concatenate([r_scr[sb % 2, k] for k in range(D_MODEL // LANES)], axis=-1)
        o_ref[:, j0:j0 + SUB_TILES, :] = (x_ref[:, j0:j0 + SUB_TILES, :]
                                          + r.reshape(TILE_ROWS, SUB_TILES, D_MODEL))
    for ci in range(N_CBLK):
        prev_scr[ci] = up_ref[ci, OUT_TILES - POOL_HALO:OUT_TILES]


def _out_proj(yh, p5, x4, pool_w, pool_scale, norm_h_g, norm_p_g, w_out, post_g):
    n_steps = N_TILES // OUT_TILES
    halo_blocks = N_TILES // POOL_HALO
    per_step = OUT_TILES // POOL_HALO
    u_group, z_group = 4, 5
    y_spec = pl.BlockSpec((None, N_CBLK, OUT_TILES, TILE_ROWS, LANES), lambda b, g: (b, 0, g, 0, 0))
    x_spec = pl.BlockSpec((None, TILE_ROWS, OUT_TILES, D_MODEL), lambda b, g: (b, 0, g, 0))
    full = lambda shape: pl.BlockSpec(shape, lambda b, g: (0,) * len(shape))
    return pl.pallas_call(
        _out_kernel,
        grid=(BATCH, n_steps),
        in_specs=[
            y_spec,
            pl.BlockSpec((None, N_CBLK, OUT_TILES, TILE_ROWS, LANES), lambda b, g: (b, u_group, g, 0, 0)),
            pl.BlockSpec((None, N_CBLK, POOL_HALO, TILE_ROWS, LANES),
                         lambda b, g: (b, u_group, halo_blocks - 1, 0, 0)),
            pl.BlockSpec((None, N_CBLK, POOL_HALO, TILE_ROWS, LANES),
                         lambda b, g: (b, u_group, (g * per_step + per_step) % halo_blocks, 0, 0)),
            pl.BlockSpec((None, N_CBLK, OUT_TILES, TILE_ROWS, LANES), lambda b, g: (b, z_group, g, 0, 0)),
            x_spec,
            full((N_CBLK, LANES, LANES)), full((1, POOL_WIDTH)),
            full((1, HYENA_WIDTH)), full((1, POOL_WIDTH)),
            pl.BlockSpec((D_MODEL, D_MODEL), lambda b, g: (0, 0), pipeline_mode=pl.Buffered(1)), full((1, D_MODEL)),
        ],
        out_specs=x_spec,
        out_shape=jax.ShapeDtypeStruct((BATCH, TILE_ROWS, N_TILES, D_MODEL), F32),
        scratch_shapes=[pltpu.VMEM((2, D_MODEL // LANES, TILE_ROWS * SUB_TILES, LANES), F32),
                        pltpu.VMEM((2, N_CBLK, TILE_ROWS * SUB_TILES, LANES), F32),
                        pltpu.VMEM((N_CBLK, POOL_HALO, TILE_ROWS, LANES), BF16)],
        compiler_params=pltpu.CompilerParams(
            dimension_semantics=("arbitrary", "arbitrary"), vmem_limit_bytes=OUT_VMEM_LIMIT),
        name="out_proj",
    )(yh, p5, p5, p5, p5, x4, pool_w, pool_scale, norm_h_g, norm_p_g, w_out, post_g)


def kernel(x, pre_norm_g, w_in, conv_w, conv_b, filt_w1, filt_b1, filt_w2, filt_b2, filt_w3, filt_b3,
           filt_freq, filt_w_out, hyena_d, pool_w, pool_scale, norm_h_g, norm_p_g, w_out, post_norm_g):
    assert x.shape == (BATCH, SEQ, D_MODEL) and pre_norm_g.shape[0] == 1
    f1, f1i, gf, gb, f1f = (jnp.asarray(m, F32).astype(BF16) for m in (_F1, _F1I, _GF, _GB, _F1F))

    x4 = x.reshape(BATCH, TILE_ROWS, N_TILES, D_MODEL)
    p5, taps = _in_proj(x4, pre_norm_g, w_in[0], jnp.asarray(_ZFEAT), filt_w1[0], filt_b1[0], filt_w2[0], filt_b2[0],
                        filt_w3[0], filt_b3[0], filt_freq[0], filt_w_out[0], jnp.asarray(_ABS_DELTAS))
    kspec = _filter_spec(taps, hyena_d[0], f1f, gf)

    yh = _hyena(p5, conv_w[0], conv_b, kspec, f1, f1i, gf, gb)
    out4 = _out_proj(yh, p5, x4, pool_w[0], pool_scale, norm_h_g, norm_p_g,
                     w_out[0].astype(BF16), post_norm_g)
    return out4.reshape(BATCH, SEQ, D_MODEL)
```

```python
import math

import numpy as np
import jax
import jax.numpy as jnp
from jax import lax
from jax.experimental import pallas as pl
from jax.experimental.pallas import tpu as pltpu

F32 = jnp.float32
BF16 = jnp.bfloat16

D_MODEL = 1024
BATCH = 4
SEQ = 8192
HYENA_WIDTH = 512
POOL_WIDTH = 512
POOL_WINDOWS = (2, 4, 8, 16)
FILTER_EMB = 33
FILTER_BANDS = 16
FILTER_HIDDEN = 64
PROJ_WIDTH = 3072
EPS = 1e-6

LANES = 128
N_FFT = 2 * SEQ
N_TILES = 128
TILE_ROWS = SEQ // N_TILES
N_SLOTS = 64
HY_SUB = 128
N_SUB = N_TILES // HY_SUB
A_STRIDE = 72
SLOT_BATCH = 1
N_CBLK = HYENA_WIDTH // LANES
PROJ_BLOCKS = PROJ_WIDTH // LANES
IN_TILES = 16
OUT_TILES = 32
SUB_TILES = 8
VMEM_LIMIT = 60 * 1024 * 1024
OUT_VMEM_LIMIT = 63 * 1024 * 1024


def _dft_tables():
    n1 = np.arange(TILE_ROWS)
    n2 = np.arange(N_TILES)
    s = np.arange(N_SLOTS)
    ph = 2 * np.pi * (n2[:, None, None] * s[None, :, None] / N_FFT
                      + n1[None, None, :] * s[None, :, None] / 128.0)
    f1 = np.zeros((N_TILES, 128, TILE_ROWS))
    f1[:, :64, :] = np.cos(ph)
    f1[:, 64:, :] = -np.sin(ph)
    f1[:, 0, :] = 1.0
    f1[:, 64, :] = (-1.0) ** n1
    php = np.transpose(ph, (0, 2, 1))
    f1i = np.zeros((N_TILES, TILE_ROWS, 128))
    f1i[:, :, :64] = 2 * np.cos(php) / N_FFT
    f1i[:, :, 64:] = -2 * np.sin(php) / N_FFT
    f1i[:, :, 0] = 1.0 / N_FFT
    f1i[:, :, 64] = ((-1.0) ** n1)[None, :] / N_FFT
    k2 = np.arange(128)
    th = 2 * np.pi * np.outer(k2, n2) / 128.0
    c, sn = np.cos(th), np.sin(th)
    g = np.block([[c, sn], [-sn, c]])
    gi = np.block([[c, -sn], [sn, c]])
    kk = np.arange(64)
    tha = 2 * np.pi * np.outer(kk, n2) / 128.0
    thb = 2 * np.pi * np.outer(64 + 128 * kk, n2) / N_FFT
    g0 = np.zeros((256, 256))
    g0[0:64, 0:128] = np.cos(tha)
    g0[64:128, 128:256] = np.cos(thb)
    g0[128:192, 0:128] = -np.sin(tha)
    g0[128, 0:128] = (-1.0) ** n2
    g0[192:256, 128:256] = -np.sin(thb)
    g0i = np.zeros((256, 256))
    g0i[0:128, 0:64] = 2 * np.cos(tha.T)
    g0i[0:128, 0] = 1.0
    g0i[0:128, 128:192] = -2 * np.sin(tha.T)
    g0i[0:128, 128] = (-1.0) ** n2
    g0i[128:256, 64:128] = 2 * np.cos(thb.T)
    g0i[128:256, 192:256] = -2 * np.sin(thb.T)
    q = np.arange(256)
    perm = np.where(q % 16 < 8, 8 * (q // 16) + q % 16, 128 + 8 * (q // 16) + q % 16 - 8)
    gf = np.stack([g0[perm, :], g[perm, :]])
    gb = np.stack([g0i[:, perm], gi[:, perm]])
    n1f = np.arange(128)
    phf = 2 * np.pi * (n2[:, None, None] * s[None, :, None] / N_FFT
                       + n1f[None, None, :] * s[None, :, None] / 128.0)
    full = np.zeros((N_TILES, 128, 128))
    full[:, :64, :] = np.cos(phf)
    full[:, 64:, :] = -np.sin(phf)
    full[:, 0, :] = 1.0
    full[:, 64, :] = (-1.0) ** n1f
    rev = full[:, :, 127:63:-1].copy()
    rev[0, :, 1:] = full[0, :, 127:64:-1]
    rev[0, :, 0] = 0.0
    f1f = np.concatenate([full[:, :, :64], rev], axis=2)
    pair64 = np.arange(128) % 2 * 64 + np.arange(128) // 2
    pair128 = np.arange(256) % 2 * 128 + np.arange(256) // 2
    f1, f1f = f1[:, pair64, :], f1f[:, pair64, :]
    f1i = f1i[:, :, pair64]
    gf = gf[:, :, pair128]
    gb = gb[:, pair128, :]
    return f1, f1i, gf, gb, f1f


def _filter_features():
    pos = np.arange(SEQ, dtype=np.float64)
    t = pos / (SEQ - 1)
    ang = 2.0 * math.pi * pos / SEQ
    bands = np.linspace(1e-4, FILTER_BANDS - 1, FILTER_BANDS)
    z = np.concatenate([t[:, None], np.cos(bands[None, :] * ang[:, None]),
                        -np.sin(bands[None, :] * ang[:, None])], axis=-1)
    z = z.reshape(TILE_ROWS, N_TILES, FILTER_EMB).transpose(1, 0, 2).reshape(SEQ, FILTER_EMB)
    zp = np.zeros((FILTER_HIDDEN, SEQ))
    zp[:FILTER_EMB, :] = z.T
    max_decay = math.log(1e-2) / 0.3
    min_decay = math.log(1e-2) / 1.5
    deltas = np.abs(np.linspace(min_decay, max_decay, HYENA_WIDTH))
    return zp.astype(np.float32), deltas.astype(np.float32)[None, :]


_F1, _F1I, _GF, _GB, _F1F = _dft_tables()
_ZFEAT, _ABS_DELTAS = _filter_features()


def _shift_down(x):
    rows = lax.broadcasted_iota(jnp.int32, x.shape, 0)
    return jnp.where(rows == 0, 0.0, pltpu.roll(x, 1, axis=0))


def _shift_up(x):
    rows = lax.broadcasted_iota(jnp.int32, x.shape, 0)
    return jnp.where(rows == x.shape[0] - 1, 0.0, pltpu.roll(x, x.shape[0] - 1, axis=0))


def _pair(ref, i):
    return jnp.concatenate([ref[0, i], ref[1, i]], axis=-1).astype(F32)


def _dup(x):
    return jnp.concatenate([x, x], axis=-1)


def _silu(z):
    hz = 0.5 * z
    return hz * (1.0 + jnp.tanh(hz))


def _tile_rows(n2):
    return pl.ds(pl.multiple_of(n2 * A_STRIDE, 8), N_SLOTS)


def _slot_rows(s):
    return pl.ds(s, N_TILES, stride=A_STRIDE)


def _stage_load(a_ref, rows):
    words = jnp.concatenate([a_ref[0, rows, :], a_ref[1, rows, :]], axis=-1)
    return pltpu.bitcast(words, BF16)


def _stage_store(a_ref, rows, val):
    words = pltpu.bitcast(val.astype(BF16), jnp.uint32)
    a_ref[0, rows, :] = words[:, :LANES]
    a_ref[1, rows, :] = words[:, LANES:]


N_BATCHES = N_SLOTS // SLOT_BATCH
SPEC_BLOCKS = 16


def _batch_slots(i):
    return [i * SLOT_BATCH + j for j in range(SLOT_BATCH)]


def _stage2_forward(a_ref, gf_ref, slots, sel0):
    xs = [_stage_load(a_ref, _slot_rows(s)) for s in slots]
    return [jnp.dot(gf_ref[sel0] if j == 0 else gf_ref[1], x, preferred_element_type=F32)
            for j, x in enumerate(xs)]


def _re_im_blocks(y):
    return [(y[16 * i:16 * i + 8], y[16 * i + 8:16 * i + 16]) for i in range(SPEC_BLOCKS)]


_GATE_GROUPS = (3, 5)
N_IN_STEPS = BATCH * (N_TILES // IN_TILES)
FILT_TILES = N_TILES // N_IN_STEPS


def _split_bf16(x):
    hi = x.astype(BF16)
    return hi, (x - hi.astype(F32)).astype(BF16)


def _dot_split(a, b):
    a_hi, a_lo = _split_bf16(a)
    b_hi, b_lo = _split_bf16(b)
    dot = lambda u, v: jnp.dot(u, v, preferred_element_type=F32)
    return dot(a_hi, b_hi) + dot(a_lo, b_hi) + dot(a_hi, b_lo)


def _first_step():
    return (pl.program_id(0) == 0) & (pl.program_id(1) == 0)


def _cast_weights_once(w_ref, w_scr, ncol):
    @pl.when(_first_step())
    def _():
        for c in range(w_ref.shape[1] // ncol):
            w_scr[:, c * ncol:(c + 1) * ncol] = w_ref[:, c * ncol:(c + 1) * ncol].astype(BF16)


class _FilterTaps:
    def __init__(self, z_ref, p_ref, wp_scr, dl_ref, first_tile, o_ref):
        self.p_ref, self.wp_scr, self.dl_ref, self.first_tile, self.o_ref = p_ref, wp_scr, dl_ref, first_tile, o_ref
        self.cols_blk = p_ref[3].T[:FILTER_HIDDEN]
        self.h = z_ref[...]
        self.n_stages = 5

    def stage(self, k):
        hid = FILTER_HIDDEN
        if k < 3:
            w_t = self.p_ref[k].T[:hid, :hid]
            freq = self.cols_blk[:, 3:4]
            self.h = jnp.sin(freq * (_dot_split(w_t, self.h) + self.cols_blk[:, k:k + 1]))
            return
        if k == 3:
            h_hi = self.h.astype(BF16).astype(F32)
            stack = jnp.concatenate([h_hi, self.h - h_hi, h_hi, jnp.zeros_like(h_hi)], axis=0)
            self.lhs = stack.T.astype(BF16)
            rows = FILT_TILES * TILE_ROWS
            r = lax.broadcasted_iota(jnp.int32, (rows, HYENA_WIDTH), 0)
            pos = 128 * (r % TILE_ROWS) + self.first_tile + r // TILE_ROWS
            t = pos.astype(F32) / float(SEQ - 1)
            self.decay = jnp.exp(-t * self.dl_ref[...])
        for od in ((0, 1) if k == 3 else (2, 3)):
            cols = slice(od * HYENA_WIDTH, (od + 1) * HYENA_WIDTH)
            taps = jnp.dot(self.lhs, self.wp_scr[:, cols], preferred_element_type=F32) * self.decay
            taps = taps.astype(BF16)
            for cb in range(N_CBLK):
                for i in range(FILT_TILES):
                    self.o_ref[od * N_CBLK + cb, i] = taps[i * TILE_ROWS:(i + 1) * TILE_ROWS,
                                                           cb * LANES:(cb + 1) * LANES]


def _in_proj_kernel(x_ref, g_ref, w32_ref, z_ref, p_ref, wp32_ref, dl_ref, o_ref, taps_ref, h_scr, w_ref, wp_scr):
    _cast_weights_once(w32_ref, w_ref, 512)

    @pl.when(_first_step())
    def _():
        wp_hi, wp_lo = _split_bf16(wp32_ref[...])
        wp_scr[...] = jnp.concatenate([wp_hi, wp_hi, wp_lo, jnp.zeros_like(wp_lo)], axis=0)

    step = pl.program_id(0) * (N_TILES // IN_TILES) + pl.program_id(1)
    filt = _FilterTaps(z_ref, p_ref, wp_scr, dl_ref, step * FILT_TILES, taps_ref)
    n_dots = 0

    ncol = 512
    for sb in range(IN_TILES // SUB_TILES):
        j0 = sb * SUB_TILES
        x = x_ref[:, j0:j0 + SUB_TILES, :].reshape(TILE_ROWS * SUB_TILES, D_MODEL)
        ms = jnp.mean(x * x, axis=-1, keepdims=True)
        hn = x * lax.rsqrt(ms + EPS) * g_ref[...]
        for k in range(D_MODEL // LANES):
            h_scr[sb, k] = hn[:, k * LANES:(k + 1) * LANES]
        h = jnp.concatenate(
            [jnp.concatenate([h_scr[sb, k, pl.ds(j, TILE_ROWS, stride=SUB_TILES), :]
                              for k in range(D_MODEL // LANES)], axis=-1).astype(BF16)
             for j in range(SUB_TILES)], axis=0)
        for c in range(PROJ_WIDTH // ncol):
            p = jnp.dot(h, w_ref[:, c * ncol:(c + 1) * ncol], preferred_element_type=F32)
            if (c * ncol) // HYENA_WIDTH in _GATE_GROUPS:
                p = _silu(p)
            p = p.astype(BF16)
            for cb in range(ncol // LANES):
                for j in range(SUB_TILES):
                    o_ref[c * (ncol // LANES) + cb, j0 + j] = p[j * TILE_ROWS:(j + 1) * TILE_ROWS,
                                                                cb * LANES:(cb + 1) * LANES]
            if n_dots % 2 == 0 and n_dots // 2 < filt.n_stages:
                filt.stage(n_dots // 2)
            n_dots += 1


def _in_proj(x4, pre_g, w_in, zfeat_t, w1, b1, w2, b2, w3, b3, freq, w_proj, abs_deltas):
    steps_per_batch = N_TILES // IN_TILES
    pad2 = lambda m: jnp.pad(m, ((0, LANES - m.shape[0]), (0, LANES - m.shape[1])))
    params = jnp.stack([pad2(w1), pad2(w2), pad2(w3), pad2(jnp.stack([b1, b2, b3, freq], axis=0))])
    full = lambda shape: pl.BlockSpec(shape, lambda b, g: (0,) * len(shape))
    once = lambda shape: pl.BlockSpec(shape, lambda b, g: (0,) * len(shape), pipeline_mode=pl.Buffered(1))
    return pl.pallas_call(
        _in_proj_kernel,
        grid=(BATCH, steps_per_batch),
        in_specs=[
            pl.BlockSpec((None, TILE_ROWS, IN_TILES, D_MODEL), lambda b, g: (b, 0, g, 0)),
            full((1, D_MODEL)),
            once((D_MODEL, PROJ_WIDTH)),
            pl.BlockSpec((FILTER_HIDDEN, FILT_TILES * TILE_ROWS), lambda b, g: (0, b * steps_per_batch + g)),
            full((4, LANES, LANES)),
            once((FILTER_HIDDEN, 4 * HYENA_WIDTH)),
            full((1, HYENA_WIDTH)),
        ],
        out_specs=[
            pl.BlockSpec((None, PROJ_BLOCKS, IN_TILES, TILE_ROWS, LANES), lambda b, g: (b, 0, g, 0, 0)),
            pl.BlockSpec((4 * N_CBLK, FILT_TILES, TILE_ROWS, LANES), lambda b, g: (0, b * steps_per_batch + g, 0, 0)),
        ],
        out_shape=[jax.ShapeDtypeStruct((BATCH, PROJ_BLOCKS, N_TILES, TILE_ROWS, LANES), BF16),
                   jax.ShapeDtypeStruct((4 * N_CBLK, N_TILES, TILE_ROWS, LANES), BF16)],
        scratch_shapes=[pltpu.VMEM((IN_TILES // SUB_TILES, D_MODEL // LANES, TILE_ROWS * SUB_TILES, LANES), F32),
                        pltpu.VMEM((D_MODEL, PROJ_WIDTH), BF16),
                        pltpu.VMEM((2 * LANES, 4 * HYENA_WIDTH), BF16)],
        compiler_params=pltpu.CompilerParams(
            dimension_semantics=("arbitrary", "arbitrary"), vmem_limit_bytes=VMEM_LIMIT),
        name="in_proj",
    )(x4, pre_g, w_in, zfeat_t, params, w_proj, abs_deltas)


def _filter_spec_kernel(fa_ref, fb_ref, ba_ref, bb_ref, d_ref, f1f_ref, gf_ref, k_ref, a_scr):
    def stage1(sb, carry):
        for i in range(HY_SUB):
            n2 = sb * HY_SUB + i
            nb = (N_TILES - n2) % N_TILES
            fwd = jnp.concatenate([fa_ref[n2], fb_ref[n2]], axis=-1)
            bwd = jnp.concatenate([ba_ref[nb], bb_ref[nb]], axis=-1)
            taps = jnp.concatenate([fwd, bwd], axis=0)
            _stage_store(a_scr, _tile_rows(n2), jnp.dot(f1f_ref[n2], taps, preferred_element_type=F32))
        return carry

    lax.fori_loop(0, N_SUB, stage1, 0)

    d = d_ref[pl.ds(pl.program_id(0), 1), :]

    def batch(slots, first):
        ys = _stage2_forward(a_scr, gf_ref, slots, 0 if first else 1)
        for j, (s, y) in enumerate(zip(slots, ys)):
            blocks = []
            for i, (yr, yi) in enumerate(_re_im_blocks(y)):
                yr = yr + d
                if first and j == 0 and i == 0:
                    rows = lax.broadcasted_iota(jnp.int32, yi.shape, 0)
                    yi = yi + jnp.where(rows == 0, d, 0.0)
                blocks += [yr, yi]
            spec = jnp.concatenate(blocks, axis=0).astype(BF16)
            k_ref[0, s] = spec[:, :LANES]
            k_ref[1, s] = spec[:, LANES:]

    batch(_batch_slots(0), True)

    def loop(i, carry):
        batch(_batch_slots(i), False)
        return carry

    lax.fori_loop(1, N_BATCHES, loop, 0, unroll=N_BATCHES // 2)


def _filter_spec(taps, hyena_d, f1f, gf):
    grid = (2, N_CBLK // 2)
    const = lambda shape: pl.BlockSpec(shape, lambda o, c: (0,) * len(shape), pipeline_mode=pl.Buffered(1))

    def taps_spec(direction, k):
        return pl.BlockSpec((None, N_TILES, TILE_ROWS, LANES),
                            lambda o, c: ((2 * o + direction) * N_CBLK + 2 * c + k, 0, 0, 0))

    return pl.pallas_call(
        _filter_spec_kernel,
        grid=grid,
        in_specs=[taps_spec(0, 0), taps_spec(0, 1), taps_spec(1, 0), taps_spec(1, 1),
                  pl.BlockSpec((2, 2 * LANES), lambda o, c: (0, c)),
                  const((N_TILES, 128, 128)), const((2, 256, 256))],
        out_specs=pl.BlockSpec((None, 2, N_SLOTS, 256, LANES), lambda o, c: (o, c, 0, 0, 0)),
        out_shape=jax.ShapeDtypeStruct((2, N_CBLK, N_SLOTS, 256, LANES), BF16),
        scratch_shapes=[pltpu.VMEM((2, N_TILES * A_STRIDE, LANES), jnp.uint32)],
        compiler_params=pltpu.CompilerParams(
            dimension_semantics=("arbitrary", "arbitrary"), vmem_limit_bytes=VMEM_LIMIT),
        name="filter_spec",
    )(taps, taps, taps, taps, hyena_d, f1f, gf)


_T_S1 = 0
_T_F0 = 1
_T_M = 2
_T_F1 = 3
_T_E = 4
_T_END = 5
MID_GROUP = 8


def _hyena_kernel(cin_ref, pz_ref, cw_ref, k_ref, f1_ref, f1i_ref, gf_ref, gb_ref, o_ref, a_scr, z_scr):
    t = pl.program_id(2)

    def stage1(n2, u_bf):
        _stage_store(a_scr, _tile_rows(n2), jnp.dot(f1_ref[n2], u_bf, preferred_element_type=F32))

    def inv_stage1(n2):
        return jnp.dot(f1i_ref[n2], _stage_load(a_scr, _tile_rows(n2)), preferred_element_type=F32)

    def short_conv(sb, row0):
        base = sb * HY_SUB
        first = _pair(cin_ref, (base + N_TILES - 1) % N_TILES)
        first = jnp.where(sb == 0, _shift_down(first), first)
        last = _pair(cin_ref, (base + HY_SUB) % N_TILES)
        last = jnp.where(sb == N_SUB - 1, _shift_up(last), last)
        tiles = [first] + [_pair(cin_ref, base + i) for i in range(HY_SUB)] + [last]
        w = cw_ref[row0:row0 + 4, :]
        w0, w1, w2, b = _dup(w[0:1]), _dup(w[1:2]), _dup(w[2:3]), _dup(w[3:4])
        return [tiles[i] * w0 + tiles[i + 1] * w1 + tiles[i + 2] * w2 + b for i in range(HY_SUB)]

    def filter_multiply(slots, ys, first, buf):
        for j, (s, y) in enumerate(zip(slots, ys)):
            blocks = []
            for i, (yr, yi) in enumerate(_re_im_blocks(y)):
                kblk = k_ref[s, 16 * i:16 * i + 16, :].astype(F32)
                kr, ki = kblk[:8], kblk[8:]
                if first and j == 0 and i == 0:
                    rows = lax.broadcasted_iota(jnp.int32, kr.shape, 0)
                    ka, kb, kd = kr, jnp.where(rows == 0, 0.0, ki), jnp.where(rows == 0, ki, kr)
                else:
                    ka, kb, kd = kr, ki, kr
                ka, kb, kd = _dup(ka), _dup(kb), _dup(kd)
                blocks += [yr * ka - yi * kb, yr * kb + yi * kd]
            z_scr[buf, j] = jnp.concatenate(blocks, axis=0).astype(BF16)

    def stage2_inverse(slots, first_sel, buf):
        bms = [jnp.dot(gb_ref[first_sel] if j == 0 else gb_ref[1], z_scr[buf, j], preferred_element_type=F32)
               for j in range(SLOT_BATCH)]
        for s, bm in zip(slots, bms):
            _stage_store(a_scr, _slot_rows(s), bm)

    def spectral_phase():
        slots0 = _batch_slots(0)
        filter_multiply(slots0, _stage2_forward(a_scr, gf_ref, slots0, 0), True, 0)

        def loop(i, carry):
            slots = _batch_slots(i)
            ys = _stage2_forward(a_scr, gf_ref, slots, 1)
            stage2_inverse(_batch_slots(i - 1), jnp.where(i == 1, 0, 1), (i - 1) % 2)
            filter_multiply(slots, ys, False, i % 2)
            return carry

        lax.fori_loop(1, N_BATCHES, loop, 0, unroll=N_BATCHES // 2)
        stage2_inverse(_batch_slots(N_BATCHES - 1), 1, (N_BATCHES - 1) % 2)

    @pl.when(t == _T_S1)
    def _():
        def sub(sb, carry):
            vs = short_conv(sb, 0)
            for i in range(HY_SUB):
                stage1(sb * HY_SUB + i, vs[i].astype(BF16))
            return carry

        lax.fori_loop(0, N_SUB, sub, 0)

    @pl.when((t == _T_F0) | (t == _T_F1))
    def _():
        spectral_phase()

    @pl.when(t == _T_M)
    def _():
        def sub(sb, carry):
            gates = short_conv(sb, 4)
            for q in range(0, HY_SUB, MID_GROUP):
                ys = [inv_stage1(sb * HY_SUB + q + i) for i in range(MID_GROUP)]
                us = [(gates[q + i] * ys[i]).astype(BF16) for i in range(MID_GROUP)]
                for i in range(MID_GROUP):
                    stage1(sb * HY_SUB + q + i, us[i])
            return carry

        lax.fori_loop(0, N_SUB, sub, 0)

    @pl.when(t == _T_E)
    def _():
        def sub(sb, carry):
            gates = short_conv(sb, 8)
            for i in range(HY_SUB):
                n2 = sb * HY_SUB + i
                res = (gates[i] * inv_stage1(n2) * _pair(pz_ref, n2)).astype(BF16)
                o_ref[0, n2] = res[:, :LANES]
                o_ref[1, n2] = res[:, LANES:]
            return carry

        lax.fori_loop(0, N_SUB, sub, 0)


def _hyena(p5, conv_w, conv_b, kspec, f1, f1i, gf, gb):
    grid = (N_CBLK, BATCH // 2, _T_END)
    cw = jnp.concatenate(
        [jnp.concatenate([conv_w[:, k * HYENA_WIDTH:(k + 1) * HYENA_WIDTH],
                          conv_b[:, k * HYENA_WIDTH:(k + 1) * HYENA_WIDTH]], axis=0) for k in range(3)], axis=0)
    seq_block = (2, None, N_TILES, TILE_ROWS, LANES)

    def conv_in_map(c, b, t):
        return (b, jnp.where(t < _T_M, 0, jnp.where(t < _T_E, 1, 2)) * N_CBLK + c, 0, 0, 0)

    def z_map(c, b, t):
        flat = c * (BATCH // 2) + b
        sel = jnp.where(t >= _T_M, flat, jnp.maximum(flat - 1, 0))
        return (sel % (BATCH // 2), 3 * N_CBLK + sel // (BATCH // 2), 0, 0, 0)

    const = lambda shape: pl.BlockSpec(shape, lambda c, b, t: (0,) * len(shape), pipeline_mode=pl.Buffered(1))
    in_specs = [
        pl.BlockSpec(seq_block, conv_in_map),
        pl.BlockSpec(seq_block, z_map),
        pl.BlockSpec((12, LANES), lambda c, b, t: (0, c)),
        pl.BlockSpec((None, None, N_SLOTS, 256, LANES), lambda c, b, t: (jnp.where(t >= _T_M, 1, 0), c, 0, 0, 0)),
        const((N_TILES, 128, TILE_ROWS)), const((N_TILES, TILE_ROWS, 128)),
        const((2, 256, 256)), const((2, 256, 256)),
    ]
    return pl.pallas_call(
        _hyena_kernel,
        grid=grid,
        in_specs=in_specs,
        out_specs=pl.BlockSpec(seq_block, lambda c, b, t: (b, c, 0, 0, 0)),
        out_shape=jax.ShapeDtypeStruct((BATCH, N_CBLK, N_TILES, TILE_ROWS, LANES), BF16),
        scratch_shapes=[pltpu.VMEM((2, N_TILES * A_STRIDE, LANES), jnp.uint32),
                        pltpu.VMEM((2, SLOT_BATCH, 256, 2 * LANES), BF16)],
        compiler_params=pltpu.CompilerParams(
            dimension_semantics=("arbitrary", "arbitrary", "arbitrary"), vmem_limit_bytes=VMEM_LIMIT),
        name="hyena",
    )(p5, p5, cw, kspec, f1, f1i, gf, gb)


POOL_HALO = 8


def _out_kernel(yh_ref, up_ref, upw_ref, upn_ref, zp_ref, x_ref, pw_ref, ps_ref, gh_ref, gp_ref, w_ref, gpost_ref,
                o_ref, r_scr, yp_scr, prev_scr):
    @pl.when(_first_step())
    def _():
        prev_scr[...] = jnp.zeros_like(prev_scr)

    g = pl.program_id(1)
    last_step = N_TILES // OUT_TILES - 1
    row = lax.broadcasted_iota(jnp.int32, (TILE_ROWS, LANES), 0)

    def pool_tile(ci, l):
        if l < 0:
            wrapped = _shift_down(upw_ref[ci, POOL_HALO + l].astype(F32))
            return jnp.where(g == 0, wrapped, prev_scr[ci, POOL_HALO + l].astype(F32))
        if l >= OUT_TILES:
            t = upn_ref[ci, l - OUT_TILES].astype(F32)
            return jnp.where(g == last_step, _shift_up(t), t)
        return up_ref[ci, l].astype(F32)

    n_sub = OUT_TILES // SUB_TILES
    window_sums = [None] * len(POOL_WINDOWS)

    def pool_group(sb, ci):
        w = POOL_WINDOWS[ci]
        lo, hi = w // 2, w - 1 - w // 2
        if sb == 0:
            window_sum = pool_tile(ci, -lo)
            for d in range(-lo + 1, hi + 1):
                window_sum = window_sum + pool_tile(ci, d)
        else:
            window_sum = window_sums[ci]
        pooled = []
        for i in range(SUB_TILES):
            l = sb * SUB_TILES + i
            if l > 0:
                window_sum = window_sum + pool_tile(ci, l + hi) - pool_tile(ci, l - 1 - lo)
            inv_cnt = 1.0 / w
            if l < lo:
                inv_cnt = jnp.where((g == 0) & (row == 0), 1.0 / (hi + l + 1), inv_cnt)
            elif l > OUT_TILES - 1 - hi:
                inv_cnt = jnp.where((g == last_step) & (row == TILE_ROWS - 1),
                                    1.0 / (lo + OUT_TILES - l), inv_cnt)
            pooled.append((window_sum * inv_cnt - pool_tile(ci, l)).astype(BF16))
        window_sums[ci] = window_sum
        y = jnp.dot(jnp.concatenate(pooled, axis=0), pw_ref[ci].astype(BF16), preferred_element_type=F32)
        y = y * ps_ref[:, ci * LANES:(ci + 1) * LANES]
        gate = jnp.concatenate([zp_ref[ci, sb * SUB_TILES + i] for i in range(SUB_TILES)], axis=0).astype(F32)
        yp_scr[sb % 2, ci] = y * gate

    def rms(y, gain_ref):
        ms = jnp.mean(y * y, axis=-1, keepdims=True)
        return (y * lax.rsqrt(ms + EPS) * gain_ref[...]).astype(BF16)

    for ci in range(N_CBLK):
        pool_group(0, ci)
    ncol = D_MODEL // N_CBLK
    for sb in range(n_sub):
        j0 = sb * SUB_TILES
        yh = jnp.concatenate(
            [jnp.concatenate([yh_ref[cb, j] for cb in range(N_CBLK)], axis=-1) for j in range(j0, j0 + SUB_TILES)],
            axis=0).astype(F32)
        yp = jnp.concatenate([yp_scr[sb % 2, ci] for ci in range(N_CBLK)], axis=-1)
        yc = jnp.concatenate([rms(yh, gh_ref), rms(yp, gp_ref)], axis=-1)
        chunks = []
        for c in range(N_CBLK):
            chunks.append(jnp.dot(yc, w_ref[:, c * ncol:(c + 1) * ncol], preferred_element_type=F32))
            if sb + 1 < n_sub:
                pool_group(sb + 1, c)
        out = jnp.concatenate(chunks, axis=-1)
        ms = jnp.mean(out * out, axis=-1, keepdims=True)
        out = out * lax.rsqrt(ms + EPS) * gpost_ref[...]
        for j in range(SUB_TILES):
            for k in range(D_MODEL // LANES):
                r_scr[sb % 2, k, pl.ds(j, TILE_ROWS, stride=SUB_TILES), :] = out[j * TILE_ROWS:(j + 1) * TILE_ROWS,
                                                                             k * LANES:(k + 1) * LANES]
        r = jnp.concatenate([r_scr[sb % 2, k] for k in range(D_MODEL // LANES)], axis=-1)
        o_ref[:, j0:j0 + SUB_TILES, :] = (x_ref[:, j0:j0 + SUB_TILES, :]
                                          + r.reshape(TILE_ROWS, SUB_TILES, D_MODEL))
    for ci in range(N_CBLK):
        prev_scr[ci] = up_ref[ci, OUT_TILES - POOL_HALO:OUT_TILES]


def _out_proj(yh, p5, x4, pool_w, pool_scale, norm_h_g, norm_p_g, w_out, post_g):
    n_steps = N_TILES // OUT_TILES
    halo_blocks = N_TILES // POOL_HALO
    per_step = OUT_TILES // POOL_HALO
    u_group, z_group = 4, 5
    y_spec = pl.BlockSpec((None, N_CBLK, OUT_TILES, TILE_ROWS, LANES), lambda b, g: (b, 0, g, 0, 0))
    x_spec = pl.BlockSpec((None, TILE_ROWS, OUT_TILES, D_MODEL), lambda b, g: (b, 0, g, 0))
    full = lambda shape: pl.BlockSpec(shape, lambda b, g: (0,) * len(shape))
    return pl.pallas_call(
        _out_kernel,
        grid=(BATCH, n_steps),
        in_specs=[
            y_spec,
            pl.BlockSpec((None, N_CBLK, OUT_TILES, TILE_ROWS, LANES), lambda b, g: (b, u_group, g, 0, 0)),
            pl.BlockSpec((None, N_CBLK, POOL_HALO, TILE_ROWS, LANES),
                         lambda b, g: (b, u_group, halo_blocks - 1, 0, 0)),
            pl.BlockSpec((None, N_CBLK, POOL_HALO, TILE_ROWS, LANES),
                         lambda b, g: (b, u_group, (g * per_step + per_step) % halo_blocks, 0, 0)),
            pl.BlockSpec((None, N_CBLK, OUT_TILES, TILE_ROWS, LANES), lambda b, g: (b, z_group, g, 0, 0)),
            x_spec,
            full((N_CBLK, LANES, LANES)), full((1, POOL_WIDTH)),
            full((1, HYENA_WIDTH)), full((1, POOL_WIDTH)),
            pl.BlockSpec((D_MODEL, D_MODEL), lambda b, g: (0, 0), pipeline_mode=pl.Buffered(1)), full((1, D_MODEL)),
        ],
        out_specs=x_spec,
        out_shape=jax.ShapeDtypeStruct((BATCH, TILE_ROWS, N_TILES, D_MODEL), F32),
        scratch_shapes=[pltpu.VMEM((2, D_MODEL // LANES, TILE_ROWS * SUB_TILES, LANES), F32),
                        pltpu.VMEM((2, N_CBLK, TILE_ROWS * SUB_TILES, LANES), F32),
                        pltpu.VMEM((N_CBLK, POOL_HALO, TILE_ROWS, LANES), BF16)],
        compiler_params=pltpu.CompilerParams(
            dimension_semantics=("arbitrary", "arbitrary"), vmem_limit_bytes=OUT_VMEM_LIMIT),
        name="out_proj",
    )(yh, p5, p5, p5, p5, x4, pool_w, pool_scale, norm_h_g, norm_p_g, w_out, post_g)


def kernel(x, pre_norm_g, w_in, conv_w, conv_b, filt_w1, filt_b1, filt_w2, filt_b2, filt_w3, filt_b3,
           filt_freq, filt_w_out, hyena_d, pool_w, pool_scale, norm_h_g, norm_p_g, w_out, post_norm_g):
    assert x.shape == (BATCH, SEQ, D_MODEL) and pre_norm_g.shape[0] == 1
    f1, f1i, gf, gb, f1f = (jnp.asarray(m, F32).astype(BF16) for m in (_F1, _F1I, _GF, _GB, _F1F))

    x4 = x.reshape(BATCH, TILE_ROWS, N_TILES, D_MODEL)
    p5, taps = _in_proj(x4, pre_norm_g, w_in[0], jnp.asarray(_ZFEAT), filt_w1[0], filt_b1[0], filt_w2[0], filt_b2[0],
                        filt_w3[0], filt_b3[0], filt_freq[0], filt_w_out[0], jnp.asarray(_ABS_DELTAS))
    kspec = _filter_spec(taps, hyena_d[0], f1f, gf)

    yh = _hyena(p5, conv_w[0], conv_b, kspec, f1, f1i, gf, gb)
    out4 = _out_proj(yh, p5, x4, pool_w[0], pool_scale, norm_h_g, norm_p_g,
                     w_out[0].astype(BF16), post_norm_g)
    return out4.reshape(BATCH, SEQ, D_MODEL)
```

```python
import math

import numpy as np
import jax
import jax.numpy as jnp
from jax import lax
from jax.experimental import pallas as pl
from jax.experimental.pallas import tpu as pltpu

F32 = jnp.float32
BF16 = jnp.bfloat16

D_MODEL = 1024
BATCH = 4
SEQ = 8192
HYENA_WIDTH = 512
POOL_WIDTH = 512
POOL_WINDOWS = (2, 4, 8, 16)
FILTER_EMB = 33
FILTER_BANDS = 16
FILTER_HIDDEN = 64
PROJ_WIDTH = 3072
EPS = 1e-6

LANES = 128
N_FFT = 2 * SEQ
N_TILES = 128
TILE_ROWS = SEQ // N_TILES
N_SLOTS = 64
HY_SUB = 128
N_SUB = N_TILES // HY_SUB
A_STRIDE = 72
SLOT_BATCH = 1
N_CBLK = HYENA_WIDTH // LANES
PROJ_BLOCKS = PROJ_WIDTH // LANES
IN_TILES = 16
OUT_TILES = 32
SUB_TILES = 8
VMEM_LIMIT = 60 * 1024 * 1024
OUT_VMEM_LIMIT = 63 * 1024 * 1024


def _dft_tables():
    n1 = np.arange(TILE_ROWS)
    n2 = np.arange(N_TILES)
    s = np.arange(N_SLOTS)
    ph = 2 * np.pi * (n2[:, None, None] * s[None, :, None] / N_FFT
                      + n1[None, None, :] * s[None, :, None] / 128.0)
    f1 = np.zeros((N_TILES, 128, TILE_ROWS))
    f1[:, :64, :] = np.cos(ph)
    f1[:, 64:, :] = -np.sin(ph)
    f1[:, 0, :] = 1.0
    f1[:, 64, :] = (-1.0) ** n1
    php = np.transpose(ph, (0, 2, 1))
    f1i = np.zeros((N_TILES, TILE_ROWS, 128))
    f1i[:, :, :64] = 2 * np.cos(php) / N_FFT
    f1i[:, :, 64:] = -2 * np.sin(php) / N_FFT
    f1i[:, :, 0] = 1.0 / N_FFT
    f1i[:, :, 64] = ((-1.0) ** n1)[None, :] / N_FFT
    k2 = np.arange(128)
    th = 2 * np.pi * np.outer(k2, n2) / 128.0
    c, sn = np.cos(th), np.sin(th)
    g = np.block([[c, sn], [-sn, c]])
    gi = np.block([[c, -sn], [sn, c]])
    kk = np.arange(64)
    tha = 2 * np.pi * np.outer(kk, n2) / 128.0
    thb = 2 * np.pi * np.outer(64 + 128 * kk, n2) / N_FFT
    g0 = np.zeros((256, 256))
    g0[0:64, 0:128] = np.cos(tha)
    g0[64:128, 128:256] = np.cos(thb)
    g0[128:192, 0:128] = -np.sin(tha)
    g0[128, 0:128] = (-1.0) ** n2
    g0[192:256, 128:256] = -np.sin(thb)
    g0i = np.zeros((256, 256))
    g0i[0:128, 0:64] = 2 * np.cos(tha.T)
    g0i[0:128, 0] = 1.0
    g0i[0:128, 128:192] = -2 * np.sin(tha.T)
    g0i[0:128, 128] = (-1.0) ** n2
    g0i[128:256, 64:128] = 2 * np.cos(thb.T)
    g0i[128:256, 192:256] = -2 * np.sin(thb.T)
    q = np.arange(256)
    perm = np.where(q % 16 < 8, 8 * (q // 16) + q % 16, 128 + 8 * (q // 16) + q % 16 - 8)
    gf = np.stack([g0[perm, :], g[perm, :]])
    gb = np.stack([g0i[:, perm], gi[:, perm]])
    n1f = np.arange(128)
    phf = 2 * np.pi * (n2[:, None, None] * s[None, :, None] / N_FFT
                       + n1f[None, None, :] * s[None, :, None] / 128.0)
    full = np.zeros((N_TILES, 128, 128))
    full[:, :64, :] = np.cos(phf)
    full[:, 64:, :] = -np.sin(phf)
    full[:, 0, :] = 1.0
    full[:, 64, :] = (-1.0) ** n1f
    rev = full[:, :, 127:63:-1].copy()
    rev[0, :, 1:] = full[0, :, 127:64:-1]
    rev[0, :, 0] = 0.0
    f1f = np.concatenate([full[:, :, :64], rev], axis=2)
    pair64 = np.arange(128) % 2 * 64 + np.arange(128) // 2
    pair128 = np.arange(256) % 2 * 128 + np.arange(256) // 2
    f1, f1f = f1[:, pair64, :], f1f[:, pair64, :]
    f1i = f1i[:, :, pair64]
    gf = gf[:, :, pair128]
    gb = gb[:, pair128, :]
    return f1, f1i, gf, gb, f1f


def _filter_features():
    pos = np.arange(SEQ, dtype=np.float64)
    t = pos / (SEQ - 1)
    ang = 2.0 * math.pi * pos / SEQ
    bands = np.linspace(1e-4, FILTER_BANDS - 1, FILTER_BANDS)
    z = np.concatenate([t[:, None], np.cos(bands[None, :] * ang[:, None]),
                        -np.sin(bands[None, :] * ang[:, None])], axis=-1)
    z = z.reshape(TILE_ROWS, N_TILES, FILTER_EMB).transpose(1, 0, 2).reshape(SEQ, FILTER_EMB)
    zp = np.zeros((FILTER_HIDDEN, SEQ))
    zp[:FILTER_EMB, :] = z.T
    max_decay = math.log(1e-2) / 0.3
    min_decay = math.log(1e-2) / 1.5
    deltas = np.abs(np.linspace(min_decay, max_decay, HYENA_WIDTH))
    return zp.astype(np.float32), deltas.astype(np.float32)[None, :]


_F1, _F1I, _GF, _GB, _F1F = _dft_tables()
_ZFEAT, _ABS_DELTAS = _filter_features()


def _shift_down(x):
    rows = lax.broadcasted_iota(jnp.int32, x.shape, 0)
    return jnp.where(rows == 0, 0.0, pltpu.roll(x, 1, axis=0))


def _shift_up(x):
    rows = lax.broadcasted_iota(jnp.int32, x.shape, 0)
    return jnp.where(rows == x.shape[0] - 1, 0.0, pltpu.roll(x, x.shape[0] - 1, axis=0))


def _pair(ref, i):
    return jnp.concatenate([ref[0, i], ref[1, i]], axis=-1).astype(F32)


def _dup(x):
    return jnp.concatenate([x, x], axis=-1)


def _silu(z):
    hz = 0.5 * z
    return hz * (1.0 + jnp.tanh(hz))


def _tile_rows(n2):
    return pl.ds(pl.multiple_of(n2 * A_STRIDE, 8), N_SLOTS)


def _slot_rows(s):
    return pl.ds(s, N_TILES, stride=A_STRIDE)


def _stage_load(a_ref, rows):
    words = jnp.concatenate([a_ref[0, rows, :], a_ref[1, rows, :]], axis=-1)
    return pltpu.bitcast(words, BF16)


def _stage_store(a_ref, rows, val):
    words = pltpu.bitcast(val.astype(BF16), jnp.uint32)
    a_ref[0, rows, :] = words[:, :LANES]
    a_ref[1, rows, :] = words[:, LANES:]


N_BATCHES = N_SLOTS // SLOT_BATCH
SPEC_BLOCKS = 16


def _batch_slots(i):
    return [i * SLOT_BATCH + j for j in range(SLOT_BATCH)]


def _stage2_forward(a_ref, gf_ref, slots, sel0):
    xs = [_stage_load(a_ref, _slot_rows(s)) for s in slots]
    return [jnp.dot(gf_ref[sel0] if j == 0 else gf_ref[1], x, preferred_element_type=F32)
            for j, x in enumerate(xs)]


def _re_im_blocks(y):
    return [(y[16 * i:16 * i + 8], y[16 * i + 8:16 * i + 16]) for i in range(SPEC_BLOCKS)]


_GATE_GROUPS = (3, 5)
N_IN_STEPS = BATCH * (N_TILES // IN_TILES)
FILT_TILES = N_TILES // N_IN_STEPS


def _split_bf16(x):
    hi = x.astype(BF16)
    return hi, (x - hi.astype(F32)).astype(BF16)


def _dot_split(a, b):
    a_hi, a_lo = _split_bf16(a)
    b_hi, b_lo = _split_bf16(b)
    dot = lambda u, v: jnp.dot(u, v, preferred_element_type=F32)
    return dot(a_hi, b_hi) + dot(a_lo, b_hi) + dot(a_hi, b_lo)


def _first_step():
    return (pl.program_id(0) == 0) & (pl.program_id(1) == 0)


def _cast_weights_once(w_ref, w_scr, ncol):
    @pl.when(_first_step())
    def _():
        for c in range(w_ref.shape[1] // ncol):
            w_scr[:, c * ncol:(c + 1) * ncol] = w_ref[:, c * ncol:(c + 1) * ncol].astype(BF16)


class _FilterTaps:
    def __init__(self, z_ref, p_ref, wp_scr, dl_ref, first_tile, o_ref):
        self.p_ref, self.wp_scr, self.dl_ref, self.first_tile, self.o_ref = p_ref, wp_scr, dl_ref, first_tile, o_ref
        self.cols_blk = p_ref[3].T[:FILTER_HIDDEN]
        self.h = z_ref[...]
        self.n_stages = 5

    def stage(self, k):
        hid = FILTER_HIDDEN
        if k < 3:
            w_t = self.p_ref[k].T[:hid, :hid]
            freq = self.cols_blk[:, 3:4]
            self.h = jnp.sin(freq * (_dot_split(w_t, self.h) + self.cols_blk[:, k:k + 1]))
            return
        if k == 3:
            h_hi = self.h.astype(BF16).astype(F32)
            stack = jnp.concatenate([h_hi, self.h - h_hi, h_hi, jnp.zeros_like(h_hi)], axis=0)
            self.lhs = stack.T.astype(BF16)
            rows = FILT_TILES * TILE_ROWS
            r = lax.broadcasted_iota(jnp.int32, (rows, HYENA_WIDTH), 0)
            pos = 128 * (r % TILE_ROWS) + self.first_tile + r // TILE_ROWS
            t = pos.astype(F32) / float(SEQ - 1)
            self.decay = jnp.exp(-t * self.dl_ref[...])
        for od in ((0, 1) if k == 3 else (2, 3)):
            cols = slice(od * HYENA_WIDTH, (od + 1) * HYENA_WIDTH)
            taps = jnp.dot(self.lhs, self.wp_scr[:, cols], preferred_element_type=F32) * self.decay
            taps = taps.astype(BF16)
            for cb in range(N_CBLK):
                for i in range(FILT_TILES):
                    self.o_ref[od * N_CBLK + cb, i] = taps[i * TILE_ROWS:(i + 1) * TILE_ROWS,
                                                           cb * LANES:(cb + 1) * LANES]


def _in_proj_kernel(x_ref, g_ref, w32_ref, z_ref, p_ref, wp32_ref, dl_ref, o_ref, taps_ref, h_scr, w_ref, wp_scr):
    _cast_weights_once(w32_ref, w_ref, 512)

    @pl.when(_first_step())
    def _():
        wp_hi, wp_lo = _split_bf16(wp32_ref[...])
        wp_scr[...] = jnp.concatenate([wp_hi, wp_hi, wp_lo, jnp.zeros_like(wp_lo)], axis=0)

    step = pl.program_id(0) * (N_TILES // IN_TILES) + pl.program_id(1)
    filt = _FilterTaps(z_ref, p_ref, wp_scr, dl_ref, step * FILT_TILES, taps_ref)
    n_dots = 0

    ncol = 512
    for sb in range(IN_TILES // SUB_TILES):
        j0 = sb * SUB_TILES
        x = x_ref[:, j0:j0 + SUB_TILES, :].reshape(TILE_ROWS * SUB_TILES, D_MODEL)
        ms = jnp.mean(x * x, axis=-1, keepdims=True)
        hn = x * lax.rsqrt(ms + EPS) * g_ref[...]
        for k in range(D_MODEL // LANES):
            h_scr[sb, k] = hn[:, k * LANES:(k + 1) * LANES]
        h = jnp.concatenate(
            [jnp.concatenate([h_scr[sb, k, pl.ds(j, TILE_ROWS, stride=SUB_TILES), :]
                              for k in range(D_MODEL // LANES)], axis=-1).astype(BF16)
             for j in range(SUB_TILES)], axis=0)
        for c in range(PROJ_WIDTH // ncol):
            p = jnp.dot(h, w_ref[:, c * ncol:(c + 1) * ncol], preferred_element_type=F32)
            if (c * ncol) // HYENA_WIDTH in _GATE_GROUPS:
                p = _silu(p)
            p = p.astype(BF16)
            for cb in range(ncol // LANES):
                for j in range(SUB_TILES):
                    o_ref[c * (ncol // LANES) + cb, j0 + j] = p[j * TILE_ROWS:(j + 1) * TILE_ROWS,
                                                                cb * LANES:(cb + 1) * LANES]
            if n_dots % 2 == 0 and n_dots // 2 < filt.n_stages:
                filt.stage(n_dots // 2)
            n_dots += 1


def _in_proj(x4, pre_g, w_in, zfeat_t, w1, b1, w2, b2, w3, b3, freq, w_proj, abs_deltas):
    steps_per_batch = N_TILES // IN_TILES
    pad2 = lambda m: jnp.pad(m, ((0, LANES - m.shape[0]), (0, LANES - m.shape[1])))
    params = jnp.stack([pad2(w1), pad2(w2), pad2(w3), pad2(jnp.stack([b1, b2, b3, freq], axis=0))])
    full = lambda shape: pl.BlockSpec(shape, lambda b, g: (0,) * len(shape))
    once = lambda shape: pl.BlockSpec(shape, lambda b, g: (0,) * len(shape), pipeline_mode=pl.Buffered(1))
    return pl.pallas_call(
        _in_proj_kernel,
        grid=(BATCH, steps_per_batch),
        in_specs=[
            pl.BlockSpec((None, TILE_ROWS, IN_TILES, D_MODEL), lambda b, g: (b, 0, g, 0)),
            full((1, D_MODEL)),
            once((D_MODEL, PROJ_WIDTH)),
            pl.BlockSpec((FILTER_HIDDEN, FILT_TILES * TILE_ROWS), lambda b, g: (0, b * steps_per_batch + g)),
            full((4, LANES, LANES)),
            once((FILTER_HIDDEN, 4 * HYENA_WIDTH)),
            full((1, HYENA_WIDTH)),
        ],
        out_specs=[
            pl.BlockSpec((None, PROJ_BLOCKS, IN_TILES, TILE_ROWS, LANES), lambda b, g: (b, 0, g, 0, 0)),
            pl.BlockSpec((4 * N_CBLK, FILT_TILES, TILE_ROWS, LANES), lambda b, g: (0, b * steps_per_batch + g, 0, 0)),
        ],
        out_shape=[jax.ShapeDtypeStruct((BATCH, PROJ_BLOCKS, N_TILES, TILE_ROWS, LANES), BF16),
                   jax.ShapeDtypeStruct((4 * N_CBLK, N_TILES, TILE_ROWS, LANES), BF16)],
        scratch_shapes=[pltpu.VMEM((IN_TILES // SUB_TILES, D_MODEL // LANES, TILE_ROWS * SUB_TILES, LANES), F32),
                        pltpu.VMEM((D_MODEL, PROJ_WIDTH), BF16),
                        pltpu.VMEM((2 * LANES, 4 * HYENA_WIDTH), BF16)],
        compiler_params=pltpu.CompilerParams(
            dimension_semantics=("arbitrary", "arbitrary"), vmem_limit_bytes=VMEM_LIMIT),
        name="in_proj",
    )(x4, pre_g, w_in, zfeat_t, params, w_proj, abs_deltas)


SPEC_BATCH = 4


def _filter_spec_kernel(fa_ref, fb_ref, ba_ref, bb_ref, d_ref, f1f_ref, gf_ref, k_ref, a_scr):
    def stage1(sb, carry):
        for i in range(HY_SUB):
            n2 = sb * HY_SUB + i
            nb = (N_TILES - n2) % N_TILES
            fwd = jnp.concatenate([fa_ref[n2], fb_ref[n2]], axis=-1)
            bwd = jnp.concatenate([ba_ref[nb], bb_ref[nb]], axis=-1)
            taps = jnp.concatenate([fwd, bwd], axis=0)
            _stage_store(a_scr, _tile_rows(n2), jnp.dot(f1f_ref[n2], taps, preferred_element_type=F32))
        return carry

    lax.fori_loop(0, N_SUB, stage1, 0)

    d = d_ref[pl.ds(pl.program_id(0), 1), :]

    def batch(slots, first):
        ys = _stage2_forward(a_scr, gf_ref, slots, 0 if first else 1)
        for j, (s, y) in enumerate(zip(slots, ys)):
            blocks = []
            for i, (yr, yi) in enumerate(_re_im_blocks(y)):
                yr = yr + d
                if first and j == 0 and i == 0:
                    rows = lax.broadcasted_iota(jnp.int32, yi.shape, 0)
                    yi = yi + jnp.where(rows == 0, d, 0.0)
                blocks += [yr, yi]
            spec = jnp.concatenate(blocks, axis=0).astype(BF16)
            k_ref[0, s] = spec[:, :LANES]
            k_ref[1, s] = spec[:, LANES:]

    group = lambda i: [i * SPEC_BATCH + j for j in range(SPEC_BATCH)]
    batch(group(0), True)

    def loop(i, carry):
        batch(group(i), False)
        return carry

    lax.fori_loop(1, N_SLOTS // SPEC_BATCH, loop, 0, unroll=3)


def _filter_spec(taps, hyena_d, f1f, gf):
    grid = (2, N_CBLK // 2)
    const = lambda shape: pl.BlockSpec(shape, lambda o, c: (0,) * len(shape), pipeline_mode=pl.Buffered(1))

    def taps_spec(direction, k):
        return pl.BlockSpec((None, N_TILES, TILE_ROWS, LANES),
                            lambda o, c: ((2 * o + direction) * N_CBLK + 2 * c + k, 0, 0, 0))

    return pl.pallas_call(
        _filter_spec_kernel,
        grid=grid,
        in_specs=[taps_spec(0, 0), taps_spec(0, 1), taps_spec(1, 0), taps_spec(1, 1),
                  pl.BlockSpec((2, 2 * LANES), lambda o, c: (0, c)),
                  const((N_TILES, 128, 128)), const((2, 256, 256))],
        out_specs=pl.BlockSpec((None, 2, N_SLOTS, 256, LANES), lambda o, c: (o, c, 0, 0, 0)),
        out_shape=jax.ShapeDtypeStruct((2, N_CBLK, N_SLOTS, 256, LANES), BF16),
        scratch_shapes=[pltpu.VMEM((2, N_TILES * A_STRIDE, LANES), jnp.uint32)],
        compiler_params=pltpu.CompilerParams(
            dimension_semantics=("arbitrary", "arbitrary"), vmem_limit_bytes=VMEM_LIMIT),
        name="filter_spec",
    )(taps, taps, taps, taps, hyena_d, f1f, gf)


_T_S1 = 0
_T_F0 = 1
_T_M = 2
_T_F1 = 3
_T_E = 4
_T_END = 5
MID_GROUP = 8


def _hyena_kernel(cin_ref, pz_ref, cw_ref, k_ref, f1_ref, f1i_ref, gf_ref, gb_ref, o_ref, a_scr, z_scr):
    t = pl.program_id(2)

    def stage1(n2, u_bf):
        _stage_store(a_scr, _tile_rows(n2), jnp.dot(f1_ref[n2], u_bf, preferred_element_type=F32))

    def inv_stage1(n2):
        return jnp.dot(f1i_ref[n2], _stage_load(a_scr, _tile_rows(n2)), preferred_element_type=F32)

    def short_conv(sb, row0):
        base = sb * HY_SUB
        first = _pair(cin_ref, (base + N_TILES - 1) % N_TILES)
        first = jnp.where(sb == 0, _shift_down(first), first)
        last = _pair(cin_ref, (base + HY_SUB) % N_TILES)
        last = jnp.where(sb == N_SUB - 1, _shift_up(last), last)
        tiles = [first] + [_pair(cin_ref, base + i) for i in range(HY_SUB)] + [last]
        w = cw_ref[row0:row0 + 4, :]
        w0, w1, w2, b = _dup(w[0:1]), _dup(w[1:2]), _dup(w[2:3]), _dup(w[3:4])
        return [tiles[i] * w0 + tiles[i + 1] * w1 + tiles[i + 2] * w2 + b for i in range(HY_SUB)]

    def filter_multiply(slots, ys, first, buf):
        for j, (s, y) in enumerate(zip(slots, ys)):
            blocks = []
            for i, (yr, yi) in enumerate(_re_im_blocks(y)):
                kblk = k_ref[s, 16 * i:16 * i + 16, :].astype(F32)
                kr, ki = kblk[:8], kblk[8:]
                if first and j == 0 and i == 0:
                    rows = lax.broadcasted_iota(jnp.int32, kr.shape, 0)
                    ka, kb, kd = kr, jnp.where(rows == 0, 0.0, ki), jnp.where(rows == 0, ki, kr)
                else:
                    ka, kb, kd = kr, ki, kr
                ka, kb, kd = _dup(ka), _dup(kb), _dup(kd)
                blocks += [yr * ka - yi * kb, yr * kb + yi * kd]
            z_scr[buf, j] = jnp.concatenate(blocks, axis=0).astype(BF16)

    def stage2_inverse(slots, first_sel, buf):
        bms = [jnp.dot(gb_ref[first_sel] if j == 0 else gb_ref[1], z_scr[buf, j], preferred_element_type=F32)
               for j in range(SLOT_BATCH)]
        for s, bm in zip(slots, bms):
            _stage_store(a_scr, _slot_rows(s), bm)

    def spectral_phase():
        slots0 = _batch_slots(0)
        filter_multiply(slots0, _stage2_forward(a_scr, gf_ref, slots0, 0), True, 0)

        def loop(i, carry):
            slots = _batch_slots(i)
            ys = _stage2_forward(a_scr, gf_ref, slots, 1)
            stage2_inverse(_batch_slots(i - 1), jnp.where(i == 1, 0, 1), (i - 1) % 2)
            filter_multiply(slots, ys, False, i % 2)
            return carry

        lax.fori_loop(1, N_BATCHES, loop, 0, unroll=N_BATCHES // 2)
        stage2_inverse(_batch_slots(N_BATCHES - 1), 1, (N_BATCHES - 1) % 2)

    @pl.when(t == _T_S1)
    def _():
        def sub(sb, carry):
            vs = short_conv(sb, 0)
            for i in range(HY_SUB):
                stage1(sb * HY_SUB + i, vs[i].astype(BF16))
            return carry

        lax.fori_loop(0, N_SUB, sub, 0)

    @pl.when((t == _T_F0) | (t == _T_F1))
    def _():
        spectral_phase()

    @pl.when(t == _T_M)
    def _():
        def sub(sb, carry):
            gates = short_conv(sb, 4)
            for q in range(0, HY_SUB, MID_GROUP):
                ys = [inv_stage1(sb * HY_SUB + q + i) for i in range(MID_GROUP)]
                us = [(gates[q + i] * ys[i]).astype(BF16) for i in range(MID_GROUP)]
                for i in range(MID_GROUP):
                    stage1(sb * HY_SUB + q + i, us[i])
            return carry

        lax.fori_loop(0, N_SUB, sub, 0)

    @pl.when(t == _T_E)
    def _():
        def sub(sb, carry):
            gates = short_conv(sb, 8)
            for i in range(HY_SUB):
                n2 = sb * HY_SUB + i
                res = (gates[i] * inv_stage1(n2) * _pair(pz_ref, n2)).astype(BF16)
                o_ref[0, n2] = res[:, :LANES]
                o_ref[1, n2] = res[:, LANES:]
            return carry

        lax.fori_loop(0, N_SUB, sub, 0)


def _hyena(p5, conv_w, conv_b, kspec, f1, f1i, gf, gb):
    grid = (N_CBLK, BATCH // 2, _T_END)
    cw = jnp.concatenate(
        [jnp.concatenate([conv_w[:, k * HYENA_WIDTH:(k + 1) * HYENA_WIDTH],
                          conv_b[:, k * HYENA_WIDTH:(k + 1) * HYENA_WIDTH]], axis=0) for k in range(3)], axis=0)
    seq_block = (2, None, N_TILES, TILE_ROWS, LANES)

    def conv_in_map(c, b, t):
        return (b, jnp.where(t < _T_M, 0, jnp.where(t < _T_E, 1, 2)) * N_CBLK + c, 0, 0, 0)

    def z_map(c, b, t):
        flat = c * (BATCH // 2) + b
        sel = jnp.where(t >= _T_M, flat, jnp.maximum(flat - 1, 0))
        return (sel % (BATCH // 2), 3 * N_CBLK + sel // (BATCH // 2), 0, 0, 0)

    const = lambda shape: pl.BlockSpec(shape, lambda c, b, t: (0,) * len(shape), pipeline_mode=pl.Buffered(1))
    in_specs = [
        pl.BlockSpec(seq_block, conv_in_map),
        pl.BlockSpec(seq_block, z_map),
        pl.BlockSpec((12, LANES), lambda c, b, t: (0, c)),
        pl.BlockSpec((None, None, N_SLOTS, 256, LANES), lambda c, b, t: (jnp.where(t >= _T_M, 1, 0), c, 0, 0, 0)),
        const((N_TILES, 128, TILE_ROWS)), const((N_TILES, TILE_ROWS, 128)),
        const((2, 256, 256)), const((2, 256, 256)),
    ]
    return pl.pallas_call(
        _hyena_kernel,
        grid=grid,
        in_specs=in_specs,
        out_specs=pl.BlockSpec(seq_block, lambda c, b, t: (b, c, 0, 0, 0)),
        out_shape=jax.ShapeDtypeStruct((BATCH, N_CBLK, N_TILES, TILE_ROWS, LANES), BF16),
        scratch_shapes=[pltpu.VMEM((2, N_TILES * A_STRIDE, LANES), jnp.uint32),
                        pltpu.VMEM((2, SLOT_BATCH, 256, 2 * LANES), BF16)],
        compiler_params=pltpu.CompilerParams(
            dimension_semantics=("arbitrary", "arbitrary", "arbitrary"), vmem_limit_bytes=VMEM_LIMIT),
        name="hyena",
    )(p5, p5, cw, kspec, f1, f1i, gf, gb)


POOL_HALO = 8


def _out_kernel(yh_ref, up_ref, upw_ref, upn_ref, zp_ref, x_ref, pw_ref, ps_ref, gh_ref, gp_ref, w_ref, gpost_ref,
                o_ref, r_scr, yp_scr, prev_scr):
    @pl.when(_first_step())
    def _():
        prev_scr[...] = jnp.zeros_like(prev_scr)

    g = pl.program_id(1)
    last_step = N_TILES // OUT_TILES - 1
    row = lax.broadcasted_iota(jnp.int32, (TILE_ROWS, LANES), 0)

    def pool_tile(ci, l):
        if l < 0:
            wrapped = _shift_down(upw_ref[ci, POOL_HALO + l].astype(F32))
            return jnp.where(g == 0, wrapped, prev_scr[ci, POOL_HALO + l].astype(F32))
        if l >= OUT_TILES:
            t = upn_ref[ci, l - OUT_TILES].astype(F32)
            return jnp.where(g == last_step, _shift_up(t), t)
        return up_ref[ci, l].astype(F32)

    n_sub = OUT_TILES // SUB_TILES
    window_sums = [None] * len(POOL_WINDOWS)

    def pool_group(sb, ci):
        w = POOL_WINDOWS[ci]
        lo, hi = w // 2, w - 1 - w // 2
        if sb == 0:
            window_sum = pool_tile(ci, -lo)
            for d in range(-lo + 1, hi + 1):
                window_sum = window_sum + pool_tile(ci, d)
        else:
            window_sum = window_sums[ci]
        pooled = []
        for i in range(SUB_TILES):
            l = sb * SUB_TILES + i
            if l > 0:
                window_sum = window_sum + pool_tile(ci, l + hi) - pool_tile(ci, l - 1 - lo)
            inv_cnt = 1.0 / w
            if l < lo:
                inv_cnt = jnp.where((g == 0) & (row == 0), 1.0 / (hi + l + 1), inv_cnt)
            elif l > OUT_TILES - 1 - hi:
                inv_cnt = jnp.where((g == last_step) & (row == TILE_ROWS - 1),
                                    1.0 / (lo + OUT_TILES - l), inv_cnt)
            pooled.append((window_sum * inv_cnt - pool_tile(ci, l)).astype(BF16))
        window_sums[ci] = window_sum
        y = jnp.dot(jnp.concatenate(pooled, axis=0), pw_ref[ci].astype(BF16), preferred_element_type=F32)
        y = y * ps_ref[:, ci * LANES:(ci + 1) * LANES]
        gate = jnp.concatenate([zp_ref[ci, sb * SUB_TILES + i] for i in range(SUB_TILES)], axis=0).astype(F32)
        yp_scr[sb % 2, ci] = y * gate

    def rms(y, gain_ref):
        ms = jnp.mean(y * y, axis=-1, keepdims=True)
        return (y * lax.rsqrt(ms + EPS) * gain_ref[...]).astype(BF16)

    for ci in range(N_CBLK):
        pool_group(0, ci)
    ncol = D_MODEL // N_CBLK
    for sb in range(n_sub):
        j0 = sb * SUB_TILES
        yh = jnp.concatenate(
            [jnp.concatenate([yh_ref[cb, j] for cb in range(N_CBLK)], axis=-1) for j in range(j0, j0 + SUB_TILES)],
            axis=0).astype(F32)
        yp = jnp.concatenate([yp_scr[sb % 2, ci] for ci in range(N_CBLK)], axis=-1)
        yc = jnp.concatenate([rms(yh, gh_ref), rms(yp, gp_ref)], axis=-1)
        chunks = []
        for c in range(N_CBLK):
            chunks.append(jnp.dot(yc, w_ref[:, c * ncol:(c + 1) * ncol], preferred_element_type=F32))
            if sb + 1 < n_sub:
                pool_group(sb + 1, c)
        out = jnp.concatenate(chunks, axis=-1)
        ms = jnp.mean(out * out, axis=-1, keepdims=True)
        out = out * lax.rsqrt(ms + EPS) * gpost_ref[...]
        for j in range(SUB_TILES):
            for k in range(D_MODEL // LANES):
                r_scr[sb % 2, k, pl.ds(j, TILE_ROWS, stride=SUB_TILES), :] = out[j * TILE_ROWS:(j + 1) * TILE_ROWS,
                                                                             k * LANES:(k + 1) * LANES]
        r = jnp.concatenate([r_scr[sb % 2, k] for k in range(D_MODEL // LANES)], axis=-1)
        o_ref[:, j0:j0 + SUB_TILES, :] = (x_ref[:, j0:j0 + SUB_TILES, :]
                                          + r.reshape(TILE_ROWS, SUB_TILES, D_MODEL))
    for ci in range(N_CBLK):
        prev_scr[ci] = up_ref[ci, OUT_TILES - POOL_HALO:OUT_TILES]


def _out_proj(yh, p5, x4, pool_w, pool_scale, norm_h_g, norm_p_g, w_out, post_g):
    n_steps = N_TILES // OUT_TILES
    halo_blocks = N_TILES // POOL_HALO
    per_step = OUT_TILES // POOL_HALO
    u_group, z_group = 4, 5
    y_spec = pl.BlockSpec((None, N_CBLK, OUT_TILES, TILE_ROWS, LANES), lambda b, g: (b, 0, g, 0, 0))
    x_spec = pl.BlockSpec((None, TILE_ROWS, OUT_TILES, D_MODEL), lambda b, g: (b, 0, g, 0))
    full = lambda shape: pl.BlockSpec(shape, lambda b, g: (0,) * len(shape))
    return pl.pallas_call(
        _out_kernel,
        grid=(BATCH, n_steps),
        in_specs=[
            y_spec,
            pl.BlockSpec((None, N_CBLK, OUT_TILES, TILE_ROWS, LANES), lambda b, g: (b, u_group, g, 0, 0)),
            pl.BlockSpec((None, N_CBLK, POOL_HALO, TILE_ROWS, LANES),
                         lambda b, g: (b, u_group, halo_blocks - 1, 0, 0)),
            pl.BlockSpec((None, N_CBLK, POOL_HALO, TILE_ROWS, LANES),
                         lambda b, g: (b, u_group, (g * per_step + per_step) % halo_blocks, 0, 0)),
            pl.BlockSpec((None, N_CBLK, OUT_TILES, TILE_ROWS, LANES), lambda b, g: (b, z_group, g, 0, 0)),
            x_spec,
            full((N_CBLK, LANES, LANES)), full((1, POOL_WIDTH)),
            full((1, HYENA_WIDTH)), full((1, POOL_WIDTH)),
            pl.BlockSpec((D_MODEL, D_MODEL), lambda b, g: (0, 0), pipeline_mode=pl.Buffered(1)), full((1, D_MODEL)),
        ],
        out_specs=x_spec,
        out_shape=jax.ShapeDtypeStruct((BATCH, TILE_ROWS, N_TILES, D_MODEL), F32),
        scratch_shapes=[pltpu.VMEM((2, D_MODEL // LANES, TILE_ROWS * SUB_TILES, LANES), F32),
                        pltpu.VMEM((2, N_CBLK, TILE_ROWS * SUB_TILES, LANES), F32),
                        pltpu.VMEM((N_CBLK, POOL_HALO, TILE_ROWS, LANES), BF16)],
        compiler_params=pltpu.CompilerParams(
            dimension_semantics=("arbitrary", "arbitrary"), vmem_limit_bytes=OUT_VMEM_LIMIT),
        name="out_proj",
    )(yh, p5, p5, p5, p5, x4, pool_w, pool_scale, norm_h_g, norm_p_g, w_out, post_g)


def kernel(x, pre_norm_g, w_in, conv_w, conv_b, filt_w1, filt_b1, filt_w2, filt_b2, filt_w3, filt_b3,
           filt_freq, filt_w_out, hyena_d, pool_w, pool_scale, norm_h_g, norm_p_g, w_out, post_norm_g):
    assert x.shape == (BATCH, SEQ, D_MODEL) and pre_norm_g.shape[0] == 1
    f1, f1i, gf, gb, f1f = (jnp.asarray(m, F32).astype(BF16) for m in (_F1, _F1I, _GF, _GB, _F1F))

    x4 = x.reshape(BATCH, TILE_ROWS, N_TILES, D_MODEL)
    p5, taps = _in_proj(x4, pre_norm_g, w_in[0], jnp.asarray(_ZFEAT), filt_w1[0], filt_b1[0], filt_w2[0], filt_b2[0],
                        filt_w3[0], filt_b3[0], filt_freq[0], filt_w_out[0], jnp.asarray(_ABS_DELTAS))
    kspec = _filter_spec(taps, hyena_d[0], f1f, gf)

    yh = _hyena(p5, conv_w[0], conv_b, kspec, f1, f1i, gf, gb)
    out4 = _out_proj(yh, p5, x4, pool_w[0], pool_scale, norm_h_g, norm_p_g,
                     w_out[0].astype(BF16), post_norm_g)
    return out4.reshape(BATCH, SEQ, D_MODEL)
```
